```python
import jax, jax.numpy as jnp
from jax import lax
import numpy as np

D_MODEL = 1024
BATCH = 32
SEQ = 2048
DEPTH = 1

GRID_W = 64
CTX_LEN = 256
D_FF = 2816
D_CONV = 1024
CONV_WIDTH = 31
GLA_HEADS = 4
GLA_DK = 128
GLA_DV = 256
GLA_LOWRANK = 16
GLA_TAU = 16.0
GLA_CHUNK = 64
N_MOD = 9
EPS = 1e-6
QK_W = GLA_HEADS * GLA_DK
V_W = GLA_HEADS * GLA_DV
IN_SPLITS = (2 * D_CONV, QK_W, QK_W, V_W, V_W, GLA_LOWRANK, GLA_LOWRANK, D_MODEL, D_MODEL)
D_IN = 2 * D_CONV + 2 * QK_W + 2 * V_W + 2 * GLA_LOWRANK + 2 * D_MODEL

kernel_name = 'hybrid_conv_gla_macaron_dit_layer'


def rmsnorm(h, g):
    hf = h.astype(jnp.float32)
    y = hf * lax.rsqrt(jnp.mean(hf * hf, axis=-1, keepdims=True) + EPS)
    return (y * g.astype(jnp.float32)).astype(h.dtype)


def layernorm(h, g, b):
    hf = h.astype(jnp.float32)
    mu = jnp.mean(hf, axis=-1, keepdims=True)
    var = jnp.mean(jnp.square(hf - mu), axis=-1, keepdims=True)
    y = (hf - mu) * lax.rsqrt(var + EPS)
    return (y * g.astype(jnp.float32) + b.astype(jnp.float32)).astype(h.dtype)


def modulate(h, g, shift, scale):
    return rmsnorm(h, g) * (1 + scale) + shift


def half_ffn(h, g, shift, scale, gate, w_gu, w_down):
    u = modulate(h, g, shift, scale)
    a, b = jnp.split(u @ w_gu, 2, axis=-1)
    return h + 0.5 * gate * ((jax.nn.silu(a) * b) @ w_down)


def split_in(p):
    offs, o = [], 0
    for w in IN_SPLITS[:-1]:
        o += w
        offs.append(o)
    return jnp.split(p, offs, axis=-1)


def heads(t, d):
    return t.reshape(t.shape[:-1] + (GLA_HEADS, d))


def flip(t):
    return jnp.flip(t, axis=1)


def log_decay(lr, w, b):
    z = (lr @ w + b).astype(jnp.float32)
    return heads(jax.nn.log_sigmoid(z) / GLA_TAU, GLA_DK)


def conformer_conv(p, dw_w, dw_b, ln_g, ln_b, w_o):
    a, b = jnp.split(p, 2, axis=-1)
    z = a * jax.nn.sigmoid(b)
    z = lax.conv_general_dilated(
        z, dw_w[:, None, :].astype(z.dtype), window_strides=(1,),
        padding=((CONV_WIDTH // 2, CONV_WIDTH // 2),),
        dimension_numbers=('NWC', 'WIO', 'NWC'), feature_group_count=D_CONV) + dw_b
    z = jax.nn.silu(layernorm(z, ln_g, ln_b))
    return z @ w_o


def gla_scan(q, k, v, log_a, s0):
    bsz, L = q.shape[0], q.shape[1]
    n = L // GLA_CHUNK

    def chunks(t):
        t = t.astype(jnp.float32).reshape(bsz, n, GLA_CHUNK, GLA_HEADS, t.shape[-1])
        return jnp.transpose(t, (1, 0, 3, 2, 4))

    mask = jnp.tril(jnp.ones((GLA_CHUNK, GLA_CHUNK), dtype=bool))[:, :, None]

    def step(s, inp):
        qc, kc, vc, gc = inp
        b = jnp.cumsum(gc, axis=2)
        inter = jnp.einsum('bhtk,bhkv->bhtv', qc * jnp.exp(b), s)
        rel = jnp.where(mask, b[:, :, :, None, :] - b[:, :, None, :, :], -jnp.inf)
        att = jnp.einsum('bhtk,bhtsk,bhsk->bhts', qc, jnp.exp(rel), kc)
        intra = jnp.einsum('bhts,bhsv->bhtv', att, vc)
        b_last = b[:, :, -1:, :]
        s_new = jnp.exp(b_last[:, :, 0, :, None]) * s + jnp.einsum(
            'bhsk,bhsv->bhkv', kc * jnp.exp(b_last - b), vc)
        return s_new, inter + intra

    _, o = lax.scan(step, s0, (chunks(q), chunks(k), chunks(v), chunks(log_a)))
    return jnp.transpose(o, (1, 0, 3, 2, 4)).reshape(bsz, L, GLA_HEADS, GLA_DV)


def gla_final_state(k, v, log_a):
    b = jnp.cumsum(log_a.astype(jnp.float32), axis=1)
    w = jnp.exp(b[:, -1:] - b)
    return jnp.einsum('blhk,blhv->bhkv', k.astype(jnp.float32) * w, v.astype(jnp.float32))


def bidir_gla(q, k, v, la_f, la_b, s_f, s_b):
    o_f = gla_scan(q, k, v, la_f, s_f)
    o_b = gla_scan(flip(q), flip(k), flip(v), flip(la_b), s_b)
    return o_f + flip(o_b)


def gla_output(o, og, gn_g, w_go):
    o = o * lax.rsqrt(jnp.mean(o * o, axis=-1, keepdims=True) + EPS)
    o = (o.reshape(o.shape[0], o.shape[1], V_W) * gn_g.astype(jnp.float32)).astype(og.dtype)
    return (o * jax.nn.silu(og)) @ w_go


def branch_merge(conv_in, o, og, ga, gb, dw_w, dw_b, ln_g, ln_b, w_co, gn_g, w_go, w_o):
    y_conv = conformer_conv(conv_in, dw_w, dw_b, ln_g, ln_b, w_co)
    y_gla = gla_output(o, og, gn_g, w_go)
    return (jax.nn.sigmoid(ga) * y_conv + jax.nn.sigmoid(gb) * y_gla) @ w_o


def _fwd_setup_inputs(seed: int = 0) -> dict:
    key = jax.random.key(seed)
    ks = jax.random.split(key, 32)
    f32 = jnp.float32
    D, L = D_MODEL, DEPTH

    def nrm(k, shape, scale):
        return jax.random.normal(k, shape, f32) * scale

    def gain(k, shape):
        return 1.0 + 0.05 * jax.random.normal(k, shape, f32)

    return {
        'x': nrm(ks[0], (BATCH, SEQ, D), 1.0),
        'c': nrm(ks[1], (BATCH, D), 1.0),
        'ctx': nrm(ks[2], (BATCH, CTX_LEN, D), 1.0),
        'c_ctx': nrm(ks[3], (D,), 1.0),
        'w_mod': nrm(ks[4], (L, D, N_MOD * D), 0.5 * D ** -0.5),
        'b_mod': nrm(ks[5], (L, N_MOD * D), 0.01),
        'g_ffn1': gain(ks[6], (L, D)),
        'w1_gu': nrm(ks[7], (L, D, 2 * D_FF), D ** -0.5),
        'w1_down': nrm(ks[8], (L, D_FF, D), D_FF ** -0.5),
        'g_mix': gain(ks[9], (L, D)),
        'w_in': nrm(ks[10], (L, D, D_IN), D ** -0.5),
        'dw_weight': nrm(ks[11], (L, CONV_WIDTH, D_CONV), CONV_WIDTH ** -0.5),
        'dw_bias': nrm(ks[12], (L, D_CONV), 0.01),
        'conv_ln_g': gain(ks[13], (L, D_CONV)),
        'conv_ln_b': nrm(ks[14], (L, D_CONV), 0.01),
        'w_conv_out': nrm(ks[15], (L, D_CONV, D), D_CONV ** -0.5),
        'w_alpha_f': nrm(ks[16], (L, GLA_LOWRANK, QK_W), GLA_LOWRANK ** -0.5),
        'b_alpha_f': nrm(ks[17], (L, QK_W), 0.1),
        'w_alpha_b': nrm(ks[18], (L, GLA_LOWRANK, QK_W), GLA_LOWRANK ** -0.5),
        'b_alpha_b': nrm(ks[19], (L, QK_W), 0.1),
        'gla_norm_g': gain(ks[20], (L, V_W)),
        'w_gla_out': nrm(ks[21], (L, V_W, D), V_W ** -0.5),
        'w_out': nrm(ks[22], (L, D, D), D ** -0.5),
        'g_ffn2': gain(ks[23], (L, D)),
        'w2_gu': nrm(ks[24], (L, D, 2 * D_FF), D ** -0.5),
        'w2_down': nrm(ks[25], (L, D_FF, D), D_FF ** -0.5),
        'g_final': gain(ks[26], (D,)),
    }


def _fwd_reference(x, c, ctx, c_ctx, w_mod, b_mod, g_ffn1, w1_gu, w1_down, g_mix, w_in,
              dw_weight, dw_bias, conv_ln_g, conv_ln_b, w_conv_out, w_alpha_f, b_alpha_f,
              w_alpha_b, b_alpha_b, gla_norm_g, w_gla_out, w_out, g_ffn2, w2_gu, w2_down, g_final):
    q_scale = GLA_DK ** -0.5
    h = ctx
    for l in range(DEPTH):
        last = l == DEPTH - 1
        mx = jnp.split((jax.nn.silu(c) @ w_mod[l] + b_mod[l])[:, None, :], N_MOD, axis=-1)
        mc = jnp.split(jax.nn.silu(c_ctx) @ w_mod[l] + b_mod[l], N_MOD, axis=-1)

        x = half_ffn(x, g_ffn1[l], mx[0], mx[1], mx[2], w1_gu[l], w1_down[l])
        h = half_ffn(h, g_ffn1[l], mc[0], mc[1], mc[2], w1_gu[l], w1_down[l])

        conv_x, q_x, k_x, v_x, og_x, af_x, ab_x, ga_x, gb_x = split_in(
            modulate(x, g_mix[l], mx[3], mx[4]) @ w_in[l])
        conv_c, q_c, k_c, v_c, og_c, af_c, ab_c, ga_c, gb_c = split_in(
            modulate(h, g_mix[l], mc[3], mc[4]) @ w_in[l])

        k_c, v_c = heads(k_c, GLA_DK), heads(v_c, GLA_DV)
        laf_c = log_decay(af_c, w_alpha_f[l], b_alpha_f[l])
        lab_c = log_decay(ab_c, w_alpha_b[l], b_alpha_b[l])
        s_f = gla_final_state(k_c, v_c, laf_c)
        s_b = gla_final_state(flip(k_c), flip(v_c), flip(lab_c))

        o_x = bidir_gla(heads(q_x, GLA_DK) * q_scale, heads(k_x, GLA_DK), heads(v_x, GLA_DV),
                        log_decay(af_x, w_alpha_f[l], b_alpha_f[l]),
                        log_decay(ab_x, w_alpha_b[l], b_alpha_b[l]), s_f, s_b)
        mix_x = branch_merge(conv_x, o_x, og_x, ga_x, gb_x, dw_weight[l], dw_bias[l], conv_ln_g[l],
                             conv_ln_b[l], w_conv_out[l], gla_norm_g[l], w_gla_out[l], w_out[l])

        if not last:
            zeros = jnp.zeros_like(s_f)
            o_c = bidir_gla(heads(q_c, GLA_DK) * q_scale, k_c, v_c, laf_c, lab_c, zeros, zeros)
            mix_c = branch_merge(conv_c, o_c, og_c, ga_c, gb_c, dw_weight[l], dw_bias[l], conv_ln_g[l],
                                 conv_ln_b[l], w_conv_out[l], gla_norm_g[l], w_gla_out[l], w_out[l])
            h = h + mc[5] * mix_c
            h = half_ffn(h, g_ffn2[l], mc[6], mc[7], mc[8], w2_gu[l], w2_down[l])

        x = x + mx[5] * mix_x
        x = half_ffn(x, g_ffn2[l], mx[6], mx[7], mx[8], w2_gu[l], w2_down[l])
    return rmsnorm(x, g_final)


import jax as _jax
import jax.numpy as _jnp

TWIN_FORMAT = 'train_step'
FWD_PARAMS = ['x', 'c', 'ctx', 'c_ctx', 'w_mod', 'b_mod', 'g_ffn1', 'w1_gu', 'w1_down', 'g_mix', 'w_in', 'dw_weight', 'dw_bias', 'conv_ln_g', 'conv_ln_b', 'w_conv_out', 'w_alpha_f', 'b_alpha_f', 'w_alpha_b', 'b_alpha_b', 'gla_norm_g', 'w_gla_out', 'w_out', 'g_ffn2', 'w2_gu', 'w2_down', 'g_final']
TWIN_WEIGHTS = ['c_ctx', 'w_mod', 'b_mod', 'g_ffn1', 'w1_gu', 'w1_down', 'g_mix', 'w_in', 'dw_weight', 'dw_bias', 'conv_ln_g', 'conv_ln_b', 'w_conv_out', 'w_alpha_f', 'b_alpha_f', 'w_alpha_b', 'b_alpha_b', 'gla_norm_g', 'w_gla_out', 'w_out', 'g_ffn2', 'w2_gu', 'w2_down', 'g_final']
TWIN_DIFF_INPUT = 'x'
TWIN_INPUTS = ['x', 'c', 'ctx', 'c_ctx', 'w_mod', 'b_mod', 'g_ffn1', 'w1_gu', 'w1_down', 'g_mix', 'w_in', 'dw_weight', 'dw_bias', 'conv_ln_g', 'conv_ln_b', 'w_conv_out', 'w_alpha_f', 'b_alpha_f', 'w_alpha_b', 'b_alpha_b', 'gla_norm_g', 'w_gla_out', 'w_out', 'g_ffn2', 'w2_gu', 'w2_down', 'g_final', 'loss_target', 'm_c_ctx', 'm_w_mod', 'm_b_mod', 'm_g_ffn1', 'm_w1_gu', 'm_w1_down', 'm_g_mix', 'm_w_in', 'm_dw_weight', 'm_dw_bias', 'm_conv_ln_g', 'm_conv_ln_b', 'm_w_conv_out', 'm_w_alpha_f', 'm_b_alpha_f', 'm_w_alpha_b', 'm_b_alpha_b', 'm_gla_norm_g', 'm_w_gla_out', 'm_w_out', 'm_g_ffn2', 'm_w2_gu', 'm_w2_down', 'm_g_final', 'v_c_ctx', 'v_w_mod', 'v_b_mod', 'v_g_ffn1', 'v_w1_gu', 'v_w1_down', 'v_g_mix', 'v_w_in', 'v_dw_weight', 'v_dw_bias', 'v_conv_ln_g', 'v_conv_ln_b', 'v_w_conv_out', 'v_w_alpha_f', 'v_b_alpha_f', 'v_w_alpha_b', 'v_b_alpha_b', 'v_gla_norm_g', 'v_w_gla_out', 'v_w_out', 'v_g_ffn2', 'v_w2_gu', 'v_w2_down', 'v_g_final']
TWIN_OUTPUTS = ['loss', 'grad_x', 'grad_c_ctx', 'grad_w_mod', 'grad_b_mod', 'grad_g_ffn1', 'grad_w1_gu', 'grad_w1_down', 'grad_g_mix', 'grad_w_in', 'grad_dw_weight', 'grad_dw_bias', 'grad_conv_ln_g', 'grad_conv_ln_b', 'grad_w_conv_out', 'grad_w_alpha_f', 'grad_b_alpha_f', 'grad_w_alpha_b', 'grad_b_alpha_b', 'grad_gla_norm_g', 'grad_w_gla_out', 'grad_w_out', 'grad_g_ffn2', 'grad_w2_gu', 'grad_w2_down', 'grad_g_final', 'delta_c_ctx', 'delta_w_mod', 'delta_b_mod', 'delta_g_ffn1', 'delta_w1_gu', 'delta_w1_down', 'delta_g_mix', 'delta_w_in', 'delta_dw_weight', 'delta_dw_bias', 'delta_conv_ln_g', 'delta_conv_ln_b', 'delta_w_conv_out', 'delta_w_alpha_f', 'delta_b_alpha_f', 'delta_w_alpha_b', 'delta_b_alpha_b', 'delta_gla_norm_g', 'delta_w_gla_out', 'delta_w_out', 'delta_g_ffn2', 'delta_w2_gu', 'delta_w2_down', 'delta_g_final', 'new_m_c_ctx', 'new_m_w_mod', 'new_m_b_mod', 'new_m_g_ffn1', 'new_m_w1_gu', 'new_m_w1_down', 'new_m_g_mix', 'new_m_w_in', 'new_m_dw_weight', 'new_m_dw_bias', 'new_m_conv_ln_g', 'new_m_conv_ln_b', 'new_m_w_conv_out', 'new_m_w_alpha_f', 'new_m_b_alpha_f', 'new_m_w_alpha_b', 'new_m_b_alpha_b', 'new_m_gla_norm_g', 'new_m_w_gla_out', 'new_m_w_out', 'new_m_g_ffn2', 'new_m_w2_gu', 'new_m_w2_down', 'new_m_g_final', 'new_v_c_ctx', 'new_v_w_mod', 'new_v_b_mod', 'new_v_g_ffn1', 'new_v_w1_gu', 'new_v_w1_down', 'new_v_g_mix', 'new_v_w_in', 'new_v_dw_weight', 'new_v_dw_bias', 'new_v_conv_ln_g', 'new_v_conv_ln_b', 'new_v_w_conv_out', 'new_v_w_alpha_f', 'new_v_b_alpha_f', 'new_v_w_alpha_b', 'new_v_b_alpha_b', 'new_v_gla_norm_g', 'new_v_w_gla_out', 'new_v_w_out', 'new_v_g_ffn2', 'new_v_w2_gu', 'new_v_w2_down', 'new_v_g_final']
TWIN_LEAF_KINDS = {'loss': 'loss', 'grad_x': 'grad_x', 'grad_c_ctx': 'grad_w', 'grad_w_mod': 'grad_w', 'grad_b_mod': 'grad_w', 'grad_g_ffn1': 'grad_w', 'grad_w1_gu': 'grad_w', 'grad_w1_down': 'grad_w', 'grad_g_mix': 'grad_w', 'grad_w_in': 'grad_w', 'grad_dw_weight': 'grad_w', 'grad_dw_bias': 'grad_w', 'grad_conv_ln_g': 'grad_w', 'grad_conv_ln_b': 'grad_w', 'grad_w_conv_out': 'grad_w', 'grad_w_alpha_f': 'grad_w', 'grad_b_alpha_f': 'grad_w', 'grad_w_alpha_b': 'grad_w', 'grad_b_alpha_b': 'grad_w', 'grad_gla_norm_g': 'grad_w', 'grad_w_gla_out': 'grad_w', 'grad_w_out': 'grad_w', 'grad_g_ffn2': 'grad_w', 'grad_w2_gu': 'grad_w', 'grad_w2_down': 'grad_w', 'grad_g_final': 'grad_w', 'delta_c_ctx': 'delta_w', 'delta_w_mod': 'delta_w', 'delta_b_mod': 'delta_w', 'delta_g_ffn1': 'delta_w', 'delta_w1_gu': 'delta_w', 'delta_w1_down': 'delta_w', 'delta_g_mix': 'delta_w', 'delta_w_in': 'delta_w', 'delta_dw_weight': 'delta_w', 'delta_dw_bias': 'delta_w', 'delta_conv_ln_g': 'delta_w', 'delta_conv_ln_b': 'delta_w', 'delta_w_conv_out': 'delta_w', 'delta_w_alpha_f': 'delta_w', 'delta_b_alpha_f': 'delta_w', 'delta_w_alpha_b': 'delta_w', 'delta_b_alpha_b': 'delta_w', 'delta_gla_norm_g': 'delta_w', 'delta_w_gla_out': 'delta_w', 'delta_w_out': 'delta_w', 'delta_g_ffn2': 'delta_w', 'delta_w2_gu': 'delta_w', 'delta_w2_down': 'delta_w', 'delta_g_final': 'delta_w', 'new_m_c_ctx': 'new_m', 'new_m_w_mod': 'new_m', 'new_m_b_mod': 'new_m', 'new_m_g_ffn1': 'new_m', 'new_m_w1_gu': 'new_m', 'new_m_w1_down': 'new_m', 'new_m_g_mix': 'new_m', 'new_m_w_in': 'new_m', 'new_m_dw_weight': 'new_m', 'new_m_dw_bias': 'new_m', 'new_m_conv_ln_g': 'new_m', 'new_m_conv_ln_b': 'new_m', 'new_m_w_conv_out': 'new_m', 'new_m_w_alpha_f': 'new_m', 'new_m_b_alpha_f': 'new_m', 'new_m_w_alpha_b': 'new_m', 'new_m_b_alpha_b': 'new_m', 'new_m_gla_norm_g': 'new_m', 'new_m_w_gla_out': 'new_m', 'new_m_w_out': 'new_m', 'new_m_g_ffn2': 'new_m', 'new_m_w2_gu': 'new_m', 'new_m_w2_down': 'new_m', 'new_m_g_final': 'new_m', 'new_v_c_ctx': 'new_v', 'new_v_w_mod': 'new_v', 'new_v_b_mod': 'new_v', 'new_v_g_ffn1': 'new_v', 'new_v_w1_gu': 'new_v', 'new_v_w1_down': 'new_v', 'new_v_g_mix': 'new_v', 'new_v_w_in': 'new_v', 'new_v_dw_weight': 'new_v', 'new_v_dw_bias': 'new_v', 'new_v_conv_ln_g': 'new_v', 'new_v_conv_ln_b': 'new_v', 'new_v_w_conv_out': 'new_v', 'new_v_w_alpha_f': 'new_v', 'new_v_b_alpha_f': 'new_v', 'new_v_w_alpha_b': 'new_v', 'new_v_b_alpha_b': 'new_v', 'new_v_gla_norm_g': 'new_v', 'new_v_w_gla_out': 'new_v', 'new_v_w_out': 'new_v', 'new_v_g_ffn2': 'new_v', 'new_v_w2_gu': 'new_v', 'new_v_w2_down': 'new_v', 'new_v_g_final': 'new_v'}


def _forward(args):
    return _fwd_reference(*[args[k] for k in FWD_PARAMS])


def _output_shape():
    out = _jax.eval_shape(lambda: _forward(_fwd_setup_inputs(0)))
    return out.shape, out.dtype

N_MICROBATCH = 1
ADAM_LR = 0.001
ADAM_B1 = 0.9
ADAM_B2 = 0.999
ADAM_EPS = 1e-08
ADAM_WD = 0.01
ADAM_STEP = 10
PER_EXAMPLE_BATCH_AXIS = {'x': 0, 'c': 0, 'ctx': 0, 'loss_target': 0}
SHARED_INPUTS = []
_WEIGHT_DTYPES = {'c_ctx': _jnp.float32, 'w_mod': _jnp.float32, 'b_mod': _jnp.float32, 'g_ffn1': _jnp.float32, 'w1_gu': _jnp.float32, 'w1_down': _jnp.float32, 'g_mix': _jnp.float32, 'w_in': _jnp.float32, 'dw_weight': _jnp.float32, 'dw_bias': _jnp.float32, 'conv_ln_g': _jnp.float32, 'conv_ln_b': _jnp.float32, 'w_conv_out': _jnp.float32, 'w_alpha_f': _jnp.float32, 'b_alpha_f': _jnp.float32, 'w_alpha_b': _jnp.float32, 'b_alpha_b': _jnp.float32, 'gla_norm_g': _jnp.float32, 'w_gla_out': _jnp.float32, 'w_out': _jnp.float32, 'g_ffn2': _jnp.float32, 'w2_gu': _jnp.float32, 'w2_down': _jnp.float32, 'g_final': _jnp.float32}
MOMENT_SCALE = {'c_ctx': 4.171761e-03, 'w_mod': 5.197534e-02, 'b_mod': 9.037981e-02, 'g_ffn1': 3.836079e-02, 'w1_gu': 1.733824e-02, 'w1_down': 2.847224e-02, 'g_mix': 6.340329e-02, 'w_in': 2.504715e-02, 'dw_weight': 2.716647e-02, 'dw_bias': 4.360327e-02, 'conv_ln_g': 3.134846e-02, 'conv_ln_b': 2.554228e-02, 'w_conv_out': 2.516147e-02, 'w_alpha_f': 5.028444e-03, 'b_alpha_f': 1.279996e-02, 'w_alpha_b': 4.787917e-03, 'b_alpha_b': 1.281208e-02, 'gla_norm_g': 2.865964e-02, 'w_gla_out': 2.780516e-02, 'w_out': 3.773495e-02, 'g_ffn2': 4.237372e-02, 'w2_gu': 1.678269e-02, 'w2_down': 2.751674e-02, 'g_final': 6.398586e+01}


def _to_microbatches(a, axis):
    t = _jnp.moveaxis(a, axis, 0)
    t = t.reshape((N_MICROBATCH, t.shape[0] // N_MICROBATCH) + t.shape[1:])
    return _jnp.moveaxis(t, 1, axis + 1)


def setup_inputs(seed: int = 0) -> dict:
    inp = _fwd_setup_inputs(seed)
    key = _jax.random.fold_in(_jax.random.key(seed), 7919)
    shape, _ = _output_shape()
    out = dict(inp)
    out["loss_target"] = _jax.random.normal(_jax.random.fold_in(key, 0), shape, _jnp.float32)
    for i, name in enumerate(TWIN_WEIGHTS):
        w = inp[name].astype(_jnp.float32)
        if MOMENT_SCALE is None:
            s = _jnp.sqrt(_jnp.mean(_jnp.square(w)) + 1e-30)
        else:
            s = MOMENT_SCALE[name]
        km, kv = _jax.random.split(_jax.random.fold_in(key, i + 1))
        out[name] = w
        out["m_" + name] = s * _jax.random.normal(km, w.shape, _jnp.float32)
        out["v_" + name] = (s * s) * _jax.random.uniform(kv, w.shape, _jnp.float32, 0.5, 1.5)
    if N_MICROBATCH > 1:
        for name, axis in PER_EXAMPLE_BATCH_AXIS.items():
            out[name] = _to_microbatches(out[name], axis)
    return {'x': out['x'], 'c': out['c'], 'ctx': out['ctx'], 'c_ctx': out['c_ctx'], 'w_mod': out['w_mod'], 'b_mod': out['b_mod'], 'g_ffn1': out['g_ffn1'], 'w1_gu': out['w1_gu'], 'w1_down': out['w1_down'], 'g_mix': out['g_mix'], 'w_in': out['w_in'], 'dw_weight': out['dw_weight'], 'dw_bias': out['dw_bias'], 'conv_ln_g': out['conv_ln_g'], 'conv_ln_b': out['conv_ln_b'], 'w_conv_out': out['w_conv_out'], 'w_alpha_f': out['w_alpha_f'], 'b_alpha_f': out['b_alpha_f'], 'w_alpha_b': out['w_alpha_b'], 'b_alpha_b': out['b_alpha_b'], 'gla_norm_g': out['gla_norm_g'], 'w_gla_out': out['w_gla_out'], 'w_out': out['w_out'], 'g_ffn2': out['g_ffn2'], 'w2_gu': out['w2_gu'], 'w2_down': out['w2_down'], 'g_final': out['g_final'], 'loss_target': out['loss_target'], 'm_c_ctx': out['m_c_ctx'], 'm_w_mod': out['m_w_mod'], 'm_b_mod': out['m_b_mod'], 'm_g_ffn1': out['m_g_ffn1'], 'm_w1_gu': out['m_w1_gu'], 'm_w1_down': out['m_w1_down'], 'm_g_mix': out['m_g_mix'], 'm_w_in': out['m_w_in'], 'm_dw_weight': out['m_dw_weight'], 'm_dw_bias': out['m_dw_bias'], 'm_conv_ln_g': out['m_conv_ln_g'], 'm_conv_ln_b': out['m_conv_ln_b'], 'm_w_conv_out': out['m_w_conv_out'], 'm_w_alpha_f': out['m_w_alpha_f'], 'm_b_alpha_f': out['m_b_alpha_f'], 'm_w_alpha_b': out['m_w_alpha_b'], 'm_b_alpha_b': out['m_b_alpha_b'], 'm_gla_norm_g': out['m_gla_norm_g'], 'm_w_gla_out': out['m_w_gla_out'], 'm_w_out': out['m_w_out'], 'm_g_ffn2': out['m_g_ffn2'], 'm_w2_gu': out['m_w2_gu'], 'm_w2_down': out['m_w2_down'], 'm_g_final': out['m_g_final'], 'v_c_ctx': out['v_c_ctx'], 'v_w_mod': out['v_w_mod'], 'v_b_mod': out['v_b_mod'], 'v_g_ffn1': out['v_g_ffn1'], 'v_w1_gu': out['v_w1_gu'], 'v_w1_down': out['v_w1_down'], 'v_g_mix': out['v_g_mix'], 'v_w_in': out['v_w_in'], 'v_dw_weight': out['v_dw_weight'], 'v_dw_bias': out['v_dw_bias'], 'v_conv_ln_g': out['v_conv_ln_g'], 'v_conv_ln_b': out['v_conv_ln_b'], 'v_w_conv_out': out['v_w_conv_out'], 'v_w_alpha_f': out['v_w_alpha_f'], 'v_b_alpha_f': out['v_b_alpha_f'], 'v_w_alpha_b': out['v_w_alpha_b'], 'v_b_alpha_b': out['v_b_alpha_b'], 'v_gla_norm_g': out['v_gla_norm_g'], 'v_w_gla_out': out['v_w_gla_out'], 'v_w_out': out['v_w_out'], 'v_g_ffn2': out['v_g_ffn2'], 'v_w2_gu': out['v_w2_gu'], 'v_w2_down': out['v_w2_down'], 'v_g_final': out['v_g_final']}


def _loss(weights, diff, rest, loss_target):
    with _jax.named_scope("forward"):
        args = {**rest, TWIN_DIFF_INPUT: diff, **{k: w.astype(_WEIGHT_DTYPES[k]) for k, w in weights.items()}}
        y = _forward(args)
    with _jax.named_scope("loss_head"):
        err = _jnp.square(y.astype(_jnp.float32) - loss_target)
        return 0.5 * _jnp.sum(_jnp.mean(err, axis=-1)) if err.ndim else 0.5 * err


def _adamw(w, g, m, v):
    m = ADAM_B1 * m + (1.0 - ADAM_B1) * g
    v = ADAM_B2 * v + (1.0 - ADAM_B2) * _jnp.square(g)
    m_hat = m / (1.0 - ADAM_B1 ** ADAM_STEP)
    v_hat = v / (1.0 - ADAM_B2 ** ADAM_STEP)
    delta = -ADAM_LR * (m_hat / (_jnp.sqrt(v_hat) + ADAM_EPS) + ADAM_WD * w)
    return delta, m, v


def reference(x, c, ctx, c_ctx, w_mod, b_mod, g_ffn1, w1_gu, w1_down, g_mix, w_in, dw_weight, dw_bias, conv_ln_g, conv_ln_b, w_conv_out, w_alpha_f, b_alpha_f, w_alpha_b, b_alpha_b, gla_norm_g, w_gla_out, w_out, g_ffn2, w2_gu, w2_down, g_final, loss_target, m_c_ctx, m_w_mod, m_b_mod, m_g_ffn1, m_w1_gu, m_w1_down, m_g_mix, m_w_in, m_dw_weight, m_dw_bias, m_conv_ln_g, m_conv_ln_b, m_w_conv_out, m_w_alpha_f, m_b_alpha_f, m_w_alpha_b, m_b_alpha_b, m_gla_norm_g, m_w_gla_out, m_w_out, m_g_ffn2, m_w2_gu, m_w2_down, m_g_final, v_c_ctx, v_w_mod, v_b_mod, v_g_ffn1, v_w1_gu, v_w1_down, v_g_mix, v_w_in, v_dw_weight, v_dw_bias, v_conv_ln_g, v_conv_ln_b, v_w_conv_out, v_w_alpha_f, v_b_alpha_f, v_w_alpha_b, v_b_alpha_b, v_gla_norm_g, v_w_gla_out, v_w_out, v_g_ffn2, v_w2_gu, v_w2_down, v_g_final):
    given = dict(x=x, c=c, ctx=ctx, c_ctx=c_ctx, w_mod=w_mod, b_mod=b_mod, g_ffn1=g_ffn1, w1_gu=w1_gu, w1_down=w1_down, g_mix=g_mix, w_in=w_in, dw_weight=dw_weight, dw_bias=dw_bias, conv_ln_g=conv_ln_g, conv_ln_b=conv_ln_b, w_conv_out=w_conv_out, w_alpha_f=w_alpha_f, b_alpha_f=b_alpha_f, w_alpha_b=w_alpha_b, b_alpha_b=b_alpha_b, gla_norm_g=gla_norm_g, w_gla_out=w_gla_out, w_out=w_out, g_ffn2=g_ffn2, w2_gu=w2_gu, w2_down=w2_down, g_final=g_final, loss_target=loss_target, m_c_ctx=m_c_ctx, m_w_mod=m_w_mod, m_b_mod=m_b_mod, m_g_ffn1=m_g_ffn1, m_w1_gu=m_w1_gu, m_w1_down=m_w1_down, m_g_mix=m_g_mix, m_w_in=m_w_in, m_dw_weight=m_dw_weight, m_dw_bias=m_dw_bias, m_conv_ln_g=m_conv_ln_g, m_conv_ln_b=m_conv_ln_b, m_w_conv_out=m_w_conv_out, m_w_alpha_f=m_w_alpha_f, m_b_alpha_f=m_b_alpha_f, m_w_alpha_b=m_w_alpha_b, m_b_alpha_b=m_b_alpha_b, m_gla_norm_g=m_gla_norm_g, m_w_gla_out=m_w_gla_out, m_w_out=m_w_out, m_g_ffn2=m_g_ffn2, m_w2_gu=m_w2_gu, m_w2_down=m_w2_down, m_g_final=m_g_final, v_c_ctx=v_c_ctx, v_w_mod=v_w_mod, v_b_mod=v_b_mod, v_g_ffn1=v_g_ffn1, v_w1_gu=v_w1_gu, v_w1_down=v_w1_down, v_g_mix=v_g_mix, v_w_in=v_w_in, v_dw_weight=v_dw_weight, v_dw_bias=v_dw_bias, v_conv_ln_g=v_conv_ln_g, v_conv_ln_b=v_conv_ln_b, v_w_conv_out=v_w_conv_out, v_w_alpha_f=v_w_alpha_f, v_b_alpha_f=v_b_alpha_f, v_w_alpha_b=v_w_alpha_b, v_b_alpha_b=v_b_alpha_b, v_gla_norm_g=v_gla_norm_g, v_w_gla_out=v_w_gla_out, v_w_out=v_w_out, v_g_ffn2=v_g_ffn2, v_w2_gu=v_w2_gu, v_w2_down=v_w2_down, v_g_final=v_g_final)
    weights = {n: given[n] for n in TWIN_WEIGHTS}
    shared = {n: given[n] for n in SHARED_INPUTS}
    per_example = {n: given[n] for n in ['x', 'c', 'ctx']}
    grad_fn = _jax.value_and_grad(_loss, argnums=(0, 1))

    def one_microbatch(ex, loss_target):
        ex = dict(ex)
        diff = ex.pop(TWIN_DIFF_INPUT)
        return grad_fn(weights, diff, {**shared, **ex}, loss_target)

    if N_MICROBATCH == 1:
        loss, (grad_w, grad_x) = one_microbatch(per_example, given["loss_target"])
    else:
        def body(carry, xs):
            loss_sum, grad_sum = carry
            l_k, (gw_k, gx_k) = one_microbatch(xs[0], xs[1])
            with _jax.named_scope("update"):
                return (loss_sum + l_k, _jax.tree.map(_jnp.add, grad_sum, gw_k)), gx_k

        init = (_jnp.zeros((), _jnp.float32), _jax.tree.map(_jnp.zeros_like, weights))
        (loss, grad_w), grad_x = _jax.lax.scan(body, init, (per_example, given["loss_target"]))
    with _jax.named_scope("update"):
        delta_w, new_m, new_v = {}, {}, {}
        for n in TWIN_WEIGHTS:
            delta_w[n], new_m[n], new_v[n] = _adamw(weights[n], grad_w[n], given["m_" + n], given["v_" + n])
    return (loss, grad_x, *[grad_w[n] for n in TWIN_WEIGHTS], *[delta_w[n] for n in TWIN_WEIGHTS],
            *[new_m[n] for n in TWIN_WEIGHTS], *[new_v[n] for n in TWIN_WEIGHTS])
```

```python
import functools

import jax
import jax.numpy as jnp
from jax import lax
from jax.experimental import pallas as pl
from jax.experimental.pallas import tpu as pltpu

f32, bf16 = jnp.float32, jnp.bfloat16

N_DEV = 8
HEADS = 4
LOWRANK = 16
CONV_W = 31
CONV_PAD = 16
CHUNK = 64
SUB = 16
GLA_ROWS = 256
TAU = 16.0
EPS = 1e-6
N_MOD = 9
LR_PAD = 128
ROW_TILE = 256
V7X_VMEM_BYTES = 64 << 20
VMEM_LIMIT = (V7X_VMEM_BYTES * 3) // 4

ADAM_LR, ADAM_B1, ADAM_B2, ADAM_EPS, ADAM_WD, ADAM_STEP = 0.001, 0.9, 0.999, 1e-08, 0.01, 10

MESH = pl.DeviceIdType.MESH


def _pc(body, **kw):
    return pl.pallas_call(body, **kw)


def _params(*sem):
    return pltpu.CompilerParams(dimension_semantics=sem, vmem_limit_bytes=VMEM_LIMIT)


def _pick(n, cap, unit=128):
    best = None
    for t in range(unit, min(n, cap) + 1, unit):
        if n % t == 0:
            best = t
    return best or n


def _matmul(a, b, mode, out_dtype, name, tm_cap=512, tn_cap=512, tk_cap=2048):
    if mode == "tn":
        (K, M), N = a.shape, b.shape[1]
    elif mode == "nt":
        (M, K), N = a.shape, b.shape[0]
    else:
        (M, K), N = a.shape, b.shape[1]
    tm, tn, tk = _pick(M, tm_cap), _pick(N, tn_cap), _pick(K, tk_cap)
    nk = K // tk
    a_spec = pl.BlockSpec((tk, tm), lambda i, j, k: (k, i)) if mode == "tn" else pl.BlockSpec((tm, tk), lambda i, j, k: (i, k))
    b_spec = pl.BlockSpec((tn, tk), lambda i, j, k: (j, k)) if mode == "nt" else pl.BlockSpec((tk, tn), lambda i, j, k: (k, j))
    dims = {"nn": ((1,), (0,)), "nt": ((1,), (1,)), "tn": ((0,), (0,))}[mode]

    def body(a_ref, b_ref, o_ref, acc_ref):
        k = pl.program_id(2)
        part = lax.dot_general(a_ref[...].astype(bf16), b_ref[...].astype(bf16), (dims, ((), ())), preferred_element_type=f32)

        @pl.when(k == 0)
        def _():
            acc_ref[...] = part

        @pl.when(k > 0)
        def _():
            acc_ref[...] += part

        @pl.when(k == nk - 1)
        def _():
            o_ref[...] = acc_ref[...].astype(out_dtype)

    return _pc(
        body, name=name, grid=(M // tm, N // tn, nk), in_specs=[a_spec, b_spec],
        out_specs=pl.BlockSpec((tm, tn), lambda i, j, k: (i, j)), out_shape=jax.ShapeDtypeStruct((M, N), out_dtype),
        scratch_shapes=[pltpu.VMEM((tm, tn), f32)], compiler_params=_params("parallel", "parallel", "arbitrary"),
    )(a, b)


def _rowwise(fn, *, name, tm, n_tiles, tpe, nx_tiles, n_ex, tok_in=(), ex_in=(), sh_in=(), tok_out=(), ex_out=(), gl_out=()):
    def seg(i):
        return jnp.minimum(i // tpe, n_ex - 1)

    in_specs, args = [], []
    for arr, w, cb, x_only in tok_in:
        if x_only:
            in_specs.append(pl.BlockSpec((tm, w), functools.partial(lambda i, cb: (jnp.minimum(i, nx_tiles - 1), cb), cb=cb)))
        else:
            in_specs.append(pl.BlockSpec((tm, w), functools.partial(lambda i, cb: (i, cb), cb=cb)))
        args.append(arr)
    for arr in ex_in:
        in_specs.append(pl.BlockSpec((1, 1, arr.shape[-1]), lambda i: (seg(i), 0, 0)))
        args.append(arr)
    for arr in sh_in:
        in_specs.append(pl.BlockSpec(arr.shape, functools.partial(lambda i, nd: (0,) * nd, nd=arr.ndim)))
        args.append(arr)
    out_specs, out_shape = [], []
    for w, dt in tok_out:
        out_specs.append(pl.BlockSpec((tm, w), lambda i: (i, 0)))
        out_shape.append(jax.ShapeDtypeStruct((n_tiles * tm, w), dt))
    for w in ex_out:
        out_specs.append(pl.BlockSpec((1, 1, w), lambda i: (seg(i), 0, 0)))
        out_shape.append(jax.ShapeDtypeStruct((n_ex, 1, w), f32))
    for r, w in gl_out:
        out_specs.append(pl.BlockSpec((r, w), lambda i: (0, 0)))
        out_shape.append(jax.ShapeDtypeStruct((r, w), f32))
    n_tok, n_exi, n_sh = len(tok_in), len(ex_in), len(sh_in)
    n_to, n_eo = len(tok_out), len(ex_out)
    x_only_flags = [t[3] for t in tok_in]

    def body(*refs):
        i = pl.program_id(0)
        ins, outs = refs[: n_tok + n_exi + n_sh], refs[n_tok + n_exi + n_sh:]
        is_x = i < nx_tiles
        tok_vals = []
        for r, xo in zip(ins[:n_tok], x_only_flags):
            v = r[...]
            tok_vals.append(jnp.where(is_x, v, jnp.zeros_like(v)) if xo else v)
        ex_vals = [r[0] for r in ins[n_tok:n_tok + n_exi]]
        sh_vals = [r[...] for r in ins[n_tok + n_exi:]]
        t_o, e_o, g_o = fn(tok_vals, ex_vals, sh_vals)
        for r, v in zip(outs[:n_to], t_o):
            r[...] = v.astype(r.dtype)
        first = jnp.logical_and(i % tpe == 0, i <= nx_tiles)
        for r, v in zip(outs[n_to:n_to + n_eo], e_o):
            @pl.when(first)
            def _(r=r, v=v):
                r[0] = v

            @pl.when(jnp.logical_not(first))
            def _(r=r, v=v):
                r[0] += v
        for r, v in zip(outs[n_to + n_eo:], g_o):
            @pl.when(i == 0)
            def _(r=r, v=v):
                r[...] = v

            @pl.when(i > 0)
            def _(r=r, v=v):
                r[...] += v

    res = _pc(body, name=name, grid=(n_tiles,), in_specs=in_specs, out_specs=out_specs, out_shape=out_shape,
              compiler_params=_params("arbitrary"))(*args)
    return list(res)


def _rms_mod(x, g, sh, sc):
    y = x * lax.rsqrt(jnp.mean(x * x, axis=-1, keepdims=True) + EPS) * g
    return y * (1.0 + sc) + sh


def _log_sigmoid(z):
    return jnp.minimum(z, 0.0) - jnp.log(1.0 + jnp.exp(-jnp.abs(z)))


def _swiglu(gu, F):
    a, b = gu[:, :F].astype(f32), gu[:, F:].astype(f32)
    return jax.nn.silu(a) * b


def _head_rms(o, DV):
    parts = []
    for h in range(HEADS):
        oh = o[:, h * DV:(h + 1) * DV]
        parts.append(oh * lax.rsqrt(jnp.mean(oh * oh, axis=-1, keepdims=True) + EPS))
    return jnp.concatenate(parts, axis=1)


def _gla_chunk(St, q, k, v, g, *, rev, scale):
    C, DK = q.shape
    r = lax.broadcasted_iota(jnp.int32, (C, C), 0)
    c = lax.broadcasted_iota(jnp.int32, (C, C), 1)
    tri = ((r <= c) if rev else (r >= c)).astype(f32)
    b = jnp.dot(tri, g, precision=lax.Precision.HIGHEST, preferred_element_type=f32)
    qs = q * scale
    inter = lax.dot_general((qs * jnp.exp(b)).astype(bf16), St.astype(bf16), (((1,), (1,)), ((), ())), preferred_element_type=f32)
    rr = lax.broadcasted_iota(jnp.int32, (SUB, SUB, DK), 0)
    cc = lax.broadcasted_iota(jnp.int32, (SUB, SUB, DK), 1)
    m3 = (rr <= cc) if rev else (rr >= cc)
    outs = []
    for i in range(C // SUB):
        lo, hi = i * SUB, (i + 1) * SUB
        bi, qi, ki, vi = b[lo:hi], qs[lo:hi], k[lo:hi], v[lo:hi]
        rel = bi[:, None, :] - bi[None, :, :]
        e = jnp.where(m3, jnp.exp(jnp.where(m3, rel, 0.0)), 0.0)
        att = jnp.sum(qi[:, None, :] * e * ki[None, :, :], axis=-1)
        acc = jnp.dot(att.astype(bf16), vi.astype(bf16), preferred_element_type=f32)
        ref_row = b[hi - 1:hi] if rev else b[lo:lo + 1]
        prev = slice(hi, C) if rev else slice(0, lo)
        if (hi < C) if rev else (lo > 0):
            qn = qi * jnp.exp(bi - ref_row)
            ks = k[prev] * jnp.exp(ref_row - b[prev])
            att_o = lax.dot_general(qn.astype(bf16), ks.astype(bf16), (((1,), (1,)), ((), ())), preferred_element_type=f32)
            acc = acc + jnp.dot(att_o.astype(bf16), v[prev].astype(bf16), preferred_element_type=f32)
        outs.append(acc)
    o = inter + jnp.concatenate(outs, axis=0)
    b_last = b[0:1] if rev else b[C - 1:C]
    kd = k * jnp.exp(b_last - b)
    St_new = St * jnp.exp(b_last) + lax.dot_general(v.astype(bf16), kd.astype(bf16), (((0,), (0,)), ((), ())), preferred_element_type=f32)
    return St_new, o


def _gla_specs(D, rev_blocks, row0, seq):
    DK, DV = D // (2 * HEADS), D // HEADS
    nblk = seq // GLA_ROWS
    rb0 = row0 // GLA_ROWS
    qb, kb, vb = 6 * D // DK, 13 * D // (2 * DK), 2 * D // DV

    def blk(j):
        return (nblk - 1 - j) if rev_blocks else j

    return DK, DV, nblk, rb0, qb, kb, vb, blk


def _gla_fwd(p_all, la_all, s0, *, rev, row0, nb, seq, D, name):
    DK, DV, nblk, rb0, qb, kb, vb, blk = _gla_specs(D, rev, row0, seq)
    cpb = GLA_ROWS // CHUNK
    lab = HEADS if rev else 0

    def rows(b, j):
        return rb0 + b * nblk + blk(j)

    in_specs = [
        pl.BlockSpec((GLA_ROWS, DK), lambda b, h, j: (rows(b, j), qb + h)),
        pl.BlockSpec((GLA_ROWS, DK), lambda b, h, j: (rows(b, j), kb + h)),
        pl.BlockSpec((GLA_ROWS, DV), lambda b, h, j: (rows(b, j), vb + h)),
        pl.BlockSpec((GLA_ROWS, DK), lambda b, h, j: (rows(b, j), lab + h)),
        pl.BlockSpec((1, 1, DV, DK), lambda b, h, j: (b, h, 0, 0)),
    ]
    out_specs = [
        pl.BlockSpec((GLA_ROWS, DV), lambda b, h, j: (b * nblk + blk(j), h)),
        pl.BlockSpec((1, 1, cpb, DV, DK), lambda b, h, j: (b, h, blk(j), 0, 0)),
        pl.BlockSpec((1, 1, DV, DK), lambda b, h, j: (b, h, 0, 0)),
    ]
    out_shape = [
        jax.ShapeDtypeStruct((nb * seq, D), f32),
        jax.ShapeDtypeStruct((nb, HEADS, seq // CHUNK, DV, DK), f32),
        jax.ShapeDtypeStruct((nb, HEADS, DV, DK), f32),
    ]
    chunk = functools.partial(_gla_chunk, rev=rev, scale=DK ** -0.5)

    def body(q_ref, k_ref, v_ref, la_ref, s0_ref, o_ref, hist_ref, sfin_ref, st_ref):
        j = pl.program_id(2)

        @pl.when(j == 0)
        def _():
            st_ref[...] = s0_ref[0, 0]

        def step(ci, St):
            cc = (cpb - 1 - ci) if rev else ci
            sl = pl.ds(pl.multiple_of(cc * CHUNK, CHUNK), CHUNK)
            hist_ref[0, 0, cc] = St
            St2, o = chunk(St, q_ref[sl, :].astype(f32), k_ref[sl, :].astype(f32), v_ref[sl, :].astype(f32), la_ref[sl, :])
            o_ref[sl, :] = o
            return St2

        St = lax.fori_loop(0, cpb, step, st_ref[...])
        st_ref[...] = St

        @pl.when(j == nblk - 1)
        def _():
            sfin_ref[0, 0] = St

    return _pc(body, name=name, grid=(nb, HEADS, nblk), in_specs=in_specs, out_specs=out_specs, out_shape=out_shape,
               scratch_shapes=[pltpu.VMEM((DV, DK), f32)], compiler_params=_params("parallel", "parallel", "arbitrary"))(
        p_all, p_all, p_all, la_all, s0)


def _gla_bwd(p_all, la_all, hist, do, dsfin, *, rev, row0, nb, seq, D, name):
    DK, DV, nblk, rb0, qb, kb, vb, blk = _gla_specs(D, not rev, row0, seq)
    cpb = GLA_ROWS // CHUNK
    lab = HEADS if rev else 0
    QK = HEADS * DK
    has_do = do is not None

    def rows(b, j):
        return rb0 + b * nblk + blk(j)

    in_specs = [
        pl.BlockSpec((GLA_ROWS, DK), lambda b, h, j: (rows(b, j), qb + h)),
        pl.BlockSpec((GLA_ROWS, DK), lambda b, h, j: (rows(b, j), kb + h)),
        pl.BlockSpec((GLA_ROWS, DV), lambda b, h, j: (rows(b, j), vb + h)),
        pl.BlockSpec((GLA_ROWS, DK), lambda b, h, j: (rows(b, j), lab + h)),
        pl.BlockSpec((1, 1, cpb, DV, DK), lambda b, h, j: (b, h, blk(j), 0, 0)),
        pl.BlockSpec((1, 1, DV, DK), lambda b, h, j: (b, h, 0, 0)),
    ]
    args = [p_all, p_all, p_all, la_all, hist, dsfin]
    if has_do:
        in_specs.append(pl.BlockSpec((GLA_ROWS, DV), lambda b, h, j: (b * nblk + blk(j), h)))
        args.append(do)
    out_specs = [
        pl.BlockSpec((GLA_ROWS, DK), lambda b, h, j: (b * nblk + blk(j), h)),
        pl.BlockSpec((GLA_ROWS, DK), lambda b, h, j: (b * nblk + blk(j), h)),
        pl.BlockSpec((GLA_ROWS, DV), lambda b, h, j: (b * nblk + blk(j), h)),
        pl.BlockSpec((GLA_ROWS, DK), lambda b, h, j: (b * nblk + blk(j), h)),
        pl.BlockSpec((1, 1, DV, DK), lambda b, h, j: (b, h, 0, 0)),
    ]
    out_shape = [
        jax.ShapeDtypeStruct((nb * seq, QK), f32), jax.ShapeDtypeStruct((nb * seq, QK), f32),
        jax.ShapeDtypeStruct((nb * seq, D), f32), jax.ShapeDtypeStruct((nb * seq, QK), f32),
        jax.ShapeDtypeStruct((nb, HEADS, DV, DK), f32),
    ]
    chunk = functools.partial(_gla_chunk, rev=rev, scale=DK ** -0.5)

    def body(*refs):
        if has_do:
            q_ref, k_ref, v_ref, la_ref, hist_ref, dsfin_ref, do_ref, dq_ref, dk_ref, dv_ref, dla_ref, ds0_ref, ds_ref = refs
        else:
            q_ref, k_ref, v_ref, la_ref, hist_ref, dsfin_ref, dq_ref, dk_ref, dv_ref, dla_ref, ds0_ref, ds_ref = refs
        j = pl.program_id(2)

        @pl.when(j == 0)
        def _():
            ds_ref[...] = dsfin_ref[0, 0]

        def step(ci, dS):
            cc = ci if rev else (cpb - 1 - ci)
            sl = pl.ds(pl.multiple_of(cc * CHUNK, CHUNK), CHUNK)
            prim = (hist_ref[0, 0, cc], q_ref[sl, :].astype(f32), k_ref[sl, :].astype(f32), v_ref[sl, :].astype(f32), la_ref[sl, :])
            _, vjp = jax.vjp(chunk, *prim)
            d_o = do_ref[sl, :] if has_do else jnp.zeros((CHUNK, DV), f32)
            dSt, dq, dk, dv, dg = vjp((dS, d_o))
            dq_ref[sl, :] = dq
            dk_ref[sl, :] = dk
            dv_ref[sl, :] = dv
            dla_ref[sl, :] = dg
            return dSt

        dS = lax.fori_loop(0, cpb, step, ds_ref[...])
        ds_ref[...] = dS

        @pl.when(j == nblk - 1)
        def _():
            ds0_ref[0, 0] = dS

    return _pc(body, name=name, grid=(nb, HEADS, nblk), in_specs=in_specs, out_specs=out_specs, out_shape=out_shape,
               scratch_shapes=[pltpu.VMEM((DV, DK), f32)], compiler_params=_params("parallel", "parallel", "arbitrary"))(*args)


def _conv_fwd(p_all, dw_w, dw_b, *, B, L, D, name):
    ct = _pick(D, 256)
    nj = D // ct
    st = _pick(L, 256, 8)

    def body(a_ref, b_ref, w_ref, bias_ref, o_ref, zp_ref):
        zp_ref[pl.ds(0, CONV_PAD), :] = jnp.zeros((CONV_PAD, ct), f32)
        zp_ref[pl.ds(CONV_PAD + L, CONV_PAD), :] = jnp.zeros((CONV_PAD, ct), f32)
        zp_ref[pl.ds(CONV_PAD, L), :] = a_ref[...].astype(f32) * jax.nn.sigmoid(b_ref[...].astype(f32))
        off = CONV_PAD - CONV_W // 2
        for t0 in range(0, L, st):
            acc = jnp.zeros((st, ct), f32) + bias_ref[...]
            for k in range(CONV_W):
                acc = acc + w_ref[pl.ds(k, 1), :] * zp_ref[pl.ds(t0 + k + off, st), :]
            o_ref[pl.ds(t0, st), :] = acc

    return _pc(
        body, name=name, grid=(B, nj),
        in_specs=[pl.BlockSpec((L, ct), lambda b, j: (b, j)), pl.BlockSpec((L, ct), lambda b, j: (b, nj + j)),
                  pl.BlockSpec((CONV_W, ct), lambda b, j: (0, j)), pl.BlockSpec((1, ct), lambda b, j: (0, j))],
        out_specs=pl.BlockSpec((L, ct), lambda b, j: (b, j)), out_shape=jax.ShapeDtypeStruct((B * L, D), f32),
        scratch_shapes=[pltpu.VMEM((L + 2 * CONV_PAD, ct), f32)], compiler_params=_params("parallel", "parallel"),
    )(p_all, p_all, dw_w, dw_b)


def _conv_bwd(p_all, dcz, dw_w, *, B, L, D, name):
    ct = _pick(D, 256)
    nj = D // ct
    st = _pick(L, 256, 8)
    half = CONV_W // 2

    def body(a_ref, b_ref, dcz_ref, w_ref, da_ref, db_ref, ddw_ref, zp_ref, dp_ref):
        bi = pl.program_id(1)
        a = a_ref[...].astype(f32)
        sg = jax.nn.sigmoid(b_ref[...].astype(f32))
        for ref in (zp_ref, dp_ref):
            ref[pl.ds(0, CONV_PAD), :] = jnp.zeros((CONV_PAD, ct), f32)
            ref[pl.ds(CONV_PAD + L, CONV_PAD), :] = jnp.zeros((CONV_PAD, ct), f32)
        zp_ref[pl.ds(CONV_PAD, L), :] = a * sg
        dp_ref[pl.ds(CONV_PAD, L), :] = dcz_ref[...]

        @pl.when(bi == 0)
        def _():
            ddw_ref[...] = jnp.zeros_like(ddw_ref)

        for t0 in range(0, L, st):
            acc = jnp.zeros((st, ct), f32)
            dout = dcz_ref[pl.ds(t0, st), :]
            for k in range(CONV_W):
                acc = acc + w_ref[pl.ds(k, 1), :] * dp_ref[pl.ds(t0 + CONV_PAD + half - k, st), :]
                ddw_ref[pl.ds(k, 1), :] += jnp.sum(dout * zp_ref[pl.ds(t0 + k + CONV_PAD - half, st), :], axis=0, keepdims=True)
            a_t = a_ref[pl.ds(t0, st), :].astype(f32)
            sg_t = jax.nn.sigmoid(b_ref[pl.ds(t0, st), :].astype(f32))
            da_ref[pl.ds(t0, st), :] = (acc * sg_t).astype(bf16)
            db_ref[pl.ds(t0, st), :] = (acc * a_t * sg_t * (1.0 - sg_t)).astype(bf16)

    return _pc(
        body, name=name, grid=(nj, B),
        in_specs=[pl.BlockSpec((L, ct), lambda j, b: (b, j)), pl.BlockSpec((L, ct), lambda j, b: (b, nj + j)),
                  pl.BlockSpec((L, ct), lambda j, b: (b, j)), pl.BlockSpec((CONV_W, ct), lambda j, b: (0, j))],
        out_specs=[pl.BlockSpec((L, ct), lambda j, b: (b, j)), pl.BlockSpec((L, ct), lambda j, b: (b, j)),
                   pl.BlockSpec((2 * CONV_PAD, ct), lambda j, b: (0, j))],
        out_shape=[jax.ShapeDtypeStruct((B * L, D), bf16), jax.ShapeDtypeStruct((B * L, D), bf16),
                   jax.ShapeDtypeStruct((2 * CONV_PAD, D), f32)],
        scratch_shapes=[pltpu.VMEM((L + 2 * CONV_PAD, ct), f32), pltpu.VMEM((L + 2 * CONV_PAD, ct), f32)],
        compiler_params=_params("parallel", "arbitrary"),
    )(p_all, p_all, dcz, dw_w)


def _exchange(arrs, scatter, name):
    n = len(arrs)
    out_shape = [jax.ShapeDtypeStruct(((N_DEV,) + a.shape[1:]) if scatter else ((N_DEV,) + a.shape), a.dtype) for a in arrs]

    def body(*refs):
        ins, outs = refs[:n], refs[n:2 * n]
        send_sems, recv_sems, local_sems = refs[2 * n:]
        x, y, c = lax.axis_index("x"), lax.axis_index("y"), lax.axis_index("c")
        me = 4 * x + 2 * y + c
        local = []
        for a in range(n):
            cp = pltpu.make_async_copy(ins[a].at[me] if scatter else ins[a], outs[a].at[me], local_sems.at[a])
            cp.start()
            local.append(cp)
        sends = []
        for k in range(1, N_DEV):
            p = (me + k) % N_DEV
            for a in range(n):
                cp = pltpu.make_async_remote_copy(
                    src_ref=ins[a].at[p] if scatter else ins[a], dst_ref=outs[a].at[me],
                    send_sem=send_sems.at[a, k - 1], recv_sem=recv_sems.at[a, k - 1],
                    device_id=(p // 4, (p // 2) % 2, p % 2), device_id_type=MESH)
                cp.start()
                sends.append(cp)
        for k in range(1, N_DEV):
            s = (me + N_DEV - k) % N_DEV
            for a in range(n):
                pltpu.make_async_remote_copy(
                    src_ref=ins[a].at[s] if scatter else ins[a], dst_ref=outs[a].at[s],
                    send_sem=send_sems.at[a, k - 1], recv_sem=recv_sems.at[a, k - 1],
                    device_id=(s // 4, (s // 2) % 2, s % 2), device_id_type=MESH).wait_recv()
        for cp in sends:
            cp.wait_send()
        for cp in local:
            cp.wait()

    any_spec = pl.BlockSpec(memory_space=pl.ANY)
    res = _pc(body, name=name, in_specs=[any_spec] * n, out_specs=[any_spec] * n, out_shape=out_shape,
              scratch_shapes=[pltpu.SemaphoreType.DMA((n, N_DEV - 1)), pltpu.SemaphoreType.DMA((n, N_DEV - 1)),
                              pltpu.SemaphoreType.DMA((n,))])(*arrs)
    return list(res)


def _mod_fwd(c_all, c_ctx, w_loc, b_loc, name):
    nr, D = c_all.shape
    nc = w_loc.shape[1]

    def body(c_ref, cc_ref, w_ref, b_ref, o_ref):
        a = jnp.concatenate([c_ref[...], jnp.broadcast_to(cc_ref[...], (8, D))], axis=0)
        s = jax.nn.silu(a).astype(bf16)
        o_ref[...] = jnp.dot(s, w_ref[...].astype(bf16), preferred_element_type=f32) + b_ref[...]

    return _pc(body, name=name, out_shape=jax.ShapeDtypeStruct((nr + 8, nc), f32), compiler_params=_params())(c_all, c_ctx, w_loc, b_loc)


def _mod_bwd(c_all, c_ctx, w_loc, dmx_loc, dmc_loc, name):
    nr, D = c_all.shape
    nc = w_loc.shape[1]

    def body(c_ref, cc_ref, w_ref, dmx_ref, dmc_ref, gw_ref, gc_ref):
        cc = cc_ref[...]
        a = jnp.concatenate([c_ref[...], jnp.broadcast_to(cc, (N_DEV, D))], axis=0)
        s = jax.nn.silu(a).astype(bf16)
        g = jnp.concatenate([dmx_ref[...], dmc_ref[...]], axis=0).astype(bf16)
        gw_ref[...] = lax.dot_general(s, g, (((0,), (0,)), ((), ())), preferred_element_type=f32)
        dmc = jnp.sum(dmc_ref[...], axis=0, keepdims=True)
        ds = lax.dot_general(jnp.broadcast_to(dmc, (8, nc)).astype(bf16), w_ref[...].astype(bf16), (((1,), (1,)), ((), ())),
                             preferred_element_type=f32)[0:1]
        sg = jax.nn.sigmoid(cc)
        gc_ref[...] = ds * (sg * (1.0 + cc * (1.0 - sg)))

    return _pc(body, name=name, out_shape=[jax.ShapeDtypeStruct((D, nc), f32), jax.ShapeDtypeStruct((1, D), f32)],
               compiler_params=_params())(c_all, c_ctx, w_loc, dmx_loc, dmc_loc)


def _adamw_math(w, g, m, v):
    m2 = ADAM_B1 * m + (1.0 - ADAM_B1) * g
    v2 = ADAM_B2 * v + (1.0 - ADAM_B2) * jnp.square(g)
    m_hat = m2 / (1.0 - ADAM_B1 ** ADAM_STEP)
    v_hat = v2 / (1.0 - ADAM_B2 ** ADAM_STEP)
    delta = -ADAM_LR * (m_hat / (jnp.sqrt(v_hat) + ADAM_EPS) + ADAM_WD * w)
    return delta, m2, v2


def _adamw(w, m, v, g, name, partials):
    r, cdim = w.shape
    tr = _pick(r, 256, 8)

    def body(w_ref, m_ref, v_ref, g_ref, og_ref, od_ref, om_ref, ov_ref):
        if partials:
            g = g_ref[0].astype(f32)
            for s in range(1, N_DEV):
                g = g + g_ref[s].astype(f32)
        else:
            g = g_ref[...]
        d, m2, v2 = _adamw_math(w_ref[...], g, m_ref[...], v_ref[...])
        og_ref[...] = g
        od_ref[...] = d
        om_ref[...] = m2
        ov_ref[...] = v2

    blk = pl.BlockSpec((tr, cdim), lambda i: (i, 0))
    g_spec = pl.BlockSpec((N_DEV, tr, cdim), lambda i: (0, i, 0)) if partials else blk
    return _pc(body, name=name, grid=(r // tr,), in_specs=[blk, blk, blk, g_spec], out_specs=[blk] * 4,
               out_shape=[jax.ShapeDtypeStruct((r, cdim), f32)] * 4, compiler_params=_params("parallel"))(w, m, v, g)


def _sum_sources(parts, name):
    def body(*refs):
        for i_ref, o_ref in zip(refs[:len(parts)], refs[len(parts):]):
            acc = i_ref[0]
            for s in range(1, i_ref.shape[0]):
                acc = acc + i_ref[s]
            o_ref[...] = acc

    return list(_pc(body, name=name, out_shape=[jax.ShapeDtypeStruct(p.shape[1:], f32) for p in parts],
                    compiler_params=_params())(*parts))


def kernel(x, c, ctx, c_ctx, w_mod, b_mod, g_ffn1, w1_gu, w1_down, g_mix, w_in, dw_weight, dw_bias, conv_ln_g, conv_ln_b, w_conv_out, w_alpha_f, b_alpha_f, w_alpha_b, b_alpha_b, gla_norm_g, w_gla_out, w_out, g_ffn2, w2_gu, w2_down, g_final, loss_target, m_c_ctx, m_w_mod, m_b_mod, m_g_ffn1, m_w1_gu, m_w1_down, m_g_mix, m_w_in, m_dw_weight, m_dw_bias, m_conv_ln_g, m_conv_ln_b, m_w_conv_out, m_w_alpha_f, m_b_alpha_f, m_w_alpha_b, m_b_alpha_b, m_gla_norm_g, m_w_gla_out, m_w_out, m_g_ffn2, m_w2_gu, m_w2_down, m_g_final, v_c_ctx, v_w_mod, v_b_mod, v_g_ffn1, v_w1_gu, v_w1_down, v_g_mix, v_w_in, v_dw_weight, v_dw_bias, v_conv_ln_g, v_conv_ln_b, v_w_conv_out, v_w_alpha_f, v_b_alpha_f, v_w_alpha_b, v_b_alpha_b, v_gla_norm_g, v_w_gla_out, v_w_out, v_g_ffn2, v_w2_gu, v_w2_down, v_g_final):
    B, L, D = x.shape
    Lc = ctx.shape[1]
    T, Tc = B * L, B * Lc
    Tall = T + Tc
    F = w1_down.shape[1] * N_DEV
    DK, DV = D // (2 * HEADS), D // HEADS
    QK = HEADS * DK
    PW = 7 * D + LR_PAD
    tm = ROW_TILE
    tpe = L // tm
    nx, nall = T // tm, Tall // tm
    me = 4 * lax.axis_index("x") + 2 * lax.axis_index("y") + lax.axis_index("c")

    rw_all = dict(tm=tm, n_tiles=nall, tpe=tpe, nx_tiles=nx, n_ex=B + 1)
    rw_x = dict(tm=tm, n_tiles=nx, tpe=tpe, nx_tiles=nx, n_ex=B)
    rw_all_h = dict(tm=tm // 2, n_tiles=2 * nall, tpe=2 * tpe, nx_tiles=2 * nx, n_ex=B + 1)
    rw_x_h = dict(tm=tm // 2, n_tiles=2 * nx, tpe=2 * tpe, nx_tiles=2 * nx, n_ex=B)

    g_list = [w1_gu[0].astype(bf16), w1_down[0].astype(bf16), w_in[0].astype(bf16), w_conv_out[0].astype(bf16),
              w_gla_out[0].astype(bf16), w_out[0].astype(bf16), w2_gu[0].astype(bf16), w2_down[0].astype(bf16),
              dw_weight[0], w_alpha_f[0], w_alpha_b[0], c]
    (w1gu_g, w1d_g, win_g, wco_g, wgo_g, wo_g, w2gu_g, w2d_g, dww_g, waf_g, wab_g, c_g) = _exchange(g_list, False, "gather_weights")

    def cols(gat):
        return jnp.transpose(gat, (1, 0, 2)).reshape(gat.shape[1], N_DEV * gat.shape[2])

    def rows_(gat):
        return gat.reshape(N_DEV * gat.shape[1], gat.shape[2])

    W1gu, W2gu = cols(w1gu_g), cols(w2gu_g)
    W1d, W2d = rows_(w1d_g), rows_(w2d_g)
    Wco, Wgo, Wo = rows_(wco_g), rows_(wgo_g), rows_(wo_g)
    win = cols(win_g)
    o_q, o_k, o_v, o_og, o_af, o_ga, o_gb = 2 * D, 2 * D + QK, 2 * D + 2 * QK, 3 * D + 2 * QK, 4 * D + 2 * QK, 4 * D + 2 * QK + 2 * LOWRANK, 5 * D + 2 * QK + 2 * LOWRANK
    Win = jnp.concatenate([win[:, :o_q], win[:, o_v:o_og], win[:, o_og:o_af], win[:, o_ga:o_gb], win[:, o_gb:],
                           win[:, o_q:o_k], win[:, o_k:o_v], win[:, o_af:o_ga], jnp.zeros((D, LR_PAD - 2 * LOWRANK), bf16)], axis=1)
    dww = cols(dww_g)
    WA = jnp.zeros((LR_PAD, 2 * QK), f32).at[:LOWRANK, :QK].set(cols(waf_g)).at[LOWRANK:2 * LOWRANK, QK:].set(cols(wab_g)).astype(bf16)
    BA = jnp.concatenate([b_alpha_f, b_alpha_b], axis=1)
    c_all = c_g.reshape(N_DEV * B, D)
    c_ctx2 = c_ctx.reshape(1, D)

    ncm = w_mod.shape[2]
    b_mod_loc = lax.dynamic_slice(b_mod, (0, me * ncm), (1, ncm))
    mod_loc = _mod_fwd(c_all, c_ctx2, w_mod[0], b_mod_loc, "mod_fwd")
    (mod_g,) = _exchange([mod_loc], False, "gather_mod")
    mod_full = cols(mod_g)
    mod_tab = jnp.concatenate([lax.dynamic_slice(mod_full, (me * B, 0), (B, N_MOD * D)), mod_full[N_DEV * B:N_DEV * B + 1]], axis=0)
    mods = [mod_tab[:, i * D:(i + 1) * D].reshape(B + 1, 1, D) for i in range(N_MOD)]
    mods_x = [mm[:B] for mm in mods]

    x_all = jnp.concatenate([x.reshape(T, D), ctx.reshape(Tc, D)], axis=0)

    def f_ffn_in(tok, ex, sh):
        return [_rms_mod(tok[0], sh[0], ex[0], ex[1])], [], []

    (u1,) = _rowwise(f_ffn_in, name="ffn1_in", tok_in=[(x_all, D, 0, False)], ex_in=[mods[0], mods[1]], sh_in=[g_ffn1],
                     tok_out=[(D, bf16)], **rw_all)
    gu1 = _matmul(u1, W1gu, "nn", bf16, "ffn1_up")

    def f_swiglu(tok, ex, sh):
        return [_swiglu(tok[0], F)], [], []

    (h1,) = _rowwise(f_swiglu, name="ffn1_act", tok_in=[(gu1, 2 * F, 0, False)], tok_out=[(F, bf16)], **rw_all_h)
    f1 = _matmul(h1, W1d, "nn", f32, "ffn1_down")

    def mix_in(xv, fv, gate, sh, sc, g):
        x1 = xv + 0.5 * gate * fv
        return x1, _rms_mod(x1, g, sh, sc)

    def f_mix_in(tok, ex, sh):
        return list(mix_in(tok[0], tok[1], ex[0], ex[1], ex[2], sh[0])), [], []

    x1, um = _rowwise(f_mix_in, name="mix_in", tok_in=[(x_all, D, 0, False), (f1, D, 0, False)], ex_in=[mods[2], mods[3], mods[4]],
                      sh_in=[g_mix], tok_out=[(D, f32), (D, bf16)], **rw_all)
    p_all = _matmul(um, Win, "nn", bf16, "in_proj", tn_cap=384)

    def log_decay(lr, wa, ba):
        z = jnp.dot(lr.astype(bf16), wa.astype(bf16), preferred_element_type=f32) + ba
        return _log_sigmoid(z) / TAU

    def f_decay(tok, ex, sh):
        return [log_decay(tok[0], sh[0], sh[1])], [], []

    lr_blk = (p_all, LR_PAD, 7 * D // LR_PAD, False)
    (la_all,) = _rowwise(f_decay, name="log_decay", tok_in=[lr_blk], sh_in=[WA, BA], tok_out=[(2 * QK, f32)], **rw_all)

    zeros_s = jnp.zeros((B, HEADS, DV, DK), f32)
    gla_c = dict(row0=T, nb=B, seq=Lc, D=D)
    gla_x = dict(row0=0, nb=B, seq=L, D=D)
    _, hist_cf, s_f = _gla_fwd(p_all, la_all, zeros_s, rev=False, name="gla_ctx_f", **gla_c)
    _, hist_cb, s_b = _gla_fwd(p_all, la_all, zeros_s, rev=True, name="gla_ctx_b", **gla_c)
    o_f, hist_f, _ = _gla_fwd(p_all, la_all, s_f, rev=False, name="gla_x_f", **gla_x)
    o_b, hist_b, _ = _gla_fwd(p_all, la_all, s_b, rev=True, name="gla_x_b", **gla_x)

    cz = _conv_fwd(p_all, dww, dw_bias, B=B, L=L, D=D, name="conv_fwd")

    def ln_silu(z, g, b):
        mu = jnp.mean(z, axis=-1, keepdims=True)
        var = jnp.mean(jnp.square(z - mu), axis=-1, keepdims=True)
        return jax.nn.silu((z - mu) * lax.rsqrt(var + EPS) * g + b)

    def f_ln(tok, ex, sh):
        return [ln_silu(tok[0], sh[0], sh[1])], [], []

    (zc,) = _rowwise(f_ln, name="conv_ln", tok_in=[(cz, D, 0, False)], sh_in=[conv_ln_g, conv_ln_b], tok_out=[(D, bf16)], **rw_x)
    yc = _matmul(zc, Wco, "nn", bf16, "conv_out")

    def gla_out(of, ob, og, gn):
        return _head_rms(of + ob, DV) * gn * jax.nn.silu(og.astype(f32))

    def f_gla_out(tok, ex, sh):
        return [gla_out(tok[0], tok[1], tok[2], sh[0])], [], []

    og_blk = (p_all, D, 3, False)
    (og2,) = _rowwise(f_gla_out, name="gla_norm", tok_in=[(o_f, D, 0, False), (o_b, D, 0, False), og_blk], sh_in=[gla_norm_g],
                      tok_out=[(D, bf16)], **rw_x)
    yg = _matmul(og2, Wgo, "nn", bf16, "gla_out")

    def merge(ga, gb, ycv, ygv):
        return jax.nn.sigmoid(ga.astype(f32)) * ycv.astype(f32) + jax.nn.sigmoid(gb.astype(f32)) * ygv.astype(f32)

    def f_merge(tok, ex, sh):
        return [merge(*tok)], [], []

    ga_blk, gb_blk = (p_all, D, 4, False), (p_all, D, 5, False)
    (mg,) = _rowwise(f_merge, name="merge", tok_in=[ga_blk, gb_blk, (yc, D, 0, False), (yg, D, 0, False)], tok_out=[(D, bf16)], **rw_x)
    mix = _matmul(mg, Wo, "nn", f32, "mix_out")

    def ffn2_in(x1v, mixv, g5, sh, sc, g):
        x2 = x1v + g5 * mixv
        return x2, _rms_mod(x2, g, sh, sc)

    def f_ffn2_in(tok, ex, sh):
        return list(ffn2_in(tok[0], tok[1], ex[0], ex[1], ex[2], sh[0])), [], []

    x2, u2 = _rowwise(f_ffn2_in, name="ffn2_in", tok_in=[(x1, D, 0, False), (mix, D, 0, False)], ex_in=[mods_x[5], mods_x[6], mods_x[7]],
                      sh_in=[g_ffn2], tok_out=[(D, f32), (D, bf16)], **rw_x)
    gu2 = _matmul(u2, W2gu, "nn", bf16, "ffn2_up")
    (h2,) = _rowwise(f_swiglu, name="ffn2_act", tok_in=[(gu2, 2 * F, 0, False)], tok_out=[(F, bf16)], **rw_x_h)
    f2 = _matmul(h2, W2d, "nn", f32, "ffn2_down")

    gf2 = g_final.reshape(1, D)

    def head_loss(x2v, f2v, g8, gf, tgt):
        x3 = x2v + 0.5 * g8 * f2v
        y = x3 * lax.rsqrt(jnp.mean(x3 * x3, axis=-1, keepdims=True) + EPS) * gf
        return 0.5 * jnp.sum(jnp.mean(jnp.square(y - tgt), axis=-1))

    def f_head(tok, ex, sh):
        loss, vjp = jax.vjp(lambda a, b_, c_, d_: head_loss(a, b_, c_, d_, tok[2]), tok[0], tok[1], ex[0], sh[0])
        dx3, df2, dg8, dgf = vjp(jnp.ones((), f32))
        return [dx3, df2], [dg8], [dgf, jnp.broadcast_to(loss.reshape(1, 1), (1, 128))]

    dx3, df2, dg8, dgf, loss_p = _rowwise(
        f_head, name="head", tok_in=[(x2, D, 0, False), (f2, D, 0, False), (loss_target.reshape(T, D), D, 0, False)], ex_in=[mods_x[8]],
        sh_in=[gf2], tok_out=[(D, f32), (D, bf16)], ex_out=[D], gl_out=[(1, D), (1, 128)], **rw_x)

    dh2 = _matmul(df2, W2d, "nt", bf16, "ffn2_down_dx")
    gW2d = _matmul(h2, df2, "tn", f32, "ffn2_down_dw")

    def f_swiglu_bwd(tok, ex, sh):
        _, vjp = jax.vjp(lambda gu: _swiglu(gu, F), tok[0].astype(f32))
        return [vjp(tok[1].astype(f32))[0]], [], []

    (dgu2,) = _rowwise(f_swiglu_bwd, name="ffn2_act_bwd", tok_in=[(gu2, 2 * F, 0, False), (dh2, F, 0, False)], tok_out=[(2 * F, bf16)], **rw_x_h)
    du2 = _matmul(dgu2, W2gu, "nt", f32, "ffn2_up_dx")
    gW2gu = _matmul(u2, dgu2, "tn", f32, "ffn2_up_dw")

    def f_ffn2_in_bwd(tok, ex, sh):
        _, vjp = jax.vjp(ffn2_in, tok[0], tok[1], ex[0], ex[1], ex[2], sh[0])
        dx2, dmix, dg5, dsh, dsc, dg = vjp((tok[3], tok[2]))
        return [dx2, dmix], [dg5, dsh, dsc], [dg]

    dx2, dmix, dg5, dsh6, dsc7, dg_ffn2 = _rowwise(
        f_ffn2_in_bwd, name="ffn2_in_bwd", tok_in=[(x1, D, 0, False), (mix, D, 0, False), (du2, D, 0, False), (dx3, D, 0, False)],
        ex_in=[mods_x[5], mods_x[6], mods_x[7]], sh_in=[g_ffn2], tok_out=[(D, f32), (D, bf16)], ex_out=[D, D, D], gl_out=[(1, D)], **rw_x)

    dmg = _matmul(dmix, Wo, "nt", bf16, "mix_out_dx")
    gWo = _matmul(mg, dmix, "tn", f32, "mix_out_dw")

    def f_merge_bwd(tok, ex, sh):
        _, vjp = jax.vjp(merge, *[t.astype(f32) for t in tok[:4]])
        dga, dgb, dyc, dyg = vjp(tok[4].astype(f32))
        return [dga, dgb, dyc, dyg], [], []

    dga, dgb, dyc, dyg = _rowwise(f_merge_bwd, name="merge_bwd",
                                  tok_in=[ga_blk, gb_blk, (yc, D, 0, False), (yg, D, 0, False), (dmg, D, 0, False)],
                                  tok_out=[(D, bf16)] * 4, **rw_x)
    dzc = _matmul(dyc, Wco, "nt", f32, "conv_out_dx")
    gWco = _matmul(zc, dyc, "tn", f32, "conv_out_dw")
    dog2 = _matmul(dyg, Wgo, "nt", f32, "gla_out_dx")
    gWgo = _matmul(og2, dyg, "tn", f32, "gla_out_dw")

    def f_ln_bwd(tok, ex, sh):
        _, vjp = jax.vjp(ln_silu, tok[0], sh[0], sh[1])
        dcz, dg, db = vjp(tok[1])
        return [dcz], [], [dg, db, jnp.sum(dcz, axis=0, keepdims=True)]

    dcz, g_ln_g, g_ln_b, g_dwb = _rowwise(f_ln_bwd, name="conv_ln_bwd", tok_in=[(cz, D, 0, False), (dzc, D, 0, False)],
                                          sh_in=[conv_ln_g, conv_ln_b], tok_out=[(D, f32)], gl_out=[(1, D)] * 3, **rw_x)
    dca, dcb, g_dww = _conv_bwd(p_all, dcz, dww, B=B, L=L, D=D, name="conv_bwd")

    def f_gla_out_bwd(tok, ex, sh):
        _, vjp = jax.vjp(gla_out, tok[0], tok[1], tok[2].astype(f32), sh[0])
        dof, _, dog, dgn = vjp(tok[3])
        return [dof, dog], [], [dgn]

    d_o, dog, g_gn = _rowwise(f_gla_out_bwd, name="gla_norm_bwd",
                              tok_in=[(o_f, D, 0, False), (o_b, D, 0, False), og_blk, (dog2, D, 0, False)], sh_in=[gla_norm_g],
                              tok_out=[(D, f32), (D, bf16)], gl_out=[(1, D)], **rw_x)

    dq_f, dk_f, dv_f, dla_f, ds_f = _gla_bwd(p_all, la_all, hist_f, d_o, zeros_s, rev=False, name="gla_x_f_bwd", **gla_x)
    dq_b, dk_b, dv_b, dla_b, ds_b = _gla_bwd(p_all, la_all, hist_b, d_o, zeros_s, rev=True, name="gla_x_b_bwd", **gla_x)
    _, dk_cf, dv_cf, dla_cf, _ = _gla_bwd(p_all, la_all, hist_cf, None, ds_f, rev=False, name="gla_ctx_f_bwd", **gla_c)
    _, dk_cb, dv_cb, dla_cb, _ = _gla_bwd(p_all, la_all, hist_cb, None, ds_b, rev=True, name="gla_ctx_b_bwd", **gla_c)

    def f_add3(tok, ex, sh):
        return [tok[0] + tok[1], tok[2] + tok[3], tok[4] + tok[5]], [], []

    dq, dk, dv = _rowwise(f_add3, name="gla_sum_x", tok_in=[(t, t.shape[1], 0, False) for t in (dq_f, dq_b, dk_f, dk_b, dv_f, dv_b)],
                          tok_out=[(QK, bf16), (QK, bf16), (D, bf16)], **rw_x)

    def f_add2(tok, ex, sh):
        return [tok[0] + tok[1], tok[2] + tok[3]], [], []

    dk_c, dv_c = _rowwise(f_add2, name="gla_sum_ctx", tok_in=[(t, t.shape[1], 0, False) for t in (dk_cf, dk_cb, dv_cf, dv_cb)],
                          tok_out=[(QK, bf16), (D, bf16)], tm=tm, n_tiles=Tc // tm, tpe=tpe, nx_tiles=Tc // tm, n_ex=1)

    dla_all = jnp.concatenate([jnp.concatenate([dla_f, dla_b], axis=1), jnp.concatenate([dla_cf, dla_cb], axis=1)], axis=0)

    def f_decay_bwd(tok, ex, sh):
        _, vjp = jax.vjp(log_decay, tok[0].astype(f32), sh[0].astype(f32), sh[1])
        dlr, dwa, dba = vjp(tok[1])
        return [dlr], [], [dwa, dba]

    dlr, g_WA, g_BA = _rowwise(f_decay_bwd, name="log_decay_bwd", tok_in=[lr_blk, (dla_all, 2 * QK, 0, False)], sh_in=[WA, BA],
                               tok_out=[(LR_PAD, bf16)], gl_out=[(LR_PAD, 2 * QK), (1, 2 * QK)], **rw_all)

    zc_ = functools.partial(jnp.zeros, dtype=bf16)
    dp_x = jnp.concatenate([dca, dcb, dv, dog, dga, dgb, dq, dk, dlr[:T]], axis=1)
    dp_c = jnp.concatenate([zc_((Tc, 2 * D)), dv_c, zc_((Tc, 3 * D)), zc_((Tc, QK)), dk_c, dlr[T:]], axis=1)
    dp_all = jnp.concatenate([dp_x, dp_c], axis=0)
    dum = _matmul(dp_all, Win, "nt", f32, "in_proj_dx", tk_cap=2432)
    gWin_p = _matmul(um, dp_all, "tn", f32, "in_proj_dw", tn_cap=384)

    def f_mix_in_bwd(tok, ex, sh):
        _, vjp = jax.vjp(mix_in, tok[0], tok[1], ex[0], ex[1], ex[2], sh[0])
        dx1, df1, dgate, dsh, dsc, dg = vjp((tok[3], tok[2]))
        return [dx1, df1], [dgate, dsh, dsc], [dg]

    dx1, df1, dg2, dsh3, dsc4, dg_mix = _rowwise(
        f_mix_in_bwd, name="mix_in_bwd", tok_in=[(x_all, D, 0, False), (f1, D, 0, False), (dum, D, 0, False), (dx2, D, 0, True)],
        ex_in=[mods[2], mods[3], mods[4]], sh_in=[g_mix], tok_out=[(D, f32), (D, bf16)], ex_out=[D, D, D], gl_out=[(1, D)], **rw_all)

    dh1 = _matmul(df1, W1d, "nt", bf16, "ffn1_down_dx")
    gW1d = _matmul(h1, df1, "tn", f32, "ffn1_down_dw")
    (dgu1,) = _rowwise(f_swiglu_bwd, name="ffn1_act_bwd", tok_in=[(gu1, 2 * F, 0, False), (dh1, F, 0, False)], tok_out=[(2 * F, bf16)], **rw_all_h)
    du1 = _matmul(dgu1, W1gu, "nt", f32, "ffn1_up_dx")
    gW1gu = _matmul(u1, dgu1, "tn", f32, "ffn1_up_dw")

    def f_ffn_in_bwd(tok, ex, sh):
        _, vjp = jax.vjp(_rms_mod, tok[0], sh[0], ex[0], ex[1])
        dx, dg, dsh, dsc = vjp(tok[1])
        return [dx + tok[2]], [dsh, dsc], [dg]

    dx_all, dsh0, dsc1, dg_ffn1 = _rowwise(
        f_ffn_in_bwd, name="ffn1_in_bwd", tok_in=[(x_all, D, 0, False), (du1, D, 0, False), (dx1, D, 0, False)],
        ex_in=[mods[0], mods[1]], sh_in=[g_ffn1], tok_out=[(D, f32)], ex_out=[D, D], gl_out=[(1, D)], **rw_all)
    grad_x = dx_all[:T].reshape(B, L, D)

    zrow = jnp.zeros((1, 1, D), f32)
    dmod_loc = jnp.concatenate([dsh0, dsc1, dg2, dsh3, dsc4] + [jnp.concatenate([t, zrow], axis=0) for t in (dg5, dsh6, dsc7, dg8)],
                               axis=2).reshape(B + 1, N_MOD * D)
    small = [loss_p, dg_ffn1, dg_mix, g_dww[:CONV_W].reshape(1, CONV_W * D), g_dwb, g_ln_g, g_ln_b,
             g_WA[:LOWRANK, :QK].reshape(1, LOWRANK * QK), g_BA[:, :QK], g_WA[LOWRANK:2 * LOWRANK, QK:].reshape(1, LOWRANK * QK), g_BA[:, QK:],
             g_gn, dg_ffn2, dgf]
    small_w = [s.shape[1] for s in small]
    def to8(v):
        n_pad = -(-v.shape[1] // 1024) * 1024
        return jnp.pad(v, ((0, 0), (0, n_pad - v.shape[1]))).reshape(8, n_pad // 8)

    def from8(a, n):
        return a.reshape(1, a.size)[:, :n]

    dmod_g, small_g = _exchange([dmod_loc, to8(jnp.concatenate(small, axis=1))], False, "gather_small")
    dmx = dmod_g[:, :B].reshape(N_DEV * B, N_MOD * D)
    dmc = dmod_g[:, B]
    gWmod, gcc_p = _mod_bwd(c_all, c_ctx2, w_mod[0], lax.dynamic_slice(dmx, (0, me * ncm), (N_DEV * B, ncm)),
                            lax.dynamic_slice(dmc, (0, me * ncm), (N_DEV, ncm)), "mod_bwd")

    def col_shards(g):
        return jnp.transpose(g.reshape(g.shape[0], N_DEV, g.shape[1] // N_DEV), (1, 0, 2)).astype(bf16)

    def row_shards(g):
        return g.reshape(N_DEV, g.shape[0] // N_DEV, g.shape[1]).astype(bf16)

    gWin = jnp.concatenate([gWin_p[:, :2 * D], gWin_p[:, 6 * D:7 * D], gWin_p[:, 2 * D:4 * D], gWin_p[:, 7 * D:7 * D + 2 * LOWRANK],
                            gWin_p[:, 4 * D:6 * D]], axis=1)
    rs_in = [col_shards(gW1gu), row_shards(gW1d), col_shards(gWin), row_shards(gWco), row_shards(gWgo), row_shards(gWo),
             col_shards(gW2gu), row_shards(gW2d)]
    rs_out = _exchange(rs_in, True, "scatter_grads")
    (gcc_g,) = _exchange([to8(gcc_p)], False, "gather_cctx")

    sums, g_cc, g_bmod = _sum_sources([small_g, gcc_g, jnp.concatenate([dmx, dmc], axis=0).reshape(N_DEV * (B + 1), 8, N_MOD * D // 8)], "sum_small")
    sums, g_cc, g_bmod = from8(sums, sum(small_w)), from8(g_cc, D), from8(g_bmod, N_MOD * D)
    offs = [0]
    for wd in small_w:
        offs.append(offs[-1] + wd)
    sm = [sums[:, offs[i]:offs[i + 1]] for i in range(len(small))]
    loss = sm[0][0, 0]
    ncd, nca = dw_weight.shape[2], w_alpha_f.shape[2]
    g_dww_loc = lax.dynamic_slice(sm[3].reshape(CONV_W, D), (0, me * ncd), (CONV_W, ncd)).reshape(1, CONV_W * ncd)
    g_waf_loc = lax.dynamic_slice(sm[7].reshape(LOWRANK, QK), (0, me * nca), (LOWRANK, nca)).reshape(1, LOWRANK * nca)
    g_wab_loc = lax.dynamic_slice(sm[9].reshape(LOWRANK, QK), (0, me * nca), (LOWRANK, nca)).reshape(1, LOWRANK * nca)

    big = {}
    for nm, wv, mv, vv, part in (("w1_gu", w1_gu, m_w1_gu, v_w1_gu, rs_out[0]), ("w1_down", w1_down, m_w1_down, v_w1_down, rs_out[1]),
                                 ("w_in", w_in, m_w_in, v_w_in, rs_out[2]), ("w_conv_out", w_conv_out, m_w_conv_out, v_w_conv_out, rs_out[3]),
                                 ("w_gla_out", w_gla_out, m_w_gla_out, v_w_gla_out, rs_out[4]), ("w_out", w_out, m_w_out, v_w_out, rs_out[5]),
                                 ("w2_gu", w2_gu, m_w2_gu, v_w2_gu, rs_out[6]), ("w2_down", w2_down, m_w2_down, v_w2_down, rs_out[7])):
        big[nm] = [t[None] for t in _adamw(wv[0], mv[0], vv[0], part, "adamw_" + nm, True)]
    big["w_mod"] = [t[None] for t in _adamw(w_mod[0], m_w_mod[0], v_w_mod[0], gWmod, "adamw_w_mod", False)]

    small_params = [("c_ctx", c_ctx, m_c_ctx, v_c_ctx, g_cc), ("b_mod", b_mod, m_b_mod, v_b_mod, g_bmod), ("g_ffn1", g_ffn1, m_g_ffn1, v_g_ffn1, sm[1]),
                    ("g_mix", g_mix, m_g_mix, v_g_mix, sm[2]), ("dw_weight", dw_weight, m_dw_weight, v_dw_weight, g_dww_loc),
                    ("dw_bias", dw_bias, m_dw_bias, v_dw_bias, sm[4]), ("conv_ln_g", conv_ln_g, m_conv_ln_g, v_conv_ln_g, sm[5]),
                    ("conv_ln_b", conv_ln_b, m_conv_ln_b, v_conv_ln_b, sm[6]), ("w_alpha_f", w_alpha_f, m_w_alpha_f, v_w_alpha_f, g_waf_loc),
                    ("b_alpha_f", b_alpha_f, m_b_alpha_f, v_b_alpha_f, sm[8]), ("w_alpha_b", w_alpha_b, m_w_alpha_b, v_w_alpha_b, g_wab_loc),
                    ("b_alpha_b", b_alpha_b, m_b_alpha_b, v_b_alpha_b, sm[10]), ("gla_norm_g", gla_norm_g, m_gla_norm_g, v_gla_norm_g, sm[11]),
                    ("g_ffn2", g_ffn2, m_g_ffn2, v_g_ffn2, sm[12]), ("g_final", g_final, m_g_final, v_g_final, sm[13])]
    flat = lambda t: t.reshape(1, t.size)
    pw, pm, pv, pg = (jnp.concatenate([flat(sp[i]) for sp in small_params], axis=1) for i in (1, 2, 3, 4))
    n_small = pw.shape[1]
    s_g, s_d, s_m, s_v = (from8(t, n_small) for t in _adamw(to8(pw), to8(pm), to8(pv), to8(pg), "adamw_small", False))
    small_out, o0 = {}, 0
    for nm, wv, _, _, _ in small_params:
        small_out[nm] = [t[:, o0:o0 + wv.size].reshape(wv.shape) for t in (s_g, s_d, s_m, s_v)]
        o0 += wv.size

    order = ["c_ctx", "w_mod", "b_mod", "g_ffn1", "w1_gu", "w1_down", "g_mix", "w_in", "dw_weight", "dw_bias", "conv_ln_g", "conv_ln_b",
             "w_conv_out", "w_alpha_f", "b_alpha_f", "w_alpha_b", "b_alpha_b", "gla_norm_g", "w_gla_out", "w_out", "g_ffn2", "w2_gu",
             "w2_down", "g_final"]
    res = {**big, **small_out}
    return (loss, grad_x, *[res[n][0] for n in order], *[res[n][1] for n in order], *[res[n][2] for n in order], *[res[n][3] for n in order])
```

```python
import functools

import jax
import jax.numpy as jnp
from jax import lax
from jax.experimental import pallas as pl
from jax.experimental.pallas import tpu as pltpu

f32, bf16 = jnp.float32, jnp.bfloat16

N_DEV = 8
HEADS = 4
LOWRANK = 16
CONV_W = 31
CONV_PAD = 16
CHUNK = 64
SUB = 16
GLA_ROWS = 256
TAU = 16.0
EPS = 1e-6
N_MOD = 9
LR_PAD = 128
ROW_TILE = 256
V7X_VMEM_BYTES = 64 << 20
VMEM_LIMIT = (V7X_VMEM_BYTES * 3) // 4

ADAM_LR, ADAM_B1, ADAM_B2, ADAM_EPS, ADAM_WD, ADAM_STEP = 0.001, 0.9, 0.999, 1e-08, 0.01, 10

MESH = pl.DeviceIdType.MESH


def _pc(body, **kw):
    return pl.pallas_call(body, **kw)


def _params(*sem):
    return pltpu.CompilerParams(dimension_semantics=sem, vmem_limit_bytes=VMEM_LIMIT)


def _pick(n, cap, unit=128):
    best = None
    for t in range(unit, min(n, cap) + 1, unit):
        if n % t == 0:
            best = t
    return best or n


def _matmul(a, b, mode, out_dtype, name, tm_cap=1024, tn_cap=1536, tk_cap=None):
    if mode == "tn":
        (K, M), N = a.shape, b.shape[1]
    elif mode == "nt":
        (M, K), N = a.shape, b.shape[0]
    else:
        (M, K), N = a.shape, b.shape[1]
    tk_cap = tk_cap or (1024 if mode == "tn" else 2816)
    tm, tn, tk = _pick(M, tm_cap), _pick(N, tn_cap), _pick(K, tk_cap)
    nk = K // tk
    a_spec = pl.BlockSpec((tk, tm), lambda i, j, k: (k, i)) if mode == "tn" else pl.BlockSpec((tm, tk), lambda i, j, k: (i, k))
    b_spec = pl.BlockSpec((tn, tk), lambda i, j, k: (j, k)) if mode == "nt" else pl.BlockSpec((tk, tn), lambda i, j, k: (k, j))
    dims = {"nn": ((1,), (0,)), "nt": ((1,), (1,)), "tn": ((0,), (0,))}[mode]

    def body_single(a_ref, b_ref, o_ref):
        o_ref[...] = lax.dot_general(a_ref[...].astype(bf16), b_ref[...].astype(bf16), (dims, ((), ())),
                                     preferred_element_type=f32).astype(out_dtype)

    def body(a_ref, b_ref, o_ref, acc_ref):
        k = pl.program_id(2)
        part = lax.dot_general(a_ref[...].astype(bf16), b_ref[...].astype(bf16), (dims, ((), ())), preferred_element_type=f32)

        @pl.when(k == 0)
        def _():
            acc_ref[...] = part

        @pl.when(k > 0)
        def _():
            acc_ref[...] += part

        @pl.when(k == nk - 1)
        def _():
            o_ref[...] = acc_ref[...].astype(out_dtype)

    return _pc(
        body_single if nk == 1 else body, name=name, grid=(M // tm, N // tn, nk), in_specs=[a_spec, b_spec],
        out_specs=pl.BlockSpec((tm, tn), lambda i, j, k: (i, j)), out_shape=jax.ShapeDtypeStruct((M, N), out_dtype),
        scratch_shapes=[] if nk == 1 else [pltpu.VMEM((tm, tn), f32)],
        compiler_params=_params("parallel", "parallel", "arbitrary"),
    )(a, b)


def _rowwise(fn, *, name, tm, n_tiles, tpe, nx_tiles, n_ex, tok_in=(), ex_in=(), sh_in=(), tok_out=(), ex_out=(), gl_out=()):
    def seg(i):
        return jnp.minimum(i // tpe, n_ex - 1)

    in_specs, args = [], []
    for arr, w, cb, x_only in tok_in:
        if x_only:
            in_specs.append(pl.BlockSpec((tm, w), functools.partial(lambda i, cb: (jnp.minimum(i, nx_tiles - 1), cb), cb=cb)))
        else:
            in_specs.append(pl.BlockSpec((tm, w), functools.partial(lambda i, cb: (i, cb), cb=cb)))
        args.append(arr)
    for arr in ex_in:
        in_specs.append(pl.BlockSpec((1, 1, arr.shape[-1]), lambda i: (seg(i), 0, 0)))
        args.append(arr)
    for arr in sh_in:
        in_specs.append(pl.BlockSpec(arr.shape, functools.partial(lambda i, nd: (0,) * nd, nd=arr.ndim)))
        args.append(arr)
    out_specs, out_shape = [], []
    for w, dt in tok_out:
        out_specs.append(pl.BlockSpec((tm, w), lambda i: (i, 0)))
        out_shape.append(jax.ShapeDtypeStruct((n_tiles * tm, w), dt))
    for w in ex_out:
        out_specs.append(pl.BlockSpec((1, 1, w), lambda i: (seg(i), 0, 0)))
        out_shape.append(jax.ShapeDtypeStruct((n_ex, 1, w), f32))
    for r, w in gl_out:
        out_specs.append(pl.BlockSpec((r, w), lambda i: (0, 0)))
        out_shape.append(jax.ShapeDtypeStruct((r, w), f32))
    n_tok, n_exi, n_sh = len(tok_in), len(ex_in), len(sh_in)
    n_to, n_eo = len(tok_out), len(ex_out)
    x_only_flags = [t[3] for t in tok_in]

    def body(*refs):
        i = pl.program_id(0)
        ins, outs = refs[: n_tok + n_exi + n_sh], refs[n_tok + n_exi + n_sh:]
        is_x = i < nx_tiles
        tok_vals = []
        for r, xo in zip(ins[:n_tok], x_only_flags):
            v = r[...]
            tok_vals.append(jnp.where(is_x, v, jnp.zeros_like(v)) if xo else v)
        ex_vals = [r[0] for r in ins[n_tok:n_tok + n_exi]]
        sh_vals = [r[...] for r in ins[n_tok + n_exi:]]
        t_o, e_o, g_o = fn(tok_vals, ex_vals, sh_vals)
        for r, v in zip(outs[:n_to], t_o):
            r[...] = v.astype(r.dtype)
        first = jnp.logical_and(i % tpe == 0, i <= nx_tiles)
        for r, v in zip(outs[n_to:n_to + n_eo], e_o):
            @pl.when(first)
            def _(r=r, v=v):
                r[0] = v

            @pl.when(jnp.logical_not(first))
            def _(r=r, v=v):
                r[0] += v
        for r, v in zip(outs[n_to + n_eo:], g_o):
            @pl.when(i == 0)
            def _(r=r, v=v):
                r[...] = v

            @pl.when(i > 0)
            def _(r=r, v=v):
                r[...] += v

    res = _pc(body, name=name, grid=(n_tiles,), in_specs=in_specs, out_specs=out_specs, out_shape=out_shape,
              compiler_params=_params("arbitrary"))(*args)
    return list(res)


def _rms_mod(x, g, sh, sc):
    y = x * lax.rsqrt(jnp.mean(x * x, axis=-1, keepdims=True) + EPS) * g
    return y * (1.0 + sc) + sh


def _log_sigmoid(z):
    return jnp.minimum(z, 0.0) - jnp.log(1.0 + jnp.exp(-jnp.abs(z)))


def _swiglu(gu, F):
    a, b = gu[:, :F].astype(f32), gu[:, F:].astype(f32)
    return jax.nn.silu(a) * b


def _head_rms(o, DV):
    parts = []
    for h in range(HEADS):
        oh = o[:, h * DV:(h + 1) * DV]
        parts.append(oh * lax.rsqrt(jnp.mean(oh * oh, axis=-1, keepdims=True) + EPS))
    return jnp.concatenate(parts, axis=1)


@functools.partial(jax.custom_vjp, nondiff_argnums=(2,))
def _bdot(a, b, dims):
    return lax.dot_general(a.astype(bf16), b.astype(bf16), (((dims[0],), (dims[1],)), ((), ())), preferred_element_type=f32)


def _bdot_fwd(a, b, dims):
    return _bdot(a, b, dims), (a, b)


def _bdot_bwd(dims, res, g):
    a, b = res
    ca, cb = dims
    da = _bdot(g, b, (1, 1 - cb)) if ca == 1 else _bdot(b, g, (1 - cb, 1))
    db = _bdot(a, g, (1 - ca, 0)) if cb == 0 else _bdot(g, a, (0, 1 - ca))
    return da, db


_bdot.defvjp(_bdot_fwd, _bdot_bwd)


def _gla_chunk(St, q, k, v, g, *, rev, scale):
    C, DK = q.shape
    r = lax.broadcasted_iota(jnp.int32, (C, C), 0)
    c = lax.broadcasted_iota(jnp.int32, (C, C), 1)
    tri = ((r <= c) if rev else (r >= c)).astype(f32)
    b = jnp.dot(tri, g, precision=lax.Precision.HIGHEST, preferred_element_type=f32)
    qs = q * scale
    inter = _bdot(qs * jnp.exp(b), St, (1, 1))
    rr = lax.broadcasted_iota(jnp.int32, (SUB, SUB, DK), 0)
    cc = lax.broadcasted_iota(jnp.int32, (SUB, SUB, DK), 1)
    m3 = (rr <= cc) if rev else (rr >= cc)
    outs = []
    for i in range(C // SUB):
        lo, hi = i * SUB, (i + 1) * SUB
        bi, qi, ki, vi = b[lo:hi], qs[lo:hi], k[lo:hi], v[lo:hi]
        rel = bi[:, None, :] - bi[None, :, :]
        e = jnp.where(m3, jnp.exp(jnp.where(m3, rel, 0.0)), 0.0)
        att = jnp.sum(qi[:, None, :] * e * ki[None, :, :], axis=-1)
        acc = _bdot(att, vi, (1, 0))
        ref_row = b[hi - 1:hi] if rev else b[lo:lo + 1]
        prev = slice(hi, C) if rev else slice(0, lo)
        if (hi < C) if rev else (lo > 0):
            qn = qi * jnp.exp(bi - ref_row)
            ks = k[prev] * jnp.exp(ref_row - b[prev])
            acc = acc + _bdot(_bdot(qn, ks, (1, 1)), v[prev], (1, 0))
        outs.append(acc)
    o = inter + jnp.concatenate(outs, axis=0)
    b_last = b[0:1] if rev else b[C - 1:C]
    kd = k * jnp.exp(b_last - b)
    St_new = St * jnp.exp(b_last) + _bdot(v, kd, (0, 0))
    return St_new, o


def _gla_specs(D, rev_blocks, row0, seq):
    DK, DV = D // (2 * HEADS), D // HEADS
    nblk = seq // GLA_ROWS
    rb0 = row0 // GLA_ROWS

    def blk(j):
        return (nblk - 1 - j) if rev_blocks else j

    return DK, DV, nblk, rb0, blk


def _gla_in_specs(D, rev, rows):
    QK = D // 2
    return [
        pl.BlockSpec((GLA_ROWS, QK), lambda b, j: (rows(b, j), 6 * D // QK)),
        pl.BlockSpec((GLA_ROWS, QK), lambda b, j: (rows(b, j), 6 * D // QK + 1)),
        pl.BlockSpec((GLA_ROWS, D), lambda b, j: (rows(b, j), 2)),
        pl.BlockSpec((GLA_ROWS, QK), lambda b, j: (rows(b, j), 1 if rev else 0)),
    ]


def _gla_fwd(p_all, la_all, s0, *, rev, row0, nb, seq, D, name):
    DK, DV, nblk, rb0, blk = _gla_specs(D, rev, row0, seq)
    cpb = GLA_ROWS // CHUNK

    def rows(b, j):
        return rb0 + b * nblk + blk(j)

    in_specs = _gla_in_specs(D, rev, rows) + [pl.BlockSpec((1, HEADS, DV, DK), lambda b, j: (b, 0, 0, 0))]
    out_specs = [
        pl.BlockSpec((GLA_ROWS, D), lambda b, j: (b * nblk + blk(j), 0)),
        pl.BlockSpec((1, HEADS, cpb, DV, DK), lambda b, j: (b, 0, blk(j), 0, 0)),
        pl.BlockSpec((1, HEADS, DV, DK), lambda b, j: (b, 0, 0, 0)),
    ]
    out_shape = [
        jax.ShapeDtypeStruct((nb * seq, D), f32),
        jax.ShapeDtypeStruct((nb, HEADS, seq // CHUNK, DV, DK), f32),
        jax.ShapeDtypeStruct((nb, HEADS, DV, DK), f32),
    ]
    chunk = functools.partial(_gla_chunk, rev=rev, scale=DK ** -0.5)

    def body(q_ref, k_ref, v_ref, la_ref, s0_ref, o_ref, hist_ref, sfin_ref, st_ref):
        j = pl.program_id(1)

        @pl.when(j == 0)
        def _():
            st_ref[...] = s0_ref[0]

        def step(ci, carry):
            cc = (cpb - 1 - ci) if rev else ci
            sl = pl.ds(cc * CHUNK, CHUNK)
            for h in range(HEADS):
                kq, kv = pl.ds(h * DK, DK), pl.ds(h * DV, DV)
                St = st_ref[h]
                hist_ref[0, h, cc] = St
                St2, o = chunk(St, q_ref[sl, kq].astype(f32), k_ref[sl, kq].astype(f32), v_ref[sl, kv].astype(f32), la_ref[sl, kq])
                o_ref[sl, kv] = o
                st_ref[h] = St2
            return carry

        for ci in range(cpb):
            step(ci, 0)

        @pl.when(j == nblk - 1)
        def _():
            sfin_ref[0] = st_ref[...]

    return _pc(body, name=name, grid=(nb, nblk), in_specs=in_specs, out_specs=out_specs, out_shape=out_shape,
               scratch_shapes=[pltpu.VMEM((HEADS, DV, DK), f32)], compiler_params=_params("parallel", "arbitrary"))(
        p_all, p_all, p_all, la_all, s0)


def _gla_bwd(p_all, la_all, hist, do, dsfin, *, rev, row0, nb, seq, D, name):
    DK, DV, nblk, rb0, blk = _gla_specs(D, not rev, row0, seq)
    cpb = GLA_ROWS // CHUNK
    QK = HEADS * DK
    has_do = do is not None

    def rows(b, j):
        return rb0 + b * nblk + blk(j)

    in_specs = _gla_in_specs(D, rev, rows) + [
        pl.BlockSpec((1, HEADS, cpb, DV, DK), lambda b, j: (b, 0, blk(j), 0, 0)),
        pl.BlockSpec((1, HEADS, DV, DK), lambda b, j: (b, 0, 0, 0)),
    ]
    args = [p_all, p_all, p_all, la_all, hist, dsfin]
    if has_do:
        in_specs.append(pl.BlockSpec((GLA_ROWS, D), lambda b, j: (b * nblk + blk(j), 0)))
        args.append(do)
    out_specs = [
        pl.BlockSpec((GLA_ROWS, QK), lambda b, j: (b * nblk + blk(j), 0)),
        pl.BlockSpec((GLA_ROWS, QK), lambda b, j: (b * nblk + blk(j), 0)),
        pl.BlockSpec((GLA_ROWS, D), lambda b, j: (b * nblk + blk(j), 0)),
        pl.BlockSpec((GLA_ROWS, QK), lambda b, j: (b * nblk + blk(j), 0)),
        pl.BlockSpec((1, HEADS, DV, DK), lambda b, j: (b, 0, 0, 0)),
    ]
    out_shape = [
        jax.ShapeDtypeStruct((nb * seq, QK), f32), jax.ShapeDtypeStruct((nb * seq, QK), f32),
        jax.ShapeDtypeStruct((nb * seq, D), f32), jax.ShapeDtypeStruct((nb * seq, QK), f32),
        jax.ShapeDtypeStruct((nb, HEADS, DV, DK), f32),
    ]
    chunk = functools.partial(_gla_chunk, rev=rev, scale=DK ** -0.5)

    def body(*refs):
        if has_do:
            q_ref, k_ref, v_ref, la_ref, hist_ref, dsfin_ref, do_ref, dq_ref, dk_ref, dv_ref, dla_ref, ds0_ref, ds_ref = refs
        else:
            q_ref, k_ref, v_ref, la_ref, hist_ref, dsfin_ref, dq_ref, dk_ref, dv_ref, dla_ref, ds0_ref, ds_ref = refs
        j = pl.program_id(1)

        @pl.when(j == 0)
        def _():
            ds_ref[...] = dsfin_ref[0]

        def step(ci, carry):
            cc = ci if rev else (cpb - 1 - ci)
            sl = pl.ds(cc * CHUNK, CHUNK)
            for h in range(HEADS):
                kq, kv = pl.ds(h * DK, DK), pl.ds(h * DV, DV)
                prim = (hist_ref[0, h, cc], q_ref[sl, kq].astype(f32), k_ref[sl, kq].astype(f32), v_ref[sl, kv].astype(f32), la_ref[sl, kq])
                _, vjp = jax.vjp(chunk, *prim)
                d_o = do_ref[sl, kv] if has_do else jnp.zeros((CHUNK, DV), f32)
                dSt, dq, dk, dv, dg = vjp((ds_ref[h], d_o))
                dq_ref[sl, kq] = dq
                dk_ref[sl, kq] = dk
                dv_ref[sl, kv] = dv
                dla_ref[sl, kq] = dg
                ds_ref[h] = dSt
            return carry

        for ci in range(cpb):
            step(ci, 0)

        @pl.when(j == nblk - 1)
        def _():
            ds0_ref[0] = ds_ref[...]

    return _pc(body, name=name, grid=(nb, nblk), in_specs=in_specs, out_specs=out_specs, out_shape=out_shape,
               scratch_shapes=[pltpu.VMEM((HEADS, DV, DK), f32)], compiler_params=_params("parallel", "arbitrary"))(*args)


def _conv_fwd(p_all, dw_w, dw_b, *, B, L, D, name):
    ct = _pick(D, 256)
    nj = D // ct
    st = _pick(L, 256, 8)

    def body(a_ref, b_ref, w_ref, bias_ref, o_ref, zp_ref):
        zp_ref[pl.ds(0, CONV_PAD), :] = jnp.zeros((CONV_PAD, ct), f32)
        zp_ref[pl.ds(CONV_PAD + L, CONV_PAD), :] = jnp.zeros((CONV_PAD, ct), f32)
        zp_ref[pl.ds(CONV_PAD, L), :] = a_ref[...].astype(f32) * jax.nn.sigmoid(b_ref[...].astype(f32))
        off = CONV_PAD - CONV_W // 2
        for t0 in range(0, L, st):
            acc = jnp.zeros((st, ct), f32) + bias_ref[...]
            for k in range(CONV_W):
                acc = acc + w_ref[pl.ds(k, 1), :] * zp_ref[pl.ds(t0 + k + off, st), :]
            o_ref[pl.ds(t0, st), :] = acc

    return _pc(
        body, name=name, grid=(B, nj),
        in_specs=[pl.BlockSpec((L, ct), lambda b, j: (b, j)), pl.BlockSpec((L, ct), lambda b, j: (b, nj + j)),
                  pl.BlockSpec((CONV_W, ct), lambda b, j: (0, j)), pl.BlockSpec((1, ct), lambda b, j: (0, j))],
        out_specs=pl.BlockSpec((L, ct), lambda b, j: (b, j)), out_shape=jax.ShapeDtypeStruct((B * L, D), f32),
        scratch_shapes=[pltpu.VMEM((L + 2 * CONV_PAD, ct), f32)], compiler_params=_params("parallel", "parallel"),
    )(p_all, p_all, dw_w, dw_b)


def _conv_bwd(p_all, dcz, dw_w, *, B, L, D, name):
    ct = _pick(D, 256)
    nj = D // ct
    st = _pick(L, 256, 8)
    half = CONV_W // 2

    def body(a_ref, b_ref, dcz_ref, w_ref, da_ref, db_ref, ddw_ref, zp_ref, dp_ref):
        bi = pl.program_id(1)
        a = a_ref[...].astype(f32)
        sg = jax.nn.sigmoid(b_ref[...].astype(f32))
        for ref in (zp_ref, dp_ref):
            ref[pl.ds(0, CONV_PAD), :] = jnp.zeros((CONV_PAD, ct), f32)
            ref[pl.ds(CONV_PAD + L, CONV_PAD), :] = jnp.zeros((CONV_PAD, ct), f32)
        zp_ref[pl.ds(CONV_PAD, L), :] = a * sg
        dp_ref[pl.ds(CONV_PAD, L), :] = dcz_ref[...]

        @pl.when(bi == 0)
        def _():
            ddw_ref[...] = jnp.zeros_like(ddw_ref)

        for t0 in range(0, L, st):
            acc = jnp.zeros((st, ct), f32)
            dout = dcz_ref[pl.ds(t0, st), :]
            for k in range(CONV_W):
                acc = acc + w_ref[pl.ds(k, 1), :] * dp_ref[pl.ds(t0 + CONV_PAD + half - k, st), :]
                ddw_ref[pl.ds(k, 1), :] += jnp.sum(dout * zp_ref[pl.ds(t0 + k + CONV_PAD - half, st), :], axis=0, keepdims=True)
            a_t = a_ref[pl.ds(t0, st), :].astype(f32)
            sg_t = jax.nn.sigmoid(b_ref[pl.ds(t0, st), :].astype(f32))
            da_ref[pl.ds(t0, st), :] = (acc * sg_t).astype(bf16)
            db_ref[pl.ds(t0, st), :] = (acc * a_t * sg_t * (1.0 - sg_t)).astype(bf16)

    return _pc(
        body, name=name, grid=(nj, B),
        in_specs=[pl.BlockSpec((L, ct), lambda j, b: (b, j)), pl.BlockSpec((L, ct), lambda j, b: (b, nj + j)),
                  pl.BlockSpec((L, ct), lambda j, b: (b, j)), pl.BlockSpec((CONV_W, ct), lambda j, b: (0, j))],
        out_specs=[pl.BlockSpec((L, ct), lambda j, b: (b, j)), pl.BlockSpec((L, ct), lambda j, b: (b, j)),
                   pl.BlockSpec((2 * CONV_PAD, ct), lambda j, b: (0, j))],
        out_shape=[jax.ShapeDtypeStruct((B * L, D), bf16), jax.ShapeDtypeStruct((B * L, D), bf16),
                   jax.ShapeDtypeStruct((2 * CONV_PAD, D), f32)],
        scratch_shapes=[pltpu.VMEM((L + 2 * CONV_PAD, ct), f32), pltpu.VMEM((L + 2 * CONV_PAD, ct), f32)],
        compiler_params=_params("parallel", "arbitrary"),
    )(p_all, p_all, dcz, dw_w)


def _exchange(arrs, scatter, name):
    n = len(arrs)
    out_shape = [jax.ShapeDtypeStruct(((N_DEV,) + a.shape[1:]) if scatter else ((N_DEV,) + a.shape), a.dtype) for a in arrs]

    def body(*refs):
        ins, outs = refs[:n], refs[n:2 * n]
        send_sems, recv_sems, local_sems = refs[2 * n:]
        x, y, c = lax.axis_index("x"), lax.axis_index("y"), lax.axis_index("c")
        me = 4 * x + 2 * y + c
        local = []
        for a in range(n):
            cp = pltpu.make_async_copy(ins[a].at[me] if scatter else ins[a], outs[a].at[me], local_sems.at[a])
            cp.start()
            local.append(cp)
        sends = []
        for k in range(1, N_DEV):
            p = (me + k) % N_DEV
            for a in range(n):
                cp = pltpu.make_async_remote_copy(
                    src_ref=ins[a].at[p] if scatter else ins[a], dst_ref=outs[a].at[me],
                    send_sem=send_sems.at[a, k - 1], recv_sem=recv_sems.at[a, k - 1],
                    device_id=(p // 4, (p // 2) % 2, p % 2), device_id_type=MESH)
                cp.start()
                sends.append(cp)
        for k in range(1, N_DEV):
            s = (me + N_DEV - k) % N_DEV
            for a in range(n):
                pltpu.make_async_remote_copy(
                    src_ref=ins[a].at[s] if scatter else ins[a], dst_ref=outs[a].at[s],
                    send_sem=send_sems.at[a, k - 1], recv_sem=recv_sems.at[a, k - 1],
                    device_id=(s // 4, (s // 2) % 2, s % 2), device_id_type=MESH).wait_recv()
        for cp in sends:
            cp.wait_send()
        for cp in local:
            cp.wait()

    any_spec = pl.BlockSpec(memory_space=pl.ANY)
    res = _pc(body, name=name, in_specs=[any_spec] * n, out_specs=[any_spec] * n, out_shape=out_shape,
              scratch_shapes=[pltpu.SemaphoreType.DMA((n, N_DEV - 1)), pltpu.SemaphoreType.DMA((n, N_DEV - 1)),
                              pltpu.SemaphoreType.DMA((n,))])(*arrs)
    return list(res)


def _mod_fwd(c_all, c_ctx, w_loc, b_loc, name):
    nr, D = c_all.shape
    nc = w_loc.shape[1]

    def body(c_ref, cc_ref, w_ref, b_ref, o_ref):
        a = jnp.concatenate([c_ref[...], jnp.broadcast_to(cc_ref[...], (8, D))], axis=0)
        s = jax.nn.silu(a).astype(bf16)
        o_ref[...] = jnp.dot(s, w_ref[...].astype(bf16), preferred_element_type=f32) + b_ref[...]

    return _pc(body, name=name, out_shape=jax.ShapeDtypeStruct((nr + 8, nc), f32), compiler_params=_params())(c_all, c_ctx, w_loc, b_loc)


def _mod_bwd(c_all, c_ctx, w_loc, dmx_loc, dmc_loc, name):
    nr, D = c_all.shape
    nc = w_loc.shape[1]

    def body(c_ref, cc_ref, w_ref, dmx_ref, dmc_ref, gw_ref, gc_ref):
        cc = cc_ref[...]
        a = jnp.concatenate([c_ref[...], jnp.broadcast_to(cc, (N_DEV, D))], axis=0)
        s = jax.nn.silu(a).astype(bf16)
        g = jnp.concatenate([dmx_ref[...], dmc_ref[...]], axis=0).astype(bf16)
        gw_ref[...] = lax.dot_general(s, g, (((0,), (0,)), ((), ())), preferred_element_type=f32)
        dmc = jnp.sum(dmc_ref[...], axis=0, keepdims=True)
        ds = lax.dot_general(jnp.broadcast_to(dmc, (8, nc)).astype(bf16), w_ref[...].astype(bf16), (((1,), (1,)), ((), ())),
                             preferred_element_type=f32)[0:1]
        sg = jax.nn.sigmoid(cc)
        gc_ref[...] = ds * (sg * (1.0 + cc * (1.0 - sg)))

    return _pc(body, name=name, out_shape=[jax.ShapeDtypeStruct((D, nc), f32), jax.ShapeDtypeStruct((1, D), f32)],
               compiler_params=_params())(c_all, c_ctx, w_loc, dmx_loc, dmc_loc)


def _adamw_math(w, g, m, v):
    m2 = ADAM_B1 * m + (1.0 - ADAM_B1) * g
    v2 = ADAM_B2 * v + (1.0 - ADAM_B2) * jnp.square(g)
    m_hat = m2 / (1.0 - ADAM_B1 ** ADAM_STEP)
    v_hat = v2 / (1.0 - ADAM_B2 ** ADAM_STEP)
    delta = -ADAM_LR * (m_hat / (jnp.sqrt(v_hat) + ADAM_EPS) + ADAM_WD * w)
    return delta, m2, v2


def _adamw(w, m, v, g, name, partials):
    r, cdim = w.shape
    tr = _pick(r, 256, 8)

    def body(w_ref, m_ref, v_ref, g_ref, og_ref, od_ref, om_ref, ov_ref):
        if partials:
            g = g_ref[0].astype(f32)
            for s in range(1, N_DEV):
                g = g + g_ref[s].astype(f32)
        else:
            g = g_ref[...]
        d, m2, v2 = _adamw_math(w_ref[...], g, m_ref[...], v_ref[...])
        og_ref[...] = g
        od_ref[...] = d
        om_ref[...] = m2
        ov_ref[...] = v2

    blk = pl.BlockSpec((tr, cdim), lambda i: (i, 0))
    g_spec = pl.BlockSpec((N_DEV, tr, cdim), lambda i: (0, i, 0)) if partials else blk
    return _pc(body, name=name, grid=(r // tr,), in_specs=[blk, blk, blk, g_spec], out_specs=[blk] * 4,
               out_shape=[jax.ShapeDtypeStruct((r, cdim), f32)] * 4, compiler_params=_params("parallel"))(w, m, v, g)


def _sum_sources(parts, name):
    def body(*refs):
        for i_ref, o_ref in zip(refs[:len(parts)], refs[len(parts):]):
            acc = i_ref[0]
            for s in range(1, i_ref.shape[0]):
                acc = acc + i_ref[s]
            o_ref[...] = acc

    return list(_pc(body, name=name, out_shape=[jax.ShapeDtypeStruct(p.shape[1:], f32) for p in parts],
                    compiler_params=_params())(*parts))


def kernel(x, c, ctx, c_ctx, w_mod, b_mod, g_ffn1, w1_gu, w1_down, g_mix, w_in, dw_weight, dw_bias, conv_ln_g, conv_ln_b, w_conv_out, w_alpha_f, b_alpha_f, w_alpha_b, b_alpha_b, gla_norm_g, w_gla_out, w_out, g_ffn2, w2_gu, w2_down, g_final, loss_target, m_c_ctx, m_w_mod, m_b_mod, m_g_ffn1, m_w1_gu, m_w1_down, m_g_mix, m_w_in, m_dw_weight, m_dw_bias, m_conv_ln_g, m_conv_ln_b, m_w_conv_out, m_w_alpha_f, m_b_alpha_f, m_w_alpha_b, m_b_alpha_b, m_gla_norm_g, m_w_gla_out, m_w_out, m_g_ffn2, m_w2_gu, m_w2_down, m_g_final, v_c_ctx, v_w_mod, v_b_mod, v_g_ffn1, v_w1_gu, v_w1_down, v_g_mix, v_w_in, v_dw_weight, v_dw_bias, v_conv_ln_g, v_conv_ln_b, v_w_conv_out, v_w_alpha_f, v_b_alpha_f, v_w_alpha_b, v_b_alpha_b, v_gla_norm_g, v_w_gla_out, v_w_out, v_g_ffn2, v_w2_gu, v_w2_down, v_g_final):
    B, L, D = x.shape
    Lc = ctx.shape[1]
    T, Tc = B * L, B * Lc
    Tall = T + Tc
    F = w1_down.shape[1] * N_DEV
    DK, DV = D // (2 * HEADS), D // HEADS
    QK = HEADS * DK
    PW = 7 * D + LR_PAD
    tm = ROW_TILE
    tpe = L // tm
    nx, nall = T // tm, Tall // tm
    me = 4 * lax.axis_index("x") + 2 * lax.axis_index("y") + lax.axis_index("c")

    rw_all = dict(tm=tm, n_tiles=nall, tpe=tpe, nx_tiles=nx, n_ex=B + 1)
    rw_x = dict(tm=tm, n_tiles=nx, tpe=tpe, nx_tiles=nx, n_ex=B)
    rw_all_h = dict(tm=tm // 2, n_tiles=2 * nall, tpe=2 * tpe, nx_tiles=2 * nx, n_ex=B + 1)
    rw_x_h = dict(tm=tm // 2, n_tiles=2 * nx, tpe=2 * tpe, nx_tiles=2 * nx, n_ex=B)

    g_list = [w1_gu[0].astype(bf16), w1_down[0].astype(bf16), w_in[0].astype(bf16), w_conv_out[0].astype(bf16),
              w_gla_out[0].astype(bf16), w_out[0].astype(bf16), w2_gu[0].astype(bf16), w2_down[0].astype(bf16),
              dw_weight[0], w_alpha_f[0], w_alpha_b[0], c]
    (w1gu_g, w1d_g, win_g, wco_g, wgo_g, wo_g, w2gu_g, w2d_g, dww_g, waf_g, wab_g, c_g) = _exchange(g_list, False, "gather_weights")

    def cols(gat):
        return jnp.transpose(gat, (1, 0, 2)).reshape(gat.shape[1], N_DEV * gat.shape[2])

    def rows_(gat):
        return gat.reshape(N_DEV * gat.shape[1], gat.shape[2])

    W1gu, W2gu = cols(w1gu_g), cols(w2gu_g)
    W1d, W2d = rows_(w1d_g), rows_(w2d_g)
    Wco, Wgo, Wo = rows_(wco_g), rows_(wgo_g), rows_(wo_g)
    win = cols(win_g)
    o_q, o_k, o_v, o_og, o_af, o_ga, o_gb = 2 * D, 2 * D + QK, 2 * D + 2 * QK, 3 * D + 2 * QK, 4 * D + 2 * QK, 4 * D + 2 * QK + 2 * LOWRANK, 5 * D + 2 * QK + 2 * LOWRANK
    Win = jnp.concatenate([win[:, :o_q], win[:, o_v:o_og], win[:, o_og:o_af], win[:, o_ga:o_gb], win[:, o_gb:],
                           win[:, o_q:o_k], win[:, o_k:o_v], win[:, o_af:o_ga], jnp.zeros((D, LR_PAD - 2 * LOWRANK), bf16)], axis=1)
    dww = cols(dww_g)
    WA = jnp.zeros((LR_PAD, 2 * QK), f32).at[:LOWRANK, :QK].set(cols(waf_g)).at[LOWRANK:2 * LOWRANK, QK:].set(cols(wab_g)).astype(bf16)
    BA = jnp.concatenate([b_alpha_f, b_alpha_b], axis=1)
    c_all = c_g.reshape(N_DEV * B, D)
    c_ctx2 = c_ctx.reshape(1, D)

    ncm = w_mod.shape[2]
    b_mod_loc = lax.dynamic_slice(b_mod, (0, me * ncm), (1, ncm))
    mod_loc = _mod_fwd(c_all, c_ctx2, w_mod[0], b_mod_loc, "mod_fwd")
    (mod_g,) = _exchange([mod_loc], False, "gather_mod")
    mod_full = cols(mod_g)
    mod_tab = jnp.concatenate([lax.dynamic_slice(mod_full, (me * B, 0), (B, N_MOD * D)), mod_full[N_DEV * B:N_DEV * B + 1]], axis=0)
    mods = [mod_tab[:, i * D:(i + 1) * D].reshape(B + 1, 1, D) for i in range(N_MOD)]
    mods_x = [mm[:B] for mm in mods]

    x_all = jnp.concatenate([x.reshape(T, D), ctx.reshape(Tc, D)], axis=0)

    def f_ffn_in(tok, ex, sh):
        return [_rms_mod(tok[0], sh[0], ex[0], ex[1])], [], []

    (u1,) = _rowwise(f_ffn_in, name="ffn1_in", tok_in=[(x_all, D, 0, False)], ex_in=[mods[0], mods[1]], sh_in=[g_ffn1],
                     tok_out=[(D, bf16)], **rw_all)
    gu1 = _matmul(u1, W1gu, "nn", bf16, "ffn1_up")

    def f_swiglu(tok, ex, sh):
        return [_swiglu(tok[0], F)], [], []

    (h1,) = _rowwise(f_swiglu, name="ffn1_act", tok_in=[(gu1, 2 * F, 0, False)], tok_out=[(F, bf16)], **rw_all_h)
    f1 = _matmul(h1, W1d, "nn", f32, "ffn1_down")

    def mix_in(xv, fv, gate, sh, sc, g):
        x1 = xv + 0.5 * gate * fv
        return x1, _rms_mod(x1, g, sh, sc)

    def f_mix_in(tok, ex, sh):
        return list(mix_in(tok[0], tok[1], ex[0], ex[1], ex[2], sh[0])), [], []

    x1, um = _rowwise(f_mix_in, name="mix_in", tok_in=[(x_all, D, 0, False), (f1, D, 0, False)], ex_in=[mods[2], mods[3], mods[4]],
                      sh_in=[g_mix], tok_out=[(D, f32), (D, bf16)], **rw_all)
    p_all = _matmul(um, Win, "nn", bf16, "in_proj", tm_cap=512, tn_cap=2432)

    def log_decay(lr, wa, ba):
        z = _bdot(lr, wa, (1, 0)) + ba
        return _log_sigmoid(z) / TAU

    def f_decay(tok, ex, sh):
        return [log_decay(tok[0], sh[0], sh[1])], [], []

    lr_blk = (p_all, LR_PAD, 7 * D // LR_PAD, False)
    (la_all,) = _rowwise(f_decay, name="log_decay", tok_in=[lr_blk], sh_in=[WA, BA], tok_out=[(2 * QK, f32)], **rw_all)

    zeros_s = jnp.zeros((B, HEADS, DV, DK), f32)
    gla_c = dict(row0=T, nb=B, seq=Lc, D=D)
    gla_x = dict(row0=0, nb=B, seq=L, D=D)
    _, hist_cf, s_f = _gla_fwd(p_all, la_all, zeros_s, rev=False, name="gla_ctx_f", **gla_c)
    _, hist_cb, s_b = _gla_fwd(p_all, la_all, zeros_s, rev=True, name="gla_ctx_b", **gla_c)
    o_f, hist_f, _ = _gla_fwd(p_all, la_all, s_f, rev=False, name="gla_x_f", **gla_x)
    o_b, hist_b, _ = _gla_fwd(p_all, la_all, s_b, rev=True, name="gla_x_b", **gla_x)

    cz = _conv_fwd(p_all, dww, dw_bias, B=B, L=L, D=D, name="conv_fwd")

    def ln_silu(z, g, b):
        mu = jnp.mean(z, axis=-1, keepdims=True)
        var = jnp.mean(jnp.square(z - mu), axis=-1, keepdims=True)
        return jax.nn.silu((z - mu) * lax.rsqrt(var + EPS) * g + b)

    def f_ln(tok, ex, sh):
        return [ln_silu(tok[0], sh[0], sh[1])], [], []

    (zc,) = _rowwise(f_ln, name="conv_ln", tok_in=[(cz, D, 0, False)], sh_in=[conv_ln_g, conv_ln_b], tok_out=[(D, bf16)], **rw_x)
    yc = _matmul(zc, Wco, "nn", bf16, "conv_out")

    def gla_out(of, ob, og, gn):
        return _head_rms(of + ob, DV) * gn * jax.nn.silu(og.astype(f32))

    def f_gla_out(tok, ex, sh):
        return [gla_out(tok[0], tok[1], tok[2], sh[0])], [], []

    og_blk = (p_all, D, 3, False)
    (og2,) = _rowwise(f_gla_out, name="gla_norm", tok_in=[(o_f, D, 0, False), (o_b, D, 0, False), og_blk], sh_in=[gla_norm_g],
                      tok_out=[(D, bf16)], **rw_x)
    yg = _matmul(og2, Wgo, "nn", bf16, "gla_out")

    def merge(ga, gb, ycv, ygv):
        return jax.nn.sigmoid(ga.astype(f32)) * ycv.astype(f32) + jax.nn.sigmoid(gb.astype(f32)) * ygv.astype(f32)

    def f_merge(tok, ex, sh):
        return [merge(*tok)], [], []

    ga_blk, gb_blk = (p_all, D, 4, False), (p_all, D, 5, False)
    (mg,) = _rowwise(f_merge, name="merge", tok_in=[ga_blk, gb_blk, (yc, D, 0, False), (yg, D, 0, False)], tok_out=[(D, bf16)], **rw_x)
    mix = _matmul(mg, Wo, "nn", f32, "mix_out")

    def ffn2_in(x1v, mixv, g5, sh, sc, g):
        x2 = x1v + g5 * mixv
        return x2, _rms_mod(x2, g, sh, sc)

    def f_ffn2_in(tok, ex, sh):
        return list(ffn2_in(tok[0], tok[1], ex[0], ex[1], ex[2], sh[0])), [], []

    x2, u2 = _rowwise(f_ffn2_in, name="ffn2_in", tok_in=[(x1, D, 0, False), (mix, D, 0, False)], ex_in=[mods_x[5], mods_x[6], mods_x[7]],
                      sh_in=[g_ffn2], tok_out=[(D, f32), (D, bf16)], **rw_x)
    gu2 = _matmul(u2, W2gu, "nn", bf16, "ffn2_up")
    (h2,) = _rowwise(f_swiglu, name="ffn2_act", tok_in=[(gu2, 2 * F, 0, False)], tok_out=[(F, bf16)], **rw_x_h)
    f2 = _matmul(h2, W2d, "nn", f32, "ffn2_down")

    gf2 = g_final.reshape(1, D)

    def head_loss(x2v, f2v, g8, gf, tgt):
        x3 = x2v + 0.5 * g8 * f2v
        y = x3 * lax.rsqrt(jnp.mean(x3 * x3, axis=-1, keepdims=True) + EPS) * gf
        return 0.5 * jnp.sum(jnp.mean(jnp.square(y - tgt), axis=-1))

    def f_head(tok, ex, sh):
        loss, vjp = jax.vjp(lambda a, b_, c_, d_: head_loss(a, b_, c_, d_, tok[2]), tok[0], tok[1], ex[0], sh[0])
        dx3, df2, dg8, dgf = vjp(jnp.ones((), f32))
        return [dx3, df2], [dg8], [dgf, jnp.broadcast_to(loss.reshape(1, 1), (1, 128))]

    dx3, df2, dg8, dgf, loss_p = _rowwise(
        f_head, name="head", tok_in=[(x2, D, 0, False), (f2, D, 0, False), (loss_target.reshape(T, D), D, 0, False)], ex_in=[mods_x[8]],
        sh_in=[gf2], tok_out=[(D, f32), (D, bf16)], ex_out=[D], gl_out=[(1, D), (1, 128)], **rw_x)

    dh2 = _matmul(df2, W2d, "nt", bf16, "ffn2_down_dx")
    gW2d = _matmul(h2, df2, "tn", f32, "ffn2_down_dw", tm_cap=1408)

    def f_swiglu_bwd(tok, ex, sh):
        _, vjp = jax.vjp(lambda gu: _swiglu(gu, F), tok[0].astype(f32))
        return [vjp(tok[1].astype(f32))[0]], [], []

    (dgu2,) = _rowwise(f_swiglu_bwd, name="ffn2_act_bwd", tok_in=[(gu2, 2 * F, 0, False), (dh2, F, 0, False)], tok_out=[(2 * F, bf16)], **rw_x_h)
    du2 = _matmul(dgu2, W2gu, "nt", f32, "ffn2_up_dx")
    gW2gu = _matmul(u2, dgu2, "tn", f32, "ffn2_up_dw")

    def f_ffn2_in_bwd(tok, ex, sh):
        _, vjp = jax.vjp(ffn2_in, tok[0], tok[1], ex[0], ex[1], ex[2], sh[0])
        dx2, dmix, dg5, dsh, dsc, dg = vjp((tok[3], tok[2]))
        return [dx2, dmix], [dg5, dsh, dsc], [dg]

    dx2, dmix, dg5, dsh6, dsc7, dg_ffn2 = _rowwise(
        f_ffn2_in_bwd, name="ffn2_in_bwd", tok_in=[(x1, D, 0, False), (mix, D, 0, False), (du2, D, 0, False), (dx3, D, 0, False)],
        ex_in=[mods_x[5], mods_x[6], mods_x[7]], sh_in=[g_ffn2], tok_out=[(D, f32), (D, bf16)], ex_out=[D, D, D], gl_out=[(1, D)], **rw_x)

    dmg = _matmul(dmix, Wo, "nt", bf16, "mix_out_dx")
    gWo = _matmul(mg, dmix, "tn", f32, "mix_out_dw")

    def f_merge_bwd(tok, ex, sh):
        _, vjp = jax.vjp(merge, *[t.astype(f32) for t in tok[:4]])
        dga, dgb, dyc, dyg = vjp(tok[4].astype(f32))
        return [dga, dgb, dyc, dyg], [], []

    dga, dgb, dyc, dyg = _rowwise(f_merge_bwd, name="merge_bwd",
                                  tok_in=[ga_blk, gb_blk, (yc, D, 0, False), (yg, D, 0, False), (dmg, D, 0, False)],
                                  tok_out=[(D, bf16)] * 4, **rw_x)
    dzc = _matmul(dyc, Wco, "nt", f32, "conv_out_dx")
    gWco = _matmul(zc, dyc, "tn", f32, "conv_out_dw")
    dog2 = _matmul(dyg, Wgo, "nt", f32, "gla_out_dx")
    gWgo = _matmul(og2, dyg, "tn", f32, "gla_out_dw")

    def f_ln_bwd(tok, ex, sh):
        _, vjp = jax.vjp(ln_silu, tok[0], sh[0], sh[1])
        dcz, dg, db = vjp(tok[1])
        return [dcz], [], [dg, db, jnp.sum(dcz, axis=0, keepdims=True)]

    dcz, g_ln_g, g_ln_b, g_dwb = _rowwise(f_ln_bwd, name="conv_ln_bwd", tok_in=[(cz, D, 0, False), (dzc, D, 0, False)],
                                          sh_in=[conv_ln_g, conv_ln_b], tok_out=[(D, f32)], gl_out=[(1, D)] * 3, **rw_x)
    dca, dcb, g_dww = _conv_bwd(p_all, dcz, dww, B=B, L=L, D=D, name="conv_bwd")

    def f_gla_out_bwd(tok, ex, sh):
        _, vjp = jax.vjp(gla_out, tok[0], tok[1], tok[2].astype(f32), sh[0])
        dof, _, dog, dgn = vjp(tok[3])
        return [dof, dog], [], [dgn]

    d_o, dog, g_gn = _rowwise(f_gla_out_bwd, name="gla_norm_bwd",
                              tok_in=[(o_f, D, 0, False), (o_b, D, 0, False), og_blk, (dog2, D, 0, False)], sh_in=[gla_norm_g],
                              tok_out=[(D, f32), (D, bf16)], gl_out=[(1, D)], **rw_x)

    dq_f, dk_f, dv_f, dla_f, ds_f = _gla_bwd(p_all, la_all, hist_f, d_o, zeros_s, rev=False, name="gla_x_f_bwd", **gla_x)
    dq_b, dk_b, dv_b, dla_b, ds_b = _gla_bwd(p_all, la_all, hist_b, d_o, zeros_s, rev=True, name="gla_x_b_bwd", **gla_x)
    _, dk_cf, dv_cf, dla_cf, _ = _gla_bwd(p_all, la_all, hist_cf, None, ds_f, rev=False, name="gla_ctx_f_bwd", **gla_c)
    _, dk_cb, dv_cb, dla_cb, _ = _gla_bwd(p_all, la_all, hist_cb, None, ds_b, rev=True, name="gla_ctx_b_bwd", **gla_c)

    def f_add3(tok, ex, sh):
        return [tok[0] + tok[1], tok[2] + tok[3], tok[4] + tok[5]], [], []

    dq, dk, dv = _rowwise(f_add3, name="gla_sum_x", tok_in=[(t, t.shape[1], 0, False) for t in (dq_f, dq_b, dk_f, dk_b, dv_f, dv_b)],
                          tok_out=[(QK, bf16), (QK, bf16), (D, bf16)], **rw_x)

    def f_add2(tok, ex, sh):
        return [tok[0] + tok[1], tok[2] + tok[3]], [], []

    dk_c, dv_c = _rowwise(f_add2, name="gla_sum_ctx", tok_in=[(t, t.shape[1], 0, False) for t in (dk_cf, dk_cb, dv_cf, dv_cb)],
                          tok_out=[(QK, bf16), (D, bf16)], tm=tm, n_tiles=Tc // tm, tpe=tpe, nx_tiles=Tc // tm, n_ex=1)

    dla_all = jnp.concatenate([jnp.concatenate([dla_f, dla_b], axis=1), jnp.concatenate([dla_cf, dla_cb], axis=1)], axis=0)

    def f_decay_bwd(tok, ex, sh):
        _, vjp = jax.vjp(log_decay, tok[0].astype(f32), sh[0].astype(f32), sh[1])
        dlr, dwa, dba = vjp(tok[1])
        return [dlr], [], [dwa, dba]

    dlr, g_WA, g_BA = _rowwise(f_decay_bwd, name="log_decay_bwd", tok_in=[lr_blk, (dla_all, 2 * QK, 0, False)], sh_in=[WA, BA],
                               tok_out=[(LR_PAD, bf16)], gl_out=[(LR_PAD, 2 * QK), (1, 2 * QK)], **rw_all)

    zc_ = functools.partial(jnp.zeros, dtype=bf16)
    dp_x = jnp.concatenate([dca, dcb, dv, dog, dga, dgb, dq, dk, dlr[:T]], axis=1)
    dp_c = jnp.concatenate([zc_((Tc, 2 * D)), dv_c, zc_((Tc, 3 * D)), zc_((Tc, QK)), dk_c, dlr[T:]], axis=1)
    dp_all = jnp.concatenate([dp_x, dp_c], axis=0)
    dum = _matmul(dp_all, Win, "nt", f32, "in_proj_dx", tk_cap=2432)
    gWin_p = _matmul(um, dp_all, "tn", f32, "in_proj_dw", tm_cap=512, tn_cap=2432)

    def f_mix_in_bwd(tok, ex, sh):
        _, vjp = jax.vjp(mix_in, tok[0], tok[1], ex[0], ex[1], ex[2], sh[0])
        dx1, df1, dgate, dsh, dsc, dg = vjp((tok[3], tok[2]))
        return [dx1, df1], [dgate, dsh, dsc], [dg]

    dx1, df1, dg2, dsh3, dsc4, dg_mix = _rowwise(
        f_mix_in_bwd, name="mix_in_bwd", tok_in=[(x_all, D, 0, False), (f1, D, 0, False), (dum, D, 0, False), (dx2, D, 0, True)],
        ex_in=[mods[2], mods[3], mods[4]], sh_in=[g_mix], tok_out=[(D, f32), (D, bf16)], ex_out=[D, D, D], gl_out=[(1, D)], **rw_all)

    dh1 = _matmul(df1, W1d, "nt", bf16, "ffn1_down_dx")
    gW1d = _matmul(h1, df1, "tn", f32, "ffn1_down_dw", tm_cap=1408)
    (dgu1,) = _rowwise(f_swiglu_bwd, name="ffn1_act_bwd", tok_in=[(gu1, 2 * F, 0, False), (dh1, F, 0, False)], tok_out=[(2 * F, bf16)], **rw_all_h)
    du1 = _matmul(dgu1, W1gu, "nt", f32, "ffn1_up_dx")
    gW1gu = _matmul(u1, dgu1, "tn", f32, "ffn1_up_dw")

    def f_ffn_in_bwd(tok, ex, sh):
        _, vjp = jax.vjp(_rms_mod, tok[0], sh[0], ex[0], ex[1])
        dx, dg, dsh, dsc = vjp(tok[1])
        return [dx + tok[2]], [dsh, dsc], [dg]

    dx_all, dsh0, dsc1, dg_ffn1 = _rowwise(
        f_ffn_in_bwd, name="ffn1_in_bwd", tok_in=[(x_all, D, 0, False), (du1, D, 0, False), (dx1, D, 0, False)],
        ex_in=[mods[0], mods[1]], sh_in=[g_ffn1], tok_out=[(D, f32)], ex_out=[D, D], gl_out=[(1, D)], **rw_all)
    grad_x = dx_all[:T].reshape(B, L, D)

    zrow = jnp.zeros((1, 1, D), f32)
    dmod_loc = jnp.concatenate([dsh0, dsc1, dg2, dsh3, dsc4] + [jnp.concatenate([t, zrow], axis=0) for t in (dg5, dsh6, dsc7, dg8)],
                               axis=2).reshape(B + 1, N_MOD * D)
    small = [loss_p, dg_ffn1, dg_mix, g_dww[:CONV_W].reshape(1, CONV_W * D), g_dwb, g_ln_g, g_ln_b,
             g_WA[:LOWRANK, :QK].reshape(1, LOWRANK * QK), g_BA[:, :QK], g_WA[LOWRANK:2 * LOWRANK, QK:].reshape(1, LOWRANK * QK), g_BA[:, QK:],
             g_gn, dg_ffn2, dgf]
    small_w = [s.shape[1] for s in small]
    def to8(v):
        n_pad = -(-v.shape[1] // 1024) * 1024
        return jnp.pad(v, ((0, 0), (0, n_pad - v.shape[1]))).reshape(8, n_pad // 8)

    def from8(a, n):
        return a.reshape(1, a.size)[:, :n]

    dmod_g, small_g = _exchange([dmod_loc, to8(jnp.concatenate(small, axis=1))], False, "gather_small")
    dmx = dmod_g[:, :B].reshape(N_DEV * B, N_MOD * D)
    dmc = dmod_g[:, B]
    gWmod, gcc_p = _mod_bwd(c_all, c_ctx2, w_mod[0], lax.dynamic_slice(dmx, (0, me * ncm), (N_DEV * B, ncm)),
                            lax.dynamic_slice(dmc, (0, me * ncm), (N_DEV, ncm)), "mod_bwd")

    def col_shards(g):
        return jnp.transpose(g.reshape(g.shape[0], N_DEV, g.shape[1] // N_DEV), (1, 0, 2)).astype(bf16)

    def row_shards(g):
        return g.reshape(N_DEV, g.shape[0] // N_DEV, g.shape[1]).astype(bf16)

    gWin = jnp.concatenate([gWin_p[:, :2 * D], gWin_p[:, 6 * D:7 * D], gWin_p[:, 2 * D:4 * D], gWin_p[:, 7 * D:7 * D + 2 * LOWRANK],
                            gWin_p[:, 4 * D:6 * D]], axis=1)
    rs_in = [col_shards(gW1gu), row_shards(gW1d), col_shards(gWin), row_shards(gWco), row_shards(gWgo), row_shards(gWo),
             col_shards(gW2gu), row_shards(gW2d)]
    rs_out = _exchange(rs_in, True, "scatter_grads")
    (gcc_g,) = _exchange([to8(gcc_p)], False, "gather_cctx")

    sums, g_cc, g_bmod = _sum_sources([small_g, gcc_g, jnp.concatenate([dmx, dmc], axis=0).reshape(N_DEV * (B + 1), 8, N_MOD * D // 8)], "sum_small")
    sums, g_cc, g_bmod = from8(sums, sum(small_w)), from8(g_cc, D), from8(g_bmod, N_MOD * D)
    offs = [0]
    for wd in small_w:
        offs.append(offs[-1] + wd)
    sm = [sums[:, offs[i]:offs[i + 1]] for i in range(len(small))]
    loss = sm[0][0, 0]
    ncd, nca = dw_weight.shape[2], w_alpha_f.shape[2]
    g_dww_loc = lax.dynamic_slice(sm[3].reshape(CONV_W, D), (0, me * ncd), (CONV_W, ncd)).reshape(1, CONV_W * ncd)
    g_waf_loc = lax.dynamic_slice(sm[7].reshape(LOWRANK, QK), (0, me * nca), (LOWRANK, nca)).reshape(1, LOWRANK * nca)
    g_wab_loc = lax.dynamic_slice(sm[9].reshape(LOWRANK, QK), (0, me * nca), (LOWRANK, nca)).reshape(1, LOWRANK * nca)

    big = {}
    for nm, wv, mv, vv, part in (("w1_gu", w1_gu, m_w1_gu, v_w1_gu, rs_out[0]), ("w1_down", w1_down, m_w1_down, v_w1_down, rs_out[1]),
                                 ("w_in", w_in, m_w_in, v_w_in, rs_out[2]), ("w_conv_out", w_conv_out, m_w_conv_out, v_w_conv_out, rs_out[3]),
                                 ("w_gla_out", w_gla_out, m_w_gla_out, v_w_gla_out, rs_out[4]), ("w_out", w_out, m_w_out, v_w_out, rs_out[5]),
                                 ("w2_gu", w2_gu, m_w2_gu, v_w2_gu, rs_out[6]), ("w2_down", w2_down, m_w2_down, v_w2_down, rs_out[7])):
        big[nm] = [t[None] for t in _adamw(wv[0], mv[0], vv[0], part, "adamw_" + nm, True)]
    big["w_mod"] = [t[None] for t in _adamw(w_mod[0], m_w_mod[0], v_w_mod[0], gWmod, "adamw_w_mod", False)]

    small_params = [("c_ctx", c_ctx, m_c_ctx, v_c_ctx, g_cc), ("b_mod", b_mod, m_b_mod, v_b_mod, g_bmod), ("g_ffn1", g_ffn1, m_g_ffn1, v_g_ffn1, sm[1]),
                    ("g_mix", g_mix, m_g_mix, v_g_mix, sm[2]), ("dw_weight", dw_weight, m_dw_weight, v_dw_weight, g_dww_loc),
                    ("dw_bias", dw_bias, m_dw_bias, v_dw_bias, sm[4]), ("conv_ln_g", conv_ln_g, m_conv_ln_g, v_conv_ln_g, sm[5]),
                    ("conv_ln_b", conv_ln_b, m_conv_ln_b, v_conv_ln_b, sm[6]), ("w_alpha_f", w_alpha_f, m_w_alpha_f, v_w_alpha_f, g_waf_loc),
                    ("b_alpha_f", b_alpha_f, m_b_alpha_f, v_b_alpha_f, sm[8]), ("w_alpha_b", w_alpha_b, m_w_alpha_b, v_w_alpha_b, g_wab_loc),
                    ("b_alpha_b", b_alpha_b, m_b_alpha_b, v_b_alpha_b, sm[10]), ("gla_norm_g", gla_norm_g, m_gla_norm_g, v_gla_norm_g, sm[11]),
                    ("g_ffn2", g_ffn2, m_g_ffn2, v_g_ffn2, sm[12]), ("g_final", g_final, m_g_final, v_g_final, sm[13])]
    flat = lambda t: t.reshape(1, t.size)
    pw, pm, pv, pg = (jnp.concatenate([flat(sp[i]) for sp in small_params], axis=1) for i in (1, 2, 3, 4))
    n_small = pw.shape[1]
    s_g, s_d, s_m, s_v = (from8(t, n_small) for t in _adamw(to8(pw), to8(pm), to8(pv), to8(pg), "adamw_small", False))
    small_out, o0 = {}, 0
    for nm, wv, _, _, _ in small_params:
        small_out[nm] = [t[:, o0:o0 + wv.size].reshape(wv.shape) for t in (s_g, s_d, s_m, s_v)]
        o0 += wv.size

    order = ["c_ctx", "w_mod", "b_mod", "g_ffn1", "w1_gu", "w1_down", "g_mix", "w_in", "dw_weight", "dw_bias", "conv_ln_g", "conv_ln_b",
             "w_conv_out", "w_alpha_f", "b_alpha_f", "w_alpha_b", "b_alpha_b", "gla_norm_g", "w_gla_out", "w_out", "g_ffn2", "w2_gu",
             "w2_down", "g_final"]
    res = {**big, **small_out}
    return (loss, grad_x, *[res[n][0] for n in order], *[res[n][1] for n in order], *[res[n][2] for n in order], *[res[n][3] for n in order])
```

```python
import functools

import jax
import jax.numpy as jnp
from jax import lax
from jax.experimental import pallas as pl
from jax.experimental.pallas import tpu as pltpu

f32, bf16 = jnp.float32, jnp.bfloat16

N_DEV = 8
HEADS = 4
LOWRANK = 16
CONV_W = 31
CONV_PAD = 16
CHUNK = 64
SUB = 16
GLA_ROWS = 256
TAU = 16.0
EPS = 1e-6
N_MOD = 9
LR_PAD = 128
ROW_TILE = 256
V7X_VMEM_BYTES = 64 << 20
VMEM_LIMIT = (V7X_VMEM_BYTES * 3) // 4

ADAM_LR, ADAM_B1, ADAM_B2, ADAM_EPS, ADAM_WD, ADAM_STEP = 0.001, 0.9, 0.999, 1e-08, 0.01, 10

MESH = pl.DeviceIdType.MESH


def _pc(body, **kw):
    return pl.pallas_call(body, **kw)


def _params(*sem):
    return pltpu.CompilerParams(dimension_semantics=sem, vmem_limit_bytes=VMEM_LIMIT)


def _pick(n, cap, unit=128):
    best = None
    for t in range(unit, min(n, cap) + 1, unit):
        if n % t == 0:
            best = t
    return best or n


def _matmul(a, b, mode, out_dtype, name, tm_cap=1024, tn_cap=1536, tk_cap=None, carry=None):
    if mode == "tn":
        (K, M), N = a.shape, b.shape[1]
    elif mode == "nt":
        (M, K), N = a.shape, b.shape[0]
    else:
        (M, K), N = a.shape, b.shape[1]
    tk_cap = tk_cap or (1024 if mode == "tn" else 2816)
    tm, tn, tk = _pick(M, tm_cap), _pick(N, tn_cap), _pick(K, tk_cap)
    nk = K // tk
    a_spec = pl.BlockSpec((tk, tm), lambda i, j, k: (k, i)) if mode == "tn" else pl.BlockSpec((tm, tk), lambda i, j, k: (i, k))
    b_spec = pl.BlockSpec((tn, tk), lambda i, j, k: (j, k)) if mode == "nt" else pl.BlockSpec((tk, tn), lambda i, j, k: (k, j))
    dims = {"nn": ((1,), (0,)), "nt": ((1,), (1,)), "tn": ((0,), (0,))}[mode]

    def body_single(a_ref, b_ref, o_ref):
        o_ref[...] = lax.dot_general(a_ref[...].astype(bf16), b_ref[...].astype(bf16), (dims, ((), ())),
                                     preferred_element_type=f32).astype(out_dtype)

    def body(a_ref, b_ref, o_ref, acc_ref):
        k = pl.program_id(2)
        part = lax.dot_general(a_ref[...].astype(bf16), b_ref[...].astype(bf16), (dims, ((), ())), preferred_element_type=f32)

        @pl.when(k == 0)
        def _():
            acc_ref[...] = part

        @pl.when(k > 0)
        def _():
            acc_ref[...] += part

        @pl.when(k == nk - 1)
        def _():
            o_ref[...] = acc_ref[...].astype(out_dtype)

    (out,), carried = _call(
        body_single if nk == 1 else body, name=name, grid=(M // tm, N // tn, nk), in_specs=[a_spec, b_spec],
        out_specs=[pl.BlockSpec((tm, tn), lambda i, j, k: (i, j))], out_shape=[jax.ShapeDtypeStruct((M, N), out_dtype)],
        scratch_shapes=[] if nk == 1 else [pltpu.VMEM((tm, tn), f32)], sem=("parallel", "parallel", "arbitrary"),
        args=(a, b), carry=carry)
    return out if carry is None else (out, carried)


def _rowwise(fn, *, name, tm, n_tiles, tpe, nx_tiles, n_ex, tok_in=(), ex_in=(), sh_in=(), tok_out=(), ex_out=(), gl_out=()):
    def seg(i):
        return jnp.minimum(i // tpe, n_ex - 1)

    in_specs, args = [], []
    for arr, w, cb, x_only in tok_in:
        if x_only:
            in_specs.append(pl.BlockSpec((tm, w), functools.partial(lambda i, cb: (jnp.minimum(i, nx_tiles - 1), cb), cb=cb)))
        else:
            in_specs.append(pl.BlockSpec((tm, w), functools.partial(lambda i, cb: (i, cb), cb=cb)))
        args.append(arr)
    for arr in ex_in:
        in_specs.append(pl.BlockSpec((1, 1, arr.shape[-1]), lambda i: (seg(i), 0, 0)))
        args.append(arr)
    for arr in sh_in:
        in_specs.append(pl.BlockSpec(arr.shape, functools.partial(lambda i, nd: (0,) * nd, nd=arr.ndim)))
        args.append(arr)
    out_specs, out_shape = [], []
    for w, dt in tok_out:
        out_specs.append(pl.BlockSpec((tm, w), lambda i: (i, 0)))
        out_shape.append(jax.ShapeDtypeStruct((n_tiles * tm, w), dt))
    for w in ex_out:
        out_specs.append(pl.BlockSpec((1, 1, w), lambda i: (seg(i), 0, 0)))
        out_shape.append(jax.ShapeDtypeStruct((n_ex, 1, w), f32))
    for r, w in gl_out:
        out_specs.append(pl.BlockSpec((r, w), lambda i: (0, 0)))
        out_shape.append(jax.ShapeDtypeStruct((r, w), f32))
    n_tok, n_exi, n_sh = len(tok_in), len(ex_in), len(sh_in)
    n_to, n_eo = len(tok_out), len(ex_out)
    x_only_flags = [t[3] for t in tok_in]

    def body(*refs):
        i = pl.program_id(0)
        ins, outs = refs[: n_tok + n_exi + n_sh], refs[n_tok + n_exi + n_sh:]
        is_x = i < nx_tiles
        tok_vals = []
        for r, xo in zip(ins[:n_tok], x_only_flags):
            v = r[...]
            tok_vals.append(jnp.where(is_x, v, jnp.zeros_like(v)) if xo else v)
        ex_vals = [r[0] for r in ins[n_tok:n_tok + n_exi]]
        sh_vals = [r[...] for r in ins[n_tok + n_exi:]]
        t_o, e_o, g_o = fn(tok_vals, ex_vals, sh_vals)
        for r, v in zip(outs[:n_to], t_o):
            r[...] = v.astype(r.dtype)
        first = jnp.logical_and(i % tpe == 0, i <= nx_tiles)
        for r, v in zip(outs[n_to:n_to + n_eo], e_o):
            @pl.when(first)
            def _(r=r, v=v):
                r[0] = v

            @pl.when(jnp.logical_not(first))
            def _(r=r, v=v):
                r[0] += v
        for r, v in zip(outs[n_to + n_eo:], g_o):
            @pl.when(i == 0)
            def _(r=r, v=v):
                r[...] = v

            @pl.when(i > 0)
            def _(r=r, v=v):
                r[...] += v

    res = _pc(body, name=name, grid=(n_tiles,), in_specs=in_specs, out_specs=out_specs, out_shape=out_shape,
              compiler_params=_params("arbitrary"))(*args)
    return list(res)


def _rms_mod(x, g, sh, sc):
    y = x * lax.rsqrt(jnp.mean(x * x, axis=-1, keepdims=True) + EPS) * g
    return y * (1.0 + sc) + sh


def _log_sigmoid(z):
    return jnp.minimum(z, 0.0) - jnp.log(1.0 + jnp.exp(-jnp.abs(z)))


def _swiglu(gu, F):
    a, b = gu[:, :F].astype(f32), gu[:, F:].astype(f32)
    return jax.nn.silu(a) * b


def _head_rms(o, DV):
    parts = []
    for h in range(HEADS):
        oh = o[:, h * DV:(h + 1) * DV]
        parts.append(oh * lax.rsqrt(jnp.mean(oh * oh, axis=-1, keepdims=True) + EPS))
    return jnp.concatenate(parts, axis=1)


@functools.partial(jax.custom_vjp, nondiff_argnums=(2,))
def _bdot(a, b, dims):
    return lax.dot_general(a.astype(bf16), b.astype(bf16), (((dims[0],), (dims[1],)), ((), ())), preferred_element_type=f32)


def _bdot_fwd(a, b, dims):
    return _bdot(a, b, dims), (a, b)


def _bdot_bwd(dims, res, g):
    a, b = res
    ca, cb = dims
    da = _bdot(g, b, (1, 1 - cb)) if ca == 1 else _bdot(b, g, (1 - cb, 1))
    db = _bdot(a, g, (1 - ca, 0)) if cb == 0 else _bdot(g, a, (0, 1 - ca))
    return da, db


_bdot.defvjp(_bdot_fwd, _bdot_bwd)


def _gla_chunk(St, q, k, v, g, *, rev, scale):
    C, DK = q.shape
    r = lax.broadcasted_iota(jnp.int32, (C, C), 0)
    c = lax.broadcasted_iota(jnp.int32, (C, C), 1)
    tri = ((r <= c) if rev else (r >= c)).astype(f32)
    b = jnp.dot(tri, g, precision=lax.Precision.HIGHEST, preferred_element_type=f32)
    qs = q * scale
    inter = _bdot(qs * jnp.exp(b), St, (1, 1))
    rr = lax.broadcasted_iota(jnp.int32, (SUB, SUB, DK), 0)
    cc = lax.broadcasted_iota(jnp.int32, (SUB, SUB, DK), 1)
    m3 = (rr <= cc) if rev else (rr >= cc)
    outs = []
    for i in range(C // SUB):
        lo, hi = i * SUB, (i + 1) * SUB
        bi, qi, ki, vi = b[lo:hi], qs[lo:hi], k[lo:hi], v[lo:hi]
        rel = bi[:, None, :] - bi[None, :, :]
        e = jnp.where(m3, jnp.exp(jnp.where(m3, rel, 0.0)), 0.0)
        att = jnp.sum(qi[:, None, :] * e * ki[None, :, :], axis=-1)
        acc = _bdot(att, vi, (1, 0))
        ref_row = b[hi - 1:hi] if rev else b[lo:lo + 1]
        prev = slice(hi, C) if rev else slice(0, lo)
        if (hi < C) if rev else (lo > 0):
            qn = qi * jnp.exp(bi - ref_row)
            ks = k[prev] * jnp.exp(ref_row - b[prev])
            acc = acc + _bdot(_bdot(qn, ks, (1, 1)), v[prev], (1, 0))
        outs.append(acc)
    o = inter + jnp.concatenate(outs, axis=0)
    b_last = b[0:1] if rev else b[C - 1:C]
    kd = k * jnp.exp(b_last - b)
    St_new = St * jnp.exp(b_last) + _bdot(v, kd, (0, 0))
    return St_new, o


def _gla_specs(D, rev_blocks, row0, seq):
    DK, DV = D // (2 * HEADS), D // HEADS
    nblk = seq // GLA_ROWS
    rb0 = row0 // GLA_ROWS

    def blk(j):
        return (nblk - 1 - j) if rev_blocks else j

    return DK, DV, nblk, rb0, blk


def _gla_in_specs(D, rev, rows):
    QK = D // 2
    return [
        pl.BlockSpec((GLA_ROWS, QK), lambda b, j: (rows(b, j), 6 * D // QK)),
        pl.BlockSpec((GLA_ROWS, QK), lambda b, j: (rows(b, j), 6 * D // QK + 1)),
        pl.BlockSpec((GLA_ROWS, D), lambda b, j: (rows(b, j), 2)),
        pl.BlockSpec((GLA_ROWS, QK), lambda b, j: (rows(b, j), 1 if rev else 0)),
    ]


def _gla_fwd(p_all, la_all, s0, *, rev, row0, nb, seq, D, name, carry=None):
    DK, DV, nblk, rb0, blk = _gla_specs(D, rev, row0, seq)
    cpb = GLA_ROWS // CHUNK

    def rows(b, j):
        return rb0 + b * nblk + blk(j)

    in_specs = _gla_in_specs(D, rev, rows) + [pl.BlockSpec((1, HEADS, DV, DK), lambda b, j: (b, 0, 0, 0))]
    out_specs = [
        pl.BlockSpec((GLA_ROWS, D), lambda b, j: (b * nblk + blk(j), 0)),
        pl.BlockSpec((1, HEADS, cpb, DV, DK), lambda b, j: (b, 0, blk(j), 0, 0)),
        pl.BlockSpec((1, HEADS, DV, DK), lambda b, j: (b, 0, 0, 0)),
    ]
    out_shape = [
        jax.ShapeDtypeStruct((nb * seq, D), f32),
        jax.ShapeDtypeStruct((nb, HEADS, seq // CHUNK, DV, DK), f32),
        jax.ShapeDtypeStruct((nb, HEADS, DV, DK), f32),
    ]
    chunk = functools.partial(_gla_chunk, rev=rev, scale=DK ** -0.5)

    def body(q_ref, k_ref, v_ref, la_ref, s0_ref, o_ref, hist_ref, sfin_ref, st_ref):
        j = pl.program_id(1)

        @pl.when(j == 0)
        def _():
            st_ref[...] = s0_ref[0]

        def step(ci, carry):
            cc = (cpb - 1 - ci) if rev else ci
            sl = pl.ds(cc * CHUNK, CHUNK)
            for h in range(HEADS):
                kq, kv = pl.ds(h * DK, DK), pl.ds(h * DV, DV)
                St = st_ref[h]
                hist_ref[0, h, cc] = St
                St2, o = chunk(St, q_ref[sl, kq].astype(f32), k_ref[sl, kq].astype(f32), v_ref[sl, kv].astype(f32), la_ref[sl, kq])
                o_ref[sl, kv] = o
                st_ref[h] = St2
            return carry

        for ci in range(cpb):
            step(ci, 0)

        @pl.when(j == nblk - 1)
        def _():
            sfin_ref[0] = st_ref[...]

    res, carried = _call(body, name=name, grid=(nb, nblk), in_specs=in_specs, out_specs=out_specs, out_shape=out_shape,
                         scratch_shapes=[pltpu.VMEM((HEADS, DV, DK), f32)], sem=("parallel", "arbitrary"),
                         args=(p_all, p_all, p_all, la_all, s0), carry=carry)
    return res if carry is None else (res, carried)


def _gla_bwd(p_all, la_all, hist, do, dsfin, *, rev, row0, nb, seq, D, name):
    DK, DV, nblk, rb0, blk = _gla_specs(D, not rev, row0, seq)
    cpb = GLA_ROWS // CHUNK
    QK = HEADS * DK
    has_do = do is not None

    def rows(b, j):
        return rb0 + b * nblk + blk(j)

    in_specs = _gla_in_specs(D, rev, rows) + [
        pl.BlockSpec((1, HEADS, cpb, DV, DK), lambda b, j: (b, 0, blk(j), 0, 0)),
        pl.BlockSpec((1, HEADS, DV, DK), lambda b, j: (b, 0, 0, 0)),
    ]
    args = [p_all, p_all, p_all, la_all, hist, dsfin]
    if has_do:
        in_specs.append(pl.BlockSpec((GLA_ROWS, D), lambda b, j: (b * nblk + blk(j), 0)))
        args.append(do)
    out_specs = [
        pl.BlockSpec((GLA_ROWS, QK), lambda b, j: (b * nblk + blk(j), 0)),
        pl.BlockSpec((GLA_ROWS, QK), lambda b, j: (b * nblk + blk(j), 0)),
        pl.BlockSpec((GLA_ROWS, D), lambda b, j: (b * nblk + blk(j), 0)),
        pl.BlockSpec((GLA_ROWS, QK), lambda b, j: (b * nblk + blk(j), 0)),
        pl.BlockSpec((1, HEADS, DV, DK), lambda b, j: (b, 0, 0, 0)),
    ]
    out_shape = [
        jax.ShapeDtypeStruct((nb * seq, QK), f32), jax.ShapeDtypeStruct((nb * seq, QK), f32),
        jax.ShapeDtypeStruct((nb * seq, D), f32), jax.ShapeDtypeStruct((nb * seq, QK), f32),
        jax.ShapeDtypeStruct((nb, HEADS, DV, DK), f32),
    ]
    chunk = functools.partial(_gla_chunk, rev=rev, scale=DK ** -0.5)

    def body(*refs):
        if has_do:
            q_ref, k_ref, v_ref, la_ref, hist_ref, dsfin_ref, do_ref, dq_ref, dk_ref, dv_ref, dla_ref, ds0_ref, ds_ref = refs
        else:
            q_ref, k_ref, v_ref, la_ref, hist_ref, dsfin_ref, dq_ref, dk_ref, dv_ref, dla_ref, ds0_ref, ds_ref = refs
        j = pl.program_id(1)

        @pl.when(j == 0)
        def _():
            ds_ref[...] = dsfin_ref[0]

        def step(ci, carry):
            cc = ci if rev else (cpb - 1 - ci)
            sl = pl.ds(cc * CHUNK, CHUNK)
            for h in range(HEADS):
                kq, kv = pl.ds(h * DK, DK), pl.ds(h * DV, DV)
                prim = (hist_ref[0, h, cc], q_ref[sl, kq].astype(f32), k_ref[sl, kq].astype(f32), v_ref[sl, kv].astype(f32), la_ref[sl, kq])
                _, vjp = jax.vjp(chunk, *prim)
                d_o = do_ref[sl, kv] if has_do else jnp.zeros((CHUNK, DV), f32)
                dSt, dq, dk, dv, dg = vjp((ds_ref[h], d_o))
                dq_ref[sl, kq] = dq
                dk_ref[sl, kq] = dk
                dv_ref[sl, kv] = dv
                dla_ref[sl, kq] = dg
                ds_ref[h] = dSt
            return carry

        for ci in range(cpb):
            step(ci, 0)

        @pl.when(j == nblk - 1)
        def _():
            ds0_ref[0] = ds_ref[...]

    return _pc(body, name=name, grid=(nb, nblk), in_specs=in_specs, out_specs=out_specs, out_shape=out_shape,
               scratch_shapes=[pltpu.VMEM((HEADS, DV, DK), f32)], compiler_params=_params("parallel", "arbitrary"))(*args)


def _conv_fwd(p_all, dw_w, dw_b, *, B, L, D, name):
    ct = _pick(D, 256)
    nj = D // ct
    st = _pick(L, 256, 8)

    def body(a_ref, b_ref, w_ref, bias_ref, o_ref, zp_ref):
        zp_ref[pl.ds(0, CONV_PAD), :] = jnp.zeros((CONV_PAD, ct), f32)
        zp_ref[pl.ds(CONV_PAD + L, CONV_PAD), :] = jnp.zeros((CONV_PAD, ct), f32)
        zp_ref[pl.ds(CONV_PAD, L), :] = a_ref[...].astype(f32) * jax.nn.sigmoid(b_ref[...].astype(f32))
        off = CONV_PAD - CONV_W // 2
        for t0 in range(0, L, st):
            acc = jnp.zeros((st, ct), f32) + bias_ref[...]
            for k in range(CONV_W):
                acc = acc + w_ref[pl.ds(k, 1), :] * zp_ref[pl.ds(t0 + k + off, st), :]
            o_ref[pl.ds(t0, st), :] = acc

    return _pc(
        body, name=name, grid=(B, nj),
        in_specs=[pl.BlockSpec((L, ct), lambda b, j: (b, j)), pl.BlockSpec((L, ct), lambda b, j: (b, nj + j)),
                  pl.BlockSpec((CONV_W, ct), lambda b, j: (0, j)), pl.BlockSpec((1, ct), lambda b, j: (0, j))],
        out_specs=pl.BlockSpec((L, ct), lambda b, j: (b, j)), out_shape=jax.ShapeDtypeStruct((B * L, D), f32),
        scratch_shapes=[pltpu.VMEM((L + 2 * CONV_PAD, ct), f32)], compiler_params=_params("parallel", "parallel"),
    )(p_all, p_all, dw_w, dw_b)


def _conv_bwd(p_all, dcz, dw_w, *, B, L, D, name, carry=None):
    ct = _pick(D, 256)
    nj = D // ct
    st = _pick(L, 256, 8)
    half = CONV_W // 2

    def body(a_ref, b_ref, dcz_ref, w_ref, da_ref, db_ref, ddw_ref, zp_ref, dp_ref):
        bi = pl.program_id(1)
        a = a_ref[...].astype(f32)
        sg = jax.nn.sigmoid(b_ref[...].astype(f32))
        for ref in (zp_ref, dp_ref):
            ref[pl.ds(0, CONV_PAD), :] = jnp.zeros((CONV_PAD, ct), f32)
            ref[pl.ds(CONV_PAD + L, CONV_PAD), :] = jnp.zeros((CONV_PAD, ct), f32)
        zp_ref[pl.ds(CONV_PAD, L), :] = a * sg
        dp_ref[pl.ds(CONV_PAD, L), :] = dcz_ref[...]

        @pl.when(bi == 0)
        def _():
            ddw_ref[...] = jnp.zeros_like(ddw_ref)

        for t0 in range(0, L, st):
            acc = jnp.zeros((st, ct), f32)
            dout = dcz_ref[pl.ds(t0, st), :]
            for k in range(CONV_W):
                acc = acc + w_ref[pl.ds(k, 1), :] * dp_ref[pl.ds(t0 + CONV_PAD + half - k, st), :]
                ddw_ref[pl.ds(k, 1), :] += jnp.sum(dout * zp_ref[pl.ds(t0 + k + CONV_PAD - half, st), :], axis=0, keepdims=True)
            a_t = a_ref[pl.ds(t0, st), :].astype(f32)
            sg_t = jax.nn.sigmoid(b_ref[pl.ds(t0, st), :].astype(f32))
            da_ref[pl.ds(t0, st), :] = (acc * sg_t).astype(bf16)
            db_ref[pl.ds(t0, st), :] = (acc * a_t * sg_t * (1.0 - sg_t)).astype(bf16)

    res, carried = _call(
        body, name=name, grid=(nj, B),
        in_specs=[pl.BlockSpec((L, ct), lambda j, b: (b, j)), pl.BlockSpec((L, ct), lambda j, b: (b, nj + j)),
                  pl.BlockSpec((L, ct), lambda j, b: (b, j)), pl.BlockSpec((CONV_W, ct), lambda j, b: (0, j))],
        out_specs=[pl.BlockSpec((L, ct), lambda j, b: (b, j)), pl.BlockSpec((L, ct), lambda j, b: (b, j)),
                   pl.BlockSpec((2 * CONV_PAD, ct), lambda j, b: (0, j))],
        out_shape=[jax.ShapeDtypeStruct((B * L, D), bf16), jax.ShapeDtypeStruct((B * L, D), bf16),
                   jax.ShapeDtypeStruct((2 * CONV_PAD, D), f32)],
        scratch_shapes=[pltpu.VMEM((L + 2 * CONV_PAD, ct), f32), pltpu.VMEM((L + 2 * CONV_PAD, ct), f32)],
        sem=("parallel", "arbitrary"), args=(p_all, p_all, dcz, dw_w), carry=carry)
    return res if carry is None else (res, carried)


def _exchange(arrs, scatter, name):
    ex = _Exchange(arrs, scatter)
    n = ex.n

    def body(*refs):
        ex.start(refs[:n], refs[n:2 * n], refs[2 * n:])
        ex.finish(refs[:n], refs[n:2 * n], refs[2 * n:])

    res = _pc(body, name=name, in_specs=ex.specs, out_specs=ex.specs, out_shape=ex.out_shape, scratch_shapes=ex.scratch)(*arrs)
    return list(res)


class _Exchange:
    def __init__(self, arrs, scatter):
        self.arrs, self.scatter, self.n = list(arrs), scatter, len(arrs)
        self.out_shape = [jax.ShapeDtypeStruct(((N_DEV,) + a.shape[1:]) if scatter else ((N_DEV,) + a.shape), a.dtype) for a in arrs]
        self.specs = [pl.BlockSpec(memory_space=pl.ANY)] * self.n
        self.scratch = [pltpu.SemaphoreType.DMA((self.n, N_DEV - 1)), pltpu.SemaphoreType.DMA((self.n, N_DEV - 1)),
                        pltpu.SemaphoreType.DMA((self.n,))]

    def _copies(self, ins, outs, sems):
        send_sems, recv_sems, local_sems = sems
        me = 4 * lax.axis_index("x") + 2 * lax.axis_index("y") + lax.axis_index("c")
        local = [pltpu.make_async_copy(ins[a].at[me] if self.scatter else ins[a], outs[a].at[me], local_sems.at[a]) for a in range(self.n)]
        sends, lands = [], []
        for k in range(1, N_DEV):
            p, s = (me + k) % N_DEV, (me + N_DEV - k) % N_DEV
            for a in range(self.n):
                sends.append(pltpu.make_async_remote_copy(
                    src_ref=ins[a].at[p] if self.scatter else ins[a], dst_ref=outs[a].at[me],
                    send_sem=send_sems.at[a, k - 1], recv_sem=recv_sems.at[a, k - 1],
                    device_id=(p // 4, (p // 2) % 2, p % 2), device_id_type=MESH))
                lands.append(pltpu.make_async_remote_copy(
                    src_ref=ins[a].at[s] if self.scatter else ins[a], dst_ref=outs[a].at[s],
                    send_sem=send_sems.at[a, k - 1], recv_sem=recv_sems.at[a, k - 1],
                    device_id=(s // 4, (s // 2) % 2, s % 2), device_id_type=MESH))
        return local, sends, lands

    def start(self, ins, outs, sems):
        local, sends, _ = self._copies(ins, outs, sems)
        for cp in local + sends:
            cp.start()

    def finish(self, ins, outs, sems):
        local, sends, lands = self._copies(ins, outs, sems)
        for cp in lands:
            cp.wait_recv()
        for cp in sends:
            cp.wait_send()
        for cp in local:
            cp.wait()


def _carried(inner, n_in, n_out, grid, ex):
    n = ex.n

    def body(*refs):
        own_in, c_in = refs[:n_in], refs[n_in:n_in + n]
        own_out, c_out = refs[n_in + n:n_in + n + n_out], refs[n_in + n + n_out:n_in + 2 * n + n_out]
        rest = refs[n_in + 2 * n + n_out:]
        own_scr, sems = rest[:len(rest) - 3], rest[len(rest) - 3:]
        pids = [pl.program_id(d) for d in range(len(grid))]
        first = functools.reduce(jnp.logical_and, [p == 0 for p in pids])
        last = functools.reduce(jnp.logical_and, [p == g - 1 for p, g in zip(pids, grid)])

        @pl.when(first)
        def _():
            ex.start(c_in, c_out, sems)

        inner(*own_in, *own_out, *own_scr)

        @pl.when(last)
        def _():
            ex.finish(c_in, c_out, sems)

    return body


def _call(inner, *, name, grid, in_specs, out_specs, out_shape, scratch_shapes, sem, args, carry=None):
    if carry is None:
        res = _pc(inner, name=name, grid=grid, in_specs=in_specs, out_specs=out_specs, out_shape=out_shape,
                  scratch_shapes=scratch_shapes, compiler_params=_params(*sem))(*args)
        return list(res), None
    ex = _Exchange(*carry)
    res = _pc(_carried(inner, len(in_specs), len(out_specs), grid, ex), name=name, grid=grid,
              in_specs=list(in_specs) + ex.specs, out_specs=list(out_specs) + ex.specs, out_shape=list(out_shape) + ex.out_shape,
              scratch_shapes=list(scratch_shapes) + ex.scratch, compiler_params=_params(*(["arbitrary"] * len(grid))))(*args, *ex.arrs)
    res = list(res)
    return res[:len(out_specs)], res[len(out_specs):]


def _mod_fwd(c_all, c_ctx, w_loc, b_loc, name):
    nr, D = c_all.shape
    nc = w_loc.shape[1]

    def body(c_ref, cc_ref, w_ref, b_ref, o_ref):
        a = jnp.concatenate([c_ref[...], jnp.broadcast_to(cc_ref[...], (8, D))], axis=0)
        s = jax.nn.silu(a).astype(bf16)
        o_ref[...] = jnp.dot(s, w_ref[...].astype(bf16), preferred_element_type=f32) + b_ref[...]

    return _pc(body, name=name, out_shape=jax.ShapeDtypeStruct((nr + 8, nc), f32), compiler_params=_params())(c_all, c_ctx, w_loc, b_loc)


def _mod_bwd(c_all, c_ctx, w_loc, dmx_loc, dmc_loc, name):
    nr, D = c_all.shape
    nc = w_loc.shape[1]

    def body(c_ref, cc_ref, w_ref, dmx_ref, dmc_ref, gw_ref, gc_ref):
        cc = cc_ref[...]
        a = jnp.concatenate([c_ref[...], jnp.broadcast_to(cc, (N_DEV, D))], axis=0)
        s = jax.nn.silu(a).astype(bf16)
        g = jnp.concatenate([dmx_ref[...], dmc_ref[...]], axis=0).astype(bf16)
        gw_ref[...] = lax.dot_general(s, g, (((0,), (0,)), ((), ())), preferred_element_type=f32)
        dmc = jnp.sum(dmc_ref[...], axis=0, keepdims=True)
        ds = lax.dot_general(jnp.broadcast_to(dmc, (8, nc)).astype(bf16), w_ref[...].astype(bf16), (((1,), (1,)), ((), ())),
                             preferred_element_type=f32)[0:1]
        sg = jax.nn.sigmoid(cc)
        gc_ref[...] = ds * (sg * (1.0 + cc * (1.0 - sg)))

    return _pc(body, name=name, out_shape=[jax.ShapeDtypeStruct((D, nc), f32), jax.ShapeDtypeStruct((1, D), f32)],
               compiler_params=_params())(c_all, c_ctx, w_loc, dmx_loc, dmc_loc)


def _adamw_math(w, g, m, v):
    m2 = ADAM_B1 * m + (1.0 - ADAM_B1) * g
    v2 = ADAM_B2 * v + (1.0 - ADAM_B2) * jnp.square(g)
    m_hat = m2 / (1.0 - ADAM_B1 ** ADAM_STEP)
    v_hat = v2 / (1.0 - ADAM_B2 ** ADAM_STEP)
    delta = -ADAM_LR * (m_hat / (jnp.sqrt(v_hat) + ADAM_EPS) + ADAM_WD * w)
    return delta, m2, v2


def _adamw(w, m, v, g, name, partials):
    r, cdim = w.shape
    tr = _pick(r, 256, 8)

    def body(w_ref, m_ref, v_ref, g_ref, og_ref, od_ref, om_ref, ov_ref):
        if partials:
            g = g_ref[0].astype(f32)
            for s in range(1, N_DEV):
                g = g + g_ref[s].astype(f32)
        else:
            g = g_ref[...]
        d, m2, v2 = _adamw_math(w_ref[...], g, m_ref[...], v_ref[...])
        og_ref[...] = g
        od_ref[...] = d
        om_ref[...] = m2
        ov_ref[...] = v2

    blk = pl.BlockSpec((tr, cdim), lambda i: (i, 0))
    g_spec = pl.BlockSpec((N_DEV, tr, cdim), lambda i: (0, i, 0)) if partials else blk
    return _pc(body, name=name, grid=(r // tr,), in_specs=[blk, blk, blk, g_spec], out_specs=[blk] * 4,
               out_shape=[jax.ShapeDtypeStruct((r, cdim), f32)] * 4, compiler_params=_params("parallel"))(w, m, v, g)


def _sum_sources(parts, name):
    def body(*refs):
        for i_ref, o_ref in zip(refs[:len(parts)], refs[len(parts):]):
            acc = i_ref[0]
            for s in range(1, i_ref.shape[0]):
                acc = acc + i_ref[s]
            o_ref[...] = acc

    return list(_pc(body, name=name, out_shape=[jax.ShapeDtypeStruct(p.shape[1:], f32) for p in parts],
                    compiler_params=_params())(*parts))


def kernel(x, c, ctx, c_ctx, w_mod, b_mod, g_ffn1, w1_gu, w1_down, g_mix, w_in, dw_weight, dw_bias, conv_ln_g, conv_ln_b, w_conv_out, w_alpha_f, b_alpha_f, w_alpha_b, b_alpha_b, gla_norm_g, w_gla_out, w_out, g_ffn2, w2_gu, w2_down, g_final, loss_target, m_c_ctx, m_w_mod, m_b_mod, m_g_ffn1, m_w1_gu, m_w1_down, m_g_mix, m_w_in, m_dw_weight, m_dw_bias, m_conv_ln_g, m_conv_ln_b, m_w_conv_out, m_w_alpha_f, m_b_alpha_f, m_w_alpha_b, m_b_alpha_b, m_gla_norm_g, m_w_gla_out, m_w_out, m_g_ffn2, m_w2_gu, m_w2_down, m_g_final, v_c_ctx, v_w_mod, v_b_mod, v_g_ffn1, v_w1_gu, v_w1_down, v_g_mix, v_w_in, v_dw_weight, v_dw_bias, v_conv_ln_g, v_conv_ln_b, v_w_conv_out, v_w_alpha_f, v_b_alpha_f, v_w_alpha_b, v_b_alpha_b, v_gla_norm_g, v_w_gla_out, v_w_out, v_g_ffn2, v_w2_gu, v_w2_down, v_g_final):
    B, L, D = x.shape
    Lc = ctx.shape[1]
    T, Tc = B * L, B * Lc
    Tall = T + Tc
    F = w1_down.shape[1] * N_DEV
    DK, DV = D // (2 * HEADS), D // HEADS
    QK = HEADS * DK
    PW = 7 * D + LR_PAD
    tm = ROW_TILE
    tpe = L // tm
    nx, nall = T // tm, Tall // tm
    me = 4 * lax.axis_index("x") + 2 * lax.axis_index("y") + lax.axis_index("c")

    rw_all = dict(tm=tm, n_tiles=nall, tpe=tpe, nx_tiles=nx, n_ex=B + 1)
    rw_x = dict(tm=tm, n_tiles=nx, tpe=tpe, nx_tiles=nx, n_ex=B)
    rw_all_h = dict(tm=tm // 2, n_tiles=2 * nall, tpe=2 * tpe, nx_tiles=2 * nx, n_ex=B + 1)
    rw_x_h = dict(tm=tm // 2, n_tiles=2 * nx, tpe=2 * tpe, nx_tiles=2 * nx, n_ex=B)

    w1gu_g, dww_g, waf_g, wab_g, c_g = _exchange([w1_gu[0].astype(bf16), dw_weight[0], w_alpha_f[0], w_alpha_b[0], c], False, "gather_first")

    def cols(gat):
        return jnp.transpose(gat, (1, 0, 2)).reshape(gat.shape[1], N_DEV * gat.shape[2])

    def rows_(gat):
        return gat.reshape(N_DEV * gat.shape[1], gat.shape[2])

    W1gu = cols(w1gu_g)
    dww = cols(dww_g)
    WA = jnp.zeros((LR_PAD, 2 * QK), f32).at[:LOWRANK, :QK].set(cols(waf_g)).at[LOWRANK:2 * LOWRANK, QK:].set(cols(wab_g)).astype(bf16)
    BA = jnp.concatenate([b_alpha_f, b_alpha_b], axis=1)
    c_all = c_g.reshape(N_DEV * B, D)
    c_ctx2 = c_ctx.reshape(1, D)

    ncm = w_mod.shape[2]
    b_mod_loc = lax.dynamic_slice(b_mod, (0, me * ncm), (1, ncm))
    mod_loc = _mod_fwd(c_all, c_ctx2, w_mod[0], b_mod_loc, "mod_fwd")
    (mod_g,) = _exchange([mod_loc], False, "gather_mod")
    mod_full = cols(mod_g)
    mod_tab = jnp.concatenate([lax.dynamic_slice(mod_full, (me * B, 0), (B, N_MOD * D)), mod_full[N_DEV * B:N_DEV * B + 1]], axis=0)
    mods = [mod_tab[:, i * D:(i + 1) * D].reshape(B + 1, 1, D) for i in range(N_MOD)]
    mods_x = [mm[:B] for mm in mods]

    x_all = jnp.concatenate([x.reshape(T, D), ctx.reshape(Tc, D)], axis=0)

    def f_ffn_in(tok, ex, sh):
        return [_rms_mod(tok[0], sh[0], ex[0], ex[1])], [], []

    (u1,) = _rowwise(f_ffn_in, name="ffn1_in", tok_in=[(x_all, D, 0, False)], ex_in=[mods[0], mods[1]], sh_in=[g_ffn1],
                     tok_out=[(D, bf16)], **rw_all)
    gu1, (w1d_g, win_g) = _matmul(u1, W1gu, "nn", bf16, "ffn1_up", carry=([w1_down[0].astype(bf16), w_in[0].astype(bf16)], False))
    W1d = rows_(w1d_g)
    win = cols(win_g)
    o_q, o_k, o_v, o_og, o_af, o_ga, o_gb = 2 * D, 2 * D + QK, 2 * D + 2 * QK, 3 * D + 2 * QK, 4 * D + 2 * QK, 4 * D + 2 * QK + 2 * LOWRANK, 5 * D + 2 * QK + 2 * LOWRANK
    Win = jnp.concatenate([win[:, :o_q], win[:, o_v:o_og], win[:, o_og:o_af], win[:, o_ga:o_gb], win[:, o_gb:],
                           win[:, o_q:o_k], win[:, o_k:o_v], win[:, o_af:o_ga], jnp.zeros((D, LR_PAD - 2 * LOWRANK), bf16)], axis=1)

    def f_swiglu(tok, ex, sh):
        return [_swiglu(tok[0], F)], [], []

    (h1,) = _rowwise(f_swiglu, name="ffn1_act", tok_in=[(gu1, 2 * F, 0, False)], tok_out=[(F, bf16)], **rw_all_h)
    f1 = _matmul(h1, W1d, "nn", f32, "ffn1_down")

    def mix_in(xv, fv, gate, sh, sc, g):
        x1 = xv + 0.5 * gate * fv
        return x1, _rms_mod(x1, g, sh, sc)

    def f_mix_in(tok, ex, sh):
        return list(mix_in(tok[0], tok[1], ex[0], ex[1], ex[2], sh[0])), [], []

    x1, um = _rowwise(f_mix_in, name="mix_in", tok_in=[(x_all, D, 0, False), (f1, D, 0, False)], ex_in=[mods[2], mods[3], mods[4]],
                      sh_in=[g_mix], tok_out=[(D, f32), (D, bf16)], **rw_all)
    p_all, (wco_g, wgo_g, wo_g, w2gu_g) = _matmul(
        um, Win, "nn", bf16, "in_proj", tm_cap=512, tn_cap=2432,
        carry=([w_conv_out[0].astype(bf16), w_gla_out[0].astype(bf16), w_out[0].astype(bf16), w2_gu[0].astype(bf16)], False))
    Wco, Wgo, Wo, W2gu = rows_(wco_g), rows_(wgo_g), rows_(wo_g), cols(w2gu_g)

    def log_decay(lr, wa, ba):
        z = _bdot(lr, wa, (1, 0)) + ba
        return _log_sigmoid(z) / TAU

    def f_decay(tok, ex, sh):
        return [log_decay(tok[0], sh[0], sh[1])], [], []

    lr_blk = (p_all, LR_PAD, 7 * D // LR_PAD, False)
    (la_all,) = _rowwise(f_decay, name="log_decay", tok_in=[lr_blk], sh_in=[WA, BA], tok_out=[(2 * QK, f32)], **rw_all)

    zeros_s = jnp.zeros((B, HEADS, DV, DK), f32)
    gla_c = dict(row0=T, nb=B, seq=Lc, D=D)
    gla_x = dict(row0=0, nb=B, seq=L, D=D)
    _, hist_cf, s_f = _gla_fwd(p_all, la_all, zeros_s, rev=False, name="gla_ctx_f", **gla_c)
    _, hist_cb, s_b = _gla_fwd(p_all, la_all, zeros_s, rev=True, name="gla_ctx_b", **gla_c)
    (o_f, hist_f, _), (w2d_g,) = _gla_fwd(p_all, la_all, s_f, rev=False, name="gla_x_f", carry=([w2_down[0].astype(bf16)], False), **gla_x)
    W2d = rows_(w2d_g)
    o_b, hist_b, _ = _gla_fwd(p_all, la_all, s_b, rev=True, name="gla_x_b", **gla_x)

    cz = _conv_fwd(p_all, dww, dw_bias, B=B, L=L, D=D, name="conv_fwd")

    def ln_silu(z, g, b):
        mu = jnp.mean(z, axis=-1, keepdims=True)
        var = jnp.mean(jnp.square(z - mu), axis=-1, keepdims=True)
        return jax.nn.silu((z - mu) * lax.rsqrt(var + EPS) * g + b)

    def f_ln(tok, ex, sh):
        return [ln_silu(tok[0], sh[0], sh[1])], [], []

    (zc,) = _rowwise(f_ln, name="conv_ln", tok_in=[(cz, D, 0, False)], sh_in=[conv_ln_g, conv_ln_b], tok_out=[(D, bf16)], **rw_x)
    yc = _matmul(zc, Wco, "nn", bf16, "conv_out")

    def gla_out(of, ob, og, gn):
        return _head_rms(of + ob, DV) * gn * jax.nn.silu(og.astype(f32))

    def f_gla_out(tok, ex, sh):
        return [gla_out(tok[0], tok[1], tok[2], sh[0])], [], []

    og_blk = (p_all, D, 3, False)
    (og2,) = _rowwise(f_gla_out, name="gla_norm", tok_in=[(o_f, D, 0, False), (o_b, D, 0, False), og_blk], sh_in=[gla_norm_g],
                      tok_out=[(D, bf16)], **rw_x)
    yg = _matmul(og2, Wgo, "nn", bf16, "gla_out")

    def merge(ga, gb, ycv, ygv):
        return jax.nn.sigmoid(ga.astype(f32)) * ycv.astype(f32) + jax.nn.sigmoid(gb.astype(f32)) * ygv.astype(f32)

    def f_merge(tok, ex, sh):
        return [merge(*tok)], [], []

    ga_blk, gb_blk = (p_all, D, 4, False), (p_all, D, 5, False)
    (mg,) = _rowwise(f_merge, name="merge", tok_in=[ga_blk, gb_blk, (yc, D, 0, False), (yg, D, 0, False)], tok_out=[(D, bf16)], **rw_x)
    mix = _matmul(mg, Wo, "nn", f32, "mix_out")

    def ffn2_in(x1v, mixv, g5, sh, sc, g):
        x2 = x1v + g5 * mixv
        return x2, _rms_mod(x2, g, sh, sc)

    def f_ffn2_in(tok, ex, sh):
        return list(ffn2_in(tok[0], tok[1], ex[0], ex[1], ex[2], sh[0])), [], []

    x2, u2 = _rowwise(f_ffn2_in, name="ffn2_in", tok_in=[(x1, D, 0, False), (mix, D, 0, False)], ex_in=[mods_x[5], mods_x[6], mods_x[7]],
                      sh_in=[g_ffn2], tok_out=[(D, f32), (D, bf16)], **rw_x)
    gu2 = _matmul(u2, W2gu, "nn", bf16, "ffn2_up")
    (h2,) = _rowwise(f_swiglu, name="ffn2_act", tok_in=[(gu2, 2 * F, 0, False)], tok_out=[(F, bf16)], **rw_x_h)
    f2 = _matmul(h2, W2d, "nn", f32, "ffn2_down")

    gf2 = g_final.reshape(1, D)

    def head_loss(x2v, f2v, g8, gf, tgt):
        x3 = x2v + 0.5 * g8 * f2v
        y = x3 * lax.rsqrt(jnp.mean(x3 * x3, axis=-1, keepdims=True) + EPS) * gf
        return 0.5 * jnp.sum(jnp.mean(jnp.square(y - tgt), axis=-1))

    def f_head(tok, ex, sh):
        loss, vjp = jax.vjp(lambda a, b_, c_, d_: head_loss(a, b_, c_, d_, tok[2]), tok[0], tok[1], ex[0], sh[0])
        dx3, df2, dg8, dgf = vjp(jnp.ones((), f32))
        return [dx3, df2], [dg8], [dgf, jnp.broadcast_to(loss.reshape(1, 1), (1, 128))]

    dx3, df2, dg8, dgf, loss_p = _rowwise(
        f_head, name="head", tok_in=[(x2, D, 0, False), (f2, D, 0, False), (loss_target.reshape(T, D), D, 0, False)], ex_in=[mods_x[8]],
        sh_in=[gf2], tok_out=[(D, f32), (D, bf16)], ex_out=[D], gl_out=[(1, D), (1, 128)], **rw_x)

    dh2 = _matmul(df2, W2d, "nt", bf16, "ffn2_down_dx")
    gW2d = _matmul(h2, df2, "tn", f32, "ffn2_down_dw", tm_cap=1408)

    def f_swiglu_bwd(tok, ex, sh):
        _, vjp = jax.vjp(lambda gu: _swiglu(gu, F), tok[0].astype(f32))
        return [vjp(tok[1].astype(f32))[0]], [], []

    (dgu2,) = _rowwise(f_swiglu_bwd, name="ffn2_act_bwd", tok_in=[(gu2, 2 * F, 0, False), (dh2, F, 0, False)], tok_out=[(2 * F, bf16)], **rw_x_h)
    du2 = _matmul(dgu2, W2gu, "nt", f32, "ffn2_up_dx")
    gW2gu = _matmul(u2, dgu2, "tn", f32, "ffn2_up_dw")

    def f_ffn2_in_bwd(tok, ex, sh):
        _, vjp = jax.vjp(ffn2_in, tok[0], tok[1], ex[0], ex[1], ex[2], sh[0])
        dx2, dmix, dg5, dsh, dsc, dg = vjp((tok[3], tok[2]))
        return [dx2, dmix], [dg5, dsh, dsc], [dg]

    dx2, dmix, dg5, dsh6, dsc7, dg_ffn2 = _rowwise(
        f_ffn2_in_bwd, name="ffn2_in_bwd", tok_in=[(x1, D, 0, False), (mix, D, 0, False), (du2, D, 0, False), (dx3, D, 0, False)],
        ex_in=[mods_x[5], mods_x[6], mods_x[7]], sh_in=[g_ffn2], tok_out=[(D, f32), (D, bf16)], ex_out=[D, D, D], gl_out=[(1, D)], **rw_x)

    dmg = _matmul(dmix, Wo, "nt", bf16, "mix_out_dx")
    gWo = _matmul(mg, dmix, "tn", f32, "mix_out_dw")

    def f_merge_bwd(tok, ex, sh):
        _, vjp = jax.vjp(merge, *[t.astype(f32) for t in tok[:4]])
        dga, dgb, dyc, dyg = vjp(tok[4].astype(f32))
        return [dga, dgb, dyc, dyg], [], []

    dga, dgb, dyc, dyg = _rowwise(f_merge_bwd, name="merge_bwd",
                                  tok_in=[ga_blk, gb_blk, (yc, D, 0, False), (yg, D, 0, False), (dmg, D, 0, False)],
                                  tok_out=[(D, bf16)] * 4, **rw_x)
    dzc = _matmul(dyc, Wco, "nt", f32, "conv_out_dx")
    gWco = _matmul(zc, dyc, "tn", f32, "conv_out_dw")
    dog2 = _matmul(dyg, Wgo, "nt", f32, "gla_out_dx")
    gWgo = _matmul(og2, dyg, "tn", f32, "gla_out_dw")

    def f_ln_bwd(tok, ex, sh):
        _, vjp = jax.vjp(ln_silu, tok[0], sh[0], sh[1])
        dcz, dg, db = vjp(tok[1])
        return [dcz], [], [dg, db, jnp.sum(dcz, axis=0, keepdims=True)]

    dcz, g_ln_g, g_ln_b, g_dwb = _rowwise(f_ln_bwd, name="conv_ln_bwd", tok_in=[(cz, D, 0, False), (dzc, D, 0, False)],
                                          sh_in=[conv_ln_g, conv_ln_b], tok_out=[(D, f32)], gl_out=[(1, D)] * 3, **rw_x)
    def col_shards(g):
        return jnp.transpose(g.reshape(g.shape[0], N_DEV, g.shape[1] // N_DEV), (1, 0, 2)).astype(bf16)

    def row_shards(g):
        return g.reshape(N_DEV, g.shape[0] // N_DEV, g.shape[1]).astype(bf16)

    (dca, dcb, g_dww), (r_w2d, r_w2gu, r_wo, r_wco, r_wgo) = _conv_bwd(
        p_all, dcz, dww, B=B, L=L, D=D, name="conv_bwd",
        carry=([row_shards(gW2d), col_shards(gW2gu), row_shards(gWo), row_shards(gWco), row_shards(gWgo)], True))

    def f_gla_out_bwd(tok, ex, sh):
        _, vjp = jax.vjp(gla_out, tok[0], tok[1], tok[2].astype(f32), sh[0])
        dof, _, dog, dgn = vjp(tok[3])
        return [dof, dog], [], [dgn]

    d_o, dog, g_gn = _rowwise(f_gla_out_bwd, name="gla_norm_bwd",
                              tok_in=[(o_f, D, 0, False), (o_b, D, 0, False), og_blk, (dog2, D, 0, False)], sh_in=[gla_norm_g],
                              tok_out=[(D, f32), (D, bf16)], gl_out=[(1, D)], **rw_x)

    dq_f, dk_f, dv_f, dla_f, ds_f = _gla_bwd(p_all, la_all, hist_f, d_o, zeros_s, rev=False, name="gla_x_f_bwd", **gla_x)
    dq_b, dk_b, dv_b, dla_b, ds_b = _gla_bwd(p_all, la_all, hist_b, d_o, zeros_s, rev=True, name="gla_x_b_bwd", **gla_x)
    _, dk_cf, dv_cf, dla_cf, _ = _gla_bwd(p_all, la_all, hist_cf, None, ds_f, rev=False, name="gla_ctx_f_bwd", **gla_c)
    _, dk_cb, dv_cb, dla_cb, _ = _gla_bwd(p_all, la_all, hist_cb, None, ds_b, rev=True, name="gla_ctx_b_bwd", **gla_c)

    def f_add3(tok, ex, sh):
        return [tok[0] + tok[1], tok[2] + tok[3], tok[4] + tok[5]], [], []

    dq, dk, dv = _rowwise(f_add3, name="gla_sum_x", tok_in=[(t, t.shape[1], 0, False) for t in (dq_f, dq_b, dk_f, dk_b, dv_f, dv_b)],
                          tok_out=[(QK, bf16), (QK, bf16), (D, bf16)], **rw_x)

    def f_add2(tok, ex, sh):
        return [tok[0] + tok[1], tok[2] + tok[3]], [], []

    dk_c, dv_c = _rowwise(f_add2, name="gla_sum_ctx", tok_in=[(t, t.shape[1], 0, False) for t in (dk_cf, dk_cb, dv_cf, dv_cb)],
                          tok_out=[(QK, bf16), (D, bf16)], tm=tm, n_tiles=Tc // tm, tpe=tpe, nx_tiles=Tc // tm, n_ex=1)

    dla_all = jnp.concatenate([jnp.concatenate([dla_f, dla_b], axis=1), jnp.concatenate([dla_cf, dla_cb], axis=1)], axis=0)

    def f_decay_bwd(tok, ex, sh):
        _, vjp = jax.vjp(log_decay, tok[0].astype(f32), sh[0].astype(f32), sh[1])
        dlr, dwa, dba = vjp(tok[1])
        return [dlr], [], [dwa, dba]

    dlr, g_WA, g_BA = _rowwise(f_decay_bwd, name="log_decay_bwd", tok_in=[lr_blk, (dla_all, 2 * QK, 0, False)], sh_in=[WA, BA],
                               tok_out=[(LR_PAD, bf16)], gl_out=[(LR_PAD, 2 * QK), (1, 2 * QK)], **rw_all)

    zc_ = functools.partial(jnp.zeros, dtype=bf16)
    dp_x = jnp.concatenate([dca, dcb, dv, dog, dga, dgb, dq, dk, dlr[:T]], axis=1)
    dp_c = jnp.concatenate([zc_((Tc, 2 * D)), dv_c, zc_((Tc, 3 * D)), zc_((Tc, QK)), dk_c, dlr[T:]], axis=1)
    dp_all = jnp.concatenate([dp_x, dp_c], axis=0)
    gWin_p = _matmul(um, dp_all, "tn", f32, "in_proj_dw", tm_cap=512, tn_cap=2432)
    gWin = jnp.concatenate([gWin_p[:, :2 * D], gWin_p[:, 6 * D:7 * D], gWin_p[:, 2 * D:4 * D], gWin_p[:, 7 * D:7 * D + 2 * LOWRANK],
                            gWin_p[:, 4 * D:6 * D]], axis=1)
    dum, (r_win,) = _matmul(dp_all, Win, "nt", f32, "in_proj_dx", tk_cap=2432, carry=([col_shards(gWin)], True))

    def f_mix_in_bwd(tok, ex, sh):
        _, vjp = jax.vjp(mix_in, tok[0], tok[1], ex[0], ex[1], ex[2], sh[0])
        dx1, df1, dgate, dsh, dsc, dg = vjp((tok[3], tok[2]))
        return [dx1, df1], [dgate, dsh, dsc], [dg]

    dx1, df1, dg2, dsh3, dsc4, dg_mix = _rowwise(
        f_mix_in_bwd, name="mix_in_bwd", tok_in=[(x_all, D, 0, False), (f1, D, 0, False), (dum, D, 0, False), (dx2, D, 0, True)],
        ex_in=[mods[2], mods[3], mods[4]], sh_in=[g_mix], tok_out=[(D, f32), (D, bf16)], ex_out=[D, D, D], gl_out=[(1, D)], **rw_all)

    dh1 = _matmul(df1, W1d, "nt", bf16, "ffn1_down_dx")
    gW1d = _matmul(h1, df1, "tn", f32, "ffn1_down_dw", tm_cap=1408)
    (dgu1,) = _rowwise(f_swiglu_bwd, name="ffn1_act_bwd", tok_in=[(gu1, 2 * F, 0, False), (dh1, F, 0, False)], tok_out=[(2 * F, bf16)], **rw_all_h)
    gW1gu, (r_w1d,) = _matmul(u1, dgu1, "tn", f32, "ffn1_up_dw", carry=([row_shards(gW1d)], True))
    du1, (r_w1gu,) = _matmul(dgu1, W1gu, "nt", f32, "ffn1_up_dx", carry=([col_shards(gW1gu)], True))

    def f_ffn_in_bwd(tok, ex, sh):
        _, vjp = jax.vjp(_rms_mod, tok[0], sh[0], ex[0], ex[1])
        dx, dg, dsh, dsc = vjp(tok[1])
        return [dx + tok[2]], [dsh, dsc], [dg]

    dx_all, dsh0, dsc1, dg_ffn1 = _rowwise(
        f_ffn_in_bwd, name="ffn1_in_bwd", tok_in=[(x_all, D, 0, False), (du1, D, 0, False), (dx1, D, 0, False)],
        ex_in=[mods[0], mods[1]], sh_in=[g_ffn1], tok_out=[(D, f32)], ex_out=[D, D], gl_out=[(1, D)], **rw_all)
    grad_x = dx_all[:T].reshape(B, L, D)

    zrow = jnp.zeros((1, 1, D), f32)
    dmod_loc = jnp.concatenate([dsh0, dsc1, dg2, dsh3, dsc4] + [jnp.concatenate([t, zrow], axis=0) for t in (dg5, dsh6, dsc7, dg8)],
                               axis=2).reshape(B + 1, N_MOD * D)
    small = [loss_p, dg_ffn1, dg_mix, g_dww[:CONV_W].reshape(1, CONV_W * D), g_dwb, g_ln_g, g_ln_b,
             g_WA[:LOWRANK, :QK].reshape(1, LOWRANK * QK), g_BA[:, :QK], g_WA[LOWRANK:2 * LOWRANK, QK:].reshape(1, LOWRANK * QK), g_BA[:, QK:],
             g_gn, dg_ffn2, dgf]
    small_w = [s.shape[1] for s in small]
    def to8(v):
        n_pad = -(-v.shape[1] // 1024) * 1024
        return jnp.pad(v, ((0, 0), (0, n_pad - v.shape[1]))).reshape(8, n_pad // 8)

    def from8(a, n):
        return a.reshape(1, a.size)[:, :n]

    dmod_g, small_g = _exchange([dmod_loc, to8(jnp.concatenate(small, axis=1))], False, "gather_small")
    dmx = dmod_g[:, :B].reshape(N_DEV * B, N_MOD * D)
    dmc = dmod_g[:, B]
    gWmod, gcc_p = _mod_bwd(c_all, c_ctx2, w_mod[0], lax.dynamic_slice(dmx, (0, me * ncm), (N_DEV * B, ncm)),
                            lax.dynamic_slice(dmc, (0, me * ncm), (N_DEV, ncm)), "mod_bwd")

    rs_out = [r_w1gu, r_w1d, r_win, r_wco, r_wgo, r_wo, r_w2gu, r_w2d]
    (gcc_g,) = _exchange([to8(gcc_p)], False, "gather_cctx")

    sums, g_cc, g_bmod = _sum_sources([small_g, gcc_g, jnp.concatenate([dmx, dmc], axis=0).reshape(N_DEV * (B + 1), 8, N_MOD * D // 8)], "sum_small")
    sums, g_cc, g_bmod = from8(sums, sum(small_w)), from8(g_cc, D), from8(g_bmod, N_MOD * D)
    offs = [0]
    for wd in small_w:
        offs.append(offs[-1] + wd)
    sm = [sums[:, offs[i]:offs[i + 1]] for i in range(len(small))]
    loss = sm[0][0, 0]
    ncd, nca = dw_weight.shape[2], w_alpha_f.shape[2]
    g_dww_loc = lax.dynamic_slice(sm[3].reshape(CONV_W, D), (0, me * ncd), (CONV_W, ncd)).reshape(1, CONV_W * ncd)
    g_waf_loc = lax.dynamic_slice(sm[7].reshape(LOWRANK, QK), (0, me * nca), (LOWRANK, nca)).reshape(1, LOWRANK * nca)
    g_wab_loc = lax.dynamic_slice(sm[9].reshape(LOWRANK, QK), (0, me * nca), (LOWRANK, nca)).reshape(1, LOWRANK * nca)

    big = {}
    for nm, wv, mv, vv, part in (("w1_gu", w1_gu, m_w1_gu, v_w1_gu, rs_out[0]), ("w1_down", w1_down, m_w1_down, v_w1_down, rs_out[1]),
                                 ("w_in", w_in, m_w_in, v_w_in, rs_out[2]), ("w_conv_out", w_conv_out, m_w_conv_out, v_w_conv_out, rs_out[3]),
                                 ("w_gla_out", w_gla_out, m_w_gla_out, v_w_gla_out, rs_out[4]), ("w_out", w_out, m_w_out, v_w_out, rs_out[5]),
                                 ("w2_gu", w2_gu, m_w2_gu, v_w2_gu, rs_out[6]), ("w2_down", w2_down, m_w2_down, v_w2_down, rs_out[7])):
        big[nm] = [t[None] for t in _adamw(wv[0], mv[0], vv[0], part, "adamw_" + nm, True)]
    big["w_mod"] = [t[None] for t in _adamw(w_mod[0], m_w_mod[0], v_w_mod[0], gWmod, "adamw_w_mod", False)]

    small_params = [("c_ctx", c_ctx, m_c_ctx, v_c_ctx, g_cc), ("b_mod", b_mod, m_b_mod, v_b_mod, g_bmod), ("g_ffn1", g_ffn1, m_g_ffn1, v_g_ffn1, sm[1]),
                    ("g_mix", g_mix, m_g_mix, v_g_mix, sm[2]), ("dw_weight", dw_weight, m_dw_weight, v_dw_weight, g_dww_loc),
                    ("dw_bias", dw_bias, m_dw_bias, v_dw_bias, sm[4]), ("conv_ln_g", conv_ln_g, m_conv_ln_g, v_conv_ln_g, sm[5]),
                    ("conv_ln_b", conv_ln_b, m_conv_ln_b, v_conv_ln_b, sm[6]), ("w_alpha_f", w_alpha_f, m_w_alpha_f, v_w_alpha_f, g_waf_loc),
                    ("b_alpha_f", b_alpha_f, m_b_alpha_f, v_b_alpha_f, sm[8]), ("w_alpha_b", w_alpha_b, m_w_alpha_b, v_w_alpha_b, g_wab_loc),
                    ("b_alpha_b", b_alpha_b, m_b_alpha_b, v_b_alpha_b, sm[10]), ("gla_norm_g", gla_norm_g, m_gla_norm_g, v_gla_norm_g, sm[11]),
                    ("g_ffn2", g_ffn2, m_g_ffn2, v_g_ffn2, sm[12]), ("g_final", g_final, m_g_final, v_g_final, sm[13])]
    flat = lambda t: t.reshape(1, t.size)
    pw, pm, pv, pg = (jnp.concatenate([flat(sp[i]) for sp in small_params], axis=1) for i in (1, 2, 3, 4))
    n_small = pw.shape[1]
    s_g, s_d, s_m, s_v = (from8(t, n_small) for t in _adamw(to8(pw), to8(pm), to8(pv), to8(pg), "adamw_small", False))
    small_out, o0 = {}, 0
    for nm, wv, _, _, _ in small_params:
        small_out[nm] = [t[:, o0:o0 + wv.size].reshape(wv.shape) for t in (s_g, s_d, s_m, s_v)]
        o0 += wv.size

    order = ["c_ctx", "w_mod", "b_mod", "g_ffn1", "w1_gu", "w1_down", "g_mix", "w_in", "dw_weight", "dw_bias", "conv_ln_g", "conv_ln_b",
             "w_conv_out", "w_alpha_f", "b_alpha_f", "w_alpha_b", "b_alpha_b", "gla_norm_g", "w_gla_out", "w_out", "g_ffn2", "w2_gu",
             "w2_down", "g_final"]
    res = {**big, **small_out}
    return (loss, grad_x, *[res[n][0] for n in order], *[res[n][1] for n in order], *[res[n][2] for n in order], *[res[n][3] for n in order])
```

```python
import functools

import jax
import jax.numpy as jnp
from jax import lax
from jax.experimental import pallas as pl
from jax.experimental.pallas import tpu as pltpu

f32, bf16 = jnp.float32, jnp.bfloat16

N_DEV = 8
HEADS = 4
LOWRANK = 16
CONV_W = 31
CONV_PAD = 16
SUBLANES = 8
CHUNK = 64
SUB = 16
GLA_ROWS = 256
GLA_SAFE_DECAY = 60.0
TAU = 16.0
EPS = 1e-6
N_MOD = 9
LR_PAD = 128
ROW_TILE = 256
V7X_VMEM_BYTES = 64 << 20
VMEM_LIMIT = (V7X_VMEM_BYTES * 3) // 4

ADAM_LR, ADAM_B1, ADAM_B2, ADAM_EPS, ADAM_WD, ADAM_STEP = 0.001, 0.9, 0.999, 1e-08, 0.01, 10

MESH = pl.DeviceIdType.MESH


def _pc(body, **kw):
    return pl.pallas_call(body, **kw)


def _params(*sem):
    return pltpu.CompilerParams(dimension_semantics=sem, vmem_limit_bytes=VMEM_LIMIT)


def _pick(n, cap, unit=128):
    best = None
    for t in range(unit, min(n, cap) + 1, unit):
        if n % t == 0:
            best = t
    return best or n


def _matmul(a, b, mode, out_dtype, name, tm_cap=1024, tn_cap=1536, tk_cap=None, carry=None):
    if mode == "tn":
        (K, M), N = a.shape, b.shape[1]
    elif mode == "nt":
        (M, K), N = a.shape, b.shape[0]
    else:
        (M, K), N = a.shape, b.shape[1]
    tk_cap = tk_cap or (1024 if mode == "tn" else 2816)
    tm, tn, tk = _pick(M, tm_cap), _pick(N, tn_cap), _pick(K, tk_cap)
    nk = K // tk
    a_spec = pl.BlockSpec((tk, tm), lambda i, j, k: (k, i)) if mode == "tn" else pl.BlockSpec((tm, tk), lambda i, j, k: (i, k))
    b_spec = pl.BlockSpec((tn, tk), lambda i, j, k: (j, k)) if mode == "nt" else pl.BlockSpec((tk, tn), lambda i, j, k: (k, j))
    dims = {"nn": ((1,), (0,)), "nt": ((1,), (1,)), "tn": ((0,), (0,))}[mode]

    def body_single(a_ref, b_ref, o_ref):
        o_ref[...] = lax.dot_general(a_ref[...].astype(bf16), b_ref[...].astype(bf16), (dims, ((), ())),
                                     preferred_element_type=f32).astype(out_dtype)

    def body(a_ref, b_ref, o_ref, acc_ref):
        k = pl.program_id(2)
        part = lax.dot_general(a_ref[...].astype(bf16), b_ref[...].astype(bf16), (dims, ((), ())), preferred_element_type=f32)

        @pl.when(k == 0)
        def _():
            acc_ref[...] = part

        @pl.when(k > 0)
        def _():
            acc_ref[...] += part

        @pl.when(k == nk - 1)
        def _():
            o_ref[...] = acc_ref[...].astype(out_dtype)

    (out,), carried = _call(
        body_single if nk == 1 else body, name=name, grid=(M // tm, N // tn, nk), in_specs=[a_spec, b_spec],
        out_specs=[pl.BlockSpec((tm, tn), lambda i, j, k: (i, j))], out_shape=[jax.ShapeDtypeStruct((M, N), out_dtype)],
        scratch_shapes=[] if nk == 1 else [pltpu.VMEM((tm, tn), f32)], sem=("parallel", "parallel", "arbitrary"),
        args=(a, b), carry=carry)
    return out if carry is None else (out, carried)


def _rowwise(fn, *, name, tm, n_tiles, tpe, nx_tiles, n_ex, tok_in=(), ex_in=(), sh_in=(), tok_out=(), ex_out=(), gl_out=()):
    def seg(i):
        return jnp.minimum(i // tpe, n_ex - 1)

    in_specs, args = [], []
    for arr, w, cb, x_only in tok_in:
        if x_only:
            in_specs.append(pl.BlockSpec((tm, w), functools.partial(lambda i, cb: (jnp.minimum(i, nx_tiles - 1), cb), cb=cb)))
        else:
            in_specs.append(pl.BlockSpec((tm, w), functools.partial(lambda i, cb: (i, cb), cb=cb)))
        args.append(arr)
    for arr in ex_in:
        in_specs.append(pl.BlockSpec((1, 1, arr.shape[-1]), lambda i: (seg(i), 0, 0)))
        args.append(arr)
    for arr in sh_in:
        in_specs.append(pl.BlockSpec(arr.shape, functools.partial(lambda i, nd: (0,) * nd, nd=arr.ndim)))
        args.append(arr)
    out_specs, out_shape = [], []
    for w, dt in tok_out:
        out_specs.append(pl.BlockSpec((tm, w), lambda i: (i, 0)))
        out_shape.append(jax.ShapeDtypeStruct((n_tiles * tm, w), dt))
    for w in ex_out:
        out_specs.append(pl.BlockSpec((1, 1, w), lambda i: (seg(i), 0, 0)))
        out_shape.append(jax.ShapeDtypeStruct((n_ex, 1, w), f32))
    for r, w in gl_out:
        out_specs.append(pl.BlockSpec((r, w), lambda i: (0, 0)))
        out_shape.append(jax.ShapeDtypeStruct((r, w), f32))
    n_tok, n_exi, n_sh = len(tok_in), len(ex_in), len(sh_in)
    n_to, n_eo = len(tok_out), len(ex_out)
    x_only_flags = [t[3] for t in tok_in]

    def body(*refs):
        i = pl.program_id(0)
        ins, outs = refs[: n_tok + n_exi + n_sh], refs[n_tok + n_exi + n_sh:]
        is_x = i < nx_tiles
        tok_vals = []
        for r, xo in zip(ins[:n_tok], x_only_flags):
            v = r[...]
            tok_vals.append(jnp.where(is_x, v, jnp.zeros_like(v)) if xo else v)
        ex_vals = [r[0] for r in ins[n_tok:n_tok + n_exi]]
        sh_vals = [r[...] for r in ins[n_tok + n_exi:]]
        t_o, e_o, g_o = fn(tok_vals, ex_vals, sh_vals)
        for r, v in zip(outs[:n_to], t_o):
            r[...] = v.astype(r.dtype)
        first = jnp.logical_and(i % tpe == 0, i <= nx_tiles)
        for r, v in zip(outs[n_to:n_to + n_eo], e_o):
            @pl.when(first)
            def _(r=r, v=v):
                r[0] = v

            @pl.when(jnp.logical_not(first))
            def _(r=r, v=v):
                r[0] += v
        for r, v in zip(outs[n_to + n_eo:], g_o):
            @pl.when(i == 0)
            def _(r=r, v=v):
                r[...] = v

            @pl.when(i > 0)
            def _(r=r, v=v):
                r[...] += v

    res = _pc(body, name=name, grid=(n_tiles,), in_specs=in_specs, out_specs=out_specs, out_shape=out_shape,
              compiler_params=_params("arbitrary"))(*args)
    return list(res)


def _rms_mod(x, g, sh, sc):
    y = x * lax.rsqrt(jnp.mean(x * x, axis=-1, keepdims=True) + EPS) * g
    return y * (1.0 + sc) + sh


def _log_sigmoid(z):
    return jnp.minimum(z, 0.0) - jnp.log(1.0 + jnp.exp(-jnp.abs(z)))


def _swiglu(gu, F):
    a, b = gu[:, :F].astype(f32), gu[:, F:].astype(f32)
    return jax.nn.silu(a) * b


def _head_rms(o, DV):
    parts = []
    for h in range(HEADS):
        oh = o[:, h * DV:(h + 1) * DV]
        parts.append(oh * lax.rsqrt(jnp.mean(oh * oh, axis=-1, keepdims=True) + EPS))
    return jnp.concatenate(parts, axis=1)


@functools.partial(jax.custom_vjp, nondiff_argnums=(2,))
def _bdot(a, b, dims):
    return lax.dot_general(a.astype(bf16), b.astype(bf16), (((dims[0],), (dims[1],)), ((), ())), preferred_element_type=f32)


def _bdot_fwd(a, b, dims):
    return _bdot(a, b, dims), (a, b)


def _bdot_bwd(dims, res, g):
    a, b = res
    ca, cb = dims
    da = _bdot(g, b, (1, 1 - cb)) if ca == 1 else _bdot(b, g, (1 - cb, 1))
    db = _bdot(a, g, (1 - ca, 0)) if cb == 0 else _bdot(g, a, (0, 1 - ca))
    return da, db


_bdot.defvjp(_bdot_fwd, _bdot_bwd)


def _split_dot(m, x, dims):
    mb, rem, acc = m.astype(bf16), x, None
    for _ in range(3):
        piece = rem.astype(bf16)
        rem = rem - piece.astype(f32)
        part = lax.dot_general(mb, piece, (((dims[0],), (dims[1],)), ((), ())), preferred_element_type=f32)
        acc = part if acc is None else acc + part
    return acc


@jax.custom_vjp
def _tri_cumsum(tri, g):
    return _split_dot(tri, g, (1, 0))


def _tri_cumsum_fwd(tri, g):
    return _tri_cumsum(tri, g), tri


def _tri_cumsum_bwd(tri, db):
    return jnp.zeros_like(tri), _split_dot(tri, db, (0, 0))


_tri_cumsum.defvjp(_tri_cumsum_fwd, _tri_cumsum_bwd)


def _gla_chunk(St, q, k, v, g, *, rev, scale, exact):
    C, DK = q.shape
    r = lax.broadcasted_iota(jnp.int32, (C, C), 0)
    c = lax.broadcasted_iota(jnp.int32, (C, C), 1)
    causal = (r <= c) if rev else (r >= c)
    b = _tri_cumsum(causal.astype(f32), g)
    qs = q * scale
    qe = qs * jnp.exp(b)
    inter = _bdot(qe, St, (1, 1))
    b_last = b[0:1] if rev else b[C - 1:C]
    kd = k * jnp.exp(b_last - b)
    St_new = St * jnp.exp(b_last) + _bdot(v, kd, (0, 0))
    if not exact:
        att = jnp.where(causal, _bdot(qe, k * jnp.exp(-b), (1, 1)), 0.0)
        return St_new, inter + _bdot(att, v, (1, 0))
    rr = lax.broadcasted_iota(jnp.int32, (SUB, SUB, DK), 0)
    cc = lax.broadcasted_iota(jnp.int32, (SUB, SUB, DK), 1)
    m3 = (rr <= cc) if rev else (rr >= cc)
    outs = []
    for i in range(C // SUB):
        lo, hi = i * SUB, (i + 1) * SUB
        bi, qi, ki, vi = b[lo:hi], qs[lo:hi], k[lo:hi], v[lo:hi]
        rel = bi[:, None, :] - bi[None, :, :]
        e = jnp.where(m3, jnp.exp(jnp.where(m3, rel, 0.0)), 0.0)
        att = jnp.sum(qi[:, None, :] * e * ki[None, :, :], axis=-1)
        acc = _bdot(att, vi, (1, 0))
        ref_row = b[hi - 1:hi] if rev else b[lo:lo + 1]
        prev = slice(hi, C) if rev else slice(0, lo)
        if (hi < C) if rev else (lo > 0):
            qn = qi * jnp.exp(bi - ref_row)
            ks = k[prev] * jnp.exp(ref_row - b[prev])
            acc = acc + _bdot(_bdot(qn, ks, (1, 1)), v[prev], (1, 0))
        outs.append(acc)
    return St_new, inter + jnp.concatenate(outs, axis=0)


def _mild_decay(la_ref):
    return jnp.min(la_ref[...]) >= -GLA_SAFE_DECAY / CHUNK


def _gla_specs(D, rev_blocks, row0, seq):
    DK, DV = D // (2 * HEADS), D // HEADS
    nblk = seq // GLA_ROWS
    rb0 = row0 // GLA_ROWS

    def blk(j):
        return (nblk - 1 - j) if rev_blocks else j

    return DK, DV, nblk, rb0, blk


def _gla_in_specs(D, rev, rows):
    QK = D // 2
    return [
        pl.BlockSpec((GLA_ROWS, QK), lambda b, j: (rows(b, j), 6 * D // QK)),
        pl.BlockSpec((GLA_ROWS, QK), lambda b, j: (rows(b, j), 6 * D // QK + 1)),
        pl.BlockSpec((GLA_ROWS, D), lambda b, j: (rows(b, j), 2)),
        pl.BlockSpec((GLA_ROWS, QK), lambda b, j: (rows(b, j), 1 if rev else 0)),
    ]


def _gla_fwd(p_all, la_all, s0, *, rev, row0, nb, seq, D, name, carry=None):
    DK, DV, nblk, rb0, blk = _gla_specs(D, rev, row0, seq)
    cpb = GLA_ROWS // CHUNK

    def rows(b, j):
        return rb0 + b * nblk + blk(j)

    in_specs = _gla_in_specs(D, rev, rows) + [pl.BlockSpec((1, HEADS, DV, DK), lambda b, j: (b, 0, 0, 0))]
    out_specs = [
        pl.BlockSpec((GLA_ROWS, D), lambda b, j: (b * nblk + blk(j), 0)),
        pl.BlockSpec((1, HEADS, cpb, DV, DK), lambda b, j: (b, 0, blk(j), 0, 0)),
        pl.BlockSpec((1, HEADS, DV, DK), lambda b, j: (b, 0, 0, 0)),
    ]
    out_shape = [
        jax.ShapeDtypeStruct((nb * seq, D), f32),
        jax.ShapeDtypeStruct((nb, HEADS, seq // CHUNK, DV, DK), f32),
        jax.ShapeDtypeStruct((nb, HEADS, DV, DK), f32),
    ]
    chunk = functools.partial(_gla_chunk, rev=rev, scale=DK ** -0.5)

    def body(q_ref, k_ref, v_ref, la_ref, s0_ref, o_ref, hist_ref, sfin_ref, st_ref):
        j = pl.program_id(1)

        @pl.when(j == 0)
        def _():
            st_ref[...] = s0_ref[0]

        def step(ci, exact):
            cc = (cpb - 1 - ci) if rev else ci
            sl = pl.ds(cc * CHUNK, CHUNK)
            for h in range(HEADS):
                kq, kv = pl.ds(h * DK, DK), pl.ds(h * DV, DV)
                St = st_ref[h]
                hist_ref[0, h, cc] = St
                St2, o = chunk(St, q_ref[sl, kq].astype(f32), k_ref[sl, kq].astype(f32), v_ref[sl, kv].astype(f32), la_ref[sl, kq],
                               exact=exact)
                o_ref[sl, kv] = o
                st_ref[h] = St2

        mild = _mild_decay(la_ref)
        for exact in (False, True):
            @pl.when(jnp.logical_not(mild) if exact else mild)
            def _(exact=exact):
                for ci in range(cpb):
                    step(ci, exact)

        @pl.when(j == nblk - 1)
        def _():
            sfin_ref[0] = st_ref[...]

    res, carried = _call(body, name=name, grid=(nb, nblk), in_specs=in_specs, out_specs=out_specs, out_shape=out_shape,
                         scratch_shapes=[pltpu.VMEM((HEADS, DV, DK), f32)], sem=("parallel", "arbitrary"),
                         args=(p_all, p_all, p_all, la_all, s0), carry=carry)
    return res if carry is None else (res, carried)


def _gla_bwd(p_all, la_all, hist, do, dsfin, *, rev, row0, nb, seq, D, name):
    DK, DV, nblk, rb0, blk = _gla_specs(D, not rev, row0, seq)
    cpb = GLA_ROWS // CHUNK
    QK = HEADS * DK
    has_do = do is not None

    def rows(b, j):
        return rb0 + b * nblk + blk(j)

    in_specs = _gla_in_specs(D, rev, rows) + [
        pl.BlockSpec((1, HEADS, cpb, DV, DK), lambda b, j: (b, 0, blk(j), 0, 0)),
        pl.BlockSpec((1, HEADS, DV, DK), lambda b, j: (b, 0, 0, 0)),
    ]
    args = [p_all, p_all, p_all, la_all, hist, dsfin]
    if has_do:
        in_specs.append(pl.BlockSpec((GLA_ROWS, D), lambda b, j: (b * nblk + blk(j), 0)))
        args.append(do)
    out_specs = [
        pl.BlockSpec((GLA_ROWS, QK), lambda b, j: (b * nblk + blk(j), 0)),
        pl.BlockSpec((GLA_ROWS, QK), lambda b, j: (b * nblk + blk(j), 0)),
        pl.BlockSpec((GLA_ROWS, D), lambda b, j: (b * nblk + blk(j), 0)),
        pl.BlockSpec((GLA_ROWS, QK), lambda b, j: (b * nblk + blk(j), 0)),
        pl.BlockSpec((1, HEADS, DV, DK), lambda b, j: (b, 0, 0, 0)),
    ]
    out_shape = [
        jax.ShapeDtypeStruct((nb * seq, QK), f32), jax.ShapeDtypeStruct((nb * seq, QK), f32),
        jax.ShapeDtypeStruct((nb * seq, D), f32), jax.ShapeDtypeStruct((nb * seq, QK), f32),
        jax.ShapeDtypeStruct((nb, HEADS, DV, DK), f32),
    ]
    chunk = functools.partial(_gla_chunk, rev=rev, scale=DK ** -0.5)

    def body(*refs):
        if has_do:
            q_ref, k_ref, v_ref, la_ref, hist_ref, dsfin_ref, do_ref, dq_ref, dk_ref, dv_ref, dla_ref, ds0_ref, ds_ref = refs
        else:
            q_ref, k_ref, v_ref, la_ref, hist_ref, dsfin_ref, dq_ref, dk_ref, dv_ref, dla_ref, ds0_ref, ds_ref = refs
        j = pl.program_id(1)

        @pl.when(j == 0)
        def _():
            ds_ref[...] = dsfin_ref[0]

        def step(ci, exact):
            cc = ci if rev else (cpb - 1 - ci)
            sl = pl.ds(cc * CHUNK, CHUNK)
            for h in range(HEADS):
                kq, kv = pl.ds(h * DK, DK), pl.ds(h * DV, DV)
                prim = (hist_ref[0, h, cc], q_ref[sl, kq].astype(f32), k_ref[sl, kq].astype(f32), v_ref[sl, kv].astype(f32), la_ref[sl, kq])
                _, vjp = jax.vjp(functools.partial(chunk, exact=exact), *prim)
                d_o = do_ref[sl, kv] if has_do else jnp.zeros((CHUNK, DV), f32)
                dSt, dq, dk, dv, dg = vjp((ds_ref[h], d_o))
                dq_ref[sl, kq] = dq
                dk_ref[sl, kq] = dk
                dv_ref[sl, kv] = dv
                dla_ref[sl, kq] = dg
                ds_ref[h] = dSt

        mild = _mild_decay(la_ref)
        for exact in (False, True):
            @pl.when(jnp.logical_not(mild) if exact else mild)
            def _(exact=exact):
                for ci in range(cpb):
                    step(ci, exact)

        @pl.when(j == nblk - 1)
        def _():
            ds0_ref[0] = ds_ref[...]

    return _pc(body, name=name, grid=(nb, nblk), in_specs=in_specs, out_specs=out_specs, out_shape=out_shape,
               scratch_shapes=[pltpu.VMEM((HEADS, DV, DK), f32)], compiler_params=_params("parallel", "arbitrary"))(*args)


def _conv_fwd(p_all, dw_w, dw_b, *, B, L, D, name):
    ct = _pick(D, 256)
    nj = D // ct
    st = _pick(L, 128, 8)
    off = CONV_PAD - CONV_W // 2

    def body(a_ref, b_ref, w_ref, bias_ref, o_ref, zs_ref):
        _fill_shifted(zs_ref, L, lambda t0, n: a_ref[pl.ds(t0, n), :].astype(f32) * jax.nn.sigmoid(b_ref[pl.ds(t0, n), :].astype(f32)))
        for t0 in range(0, L, st):
            acc = jnp.zeros((st, ct), f32) + bias_ref[...]
            for k in range(CONV_W):
                acc = acc + w_ref[pl.ds(k, 1), :] * _window(zs_ref, t0 + k + off, st)
            o_ref[pl.ds(t0, st), :] = acc

    return _pc(
        body, name=name, grid=(B, nj),
        in_specs=[pl.BlockSpec((L, ct), lambda b, j: (b, j)), pl.BlockSpec((L, ct), lambda b, j: (b, nj + j)),
                  pl.BlockSpec((CONV_W, ct), lambda b, j: (0, j)), pl.BlockSpec((1, ct), lambda b, j: (0, j))],
        out_specs=pl.BlockSpec((L, ct), lambda b, j: (b, j)), out_shape=jax.ShapeDtypeStruct((B * L, D), f32),
        scratch_shapes=[pltpu.VMEM((SUBLANES, L + 2 * CONV_PAD, ct), f32)], compiler_params=_params("parallel", "parallel"),
    )(p_all, p_all, dw_w, dw_b)


def _fill_shifted(zs_ref, L, rows):
    lp = L + 2 * CONV_PAD
    ct = zs_ref.shape[2]
    step = 256
    zs_ref[0, pl.ds(0, CONV_PAD), :] = jnp.zeros((CONV_PAD, ct), f32)
    zs_ref[0, pl.ds(CONV_PAD + L, CONV_PAD), :] = jnp.zeros((CONV_PAD, ct), f32)
    for t0 in range(0, L, step):
        n = min(step, L - t0)
        zs_ref[0, pl.ds(CONV_PAD + t0, n), :] = rows(t0, n)
    for r in range(1, SUBLANES):
        for i0 in range(0, lp - SUBLANES, step):
            n = min(step, lp - SUBLANES - i0)
            zs_ref[r, pl.ds(i0, n), :] = zs_ref[0, pl.ds(i0 + r, n), :]


def _window(zs_ref, start, n):
    r = start % SUBLANES
    return zs_ref[r, pl.ds(start - r, n), :]


def _conv_bwd(p_all, dcz, dw_w, *, B, L, D, name, carry=None):
    ct = _pick(D, 128)
    nj = D // ct
    st = _pick(L, 256, 8)
    half = CONV_W // 2

    def body(a_ref, b_ref, dcz_ref, w_ref, da_ref, db_ref, ddw_ref, zs_ref, ds_ref):
        bi = pl.program_id(1)
        _fill_shifted(zs_ref, L, lambda t0, n: a_ref[pl.ds(t0, n), :].astype(f32) * jax.nn.sigmoid(b_ref[pl.ds(t0, n), :].astype(f32)))
        _fill_shifted(ds_ref, L, lambda t0, n: dcz_ref[pl.ds(t0, n), :])

        @pl.when(bi == 0)
        def _():
            ddw_ref[...] = jnp.zeros_like(ddw_ref)

        for t0 in range(0, L, st):
            acc = jnp.zeros((st, ct), f32)
            for k in range(CONV_W):
                acc = acc + w_ref[pl.ds(k, 1), :] * _window(ds_ref, t0 + CONV_PAD + half - k, st)
            a_t = a_ref[pl.ds(t0, st), :].astype(f32)
            sg_t = jax.nn.sigmoid(b_ref[pl.ds(t0, st), :].astype(f32))
            da_ref[pl.ds(t0, st), :] = (acc * sg_t).astype(bf16)
            db_ref[pl.ds(t0, st), :] = (acc * a_t * sg_t * (1.0 - sg_t)).astype(bf16)

        for k in range(CONV_W):
            part = jnp.zeros((SUBLANES, ct), f32)
            for t0 in range(0, L, st):
                prod = dcz_ref[pl.ds(t0, st), :] * _window(zs_ref, t0 + k + CONV_PAD - half, st)
                for i in range(0, st, SUBLANES):
                    part = part + prod[i:i + SUBLANES]
            ddw_ref[pl.ds(k, 1), :] += jnp.sum(part, axis=0, keepdims=True)

    res, carried = _call(
        body, name=name, grid=(nj, B),
        in_specs=[pl.BlockSpec((L, ct), lambda j, b: (b, j)), pl.BlockSpec((L, ct), lambda j, b: (b, nj + j)),
                  pl.BlockSpec((L, ct), lambda j, b: (b, j)), pl.BlockSpec((CONV_W, ct), lambda j, b: (0, j))],
        out_specs=[pl.BlockSpec((L, ct), lambda j, b: (b, j)), pl.BlockSpec((L, ct), lambda j, b: (b, j)),
                   pl.BlockSpec((2 * CONV_PAD, ct), lambda j, b: (0, j))],
        out_shape=[jax.ShapeDtypeStruct((B * L, D), bf16), jax.ShapeDtypeStruct((B * L, D), bf16),
                   jax.ShapeDtypeStruct((2 * CONV_PAD, D), f32)],
        scratch_shapes=[pltpu.VMEM((SUBLANES, L + 2 * CONV_PAD, ct), f32), pltpu.VMEM((SUBLANES, L + 2 * CONV_PAD, ct), f32)],
        sem=("parallel", "arbitrary"), args=(p_all, p_all, dcz, dw_w), carry=carry)
    return res if carry is None else (res, carried)


def _exchange(arrs, scatter, name):
    ex = _Exchange(arrs, scatter)
    n = ex.n

    def body(*refs):
        ex.start(refs[:n], refs[n:2 * n], refs[2 * n:])
        ex.finish(refs[:n], refs[n:2 * n], refs[2 * n:])

    res = _pc(body, name=name, in_specs=ex.specs, out_specs=ex.specs, out_shape=ex.out_shape, scratch_shapes=ex.scratch)(*arrs)
    return list(res)


class _Exchange:
    def __init__(self, arrs, scatter):
        self.arrs, self.scatter, self.n = list(arrs), scatter, len(arrs)
        self.out_shape = [jax.ShapeDtypeStruct(((N_DEV,) + a.shape[1:]) if scatter else ((N_DEV,) + a.shape), a.dtype) for a in arrs]
        self.specs = [pl.BlockSpec(memory_space=pl.ANY)] * self.n
        self.scratch = [pltpu.SemaphoreType.DMA((self.n, N_DEV - 1)), pltpu.SemaphoreType.DMA((self.n, N_DEV - 1)),
                        pltpu.SemaphoreType.DMA((self.n,))]

    def _copies(self, ins, outs, sems, landing):
        send_sems, recv_sems, local_sems = sems
        me = 4 * lax.axis_index("x") + 2 * lax.axis_index("y") + lax.axis_index("c")
        if landing:
            local = []
        else:
            local = [pltpu.make_async_copy(ins[a].at[me] if self.scatter else ins[a], outs[a].at[me], local_sems.at[a]) for a in range(self.n)]
        remote = []
        for k in range(1, N_DEV):
            p = (me + (N_DEV - k if landing else k)) % N_DEV
            for a in range(self.n):
                remote.append(pltpu.make_async_remote_copy(
                    src_ref=ins[a].at[p] if self.scatter else ins[a], dst_ref=outs[a].at[p if landing else me],
                    send_sem=send_sems.at[a, k - 1], recv_sem=recv_sems.at[a, k - 1],
                    device_id=(p // 4, (p // 2) % 2, p % 2), device_id_type=MESH))
        return local, remote

    def start(self, ins, outs, sems):
        local, sends = self._copies(ins, outs, sems, False)
        for cp in local + sends:
            cp.start()

    def finish(self, ins, outs, sems):
        for cp in self._copies(ins, outs, sems, True)[1]:
            cp.wait_recv()
        local, sends = self._copies(ins, outs, sems, False)
        for cp in sends:
            cp.wait_send()
        for cp in local:
            cp.wait()


def _carried(inner, n_in, n_out, grid, ex):
    n = ex.n

    def body(*refs):
        own_in, c_in = refs[:n_in], refs[n_in:n_in + n]
        own_out, c_out = refs[n_in + n:n_in + n + n_out], refs[n_in + n + n_out:n_in + 2 * n + n_out]
        rest = refs[n_in + 2 * n + n_out:]
        own_scr, sems = rest[:len(rest) - 3], rest[len(rest) - 3:]
        pids = [pl.program_id(d) for d in range(len(grid))]
        first = functools.reduce(jnp.logical_and, [p == 0 for p in pids])
        last = functools.reduce(jnp.logical_and, [p == g - 1 for p, g in zip(pids, grid)])

        @pl.when(first)
        def _():
            ex.start(c_in, c_out, sems)

        inner(*own_in, *own_out, *own_scr)

        @pl.when(last)
        def _():
            ex.finish(c_in, c_out, sems)

    return body


def _call(inner, *, name, grid, in_specs, out_specs, out_shape, scratch_shapes, sem, args, carry=None):
    if carry is None:
        res = _pc(inner, name=name, grid=grid, in_specs=in_specs, out_specs=out_specs, out_shape=out_shape,
                  scratch_shapes=scratch_shapes, compiler_params=_params(*sem))(*args)
        return list(res), None
    ex = _Exchange(*carry)
    res = _pc(_carried(inner, len(in_specs), len(out_specs), grid, ex), name=name, grid=grid,
              in_specs=list(in_specs) + ex.specs, out_specs=list(out_specs) + ex.specs, out_shape=list(out_shape) + ex.out_shape,
              scratch_shapes=list(scratch_shapes) + ex.scratch, compiler_params=_params(*(["arbitrary"] * len(grid))))(*args, *ex.arrs)
    res = list(res)
    return res[:len(out_specs)], res[len(out_specs):]


def _mod_fwd(c_all, c_ctx, w_loc, b_loc, name):
    nr, D = c_all.shape
    nc = w_loc.shape[1]

    def body(c_ref, cc_ref, w_ref, b_ref, o_ref):
        a = jnp.concatenate([c_ref[...], jnp.broadcast_to(cc_ref[...], (8, D))], axis=0)
        s = jax.nn.silu(a).astype(bf16)
        o_ref[...] = jnp.dot(s, w_ref[...].astype(bf16), preferred_element_type=f32) + b_ref[...]

    return _pc(body, name=name, out_shape=jax.ShapeDtypeStruct((nr + 8, nc), f32), compiler_params=_params())(c_all, c_ctx, w_loc, b_loc)


def _mod_bwd(c_all, c_ctx, w_loc, dmx_loc, dmc_loc, name):
    nr, D = c_all.shape
    nc = w_loc.shape[1]

    def body(c_ref, cc_ref, w_ref, dmx_ref, dmc_ref, gw_ref, gc_ref):
        cc = cc_ref[...]
        a = jnp.concatenate([c_ref[...], jnp.broadcast_to(cc, (N_DEV, D))], axis=0)
        s = jax.nn.silu(a).astype(bf16)
        g = jnp.concatenate([dmx_ref[...], dmc_ref[...]], axis=0).astype(bf16)
        gw_ref[...] = lax.dot_general(s, g, (((0,), (0,)), ((), ())), preferred_element_type=f32)
        dmc = jnp.sum(dmc_ref[...], axis=0, keepdims=True)
        ds = lax.dot_general(jnp.broadcast_to(dmc, (8, nc)).astype(bf16), w_ref[...].astype(bf16), (((1,), (1,)), ((), ())),
                             preferred_element_type=f32)[0:1]
        sg = jax.nn.sigmoid(cc)
        gc_ref[...] = ds * (sg * (1.0 + cc * (1.0 - sg)))

    return _pc(body, name=name, out_shape=[jax.ShapeDtypeStruct((D, nc), f32), jax.ShapeDtypeStruct((1, D), f32)],
               compiler_params=_params())(c_all, c_ctx, w_loc, dmx_loc, dmc_loc)


def _adamw_math(w, g, m, v):
    m2 = ADAM_B1 * m + (1.0 - ADAM_B1) * g
    v2 = ADAM_B2 * v + (1.0 - ADAM_B2) * jnp.square(g)
    m_hat = m2 / (1.0 - ADAM_B1 ** ADAM_STEP)
    v_hat = v2 / (1.0 - ADAM_B2 ** ADAM_STEP)
    delta = -ADAM_LR * (m_hat / (jnp.sqrt(v_hat) + ADAM_EPS) + ADAM_WD * w)
    return delta, m2, v2


def _adamw(w, m, v, g, name, partials):
    r, cdim = w.shape
    tr = _pick(r, 256, 8)

    def body(w_ref, m_ref, v_ref, g_ref, og_ref, od_ref, om_ref, ov_ref):
        if partials:
            g = g_ref[0].astype(f32)
            for s in range(1, N_DEV):
                g = g + g_ref[s].astype(f32)
        else:
            g = g_ref[...]
        d, m2, v2 = _adamw_math(w_ref[...], g, m_ref[...], v_ref[...])
        og_ref[...] = g
        od_ref[...] = d
        om_ref[...] = m2
        ov_ref[...] = v2

    blk = pl.BlockSpec((tr, cdim), lambda i: (i, 0))
    g_spec = pl.BlockSpec((N_DEV, tr, cdim), lambda i: (0, i, 0)) if partials else blk
    return _pc(body, name=name, grid=(r // tr,), in_specs=[blk, blk, blk, g_spec], out_specs=[blk] * 4,
               out_shape=[jax.ShapeDtypeStruct((r, cdim), f32)] * 4, compiler_params=_params("parallel"))(w, m, v, g)


def _sum_sources(parts, name):
    def body(*refs):
        for i_ref, o_ref in zip(refs[:len(parts)], refs[len(parts):]):
            acc = i_ref[0]
            for s in range(1, i_ref.shape[0]):
                acc = acc + i_ref[s]
            o_ref[...] = acc

    return list(_pc(body, name=name, out_shape=[jax.ShapeDtypeStruct(p.shape[1:], f32) for p in parts],
                    compiler_params=_params())(*parts))


def kernel(x, c, ctx, c_ctx, w_mod, b_mod, g_ffn1, w1_gu, w1_down, g_mix, w_in, dw_weight, dw_bias, conv_ln_g, conv_ln_b, w_conv_out, w_alpha_f, b_alpha_f, w_alpha_b, b_alpha_b, gla_norm_g, w_gla_out, w_out, g_ffn2, w2_gu, w2_down, g_final, loss_target, m_c_ctx, m_w_mod, m_b_mod, m_g_ffn1, m_w1_gu, m_w1_down, m_g_mix, m_w_in, m_dw_weight, m_dw_bias, m_conv_ln_g, m_conv_ln_b, m_w_conv_out, m_w_alpha_f, m_b_alpha_f, m_w_alpha_b, m_b_alpha_b, m_gla_norm_g, m_w_gla_out, m_w_out, m_g_ffn2, m_w2_gu, m_w2_down, m_g_final, v_c_ctx, v_w_mod, v_b_mod, v_g_ffn1, v_w1_gu, v_w1_down, v_g_mix, v_w_in, v_dw_weight, v_dw_bias, v_conv_ln_g, v_conv_ln_b, v_w_conv_out, v_w_alpha_f, v_b_alpha_f, v_w_alpha_b, v_b_alpha_b, v_gla_norm_g, v_w_gla_out, v_w_out, v_g_ffn2, v_w2_gu, v_w2_down, v_g_final):
    B, L, D = x.shape
    Lc = ctx.shape[1]
    T, Tc = B * L, B * Lc
    Tall = T + Tc
    F = w1_down.shape[1] * N_DEV
    DK, DV = D // (2 * HEADS), D // HEADS
    QK = HEADS * DK
    PW = 7 * D + LR_PAD
    tm = ROW_TILE
    tpe = L // tm
    nx, nall = T // tm, Tall // tm
    me = 4 * lax.axis_index("x") + 2 * lax.axis_index("y") + lax.axis_index("c")

    rw_all = dict(tm=tm, n_tiles=nall, tpe=tpe, nx_tiles=nx, n_ex=B + 1)
    rw_x = dict(tm=tm, n_tiles=nx, tpe=tpe, nx_tiles=nx, n_ex=B)
    rw_all_h = dict(tm=tm // 2, n_tiles=2 * nall, tpe=2 * tpe, nx_tiles=2 * nx, n_ex=B + 1)
    rw_x_h = dict(tm=tm // 2, n_tiles=2 * nx, tpe=2 * tpe, nx_tiles=2 * nx, n_ex=B)

    w1gu_g, dww_g, waf_g, wab_g, c_g = _exchange([w1_gu[0].astype(bf16), dw_weight[0], w_alpha_f[0], w_alpha_b[0], c], False, "gather_first")

    def cols(gat):
        return jnp.transpose(gat, (1, 0, 2)).reshape(gat.shape[1], N_DEV * gat.shape[2])

    def rows_(gat):
        return gat.reshape(N_DEV * gat.shape[1], gat.shape[2])

    W1gu = cols(w1gu_g)
    dww = cols(dww_g)
    WA = jnp.zeros((LR_PAD, 2 * QK), f32).at[:LOWRANK, :QK].set(cols(waf_g)).at[LOWRANK:2 * LOWRANK, QK:].set(cols(wab_g)).astype(bf16)
    BA = jnp.concatenate([b_alpha_f, b_alpha_b], axis=1)
    c_all = c_g.reshape(N_DEV * B, D)
    c_ctx2 = c_ctx.reshape(1, D)

    ncm = w_mod.shape[2]
    b_mod_loc = lax.dynamic_slice(b_mod, (0, me * ncm), (1, ncm))
    mod_loc = _mod_fwd(c_all, c_ctx2, w_mod[0], b_mod_loc, "mod_fwd")
    (mod_g,) = _exchange([mod_loc], False, "gather_mod")
    mod_full = cols(mod_g)
    mod_tab = jnp.concatenate([lax.dynamic_slice(mod_full, (me * B, 0), (B, N_MOD * D)), mod_full[N_DEV * B:N_DEV * B + 1]], axis=0)
    mods = [mod_tab[:, i * D:(i + 1) * D].reshape(B + 1, 1, D) for i in range(N_MOD)]
    mods_x = [mm[:B] for mm in mods]

    x_all = jnp.concatenate([x.reshape(T, D), ctx.reshape(Tc, D)], axis=0)

    def f_ffn_in(tok, ex, sh):
        return [_rms_mod(tok[0], sh[0], ex[0], ex[1])], [], []

    (u1,) = _rowwise(f_ffn_in, name="ffn1_in", tok_in=[(x_all, D, 0, False)], ex_in=[mods[0], mods[1]], sh_in=[g_ffn1],
                     tok_out=[(D, bf16)], **rw_all)
    gu1, (w1d_g, win_g) = _matmul(u1, W1gu, "nn", bf16, "ffn1_up", carry=([w1_down[0].astype(bf16), w_in[0].astype(bf16)], False))
    W1d = rows_(w1d_g)
    win = cols(win_g)
    o_q, o_k, o_v, o_og, o_af, o_ga, o_gb = 2 * D, 2 * D + QK, 2 * D + 2 * QK, 3 * D + 2 * QK, 4 * D + 2 * QK, 4 * D + 2 * QK + 2 * LOWRANK, 5 * D + 2 * QK + 2 * LOWRANK
    Win = jnp.concatenate([win[:, :o_q], win[:, o_v:o_og], win[:, o_og:o_af], win[:, o_ga:o_gb], win[:, o_gb:],
                           win[:, o_q:o_k], win[:, o_k:o_v], win[:, o_af:o_ga], jnp.zeros((D, LR_PAD - 2 * LOWRANK), bf16)], axis=1)

    def f_swiglu(tok, ex, sh):
        return [_swiglu(tok[0], F)], [], []

    (h1,) = _rowwise(f_swiglu, name="ffn1_act", tok_in=[(gu1, 2 * F, 0, False)], tok_out=[(F, bf16)], **rw_all_h)
    f1 = _matmul(h1, W1d, "nn", f32, "ffn1_down")

    def mix_in(xv, fv, gate, sh, sc, g):
        x1 = xv + 0.5 * gate * fv
        return x1, _rms_mod(x1, g, sh, sc)

    def f_mix_in(tok, ex, sh):
        return list(mix_in(tok[0], tok[1], ex[0], ex[1], ex[2], sh[0])), [], []

    x1, um = _rowwise(f_mix_in, name="mix_in", tok_in=[(x_all, D, 0, False), (f1, D, 0, False)], ex_in=[mods[2], mods[3], mods[4]],
                      sh_in=[g_mix], tok_out=[(D, f32), (D, bf16)], **rw_all)
    p_all, (wco_g, wgo_g, wo_g, w2gu_g) = _matmul(
        um, Win, "nn", bf16, "in_proj", tm_cap=512, tn_cap=2432,
        carry=([w_conv_out[0].astype(bf16), w_gla_out[0].astype(bf16), w_out[0].astype(bf16), w2_gu[0].astype(bf16)], False))
    Wco, Wgo, Wo, W2gu = rows_(wco_g), rows_(wgo_g), rows_(wo_g), cols(w2gu_g)

    def log_decay(lr, wa, ba):
        z = _bdot(lr, wa, (1, 0)) + ba
        return _log_sigmoid(z) / TAU

    def f_decay(tok, ex, sh):
        return [log_decay(tok[0], sh[0], sh[1])], [], []

    lr_blk = (p_all, LR_PAD, 7 * D // LR_PAD, False)
    (la_all,) = _rowwise(f_decay, name="log_decay", tok_in=[lr_blk], sh_in=[WA, BA], tok_out=[(2 * QK, f32)], **rw_all)

    zeros_s = jnp.zeros((B, HEADS, DV, DK), f32)
    gla_c = dict(row0=T, nb=B, seq=Lc, D=D)
    gla_x = dict(row0=0, nb=B, seq=L, D=D)
    _, hist_cf, s_f = _gla_fwd(p_all, la_all, zeros_s, rev=False, name="gla_ctx_f", **gla_c)
    _, hist_cb, s_b = _gla_fwd(p_all, la_all, zeros_s, rev=True, name="gla_ctx_b", **gla_c)
    (o_f, hist_f, _), (w2d_g,) = _gla_fwd(p_all, la_all, s_f, rev=False, name="gla_x_f", carry=([w2_down[0].astype(bf16)], False), **gla_x)
    W2d = rows_(w2d_g)
    o_b, hist_b, _ = _gla_fwd(p_all, la_all, s_b, rev=True, name="gla_x_b", **gla_x)

    cz = _conv_fwd(p_all, dww, dw_bias, B=B, L=L, D=D, name="conv_fwd")

    def ln_silu(z, g, b):
        mu = jnp.mean(z, axis=-1, keepdims=True)
        var = jnp.mean(jnp.square(z - mu), axis=-1, keepdims=True)
        return jax.nn.silu((z - mu) * lax.rsqrt(var + EPS) * g + b)

    def f_ln(tok, ex, sh):
        return [ln_silu(tok[0], sh[0], sh[1])], [], []

    (zc,) = _rowwise(f_ln, name="conv_ln", tok_in=[(cz, D, 0, False)], sh_in=[conv_ln_g, conv_ln_b], tok_out=[(D, bf16)], **rw_x)
    yc = _matmul(zc, Wco, "nn", bf16, "conv_out")

    def gla_out(of, ob, og, gn):
        return _head_rms(of + ob, DV) * gn * jax.nn.silu(og.astype(f32))

    def f_gla_out(tok, ex, sh):
        return [gla_out(tok[0], tok[1], tok[2], sh[0])], [], []

    og_blk = (p_all, D, 3, False)
    (og2,) = _rowwise(f_gla_out, name="gla_norm", tok_in=[(o_f, D, 0, False), (o_b, D, 0, False), og_blk], sh_in=[gla_norm_g],
                      tok_out=[(D, bf16)], **rw_x)
    yg = _matmul(og2, Wgo, "nn", bf16, "gla_out")

    def merge(ga, gb, ycv, ygv):
        return jax.nn.sigmoid(ga.astype(f32)) * ycv.astype(f32) + jax.nn.sigmoid(gb.astype(f32)) * ygv.astype(f32)

    def f_merge(tok, ex, sh):
        return [merge(*tok)], [], []

    ga_blk, gb_blk = (p_all, D, 4, False), (p_all, D, 5, False)
    (mg,) = _rowwise(f_merge, name="merge", tok_in=[ga_blk, gb_blk, (yc, D, 0, False), (yg, D, 0, False)], tok_out=[(D, bf16)], **rw_x)
    mix = _matmul(mg, Wo, "nn", f32, "mix_out")

    def ffn2_in(x1v, mixv, g5, sh, sc, g):
        x2 = x1v + g5 * mixv
        return x2, _rms_mod(x2, g, sh, sc)

    def f_ffn2_in(tok, ex, sh):
        return list(ffn2_in(tok[0], tok[1], ex[0], ex[1], ex[2], sh[0])), [], []

    x2, u2 = _rowwise(f_ffn2_in, name="ffn2_in", tok_in=[(x1, D, 0, False), (mix, D, 0, False)], ex_in=[mods_x[5], mods_x[6], mods_x[7]],
                      sh_in=[g_ffn2], tok_out=[(D, f32), (D, bf16)], **rw_x)
    gu2 = _matmul(u2, W2gu, "nn", bf16, "ffn2_up")
    (h2,) = _rowwise(f_swiglu, name="ffn2_act", tok_in=[(gu2, 2 * F, 0, False)], tok_out=[(F, bf16)], **rw_x_h)
    f2 = _matmul(h2, W2d, "nn", f32, "ffn2_down")

    gf2 = g_final.reshape(1, D)

    def head_loss(x2v, f2v, g8, gf, tgt):
        x3 = x2v + 0.5 * g8 * f2v
        y = x3 * lax.rsqrt(jnp.mean(x3 * x3, axis=-1, keepdims=True) + EPS) * gf
        return 0.5 * jnp.sum(jnp.mean(jnp.square(y - tgt), axis=-1))

    def f_head(tok, ex, sh):
        loss, vjp = jax.vjp(lambda a, b_, c_, d_: head_loss(a, b_, c_, d_, tok[2]), tok[0], tok[1], ex[0], sh[0])
        dx3, df2, dg8, dgf = vjp(jnp.ones((), f32))
        return [dx3, df2], [dg8], [dgf, jnp.broadcast_to(loss.reshape(1, 1), (1, 128))]

    dx3, df2, dg8, dgf, loss_p = _rowwise(
        f_head, name="head", tok_in=[(x2, D, 0, False), (f2, D, 0, False), (loss_target.reshape(T, D), D, 0, False)], ex_in=[mods_x[8]],
        sh_in=[gf2], tok_out=[(D, f32), (D, bf16)], ex_out=[D], gl_out=[(1, D), (1, 128)], **rw_x)

    dh2 = _matmul(df2, W2d, "nt", bf16, "ffn2_down_dx")
    gW2d = _matmul(h2, df2, "tn", f32, "ffn2_down_dw", tm_cap=1408)

    def f_swiglu_bwd(tok, ex, sh):
        _, vjp = jax.vjp(lambda gu: _swiglu(gu, F), tok[0].astype(f32))
        return [vjp(tok[1].astype(f32))[0]], [], []

    (dgu2,) = _rowwise(f_swiglu_bwd, name="ffn2_act_bwd", tok_in=[(gu2, 2 * F, 0, False), (dh2, F, 0, False)], tok_out=[(2 * F, bf16)], **rw_x_h)
    du2 = _matmul(dgu2, W2gu, "nt", f32, "ffn2_up_dx")
    gW2gu = _matmul(u2, dgu2, "tn", f32, "ffn2_up_dw")

    def f_ffn2_in_bwd(tok, ex, sh):
        _, vjp = jax.vjp(ffn2_in, tok[0], tok[1], ex[0], ex[1], ex[2], sh[0])
        dx2, dmix, dg5, dsh, dsc, dg = vjp((tok[3], tok[2]))
        return [dx2, dmix], [dg5, dsh, dsc], [dg]

    dx2, dmix, dg5, dsh6, dsc7, dg_ffn2 = _rowwise(
        f_ffn2_in_bwd, name="ffn2_in_bwd", tok_in=[(x1, D, 0, False), (mix, D, 0, False), (du2, D, 0, False), (dx3, D, 0, False)],
        ex_in=[mods_x[5], mods_x[6], mods_x[7]], sh_in=[g_ffn2], tok_out=[(D, f32), (D, bf16)], ex_out=[D, D, D], gl_out=[(1, D)], **rw_x)

    dmg = _matmul(dmix, Wo, "nt", bf16, "mix_out_dx")
    gWo = _matmul(mg, dmix, "tn", f32, "mix_out_dw")

    def f_merge_bwd(tok, ex, sh):
        _, vjp = jax.vjp(merge, *[t.astype(f32) for t in tok[:4]])
        dga, dgb, dyc, dyg = vjp(tok[4].astype(f32))
        return [dga, dgb, dyc, dyg], [], []

    dga, dgb, dyc, dyg = _rowwise(f_merge_bwd, name="merge_bwd",
                                  tok_in=[ga_blk, gb_blk, (yc, D, 0, False), (yg, D, 0, False), (dmg, D, 0, False)],
                                  tok_out=[(D, bf16)] * 4, **rw_x)
    dzc = _matmul(dyc, Wco, "nt", f32, "conv_out_dx")
    gWco = _matmul(zc, dyc, "tn", f32, "conv_out_dw")
    dog2 = _matmul(dyg, Wgo, "nt", f32, "gla_out_dx")
    gWgo = _matmul(og2, dyg, "tn", f32, "gla_out_dw")

    def f_ln_bwd(tok, ex, sh):
        _, vjp = jax.vjp(ln_silu, tok[0], sh[0], sh[1])
        dcz, dg, db = vjp(tok[1])
        return [dcz], [], [dg, db, jnp.sum(dcz, axis=0, keepdims=True)]

    dcz, g_ln_g, g_ln_b, g_dwb = _rowwise(f_ln_bwd, name="conv_ln_bwd", tok_in=[(cz, D, 0, False), (dzc, D, 0, False)],
                                          sh_in=[conv_ln_g, conv_ln_b], tok_out=[(D, f32)], gl_out=[(1, D)] * 3, **rw_x)
    def col_shards(g):
        return jnp.transpose(g.reshape(g.shape[0], N_DEV, g.shape[1] // N_DEV), (1, 0, 2)).astype(bf16)

    def row_shards(g):
        return g.reshape(N_DEV, g.shape[0] // N_DEV, g.shape[1]).astype(bf16)

    (dca, dcb, g_dww), (r_w2d, r_w2gu, r_wo, r_wco, r_wgo) = _conv_bwd(
        p_all, dcz, dww, B=B, L=L, D=D, name="conv_bwd",
        carry=([row_shards(gW2d), col_shards(gW2gu), row_shards(gWo), row_shards(gWco), row_shards(gWgo)], True))

    def f_gla_out_bwd(tok, ex, sh):
        _, vjp = jax.vjp(gla_out, tok[0], tok[1], tok[2].astype(f32), sh[0])
        dof, _, dog, dgn = vjp(tok[3])
        return [dof, dog], [], [dgn]

    d_o, dog, g_gn = _rowwise(f_gla_out_bwd, name="gla_norm_bwd",
                              tok_in=[(o_f, D, 0, False), (o_b, D, 0, False), og_blk, (dog2, D, 0, False)], sh_in=[gla_norm_g],
                              tok_out=[(D, f32), (D, bf16)], gl_out=[(1, D)], **rw_x)

    dq_f, dk_f, dv_f, dla_f, ds_f = _gla_bwd(p_all, la_all, hist_f, d_o, zeros_s, rev=False, name="gla_x_f_bwd", **gla_x)
    dq_b, dk_b, dv_b, dla_b, ds_b = _gla_bwd(p_all, la_all, hist_b, d_o, zeros_s, rev=True, name="gla_x_b_bwd", **gla_x)
    _, dk_cf, dv_cf, dla_cf, _ = _gla_bwd(p_all, la_all, hist_cf, None, ds_f, rev=False, name="gla_ctx_f_bwd", **gla_c)
    _, dk_cb, dv_cb, dla_cb, _ = _gla_bwd(p_all, la_all, hist_cb, None, ds_b, rev=True, name="gla_ctx_b_bwd", **gla_c)

    def f_add3(tok, ex, sh):
        return [tok[0] + tok[1], tok[2] + tok[3], tok[4] + tok[5]], [], []

    dq, dk, dv = _rowwise(f_add3, name="gla_sum_x", tok_in=[(t, t.shape[1], 0, False) for t in (dq_f, dq_b, dk_f, dk_b, dv_f, dv_b)],
                          tok_out=[(QK, bf16), (QK, bf16), (D, bf16)], **rw_x)

    def f_add2(tok, ex, sh):
        return [tok[0] + tok[1], tok[2] + tok[3]], [], []

    dk_c, dv_c = _rowwise(f_add2, name="gla_sum_ctx", tok_in=[(t, t.shape[1], 0, False) for t in (dk_cf, dk_cb, dv_cf, dv_cb)],
                          tok_out=[(QK, bf16), (D, bf16)], tm=tm, n_tiles=Tc // tm, tpe=tpe, nx_tiles=Tc // tm, n_ex=1)

    dla_all = jnp.concatenate([jnp.concatenate([dla_f, dla_b], axis=1), jnp.concatenate([dla_cf, dla_cb], axis=1)], axis=0)

    def f_decay_bwd(tok, ex, sh):
        _, vjp = jax.vjp(log_decay, tok[0].astype(f32), sh[0].astype(f32), sh[1])
        dlr, dwa, dba = vjp(tok[1])
        return [dlr], [], [dwa, dba]

    dlr, g_WA, g_BA = _rowwise(f_decay_bwd, name="log_decay_bwd", tok_in=[lr_blk, (dla_all, 2 * QK, 0, False)], sh_in=[WA, BA],
                               tok_out=[(LR_PAD, bf16)], gl_out=[(LR_PAD, 2 * QK), (1, 2 * QK)], **rw_all)

    zc_ = functools.partial(jnp.zeros, dtype=bf16)
    dp_x = jnp.concatenate([dca, dcb, dv, dog, dga, dgb, dq, dk, dlr[:T]], axis=1)
    dp_c = jnp.concatenate([zc_((Tc, 2 * D)), dv_c, zc_((Tc, 3 * D)), zc_((Tc, QK)), dk_c, dlr[T:]], axis=1)
    dp_all = jnp.concatenate([dp_x, dp_c], axis=0)
    gWin_p = _matmul(um, dp_all, "tn", f32, "in_proj_dw", tm_cap=512, tn_cap=2432)
    gWin = jnp.concatenate([gWin_p[:, :2 * D], gWin_p[:, 6 * D:7 * D], gWin_p[:, 2 * D:4 * D], gWin_p[:, 7 * D:7 * D + 2 * LOWRANK],
                            gWin_p[:, 4 * D:6 * D]], axis=1)
    dum, (r_win,) = _matmul(dp_all, Win, "nt", f32, "in_proj_dx", tk_cap=2432, carry=([col_shards(gWin)], True))

    def f_mix_in_bwd(tok, ex, sh):
        _, vjp = jax.vjp(mix_in, tok[0], tok[1], ex[0], ex[1], ex[2], sh[0])
        dx1, df1, dgate, dsh, dsc, dg = vjp((tok[3], tok[2]))
        return [dx1, df1], [dgate, dsh, dsc], [dg]

    dx1, df1, dg2, dsh3, dsc4, dg_mix = _rowwise(
        f_mix_in_bwd, name="mix_in_bwd", tok_in=[(x_all, D, 0, False), (f1, D, 0, False), (dum, D, 0, False), (dx2, D, 0, True)],
        ex_in=[mods[2], mods[3], mods[4]], sh_in=[g_mix], tok_out=[(D, f32), (D, bf16)], ex_out=[D, D, D], gl_out=[(1, D)], **rw_all)

    dh1 = _matmul(df1, W1d, "nt", bf16, "ffn1_down_dx")
    gW1d = _matmul(h1, df1, "tn", f32, "ffn1_down_dw", tm_cap=1408)
    (dgu1,) = _rowwise(f_swiglu_bwd, name="ffn1_act_bwd", tok_in=[(gu1, 2 * F, 0, False), (dh1, F, 0, False)], tok_out=[(2 * F, bf16)], **rw_all_h)
    gW1gu, (r_w1d,) = _matmul(u1, dgu1, "tn", f32, "ffn1_up_dw", carry=([row_shards(gW1d)], True))
    du1, (r_w1gu,) = _matmul(dgu1, W1gu, "nt", f32, "ffn1_up_dx", carry=([col_shards(gW1gu)], True))

    def f_ffn_in_bwd(tok, ex, sh):
        _, vjp = jax.vjp(_rms_mod, tok[0], sh[0], ex[0], ex[1])
        dx, dg, dsh, dsc = vjp(tok[1])
        return [dx + tok[2]], [dsh, dsc], [dg]

    dx_all, dsh0, dsc1, dg_ffn1 = _rowwise(
        f_ffn_in_bwd, name="ffn1_in_bwd", tok_in=[(x_all, D, 0, False), (du1, D, 0, False), (dx1, D, 0, False)],
        ex_in=[mods[0], mods[1]], sh_in=[g_ffn1], tok_out=[(D, f32)], ex_out=[D, D], gl_out=[(1, D)], **rw_all)
    grad_x = dx_all[:T].reshape(B, L, D)

    zrow = jnp.zeros((1, 1, D), f32)
    dmod_loc = jnp.concatenate([dsh0, dsc1, dg2, dsh3, dsc4] + [jnp.concatenate([t, zrow], axis=0) for t in (dg5, dsh6, dsc7, dg8)],
                               axis=2).reshape(B + 1, N_MOD * D)
    small = [loss_p, dg_ffn1, dg_mix, g_dww[:CONV_W].reshape(1, CONV_W * D), g_dwb, g_ln_g, g_ln_b,
             g_WA[:LOWRANK, :QK].reshape(1, LOWRANK * QK), g_BA[:, :QK], g_WA[LOWRANK:2 * LOWRANK, QK:].reshape(1, LOWRANK * QK), g_BA[:, QK:],
             g_gn, dg_ffn2, dgf]
    small_w = [s.shape[1] for s in small]
    def to8(v):
        n_pad = -(-v.shape[1] // 1024) * 1024
        return jnp.pad(v, ((0, 0), (0, n_pad - v.shape[1]))).reshape(8, n_pad // 8)

    def from8(a, n):
        return a.reshape(1, a.size)[:, :n]

    dmod_g, small_g = _exchange([dmod_loc, to8(jnp.concatenate(small, axis=1))], False, "gather_small")
    dmx = dmod_g[:, :B].reshape(N_DEV * B, N_MOD * D)
    dmc = dmod_g[:, B]
    gWmod, gcc_p = _mod_bwd(c_all, c_ctx2, w_mod[0], lax.dynamic_slice(dmx, (0, me * ncm), (N_DEV * B, ncm)),
                            lax.dynamic_slice(dmc, (0, me * ncm), (N_DEV, ncm)), "mod_bwd")

    rs_out = [r_w1gu, r_w1d, r_win, r_wco, r_wgo, r_wo, r_w2gu, r_w2d]
    (gcc_g,) = _exchange([to8(gcc_p)], False, "gather_cctx")

    sums, g_cc, g_bmod = _sum_sources([small_g, gcc_g, jnp.concatenate([dmx, dmc], axis=0).reshape(N_DEV * (B + 1), 8, N_MOD * D // 8)], "sum_small")
    sums, g_cc, g_bmod = from8(sums, sum(small_w)), from8(g_cc, D), from8(g_bmod, N_MOD * D)
    offs = [0]
    for wd in small_w:
        offs.append(offs[-1] + wd)
    sm = [sums[:, offs[i]:offs[i + 1]] for i in range(len(small))]
    loss = sm[0][0, 0]
    ncd, nca = dw_weight.shape[2], w_alpha_f.shape[2]
    g_dww_loc = lax.dynamic_slice(sm[3].reshape(CONV_W, D), (0, me * ncd), (CONV_W, ncd)).reshape(1, CONV_W * ncd)
    g_waf_loc = lax.dynamic_slice(sm[7].reshape(LOWRANK, QK), (0, me * nca), (LOWRANK, nca)).reshape(1, LOWRANK * nca)
    g_wab_loc = lax.dynamic_slice(sm[9].reshape(LOWRANK, QK), (0, me * nca), (LOWRANK, nca)).reshape(1, LOWRANK * nca)

    big = {}
    for nm, wv, mv, vv, part in (("w1_gu", w1_gu, m_w1_gu, v_w1_gu, rs_out[0]), ("w1_down", w1_down, m_w1_down, v_w1_down, rs_out[1]),
                                 ("w_in", w_in, m_w_in, v_w_in, rs_out[2]), ("w_conv_out", w_conv_out, m_w_conv_out, v_w_conv_out, rs_out[3]),
                                 ("w_gla_out", w_gla_out, m_w_gla_out, v_w_gla_out, rs_out[4]), ("w_out", w_out, m_w_out, v_w_out, rs_out[5]),
                                 ("w2_gu", w2_gu, m_w2_gu, v_w2_gu, rs_out[6]), ("w2_down", w2_down, m_w2_down, v_w2_down, rs_out[7])):
        big[nm] = [t[None] for t in _adamw(wv[0], mv[0], vv[0], part, "adamw_" + nm, True)]
    big["w_mod"] = [t[None] for t in _adamw(w_mod[0], m_w_mod[0], v_w_mod[0], gWmod, "adamw_w_mod", False)]

    small_params = [("c_ctx", c_ctx, m_c_ctx, v_c_ctx, g_cc), ("b_mod", b_mod, m_b_mod, v_b_mod, g_bmod), ("g_ffn1", g_ffn1, m_g_ffn1, v_g_ffn1, sm[1]),
                    ("g_mix", g_mix, m_g_mix, v_g_mix, sm[2]), ("dw_weight", dw_weight, m_dw_weight, v_dw_weight, g_dww_loc),
                    ("dw_bias", dw_bias, m_dw_bias, v_dw_bias, sm[4]), ("conv_ln_g", conv_ln_g, m_conv_ln_g, v_conv_ln_g, sm[5]),
                    ("conv_ln_b", conv_ln_b, m_conv_ln_b, v_conv_ln_b, sm[6]), ("w_alpha_f", w_alpha_f, m_w_alpha_f, v_w_alpha_f, g_waf_loc),
                    ("b_alpha_f", b_alpha_f, m_b_alpha_f, v_b_alpha_f, sm[8]), ("w_alpha_b", w_alpha_b, m_w_alpha_b, v_w_alpha_b, g_wab_loc),
                    ("b_alpha_b", b_alpha_b, m_b_alpha_b, v_b_alpha_b, sm[10]), ("gla_norm_g", gla_norm_g, m_gla_norm_g, v_gla_norm_g, sm[11]),
                    ("g_ffn2", g_ffn2, m_g_ffn2, v_g_ffn2, sm[12]), ("g_final", g_final, m_g_final, v_g_final, sm[13])]
    flat = lambda t: t.reshape(1, t.size)
    pw, pm, pv, pg = (jnp.concatenate([flat(sp[i]) for sp in small_params], axis=1) for i in (1, 2, 3, 4))
    n_small = pw.shape[1]
    s_g, s_d, s_m, s_v = (from8(t, n_small) for t in _adamw(to8(pw), to8(pm), to8(pv), to8(pg), "adamw_small", False))
    small_out, o0 = {}, 0
    for nm, wv, _, _, _ in small_params:
        small_out[nm] = [t[:, o0:o0 + wv.size].reshape(wv.shape) for t in (s_g, s_d, s_m, s_v)]
        o0 += wv.size

    order = ["c_ctx", "w_mod", "b_mod", "g_ffn1", "w1_gu", "w1_down", "g_mix", "w_in", "dw_weight", "dw_bias", "conv_ln_g", "conv_ln_b",
             "w_conv_out", "w_alpha_f", "b_alpha_f", "w_alpha_b", "b_alpha_b", "gla_norm_g", "w_gla_out", "w_out", "g_ffn2", "w2_gu",
             "w2_down", "g_final"]
    res = {**big, **small_out}
    return (loss, grad_x, *[res[n][0] for n in order], *[res[n][1] for n in order], *[res[n][2] for n in order], *[res[n][3] for n in order])
```

```python
import functools

import jax
import jax.numpy as jnp
from jax import lax
from jax.experimental import pallas as pl
from jax.experimental.pallas import tpu as pltpu

f32, bf16 = jnp.float32, jnp.bfloat16

N_DEV = 8
HEADS = 4
LOWRANK = 16
CONV_W = 31
CONV_PAD = 16
SUBLANES = 8
CHUNK = 64
SUB = 16
GLA_ROWS = 256
GLA_SAFE_DECAY = 60.0
TAU = 16.0
EPS = 1e-6
N_MOD = 9
LR_PAD = 128
ROW_TILE = 256
V7X_VMEM_BYTES = 64 << 20
VMEM_LIMIT = (V7X_VMEM_BYTES * 3) // 4

ADAM_LR, ADAM_B1, ADAM_B2, ADAM_EPS, ADAM_WD, ADAM_STEP = 0.001, 0.9, 0.999, 1e-08, 0.01, 10

MESH = pl.DeviceIdType.MESH


def _pc(body, **kw):
    return pl.pallas_call(body, **kw)


def _params(*sem):
    return pltpu.CompilerParams(dimension_semantics=sem, vmem_limit_bytes=VMEM_LIMIT)


def _pick(n, cap, unit=128):
    best = None
    for t in range(unit, min(n, cap) + 1, unit):
        if n % t == 0:
            best = t
    return best or n


def _matmul(a, b, mode, out_dtype, name, tm_cap=1024, tn_cap=1536, tk_cap=None, carry=None, halves=None):
    tk_cap = tk_cap or (1024 if mode == "tn" else 2816)
    if halves == "a":
        (_, M, Kh), N = a.shape, b.shape[0]
        K, tk = 2 * Kh, _pick(Kh, tk_cap)
        tm, tn = _pick(M, tm_cap), _pick(N, tn_cap)
        a_spec = pl.BlockSpec((None, tm, tk), lambda i, j, k: (k // (Kh // tk), i, k % (Kh // tk)))
    elif halves == "b":
        (K, M), (_, _, Nh) = a.shape, b.shape
        N, tn = 2 * Nh, _pick(Nh, tn_cap)
        tm, tk = _pick(M, tm_cap), _pick(K, tk_cap)
    else:
        if mode == "tn":
            (K, M), N = a.shape, b.shape[1]
        elif mode == "nt":
            (M, K), N = a.shape, b.shape[0]
        else:
            (M, K), N = a.shape, b.shape[1]
        tm, tn, tk = _pick(M, tm_cap), _pick(N, tn_cap), _pick(K, tk_cap)
    nk = K // tk
    if halves != "a":
        a_spec = pl.BlockSpec((tk, tm), lambda i, j, k: (k, i)) if mode == "tn" else pl.BlockSpec((tm, tk), lambda i, j, k: (i, k))
    if halves == "b":
        b_spec = pl.BlockSpec((None, tk, tn), lambda i, j, k: (j // (Nh // tn), k, j % (Nh // tn)))
    else:
        b_spec = pl.BlockSpec((tn, tk), lambda i, j, k: (j, k)) if mode == "nt" else pl.BlockSpec((tk, tn), lambda i, j, k: (k, j))
    dims = {"nn": ((1,), (0,)), "nt": ((1,), (1,)), "tn": ((0,), (0,))}[mode]

    def body_single(a_ref, b_ref, o_ref):
        o_ref[...] = lax.dot_general(a_ref[...].astype(bf16), b_ref[...].astype(bf16), (dims, ((), ())),
                                     preferred_element_type=f32).astype(out_dtype)

    def body(a_ref, b_ref, o_ref, acc_ref):
        k = pl.program_id(2)
        part = lax.dot_general(a_ref[...].astype(bf16), b_ref[...].astype(bf16), (dims, ((), ())), preferred_element_type=f32)

        @pl.when(k == 0)
        def _():
            acc_ref[...] = part

        @pl.when(k > 0)
        def _():
            acc_ref[...] += part

        @pl.when(k == nk - 1)
        def _():
            o_ref[...] = acc_ref[...].astype(out_dtype)

    (out,), carried = _call(
        body_single if nk == 1 else body, name=name, grid=(M // tm, N // tn, nk), in_specs=[a_spec, b_spec],
        out_specs=[pl.BlockSpec((tm, tn), lambda i, j, k: (i, j))], out_shape=[jax.ShapeDtypeStruct((M, N), out_dtype)],
        scratch_shapes=[] if nk == 1 else [pltpu.VMEM((tm, tn), f32)], sem=("parallel", "parallel", "arbitrary"),
        args=(a, b), carry=carry)
    return out if carry is None else (out, carried)


def _ffn_up(u, Wgu, name, carry=None):
    M, K = u.shape
    F = Wgu.shape[1] // 2
    tm, tn = _pick(M, 512), _pick(F, 1408)
    nj = F // tn

    def body(u_ref, wa_ref, wb_ref, gu_ref, h_ref):
        uv = u_ref[...]
        a = jnp.dot(uv, wa_ref[...], preferred_element_type=f32)
        b = jnp.dot(uv, wb_ref[...], preferred_element_type=f32)
        gu_ref[0] = a.astype(bf16)
        gu_ref[1] = b.astype(bf16)
        h_ref[...] = (jax.nn.silu(a) * b).astype(bf16)

    res, carried = _call(
        body, name=name, grid=(nj, M // tm),
        in_specs=[pl.BlockSpec((tm, K), lambda j, i: (i, 0)), pl.BlockSpec((K, tn), lambda j, i: (0, j)),
                  pl.BlockSpec((K, tn), lambda j, i: (0, nj + j))],
        out_specs=[pl.BlockSpec((2, tm, tn), lambda j, i: (0, i, j)), pl.BlockSpec((tm, tn), lambda j, i: (i, j))],
        out_shape=[jax.ShapeDtypeStruct((2, M, F), bf16), jax.ShapeDtypeStruct((M, F), bf16)],
        scratch_shapes=[], sem=("parallel", "parallel"), args=(u, Wgu, Wgu), carry=carry)
    return res if carry is None else (res, carried)


def _ffn_down_dx(df, Wd, gu, name):
    M, D = df.shape
    F = Wd.shape[0]
    tm, tn = _pick(M, 512), _pick(F, 1408)

    def body(df_ref, w_ref, gu_ref, o_ref):
        dh = lax.dot_general(df_ref[...], w_ref[...], (((1,), (1,)), ((), ())), preferred_element_type=f32)
        a, b = gu_ref[0].astype(f32), gu_ref[1].astype(f32)
        sg = jax.nn.sigmoid(a)
        o_ref[0] = (dh * b * sg * (1.0 + a * (1.0 - sg))).astype(bf16)
        o_ref[1] = (dh * a * sg).astype(bf16)

    return _pc(
        body, name=name, grid=(F // tn, M // tm),
        in_specs=[pl.BlockSpec((tm, D), lambda j, i: (i, 0)), pl.BlockSpec((tn, D), lambda j, i: (j, 0)),
                  pl.BlockSpec((2, tm, tn), lambda j, i: (0, i, j))],
        out_specs=pl.BlockSpec((2, tm, tn), lambda j, i: (0, i, j)), out_shape=jax.ShapeDtypeStruct((2, M, F), bf16),
        compiler_params=_params("parallel", "parallel"))(df, Wd, gu)


def _rowwise(fn, *, name, tm, n_tiles, tpe, nx_tiles, n_ex, tok_in=(), ex_in=(), sh_in=(), tok_out=(), ex_out=(), gl_out=()):
    def seg(i):
        return jnp.minimum(i // tpe, n_ex - 1)

    in_specs, args = [], []
    for arr, w, cb, x_only in tok_in:
        if x_only:
            in_specs.append(pl.BlockSpec((tm, w), functools.partial(lambda i, cb: (jnp.minimum(i, nx_tiles - 1), cb), cb=cb)))
        else:
            in_specs.append(pl.BlockSpec((tm, w), functools.partial(lambda i, cb: (i, cb), cb=cb)))
        args.append(arr)
    for arr in ex_in:
        in_specs.append(pl.BlockSpec((1, 1, arr.shape[-1]), lambda i: (seg(i), 0, 0)))
        args.append(arr)
    for arr in sh_in:
        in_specs.append(pl.BlockSpec(arr.shape, functools.partial(lambda i, nd: (0,) * nd, nd=arr.ndim)))
        args.append(arr)
    out_specs, out_shape = [], []
    for w, dt in tok_out:
        out_specs.append(pl.BlockSpec((tm, w), lambda i: (i, 0)))
        out_shape.append(jax.ShapeDtypeStruct((n_tiles * tm, w), dt))
    for w in ex_out:
        out_specs.append(pl.BlockSpec((1, 1, w), lambda i: (seg(i), 0, 0)))
        out_shape.append(jax.ShapeDtypeStruct((n_ex, 1, w), f32))
    for r, w in gl_out:
        out_specs.append(pl.BlockSpec((r, w), lambda i: (0, 0)))
        out_shape.append(jax.ShapeDtypeStruct((r, w), f32))
    n_tok, n_exi, n_sh = len(tok_in), len(ex_in), len(sh_in)
    n_to, n_eo = len(tok_out), len(ex_out)
    x_only_flags = [t[3] for t in tok_in]

    def body(*refs):
        i = pl.program_id(0)
        ins, outs = refs[: n_tok + n_exi + n_sh], refs[n_tok + n_exi + n_sh:]
        is_x = i < nx_tiles
        tok_vals = []
        for r, xo in zip(ins[:n_tok], x_only_flags):
            v = r[...]
            tok_vals.append(jnp.where(is_x, v, jnp.zeros_like(v)) if xo else v)
        ex_vals = [r[0] for r in ins[n_tok:n_tok + n_exi]]
        sh_vals = [r[...] for r in ins[n_tok + n_exi:]]
        t_o, e_o, g_o = fn(tok_vals, ex_vals, sh_vals)
        for r, v in zip(outs[:n_to], t_o):
            r[...] = v.astype(r.dtype)
        first = jnp.logical_and(i % tpe == 0, i <= nx_tiles)
        for r, v in zip(outs[n_to:n_to + n_eo], e_o):
            @pl.when(first)
            def _(r=r, v=v):
                r[0] = v

            @pl.when(jnp.logical_not(first))
            def _(r=r, v=v):
                r[0] += v
        for r, v in zip(outs[n_to + n_eo:], g_o):
            @pl.when(i == 0)
            def _(r=r, v=v):
                r[...] = v

            @pl.when(i > 0)
            def _(r=r, v=v):
                r[...] += v

    res = _pc(body, name=name, grid=(n_tiles,), in_specs=in_specs, out_specs=out_specs, out_shape=out_shape,
              compiler_params=_params("arbitrary"))(*args)
    return list(res)


def _rms_mod(x, g, sh, sc):
    y = x * lax.rsqrt(jnp.mean(x * x, axis=-1, keepdims=True) + EPS) * g
    return y * (1.0 + sc) + sh


def _log_sigmoid(z):
    return jnp.minimum(z, 0.0) - jnp.log(1.0 + jnp.exp(-jnp.abs(z)))


def _head_rms(o, DV):
    parts = []
    for h in range(HEADS):
        oh = o[:, h * DV:(h + 1) * DV]
        parts.append(oh * lax.rsqrt(jnp.mean(oh * oh, axis=-1, keepdims=True) + EPS))
    return jnp.concatenate(parts, axis=1)


@functools.partial(jax.custom_vjp, nondiff_argnums=(2,))
def _bdot(a, b, dims):
    return lax.dot_general(a.astype(bf16), b.astype(bf16), (((dims[0],), (dims[1],)), ((), ())), preferred_element_type=f32)


def _bdot_fwd(a, b, dims):
    return _bdot(a, b, dims), (a, b)


def _bdot_bwd(dims, res, g):
    a, b = res
    ca, cb = dims
    da = _bdot(g, b, (1, 1 - cb)) if ca == 1 else _bdot(b, g, (1 - cb, 1))
    db = _bdot(a, g, (1 - ca, 0)) if cb == 0 else _bdot(g, a, (0, 1 - ca))
    return da, db


_bdot.defvjp(_bdot_fwd, _bdot_bwd)


def _split_dot(m, x, dims):
    mb, rem, acc = m.astype(bf16), x, None
    for _ in range(3):
        piece = rem.astype(bf16)
        rem = rem - piece.astype(f32)
        part = lax.dot_general(mb, piece, (((dims[0],), (dims[1],)), ((), ())), preferred_element_type=f32)
        acc = part if acc is None else acc + part
    return acc


@jax.custom_vjp
def _tri_cumsum(tri, g):
    return _split_dot(tri, g, (1, 0))


def _tri_cumsum_fwd(tri, g):
    return _tri_cumsum(tri, g), tri


def _tri_cumsum_bwd(tri, db):
    return jnp.zeros_like(tri), _split_dot(tri, db, (0, 0))


_tri_cumsum.defvjp(_tri_cumsum_fwd, _tri_cumsum_bwd)


def _gla_chunk(St, q, k, v, g, *, rev, scale, exact):
    C, DK = q.shape
    r = lax.broadcasted_iota(jnp.int32, (C, C), 0)
    c = lax.broadcasted_iota(jnp.int32, (C, C), 1)
    causal = (r <= c) if rev else (r >= c)
    b = _tri_cumsum(causal.astype(f32), g)
    qs = q * scale
    qe = qs * jnp.exp(b)
    inter = _bdot(qe, St, (1, 1))
    b_last = b[0:1] if rev else b[C - 1:C]
    kd = k * jnp.exp(b_last - b)
    St_new = St * jnp.exp(b_last) + _bdot(v, kd, (0, 0))
    if not exact:
        att = jnp.where(causal, _bdot(qe, k * jnp.exp(-b), (1, 1)), 0.0)
        return St_new, inter + _bdot(att, v, (1, 0))
    rr = lax.broadcasted_iota(jnp.int32, (SUB, SUB, DK), 0)
    cc = lax.broadcasted_iota(jnp.int32, (SUB, SUB, DK), 1)
    m3 = (rr <= cc) if rev else (rr >= cc)
    outs = []
    for i in range(C // SUB):
        lo, hi = i * SUB, (i + 1) * SUB
        bi, qi, ki, vi = b[lo:hi], qs[lo:hi], k[lo:hi], v[lo:hi]
        rel = bi[:, None, :] - bi[None, :, :]
        e = jnp.where(m3, jnp.exp(jnp.where(m3, rel, 0.0)), 0.0)
        att = jnp.sum(qi[:, None, :] * e * ki[None, :, :], axis=-1)
        acc = _bdot(att, vi, (1, 0))
        ref_row = b[hi - 1:hi] if rev else b[lo:lo + 1]
        prev = slice(hi, C) if rev else slice(0, lo)
        if (hi < C) if rev else (lo > 0):
            qn = qi * jnp.exp(bi - ref_row)
            ks = k[prev] * jnp.exp(ref_row - b[prev])
            acc = acc + _bdot(_bdot(qn, ks, (1, 1)), v[prev], (1, 0))
        outs.append(acc)
    return St_new, inter + jnp.concatenate(outs, axis=0)


def _mild_decay(la_ref):
    return jnp.min(la_ref[...]) >= -GLA_SAFE_DECAY / CHUNK


def _gla_specs(D, rev_blocks, row0, seq):
    DK, DV = D // (2 * HEADS), D // HEADS
    nblk = seq // GLA_ROWS
    rb0 = row0 // GLA_ROWS

    def blk(j):
        return (nblk - 1 - j) if rev_blocks else j

    return DK, DV, nblk, rb0, blk


def _gla_in_specs(D, rev, rows):
    QK = D // 2
    return [
        pl.BlockSpec((GLA_ROWS, QK), lambda b, j: (rows(b, j), 6 * D // QK)),
        pl.BlockSpec((GLA_ROWS, QK), lambda b, j: (rows(b, j), 6 * D // QK + 1)),
        pl.BlockSpec((GLA_ROWS, D), lambda b, j: (rows(b, j), 2)),
        pl.BlockSpec((GLA_ROWS, QK), lambda b, j: (rows(b, j), 1 if rev else 0)),
    ]


def _gla_fwd(p_all, la_all, s0, *, rev, row0, nb, seq, D, name, carry=None):
    DK, DV, nblk, rb0, blk = _gla_specs(D, rev, row0, seq)
    cpb = GLA_ROWS // CHUNK

    def rows(b, j):
        return rb0 + b * nblk + blk(j)

    in_specs = _gla_in_specs(D, rev, rows) + [pl.BlockSpec((1, HEADS, DV, DK), lambda b, j: (b, 0, 0, 0))]
    out_specs = [
        pl.BlockSpec((GLA_ROWS, D), lambda b, j: (b * nblk + blk(j), 0)),
        pl.BlockSpec((1, HEADS, cpb, DV, DK), lambda b, j: (b, 0, blk(j), 0, 0)),
        pl.BlockSpec((1, HEADS, DV, DK), lambda b, j: (b, 0, 0, 0)),
    ]
    out_shape = [
        jax.ShapeDtypeStruct((nb * seq, D), f32),
        jax.ShapeDtypeStruct((nb, HEADS, seq // CHUNK, DV, DK), f32),
        jax.ShapeDtypeStruct((nb, HEADS, DV, DK), f32),
    ]
    chunk = functools.partial(_gla_chunk, rev=rev, scale=DK ** -0.5)

    def body(q_ref, k_ref, v_ref, la_ref, s0_ref, o_ref, hist_ref, sfin_ref, st_ref):
        j = pl.program_id(1)

        @pl.when(j == 0)
        def _():
            st_ref[...] = s0_ref[0]

        def step(ci, exact):
            cc = (cpb - 1 - ci) if rev else ci
            sl = pl.ds(cc * CHUNK, CHUNK)
            for h in range(HEADS):
                kq, kv = pl.ds(h * DK, DK), pl.ds(h * DV, DV)
                St = st_ref[h]
                hist_ref[0, h, cc] = St
                St2, o = chunk(St, q_ref[sl, kq].astype(f32), k_ref[sl, kq].astype(f32), v_ref[sl, kv].astype(f32), la_ref[sl, kq],
                               exact=exact)
                o_ref[sl, kv] = o
                st_ref[h] = St2

        mild = _mild_decay(la_ref)
        for exact in (False, True):
            @pl.when(jnp.logical_not(mild) if exact else mild)
            def _(exact=exact):
                for ci in range(cpb):
                    step(ci, exact)

        @pl.when(j == nblk - 1)
        def _():
            sfin_ref[0] = st_ref[...]

    res, carried = _call(body, name=name, grid=(nb, nblk), in_specs=in_specs, out_specs=out_specs, out_shape=out_shape,
                         scratch_shapes=[pltpu.VMEM((HEADS, DV, DK), f32)], sem=("parallel", "arbitrary"),
                         args=(p_all, p_all, p_all, la_all, s0), carry=carry)
    return res if carry is None else (res, carried)


def _gla_bwd(p_all, la_all, hist, do, dsfin, *, rev, row0, nb, seq, D, name, add=None):
    DK, DV, nblk, rb0, blk = _gla_specs(D, not rev, row0, seq)
    cpb = GLA_ROWS // CHUNK
    QK = HEADS * DK
    has_do = do is not None

    def rows(b, j):
        return rb0 + b * nblk + blk(j)

    in_specs = _gla_in_specs(D, rev, rows) + [
        pl.BlockSpec((1, HEADS, cpb, DV, DK), lambda b, j: (b, 0, blk(j), 0, 0)),
        pl.BlockSpec((1, HEADS, DV, DK), lambda b, j: (b, 0, 0, 0)),
    ]
    args = [p_all, p_all, p_all, la_all, hist, dsfin]
    if has_do:
        in_specs.append(pl.BlockSpec((GLA_ROWS, D), lambda b, j: (b * nblk + blk(j), 0)))
        args.append(do)
    if add is not None:
        in_specs += [pl.BlockSpec((GLA_ROWS, t.shape[1]), lambda b, j: (b * nblk + blk(j), 0)) for t in add]
        args += list(add)
    gdt = f32 if add is None else bf16
    out_specs = [
        pl.BlockSpec((GLA_ROWS, QK), lambda b, j: (b * nblk + blk(j), 0)),
        pl.BlockSpec((GLA_ROWS, QK), lambda b, j: (b * nblk + blk(j), 0)),
        pl.BlockSpec((GLA_ROWS, D), lambda b, j: (b * nblk + blk(j), 0)),
        pl.BlockSpec((GLA_ROWS, QK), lambda b, j: (b * nblk + blk(j), 0)),
        pl.BlockSpec((1, HEADS, DV, DK), lambda b, j: (b, 0, 0, 0)),
    ]
    out_shape = [
        jax.ShapeDtypeStruct((nb * seq, QK), gdt), jax.ShapeDtypeStruct((nb * seq, QK), gdt),
        jax.ShapeDtypeStruct((nb * seq, D), gdt), jax.ShapeDtypeStruct((nb * seq, QK), f32),
        jax.ShapeDtypeStruct((nb, HEADS, DV, DK), f32),
    ]
    chunk = functools.partial(_gla_chunk, rev=rev, scale=DK ** -0.5)

    def body(*refs):
        refs = list(refs)
        q_ref, k_ref, v_ref, la_ref, hist_ref, dsfin_ref = refs[:6]
        do_ref = refs[6] if has_do else None
        add_refs = refs[6 + has_do:len(refs) - 6]
        dq_ref, dk_ref, dv_ref, dla_ref, ds0_ref, ds_ref = refs[len(refs) - 6:]
        j = pl.program_id(1)

        @pl.when(j == 0)
        def _():
            ds_ref[...] = dsfin_ref[0]

        def step(ci, exact):
            cc = ci if rev else (cpb - 1 - ci)
            sl = pl.ds(cc * CHUNK, CHUNK)
            for h in range(HEADS):
                kq, kv = pl.ds(h * DK, DK), pl.ds(h * DV, DV)
                prim = (hist_ref[0, h, cc], q_ref[sl, kq].astype(f32), k_ref[sl, kq].astype(f32), v_ref[sl, kv].astype(f32), la_ref[sl, kq])
                _, vjp = jax.vjp(functools.partial(chunk, exact=exact), *prim)
                d_o = do_ref[sl, kv] if has_do else jnp.zeros((CHUNK, DV), f32)
                dSt, dq, dk, dv, dg = vjp((ds_ref[h], d_o))
                if add is not None:
                    dq, dk, dv = dq + add_refs[0][sl, kq], dk + add_refs[1][sl, kq], dv + add_refs[2][sl, kv]
                dq_ref[sl, kq] = dq.astype(gdt)
                dk_ref[sl, kq] = dk.astype(gdt)
                dv_ref[sl, kv] = dv.astype(gdt)
                dla_ref[sl, kq] = dg
                ds_ref[h] = dSt

        mild = _mild_decay(la_ref)
        for exact in (False, True):
            @pl.when(jnp.logical_not(mild) if exact else mild)
            def _(exact=exact):
                for ci in range(cpb):
                    step(ci, exact)

        @pl.when(j == nblk - 1)
        def _():
            ds0_ref[0] = ds_ref[...]

    return _pc(body, name=name, grid=(nb, nblk), in_specs=in_specs, out_specs=out_specs, out_shape=out_shape,
               scratch_shapes=[pltpu.VMEM((HEADS, DV, DK), f32)], compiler_params=_params("parallel", "arbitrary"))(*args)


def _conv_fwd(p_all, dw_w, dw_b, *, B, L, D, name):
    ct = _pick(D, 256)
    nj = D // ct
    st = _pick(L, 128, 8)
    off = CONV_PAD - CONV_W // 2

    def body(a_ref, b_ref, w_ref, bias_ref, o_ref, zs_ref):
        _fill_shifted(zs_ref, L, lambda t0, n: a_ref[pl.ds(t0, n), :].astype(f32) * jax.nn.sigmoid(b_ref[pl.ds(t0, n), :].astype(f32)))
        for t0 in range(0, L, st):
            acc = jnp.zeros((st, ct), f32) + bias_ref[...]
            for k in range(CONV_W):
                acc = acc + w_ref[pl.ds(k, 1), :] * _window(zs_ref, t0 + k + off, st)
            o_ref[pl.ds(t0, st), :] = acc

    return _pc(
        body, name=name, grid=(B, nj),
        in_specs=[pl.BlockSpec((L, ct), lambda b, j: (b, j)), pl.BlockSpec((L, ct), lambda b, j: (b, nj + j)),
                  pl.BlockSpec((CONV_W, ct), lambda b, j: (0, j)), pl.BlockSpec((1, ct), lambda b, j: (0, j))],
        out_specs=pl.BlockSpec((L, ct), lambda b, j: (b, j)), out_shape=jax.ShapeDtypeStruct((B * L, D), f32),
        scratch_shapes=[pltpu.VMEM((SUBLANES, L + 2 * CONV_PAD, ct), f32)], compiler_params=_params("parallel", "parallel"),
    )(p_all, p_all, dw_w, dw_b)


def _fill_shifted(zs_ref, L, rows):
    lp = L + 2 * CONV_PAD
    ct = zs_ref.shape[2]
    step = 256
    zs_ref[0, pl.ds(0, CONV_PAD), :] = jnp.zeros((CONV_PAD, ct), f32)
    zs_ref[0, pl.ds(CONV_PAD + L, CONV_PAD), :] = jnp.zeros((CONV_PAD, ct), f32)
    for t0 in range(0, L, step):
        n = min(step, L - t0)
        zs_ref[0, pl.ds(CONV_PAD + t0, n), :] = rows(t0, n)
    for r in range(1, SUBLANES):
        for i0 in range(0, lp - SUBLANES, step):
            n = min(step, lp - SUBLANES - i0)
            zs_ref[r, pl.ds(i0, n), :] = zs_ref[0, pl.ds(i0 + r, n), :]


def _window(zs_ref, start, n):
    r = start % SUBLANES
    return zs_ref[r, pl.ds(start - r, n), :]


def _conv_bwd(p_all, dcz, dw_w, *, B, L, D, name, carry=None):
    ct = _pick(D, 128)
    nj = D // ct
    st = _pick(L, 256, 8)
    half = CONV_W // 2

    def body(a_ref, b_ref, dcz_ref, w_ref, da_ref, db_ref, ddw_ref, zs_ref, ds_ref):
        bi = pl.program_id(1)
        _fill_shifted(zs_ref, L, lambda t0, n: a_ref[pl.ds(t0, n), :].astype(f32) * jax.nn.sigmoid(b_ref[pl.ds(t0, n), :].astype(f32)))
        _fill_shifted(ds_ref, L, lambda t0, n: dcz_ref[pl.ds(t0, n), :])

        @pl.when(bi == 0)
        def _():
            ddw_ref[...] = jnp.zeros_like(ddw_ref)

        for t0 in range(0, L, st):
            acc = jnp.zeros((st, ct), f32)
            for k in range(CONV_W):
                acc = acc + w_ref[pl.ds(k, 1), :] * _window(ds_ref, t0 + CONV_PAD + half - k, st)
            a_t = a_ref[pl.ds(t0, st), :].astype(f32)
            sg_t = jax.nn.sigmoid(b_ref[pl.ds(t0, st), :].astype(f32))
            da_ref[pl.ds(t0, st), :] = (acc * sg_t).astype(bf16)
            db_ref[pl.ds(t0, st), :] = (acc * a_t * sg_t * (1.0 - sg_t)).astype(bf16)

        for k in range(CONV_W):
            part = jnp.zeros((SUBLANES, ct), f32)
            for t0 in range(0, L, st):
                prod = dcz_ref[pl.ds(t0, st), :] * _window(zs_ref, t0 + k + CONV_PAD - half, st)
                for i in range(0, st, SUBLANES):
                    part = part + prod[i:i + SUBLANES]
            ddw_ref[pl.ds(k, 1), :] += jnp.sum(part, axis=0, keepdims=True)

    res, carried = _call(
        body, name=name, grid=(nj, B),
        in_specs=[pl.BlockSpec((L, ct), lambda j, b: (b, j)), pl.BlockSpec((L, ct), lambda j, b: (b, nj + j)),
                  pl.BlockSpec((L, ct), lambda j, b: (b, j)), pl.BlockSpec((CONV_W, ct), lambda j, b: (0, j))],
        out_specs=[pl.BlockSpec((L, ct), lambda j, b: (b, j)), pl.BlockSpec((L, ct), lambda j, b: (b, j)),
                   pl.BlockSpec((2 * CONV_PAD, ct), lambda j, b: (0, j))],
        out_shape=[jax.ShapeDtypeStruct((B * L, D), bf16), jax.ShapeDtypeStruct((B * L, D), bf16),
                   jax.ShapeDtypeStruct((2 * CONV_PAD, D), f32)],
        scratch_shapes=[pltpu.VMEM((SUBLANES, L + 2 * CONV_PAD, ct), f32), pltpu.VMEM((SUBLANES, L + 2 * CONV_PAD, ct), f32)],
        sem=("parallel", "arbitrary"), args=(p_all, p_all, dcz, dw_w), carry=carry)
    return res if carry is None else (res, carried)


def _exchange(arrs, scatter, name):
    ex = _Exchange(arrs, scatter)
    n = ex.n

    def body(*refs):
        ex.start(refs[:n], refs[n:2 * n], refs[2 * n:])
        ex.finish(refs[:n], refs[n:2 * n], refs[2 * n:])

    res = _pc(body, name=name, in_specs=ex.specs, out_specs=ex.specs, out_shape=ex.out_shape, scratch_shapes=ex.scratch)(*arrs)
    return list(res)


class _Exchange:
    def __init__(self, arrs, scatter):
        self.arrs, self.scatter, self.n = list(arrs), scatter, len(arrs)
        self.out_shape = [jax.ShapeDtypeStruct(((N_DEV,) + a.shape[1:]) if scatter else ((N_DEV,) + a.shape), a.dtype) for a in arrs]
        self.specs = [pl.BlockSpec(memory_space=pl.ANY)] * self.n
        self.scratch = [pltpu.SemaphoreType.DMA((self.n, N_DEV - 1)), pltpu.SemaphoreType.DMA((self.n, N_DEV - 1)),
                        pltpu.SemaphoreType.DMA((self.n,))]

    def _copies(self, ins, outs, sems, landing):
        send_sems, recv_sems, local_sems = sems
        me = 4 * lax.axis_index("x") + 2 * lax.axis_index("y") + lax.axis_index("c")
        if landing:
            local = []
        else:
            local = [pltpu.make_async_copy(ins[a].at[me] if self.scatter else ins[a], outs[a].at[me], local_sems.at[a]) for a in range(self.n)]
        remote = []
        for k in range(1, N_DEV):
            p = (me + (N_DEV - k if landing else k)) % N_DEV
            for a in range(self.n):
                remote.append(pltpu.make_async_remote_copy(
                    src_ref=ins[a].at[p] if self.scatter else ins[a], dst_ref=outs[a].at[p if landing else me],
                    send_sem=send_sems.at[a, k - 1], recv_sem=recv_sems.at[a, k - 1],
                    device_id=(p // 4, (p // 2) % 2, p % 2), device_id_type=MESH))
        return local, remote

    def start(self, ins, outs, sems):
        local, sends = self._copies(ins, outs, sems, False)
        for cp in local + sends:
            cp.start()

    def finish(self, ins, outs, sems):
        for cp in self._copies(ins, outs, sems, True)[1]:
            cp.wait_recv()
        local, sends = self._copies(ins, outs, sems, False)
        for cp in sends:
            cp.wait_send()
        for cp in local:
            cp.wait()


def _carried(inner, n_in, n_out, grid, ex):
    n = ex.n

    def body(*refs):
        own_in, c_in = refs[:n_in], refs[n_in:n_in + n]
        own_out, c_out = refs[n_in + n:n_in + n + n_out], refs[n_in + n + n_out:n_in + 2 * n + n_out]
        rest = refs[n_in + 2 * n + n_out:]
        own_scr, sems = rest[:len(rest) - 3], rest[len(rest) - 3:]
        pids = [pl.program_id(d) for d in range(len(grid))]
        first = functools.reduce(jnp.logical_and, [p == 0 for p in pids])
        last = functools.reduce(jnp.logical_and, [p == g - 1 for p, g in zip(pids, grid)])

        @pl.when(first)
        def _():
            ex.start(c_in, c_out, sems)

        inner(*own_in, *own_out, *own_scr)

        @pl.when(last)
        def _():
            ex.finish(c_in, c_out, sems)

    return body


def _call(inner, *, name, grid, in_specs, out_specs, out_shape, scratch_shapes, sem, args, carry=None):
    if carry is None:
        res = _pc(inner, name=name, grid=grid, in_specs=in_specs, out_specs=out_specs, out_shape=out_shape,
                  scratch_shapes=scratch_shapes, compiler_params=_params(*sem))(*args)
        return list(res), None
    ex = _Exchange(*carry)
    res = _pc(_carried(inner, len(in_specs), len(out_specs), grid, ex), name=name, grid=grid,
              in_specs=list(in_specs) + ex.specs, out_specs=list(out_specs) + ex.specs, out_shape=list(out_shape) + ex.out_shape,
              scratch_shapes=list(scratch_shapes) + ex.scratch, compiler_params=_params(*(["arbitrary"] * len(grid))))(*args, *ex.arrs)
    res = list(res)
    return res[:len(out_specs)], res[len(out_specs):]


def _mod_fwd(c_all, c_ctx, w_loc, b_loc, name):
    nr, D = c_all.shape
    nc = w_loc.shape[1]

    def body(c_ref, cc_ref, w_ref, b_ref, o_ref):
        a = jnp.concatenate([c_ref[...], jnp.broadcast_to(cc_ref[...], (8, D))], axis=0)
        s = jax.nn.silu(a).astype(bf16)
        o_ref[...] = jnp.dot(s, w_ref[...].astype(bf16), preferred_element_type=f32) + b_ref[...]

    return _pc(body, name=name, out_shape=jax.ShapeDtypeStruct((nr + 8, nc), f32), compiler_params=_params())(c_all, c_ctx, w_loc, b_loc)


def _mod_bwd(c_all, c_ctx, w_loc, dmx_loc, dmc_loc, name):
    nr, D = c_all.shape
    nc = w_loc.shape[1]

    def body(c_ref, cc_ref, w_ref, dmx_ref, dmc_ref, gw_ref, gc_ref):
        cc = cc_ref[...]
        a = jnp.concatenate([c_ref[...], jnp.broadcast_to(cc, (N_DEV, D))], axis=0)
        s = jax.nn.silu(a).astype(bf16)
        g = jnp.concatenate([dmx_ref[...], dmc_ref[...]], axis=0).astype(bf16)
        gw_ref[...] = lax.dot_general(s, g, (((0,), (0,)), ((), ())), preferred_element_type=f32)
        dmc = jnp.sum(dmc_ref[...], axis=0, keepdims=True)
        ds = lax.dot_general(jnp.broadcast_to(dmc, (8, nc)).astype(bf16), w_ref[...].astype(bf16), (((1,), (1,)), ((), ())),
                             preferred_element_type=f32)[0:1]
        sg = jax.nn.sigmoid(cc)
        gc_ref[...] = ds * (sg * (1.0 + cc * (1.0 - sg)))

    return _pc(body, name=name, out_shape=[jax.ShapeDtypeStruct((D, nc), f32), jax.ShapeDtypeStruct((1, D), f32)],
               compiler_params=_params())(c_all, c_ctx, w_loc, dmx_loc, dmc_loc)


def _adamw_math(w, g, m, v):
    m2 = ADAM_B1 * m + (1.0 - ADAM_B1) * g
    v2 = ADAM_B2 * v + (1.0 - ADAM_B2) * jnp.square(g)
    m_hat = m2 / (1.0 - ADAM_B1 ** ADAM_STEP)
    v_hat = v2 / (1.0 - ADAM_B2 ** ADAM_STEP)
    delta = -ADAM_LR * (m_hat / (jnp.sqrt(v_hat) + ADAM_EPS) + ADAM_WD * w)
    return delta, m2, v2


def _adamw(w, m, v, g, name, partials):
    r, cdim = w.shape
    tr = _pick(r, 256, 8)

    def body(w_ref, m_ref, v_ref, g_ref, og_ref, od_ref, om_ref, ov_ref):
        if partials:
            g = g_ref[0].astype(f32)
            for s in range(1, N_DEV):
                g = g + g_ref[s].astype(f32)
        else:
            g = g_ref[...]
        d, m2, v2 = _adamw_math(w_ref[...], g, m_ref[...], v_ref[...])
        og_ref[...] = g
        od_ref[...] = d
        om_ref[...] = m2
        ov_ref[...] = v2

    blk = pl.BlockSpec((tr, cdim), lambda i: (i, 0))
    g_spec = pl.BlockSpec((N_DEV, tr, cdim), lambda i: (0, i, 0)) if partials else blk
    return _pc(body, name=name, grid=(r // tr,), in_specs=[blk, blk, blk, g_spec], out_specs=[blk] * 4,
               out_shape=[jax.ShapeDtypeStruct((r, cdim), f32)] * 4, compiler_params=_params("parallel"))(w, m, v, g)


def _sum_sources(parts, name):
    def body(*refs):
        for i_ref, o_ref in zip(refs[:len(parts)], refs[len(parts):]):
            acc = i_ref[0]
            for s in range(1, i_ref.shape[0]):
                acc = acc + i_ref[s]
            o_ref[...] = acc

    return list(_pc(body, name=name, out_shape=[jax.ShapeDtypeStruct(p.shape[1:], f32) for p in parts],
                    compiler_params=_params())(*parts))


def kernel(x, c, ctx, c_ctx, w_mod, b_mod, g_ffn1, w1_gu, w1_down, g_mix, w_in, dw_weight, dw_bias, conv_ln_g, conv_ln_b, w_conv_out, w_alpha_f, b_alpha_f, w_alpha_b, b_alpha_b, gla_norm_g, w_gla_out, w_out, g_ffn2, w2_gu, w2_down, g_final, loss_target, m_c_ctx, m_w_mod, m_b_mod, m_g_ffn1, m_w1_gu, m_w1_down, m_g_mix, m_w_in, m_dw_weight, m_dw_bias, m_conv_ln_g, m_conv_ln_b, m_w_conv_out, m_w_alpha_f, m_b_alpha_f, m_w_alpha_b, m_b_alpha_b, m_gla_norm_g, m_w_gla_out, m_w_out, m_g_ffn2, m_w2_gu, m_w2_down, m_g_final, v_c_ctx, v_w_mod, v_b_mod, v_g_ffn1, v_w1_gu, v_w1_down, v_g_mix, v_w_in, v_dw_weight, v_dw_bias, v_conv_ln_g, v_conv_ln_b, v_w_conv_out, v_w_alpha_f, v_b_alpha_f, v_w_alpha_b, v_b_alpha_b, v_gla_norm_g, v_w_gla_out, v_w_out, v_g_ffn2, v_w2_gu, v_w2_down, v_g_final):
    B, L, D = x.shape
    Lc = ctx.shape[1]
    T, Tc = B * L, B * Lc
    Tall = T + Tc
    F = w1_down.shape[1] * N_DEV
    DK, DV = D // (2 * HEADS), D // HEADS
    QK = HEADS * DK
    PW = 7 * D + LR_PAD
    tm = ROW_TILE
    tpe = L // tm
    nx, nall = T // tm, Tall // tm
    me = 4 * lax.axis_index("x") + 2 * lax.axis_index("y") + lax.axis_index("c")

    rw_all = dict(tm=tm, n_tiles=nall, tpe=tpe, nx_tiles=nx, n_ex=B + 1)
    rw_x = dict(tm=tm, n_tiles=nx, tpe=tpe, nx_tiles=nx, n_ex=B)

    w1gu_g, dww_g, waf_g, wab_g, c_g = _exchange([w1_gu[0].astype(bf16), dw_weight[0], w_alpha_f[0], w_alpha_b[0], c], False, "gather_first")

    def cols(gat):
        return jnp.transpose(gat, (1, 0, 2)).reshape(gat.shape[1], N_DEV * gat.shape[2])

    def rows_(gat):
        return gat.reshape(N_DEV * gat.shape[1], gat.shape[2])

    W1gu = cols(w1gu_g)
    dww = cols(dww_g)
    WA = jnp.zeros((LR_PAD, 2 * QK), f32).at[:LOWRANK, :QK].set(cols(waf_g)).at[LOWRANK:2 * LOWRANK, QK:].set(cols(wab_g)).astype(bf16)
    BA = jnp.concatenate([b_alpha_f, b_alpha_b], axis=1)
    c_all = c_g.reshape(N_DEV * B, D)
    c_ctx2 = c_ctx.reshape(1, D)

    ncm = w_mod.shape[2]
    b_mod_loc = lax.dynamic_slice(b_mod, (0, me * ncm), (1, ncm))
    mod_loc = _mod_fwd(c_all, c_ctx2, w_mod[0], b_mod_loc, "mod_fwd")
    (mod_g,) = _exchange([mod_loc], False, "gather_mod")
    mod_full = cols(mod_g)
    mod_tab = jnp.concatenate([lax.dynamic_slice(mod_full, (me * B, 0), (B, N_MOD * D)), mod_full[N_DEV * B:N_DEV * B + 1]], axis=0)
    mods = [mod_tab[:, i * D:(i + 1) * D].reshape(B + 1, 1, D) for i in range(N_MOD)]
    mods_x = [mm[:B] for mm in mods]

    x_all = jnp.concatenate([x.reshape(T, D), ctx.reshape(Tc, D)], axis=0)

    def f_ffn_in(tok, ex, sh):
        return [_rms_mod(tok[0], sh[0], ex[0], ex[1])], [], []

    (u1,) = _rowwise(f_ffn_in, name="ffn1_in", tok_in=[(x_all, D, 0, False)], ex_in=[mods[0], mods[1]], sh_in=[g_ffn1],
                     tok_out=[(D, bf16)], **rw_all)
    (gu1, h1), (w1d_g, win_g) = _ffn_up(u1, W1gu, "ffn1_up", carry=([w1_down[0].astype(bf16), w_in[0].astype(bf16)], False))
    W1d = rows_(w1d_g)
    lr2 = 2 * LOWRANK
    segs = [(0, 2 * D, 0), (2 * D, 2 * D + QK, 6 * D), (2 * D + QK, 3 * D, 6 * D + QK), (3 * D, 4 * D, 2 * D), (4 * D, 5 * D, 3 * D),
            (5 * D, 5 * D + lr2, 7 * D), (5 * D + lr2, 6 * D + lr2, 4 * D), (6 * D + lr2, 7 * D + lr2, 5 * D)]
    wc = w_in.shape[2]
    win_parts = []
    for lo, hi, _ in sorted(segs, key=lambda t: t[2]):
        for d in range(N_DEV):
            a0, a1 = max(lo, d * wc), min(hi, (d + 1) * wc)
            if a0 < a1:
                win_parts.append(win_g[d][:, a0 - d * wc:a1 - d * wc])
    Win = jnp.concatenate(win_parts + [jnp.zeros((D, LR_PAD - lr2), bf16)], axis=1)
    f1 = _matmul(h1, W1d, "nn", f32, "ffn1_down")

    def mix_in(xv, fv, gate, sh, sc, g):
        x1 = xv + 0.5 * gate * fv
        return x1, _rms_mod(x1, g, sh, sc)

    def f_mix_in(tok, ex, sh):
        return list(mix_in(tok[0], tok[1], ex[0], ex[1], ex[2], sh[0])), [], []

    x1, um = _rowwise(f_mix_in, name="mix_in", tok_in=[(x_all, D, 0, False), (f1, D, 0, False)], ex_in=[mods[2], mods[3], mods[4]],
                      sh_in=[g_mix], tok_out=[(D, f32), (D, bf16)], **rw_all)
    p_all, (wco_g, wgo_g, wo_g, w2gu_g) = _matmul(
        um, Win, "nn", bf16, "in_proj", tm_cap=512, tn_cap=2432,
        carry=([w_conv_out[0].astype(bf16), w_gla_out[0].astype(bf16), w_out[0].astype(bf16), w2_gu[0].astype(bf16)], False))
    Wco, Wgo, Wo, W2gu = rows_(wco_g), rows_(wgo_g), rows_(wo_g), cols(w2gu_g)

    def log_decay(lr, wa, ba):
        z = _bdot(lr, wa, (1, 0)) + ba
        return _log_sigmoid(z) / TAU

    def f_decay(tok, ex, sh):
        return [log_decay(tok[0], sh[0], sh[1])], [], []

    lr_blk = (p_all, LR_PAD, 7 * D // LR_PAD, False)
    (la_all,) = _rowwise(f_decay, name="log_decay", tok_in=[lr_blk], sh_in=[WA, BA], tok_out=[(2 * QK, f32)], **rw_all)

    zeros_s = jnp.zeros((B, HEADS, DV, DK), f32)
    gla_c = dict(row0=T, nb=B, seq=Lc, D=D)
    gla_x = dict(row0=0, nb=B, seq=L, D=D)
    _, hist_cf, s_f = _gla_fwd(p_all, la_all, zeros_s, rev=False, name="gla_ctx_f", **gla_c)
    _, hist_cb, s_b = _gla_fwd(p_all, la_all, zeros_s, rev=True, name="gla_ctx_b", **gla_c)
    (o_f, hist_f, _), (w2d_g,) = _gla_fwd(p_all, la_all, s_f, rev=False, name="gla_x_f", carry=([w2_down[0].astype(bf16)], False), **gla_x)
    W2d = rows_(w2d_g)
    o_b, hist_b, _ = _gla_fwd(p_all, la_all, s_b, rev=True, name="gla_x_b", **gla_x)

    cz = _conv_fwd(p_all, dww, dw_bias, B=B, L=L, D=D, name="conv_fwd")

    def ln_silu(z, g, b):
        mu = jnp.mean(z, axis=-1, keepdims=True)
        var = jnp.mean(jnp.square(z - mu), axis=-1, keepdims=True)
        return jax.nn.silu((z - mu) * lax.rsqrt(var + EPS) * g + b)

    def f_ln(tok, ex, sh):
        return [ln_silu(tok[0], sh[0], sh[1])], [], []

    (zc,) = _rowwise(f_ln, name="conv_ln", tok_in=[(cz, D, 0, False)], sh_in=[conv_ln_g, conv_ln_b], tok_out=[(D, bf16)], **rw_x)
    yc = _matmul(zc, Wco, "nn", bf16, "conv_out")

    def gla_out(of, ob, og, gn):
        return _head_rms(of + ob, DV) * gn * jax.nn.silu(og.astype(f32))

    def f_gla_out(tok, ex, sh):
        return [gla_out(tok[0], tok[1], tok[2], sh[0])], [], []

    og_blk = (p_all, D, 3, False)
    (og2,) = _rowwise(f_gla_out, name="gla_norm", tok_in=[(o_f, D, 0, False), (o_b, D, 0, False), og_blk], sh_in=[gla_norm_g],
                      tok_out=[(D, bf16)], **rw_x)
    yg = _matmul(og2, Wgo, "nn", bf16, "gla_out")

    def merge(ga, gb, ycv, ygv):
        return jax.nn.sigmoid(ga.astype(f32)) * ycv.astype(f32) + jax.nn.sigmoid(gb.astype(f32)) * ygv.astype(f32)

    def f_merge(tok, ex, sh):
        return [merge(*tok)], [], []

    ga_blk, gb_blk = (p_all, D, 4, False), (p_all, D, 5, False)
    (mg,) = _rowwise(f_merge, name="merge", tok_in=[ga_blk, gb_blk, (yc, D, 0, False), (yg, D, 0, False)], tok_out=[(D, bf16)], **rw_x)
    mix = _matmul(mg, Wo, "nn", f32, "mix_out")

    def ffn2_in(x1v, mixv, g5, sh, sc, g):
        x2 = x1v + g5 * mixv
        return x2, _rms_mod(x2, g, sh, sc)

    def f_ffn2_in(tok, ex, sh):
        return list(ffn2_in(tok[0], tok[1], ex[0], ex[1], ex[2], sh[0])), [], []

    x2, u2 = _rowwise(f_ffn2_in, name="ffn2_in", tok_in=[(x1, D, 0, False), (mix, D, 0, False)], ex_in=[mods_x[5], mods_x[6], mods_x[7]],
                      sh_in=[g_ffn2], tok_out=[(D, f32), (D, bf16)], **rw_x)
    gu2, h2 = _ffn_up(u2, W2gu, "ffn2_up")
    f2 = _matmul(h2, W2d, "nn", f32, "ffn2_down")

    gf2 = g_final.reshape(1, D)

    def head_loss(x2v, f2v, g8, gf, tgt):
        x3 = x2v + 0.5 * g8 * f2v
        y = x3 * lax.rsqrt(jnp.mean(x3 * x3, axis=-1, keepdims=True) + EPS) * gf
        return 0.5 * jnp.sum(jnp.mean(jnp.square(y - tgt), axis=-1))

    def f_head(tok, ex, sh):
        loss, vjp = jax.vjp(lambda a, b_, c_, d_: head_loss(a, b_, c_, d_, tok[2]), tok[0], tok[1], ex[0], sh[0])
        dx3, df2, dg8, dgf = vjp(jnp.ones((), f32))
        return [dx3, df2], [dg8], [dgf, jnp.broadcast_to(loss.reshape(1, 1), (1, 128))]

    dx3, df2, dg8, dgf, loss_p = _rowwise(
        f_head, name="head", tok_in=[(x2, D, 0, False), (f2, D, 0, False), (loss_target.reshape(T, D), D, 0, False)], ex_in=[mods_x[8]],
        sh_in=[gf2], tok_out=[(D, f32), (D, bf16)], ex_out=[D], gl_out=[(1, D), (1, 128)], **rw_x)

    dgu2 = _ffn_down_dx(df2, W2d, gu2, "ffn2_down_dx")
    gW2d = _matmul(h2, df2, "tn", f32, "ffn2_down_dw", tm_cap=1408)
    du2 = _matmul(dgu2, W2gu, "nt", f32, "ffn2_up_dx", halves="a")
    gW2gu = _matmul(u2, dgu2, "tn", f32, "ffn2_up_dw", halves="b")

    def f_ffn2_in_bwd(tok, ex, sh):
        _, vjp = jax.vjp(ffn2_in, tok[0], tok[1], ex[0], ex[1], ex[2], sh[0])
        dx2, dmix, dg5, dsh, dsc, dg = vjp((tok[3], tok[2]))
        return [dx2, dmix], [dg5, dsh, dsc], [dg]

    dx2, dmix, dg5, dsh6, dsc7, dg_ffn2 = _rowwise(
        f_ffn2_in_bwd, name="ffn2_in_bwd", tok_in=[(x1, D, 0, False), (mix, D, 0, False), (du2, D, 0, False), (dx3, D, 0, False)],
        ex_in=[mods_x[5], mods_x[6], mods_x[7]], sh_in=[g_ffn2], tok_out=[(D, f32), (D, bf16)], ex_out=[D, D, D], gl_out=[(1, D)], **rw_x)

    dmg = _matmul(dmix, Wo, "nt", bf16, "mix_out_dx")
    gWo = _matmul(mg, dmix, "tn", f32, "mix_out_dw")

    def f_merge_bwd(tok, ex, sh):
        _, vjp = jax.vjp(merge, *[t.astype(f32) for t in tok[:4]])
        dga, dgb, dyc, dyg = vjp(tok[4].astype(f32))
        return [dga, dgb, dyc, dyg], [], []

    dga, dgb, dyc, dyg = _rowwise(f_merge_bwd, name="merge_bwd",
                                  tok_in=[ga_blk, gb_blk, (yc, D, 0, False), (yg, D, 0, False), (dmg, D, 0, False)],
                                  tok_out=[(D, bf16)] * 4, **rw_x)
    dzc = _matmul(dyc, Wco, "nt", f32, "conv_out_dx")
    gWco = _matmul(zc, dyc, "tn", f32, "conv_out_dw")
    dog2 = _matmul(dyg, Wgo, "nt", f32, "gla_out_dx")
    gWgo = _matmul(og2, dyg, "tn", f32, "gla_out_dw")

    def f_ln_bwd(tok, ex, sh):
        _, vjp = jax.vjp(ln_silu, tok[0], sh[0], sh[1])
        dcz, dg, db = vjp(tok[1])
        return [dcz], [], [dg, db, jnp.sum(dcz, axis=0, keepdims=True)]

    dcz, g_ln_g, g_ln_b, g_dwb = _rowwise(f_ln_bwd, name="conv_ln_bwd", tok_in=[(cz, D, 0, False), (dzc, D, 0, False)],
                                          sh_in=[conv_ln_g, conv_ln_b], tok_out=[(D, f32)], gl_out=[(1, D)] * 3, **rw_x)
    def col_shards(g):
        return jnp.transpose(g.reshape(g.shape[0], N_DEV, g.shape[1] // N_DEV), (1, 0, 2)).astype(bf16)

    def row_shards(g):
        return g.reshape(N_DEV, g.shape[0] // N_DEV, g.shape[1]).astype(bf16)

    (dca, dcb, g_dww), (r_w2d, r_w2gu, r_wo, r_wco, r_wgo) = _conv_bwd(
        p_all, dcz, dww, B=B, L=L, D=D, name="conv_bwd",
        carry=([row_shards(gW2d), col_shards(gW2gu), row_shards(gWo), row_shards(gWco), row_shards(gWgo)], True))

    def f_gla_out_bwd(tok, ex, sh):
        _, vjp = jax.vjp(gla_out, tok[0], tok[1], tok[2].astype(f32), sh[0])
        dof, _, dog, dgn = vjp(tok[3])
        return [dof, dog], [], [dgn]

    d_o, dog, g_gn = _rowwise(f_gla_out_bwd, name="gla_norm_bwd",
                              tok_in=[(o_f, D, 0, False), (o_b, D, 0, False), og_blk, (dog2, D, 0, False)], sh_in=[gla_norm_g],
                              tok_out=[(D, f32), (D, bf16)], gl_out=[(1, D)], **rw_x)

    dq_f, dk_f, dv_f, dla_f, ds_f = _gla_bwd(p_all, la_all, hist_f, d_o, zeros_s, rev=False, name="gla_x_f_bwd", **gla_x)
    dq, dk, dv, dla_b, ds_b = _gla_bwd(p_all, la_all, hist_b, d_o, zeros_s, rev=True, name="gla_x_b_bwd", add=(dq_f, dk_f, dv_f), **gla_x)
    dq_cf, dk_cf, dv_cf, dla_cf, _ = _gla_bwd(p_all, la_all, hist_cf, None, ds_f, rev=False, name="gla_ctx_f_bwd", **gla_c)
    _, dk_c, dv_c, dla_cb, _ = _gla_bwd(p_all, la_all, hist_cb, None, ds_b, rev=True, name="gla_ctx_b_bwd", add=(dq_cf, dk_cf, dv_cf), **gla_c)

    dla_all = jnp.concatenate([jnp.concatenate([dla_f, dla_b], axis=1), jnp.concatenate([dla_cf, dla_cb], axis=1)], axis=0)

    def f_decay_bwd(tok, ex, sh):
        _, vjp = jax.vjp(log_decay, tok[0].astype(f32), sh[0].astype(f32), sh[1])
        dlr, dwa, dba = vjp(tok[1])
        return [dlr], [], [dwa, dba]

    dlr, g_WA, g_BA = _rowwise(f_decay_bwd, name="log_decay_bwd", tok_in=[lr_blk, (dla_all, 2 * QK, 0, False)], sh_in=[WA, BA],
                               tok_out=[(LR_PAD, bf16)], gl_out=[(LR_PAD, 2 * QK), (1, 2 * QK)], **rw_all)

    zc_ = functools.partial(jnp.zeros, dtype=bf16)
    dp_x = jnp.concatenate([dca, dcb, dv, dog, dga, dgb, dq, dk, dlr[:T]], axis=1)
    dp_c = jnp.concatenate([zc_((Tc, 2 * D)), dv_c, zc_((Tc, 3 * D)), zc_((Tc, QK)), dk_c, dlr[T:]], axis=1)
    dp_all = jnp.concatenate([dp_x, dp_c], axis=0)
    gWin_p = _matmul(um, dp_all, "tn", f32, "in_proj_dw", tm_cap=512, tn_cap=2432)
    gwin_shards = []
    for d in range(N_DEV):
        parts = []
        for lo, hi, po in segs:
            a0, a1 = max(lo, d * wc), min(hi, (d + 1) * wc)
            if a0 < a1:
                parts.append(gWin_p[:, po + a0 - lo:po + a1 - lo])
        gwin_shards.append(jnp.concatenate(parts, axis=1))
    dum, (r_win,) = _matmul(dp_all, Win, "nt", f32, "in_proj_dx", tk_cap=2432, carry=([jnp.stack(gwin_shards).astype(bf16)], True))

    def f_mix_in_bwd(tok, ex, sh):
        _, vjp = jax.vjp(mix_in, tok[0], tok[1], ex[0], ex[1], ex[2], sh[0])
        dx1, df1, dgate, dsh, dsc, dg = vjp((tok[3], tok[2]))
        return [dx1, df1], [dgate, dsh, dsc], [dg]

    dx1, df1, dg2, dsh3, dsc4, dg_mix = _rowwise(
        f_mix_in_bwd, name="mix_in_bwd", tok_in=[(x_all, D, 0, False), (f1, D, 0, False), (dum, D, 0, False), (dx2, D, 0, True)],
        ex_in=[mods[2], mods[3], mods[4]], sh_in=[g_mix], tok_out=[(D, f32), (D, bf16)], ex_out=[D, D, D], gl_out=[(1, D)], **rw_all)

    dgu1 = _ffn_down_dx(df1, W1d, gu1, "ffn1_down_dx")
    gW1d = _matmul(h1, df1, "tn", f32, "ffn1_down_dw", tm_cap=1408)
    gW1gu, (r_w1d,) = _matmul(u1, dgu1, "tn", f32, "ffn1_up_dw", carry=([row_shards(gW1d)], True), halves="b")
    du1, (r_w1gu,) = _matmul(dgu1, W1gu, "nt", f32, "ffn1_up_dx", carry=([col_shards(gW1gu)], True), halves="a")

    def f_ffn_in_bwd(tok, ex, sh):
        _, vjp = jax.vjp(_rms_mod, tok[0], sh[0], ex[0], ex[1])
        dx, dg, dsh, dsc = vjp(tok[1])
        return [dx + tok[2]], [dsh, dsc], [dg]

    dx_all, dsh0, dsc1, dg_ffn1 = _rowwise(
        f_ffn_in_bwd, name="ffn1_in_bwd", tok_in=[(x_all, D, 0, False), (du1, D, 0, False), (dx1, D, 0, False)],
        ex_in=[mods[0], mods[1]], sh_in=[g_ffn1], tok_out=[(D, f32)], ex_out=[D, D], gl_out=[(1, D)], **rw_all)
    grad_x = dx_all[:T].reshape(B, L, D)

    zrow = jnp.zeros((1, 1, D), f32)
    dmod_loc = jnp.concatenate([dsh0, dsc1, dg2, dsh3, dsc4] + [jnp.concatenate([t, zrow], axis=0) for t in (dg5, dsh6, dsc7, dg8)],
                               axis=2).reshape(B + 1, N_MOD * D)
    small = [loss_p, dg_ffn1, dg_mix, g_dww[:CONV_W].reshape(1, CONV_W * D), g_dwb, g_ln_g, g_ln_b,
             g_WA[:LOWRANK, :QK].reshape(1, LOWRANK * QK), g_BA[:, :QK], g_WA[LOWRANK:2 * LOWRANK, QK:].reshape(1, LOWRANK * QK), g_BA[:, QK:],
             g_gn, dg_ffn2, dgf]
    small_w = [s.shape[1] for s in small]
    def to8(v):
        n_pad = -(-v.shape[1] // 1024) * 1024
        return jnp.pad(v, ((0, 0), (0, n_pad - v.shape[1]))).reshape(8, n_pad // 8)

    def from8(a, n):
        return a.reshape(1, a.size)[:, :n]

    dmod_g, small_g = _exchange([dmod_loc, to8(jnp.concatenate(small, axis=1))], False, "gather_small")
    dmx = dmod_g[:, :B].reshape(N_DEV * B, N_MOD * D)
    dmc = dmod_g[:, B]
    gWmod, gcc_p = _mod_bwd(c_all, c_ctx2, w_mod[0], lax.dynamic_slice(dmx, (0, me * ncm), (N_DEV * B, ncm)),
                            lax.dynamic_slice(dmc, (0, me * ncm), (N_DEV, ncm)), "mod_bwd")

    rs_out = [r_w1gu, r_w1d, r_win, r_wco, r_wgo, r_wo, r_w2gu, r_w2d]
    (gcc_g,) = _exchange([to8(gcc_p)], False, "gather_cctx")

    sums, g_cc, g_bmod = _sum_sources([small_g, gcc_g, jnp.concatenate([dmx, dmc], axis=0).reshape(N_DEV * (B + 1), 8, N_MOD * D // 8)], "sum_small")
    sums, g_cc, g_bmod = from8(sums, sum(small_w)), from8(g_cc, D), from8(g_bmod, N_MOD * D)
    offs = [0]
    for wd in small_w:
        offs.append(offs[-1] + wd)
    sm = [sums[:, offs[i]:offs[i + 1]] for i in range(len(small))]
    loss = sm[0][0, 0]
    ncd, nca = dw_weight.shape[2], w_alpha_f.shape[2]
    g_dww_loc = lax.dynamic_slice(sm[3].reshape(CONV_W, D), (0, me * ncd), (CONV_W, ncd)).reshape(1, CONV_W * ncd)
    g_waf_loc = lax.dynamic_slice(sm[7].reshape(LOWRANK, QK), (0, me * nca), (LOWRANK, nca)).reshape(1, LOWRANK * nca)
    g_wab_loc = lax.dynamic_slice(sm[9].reshape(LOWRANK, QK), (0, me * nca), (LOWRANK, nca)).reshape(1, LOWRANK * nca)

    big = {}
    for nm, wv, mv, vv, part in (("w1_gu", w1_gu, m_w1_gu, v_w1_gu, rs_out[0]), ("w1_down", w1_down, m_w1_down, v_w1_down, rs_out[1]),
                                 ("w_in", w_in, m_w_in, v_w_in, rs_out[2]), ("w_conv_out", w_conv_out, m_w_conv_out, v_w_conv_out, rs_out[3]),
                                 ("w_gla_out", w_gla_out, m_w_gla_out, v_w_gla_out, rs_out[4]), ("w_out", w_out, m_w_out, v_w_out, rs_out[5]),
                                 ("w2_gu", w2_gu, m_w2_gu, v_w2_gu, rs_out[6]), ("w2_down", w2_down, m_w2_down, v_w2_down, rs_out[7])):
        big[nm] = [t[None] for t in _adamw(wv[0], mv[0], vv[0], part, "adamw_" + nm, True)]
    big["w_mod"] = [t[None] for t in _adamw(w_mod[0], m_w_mod[0], v_w_mod[0], gWmod, "adamw_w_mod", False)]

    small_params = [("c_ctx", c_ctx, m_c_ctx, v_c_ctx, g_cc), ("b_mod", b_mod, m_b_mod, v_b_mod, g_bmod), ("g_ffn1", g_ffn1, m_g_ffn1, v_g_ffn1, sm[1]),
                    ("g_mix", g_mix, m_g_mix, v_g_mix, sm[2]), ("dw_weight", dw_weight, m_dw_weight, v_dw_weight, g_dww_loc),
                    ("dw_bias", dw_bias, m_dw_bias, v_dw_bias, sm[4]), ("conv_ln_g", conv_ln_g, m_conv_ln_g, v_conv_ln_g, sm[5]),
                    ("conv_ln_b", conv_ln_b, m_conv_ln_b, v_conv_ln_b, sm[6]), ("w_alpha_f", w_alpha_f, m_w_alpha_f, v_w_alpha_f, g_waf_loc),
                    ("b_alpha_f", b_alpha_f, m_b_alpha_f, v_b_alpha_f, sm[8]), ("w_alpha_b", w_alpha_b, m_w_alpha_b, v_w_alpha_b, g_wab_loc),
                    ("b_alpha_b", b_alpha_b, m_b_alpha_b, v_b_alpha_b, sm[10]), ("gla_norm_g", gla_norm_g, m_gla_norm_g, v_gla_norm_g, sm[11]),
                    ("g_ffn2", g_ffn2, m_g_ffn2, v_g_ffn2, sm[12]), ("g_final", g_final, m_g_final, v_g_final, sm[13])]
    flat = lambda t: t.reshape(1, t.size)
    pw, pm, pv, pg = (jnp.concatenate([flat(sp[i]) for sp in small_params], axis=1) for i in (1, 2, 3, 4))
    n_small = pw.shape[1]
    s_g, s_d, s_m, s_v = (from8(t, n_small) for t in _adamw(to8(pw), to8(pm), to8(pv), to8(pg), "adamw_small", False))
    small_out, o0 = {}, 0
    for nm, wv, _, _, _ in small_params:
        small_out[nm] = [t[:, o0:o0 + wv.size].reshape(wv.shape) for t in (s_g, s_d, s_m, s_v)]
        o0 += wv.size

    order = ["c_ctx", "w_mod", "b_mod", "g_ffn1", "w1_gu", "w1_down", "g_mix", "w_in", "dw_weight", "dw_bias", "conv_ln_g", "conv_ln_b",
             "w_conv_out", "w_alpha_f", "b_alpha_f", "w_alpha_b", "b_alpha_b", "gla_norm_g", "w_gla_out", "w_out", "g_ffn2", "w2_gu",
             "w2_down", "g_final"]
    res = {**big, **small_out}
    return (loss, grad_x, *[res[n][0] for n in order], *[res[n][1] for n in order], *[res[n][2] for n in order], *[res[n][3] for n in order])
```

```python
import functools

import jax
import jax.numpy as jnp
from jax import lax
from jax.experimental import pallas as pl
from jax.experimental.pallas import tpu as pltpu

f32, bf16 = jnp.float32, jnp.bfloat16

N_DEV = 8
HEADS = 4
LOWRANK = 16
CONV_W = 31
CONV_PAD = 16
SUBLANES = 8
CHUNK = 64
SUB = 16
GLA_ROWS = 256
GLA_SAFE_DECAY = 60.0
TAU = 16.0
EPS = 1e-6
N_MOD = 9
LR_PAD = 128
ROW_TILE = 256
V7X_VMEM_BYTES = 64 << 20
VMEM_LIMIT = (V7X_VMEM_BYTES * 3) // 4

ADAM_LR, ADAM_B1, ADAM_B2, ADAM_EPS, ADAM_WD, ADAM_STEP = 0.001, 0.9, 0.999, 1e-08, 0.01, 10

MESH = pl.DeviceIdType.MESH


def _pc(body, **kw):
    return pl.pallas_call(body, **kw)


def _params(*sem):
    return pltpu.CompilerParams(dimension_semantics=sem, vmem_limit_bytes=VMEM_LIMIT)


def _pick(n, cap, unit=128):
    best = None
    for t in range(unit, min(n, cap) + 1, unit):
        if n % t == 0:
            best = t
    return best or n


def _matmul(a, b, mode, out_dtype, name, tm_cap=1024, tn_cap=1536, tk_cap=None, carry=None, halves=None):
    tk_cap = tk_cap or (2048 if mode == "tn" else 2816)
    if halves == "a":
        (_, M, Kh), N = a.shape, b.shape[0]
        K, tk = 2 * Kh, _pick(Kh, tk_cap)
        tm, tn = _pick(M, tm_cap), _pick(N, tn_cap)
        a_spec = pl.BlockSpec((None, tm, tk), lambda i, j, k: (k // (Kh // tk), i, k % (Kh // tk)))
    elif halves == "b":
        (K, M), (_, _, Nh) = a.shape, b.shape
        N, tn = 2 * Nh, _pick(Nh, tn_cap)
        tm, tk = _pick(M, tm_cap), _pick(K, tk_cap)
    else:
        if mode == "tn":
            (K, M), N = a.shape, b.shape[1]
        elif mode == "nt":
            (M, K), N = a.shape, b.shape[0]
        else:
            (M, K), N = a.shape, b.shape[1]
        tm, tn, tk = _pick(M, tm_cap), _pick(N, tn_cap), _pick(K, tk_cap)
    nk = K // tk
    if halves != "a":
        a_spec = pl.BlockSpec((tk, tm), lambda i, j, k: (k, i)) if mode == "tn" else pl.BlockSpec((tm, tk), lambda i, j, k: (i, k))
    if halves == "b":
        b_spec = pl.BlockSpec((None, tk, tn), lambda i, j, k: (j // (Nh // tn), k, j % (Nh // tn)))
    else:
        b_spec = pl.BlockSpec((tn, tk), lambda i, j, k: (j, k)) if mode == "nt" else pl.BlockSpec((tk, tn), lambda i, j, k: (k, j))
    dims = {"nn": ((1,), (0,)), "nt": ((1,), (1,)), "tn": ((0,), (0,))}[mode]

    def body_single(a_ref, b_ref, o_ref):
        o_ref[...] = lax.dot_general(a_ref[...].astype(bf16), b_ref[...].astype(bf16), (dims, ((), ())),
                                     preferred_element_type=f32).astype(out_dtype)

    def body(a_ref, b_ref, o_ref, acc_ref):
        k = pl.program_id(2)
        part = lax.dot_general(a_ref[...].astype(bf16), b_ref[...].astype(bf16), (dims, ((), ())), preferred_element_type=f32)

        @pl.when(k == 0)
        def _():
            acc_ref[...] = part

        @pl.when(k > 0)
        def _():
            acc_ref[...] += part

        @pl.when(k == nk - 1)
        def _():
            o_ref[...] = acc_ref[...].astype(out_dtype)

    (out,), carried = _call(
        body_single if nk == 1 else body, name=name, grid=(M // tm, N // tn, nk), in_specs=[a_spec, b_spec],
        out_specs=[pl.BlockSpec((tm, tn), lambda i, j, k: (i, j))], out_shape=[jax.ShapeDtypeStruct((M, N), out_dtype)],
        scratch_shapes=[] if nk == 1 else [pltpu.VMEM((tm, tn), f32)], sem=("parallel", "parallel", "arbitrary"),
        args=(a, b), carry=carry)
    return out if carry is None else (out, carried)


def _ffn_up(u, Wgu, name, carry=None):
    M, K = u.shape
    F = Wgu.shape[1] // 2
    tm, tn = _pick(M, 512), _pick(F, 1408)
    nj = F // tn

    def body(u_ref, wa_ref, wb_ref, gu_ref, h_ref):
        uv = u_ref[...]
        a = jnp.dot(uv, wa_ref[...], preferred_element_type=f32)
        b = jnp.dot(uv, wb_ref[...], preferred_element_type=f32)
        gu_ref[0] = a.astype(bf16)
        gu_ref[1] = b.astype(bf16)
        h_ref[...] = (jax.nn.silu(a) * b).astype(bf16)

    res, carried = _call(
        body, name=name, grid=(nj, M // tm),
        in_specs=[pl.BlockSpec((tm, K), lambda j, i: (i, 0)), pl.BlockSpec((K, tn), lambda j, i: (0, j)),
                  pl.BlockSpec((K, tn), lambda j, i: (0, nj + j))],
        out_specs=[pl.BlockSpec((2, tm, tn), lambda j, i: (0, i, j)), pl.BlockSpec((tm, tn), lambda j, i: (i, j))],
        out_shape=[jax.ShapeDtypeStruct((2, M, F), bf16), jax.ShapeDtypeStruct((M, F), bf16)],
        scratch_shapes=[], sem=("parallel", "parallel"), args=(u, Wgu, Wgu), carry=carry)
    return res if carry is None else (res, carried)


def _ffn_down_dx(df, Wd, gu, name):
    M, D = df.shape
    F = Wd.shape[0]
    tm, tn = _pick(M, 512), _pick(F, 1408)

    def body(df_ref, w_ref, gu_ref, o_ref):
        dh = lax.dot_general(df_ref[...], w_ref[...], (((1,), (1,)), ((), ())), preferred_element_type=f32)
        a, b = gu_ref[0].astype(f32), gu_ref[1].astype(f32)
        sg = jax.nn.sigmoid(a)
        o_ref[0] = (dh * b * sg * (1.0 + a * (1.0 - sg))).astype(bf16)
        o_ref[1] = (dh * a * sg).astype(bf16)

    return _pc(
        body, name=name, grid=(F // tn, M // tm),
        in_specs=[pl.BlockSpec((tm, D), lambda j, i: (i, 0)), pl.BlockSpec((tn, D), lambda j, i: (j, 0)),
                  pl.BlockSpec((2, tm, tn), lambda j, i: (0, i, j))],
        out_specs=pl.BlockSpec((2, tm, tn), lambda j, i: (0, i, j)), out_shape=jax.ShapeDtypeStruct((2, M, F), bf16),
        compiler_params=_params("parallel", "parallel"))(df, Wd, gu)


def _rowwise(fn, *, name, tm, n_tiles, tpe, nx_tiles, n_ex, tok_in=(), ex_in=(), sh_in=(), tok_out=(), ex_out=(), gl_out=()):
    def seg(i):
        return jnp.minimum(i // tpe, n_ex - 1)

    in_specs, args = [], []
    for arr, w, cb, x_only in tok_in:
        if x_only:
            in_specs.append(pl.BlockSpec((tm, w), functools.partial(lambda i, cb: (jnp.minimum(i, nx_tiles - 1), cb), cb=cb)))
        else:
            in_specs.append(pl.BlockSpec((tm, w), functools.partial(lambda i, cb: (i, cb), cb=cb)))
        args.append(arr)
    for arr in ex_in:
        in_specs.append(pl.BlockSpec((1, 1, arr.shape[-1]), lambda i: (seg(i), 0, 0)))
        args.append(arr)
    for arr in sh_in:
        in_specs.append(pl.BlockSpec(arr.shape, functools.partial(lambda i, nd: (0,) * nd, nd=arr.ndim)))
        args.append(arr)
    out_specs, out_shape = [], []
    for w, dt in tok_out:
        out_specs.append(pl.BlockSpec((tm, w), lambda i: (i, 0)))
        out_shape.append(jax.ShapeDtypeStruct((n_tiles * tm, w), dt))
    for w in ex_out:
        out_specs.append(pl.BlockSpec((1, 1, w), lambda i: (seg(i), 0, 0)))
        out_shape.append(jax.ShapeDtypeStruct((n_ex, 1, w), f32))
    for r, w in gl_out:
        out_specs.append(pl.BlockSpec((r, w), lambda i: (0, 0)))
        out_shape.append(jax.ShapeDtypeStruct((r, w), f32))
    n_tok, n_exi, n_sh = len(tok_in), len(ex_in), len(sh_in)
    n_to, n_eo = len(tok_out), len(ex_out)
    x_only_flags = [t[3] for t in tok_in]

    def body(*refs):
        i = pl.program_id(0)
        ins, outs = refs[: n_tok + n_exi + n_sh], refs[n_tok + n_exi + n_sh:]
        is_x = i < nx_tiles
        tok_vals = []
        for r, xo in zip(ins[:n_tok], x_only_flags):
            v = r[...]
            tok_vals.append(jnp.where(is_x, v, jnp.zeros_like(v)) if xo else v)
        ex_vals = [r[0] for r in ins[n_tok:n_tok + n_exi]]
        sh_vals = [r[...] for r in ins[n_tok + n_exi:]]
        t_o, e_o, g_o = fn(tok_vals, ex_vals, sh_vals)
        for r, v in zip(outs[:n_to], t_o):
            r[...] = v.astype(r.dtype)
        first = jnp.logical_and(i % tpe == 0, i <= nx_tiles)
        for r, v in zip(outs[n_to:n_to + n_eo], e_o):
            @pl.when(first)
            def _(r=r, v=v):
                r[0] = v

            @pl.when(jnp.logical_not(first))
            def _(r=r, v=v):
                r[0] += v
        for r, v in zip(outs[n_to + n_eo:], g_o):
            @pl.when(i == 0)
            def _(r=r, v=v):
                r[...] = v

            @pl.when(i > 0)
            def _(r=r, v=v):
                r[...] += v

    res = _pc(body, name=name, grid=(n_tiles,), in_specs=in_specs, out_specs=out_specs, out_shape=out_shape,
              compiler_params=_params("arbitrary"))(*args)
    return list(res)


def _rms_mod(x, g, sh, sc):
    y = x * lax.rsqrt(jnp.mean(x * x, axis=-1, keepdims=True) + EPS) * g
    return y * (1.0 + sc) + sh


def _log_sigmoid(z):
    return jnp.minimum(z, 0.0) - jnp.log(1.0 + jnp.exp(-jnp.abs(z)))


def _head_rms(o, DV):
    parts = []
    for h in range(HEADS):
        oh = o[:, h * DV:(h + 1) * DV]
        parts.append(oh * lax.rsqrt(jnp.mean(oh * oh, axis=-1, keepdims=True) + EPS))
    return jnp.concatenate(parts, axis=1)


@functools.partial(jax.custom_vjp, nondiff_argnums=(2,))
def _bdot(a, b, dims):
    return lax.dot_general(a.astype(bf16), b.astype(bf16), (((dims[0],), (dims[1],)), ((), ())), preferred_element_type=f32)


def _bdot_fwd(a, b, dims):
    return _bdot(a, b, dims), (a, b)


def _bdot_bwd(dims, res, g):
    a, b = res
    ca, cb = dims
    da = _bdot(g, b, (1, 1 - cb)) if ca == 1 else _bdot(b, g, (1 - cb, 1))
    db = _bdot(a, g, (1 - ca, 0)) if cb == 0 else _bdot(g, a, (0, 1 - ca))
    return da, db


_bdot.defvjp(_bdot_fwd, _bdot_bwd)


def _split_dot(m, x, dims):
    mb, rem, acc = m.astype(bf16), x, None
    for _ in range(3):
        piece = rem.astype(bf16)
        rem = rem - piece.astype(f32)
        part = lax.dot_general(mb, piece, (((dims[0],), (dims[1],)), ((), ())), preferred_element_type=f32)
        acc = part if acc is None else acc + part
    return acc


@jax.custom_vjp
def _tri_cumsum(tri, g):
    return _split_dot(tri, g, (1, 0))


def _tri_cumsum_fwd(tri, g):
    return _tri_cumsum(tri, g), tri


def _tri_cumsum_bwd(tri, db):
    return jnp.zeros_like(tri), _split_dot(tri, db, (0, 0))


_tri_cumsum.defvjp(_tri_cumsum_fwd, _tri_cumsum_bwd)


def _gla_chunk(St, q, k, v, g, *, rev, scale, exact):
    C, DK = q.shape
    r = lax.broadcasted_iota(jnp.int32, (C, C), 0)
    c = lax.broadcasted_iota(jnp.int32, (C, C), 1)
    causal = (r <= c) if rev else (r >= c)
    b = _tri_cumsum(causal.astype(f32), g)
    qs = q * scale
    qe = qs * jnp.exp(b)
    inter = _bdot(qe, St, (1, 1))
    b_last = b[0:1] if rev else b[C - 1:C]
    kd = k * jnp.exp(b_last - b)
    St_new = St * jnp.exp(b_last) + _bdot(v, kd, (0, 0))
    if not exact:
        att = jnp.where(causal, _bdot(qe, k * jnp.exp(-b), (1, 1)), 0.0)
        return St_new, inter + _bdot(att, v, (1, 0))
    rr = lax.broadcasted_iota(jnp.int32, (SUB, SUB, DK), 0)
    cc = lax.broadcasted_iota(jnp.int32, (SUB, SUB, DK), 1)
    m3 = (rr <= cc) if rev else (rr >= cc)
    outs = []
    for i in range(C // SUB):
        lo, hi = i * SUB, (i + 1) * SUB
        bi, qi, ki, vi = b[lo:hi], qs[lo:hi], k[lo:hi], v[lo:hi]
        rel = bi[:, None, :] - bi[None, :, :]
        e = jnp.where(m3, jnp.exp(jnp.where(m3, rel, 0.0)), 0.0)
        att = jnp.sum(qi[:, None, :] * e * ki[None, :, :], axis=-1)
        acc = _bdot(att, vi, (1, 0))
        ref_row = b[hi - 1:hi] if rev else b[lo:lo + 1]
        prev = slice(hi, C) if rev else slice(0, lo)
        if (hi < C) if rev else (lo > 0):
            qn = qi * jnp.exp(bi - ref_row)
            ks = k[prev] * jnp.exp(ref_row - b[prev])
            acc = acc + _bdot(_bdot(qn, ks, (1, 1)), v[prev], (1, 0))
        outs.append(acc)
    return St_new, inter + jnp.concatenate(outs, axis=0)


def _mild_decay(la_ref):
    return jnp.min(la_ref[...]) >= -GLA_SAFE_DECAY / CHUNK


def _gla_specs(D, rev_blocks, row0, seq):
    DK, DV = D // (2 * HEADS), D // HEADS
    nblk = seq // GLA_ROWS
    rb0 = row0 // GLA_ROWS

    def blk(j):
        return (nblk - 1 - j) if rev_blocks else j

    return DK, DV, nblk, rb0, blk


def _gla_in_specs(D, rev, rows):
    QK = D // 2
    return [
        pl.BlockSpec((GLA_ROWS, QK), lambda b, j: (rows(b, j), 6 * D // QK)),
        pl.BlockSpec((GLA_ROWS, QK), lambda b, j: (rows(b, j), 6 * D // QK + 1)),
        pl.BlockSpec((GLA_ROWS, D), lambda b, j: (rows(b, j), 2)),
        pl.BlockSpec((GLA_ROWS, QK), lambda b, j: (rows(b, j), 1 if rev else 0)),
    ]


def _gla_fwd(p_all, la_all, s0, *, rev, row0, nb, seq, D, name, carry=None):
    DK, DV, nblk, rb0, blk = _gla_specs(D, rev, row0, seq)
    cpb = GLA_ROWS // CHUNK

    def rows(b, j):
        return rb0 + b * nblk + blk(j)

    in_specs = _gla_in_specs(D, rev, rows) + [pl.BlockSpec((1, HEADS, DV, DK), lambda b, j: (b, 0, 0, 0))]
    out_specs = [
        pl.BlockSpec((GLA_ROWS, D), lambda b, j: (b * nblk + blk(j), 0)),
        pl.BlockSpec((1, HEADS, cpb, DV, DK), lambda b, j: (b, 0, blk(j), 0, 0)),
        pl.BlockSpec((1, HEADS, DV, DK), lambda b, j: (b, 0, 0, 0)),
    ]
    out_shape = [
        jax.ShapeDtypeStruct((nb * seq, D), f32),
        jax.ShapeDtypeStruct((nb, HEADS, seq // CHUNK, DV, DK), f32),
        jax.ShapeDtypeStruct((nb, HEADS, DV, DK), f32),
    ]
    chunk = functools.partial(_gla_chunk, rev=rev, scale=DK ** -0.5)

    def body(q_ref, k_ref, v_ref, la_ref, s0_ref, o_ref, hist_ref, sfin_ref, st_ref):
        j = pl.program_id(1)

        @pl.when(j == 0)
        def _():
            st_ref[...] = s0_ref[0]

        def step(ci, exact):
            cc = (cpb - 1 - ci) if rev else ci
            sl = pl.ds(cc * CHUNK, CHUNK)
            for h in range(HEADS):
                kq, kv = pl.ds(h * DK, DK), pl.ds(h * DV, DV)
                St = st_ref[h]
                hist_ref[0, h, cc] = St
                St2, o = chunk(St, q_ref[sl, kq].astype(f32), k_ref[sl, kq].astype(f32), v_ref[sl, kv].astype(f32), la_ref[sl, kq],
                               exact=exact)
                o_ref[sl, kv] = o
                st_ref[h] = St2

        mild = _mild_decay(la_ref)
        for exact in (False, True):
            @pl.when(jnp.logical_not(mild) if exact else mild)
            def _(exact=exact):
                for ci in range(cpb):
                    step(ci, exact)

        @pl.when(j == nblk - 1)
        def _():
            sfin_ref[0] = st_ref[...]

    res, carried = _call(body, name=name, grid=(nb, nblk), in_specs=in_specs, out_specs=out_specs, out_shape=out_shape,
                         scratch_shapes=[pltpu.VMEM((HEADS, DV, DK), f32)], sem=("parallel", "arbitrary"),
                         args=(p_all, p_all, p_all, la_all, s0), carry=carry)
    return res if carry is None else (res, carried)


def _gla_bwd(p_all, la_all, hist, do, dsfin, *, rev, row0, nb, seq, D, name, add=None):
    DK, DV, nblk, rb0, blk = _gla_specs(D, not rev, row0, seq)
    cpb = GLA_ROWS // CHUNK
    QK = HEADS * DK
    has_do = do is not None

    def rows(b, j):
        return rb0 + b * nblk + blk(j)

    in_specs = _gla_in_specs(D, rev, rows) + [
        pl.BlockSpec((1, HEADS, cpb, DV, DK), lambda b, j: (b, 0, blk(j), 0, 0)),
        pl.BlockSpec((1, HEADS, DV, DK), lambda b, j: (b, 0, 0, 0)),
    ]
    args = [p_all, p_all, p_all, la_all, hist, dsfin]
    if has_do:
        in_specs.append(pl.BlockSpec((GLA_ROWS, D), lambda b, j: (b * nblk + blk(j), 0)))
        args.append(do)
    if add is not None:
        in_specs += [pl.BlockSpec((GLA_ROWS, t.shape[1]), lambda b, j: (b * nblk + blk(j), 0)) for t in add]
        args += list(add)
    gdt = f32 if add is None else bf16
    out_specs = [
        pl.BlockSpec((GLA_ROWS, QK), lambda b, j: (b * nblk + blk(j), 0)),
        pl.BlockSpec((GLA_ROWS, QK), lambda b, j: (b * nblk + blk(j), 0)),
        pl.BlockSpec((GLA_ROWS, D), lambda b, j: (b * nblk + blk(j), 0)),
        pl.BlockSpec((GLA_ROWS, QK), lambda b, j: (b * nblk + blk(j), 0)),
        pl.BlockSpec((1, HEADS, DV, DK), lambda b, j: (b, 0, 0, 0)),
    ]
    out_shape = [
        jax.ShapeDtypeStruct((nb * seq, QK), gdt), jax.ShapeDtypeStruct((nb * seq, QK), gdt),
        jax.ShapeDtypeStruct((nb * seq, D), gdt), jax.ShapeDtypeStruct((nb * seq, QK), f32),
        jax.ShapeDtypeStruct((nb, HEADS, DV, DK), f32),
    ]
    chunk = functools.partial(_gla_chunk, rev=rev, scale=DK ** -0.5)

    def body(*refs):
        refs = list(refs)
        q_ref, k_ref, v_ref, la_ref, hist_ref, dsfin_ref = refs[:6]
        do_ref = refs[6] if has_do else None
        add_refs = refs[6 + has_do:len(refs) - 6]
        dq_ref, dk_ref, dv_ref, dla_ref, ds0_ref, ds_ref = refs[len(refs) - 6:]
        j = pl.program_id(1)

        @pl.when(j == 0)
        def _():
            ds_ref[...] = dsfin_ref[0]

        def step(ci, exact):
            cc = ci if rev else (cpb - 1 - ci)
            sl = pl.ds(cc * CHUNK, CHUNK)
            for h in range(HEADS):
                kq, kv = pl.ds(h * DK, DK), pl.ds(h * DV, DV)
                prim = (hist_ref[0, h, cc], q_ref[sl, kq].astype(f32), k_ref[sl, kq].astype(f32), v_ref[sl, kv].astype(f32), la_ref[sl, kq])
                _, vjp = jax.vjp(functools.partial(chunk, exact=exact), *prim)
                d_o = do_ref[sl, kv] if has_do else jnp.zeros((CHUNK, DV), f32)
                dSt, dq, dk, dv, dg = vjp((ds_ref[h], d_o))
                if add is not None:
                    dq, dk, dv = dq + add_refs[0][sl, kq], dk + add_refs[1][sl, kq], dv + add_refs[2][sl, kv]
                dq_ref[sl, kq] = dq.astype(gdt)
                dk_ref[sl, kq] = dk.astype(gdt)
                dv_ref[sl, kv] = dv.astype(gdt)
                dla_ref[sl, kq] = dg
                ds_ref[h] = dSt

        mild = _mild_decay(la_ref)
        for exact in (False, True):
            @pl.when(jnp.logical_not(mild) if exact else mild)
            def _(exact=exact):
                for ci in range(cpb):
                    step(ci, exact)

        @pl.when(j == nblk - 1)
        def _():
            ds0_ref[0] = ds_ref[...]

    return _pc(body, name=name, grid=(nb, nblk), in_specs=in_specs, out_specs=out_specs, out_shape=out_shape,
               scratch_shapes=[pltpu.VMEM((HEADS, DV, DK), f32)], compiler_params=_params("parallel", "arbitrary"))(*args)


def _conv_fwd(p_all, dw_w, dw_b, *, B, L, D, name):
    ct = _pick(D, 256)
    nj = D // ct
    st = _pick(L, 128, 8)
    off = CONV_PAD - CONV_W // 2

    def body(a_ref, b_ref, w_ref, bias_ref, o_ref, zs_ref):
        _fill_shifted(zs_ref, L, lambda t0, n: a_ref[pl.ds(t0, n), :].astype(f32) * jax.nn.sigmoid(b_ref[pl.ds(t0, n), :].astype(f32)))
        for t0 in range(0, L, st):
            acc = jnp.zeros((st, ct), f32) + bias_ref[...]
            for k in range(CONV_W):
                acc = acc + w_ref[pl.ds(k, 1), :] * _window(zs_ref, t0 + k + off, st)
            o_ref[pl.ds(t0, st), :] = acc

    return _pc(
        body, name=name, grid=(B, nj),
        in_specs=[pl.BlockSpec((L, ct), lambda b, j: (b, j)), pl.BlockSpec((L, ct), lambda b, j: (b, nj + j)),
                  pl.BlockSpec((CONV_W, ct), lambda b, j: (0, j)), pl.BlockSpec((1, ct), lambda b, j: (0, j))],
        out_specs=pl.BlockSpec((L, ct), lambda b, j: (b, j)), out_shape=jax.ShapeDtypeStruct((B * L, D), f32),
        scratch_shapes=[pltpu.VMEM((SUBLANES, L + 2 * CONV_PAD, ct), f32)], compiler_params=_params("parallel", "parallel"),
    )(p_all, p_all, dw_w, dw_b)


def _fill_shifted(zs_ref, L, rows):
    lp = L + 2 * CONV_PAD
    ct = zs_ref.shape[2]
    step = 256
    zs_ref[0, pl.ds(0, CONV_PAD), :] = jnp.zeros((CONV_PAD, ct), f32)
    zs_ref[0, pl.ds(CONV_PAD + L, CONV_PAD), :] = jnp.zeros((CONV_PAD, ct), f32)
    for t0 in range(0, L, step):
        n = min(step, L - t0)
        zs_ref[0, pl.ds(CONV_PAD + t0, n), :] = rows(t0, n)
    for r in range(1, SUBLANES):
        for i0 in range(0, lp - SUBLANES, step):
            n = min(step, lp - SUBLANES - i0)
            zs_ref[r, pl.ds(i0, n), :] = zs_ref[0, pl.ds(i0 + r, n), :]


def _window(zs_ref, start, n):
    r = start % SUBLANES
    return zs_ref[r, pl.ds(start - r, n), :]


def _conv_bwd(p_all, dcz, dw_w, *, B, L, D, name, carry=None):
    ct = _pick(D, 128)
    nj = D // ct
    st = _pick(L, 256, 8)
    half = CONV_W // 2

    def body(a_ref, b_ref, dcz_ref, w_ref, da_ref, db_ref, ddw_ref, zs_ref, ds_ref):
        bi = pl.program_id(1)
        _fill_shifted(zs_ref, L, lambda t0, n: a_ref[pl.ds(t0, n), :].astype(f32) * jax.nn.sigmoid(b_ref[pl.ds(t0, n), :].astype(f32)))
        _fill_shifted(ds_ref, L, lambda t0, n: dcz_ref[pl.ds(t0, n), :])

        @pl.when(bi == 0)
        def _():
            ddw_ref[...] = jnp.zeros_like(ddw_ref)

        for t0 in range(0, L, st):
            acc = jnp.zeros((st, ct), f32)
            for k in range(CONV_W):
                acc = acc + w_ref[pl.ds(k, 1), :] * _window(ds_ref, t0 + CONV_PAD + half - k, st)
            a_t = a_ref[pl.ds(t0, st), :].astype(f32)
            sg_t = jax.nn.sigmoid(b_ref[pl.ds(t0, st), :].astype(f32))
            da_ref[pl.ds(t0, st), :] = (acc * sg_t).astype(bf16)
            db_ref[pl.ds(t0, st), :] = (acc * a_t * sg_t * (1.0 - sg_t)).astype(bf16)

        for k in range(CONV_W):
            part = jnp.zeros((SUBLANES, ct), f32)
            for t0 in range(0, L, st):
                prod = dcz_ref[pl.ds(t0, st), :] * _window(zs_ref, t0 + k + CONV_PAD - half, st)
                for i in range(0, st, SUBLANES):
                    part = part + prod[i:i + SUBLANES]
            ddw_ref[pl.ds(k, 1), :] += jnp.sum(part, axis=0, keepdims=True)

    res, carried = _call(
        body, name=name, grid=(nj, B),
        in_specs=[pl.BlockSpec((L, ct), lambda j, b: (b, j)), pl.BlockSpec((L, ct), lambda j, b: (b, nj + j)),
                  pl.BlockSpec((L, ct), lambda j, b: (b, j)), pl.BlockSpec((CONV_W, ct), lambda j, b: (0, j))],
        out_specs=[pl.BlockSpec((L, ct), lambda j, b: (b, j)), pl.BlockSpec((L, ct), lambda j, b: (b, j)),
                   pl.BlockSpec((2 * CONV_PAD, ct), lambda j, b: (0, j))],
        out_shape=[jax.ShapeDtypeStruct((B * L, D), bf16), jax.ShapeDtypeStruct((B * L, D), bf16),
                   jax.ShapeDtypeStruct((2 * CONV_PAD, D), f32)],
        scratch_shapes=[pltpu.VMEM((SUBLANES, L + 2 * CONV_PAD, ct), f32), pltpu.VMEM((SUBLANES, L + 2 * CONV_PAD, ct), f32)],
        sem=("parallel", "arbitrary"), args=(p_all, p_all, dcz, dw_w), carry=carry)
    return res if carry is None else (res, carried)


def _exchange(arrs, scatter, name):
    ex = _Exchange(arrs, scatter)
    n = ex.n

    def body(*refs):
        ex.start(refs[:n], refs[n:2 * n], refs[2 * n:])
        ex.finish(refs[:n], refs[n:2 * n], refs[2 * n:])

    res = _pc(body, name=name, in_specs=ex.specs, out_specs=ex.specs, out_shape=ex.out_shape, scratch_shapes=ex.scratch)(*arrs)
    return list(res)


class _Exchange:
    def __init__(self, arrs, scatter):
        self.arrs, self.scatter, self.n = list(arrs), scatter, len(arrs)
        self.out_shape = [jax.ShapeDtypeStruct(((N_DEV,) + a.shape[1:]) if scatter else ((N_DEV,) + a.shape), a.dtype) for a in arrs]
        self.specs = [pl.BlockSpec(memory_space=pl.ANY)] * self.n
        self.scratch = [pltpu.SemaphoreType.DMA((self.n, N_DEV - 1)), pltpu.SemaphoreType.DMA((self.n, N_DEV - 1)),
                        pltpu.SemaphoreType.DMA((self.n,))]

    def _copies(self, ins, outs, sems, landing):
        send_sems, recv_sems, local_sems = sems
        me = 4 * lax.axis_index("x") + 2 * lax.axis_index("y") + lax.axis_index("c")
        if landing:
            local = []
        else:
            local = [pltpu.make_async_copy(ins[a].at[me] if self.scatter else ins[a], outs[a].at[me], local_sems.at[a]) for a in range(self.n)]
        remote = []
        for k in range(1, N_DEV):
            p = (me + (N_DEV - k if landing else k)) % N_DEV
            for a in range(self.n):
                remote.append(pltpu.make_async_remote_copy(
                    src_ref=ins[a].at[p] if self.scatter else ins[a], dst_ref=outs[a].at[p if landing else me],
                    send_sem=send_sems.at[a, k - 1], recv_sem=recv_sems.at[a, k - 1],
                    device_id=(p // 4, (p // 2) % 2, p % 2), device_id_type=MESH))
        return local, remote

    def _gather_plan(self, ins, outs, sems):
        send_sems, recv_sems, local_sems = sems
        x, y, c = lax.axis_index("x"), lax.axis_index("y"), lax.axis_index("c")
        chips = [(1 - x, y), (x, 1 - y), (1 - x, 1 - y)]

        def blk(px, py, pc):
            return 4 * px + 2 * py + pc

        def copy(a, k, block, to, own):
            return pltpu.make_async_remote_copy(
                src_ref=ins[a] if own else outs[a].at[block], dst_ref=outs[a].at[block],
                send_sem=send_sems.at[a, k], recv_sem=recv_sems.at[a, k], device_id=to, device_id_type=MESH)

        me = blk(x, y, c)
        local = [pltpu.make_async_copy(ins[a], outs[a].at[me], local_sems.at[a]) for a in range(self.n)]
        return local, copy, me, (x, y, 1 - c), chips, blk, c

    def start(self, ins, outs, sems):
        if self.scatter:
            local, sends = self._copies(ins, outs, sems, False)
            for cp in local + sends:
                cp.start()
            return
        local, copy, me, sibling, chips, _, c = self._gather_plan(ins, outs, sems)
        for cp in local:
            cp.start()
        for a in range(self.n):
            copy(a, 0, me, sibling, True).start()
            for j, chip in enumerate(chips):
                copy(a, 1 + j, me, (*chip, c), True).start()

    def finish(self, ins, outs, sems):
        if self.scatter:
            for cp in self._copies(ins, outs, sems, True)[1]:
                cp.wait_recv()
            local, sends = self._copies(ins, outs, sems, False)
            for cp in sends:
                cp.wait_send()
            for cp in local:
                cp.wait()
            return
        local, copy, me, sibling, chips, blk, c = self._gather_plan(ins, outs, sems)
        for j, chip in enumerate(chips):
            for a in range(self.n):
                copy(a, 1 + j, blk(*chip, c), sibling, True).wait_recv()
                copy(a, 4 + j, blk(*chip, c), sibling, False).start()
        for a in range(self.n):
            copy(a, 0, blk(*sibling), sibling, True).wait_recv()
            for j, chip in enumerate(chips):
                copy(a, 4 + j, blk(*chip, 1 - c), sibling, False).wait_recv()
        for a in range(self.n):
            copy(a, 0, me, sibling, True).wait_send()
            for j, chip in enumerate(chips):
                copy(a, 1 + j, me, (*chip, c), True).wait_send()
                copy(a, 4 + j, blk(*chip, c), sibling, False).wait_send()
        for cp in local:
            cp.wait()


def _carried(inner, n_in, n_out, grid, ex):
    n = ex.n

    def body(*refs):
        own_in, c_in = refs[:n_in], refs[n_in:n_in + n]
        own_out, c_out = refs[n_in + n:n_in + n + n_out], refs[n_in + n + n_out:n_in + 2 * n + n_out]
        rest = refs[n_in + 2 * n + n_out:]
        own_scr, sems = rest[:len(rest) - 3], rest[len(rest) - 3:]
        pids = [pl.program_id(d) for d in range(len(grid))]
        first = functools.reduce(jnp.logical_and, [p == 0 for p in pids])
        last = functools.reduce(jnp.logical_and, [p == g - 1 for p, g in zip(pids, grid)])

        @pl.when(first)
        def _():
            ex.start(c_in, c_out, sems)

        inner(*own_in, *own_out, *own_scr)

        @pl.when(last)
        def _():
            ex.finish(c_in, c_out, sems)

    return body


def _call(inner, *, name, grid, in_specs, out_specs, out_shape, scratch_shapes, sem, args, carry=None):
    if carry is None:
        res = _pc(inner, name=name, grid=grid, in_specs=in_specs, out_specs=out_specs, out_shape=out_shape,
                  scratch_shapes=scratch_shapes, compiler_params=_params(*sem))(*args)
        return list(res), None
    ex = _Exchange(*carry)
    res = _pc(_carried(inner, len(in_specs), len(out_specs), grid, ex), name=name, grid=grid,
              in_specs=list(in_specs) + ex.specs, out_specs=list(out_specs) + ex.specs, out_shape=list(out_shape) + ex.out_shape,
              scratch_shapes=list(scratch_shapes) + ex.scratch, compiler_params=_params(*(["arbitrary"] * len(grid))))(*args, *ex.arrs)
    res = list(res)
    return res[:len(out_specs)], res[len(out_specs):]


def _mod_fwd(c_all, c_ctx, w_loc, b_loc, name):
    nr, D = c_all.shape
    nc = w_loc.shape[1]

    def body(c_ref, cc_ref, w_ref, b_ref, o_ref):
        a = jnp.concatenate([c_ref[...], jnp.broadcast_to(cc_ref[...], (8, D))], axis=0)
        s = jax.nn.silu(a).astype(bf16)
        o_ref[...] = jnp.dot(s, w_ref[...].astype(bf16), preferred_element_type=f32) + b_ref[...]

    return _pc(body, name=name, out_shape=jax.ShapeDtypeStruct((nr + 8, nc), f32), compiler_params=_params())(c_all, c_ctx, w_loc, b_loc)


def _mod_bwd(c_all, c_ctx, w_loc, dmx_loc, dmc_loc, name):
    nr, D = c_all.shape
    nc = w_loc.shape[1]

    def body(c_ref, cc_ref, w_ref, dmx_ref, dmc_ref, gw_ref, gc_ref):
        cc = cc_ref[...]
        a = jnp.concatenate([c_ref[...], jnp.broadcast_to(cc, (N_DEV, D))], axis=0)
        s = jax.nn.silu(a).astype(bf16)
        g = jnp.concatenate([dmx_ref[...], dmc_ref[...]], axis=0).astype(bf16)
        gw_ref[...] = lax.dot_general(s, g, (((0,), (0,)), ((), ())), preferred_element_type=f32)
        dmc = jnp.sum(dmc_ref[...], axis=0, keepdims=True)
        ds = lax.dot_general(jnp.broadcast_to(dmc, (8, nc)).astype(bf16), w_ref[...].astype(bf16), (((1,), (1,)), ((), ())),
                             preferred_element_type=f32)[0:1]
        sg = jax.nn.sigmoid(cc)
        gc_ref[...] = ds * (sg * (1.0 + cc * (1.0 - sg)))

    return _pc(body, name=name, out_shape=[jax.ShapeDtypeStruct((D, nc), f32), jax.ShapeDtypeStruct((1, D), f32)],
               compiler_params=_params())(c_all, c_ctx, w_loc, dmx_loc, dmc_loc)


def _adamw_math(w, g, m, v):
    m2 = ADAM_B1 * m + (1.0 - ADAM_B1) * g
    v2 = ADAM_B2 * v + (1.0 - ADAM_B2) * jnp.square(g)
    m_hat = m2 / (1.0 - ADAM_B1 ** ADAM_STEP)
    v_hat = v2 / (1.0 - ADAM_B2 ** ADAM_STEP)
    delta = -ADAM_LR * (m_hat / (jnp.sqrt(v_hat) + ADAM_EPS) + ADAM_WD * w)
    return delta, m2, v2


def _adamw(w, m, v, g, name, partials):
    r, cdim = w.shape
    tr = _pick(r, 256, 8)

    def body(w_ref, m_ref, v_ref, g_ref, og_ref, od_ref, om_ref, ov_ref):
        if partials:
            g = g_ref[0].astype(f32)
            for s in range(1, N_DEV):
                g = g + g_ref[s].astype(f32)
        else:
            g = g_ref[...]
        d, m2, v2 = _adamw_math(w_ref[...], g, m_ref[...], v_ref[...])
        og_ref[...] = g
        od_ref[...] = d
        om_ref[...] = m2
        ov_ref[...] = v2

    blk = pl.BlockSpec((tr, cdim), lambda i: (i, 0))
    g_spec = pl.BlockSpec((N_DEV, tr, cdim), lambda i: (0, i, 0)) if partials else blk
    return _pc(body, name=name, grid=(r // tr,), in_specs=[blk, blk, blk, g_spec], out_specs=[blk] * 4,
               out_shape=[jax.ShapeDtypeStruct((r, cdim), f32)] * 4, compiler_params=_params("parallel"))(w, m, v, g)


def _sum_sources(parts, name):
    def body(*refs):
        for i_ref, o_ref in zip(refs[:len(parts)], refs[len(parts):]):
            acc = i_ref[0]
            for s in range(1, i_ref.shape[0]):
                acc = acc + i_ref[s]
            o_ref[...] = acc

    return list(_pc(body, name=name, out_shape=[jax.ShapeDtypeStruct(p.shape[1:], f32) for p in parts],
                    compiler_params=_params())(*parts))


def kernel(x, c, ctx, c_ctx, w_mod, b_mod, g_ffn1, w1_gu, w1_down, g_mix, w_in, dw_weight, dw_bias, conv_ln_g, conv_ln_b, w_conv_out, w_alpha_f, b_alpha_f, w_alpha_b, b_alpha_b, gla_norm_g, w_gla_out, w_out, g_ffn2, w2_gu, w2_down, g_final, loss_target, m_c_ctx, m_w_mod, m_b_mod, m_g_ffn1, m_w1_gu, m_w1_down, m_g_mix, m_w_in, m_dw_weight, m_dw_bias, m_conv_ln_g, m_conv_ln_b, m_w_conv_out, m_w_alpha_f, m_b_alpha_f, m_w_alpha_b, m_b_alpha_b, m_gla_norm_g, m_w_gla_out, m_w_out, m_g_ffn2, m_w2_gu, m_w2_down, m_g_final, v_c_ctx, v_w_mod, v_b_mod, v_g_ffn1, v_w1_gu, v_w1_down, v_g_mix, v_w_in, v_dw_weight, v_dw_bias, v_conv_ln_g, v_conv_ln_b, v_w_conv_out, v_w_alpha_f, v_b_alpha_f, v_w_alpha_b, v_b_alpha_b, v_gla_norm_g, v_w_gla_out, v_w_out, v_g_ffn2, v_w2_gu, v_w2_down, v_g_final):
    B, L, D = x.shape
    Lc = ctx.shape[1]
    T, Tc = B * L, B * Lc
    Tall = T + Tc
    F = w1_down.shape[1] * N_DEV
    DK, DV = D // (2 * HEADS), D // HEADS
    QK = HEADS * DK
    PW = 7 * D + LR_PAD
    tm = ROW_TILE
    tpe = L // tm
    nx, nall = T // tm, Tall // tm
    me = 4 * lax.axis_index("x") + 2 * lax.axis_index("y") + lax.axis_index("c")

    rw_all = dict(tm=tm, n_tiles=nall, tpe=tpe, nx_tiles=nx, n_ex=B + 1)
    rw_x = dict(tm=tm, n_tiles=nx, tpe=tpe, nx_tiles=nx, n_ex=B)

    w1gu_g, dww_g, waf_g, wab_g, c_g = _exchange([w1_gu[0].astype(bf16), dw_weight[0], w_alpha_f[0], w_alpha_b[0], c], False, "gather_first")

    def cols(gat):
        return jnp.transpose(gat, (1, 0, 2)).reshape(gat.shape[1], N_DEV * gat.shape[2])

    def rows_(gat):
        return gat.reshape(N_DEV * gat.shape[1], gat.shape[2])

    W1gu = cols(w1gu_g)
    dww = cols(dww_g)
    WA = jnp.zeros((LR_PAD, 2 * QK), f32).at[:LOWRANK, :QK].set(cols(waf_g)).at[LOWRANK:2 * LOWRANK, QK:].set(cols(wab_g)).astype(bf16)
    BA = jnp.concatenate([b_alpha_f, b_alpha_b], axis=1)
    c_all = c_g.reshape(N_DEV * B, D)
    c_ctx2 = c_ctx.reshape(1, D)

    ncm = w_mod.shape[2]
    b_mod_loc = lax.dynamic_slice(b_mod, (0, me * ncm), (1, ncm))
    mod_loc = _mod_fwd(c_all, c_ctx2, w_mod[0], b_mod_loc, "mod_fwd")
    (mod_g,) = _exchange([mod_loc], False, "gather_mod")
    mod_full = cols(mod_g)
    mod_tab = jnp.concatenate([lax.dynamic_slice(mod_full, (me * B, 0), (B, N_MOD * D)), mod_full[N_DEV * B:N_DEV * B + 1]], axis=0)
    mods = [mod_tab[:, i * D:(i + 1) * D].reshape(B + 1, 1, D) for i in range(N_MOD)]
    mods_x = [mm[:B] for mm in mods]

    x_all = jnp.concatenate([x.reshape(T, D), ctx.reshape(Tc, D)], axis=0)

    def f_ffn_in(tok, ex, sh):
        return [_rms_mod(tok[0], sh[0], ex[0], ex[1])], [], []

    (u1,) = _rowwise(f_ffn_in, name="ffn1_in", tok_in=[(x_all, D, 0, False)], ex_in=[mods[0], mods[1]], sh_in=[g_ffn1],
                     tok_out=[(D, bf16)], **rw_all)
    (gu1, h1), (w1d_g, win_g) = _ffn_up(u1, W1gu, "ffn1_up", carry=([w1_down[0].astype(bf16), w_in[0].astype(bf16)], False))
    W1d = rows_(w1d_g)
    lr2 = 2 * LOWRANK
    segs = [(0, 2 * D, 0), (2 * D, 2 * D + QK, 6 * D), (2 * D + QK, 3 * D, 6 * D + QK), (3 * D, 4 * D, 2 * D), (4 * D, 5 * D, 3 * D),
            (5 * D, 5 * D + lr2, 7 * D), (5 * D + lr2, 6 * D + lr2, 4 * D), (6 * D + lr2, 7 * D + lr2, 5 * D)]
    wc = w_in.shape[2]
    win_parts = []
    for lo, hi, _ in sorted(segs, key=lambda t: t[2]):
        for d in range(N_DEV):
            a0, a1 = max(lo, d * wc), min(hi, (d + 1) * wc)
            if a0 < a1:
                win_parts.append(win_g[d][:, a0 - d * wc:a1 - d * wc])
    Win = jnp.concatenate(win_parts + [jnp.zeros((D, LR_PAD - lr2), bf16)], axis=1)
    f1 = _matmul(h1, W1d, "nn", f32, "ffn1_down")

    def mix_in(xv, fv, gate, sh, sc, g):
        x1 = xv + 0.5 * gate * fv
        return x1, _rms_mod(x1, g, sh, sc)

    def f_mix_in(tok, ex, sh):
        return list(mix_in(tok[0], tok[1], ex[0], ex[1], ex[2], sh[0])), [], []

    x1, um = _rowwise(f_mix_in, name="mix_in", tok_in=[(x_all, D, 0, False), (f1, D, 0, False)], ex_in=[mods[2], mods[3], mods[4]],
                      sh_in=[g_mix], tok_out=[(D, f32), (D, bf16)], **rw_all)
    p_all, (wco_g, wgo_g, wo_g, w2gu_g) = _matmul(
        um, Win, "nn", bf16, "in_proj", tm_cap=512, tn_cap=2432,
        carry=([w_conv_out[0].astype(bf16), w_gla_out[0].astype(bf16), w_out[0].astype(bf16), w2_gu[0].astype(bf16)], False))
    Wco, Wgo, Wo, W2gu = rows_(wco_g), rows_(wgo_g), rows_(wo_g), cols(w2gu_g)

    def log_decay(lr, wa, ba):
        z = _bdot(lr, wa, (1, 0)) + ba
        return _log_sigmoid(z) / TAU

    def f_decay(tok, ex, sh):
        return [log_decay(tok[0], sh[0], sh[1])], [], []

    lr_blk = (p_all, LR_PAD, 7 * D // LR_PAD, False)
    (la_all,) = _rowwise(f_decay, name="log_decay", tok_in=[lr_blk], sh_in=[WA, BA], tok_out=[(2 * QK, f32)], **rw_all)

    zeros_s = jnp.zeros((B, HEADS, DV, DK), f32)
    gla_c = dict(row0=T, nb=B, seq=Lc, D=D)
    gla_x = dict(row0=0, nb=B, seq=L, D=D)
    _, hist_cf, s_f = _gla_fwd(p_all, la_all, zeros_s, rev=False, name="gla_ctx_f", **gla_c)
    _, hist_cb, s_b = _gla_fwd(p_all, la_all, zeros_s, rev=True, name="gla_ctx_b", **gla_c)
    (o_f, hist_f, _), (w2d_g,) = _gla_fwd(p_all, la_all, s_f, rev=False, name="gla_x_f", carry=([w2_down[0].astype(bf16)], False), **gla_x)
    W2d = rows_(w2d_g)
    o_b, hist_b, _ = _gla_fwd(p_all, la_all, s_b, rev=True, name="gla_x_b", **gla_x)

    cz = _conv_fwd(p_all, dww, dw_bias, B=B, L=L, D=D, name="conv_fwd")

    def ln_silu(z, g, b):
        mu = jnp.mean(z, axis=-1, keepdims=True)
        var = jnp.mean(jnp.square(z - mu), axis=-1, keepdims=True)
        return jax.nn.silu((z - mu) * lax.rsqrt(var + EPS) * g + b)

    def f_ln(tok, ex, sh):
        return [ln_silu(tok[0], sh[0], sh[1])], [], []

    (zc,) = _rowwise(f_ln, name="conv_ln", tok_in=[(cz, D, 0, False)], sh_in=[conv_ln_g, conv_ln_b], tok_out=[(D, bf16)], **rw_x)
    yc = _matmul(zc, Wco, "nn", bf16, "conv_out")

    def gla_out(of, ob, og, gn):
        return _head_rms(of + ob, DV) * gn * jax.nn.silu(og.astype(f32))

    def f_gla_out(tok, ex, sh):
        return [gla_out(tok[0], tok[1], tok[2], sh[0])], [], []

    og_blk = (p_all, D, 3, False)
    (og2,) = _rowwise(f_gla_out, name="gla_norm", tok_in=[(o_f, D, 0, False), (o_b, D, 0, False), og_blk], sh_in=[gla_norm_g],
                      tok_out=[(D, bf16)], **rw_x)
    yg = _matmul(og2, Wgo, "nn", bf16, "gla_out")

    def merge(ga, gb, ycv, ygv):
        return jax.nn.sigmoid(ga.astype(f32)) * ycv.astype(f32) + jax.nn.sigmoid(gb.astype(f32)) * ygv.astype(f32)

    def f_merge(tok, ex, sh):
        return [merge(*tok)], [], []

    ga_blk, gb_blk = (p_all, D, 4, False), (p_all, D, 5, False)
    (mg,) = _rowwise(f_merge, name="merge", tok_in=[ga_blk, gb_blk, (yc, D, 0, False), (yg, D, 0, False)], tok_out=[(D, bf16)], **rw_x)
    mix = _matmul(mg, Wo, "nn", f32, "mix_out")

    def ffn2_in(x1v, mixv, g5, sh, sc, g):
        x2 = x1v + g5 * mixv
        return x2, _rms_mod(x2, g, sh, sc)

    def f_ffn2_in(tok, ex, sh):
        return list(ffn2_in(tok[0], tok[1], ex[0], ex[1], ex[2], sh[0])), [], []

    x2, u2 = _rowwise(f_ffn2_in, name="ffn2_in", tok_in=[(x1, D, 0, False), (mix, D, 0, False)], ex_in=[mods_x[5], mods_x[6], mods_x[7]],
                      sh_in=[g_ffn2], tok_out=[(D, f32), (D, bf16)], **rw_x)
    gu2, h2 = _ffn_up(u2, W2gu, "ffn2_up")
    f2 = _matmul(h2, W2d, "nn", f32, "ffn2_down")

    gf2 = g_final.reshape(1, D)

    def head_loss(x2v, f2v, g8, gf, tgt):
        x3 = x2v + 0.5 * g8 * f2v
        y = x3 * lax.rsqrt(jnp.mean(x3 * x3, axis=-1, keepdims=True) + EPS) * gf
        return 0.5 * jnp.sum(jnp.mean(jnp.square(y - tgt), axis=-1))

    def f_head(tok, ex, sh):
        loss, vjp = jax.vjp(lambda a, b_, c_, d_: head_loss(a, b_, c_, d_, tok[2]), tok[0], tok[1], ex[0], sh[0])
        dx3, df2, dg8, dgf = vjp(jnp.ones((), f32))
        return [dx3, df2], [dg8], [dgf, jnp.broadcast_to(loss.reshape(1, 1), (1, 128))]

    dx3, df2, dg8, dgf, loss_p = _rowwise(
        f_head, name="head", tok_in=[(x2, D, 0, False), (f2, D, 0, False), (loss_target.reshape(T, D), D, 0, False)], ex_in=[mods_x[8]],
        sh_in=[gf2], tok_out=[(D, f32), (D, bf16)], ex_out=[D], gl_out=[(1, D), (1, 128)], **rw_x)

    dgu2 = _ffn_down_dx(df2, W2d, gu2, "ffn2_down_dx")
    gW2d = _matmul(h2, df2, "tn", f32, "ffn2_down_dw", tm_cap=1408)
    du2 = _matmul(dgu2, W2gu, "nt", f32, "ffn2_up_dx", halves="a")
    gW2gu = _matmul(u2, dgu2, "tn", f32, "ffn2_up_dw", halves="b")

    def f_ffn2_in_bwd(tok, ex, sh):
        _, vjp = jax.vjp(ffn2_in, tok[0], tok[1], ex[0], ex[1], ex[2], sh[0])
        dx2, dmix, dg5, dsh, dsc, dg = vjp((tok[3], tok[2]))
        return [dx2, dmix], [dg5, dsh, dsc], [dg]

    dx2, dmix, dg5, dsh6, dsc7, dg_ffn2 = _rowwise(
        f_ffn2_in_bwd, name="ffn2_in_bwd", tok_in=[(x1, D, 0, False), (mix, D, 0, False), (du2, D, 0, False), (dx3, D, 0, False)],
        ex_in=[mods_x[5], mods_x[6], mods_x[7]], sh_in=[g_ffn2], tok_out=[(D, f32), (D, bf16)], ex_out=[D, D, D], gl_out=[(1, D)], **rw_x)

    dmg = _matmul(dmix, Wo, "nt", bf16, "mix_out_dx")
    gWo = _matmul(mg, dmix, "tn", f32, "mix_out_dw")

    def f_merge_bwd(tok, ex, sh):
        _, vjp = jax.vjp(merge, *[t.astype(f32) for t in tok[:4]])
        dga, dgb, dyc, dyg = vjp(tok[4].astype(f32))
        return [dga, dgb, dyc, dyg], [], []

    dga, dgb, dyc, dyg = _rowwise(f_merge_bwd, name="merge_bwd",
                                  tok_in=[ga_blk, gb_blk, (yc, D, 0, False), (yg, D, 0, False), (dmg, D, 0, False)],
                                  tok_out=[(D, bf16)] * 4, **rw_x)
    dzc = _matmul(dyc, Wco, "nt", f32, "conv_out_dx")
    gWco = _matmul(zc, dyc, "tn", f32, "conv_out_dw")
    dog2 = _matmul(dyg, Wgo, "nt", f32, "gla_out_dx")
    gWgo = _matmul(og2, dyg, "tn", f32, "gla_out_dw")

    def f_ln_bwd(tok, ex, sh):
        _, vjp = jax.vjp(ln_silu, tok[0], sh[0], sh[1])
        dcz, dg, db = vjp(tok[1])
        return [dcz], [], [dg, db, jnp.sum(dcz, axis=0, keepdims=True)]

    dcz, g_ln_g, g_ln_b, g_dwb = _rowwise(f_ln_bwd, name="conv_ln_bwd", tok_in=[(cz, D, 0, False), (dzc, D, 0, False)],
                                          sh_in=[conv_ln_g, conv_ln_b], tok_out=[(D, f32)], gl_out=[(1, D)] * 3, **rw_x)
    def col_shards(g):
        return jnp.transpose(g.reshape(g.shape[0], N_DEV, g.shape[1] // N_DEV), (1, 0, 2)).astype(bf16)

    def row_shards(g):
        return g.reshape(N_DEV, g.shape[0] // N_DEV, g.shape[1]).astype(bf16)

    (dca, dcb, g_dww), (r_w2d, r_w2gu, r_wo, r_wco, r_wgo) = _conv_bwd(
        p_all, dcz, dww, B=B, L=L, D=D, name="conv_bwd",
        carry=([row_shards(gW2d), col_shards(gW2gu), row_shards(gWo), row_shards(gWco), row_shards(gWgo)], True))

    def f_gla_out_bwd(tok, ex, sh):
        _, vjp = jax.vjp(gla_out, tok[0], tok[1], tok[2].astype(f32), sh[0])
        dof, _, dog, dgn = vjp(tok[3])
        return [dof, dog], [], [dgn]

    d_o, dog, g_gn = _rowwise(f_gla_out_bwd, name="gla_norm_bwd",
                              tok_in=[(o_f, D, 0, False), (o_b, D, 0, False), og_blk, (dog2, D, 0, False)], sh_in=[gla_norm_g],
                              tok_out=[(D, f32), (D, bf16)], gl_out=[(1, D)], **rw_x)

    dq_f, dk_f, dv_f, dla_f, ds_f = _gla_bwd(p_all, la_all, hist_f, d_o, zeros_s, rev=False, name="gla_x_f_bwd", **gla_x)
    dq, dk, dv, dla_b, ds_b = _gla_bwd(p_all, la_all, hist_b, d_o, zeros_s, rev=True, name="gla_x_b_bwd", add=(dq_f, dk_f, dv_f), **gla_x)
    dq_cf, dk_cf, dv_cf, dla_cf, _ = _gla_bwd(p_all, la_all, hist_cf, None, ds_f, rev=False, name="gla_ctx_f_bwd", **gla_c)
    _, dk_c, dv_c, dla_cb, _ = _gla_bwd(p_all, la_all, hist_cb, None, ds_b, rev=True, name="gla_ctx_b_bwd", add=(dq_cf, dk_cf, dv_cf), **gla_c)

    dla_all = jnp.concatenate([jnp.concatenate([dla_f, dla_b], axis=1), jnp.concatenate([dla_cf, dla_cb], axis=1)], axis=0)

    def f_decay_bwd(tok, ex, sh):
        _, vjp = jax.vjp(log_decay, tok[0].astype(f32), sh[0].astype(f32), sh[1])
        dlr, dwa, dba = vjp(tok[1])
        return [dlr], [], [dwa, dba]

    dlr, g_WA, g_BA = _rowwise(f_decay_bwd, name="log_decay_bwd", tok_in=[lr_blk, (dla_all, 2 * QK, 0, False)], sh_in=[WA, BA],
                               tok_out=[(LR_PAD, bf16)], gl_out=[(LR_PAD, 2 * QK), (1, 2 * QK)], **rw_all)

    zc_ = functools.partial(jnp.zeros, dtype=bf16)
    dp_x = jnp.concatenate([dca, dcb, dv, dog, dga, dgb, dq, dk, dlr[:T]], axis=1)
    dp_c = jnp.concatenate([zc_((Tc, 2 * D)), dv_c, zc_((Tc, 3 * D)), zc_((Tc, QK)), dk_c, dlr[T:]], axis=1)
    dp_all = jnp.concatenate([dp_x, dp_c], axis=0)
    gWin_p = _matmul(um, dp_all, "tn", f32, "in_proj_dw", tm_cap=512, tn_cap=2432)
    gwin_shards = []
    for d in range(N_DEV):
        parts = []
        for lo, hi, po in segs:
            a0, a1 = max(lo, d * wc), min(hi, (d + 1) * wc)
            if a0 < a1:
                parts.append(gWin_p[:, po + a0 - lo:po + a1 - lo])
        gwin_shards.append(jnp.concatenate(parts, axis=1))
    dum, (r_win,) = _matmul(dp_all, Win, "nt", f32, "in_proj_dx", tk_cap=2432, carry=([jnp.stack(gwin_shards).astype(bf16)], True))

    def f_mix_in_bwd(tok, ex, sh):
        _, vjp = jax.vjp(mix_in, tok[0], tok[1], ex[0], ex[1], ex[2], sh[0])
        dx1, df1, dgate, dsh, dsc, dg = vjp((tok[3], tok[2]))
        return [dx1, df1], [dgate, dsh, dsc], [dg]

    dx1, df1, dg2, dsh3, dsc4, dg_mix = _rowwise(
        f_mix_in_bwd, name="mix_in_bwd", tok_in=[(x_all, D, 0, False), (f1, D, 0, False), (dum, D, 0, False), (dx2, D, 0, True)],
        ex_in=[mods[2], mods[3], mods[4]], sh_in=[g_mix], tok_out=[(D, f32), (D, bf16)], ex_out=[D, D, D], gl_out=[(1, D)], **rw_all)

    dgu1 = _ffn_down_dx(df1, W1d, gu1, "ffn1_down_dx")
    gW1d = _matmul(h1, df1, "tn", f32, "ffn1_down_dw", tm_cap=1408)
    gW1gu, (r_w1d,) = _matmul(u1, dgu1, "tn", f32, "ffn1_up_dw", carry=([row_shards(gW1d)], True), halves="b")
    du1, (r_w1gu,) = _matmul(dgu1, W1gu, "nt", f32, "ffn1_up_dx", carry=([col_shards(gW1gu)], True), halves="a")

    def f_ffn_in_bwd(tok, ex, sh):
        _, vjp = jax.vjp(_rms_mod, tok[0], sh[0], ex[0], ex[1])
        dx, dg, dsh, dsc = vjp(tok[1])
        return [dx + tok[2]], [dsh, dsc], [dg]

    dx_all, dsh0, dsc1, dg_ffn1 = _rowwise(
        f_ffn_in_bwd, name="ffn1_in_bwd", tok_in=[(x_all, D, 0, False), (du1, D, 0, False), (dx1, D, 0, False)],
        ex_in=[mods[0], mods[1]], sh_in=[g_ffn1], tok_out=[(D, f32)], ex_out=[D, D], gl_out=[(1, D)], **rw_all)
    grad_x = dx_all[:T].reshape(B, L, D)

    zrow = jnp.zeros((1, 1, D), f32)
    dmod_loc = jnp.concatenate([dsh0, dsc1, dg2, dsh3, dsc4] + [jnp.concatenate([t, zrow], axis=0) for t in (dg5, dsh6, dsc7, dg8)],
                               axis=2).reshape(B + 1, N_MOD * D)
    small = [loss_p, dg_ffn1, dg_mix, g_dww[:CONV_W].reshape(1, CONV_W * D), g_dwb, g_ln_g, g_ln_b,
             g_WA[:LOWRANK, :QK].reshape(1, LOWRANK * QK), g_BA[:, :QK], g_WA[LOWRANK:2 * LOWRANK, QK:].reshape(1, LOWRANK * QK), g_BA[:, QK:],
             g_gn, dg_ffn2, dgf]
    small_w = [s.shape[1] for s in small]
    def to8(v):
        n_pad = -(-v.shape[1] // 1024) * 1024
        return jnp.pad(v, ((0, 0), (0, n_pad - v.shape[1]))).reshape(8, n_pad // 8)

    def from8(a, n):
        return a.reshape(1, a.size)[:, :n]

    dmod_g, small_g = _exchange([dmod_loc, to8(jnp.concatenate(small, axis=1))], False, "gather_small")
    dmx = dmod_g[:, :B].reshape(N_DEV * B, N_MOD * D)
    dmc = dmod_g[:, B]
    gWmod, gcc_p = _mod_bwd(c_all, c_ctx2, w_mod[0], lax.dynamic_slice(dmx, (0, me * ncm), (N_DEV * B, ncm)),
                            lax.dynamic_slice(dmc, (0, me * ncm), (N_DEV, ncm)), "mod_bwd")

    rs_out = [r_w1gu, r_w1d, r_win, r_wco, r_wgo, r_wo, r_w2gu, r_w2d]
    (gcc_g,) = _exchange([to8(gcc_p)], False, "gather_cctx")

    sums, g_cc, g_bmod = _sum_sources([small_g, gcc_g, jnp.concatenate([dmx, dmc], axis=0).reshape(N_DEV * (B + 1), 8, N_MOD * D // 8)], "sum_small")
    sums, g_cc, g_bmod = from8(sums, sum(small_w)), from8(g_cc, D), from8(g_bmod, N_MOD * D)
    offs = [0]
    for wd in small_w:
        offs.append(offs[-1] + wd)
    sm = [sums[:, offs[i]:offs[i + 1]] for i in range(len(small))]
    loss = sm[0][0, 0]
    ncd, nca = dw_weight.shape[2], w_alpha_f.shape[2]
    g_dww_loc = lax.dynamic_slice(sm[3].reshape(CONV_W, D), (0, me * ncd), (CONV_W, ncd)).reshape(1, CONV_W * ncd)
    g_waf_loc = lax.dynamic_slice(sm[7].reshape(LOWRANK, QK), (0, me * nca), (LOWRANK, nca)).reshape(1, LOWRANK * nca)
    g_wab_loc = lax.dynamic_slice(sm[9].reshape(LOWRANK, QK), (0, me * nca), (LOWRANK, nca)).reshape(1, LOWRANK * nca)

    big = {}
    for nm, wv, mv, vv, part in (("w1_gu", w1_gu, m_w1_gu, v_w1_gu, rs_out[0]), ("w1_down", w1_down, m_w1_down, v_w1_down, rs_out[1]),
                                 ("w_in", w_in, m_w_in, v_w_in, rs_out[2]), ("w_conv_out", w_conv_out, m_w_conv_out, v_w_conv_out, rs_out[3]),
                                 ("w_gla_out", w_gla_out, m_w_gla_out, v_w_gla_out, rs_out[4]), ("w_out", w_out, m_w_out, v_w_out, rs_out[5]),
                                 ("w2_gu", w2_gu, m_w2_gu, v_w2_gu, rs_out[6]), ("w2_down", w2_down, m_w2_down, v_w2_down, rs_out[7])):
        big[nm] = [t[None] for t in _adamw(wv[0], mv[0], vv[0], part, "adamw_" + nm, True)]
    big["w_mod"] = [t[None] for t in _adamw(w_mod[0], m_w_mod[0], v_w_mod[0], gWmod, "adamw_w_mod", False)]

    small_params = [("c_ctx", c_ctx, m_c_ctx, v_c_ctx, g_cc), ("b_mod", b_mod, m_b_mod, v_b_mod, g_bmod), ("g_ffn1", g_ffn1, m_g_ffn1, v_g_ffn1, sm[1]),
                    ("g_mix", g_mix, m_g_mix, v_g_mix, sm[2]), ("dw_weight", dw_weight, m_dw_weight, v_dw_weight, g_dww_loc),
                    ("dw_bias", dw_bias, m_dw_bias, v_dw_bias, sm[4]), ("conv_ln_g", conv_ln_g, m_conv_ln_g, v_conv_ln_g, sm[5]),
                    ("conv_ln_b", conv_ln_b, m_conv_ln_b, v_conv_ln_b, sm[6]), ("w_alpha_f", w_alpha_f, m_w_alpha_f, v_w_alpha_f, g_waf_loc),
                    ("b_alpha_f", b_alpha_f, m_b_alpha_f, v_b_alpha_f, sm[8]), ("w_alpha_b", w_alpha_b, m_w_alpha_b, v_w_alpha_b, g_wab_loc),
                    ("b_alpha_b", b_alpha_b, m_b_alpha_b, v_b_alpha_b, sm[10]), ("gla_norm_g", gla_norm_g, m_gla_norm_g, v_gla_norm_g, sm[11]),
                    ("g_ffn2", g_ffn2, m_g_ffn2, v_g_ffn2, sm[12]), ("g_final", g_final, m_g_final, v_g_final, sm[13])]
    flat = lambda t: t.reshape(1, t.size)
    pw, pm, pv, pg = (jnp.concatenate([flat(sp[i]) for sp in small_params], axis=1) for i in (1, 2, 3, 4))
    n_small = pw.shape[1]
    s_g, s_d, s_m, s_v = (from8(t, n_small) for t in _adamw(to8(pw), to8(pm), to8(pv), to8(pg), "adamw_small", False))
    small_out, o0 = {}, 0
    for nm, wv, _, _, _ in small_params:
        small_out[nm] = [t[:, o0:o0 + wv.size].reshape(wv.shape) for t in (s_g, s_d, s_m, s_v)]
        o0 += wv.size

    order = ["c_ctx", "w_mod", "b_mod", "g_ffn1", "w1_gu", "w1_down", "g_mix", "w_in", "dw_weight", "dw_bias", "conv_ln_g", "conv_ln_b",
             "w_conv_out", "w_alpha_f", "b_alpha_f", "w_alpha_b", "b_alpha_b", "gla_norm_g", "w_gla_out", "w_out", "g_ffn2", "w2_gu",
             "w2_down", "g_final"]
    res = {**big, **small_out}
    return (loss, grad_x, *[res[n][0] for n in order], *[res[n][1] for n in order], *[res[n][2] for n in order], *[res[n][3] for n in order])
```

```python
import functools

import jax
import jax.numpy as jnp
from jax import lax
from jax.experimental import pallas as pl
from jax.experimental.pallas import tpu as pltpu

f32, bf16 = jnp.float32, jnp.bfloat16

N_DEV = 8
HEADS = 4
LOWRANK = 16
CONV_W = 31
CONV_PAD = 16
SUBLANES = 8
CHUNK = 64
SUB = 16
GLA_ROWS = 256
GLA_SAFE_DECAY = 60.0
TAU = 16.0
EPS = 1e-6
N_MOD = 9
LR_PAD = 128
ROW_TILE = 256
V7X_VMEM_BYTES = 64 << 20
VMEM_LIMIT = (V7X_VMEM_BYTES * 3) // 4

ADAM_LR, ADAM_B1, ADAM_B2, ADAM_EPS, ADAM_WD, ADAM_STEP = 0.001, 0.9, 0.999, 1e-08, 0.01, 10

MESH = pl.DeviceIdType.MESH


def _pc(body, **kw):
    return pl.pallas_call(body, **kw)


def _params(*sem):
    return pltpu.CompilerParams(dimension_semantics=sem, vmem_limit_bytes=VMEM_LIMIT)


def _pick(n, cap, unit=128):
    best = None
    for t in range(unit, min(n, cap) + 1, unit):
        if n % t == 0:
            best = t
    return best or n


def _matmul(a, b, mode, out_dtype, name, tm_cap=1024, tn_cap=1536, tk_cap=None, carry=None, halves=None):
    tk_cap = tk_cap or (2048 if mode == "tn" else 2816)
    if halves == "a":
        (_, M, Kh), N = a.shape, b.shape[0]
        K, tk = 2 * Kh, _pick(Kh, tk_cap)
        tm, tn = _pick(M, tm_cap), _pick(N, tn_cap)
        a_spec = pl.BlockSpec((None, tm, tk), lambda i, j, k: (k // (Kh // tk), i, k % (Kh // tk)))
    elif halves == "b":
        (K, M), (_, _, Nh) = a.shape, b.shape
        N, tn = 2 * Nh, _pick(Nh, tn_cap)
        tm, tk = _pick(M, tm_cap), _pick(K, tk_cap)
    else:
        if mode == "tn":
            (K, M), N = a.shape, b.shape[1]
        elif mode == "nt":
            (M, K), N = a.shape, b.shape[0]
        else:
            (M, K), N = a.shape, b.shape[1]
        tm, tn, tk = _pick(M, tm_cap), _pick(N, tn_cap), _pick(K, tk_cap)
    nk = K // tk
    if halves != "a":
        a_spec = pl.BlockSpec((tk, tm), lambda i, j, k: (k, i)) if mode == "tn" else pl.BlockSpec((tm, tk), lambda i, j, k: (i, k))
    if halves == "b":
        b_spec = pl.BlockSpec((None, tk, tn), lambda i, j, k: (j // (Nh // tn), k, j % (Nh // tn)))
    else:
        b_spec = pl.BlockSpec((tn, tk), lambda i, j, k: (j, k)) if mode == "nt" else pl.BlockSpec((tk, tn), lambda i, j, k: (k, j))
    dims = {"nn": ((1,), (0,)), "nt": ((1,), (1,)), "tn": ((0,), (0,))}[mode]

    def body_single(a_ref, b_ref, o_ref):
        o_ref[...] = lax.dot_general(a_ref[...].astype(bf16), b_ref[...].astype(bf16), (dims, ((), ())),
                                     preferred_element_type=f32).astype(out_dtype)

    def body(a_ref, b_ref, o_ref, acc_ref):
        k = pl.program_id(2)
        part = lax.dot_general(a_ref[...].astype(bf16), b_ref[...].astype(bf16), (dims, ((), ())), preferred_element_type=f32)

        @pl.when(k == 0)
        def _():
            acc_ref[...] = part

        @pl.when(k > 0)
        def _():
            acc_ref[...] += part

        @pl.when(k == nk - 1)
        def _():
            o_ref[...] = acc_ref[...].astype(out_dtype)

    (out,), carried = _call(
        body_single if nk == 1 else body, name=name, grid=(M // tm, N // tn, nk), in_specs=[a_spec, b_spec],
        out_specs=[pl.BlockSpec((tm, tn), lambda i, j, k: (i, j))], out_shape=[jax.ShapeDtypeStruct((M, N), out_dtype)],
        scratch_shapes=[] if nk == 1 else [pltpu.VMEM((tm, tn), f32)], sem=("parallel", "parallel", "arbitrary"),
        args=(a, b), carry=carry)
    return out if carry is None else (out, carried)


def _ffn_up(u, Wgu, name, carry=None):
    M, K = u.shape
    F = Wgu.shape[1] // 2
    tm, tn = _pick(M, 512), _pick(F, 1408)
    nj = F // tn

    def body(u_ref, wa_ref, wb_ref, gu_ref, h_ref):
        uv = u_ref[...]
        a = jnp.dot(uv, wa_ref[...], preferred_element_type=f32)
        b = jnp.dot(uv, wb_ref[...], preferred_element_type=f32)
        gu_ref[0] = a.astype(bf16)
        gu_ref[1] = b.astype(bf16)
        h_ref[...] = (jax.nn.silu(a) * b).astype(bf16)

    res, carried = _call(
        body, name=name, grid=(nj, M // tm),
        in_specs=[pl.BlockSpec((tm, K), lambda j, i: (i, 0)), pl.BlockSpec((K, tn), lambda j, i: (0, j)),
                  pl.BlockSpec((K, tn), lambda j, i: (0, nj + j))],
        out_specs=[pl.BlockSpec((2, tm, tn), lambda j, i: (0, i, j)), pl.BlockSpec((tm, tn), lambda j, i: (i, j))],
        out_shape=[jax.ShapeDtypeStruct((2, M, F), bf16), jax.ShapeDtypeStruct((M, F), bf16)],
        scratch_shapes=[], sem=("parallel", "parallel"), args=(u, Wgu, Wgu), carry=carry)
    return res if carry is None else (res, carried)


def _ffn_down_dx(df, Wd, gu, name):
    M, D = df.shape
    F = Wd.shape[0]
    tm, tn = _pick(M, 512), _pick(F, 1408)

    def body(df_ref, w_ref, gu_ref, o_ref):
        dh = lax.dot_general(df_ref[...], w_ref[...], (((1,), (1,)), ((), ())), preferred_element_type=f32)
        a, b = gu_ref[0].astype(f32), gu_ref[1].astype(f32)
        sg = jax.nn.sigmoid(a)
        o_ref[0] = (dh * b * sg * (1.0 + a * (1.0 - sg))).astype(bf16)
        o_ref[1] = (dh * a * sg).astype(bf16)

    return _pc(
        body, name=name, grid=(F // tn, M // tm),
        in_specs=[pl.BlockSpec((tm, D), lambda j, i: (i, 0)), pl.BlockSpec((tn, D), lambda j, i: (j, 0)),
                  pl.BlockSpec((2, tm, tn), lambda j, i: (0, i, j))],
        out_specs=pl.BlockSpec((2, tm, tn), lambda j, i: (0, i, j)), out_shape=jax.ShapeDtypeStruct((2, M, F), bf16),
        compiler_params=_params("parallel", "parallel"))(df, Wd, gu)


def _rowwise(fn, *, name, tm, n_tiles, tpe, nx_tiles, n_ex, tok_in=(), ex_in=(), sh_in=(), tok_out=(), ex_out=(), gl_out=()):
    def seg(i):
        return jnp.minimum(i // tpe, n_ex - 1)

    in_specs, args = [], []
    for arr, w, cb, x_only in tok_in:
        if x_only:
            in_specs.append(pl.BlockSpec((tm, w), functools.partial(lambda i, cb: (jnp.minimum(i, nx_tiles - 1), cb), cb=cb)))
        else:
            in_specs.append(pl.BlockSpec((tm, w), functools.partial(lambda i, cb: (i, cb), cb=cb)))
        args.append(arr)
    for arr in ex_in:
        in_specs.append(pl.BlockSpec((1, 1, arr.shape[-1]), lambda i: (seg(i), 0, 0)))
        args.append(arr)
    for arr in sh_in:
        in_specs.append(pl.BlockSpec(arr.shape, functools.partial(lambda i, nd: (0,) * nd, nd=arr.ndim)))
        args.append(arr)
    out_specs, out_shape = [], []
    for w, dt in tok_out:
        out_specs.append(pl.BlockSpec((tm, w), lambda i: (i, 0)))
        out_shape.append(jax.ShapeDtypeStruct((n_tiles * tm, w), dt))
    for w in ex_out:
        out_specs.append(pl.BlockSpec((1, 1, w), lambda i: (seg(i), 0, 0)))
        out_shape.append(jax.ShapeDtypeStruct((n_ex, 1, w), f32))
    for r, w in gl_out:
        out_specs.append(pl.BlockSpec((r, w), lambda i: (0, 0)))
        out_shape.append(jax.ShapeDtypeStruct((r, w), f32))
    n_tok, n_exi, n_sh = len(tok_in), len(ex_in), len(sh_in)
    n_to, n_eo = len(tok_out), len(ex_out)
    x_only_flags = [t[3] for t in tok_in]

    def body(*refs):
        i = pl.program_id(0)
        ins, outs = refs[: n_tok + n_exi + n_sh], refs[n_tok + n_exi + n_sh:]
        is_x = i < nx_tiles
        tok_vals = []
        for r, xo in zip(ins[:n_tok], x_only_flags):
            v = r[...]
            tok_vals.append(jnp.where(is_x, v, jnp.zeros_like(v)) if xo else v)
        ex_vals = [r[0] for r in ins[n_tok:n_tok + n_exi]]
        sh_vals = [r[...] for r in ins[n_tok + n_exi:]]
        t_o, e_o, g_o = fn(tok_vals, ex_vals, sh_vals)
        for r, v in zip(outs[:n_to], t_o):
            r[...] = v.astype(r.dtype)
        first = jnp.logical_and(i % tpe == 0, i <= nx_tiles)
        for r, v in zip(outs[n_to:n_to + n_eo], e_o):
            @pl.when(first)
            def _(r=r, v=v):
                r[0] = v

            @pl.when(jnp.logical_not(first))
            def _(r=r, v=v):
                r[0] += v
        for r, v in zip(outs[n_to + n_eo:], g_o):
            @pl.when(i == 0)
            def _(r=r, v=v):
                r[...] = v

            @pl.when(i > 0)
            def _(r=r, v=v):
                r[...] += v

    res = _pc(body, name=name, grid=(n_tiles,), in_specs=in_specs, out_specs=out_specs, out_shape=out_shape,
              compiler_params=_params("arbitrary"))(*args)
    return list(res)


def _rms_mod(x, g, sh, sc):
    y = x * lax.rsqrt(jnp.mean(x * x, axis=-1, keepdims=True) + EPS) * g
    return y * (1.0 + sc) + sh


def _log_sigmoid(z):
    return jnp.minimum(z, 0.0) - jnp.log(1.0 + jnp.exp(-jnp.abs(z)))


def _head_rms(o, DV):
    parts = []
    for h in range(HEADS):
        oh = o[:, h * DV:(h + 1) * DV]
        parts.append(oh * lax.rsqrt(jnp.mean(oh * oh, axis=-1, keepdims=True) + EPS))
    return jnp.concatenate(parts, axis=1)


@functools.partial(jax.custom_vjp, nondiff_argnums=(2,))
def _bdot(a, b, dims):
    return lax.dot_general(a.astype(bf16), b.astype(bf16), (((dims[0],), (dims[1],)), ((), ())), preferred_element_type=f32)


def _bdot_fwd(a, b, dims):
    return _bdot(a, b, dims), (a, b)


def _bdot_bwd(dims, res, g):
    a, b = res
    ca, cb = dims
    da = _bdot(g, b, (1, 1 - cb)) if ca == 1 else _bdot(b, g, (1 - cb, 1))
    db = _bdot(a, g, (1 - ca, 0)) if cb == 0 else _bdot(g, a, (0, 1 - ca))
    return da, db


_bdot.defvjp(_bdot_fwd, _bdot_bwd)


def _split_dot(m, x, dims):
    mb, rem, acc = m.astype(bf16), x, None
    for _ in range(3):
        piece = rem.astype(bf16)
        rem = rem - piece.astype(f32)
        part = lax.dot_general(mb, piece, (((dims[0],), (dims[1],)), ((), ())), preferred_element_type=f32)
        acc = part if acc is None else acc + part
    return acc


@jax.custom_vjp
def _tri_cumsum(tri, g):
    return _split_dot(tri, g, (1, 0))


def _tri_cumsum_fwd(tri, g):
    return _tri_cumsum(tri, g), tri


def _tri_cumsum_bwd(tri, db):
    return jnp.zeros_like(tri), _split_dot(tri, db, (0, 0))


_tri_cumsum.defvjp(_tri_cumsum_fwd, _tri_cumsum_bwd)


def _gla_chunk(St, q, k, v, g, *, rev, scale, exact):
    C, DK = q.shape
    r = lax.broadcasted_iota(jnp.int32, (C, C), 0)
    c = lax.broadcasted_iota(jnp.int32, (C, C), 1)
    causal = (r <= c) if rev else (r >= c)
    b = _tri_cumsum(causal.astype(f32), g)
    qs = q * scale
    qe = qs * jnp.exp(b)
    inter = _bdot(qe, St, (1, 1))
    b_last = b[0:1] if rev else b[C - 1:C]
    kd = k * jnp.exp(b_last - b)
    St_new = St * jnp.exp(b_last) + _bdot(v, kd, (0, 0))
    if not exact:
        att = jnp.where(causal, _bdot(qe, k * jnp.exp(-b), (1, 1)), 0.0)
        return St_new, inter + _bdot(att, v, (1, 0))
    rr = lax.broadcasted_iota(jnp.int32, (SUB, SUB, DK), 0)
    cc = lax.broadcasted_iota(jnp.int32, (SUB, SUB, DK), 1)
    m3 = (rr <= cc) if rev else (rr >= cc)
    outs = []
    for i in range(C // SUB):
        lo, hi = i * SUB, (i + 1) * SUB
        bi, qi, ki, vi = b[lo:hi], qs[lo:hi], k[lo:hi], v[lo:hi]
        rel = bi[:, None, :] - bi[None, :, :]
        e = jnp.where(m3, jnp.exp(jnp.where(m3, rel, 0.0)), 0.0)
        att = jnp.sum(qi[:, None, :] * e * ki[None, :, :], axis=-1)
        acc = _bdot(att, vi, (1, 0))
        ref_row = b[hi - 1:hi] if rev else b[lo:lo + 1]
        prev = slice(hi, C) if rev else slice(0, lo)
        if (hi < C) if rev else (lo > 0):
            qn = qi * jnp.exp(bi - ref_row)
            ks = k[prev] * jnp.exp(ref_row - b[prev])
            acc = acc + _bdot(_bdot(qn, ks, (1, 1)), v[prev], (1, 0))
        outs.append(acc)
    return St_new, inter + jnp.concatenate(outs, axis=0)


def _mild_decay(la_ref):
    return jnp.min(la_ref[...]) >= -GLA_SAFE_DECAY / CHUNK


def _gla_specs(D, rev_blocks, row0, seq):
    DK, DV = D // (2 * HEADS), D // HEADS
    nblk = seq // GLA_ROWS
    rb0 = row0 // GLA_ROWS

    def blk(j):
        return (nblk - 1 - j) if rev_blocks else j

    return DK, DV, nblk, rb0, blk


def _gla_in_specs(D, rev, rows):
    QK = D // 2
    return [
        pl.BlockSpec((GLA_ROWS, QK), lambda b, j: (rows(b, j), 6 * D // QK)),
        pl.BlockSpec((GLA_ROWS, QK), lambda b, j: (rows(b, j), 6 * D // QK + 1)),
        pl.BlockSpec((GLA_ROWS, D), lambda b, j: (rows(b, j), 2)),
        pl.BlockSpec((GLA_ROWS, QK), lambda b, j: (rows(b, j), 1 if rev else 0)),
    ]


def _gla_fwd(p_all, la_all, s0, *, rev, row0, nb, seq, D, name, carry=None):
    DK, DV, nblk, rb0, blk = _gla_specs(D, rev, row0, seq)
    cpb = GLA_ROWS // CHUNK

    def rows(b, j):
        return rb0 + b * nblk + blk(j)

    in_specs = _gla_in_specs(D, rev, rows) + [pl.BlockSpec((1, HEADS, DV, DK), lambda b, j: (b, 0, 0, 0))]
    out_specs = [
        pl.BlockSpec((GLA_ROWS, D), lambda b, j: (b * nblk + blk(j), 0)),
        pl.BlockSpec((1, HEADS, cpb, DV, DK), lambda b, j: (b, 0, blk(j), 0, 0)),
        pl.BlockSpec((1, HEADS, DV, DK), lambda b, j: (b, 0, 0, 0)),
    ]
    out_shape = [
        jax.ShapeDtypeStruct((nb * seq, D), f32),
        jax.ShapeDtypeStruct((nb, HEADS, seq // CHUNK, DV, DK), f32),
        jax.ShapeDtypeStruct((nb, HEADS, DV, DK), f32),
    ]
    chunk = functools.partial(_gla_chunk, rev=rev, scale=DK ** -0.5)

    def body(q_ref, k_ref, v_ref, la_ref, s0_ref, o_ref, hist_ref, sfin_ref, st_ref):
        j = pl.program_id(1)

        @pl.when(j == 0)
        def _():
            st_ref[...] = s0_ref[0]

        def step(ci, exact):
            cc = (cpb - 1 - ci) if rev else ci
            sl = pl.ds(cc * CHUNK, CHUNK)
            for h in range(HEADS):
                kq, kv = pl.ds(h * DK, DK), pl.ds(h * DV, DV)
                St = st_ref[h]
                hist_ref[0, h, cc] = St
                St2, o = chunk(St, q_ref[sl, kq].astype(f32), k_ref[sl, kq].astype(f32), v_ref[sl, kv].astype(f32), la_ref[sl, kq],
                               exact=exact)
                o_ref[sl, kv] = o
                st_ref[h] = St2

        mild = _mild_decay(la_ref)
        for exact in (False, True):
            @pl.when(jnp.logical_not(mild) if exact else mild)
            def _(exact=exact):
                for ci in range(cpb):
                    step(ci, exact)

        @pl.when(j == nblk - 1)
        def _():
            sfin_ref[0] = st_ref[...]

    res, carried = _call(body, name=name, grid=(nb, nblk), in_specs=in_specs, out_specs=out_specs, out_shape=out_shape,
                         scratch_shapes=[pltpu.VMEM((HEADS, DV, DK), f32)], sem=("parallel", "arbitrary"),
                         args=(p_all, p_all, p_all, la_all, s0), carry=carry)
    return res if carry is None else (res, carried)


def _gla_bwd(p_all, la_all, hist, do, dsfin, *, rev, row0, nb, seq, D, name, add=None):
    DK, DV, nblk, rb0, blk = _gla_specs(D, not rev, row0, seq)
    cpb = GLA_ROWS // CHUNK
    QK = HEADS * DK
    has_do = do is not None

    def rows(b, j):
        return rb0 + b * nblk + blk(j)

    in_specs = _gla_in_specs(D, rev, rows) + [
        pl.BlockSpec((1, HEADS, cpb, DV, DK), lambda b, j: (b, 0, blk(j), 0, 0)),
        pl.BlockSpec((1, HEADS, DV, DK), lambda b, j: (b, 0, 0, 0)),
    ]
    args = [p_all, p_all, p_all, la_all, hist, dsfin]
    if has_do:
        in_specs.append(pl.BlockSpec((GLA_ROWS, D), lambda b, j: (b * nblk + blk(j), 0)))
        args.append(do)
    if add is not None:
        in_specs += [pl.BlockSpec((GLA_ROWS, t.shape[1]), lambda b, j: (b * nblk + blk(j), 0)) for t in add]
        args += list(add)
    gdt = f32 if add is None else bf16
    out_specs = [
        pl.BlockSpec((GLA_ROWS, QK), lambda b, j: (b * nblk + blk(j), 0)),
        pl.BlockSpec((GLA_ROWS, QK), lambda b, j: (b * nblk + blk(j), 0)),
        pl.BlockSpec((GLA_ROWS, D), lambda b, j: (b * nblk + blk(j), 0)),
        pl.BlockSpec((GLA_ROWS, QK), lambda b, j: (b * nblk + blk(j), 0)),
        pl.BlockSpec((1, HEADS, DV, DK), lambda b, j: (b, 0, 0, 0)),
    ]
    out_shape = [
        jax.ShapeDtypeStruct((nb * seq, QK), gdt), jax.ShapeDtypeStruct((nb * seq, QK), gdt),
        jax.ShapeDtypeStruct((nb * seq, D), gdt), jax.ShapeDtypeStruct((nb * seq, QK), f32),
        jax.ShapeDtypeStruct((nb, HEADS, DV, DK), f32),
    ]
    chunk = functools.partial(_gla_chunk, rev=rev, scale=DK ** -0.5)

    def body(*refs):
        refs = list(refs)
        q_ref, k_ref, v_ref, la_ref, hist_ref, dsfin_ref = refs[:6]
        do_ref = refs[6] if has_do else None
        add_refs = refs[6 + has_do:len(refs) - 6]
        dq_ref, dk_ref, dv_ref, dla_ref, ds0_ref, ds_ref = refs[len(refs) - 6:]
        j = pl.program_id(1)

        @pl.when(j == 0)
        def _():
            ds_ref[...] = dsfin_ref[0]

        def step(ci, exact):
            cc = ci if rev else (cpb - 1 - ci)
            sl = pl.ds(cc * CHUNK, CHUNK)
            for h in range(HEADS):
                kq, kv = pl.ds(h * DK, DK), pl.ds(h * DV, DV)
                prim = (hist_ref[0, h, cc], q_ref[sl, kq].astype(f32), k_ref[sl, kq].astype(f32), v_ref[sl, kv].astype(f32), la_ref[sl, kq])
                _, vjp = jax.vjp(functools.partial(chunk, exact=exact), *prim)
                d_o = do_ref[sl, kv] if has_do else jnp.zeros((CHUNK, DV), f32)
                dSt, dq, dk, dv, dg = vjp((ds_ref[h], d_o))
                if add is not None:
                    dq, dk, dv = dq + add_refs[0][sl, kq], dk + add_refs[1][sl, kq], dv + add_refs[2][sl, kv]
                dq_ref[sl, kq] = dq.astype(gdt)
                dk_ref[sl, kq] = dk.astype(gdt)
                dv_ref[sl, kv] = dv.astype(gdt)
                dla_ref[sl, kq] = dg
                ds_ref[h] = dSt

        mild = _mild_decay(la_ref)
        for exact in (False, True):
            @pl.when(jnp.logical_not(mild) if exact else mild)
            def _(exact=exact):
                for ci in range(cpb):
                    step(ci, exact)

        @pl.when(j == nblk - 1)
        def _():
            ds0_ref[0] = ds_ref[...]

    return _pc(body, name=name, grid=(nb, nblk), in_specs=in_specs, out_specs=out_specs, out_shape=out_shape,
               scratch_shapes=[pltpu.VMEM((HEADS, DV, DK), f32)], compiler_params=_params("parallel", "arbitrary"))(*args)


def _conv_fwd(p_all, dw_w, dw_b, *, B, L, D, name):
    ct = _pick(D, 256)
    nj = D // ct
    st = _pick(L, 128, 8)
    off = CONV_PAD - CONV_W // 2

    def body(a_ref, b_ref, w_ref, bias_ref, o_ref, zs_ref):
        _fill_shifted(zs_ref, L, lambda t0, n: a_ref[pl.ds(t0, n), :].astype(f32) * jax.nn.sigmoid(b_ref[pl.ds(t0, n), :].astype(f32)))
        for t0 in range(0, L, st):
            acc = jnp.zeros((st, ct), f32) + bias_ref[...]
            for k in range(CONV_W):
                acc = acc + w_ref[pl.ds(k, 1), :] * _window(zs_ref, t0 + k + off, st)
            o_ref[pl.ds(t0, st), :] = acc

    return _pc(
        body, name=name, grid=(B, nj),
        in_specs=[pl.BlockSpec((L, ct), lambda b, j: (b, j)), pl.BlockSpec((L, ct), lambda b, j: (b, nj + j)),
                  pl.BlockSpec((CONV_W, ct), lambda b, j: (0, j)), pl.BlockSpec((1, ct), lambda b, j: (0, j))],
        out_specs=pl.BlockSpec((L, ct), lambda b, j: (b, j)), out_shape=jax.ShapeDtypeStruct((B * L, D), f32),
        scratch_shapes=[pltpu.VMEM((SUBLANES, L + 2 * CONV_PAD, ct), f32)], compiler_params=_params("parallel", "parallel"),
    )(p_all, p_all, dw_w, dw_b)


def _fill_shifted(zs_ref, L, rows):
    lp = L + 2 * CONV_PAD
    ct = zs_ref.shape[2]
    step = 256
    zs_ref[0, pl.ds(0, CONV_PAD), :] = jnp.zeros((CONV_PAD, ct), f32)
    zs_ref[0, pl.ds(CONV_PAD + L, CONV_PAD), :] = jnp.zeros((CONV_PAD, ct), f32)
    for t0 in range(0, L, step):
        n = min(step, L - t0)
        zs_ref[0, pl.ds(CONV_PAD + t0, n), :] = rows(t0, n)
    for r in range(1, SUBLANES):
        for i0 in range(0, lp - SUBLANES, step):
            n = min(step, lp - SUBLANES - i0)
            zs_ref[r, pl.ds(i0, n), :] = zs_ref[0, pl.ds(i0 + r, n), :]


def _window(zs_ref, start, n):
    r = start % SUBLANES
    return zs_ref[r, pl.ds(start - r, n), :]


def _conv_bwd(p_all, dcz, dw_w, *, B, L, D, name, carry=None):
    ct = _pick(D, 128)
    nj = D // ct
    st = _pick(L, 256, 8)
    half = CONV_W // 2

    def body(a_ref, b_ref, dcz_ref, w_ref, da_ref, db_ref, ddw_ref, zs_ref, ds_ref):
        bi = pl.program_id(1)
        _fill_shifted(zs_ref, L, lambda t0, n: a_ref[pl.ds(t0, n), :].astype(f32) * jax.nn.sigmoid(b_ref[pl.ds(t0, n), :].astype(f32)))
        _fill_shifted(ds_ref, L, lambda t0, n: dcz_ref[pl.ds(t0, n), :])

        @pl.when(bi == 0)
        def _():
            ddw_ref[...] = jnp.zeros_like(ddw_ref)

        for t0 in range(0, L, st):
            acc = jnp.zeros((st, ct), f32)
            for k in range(CONV_W):
                acc = acc + w_ref[pl.ds(k, 1), :] * _window(ds_ref, t0 + CONV_PAD + half - k, st)
            a_t = a_ref[pl.ds(t0, st), :].astype(f32)
            sg_t = jax.nn.sigmoid(b_ref[pl.ds(t0, st), :].astype(f32))
            da_ref[pl.ds(t0, st), :] = (acc * sg_t).astype(bf16)
            db_ref[pl.ds(t0, st), :] = (acc * a_t * sg_t * (1.0 - sg_t)).astype(bf16)

        parts = [jnp.zeros((SUBLANES, ct), f32) for _ in range(CONV_W)]
        sw = _pick(L, 64, SUBLANES)
        for t0 in range(0, L, sw):
            dout = dcz_ref[pl.ds(t0, sw), :]
            for k in range(CONV_W):
                prod = dout * _window(zs_ref, t0 + k + CONV_PAD - half, sw)
                for i in range(0, sw, SUBLANES):
                    parts[k] = parts[k] + prod[i:i + SUBLANES]
        for k in range(CONV_W):
            ddw_ref[pl.ds(k, 1), :] += jnp.sum(parts[k], axis=0, keepdims=True)

    res, carried = _call(
        body, name=name, grid=(nj, B),
        in_specs=[pl.BlockSpec((L, ct), lambda j, b: (b, j)), pl.BlockSpec((L, ct), lambda j, b: (b, nj + j)),
                  pl.BlockSpec((L, ct), lambda j, b: (b, j)), pl.BlockSpec((CONV_W, ct), lambda j, b: (0, j))],
        out_specs=[pl.BlockSpec((L, ct), lambda j, b: (b, j)), pl.BlockSpec((L, ct), lambda j, b: (b, j)),
                   pl.BlockSpec((2 * CONV_PAD, ct), lambda j, b: (0, j))],
        out_shape=[jax.ShapeDtypeStruct((B * L, D), bf16), jax.ShapeDtypeStruct((B * L, D), bf16),
                   jax.ShapeDtypeStruct((2 * CONV_PAD, D), f32)],
        scratch_shapes=[pltpu.VMEM((SUBLANES, L + 2 * CONV_PAD, ct), f32), pltpu.VMEM((SUBLANES, L + 2 * CONV_PAD, ct), f32)],
        sem=("parallel", "arbitrary"), args=(p_all, p_all, dcz, dw_w), carry=carry)
    return res if carry is None else (res, carried)


def _exchange(arrs, scatter, name):
    ex = _Exchange(arrs, scatter)
    n = ex.n

    def body(*refs):
        ex.start(refs[:n], refs[n:2 * n], refs[2 * n:])
        ex.finish(refs[:n], refs[n:2 * n], refs[2 * n:])

    res = _pc(body, name=name, in_specs=ex.specs, out_specs=ex.specs, out_shape=ex.out_shape, scratch_shapes=ex.scratch)(*arrs)
    return list(res)


class _Exchange:
    def __init__(self, arrs, scatter):
        self.arrs, self.scatter, self.n = list(arrs), scatter, len(arrs)
        self.out_shape = [jax.ShapeDtypeStruct(((N_DEV,) + a.shape[1:]) if scatter else ((N_DEV,) + a.shape), a.dtype) for a in arrs]
        self.specs = [pl.BlockSpec(memory_space=pl.ANY)] * self.n
        self.scratch = [pltpu.SemaphoreType.DMA((self.n, N_DEV - 1)), pltpu.SemaphoreType.DMA((self.n, N_DEV - 1)),
                        pltpu.SemaphoreType.DMA((self.n,))]

    def _copies(self, ins, outs, sems, landing):
        send_sems, recv_sems, local_sems = sems
        me = 4 * lax.axis_index("x") + 2 * lax.axis_index("y") + lax.axis_index("c")
        if landing:
            local = []
        else:
            local = [pltpu.make_async_copy(ins[a].at[me] if self.scatter else ins[a], outs[a].at[me], local_sems.at[a]) for a in range(self.n)]
        remote = []
        for k in range(1, N_DEV):
            p = (me + (N_DEV - k if landing else k)) % N_DEV
            for a in range(self.n):
                remote.append(pltpu.make_async_remote_copy(
                    src_ref=ins[a].at[p] if self.scatter else ins[a], dst_ref=outs[a].at[p if landing else me],
                    send_sem=send_sems.at[a, k - 1], recv_sem=recv_sems.at[a, k - 1],
                    device_id=(p // 4, (p // 2) % 2, p % 2), device_id_type=MESH))
        return local, remote

    def _gather_plan(self, ins, outs, sems):
        send_sems, recv_sems, local_sems = sems
        x, y, c = lax.axis_index("x"), lax.axis_index("y"), lax.axis_index("c")
        chips = [(1 - x, y), (x, 1 - y), (1 - x, 1 - y)]

        def blk(px, py, pc):
            return 4 * px + 2 * py + pc

        def copy(a, k, block, to, own):
            return pltpu.make_async_remote_copy(
                src_ref=ins[a] if own else outs[a].at[block], dst_ref=outs[a].at[block],
                send_sem=send_sems.at[a, k], recv_sem=recv_sems.at[a, k], device_id=to, device_id_type=MESH)

        me = blk(x, y, c)
        local = [pltpu.make_async_copy(ins[a], outs[a].at[me], local_sems.at[a]) for a in range(self.n)]
        return local, copy, me, (x, y, 1 - c), chips, blk, c

    def start(self, ins, outs, sems):
        if self.scatter:
            local, sends = self._copies(ins, outs, sems, False)
            for cp in local + sends:
                cp.start()
            return
        local, copy, me, sibling, chips, _, c = self._gather_plan(ins, outs, sems)
        for cp in local:
            cp.start()
        for a in range(self.n):
            copy(a, 0, me, sibling, True).start()
            for j, chip in enumerate(chips):
                copy(a, 1 + j, me, (*chip, c), True).start()

    def finish(self, ins, outs, sems):
        if self.scatter:
            for cp in self._copies(ins, outs, sems, True)[1]:
                cp.wait_recv()
            local, sends = self._copies(ins, outs, sems, False)
            for cp in sends:
                cp.wait_send()
            for cp in local:
                cp.wait()
            return
        local, copy, me, sibling, chips, blk, c = self._gather_plan(ins, outs, sems)
        for j, chip in enumerate(chips):
            for a in range(self.n):
                copy(a, 1 + j, blk(*chip, c), sibling, True).wait_recv()
                copy(a, 4 + j, blk(*chip, c), sibling, False).start()
        for a in range(self.n):
            copy(a, 0, blk(*sibling), sibling, True).wait_recv()
            for j, chip in enumerate(chips):
                copy(a, 4 + j, blk(*chip, 1 - c), sibling, False).wait_recv()
        for a in range(self.n):
            copy(a, 0, me, sibling, True).wait_send()
            for j, chip in enumerate(chips):
                copy(a, 1 + j, me, (*chip, c), True).wait_send()
                copy(a, 4 + j, blk(*chip, c), sibling, False).wait_send()
        for cp in local:
            cp.wait()


def _carried(inner, n_in, n_out, grid, ex):
    n = ex.n

    def body(*refs):
        own_in, c_in = refs[:n_in], refs[n_in:n_in + n]
        own_out, c_out = refs[n_in + n:n_in + n + n_out], refs[n_in + n + n_out:n_in + 2 * n + n_out]
        rest = refs[n_in + 2 * n + n_out:]
        own_scr, sems = rest[:len(rest) - 3], rest[len(rest) - 3:]
        pids = [pl.program_id(d) for d in range(len(grid))]
        first = functools.reduce(jnp.logical_and, [p == 0 for p in pids])
        last = functools.reduce(jnp.logical_and, [p == g - 1 for p, g in zip(pids, grid)])

        @pl.when(first)
        def _():
            ex.start(c_in, c_out, sems)

        inner(*own_in, *own_out, *own_scr)

        @pl.when(last)
        def _():
            ex.finish(c_in, c_out, sems)

    return body


def _call(inner, *, name, grid, in_specs, out_specs, out_shape, scratch_shapes, sem, args, carry=None):
    if carry is None:
        res = _pc(inner, name=name, grid=grid, in_specs=in_specs, out_specs=out_specs, out_shape=out_shape,
                  scratch_shapes=scratch_shapes, compiler_params=_params(*sem))(*args)
        return list(res), None
    ex = _Exchange(*carry)
    res = _pc(_carried(inner, len(in_specs), len(out_specs), grid, ex), name=name, grid=grid,
              in_specs=list(in_specs) + ex.specs, out_specs=list(out_specs) + ex.specs, out_shape=list(out_shape) + ex.out_shape,
              scratch_shapes=list(scratch_shapes) + ex.scratch, compiler_params=_params(*(["arbitrary"] * len(grid))))(*args, *ex.arrs)
    res = list(res)
    return res[:len(out_specs)], res[len(out_specs):]


def _mod_fwd(c_all, c_ctx, w_loc, b_loc, name):
    nr, D = c_all.shape
    nc = w_loc.shape[1]

    def body(c_ref, cc_ref, w_ref, b_ref, o_ref):
        a = jnp.concatenate([c_ref[...], jnp.broadcast_to(cc_ref[...], (8, D))], axis=0)
        s = jax.nn.silu(a).astype(bf16)
        o_ref[...] = jnp.dot(s, w_ref[...].astype(bf16), preferred_element_type=f32) + b_ref[...]

    return _pc(body, name=name, out_shape=jax.ShapeDtypeStruct((nr + 8, nc), f32), compiler_params=_params())(c_all, c_ctx, w_loc, b_loc)


def _mod_bwd(c_all, c_ctx, w_loc, dmx_loc, dmc_loc, name):
    nr, D = c_all.shape
    nc = w_loc.shape[1]

    def body(c_ref, cc_ref, w_ref, dmx_ref, dmc_ref, gw_ref, gc_ref):
        cc = cc_ref[...]
        a = jnp.concatenate([c_ref[...], jnp.broadcast_to(cc, (N_DEV, D))], axis=0)
        s = jax.nn.silu(a).astype(bf16)
        g = jnp.concatenate([dmx_ref[...], dmc_ref[...]], axis=0).astype(bf16)
        gw_ref[...] = lax.dot_general(s, g, (((0,), (0,)), ((), ())), preferred_element_type=f32)
        dmc = jnp.sum(dmc_ref[...], axis=0, keepdims=True)
        ds = lax.dot_general(jnp.broadcast_to(dmc, (8, nc)).astype(bf16), w_ref[...].astype(bf16), (((1,), (1,)), ((), ())),
                             preferred_element_type=f32)[0:1]
        sg = jax.nn.sigmoid(cc)
        gc_ref[...] = ds * (sg * (1.0 + cc * (1.0 - sg)))

    return _pc(body, name=name, out_shape=[jax.ShapeDtypeStruct((D, nc), f32), jax.ShapeDtypeStruct((1, D), f32)],
               compiler_params=_params())(c_all, c_ctx, w_loc, dmx_loc, dmc_loc)


def _adamw_math(w, g, m, v):
    m2 = ADAM_B1 * m + (1.0 - ADAM_B1) * g
    v2 = ADAM_B2 * v + (1.0 - ADAM_B2) * jnp.square(g)
    m_hat = m2 / (1.0 - ADAM_B1 ** ADAM_STEP)
    v_hat = v2 / (1.0 - ADAM_B2 ** ADAM_STEP)
    delta = -ADAM_LR * (m_hat / (jnp.sqrt(v_hat) + ADAM_EPS) + ADAM_WD * w)
    return delta, m2, v2


def _adamw(w, m, v, g, name, partials):
    r, cdim = w.shape
    tr = _pick(r, 256, 8)

    def body(w_ref, m_ref, v_ref, g_ref, og_ref, od_ref, om_ref, ov_ref):
        if partials:
            g = g_ref[0].astype(f32)
            for s in range(1, N_DEV):
                g = g + g_ref[s].astype(f32)
        else:
            g = g_ref[...]
        d, m2, v2 = _adamw_math(w_ref[...], g, m_ref[...], v_ref[...])
        og_ref[...] = g
        od_ref[...] = d
        om_ref[...] = m2
        ov_ref[...] = v2

    blk = pl.BlockSpec((tr, cdim), lambda i: (i, 0))
    g_spec = pl.BlockSpec((N_DEV, tr, cdim), lambda i: (0, i, 0)) if partials else blk
    return _pc(body, name=name, grid=(r // tr,), in_specs=[blk, blk, blk, g_spec], out_specs=[blk] * 4,
               out_shape=[jax.ShapeDtypeStruct((r, cdim), f32)] * 4, compiler_params=_params("parallel"))(w, m, v, g)


def _sum_sources(parts, name):
    def body(*refs):
        for i_ref, o_ref in zip(refs[:len(parts)], refs[len(parts):]):
            acc = i_ref[0]
            for s in range(1, i_ref.shape[0]):
                acc = acc + i_ref[s]
            o_ref[...] = acc

    return list(_pc(body, name=name, out_shape=[jax.ShapeDtypeStruct(p.shape[1:], f32) for p in parts],
                    compiler_params=_params())(*parts))


def kernel(x, c, ctx, c_ctx, w_mod, b_mod, g_ffn1, w1_gu, w1_down, g_mix, w_in, dw_weight, dw_bias, conv_ln_g, conv_ln_b, w_conv_out, w_alpha_f, b_alpha_f, w_alpha_b, b_alpha_b, gla_norm_g, w_gla_out, w_out, g_ffn2, w2_gu, w2_down, g_final, loss_target, m_c_ctx, m_w_mod, m_b_mod, m_g_ffn1, m_w1_gu, m_w1_down, m_g_mix, m_w_in, m_dw_weight, m_dw_bias, m_conv_ln_g, m_conv_ln_b, m_w_conv_out, m_w_alpha_f, m_b_alpha_f, m_w_alpha_b, m_b_alpha_b, m_gla_norm_g, m_w_gla_out, m_w_out, m_g_ffn2, m_w2_gu, m_w2_down, m_g_final, v_c_ctx, v_w_mod, v_b_mod, v_g_ffn1, v_w1_gu, v_w1_down, v_g_mix, v_w_in, v_dw_weight, v_dw_bias, v_conv_ln_g, v_conv_ln_b, v_w_conv_out, v_w_alpha_f, v_b_alpha_f, v_w_alpha_b, v_b_alpha_b, v_gla_norm_g, v_w_gla_out, v_w_out, v_g_ffn2, v_w2_gu, v_w2_down, v_g_final):
    B, L, D = x.shape
    Lc = ctx.shape[1]
    T, Tc = B * L, B * Lc
    Tall = T + Tc
    F = w1_down.shape[1] * N_DEV
    DK, DV = D // (2 * HEADS), D // HEADS
    QK = HEADS * DK
    PW = 7 * D + LR_PAD
    tm = ROW_TILE
    tpe = L // tm
    nx, nall = T // tm, Tall // tm
    me = 4 * lax.axis_index("x") + 2 * lax.axis_index("y") + lax.axis_index("c")

    rw_all = dict(tm=tm, n_tiles=nall, tpe=tpe, nx_tiles=nx, n_ex=B + 1)
    rw_x = dict(tm=tm, n_tiles=nx, tpe=tpe, nx_tiles=nx, n_ex=B)

    w1gu_g, dww_g, waf_g, wab_g, c_g = _exchange([w1_gu[0].astype(bf16), dw_weight[0], w_alpha_f[0], w_alpha_b[0], c], False, "gather_first")

    def cols(gat):
        return jnp.transpose(gat, (1, 0, 2)).reshape(gat.shape[1], N_DEV * gat.shape[2])

    def rows_(gat):
        return gat.reshape(N_DEV * gat.shape[1], gat.shape[2])

    W1gu = cols(w1gu_g)
    dww = cols(dww_g)
    WA = jnp.zeros((LR_PAD, 2 * QK), f32).at[:LOWRANK, :QK].set(cols(waf_g)).at[LOWRANK:2 * LOWRANK, QK:].set(cols(wab_g)).astype(bf16)
    BA = jnp.concatenate([b_alpha_f, b_alpha_b], axis=1)
    c_all = c_g.reshape(N_DEV * B, D)
    c_ctx2 = c_ctx.reshape(1, D)

    ncm = w_mod.shape[2]
    b_mod_loc = lax.dynamic_slice(b_mod, (0, me * ncm), (1, ncm))
    mod_loc = _mod_fwd(c_all, c_ctx2, w_mod[0], b_mod_loc, "mod_fwd")
    (mod_g,) = _exchange([mod_loc], False, "gather_mod")
    mod_full = cols(mod_g)
    mod_tab = jnp.concatenate([lax.dynamic_slice(mod_full, (me * B, 0), (B, N_MOD * D)), mod_full[N_DEV * B:N_DEV * B + 1]], axis=0)
    mods = [mod_tab[:, i * D:(i + 1) * D].reshape(B + 1, 1, D) for i in range(N_MOD)]
    mods_x = [mm[:B] for mm in mods]

    x_all = jnp.concatenate([x.reshape(T, D), ctx.reshape(Tc, D)], axis=0)

    def f_ffn_in(tok, ex, sh):
        return [_rms_mod(tok[0], sh[0], ex[0], ex[1])], [], []

    (u1,) = _rowwise(f_ffn_in, name="ffn1_in", tok_in=[(x_all, D, 0, False)], ex_in=[mods[0], mods[1]], sh_in=[g_ffn1],
                     tok_out=[(D, bf16)], **rw_all)
    (gu1, h1), (w1d_g, win_g) = _ffn_up(u1, W1gu, "ffn1_up", carry=([w1_down[0].astype(bf16), w_in[0].astype(bf16)], False))
    W1d = rows_(w1d_g)
    lr2 = 2 * LOWRANK
    segs = [(0, 2 * D, 0), (2 * D, 2 * D + QK, 6 * D), (2 * D + QK, 3 * D, 6 * D + QK), (3 * D, 4 * D, 2 * D), (4 * D, 5 * D, 3 * D),
            (5 * D, 5 * D + lr2, 7 * D), (5 * D + lr2, 6 * D + lr2, 4 * D), (6 * D + lr2, 7 * D + lr2, 5 * D)]
    wc = w_in.shape[2]
    win_parts = []
    for lo, hi, _ in sorted(segs, key=lambda t: t[2]):
        for d in range(N_DEV):
            a0, a1 = max(lo, d * wc), min(hi, (d + 1) * wc)
            if a0 < a1:
                win_parts.append(win_g[d][:, a0 - d * wc:a1 - d * wc])
    Win = jnp.concatenate(win_parts + [jnp.zeros((D, LR_PAD - lr2), bf16)], axis=1)
    f1 = _matmul(h1, W1d, "nn", f32, "ffn1_down")

    def mix_in(xv, fv, gate, sh, sc, g):
        x1 = xv + 0.5 * gate * fv
        return x1, _rms_mod(x1, g, sh, sc)

    def f_mix_in(tok, ex, sh):
        return list(mix_in(tok[0], tok[1], ex[0], ex[1], ex[2], sh[0])), [], []

    x1, um = _rowwise(f_mix_in, name="mix_in", tok_in=[(x_all, D, 0, False), (f1, D, 0, False)], ex_in=[mods[2], mods[3], mods[4]],
                      sh_in=[g_mix], tok_out=[(D, f32), (D, bf16)], **rw_all)
    p_all, (wco_g, wgo_g, wo_g, w2gu_g) = _matmul(
        um, Win, "nn", bf16, "in_proj", tm_cap=512, tn_cap=2432,
        carry=([w_conv_out[0].astype(bf16), w_gla_out[0].astype(bf16), w_out[0].astype(bf16), w2_gu[0].astype(bf16)], False))
    Wco, Wgo, Wo, W2gu = rows_(wco_g), rows_(wgo_g), rows_(wo_g), cols(w2gu_g)

    def log_decay(lr, wa, ba):
        z = _bdot(lr, wa, (1, 0)) + ba
        return _log_sigmoid(z) / TAU

    def f_decay(tok, ex, sh):
        return [log_decay(tok[0], sh[0], sh[1])], [], []

    lr_blk = (p_all, LR_PAD, 7 * D // LR_PAD, False)
    (la_all,) = _rowwise(f_decay, name="log_decay", tok_in=[lr_blk], sh_in=[WA, BA], tok_out=[(2 * QK, f32)], **rw_all)

    zeros_s = jnp.zeros((B, HEADS, DV, DK), f32)
    gla_c = dict(row0=T, nb=B, seq=Lc, D=D)
    gla_x = dict(row0=0, nb=B, seq=L, D=D)
    _, hist_cf, s_f = _gla_fwd(p_all, la_all, zeros_s, rev=False, name="gla_ctx_f", **gla_c)
    _, hist_cb, s_b = _gla_fwd(p_all, la_all, zeros_s, rev=True, name="gla_ctx_b", **gla_c)
    (o_f, hist_f, _), (w2d_g,) = _gla_fwd(p_all, la_all, s_f, rev=False, name="gla_x_f", carry=([w2_down[0].astype(bf16)], False), **gla_x)
    W2d = rows_(w2d_g)
    o_b, hist_b, _ = _gla_fwd(p_all, la_all, s_b, rev=True, name="gla_x_b", **gla_x)

    cz = _conv_fwd(p_all, dww, dw_bias, B=B, L=L, D=D, name="conv_fwd")

    def ln_silu(z, g, b):
        mu = jnp.mean(z, axis=-1, keepdims=True)
        var = jnp.mean(jnp.square(z - mu), axis=-1, keepdims=True)
        return jax.nn.silu((z - mu) * lax.rsqrt(var + EPS) * g + b)

    def nn(a, w):
        return jnp.dot(a.astype(bf16), w, preferred_element_type=f32)

    def nt(a, w):
        return lax.dot_general(a.astype(bf16), w, (((1,), (1,)), ((), ())), preferred_element_type=f32)

    def f_ln(tok, ex, sh):
        zc = ln_silu(tok[0], sh[0], sh[1])
        return [zc, nn(zc, sh[2])], [], []

    zc, yc = _rowwise(f_ln, name="conv_ln_out", tok_in=[(cz, D, 0, False)], sh_in=[conv_ln_g, conv_ln_b, Wco],
                      tok_out=[(D, bf16), (D, bf16)], **rw_x)

    def gla_out(of, ob, og, gn):
        return _head_rms(of + ob, DV) * gn * jax.nn.silu(og.astype(f32))

    def f_gla_out(tok, ex, sh):
        og2 = gla_out(tok[0], tok[1], tok[2], sh[0])
        return [og2, nn(og2, sh[1])], [], []

    og_blk = (p_all, D, 3, False)
    og2, yg = _rowwise(f_gla_out, name="gla_norm_out", tok_in=[(o_f, D, 0, False), (o_b, D, 0, False), og_blk], sh_in=[gla_norm_g, Wgo],
                       tok_out=[(D, bf16), (D, bf16)], **rw_x)

    def merge(ga, gb, ycv, ygv):
        return jax.nn.sigmoid(ga.astype(f32)) * ycv.astype(f32) + jax.nn.sigmoid(gb.astype(f32)) * ygv.astype(f32)

    def f_merge(tok, ex, sh):
        mg = merge(*tok)
        return [mg, nn(mg, sh[0])], [], []

    ga_blk, gb_blk = (p_all, D, 4, False), (p_all, D, 5, False)
    mg, mix = _rowwise(f_merge, name="merge_mix_out", tok_in=[ga_blk, gb_blk, (yc, D, 0, False), (yg, D, 0, False)], sh_in=[Wo],
                       tok_out=[(D, bf16), (D, f32)], **rw_x)

    def ffn2_in(x1v, mixv, g5, sh, sc, g):
        x2 = x1v + g5 * mixv
        return x2, _rms_mod(x2, g, sh, sc)

    def f_ffn2_in(tok, ex, sh):
        return list(ffn2_in(tok[0], tok[1], ex[0], ex[1], ex[2], sh[0])), [], []

    x2, u2 = _rowwise(f_ffn2_in, name="ffn2_in", tok_in=[(x1, D, 0, False), (mix, D, 0, False)], ex_in=[mods_x[5], mods_x[6], mods_x[7]],
                      sh_in=[g_ffn2], tok_out=[(D, f32), (D, bf16)], **rw_x)
    gu2, h2 = _ffn_up(u2, W2gu, "ffn2_up")
    f2 = _matmul(h2, W2d, "nn", f32, "ffn2_down")

    gf2 = g_final.reshape(1, D)

    def head_loss(x2v, f2v, g8, gf, tgt):
        x3 = x2v + 0.5 * g8 * f2v
        y = x3 * lax.rsqrt(jnp.mean(x3 * x3, axis=-1, keepdims=True) + EPS) * gf
        return 0.5 * jnp.sum(jnp.mean(jnp.square(y - tgt), axis=-1))

    def f_head(tok, ex, sh):
        loss, vjp = jax.vjp(lambda a, b_, c_, d_: head_loss(a, b_, c_, d_, tok[2]), tok[0], tok[1], ex[0], sh[0])
        dx3, df2, dg8, dgf = vjp(jnp.ones((), f32))
        return [dx3, df2], [dg8], [dgf, jnp.broadcast_to(loss.reshape(1, 1), (1, 128))]

    dx3, df2, dg8, dgf, loss_p = _rowwise(
        f_head, name="head", tok_in=[(x2, D, 0, False), (f2, D, 0, False), (loss_target.reshape(T, D), D, 0, False)], ex_in=[mods_x[8]],
        sh_in=[gf2], tok_out=[(D, f32), (D, bf16)], ex_out=[D], gl_out=[(1, D), (1, 128)], **rw_x)

    dgu2 = _ffn_down_dx(df2, W2d, gu2, "ffn2_down_dx")
    gW2d = _matmul(h2, df2, "tn", f32, "ffn2_down_dw", tm_cap=1408)
    du2 = _matmul(dgu2, W2gu, "nt", f32, "ffn2_up_dx", halves="a")
    gW2gu = _matmul(u2, dgu2, "tn", f32, "ffn2_up_dw", halves="b")

    def f_ffn2_in_bwd(tok, ex, sh):
        _, vjp = jax.vjp(ffn2_in, tok[0], tok[1], ex[0], ex[1], ex[2], sh[0])
        dx2, dmix, dg5, dsh, dsc, dg = vjp((tok[3], tok[2]))
        return [dx2, dmix], [dg5, dsh, dsc], [dg]

    dx2, dmix, dg5, dsh6, dsc7, dg_ffn2 = _rowwise(
        f_ffn2_in_bwd, name="ffn2_in_bwd", tok_in=[(x1, D, 0, False), (mix, D, 0, False), (du2, D, 0, False), (dx3, D, 0, False)],
        ex_in=[mods_x[5], mods_x[6], mods_x[7]], sh_in=[g_ffn2], tok_out=[(D, f32), (D, bf16)], ex_out=[D, D, D], gl_out=[(1, D)], **rw_x)

    gWo = _matmul(mg, dmix, "tn", f32, "mix_out_dw")

    def f_merge_bwd(tok, ex, sh):
        _, vjp = jax.vjp(merge, *[t.astype(f32) for t in tok[:4]])
        dga, dgb, dyc, dyg = vjp(nt(tok[4], sh[0]))
        return [dga, dgb, dyc, dyg], [], []

    dga, dgb, dyc, dyg = _rowwise(f_merge_bwd, name="mix_out_merge_bwd",
                                  tok_in=[ga_blk, gb_blk, (yc, D, 0, False), (yg, D, 0, False), (dmix, D, 0, False)], sh_in=[Wo],
                                  tok_out=[(D, bf16)] * 4, **rw_x)
    gWco = _matmul(zc, dyc, "tn", f32, "conv_out_dw")
    gWgo = _matmul(og2, dyg, "tn", f32, "gla_out_dw")

    def f_ln_bwd(tok, ex, sh):
        _, vjp = jax.vjp(ln_silu, tok[0], sh[0], sh[1])
        dcz, dg, db = vjp(nt(tok[1], sh[2]))
        return [dcz], [], [dg, db, jnp.sum(dcz, axis=0, keepdims=True)]

    dcz, g_ln_g, g_ln_b, g_dwb = _rowwise(f_ln_bwd, name="conv_out_ln_bwd", tok_in=[(cz, D, 0, False), (dyc, D, 0, False)],
                                          sh_in=[conv_ln_g, conv_ln_b, Wco], tok_out=[(D, f32)], gl_out=[(1, D)] * 3, **rw_x)
    def col_shards(g):
        return jnp.transpose(g.reshape(g.shape[0], N_DEV, g.shape[1] // N_DEV), (1, 0, 2)).astype(bf16)

    def row_shards(g):
        return g.reshape(N_DEV, g.shape[0] // N_DEV, g.shape[1]).astype(bf16)

    (dca, dcb, g_dww), (r_w2d, r_w2gu, r_wo, r_wco, r_wgo) = _conv_bwd(
        p_all, dcz, dww, B=B, L=L, D=D, name="conv_bwd",
        carry=([row_shards(gW2d), col_shards(gW2gu), row_shards(gWo), row_shards(gWco), row_shards(gWgo)], True))

    def f_gla_out_bwd(tok, ex, sh):
        _, vjp = jax.vjp(gla_out, tok[0], tok[1], tok[2].astype(f32), sh[0])
        dof, _, dog, dgn = vjp(nt(tok[3], sh[1]))
        return [dof, dog], [], [dgn]

    d_o, dog, g_gn = _rowwise(f_gla_out_bwd, name="gla_out_norm_bwd",
                              tok_in=[(o_f, D, 0, False), (o_b, D, 0, False), og_blk, (dyg, D, 0, False)], sh_in=[gla_norm_g, Wgo],
                              tok_out=[(D, f32), (D, bf16)], gl_out=[(1, D)], **rw_x)

    dq_f, dk_f, dv_f, dla_f, ds_f = _gla_bwd(p_all, la_all, hist_f, d_o, zeros_s, rev=False, name="gla_x_f_bwd", **gla_x)
    dq, dk, dv, dla_b, ds_b = _gla_bwd(p_all, la_all, hist_b, d_o, zeros_s, rev=True, name="gla_x_b_bwd", add=(dq_f, dk_f, dv_f), **gla_x)
    dq_cf, dk_cf, dv_cf, dla_cf, _ = _gla_bwd(p_all, la_all, hist_cf, None, ds_f, rev=False, name="gla_ctx_f_bwd", **gla_c)
    _, dk_c, dv_c, dla_cb, _ = _gla_bwd(p_all, la_all, hist_cb, None, ds_b, rev=True, name="gla_ctx_b_bwd", add=(dq_cf, dk_cf, dv_cf), **gla_c)

    dla_all = jnp.concatenate([jnp.concatenate([dla_f, dla_b], axis=1), jnp.concatenate([dla_cf, dla_cb], axis=1)], axis=0)

    def f_decay_bwd(tok, ex, sh):
        _, vjp = jax.vjp(log_decay, tok[0].astype(f32), sh[0].astype(f32), sh[1])
        dlr, dwa, dba = vjp(tok[1])
        return [dlr], [], [dwa, dba]

    dlr, g_WA, g_BA = _rowwise(f_decay_bwd, name="log_decay_bwd", tok_in=[lr_blk, (dla_all, 2 * QK, 0, False)], sh_in=[WA, BA],
                               tok_out=[(LR_PAD, bf16)], gl_out=[(LR_PAD, 2 * QK), (1, 2 * QK)], **rw_all)

    zc_ = functools.partial(jnp.zeros, dtype=bf16)
    dp_x = jnp.concatenate([dca, dcb, dv, dog, dga, dgb, dq, dk, dlr[:T]], axis=1)
    dp_c = jnp.concatenate([zc_((Tc, 2 * D)), dv_c, zc_((Tc, 3 * D)), zc_((Tc, QK)), dk_c, dlr[T:]], axis=1)
    dp_all = jnp.concatenate([dp_x, dp_c], axis=0)
    gWin_p = _matmul(um, dp_all, "tn", f32, "in_proj_dw", tm_cap=512, tn_cap=2432)
    gwin_shards = []
    for d in range(N_DEV):
        parts = []
        for lo, hi, po in segs:
            a0, a1 = max(lo, d * wc), min(hi, (d + 1) * wc)
            if a0 < a1:
                parts.append(gWin_p[:, po + a0 - lo:po + a1 - lo])
        gwin_shards.append(jnp.concatenate(parts, axis=1))
    dum, (r_win,) = _matmul(dp_all, Win, "nt", f32, "in_proj_dx", tk_cap=2432, carry=([jnp.stack(gwin_shards).astype(bf16)], True))

    def f_mix_in_bwd(tok, ex, sh):
        _, vjp = jax.vjp(mix_in, tok[0], tok[1], ex[0], ex[1], ex[2], sh[0])
        dx1, df1, dgate, dsh, dsc, dg = vjp((tok[3], tok[2]))
        return [dx1, df1], [dgate, dsh, dsc], [dg]

    dx1, df1, dg2, dsh3, dsc4, dg_mix = _rowwise(
        f_mix_in_bwd, name="mix_in_bwd", tok_in=[(x_all, D, 0, False), (f1, D, 0, False), (dum, D, 0, False), (dx2, D, 0, True)],
        ex_in=[mods[2], mods[3], mods[4]], sh_in=[g_mix], tok_out=[(D, f32), (D, bf16)], ex_out=[D, D, D], gl_out=[(1, D)], **rw_all)

    dgu1 = _ffn_down_dx(df1, W1d, gu1, "ffn1_down_dx")
    gW1d = _matmul(h1, df1, "tn", f32, "ffn1_down_dw", tm_cap=1408)
    gW1gu, (r_w1d,) = _matmul(u1, dgu1, "tn", f32, "ffn1_up_dw", carry=([row_shards(gW1d)], True), halves="b")
    du1, (r_w1gu,) = _matmul(dgu1, W1gu, "nt", f32, "ffn1_up_dx", carry=([col_shards(gW1gu)], True), halves="a")

    def f_ffn_in_bwd(tok, ex, sh):
        _, vjp = jax.vjp(_rms_mod, tok[0], sh[0], ex[0], ex[1])
        dx, dg, dsh, dsc = vjp(tok[1])
        return [dx + tok[2]], [dsh, dsc], [dg]

    dx_all, dsh0, dsc1, dg_ffn1 = _rowwise(
        f_ffn_in_bwd, name="ffn1_in_bwd", tok_in=[(x_all, D, 0, False), (du1, D, 0, False), (dx1, D, 0, False)],
        ex_in=[mods[0], mods[1]], sh_in=[g_ffn1], tok_out=[(D, f32)], ex_out=[D, D], gl_out=[(1, D)], **rw_all)
    grad_x = dx_all[:T].reshape(B, L, D)

    zrow = jnp.zeros((1, 1, D), f32)
    dmod_loc = jnp.concatenate([dsh0, dsc1, dg2, dsh3, dsc4] + [jnp.concatenate([t, zrow], axis=0) for t in (dg5, dsh6, dsc7, dg8)],
                               axis=2).reshape(B + 1, N_MOD * D)
    small = [loss_p, dg_ffn1, dg_mix, g_dww[:CONV_W].reshape(1, CONV_W * D), g_dwb, g_ln_g, g_ln_b,
             g_WA[:LOWRANK, :QK].reshape(1, LOWRANK * QK), g_BA[:, :QK], g_WA[LOWRANK:2 * LOWRANK, QK:].reshape(1, LOWRANK * QK), g_BA[:, QK:],
             g_gn, dg_ffn2, dgf]
    small_w = [s.shape[1] for s in small]
    def to8(v):
        n_pad = -(-v.shape[1] // 1024) * 1024
        return jnp.pad(v, ((0, 0), (0, n_pad - v.shape[1]))).reshape(8, n_pad // 8)

    def from8(a, n):
        return a.reshape(1, a.size)[:, :n]

    dmod_g, small_g = _exchange([dmod_loc, to8(jnp.concatenate(small, axis=1))], False, "gather_small")
    dmx = dmod_g[:, :B].reshape(N_DEV * B, N_MOD * D)
    dmc = dmod_g[:, B]
    gWmod, gcc_p = _mod_bwd(c_all, c_ctx2, w_mod[0], lax.dynamic_slice(dmx, (0, me * ncm), (N_DEV * B, ncm)),
                            lax.dynamic_slice(dmc, (0, me * ncm), (N_DEV, ncm)), "mod_bwd")

    rs_out = [r_w1gu, r_w1d, r_win, r_wco, r_wgo, r_wo, r_w2gu, r_w2d]
    (gcc_g,) = _exchange([to8(gcc_p)], False, "gather_cctx")

    sums, g_cc, g_bmod = _sum_sources([small_g, gcc_g, jnp.concatenate([dmx, dmc], axis=0).reshape(N_DEV * (B + 1), 8, N_MOD * D // 8)], "sum_small")
    sums, g_cc, g_bmod = from8(sums, sum(small_w)), from8(g_cc, D), from8(g_bmod, N_MOD * D)
    offs = [0]
    for wd in small_w:
        offs.append(offs[-1] + wd)
    sm = [sums[:, offs[i]:offs[i + 1]] for i in range(len(small))]
    loss = sm[0][0, 0]
    ncd, nca = dw_weight.shape[2], w_alpha_f.shape[2]
    g_dww_loc = lax.dynamic_slice(sm[3].reshape(CONV_W, D), (0, me * ncd), (CONV_W, ncd)).reshape(1, CONV_W * ncd)
    g_waf_loc = lax.dynamic_slice(sm[7].reshape(LOWRANK, QK), (0, me * nca), (LOWRANK, nca)).reshape(1, LOWRANK * nca)
    g_wab_loc = lax.dynamic_slice(sm[9].reshape(LOWRANK, QK), (0, me * nca), (LOWRANK, nca)).reshape(1, LOWRANK * nca)

    big = {}
    for nm, wv, mv, vv, part in (("w1_gu", w1_gu, m_w1_gu, v_w1_gu, rs_out[0]), ("w1_down", w1_down, m_w1_down, v_w1_down, rs_out[1]),
                                 ("w_in", w_in, m_w_in, v_w_in, rs_out[2]), ("w_conv_out", w_conv_out, m_w_conv_out, v_w_conv_out, rs_out[3]),
                                 ("w_gla_out", w_gla_out, m_w_gla_out, v_w_gla_out, rs_out[4]), ("w_out", w_out, m_w_out, v_w_out, rs_out[5]),
                                 ("w2_gu", w2_gu, m_w2_gu, v_w2_gu, rs_out[6]), ("w2_down", w2_down, m_w2_down, v_w2_down, rs_out[7])):
        big[nm] = [t[None] for t in _adamw(wv[0], mv[0], vv[0], part, "adamw_" + nm, True)]
    big["w_mod"] = [t[None] for t in _adamw(w_mod[0], m_w_mod[0], v_w_mod[0], gWmod, "adamw_w_mod", False)]

    small_params = [("c_ctx", c_ctx, m_c_ctx, v_c_ctx, g_cc), ("b_mod", b_mod, m_b_mod, v_b_mod, g_bmod), ("g_ffn1", g_ffn1, m_g_ffn1, v_g_ffn1, sm[1]),
                    ("g_mix", g_mix, m_g_mix, v_g_mix, sm[2]), ("dw_weight", dw_weight, m_dw_weight, v_dw_weight, g_dww_loc),
                    ("dw_bias", dw_bias, m_dw_bias, v_dw_bias, sm[4]), ("conv_ln_g", conv_ln_g, m_conv_ln_g, v_conv_ln_g, sm[5]),
                    ("conv_ln_b", conv_ln_b, m_conv_ln_b, v_conv_ln_b, sm[6]), ("w_alpha_f", w_alpha_f, m_w_alpha_f, v_w_alpha_f, g_waf_loc),
                    ("b_alpha_f", b_alpha_f, m_b_alpha_f, v_b_alpha_f, sm[8]), ("w_alpha_b", w_alpha_b, m_w_alpha_b, v_w_alpha_b, g_wab_loc),
                    ("b_alpha_b", b_alpha_b, m_b_alpha_b, v_b_alpha_b, sm[10]), ("gla_norm_g", gla_norm_g, m_gla_norm_g, v_gla_norm_g, sm[11]),
                    ("g_ffn2", g_ffn2, m_g_ffn2, v_g_ffn2, sm[12]), ("g_final", g_final, m_g_final, v_g_final, sm[13])]
    flat = lambda t: t.reshape(1, t.size)
    pw, pm, pv, pg = (jnp.concatenate([flat(sp[i]) for sp in small_params], axis=1) for i in (1, 2, 3, 4))
    n_small = pw.shape[1]
    s_g, s_d, s_m, s_v = (from8(t, n_small) for t in _adamw(to8(pw), to8(pm), to8(pv), to8(pg), "adamw_small", False))
    small_out, o0 = {}, 0
    for nm, wv, _, _, _ in small_params:
        small_out[nm] = [t[:, o0:o0 + wv.size].reshape(wv.shape) for t in (s_g, s_d, s_m, s_v)]
        o0 += wv.size

    order = ["c_ctx", "w_mod", "b_mod", "g_ffn1", "w1_gu", "w1_down", "g_mix", "w_in", "dw_weight", "dw_bias", "conv_ln_g", "conv_ln_b",
             "w_conv_out", "w_alpha_f", "b_alpha_f", "w_alpha_b", "b_alpha_b", "gla_norm_g", "w_gla_out", "w_out", "g_ffn2", "w2_gu",
             "w2_down", "g_final"]
    res = {**big, **small_out}
    return (loss, grad_x, *[res[n][0] for n in order], *[res[n][1] for n in order], *[res[n][2] for n in order], *[res[n][3] for n in order])
```

```python
import functools

import jax
import jax.numpy as jnp
from jax import lax
from jax.experimental import pallas as pl
from jax.experimental.pallas import tpu as pltpu

f32, bf16 = jnp.float32, jnp.bfloat16

N_DEV = 8
HEADS = 4
LOWRANK = 16
CONV_W = 31
CONV_PAD = 16
SUBLANES = 8
CHUNK = 64
SUB = 16
GLA_ROWS = 256
GLA_SAFE_DECAY = 60.0
TAU = 16.0
EPS = 1e-6
N_MOD = 9
LR_PAD = 128
ROW_TILE = 256
V7X_VMEM_BYTES = 64 << 20
VMEM_LIMIT = (V7X_VMEM_BYTES * 3) // 4

ADAM_LR, ADAM_B1, ADAM_B2, ADAM_EPS, ADAM_WD, ADAM_STEP = 0.001, 0.9, 0.999, 1e-08, 0.01, 10

MESH = pl.DeviceIdType.MESH


def _pc(body, **kw):
    return pl.pallas_call(body, **kw)


def _params(*sem):
    return pltpu.CompilerParams(dimension_semantics=sem, vmem_limit_bytes=VMEM_LIMIT)


def _pick(n, cap, unit=128):
    best = None
    for t in range(unit, min(n, cap) + 1, unit):
        if n % t == 0:
            best = t
    return best or n


def _matmul(a, b, mode, out_dtype, name, tm_cap=1024, tn_cap=1536, tk_cap=None, carry=None, halves=None):
    tk_cap = tk_cap or (2048 if mode == "tn" else 2816)
    if halves == "a":
        (_, M, Kh), N = a.shape, b.shape[0]
        K, tk = 2 * Kh, _pick(Kh, tk_cap)
        tm, tn = _pick(M, tm_cap), _pick(N, tn_cap)
        a_spec = pl.BlockSpec((None, tm, tk), lambda i, j, k: (k // (Kh // tk), i, k % (Kh // tk)))
    elif halves == "b":
        (K, M), (_, _, Nh) = a.shape, b.shape
        N, tn = 2 * Nh, _pick(Nh, tn_cap)
        tm, tk = _pick(M, tm_cap), _pick(K, tk_cap)
    else:
        if mode == "tn":
            (K, M), N = a.shape, b.shape[1]
        elif mode == "nt":
            (M, K), N = a.shape, b.shape[0]
        else:
            (M, K), N = a.shape, b.shape[1]
        tm, tn, tk = _pick(M, tm_cap), _pick(N, tn_cap), _pick(K, tk_cap)
    nk = K // tk
    if halves != "a":
        a_spec = pl.BlockSpec((tk, tm), lambda i, j, k: (k, i)) if mode == "tn" else pl.BlockSpec((tm, tk), lambda i, j, k: (i, k))
    if halves == "b":
        b_spec = pl.BlockSpec((None, tk, tn), lambda i, j, k: (j // (Nh // tn), k, j % (Nh // tn)))
    else:
        b_spec = pl.BlockSpec((tn, tk), lambda i, j, k: (j, k)) if mode == "nt" else pl.BlockSpec((tk, tn), lambda i, j, k: (k, j))
    dims = {"nn": ((1,), (0,)), "nt": ((1,), (1,)), "tn": ((0,), (0,))}[mode]

    def body_single(a_ref, b_ref, o_ref):
        o_ref[...] = lax.dot_general(a_ref[...].astype(bf16), b_ref[...].astype(bf16), (dims, ((), ())),
                                     preferred_element_type=f32).astype(out_dtype)

    def body(a_ref, b_ref, o_ref, acc_ref):
        k = pl.program_id(2)
        part = lax.dot_general(a_ref[...].astype(bf16), b_ref[...].astype(bf16), (dims, ((), ())), preferred_element_type=f32)

        @pl.when(k == 0)
        def _():
            acc_ref[...] = part

        @pl.when(k > 0)
        def _():
            acc_ref[...] += part

        @pl.when(k == nk - 1)
        def _():
            o_ref[...] = acc_ref[...].astype(out_dtype)

    (out,), carried = _call(
        body_single if nk == 1 else body, name=name, grid=(M // tm, N // tn, nk), in_specs=[a_spec, b_spec],
        out_specs=[pl.BlockSpec((tm, tn), lambda i, j, k: (i, j))], out_shape=[jax.ShapeDtypeStruct((M, N), out_dtype)],
        scratch_shapes=[] if nk == 1 else [pltpu.VMEM((tm, tn), f32)], sem=("parallel", "parallel", "arbitrary"),
        args=(a, b), carry=carry)
    return out if carry is None else (out, carried)


def _ffn_up(u, Wgu, name, carry=None):
    M, K = u.shape
    F = Wgu.shape[1] // 2
    tm, tn = _pick(M, 512), _pick(F, 1408)
    nj = F // tn

    def body(u_ref, wa_ref, wb_ref, gu_ref, h_ref):
        uv = u_ref[...]
        a = jnp.dot(uv, wa_ref[...], preferred_element_type=f32)
        b = jnp.dot(uv, wb_ref[...], preferred_element_type=f32)
        gu_ref[0] = a.astype(bf16)
        gu_ref[1] = b.astype(bf16)
        h_ref[...] = (jax.nn.silu(a) * b).astype(bf16)

    res, carried = _call(
        body, name=name, grid=(nj, M // tm),
        in_specs=[pl.BlockSpec((tm, K), lambda j, i: (i, 0)), pl.BlockSpec((K, tn), lambda j, i: (0, j)),
                  pl.BlockSpec((K, tn), lambda j, i: (0, nj + j))],
        out_specs=[pl.BlockSpec((2, tm, tn), lambda j, i: (0, i, j)), pl.BlockSpec((tm, tn), lambda j, i: (i, j))],
        out_shape=[jax.ShapeDtypeStruct((2, M, F), bf16), jax.ShapeDtypeStruct((M, F), bf16)],
        scratch_shapes=[], sem=("parallel", "parallel"), args=(u, Wgu, Wgu), carry=carry)
    return res if carry is None else (res, carried)


def _ffn_down_dx(df, Wd, gu, name):
    M, D = df.shape
    F = Wd.shape[0]
    tm, tn = _pick(M, 512), _pick(F, 1408)

    def body(df_ref, w_ref, gu_ref, o_ref):
        dh = lax.dot_general(df_ref[...], w_ref[...], (((1,), (1,)), ((), ())), preferred_element_type=f32)
        a, b = gu_ref[0].astype(f32), gu_ref[1].astype(f32)
        sg = jax.nn.sigmoid(a)
        o_ref[0] = (dh * b * sg * (1.0 + a * (1.0 - sg))).astype(bf16)
        o_ref[1] = (dh * a * sg).astype(bf16)

    return _pc(
        body, name=name, grid=(F // tn, M // tm),
        in_specs=[pl.BlockSpec((tm, D), lambda j, i: (i, 0)), pl.BlockSpec((tn, D), lambda j, i: (j, 0)),
                  pl.BlockSpec((2, tm, tn), lambda j, i: (0, i, j))],
        out_specs=pl.BlockSpec((2, tm, tn), lambda j, i: (0, i, j)), out_shape=jax.ShapeDtypeStruct((2, M, F), bf16),
        compiler_params=_params("parallel", "parallel"))(df, Wd, gu)


def _rowwise(fn, *, name, tm, n_tiles, tpe, nx_tiles, n_ex, tok_in=(), ex_in=(), sh_in=(), tok_out=(), ex_out=(), gl_out=(), carry=None):
    def seg(i):
        return jnp.minimum(i // tpe, n_ex - 1)

    in_specs, args = [], []
    for arr, w, cb, x_only in tok_in:
        if x_only:
            in_specs.append(pl.BlockSpec((tm, w), functools.partial(lambda i, cb: (jnp.minimum(i, nx_tiles - 1), cb), cb=cb)))
        else:
            in_specs.append(pl.BlockSpec((tm, w), functools.partial(lambda i, cb: (i, cb), cb=cb)))
        args.append(arr)
    for arr in ex_in:
        in_specs.append(pl.BlockSpec((1, 1, arr.shape[-1]), lambda i: (seg(i), 0, 0)))
        args.append(arr)
    for arr in sh_in:
        in_specs.append(pl.BlockSpec(arr.shape, functools.partial(lambda i, nd: (0,) * nd, nd=arr.ndim)))
        args.append(arr)
    out_specs, out_shape = [], []
    for w, dt in tok_out:
        out_specs.append(pl.BlockSpec((tm, w), lambda i: (i, 0)))
        out_shape.append(jax.ShapeDtypeStruct((n_tiles * tm, w), dt))
    for w in ex_out:
        out_specs.append(pl.BlockSpec((1, 1, w), lambda i: (seg(i), 0, 0)))
        out_shape.append(jax.ShapeDtypeStruct((n_ex, 1, w), f32))
    for r, w in gl_out:
        out_specs.append(pl.BlockSpec((r, w), lambda i: (0, 0)))
        out_shape.append(jax.ShapeDtypeStruct((r, w), f32))
    n_tok, n_exi, n_sh = len(tok_in), len(ex_in), len(sh_in)
    n_to, n_eo = len(tok_out), len(ex_out)
    x_only_flags = [t[3] for t in tok_in]

    def body(*refs):
        i = pl.program_id(0)
        ins, outs = refs[: n_tok + n_exi + n_sh], refs[n_tok + n_exi + n_sh:]
        is_x = i < nx_tiles
        tok_vals = []
        for r, xo in zip(ins[:n_tok], x_only_flags):
            v = r[...]
            tok_vals.append(jnp.where(is_x, v, jnp.zeros_like(v)) if xo else v)
        ex_vals = [r[0] for r in ins[n_tok:n_tok + n_exi]]
        sh_vals = [r[...] for r in ins[n_tok + n_exi:]]
        t_o, e_o, g_o = fn(tok_vals, ex_vals, sh_vals)
        for r, v in zip(outs[:n_to], t_o):
            r[...] = v.astype(r.dtype)
        first = jnp.logical_and(i % tpe == 0, i <= nx_tiles)
        for r, v in zip(outs[n_to:n_to + n_eo], e_o):
            @pl.when(first)
            def _(r=r, v=v):
                r[0] = v

            @pl.when(jnp.logical_not(first))
            def _(r=r, v=v):
                r[0] += v
        for r, v in zip(outs[n_to + n_eo:], g_o):
            @pl.when(i == 0)
            def _(r=r, v=v):
                r[...] = v

            @pl.when(i > 0)
            def _(r=r, v=v):
                r[...] += v

    res, carried = _call(body, name=name, grid=(n_tiles,), in_specs=in_specs, out_specs=out_specs, out_shape=out_shape,
                         scratch_shapes=[], sem=("arbitrary",), args=args, carry=carry)
    return res if carry is None else (res, carried)


def _rms_mod(x, g, sh, sc):
    y = x * lax.rsqrt(jnp.mean(x * x, axis=-1, keepdims=True) + EPS) * g
    return y * (1.0 + sc) + sh


def _log_sigmoid(z):
    return jnp.minimum(z, 0.0) - jnp.log(1.0 + jnp.exp(-jnp.abs(z)))


def _head_rms(o, DV):
    parts = []
    for h in range(HEADS):
        oh = o[:, h * DV:(h + 1) * DV]
        parts.append(oh * lax.rsqrt(jnp.mean(oh * oh, axis=-1, keepdims=True) + EPS))
    return jnp.concatenate(parts, axis=1)


@functools.partial(jax.custom_vjp, nondiff_argnums=(2,))
def _bdot(a, b, dims):
    return lax.dot_general(a.astype(bf16), b.astype(bf16), (((dims[0],), (dims[1],)), ((), ())), preferred_element_type=f32)


def _bdot_fwd(a, b, dims):
    return _bdot(a, b, dims), (a, b)


def _bdot_bwd(dims, res, g):
    a, b = res
    ca, cb = dims
    da = _bdot(g, b, (1, 1 - cb)) if ca == 1 else _bdot(b, g, (1 - cb, 1))
    db = _bdot(a, g, (1 - ca, 0)) if cb == 0 else _bdot(g, a, (0, 1 - ca))
    return da, db


_bdot.defvjp(_bdot_fwd, _bdot_bwd)


def _split_dot(m, x, dims):
    mb, rem, acc = m.astype(bf16), x, None
    for _ in range(3):
        piece = rem.astype(bf16)
        rem = rem - piece.astype(f32)
        part = lax.dot_general(mb, piece, (((dims[0],), (dims[1],)), ((), ())), preferred_element_type=f32)
        acc = part if acc is None else acc + part
    return acc


@jax.custom_vjp
def _tri_cumsum(tri, g):
    return _split_dot(tri, g, (1, 0))


def _tri_cumsum_fwd(tri, g):
    return _tri_cumsum(tri, g), tri


def _tri_cumsum_bwd(tri, db):
    return jnp.zeros_like(tri), _split_dot(tri, db, (0, 0))


_tri_cumsum.defvjp(_tri_cumsum_fwd, _tri_cumsum_bwd)


def _gla_chunk(St, q, k, v, g, *, rev, scale, exact):
    C, DK = q.shape
    r = lax.broadcasted_iota(jnp.int32, (C, C), 0)
    c = lax.broadcasted_iota(jnp.int32, (C, C), 1)
    causal = (r <= c) if rev else (r >= c)
    b = _tri_cumsum(causal.astype(f32), g)
    qs = q * scale
    qe = qs * jnp.exp(b)
    inter = _bdot(qe, St, (1, 1))
    b_last = b[0:1] if rev else b[C - 1:C]
    kd = k * jnp.exp(b_last - b)
    St_new = St * jnp.exp(b_last) + _bdot(v, kd, (0, 0))
    if not exact:
        att = jnp.where(causal, _bdot(qe, k * jnp.exp(-b), (1, 1)), 0.0)
        return St_new, inter + _bdot(att, v, (1, 0))
    rr = lax.broadcasted_iota(jnp.int32, (SUB, SUB, DK), 0)
    cc = lax.broadcasted_iota(jnp.int32, (SUB, SUB, DK), 1)
    m3 = (rr <= cc) if rev else (rr >= cc)
    outs = []
    for i in range(C // SUB):
        lo, hi = i * SUB, (i + 1) * SUB
        bi, qi, ki, vi = b[lo:hi], qs[lo:hi], k[lo:hi], v[lo:hi]
        rel = bi[:, None, :] - bi[None, :, :]
        e = jnp.where(m3, jnp.exp(jnp.where(m3, rel, 0.0)), 0.0)
        att = jnp.sum(qi[:, None, :] * e * ki[None, :, :], axis=-1)
        acc = _bdot(att, vi, (1, 0))
        ref_row = b[hi - 1:hi] if rev else b[lo:lo + 1]
        prev = slice(hi, C) if rev else slice(0, lo)
        if (hi < C) if rev else (lo > 0):
            qn = qi * jnp.exp(bi - ref_row)
            ks = k[prev] * jnp.exp(ref_row - b[prev])
            acc = acc + _bdot(_bdot(qn, ks, (1, 1)), v[prev], (1, 0))
        outs.append(acc)
    return St_new, inter + jnp.concatenate(outs, axis=0)


def _mild_decay(la_ref):
    return jnp.min(la_ref[...]) >= -GLA_SAFE_DECAY / CHUNK


def _gla_specs(D, rev_blocks, row0, seq):
    DK, DV = D // (2 * HEADS), D // HEADS
    nblk = seq // GLA_ROWS
    rb0 = row0 // GLA_ROWS

    def blk(j):
        return (nblk - 1 - j) if rev_blocks else j

    return DK, DV, nblk, rb0, blk


def _gla_in_specs(D, rev, rows):
    QK = D // 2
    return [
        pl.BlockSpec((GLA_ROWS, QK), lambda b, j: (rows(b, j), 6 * D // QK)),
        pl.BlockSpec((GLA_ROWS, QK), lambda b, j: (rows(b, j), 6 * D // QK + 1)),
        pl.BlockSpec((GLA_ROWS, D), lambda b, j: (rows(b, j), 2)),
        pl.BlockSpec((GLA_ROWS, QK), lambda b, j: (rows(b, j), 1 if rev else 0)),
    ]


def _gla_fwd(p_all, la_all, s0, *, rev, row0, nb, seq, D, name, carry=None):
    DK, DV, nblk, rb0, blk = _gla_specs(D, rev, row0, seq)
    cpb = GLA_ROWS // CHUNK

    def rows(b, j):
        return rb0 + b * nblk + blk(j)

    in_specs = _gla_in_specs(D, rev, rows) + [pl.BlockSpec((1, HEADS, DV, DK), lambda b, j: (b, 0, 0, 0))]
    out_specs = [
        pl.BlockSpec((GLA_ROWS, D), lambda b, j: (b * nblk + blk(j), 0)),
        pl.BlockSpec((1, HEADS, cpb, DV, DK), lambda b, j: (b, 0, blk(j), 0, 0)),
        pl.BlockSpec((1, HEADS, DV, DK), lambda b, j: (b, 0, 0, 0)),
    ]
    out_shape = [
        jax.ShapeDtypeStruct((nb * seq, D), f32),
        jax.ShapeDtypeStruct((nb, HEADS, seq // CHUNK, DV, DK), bf16),
        jax.ShapeDtypeStruct((nb, HEADS, DV, DK), f32),
    ]
    chunk = functools.partial(_gla_chunk, rev=rev, scale=DK ** -0.5)

    def body(q_ref, k_ref, v_ref, la_ref, s0_ref, o_ref, hist_ref, sfin_ref, st_ref):
        j = pl.program_id(1)

        @pl.when(j == 0)
        def _():
            st_ref[...] = s0_ref[0]

        def step(ci, exact):
            cc = (cpb - 1 - ci) if rev else ci
            sl = pl.ds(cc * CHUNK, CHUNK)
            for h in range(HEADS):
                kq, kv = pl.ds(h * DK, DK), pl.ds(h * DV, DV)
                St = st_ref[h]
                hist_ref[0, h, cc] = St.astype(bf16)
                St2, o = chunk(St, q_ref[sl, kq].astype(f32), k_ref[sl, kq].astype(f32), v_ref[sl, kv].astype(f32), la_ref[sl, kq],
                               exact=exact)
                o_ref[sl, kv] = o
                st_ref[h] = St2

        mild = _mild_decay(la_ref)
        for exact in (False, True):
            @pl.when(jnp.logical_not(mild) if exact else mild)
            def _(exact=exact):
                for ci in range(cpb):
                    step(ci, exact)

        @pl.when(j == nblk - 1)
        def _():
            sfin_ref[0] = st_ref[...]

    res, carried = _call(body, name=name, grid=(nb, nblk), in_specs=in_specs, out_specs=out_specs, out_shape=out_shape,
                         scratch_shapes=[pltpu.VMEM((HEADS, DV, DK), f32)], sem=("parallel", "arbitrary"),
                         args=(p_all, p_all, p_all, la_all, s0), carry=carry)
    return res if carry is None else (res, carried)


def _gla_bwd(p_all, la_all, hist, do, dsfin, *, rev, row0, nb, seq, D, name, add=None):
    DK, DV, nblk, rb0, blk = _gla_specs(D, not rev, row0, seq)
    cpb = GLA_ROWS // CHUNK
    QK = HEADS * DK
    has_do = do is not None

    def rows(b, j):
        return rb0 + b * nblk + blk(j)

    in_specs = _gla_in_specs(D, rev, rows) + [
        pl.BlockSpec((1, HEADS, cpb, DV, DK), lambda b, j: (b, 0, blk(j), 0, 0)),
        pl.BlockSpec((1, HEADS, DV, DK), lambda b, j: (b, 0, 0, 0)),
    ]
    args = [p_all, p_all, p_all, la_all, hist, dsfin]
    if has_do:
        in_specs.append(pl.BlockSpec((GLA_ROWS, D), lambda b, j: (b * nblk + blk(j), 0)))
        args.append(do)
    if add is not None:
        in_specs += [pl.BlockSpec((GLA_ROWS, t.shape[1]), lambda b, j: (b * nblk + blk(j), 0)) for t in add]
        args += list(add)
    gdt = f32 if add is None else bf16
    out_specs = [
        pl.BlockSpec((GLA_ROWS, QK), lambda b, j: (b * nblk + blk(j), 0)),
        pl.BlockSpec((GLA_ROWS, QK), lambda b, j: (b * nblk + blk(j), 0)),
        pl.BlockSpec((GLA_ROWS, D), lambda b, j: (b * nblk + blk(j), 0)),
        pl.BlockSpec((GLA_ROWS, QK), lambda b, j: (b * nblk + blk(j), 0)),
        pl.BlockSpec((1, HEADS, DV, DK), lambda b, j: (b, 0, 0, 0)),
    ]
    out_shape = [
        jax.ShapeDtypeStruct((nb * seq, QK), gdt), jax.ShapeDtypeStruct((nb * seq, QK), gdt),
        jax.ShapeDtypeStruct((nb * seq, D), gdt), jax.ShapeDtypeStruct((nb * seq, QK), f32),
        jax.ShapeDtypeStruct((nb, HEADS, DV, DK), f32),
    ]
    chunk = functools.partial(_gla_chunk, rev=rev, scale=DK ** -0.5)

    def body(*refs):
        refs = list(refs)
        q_ref, k_ref, v_ref, la_ref, hist_ref, dsfin_ref = refs[:6]
        do_ref = refs[6] if has_do else None
        add_refs = refs[6 + has_do:len(refs) - 6]
        dq_ref, dk_ref, dv_ref, dla_ref, ds0_ref, ds_ref = refs[len(refs) - 6:]
        j = pl.program_id(1)

        @pl.when(j == 0)
        def _():
            ds_ref[...] = dsfin_ref[0]

        def step(ci, exact):
            cc = ci if rev else (cpb - 1 - ci)
            sl = pl.ds(cc * CHUNK, CHUNK)
            for h in range(HEADS):
                kq, kv = pl.ds(h * DK, DK), pl.ds(h * DV, DV)
                prim = (hist_ref[0, h, cc].astype(f32), q_ref[sl, kq].astype(f32), k_ref[sl, kq].astype(f32), v_ref[sl, kv].astype(f32), la_ref[sl, kq])
                _, vjp = jax.vjp(functools.partial(chunk, exact=exact), *prim)
                d_o = do_ref[sl, kv] if has_do else jnp.zeros((CHUNK, DV), f32)
                dSt, dq, dk, dv, dg = vjp((ds_ref[h], d_o))
                if add is not None:
                    dq, dk, dv = dq + add_refs[0][sl, kq], dk + add_refs[1][sl, kq], dv + add_refs[2][sl, kv]
                dq_ref[sl, kq] = dq.astype(gdt)
                dk_ref[sl, kq] = dk.astype(gdt)
                dv_ref[sl, kv] = dv.astype(gdt)
                dla_ref[sl, kq] = dg
                ds_ref[h] = dSt

        mild = _mild_decay(la_ref)
        for exact in (False, True):
            @pl.when(jnp.logical_not(mild) if exact else mild)
            def _(exact=exact):
                for ci in range(cpb):
                    step(ci, exact)

        @pl.when(j == nblk - 1)
        def _():
            ds0_ref[0] = ds_ref[...]

    return _pc(body, name=name, grid=(nb, nblk), in_specs=in_specs, out_specs=out_specs, out_shape=out_shape,
               scratch_shapes=[pltpu.VMEM((HEADS, DV, DK), f32)], compiler_params=_params("parallel", "arbitrary"))(*args)


def _conv_fwd(p_all, dw_w, dw_b, *, B, L, D, name):
    ct = _pick(D, 256)
    nj = D // ct
    st = _pick(L, 128, 8)
    off = CONV_PAD - CONV_W // 2

    def body(a_ref, b_ref, w_ref, bias_ref, o_ref, zs_ref):
        _fill_shifted(zs_ref, L, lambda t0, n: a_ref[pl.ds(t0, n), :].astype(f32) * jax.nn.sigmoid(b_ref[pl.ds(t0, n), :].astype(f32)))
        for t0 in range(0, L, st):
            acc = jnp.zeros((st, ct), f32) + bias_ref[...]
            for k in range(CONV_W):
                acc = acc + w_ref[pl.ds(k, 1), :] * _window(zs_ref, t0 + k + off, st)
            o_ref[pl.ds(t0, st), :] = acc

    return _pc(
        body, name=name, grid=(B, nj),
        in_specs=[pl.BlockSpec((L, ct), lambda b, j: (b, j)), pl.BlockSpec((L, ct), lambda b, j: (b, nj + j)),
                  pl.BlockSpec((CONV_W, ct), lambda b, j: (0, j)), pl.BlockSpec((1, ct), lambda b, j: (0, j))],
        out_specs=pl.BlockSpec((L, ct), lambda b, j: (b, j)), out_shape=jax.ShapeDtypeStruct((B * L, D), f32),
        scratch_shapes=[pltpu.VMEM((SUBLANES, L + 2 * CONV_PAD, ct), f32)], compiler_params=_params("parallel", "parallel"),
    )(p_all, p_all, dw_w, dw_b)


def _fill_shifted(zs_ref, L, rows):
    lp = L + 2 * CONV_PAD
    ct = zs_ref.shape[2]
    step = 256
    zs_ref[0, pl.ds(0, CONV_PAD), :] = jnp.zeros((CONV_PAD, ct), f32)
    zs_ref[0, pl.ds(CONV_PAD + L, CONV_PAD), :] = jnp.zeros((CONV_PAD, ct), f32)
    for t0 in range(0, L, step):
        n = min(step, L - t0)
        zs_ref[0, pl.ds(CONV_PAD + t0, n), :] = rows(t0, n)
    for r in range(1, SUBLANES):
        for i0 in range(0, lp - SUBLANES, step):
            n = min(step, lp - SUBLANES - i0)
            zs_ref[r, pl.ds(i0, n), :] = zs_ref[0, pl.ds(i0 + r, n), :]


def _window(zs_ref, start, n):
    r = start % SUBLANES
    return zs_ref[r, pl.ds(start - r, n), :]


def _conv_bwd(p_all, dcz, dw_w, *, B, L, D, name, carry=None):
    ct = _pick(D, 128)
    nj = D // ct
    st = _pick(L, 256, 8)
    half = CONV_W // 2

    def body(a_ref, b_ref, dcz_ref, w_ref, da_ref, db_ref, ddw_ref, zs_ref, ds_ref):
        bi = pl.program_id(1)
        _fill_shifted(zs_ref, L, lambda t0, n: a_ref[pl.ds(t0, n), :].astype(f32) * jax.nn.sigmoid(b_ref[pl.ds(t0, n), :].astype(f32)))
        _fill_shifted(ds_ref, L, lambda t0, n: dcz_ref[pl.ds(t0, n), :])

        @pl.when(bi == 0)
        def _():
            ddw_ref[...] = jnp.zeros_like(ddw_ref)

        for t0 in range(0, L, st):
            acc = jnp.zeros((st, ct), f32)
            for k in range(CONV_W):
                acc = acc + w_ref[pl.ds(k, 1), :] * _window(ds_ref, t0 + CONV_PAD + half - k, st)
            a_t = a_ref[pl.ds(t0, st), :].astype(f32)
            sg_t = jax.nn.sigmoid(b_ref[pl.ds(t0, st), :].astype(f32))
            da_ref[pl.ds(t0, st), :] = (acc * sg_t).astype(bf16)
            db_ref[pl.ds(t0, st), :] = (acc * a_t * sg_t * (1.0 - sg_t)).astype(bf16)

        parts = [jnp.zeros((SUBLANES, ct), f32) for _ in range(CONV_W)]
        sw = _pick(L, 64, SUBLANES)
        for t0 in range(0, L, sw):
            dout = dcz_ref[pl.ds(t0, sw), :]
            for k in range(CONV_W):
                prod = dout * _window(zs_ref, t0 + k + CONV_PAD - half, sw)
                for i in range(0, sw, SUBLANES):
                    parts[k] = parts[k] + prod[i:i + SUBLANES]
        for k in range(CONV_W):
            ddw_ref[pl.ds(k, 1), :] += jnp.sum(parts[k], axis=0, keepdims=True)

    res, carried = _call(
        body, name=name, grid=(nj, B),
        in_specs=[pl.BlockSpec((L, ct), lambda j, b: (b, j)), pl.BlockSpec((L, ct), lambda j, b: (b, nj + j)),
                  pl.BlockSpec((L, ct), lambda j, b: (b, j)), pl.BlockSpec((CONV_W, ct), lambda j, b: (0, j))],
        out_specs=[pl.BlockSpec((L, ct), lambda j, b: (b, j)), pl.BlockSpec((L, ct), lambda j, b: (b, j)),
                   pl.BlockSpec((2 * CONV_PAD, ct), lambda j, b: (0, j))],
        out_shape=[jax.ShapeDtypeStruct((B * L, D), bf16), jax.ShapeDtypeStruct((B * L, D), bf16),
                   jax.ShapeDtypeStruct((2 * CONV_PAD, D), f32)],
        scratch_shapes=[pltpu.VMEM((SUBLANES, L + 2 * CONV_PAD, ct), f32), pltpu.VMEM((SUBLANES, L + 2 * CONV_PAD, ct), f32)],
        sem=("parallel", "arbitrary"), args=(p_all, p_all, dcz, dw_w), carry=carry)
    return res if carry is None else (res, carried)


def _exchange(arrs, scatter, name):
    ex = _Exchange(arrs, scatter)
    n = ex.n

    def body(*refs):
        ex.start(refs[:n], refs[n:2 * n], refs[2 * n:])
        ex.finish(refs[:n], refs[n:2 * n], refs[2 * n:])

    res = _pc(body, name=name, in_specs=ex.specs, out_specs=ex.specs, out_shape=ex.out_shape, scratch_shapes=ex.scratch)(*arrs)
    return list(res)


class _Exchange:
    def __init__(self, arrs, scatter):
        self.arrs, self.scatter, self.n = list(arrs), scatter, len(arrs)
        self.out_shape = [jax.ShapeDtypeStruct(((N_DEV,) + a.shape[1:]) if scatter else ((N_DEV,) + a.shape), a.dtype) for a in arrs]
        self.specs = [pl.BlockSpec(memory_space=pl.ANY)] * self.n
        self.scratch = [pltpu.SemaphoreType.DMA((self.n, N_DEV - 1)), pltpu.SemaphoreType.DMA((self.n, N_DEV - 1)),
                        pltpu.SemaphoreType.DMA((self.n,))]

    def _copies(self, ins, outs, sems, landing):
        send_sems, recv_sems, local_sems = sems
        me = 4 * lax.axis_index("x") + 2 * lax.axis_index("y") + lax.axis_index("c")
        if landing:
            local = []
        else:
            local = [pltpu.make_async_copy(ins[a].at[me] if self.scatter else ins[a], outs[a].at[me], local_sems.at[a]) for a in range(self.n)]
        remote = []
        for k in range(1, N_DEV):
            p = (me + (N_DEV - k if landing else k)) % N_DEV
            for a in range(self.n):
                remote.append(pltpu.make_async_remote_copy(
                    src_ref=ins[a].at[p] if self.scatter else ins[a], dst_ref=outs[a].at[p if landing else me],
                    send_sem=send_sems.at[a, k - 1], recv_sem=recv_sems.at[a, k - 1],
                    device_id=(p // 4, (p // 2) % 2, p % 2), device_id_type=MESH))
        return local, remote

    def _gather_plan(self, ins, outs, sems):
        send_sems, recv_sems, local_sems = sems
        x, y, c = lax.axis_index("x"), lax.axis_index("y"), lax.axis_index("c")
        chips = [(1 - x, y), (x, 1 - y), (1 - x, 1 - y)]

        def blk(px, py, pc):
            return 4 * px + 2 * py + pc

        def copy(a, k, block, to, own):
            return pltpu.make_async_remote_copy(
                src_ref=ins[a] if own else outs[a].at[block], dst_ref=outs[a].at[block],
                send_sem=send_sems.at[a, k], recv_sem=recv_sems.at[a, k], device_id=to, device_id_type=MESH)

        me = blk(x, y, c)
        local = [pltpu.make_async_copy(ins[a], outs[a].at[me], local_sems.at[a]) for a in range(self.n)]
        return local, copy, me, (x, y, 1 - c), chips, blk, c

    def start(self, ins, outs, sems):
        if self.scatter:
            local, sends = self._copies(ins, outs, sems, False)
            for cp in local + sends:
                cp.start()
            return
        local, copy, me, sibling, chips, _, c = self._gather_plan(ins, outs, sems)
        for cp in local:
            cp.start()
        for a in range(self.n):
            copy(a, 0, me, sibling, True).start()
            for j, chip in enumerate(chips):
                copy(a, 1 + j, me, (*chip, c), True).start()

    def finish(self, ins, outs, sems):
        if self.scatter:
            for cp in self._copies(ins, outs, sems, True)[1]:
                cp.wait_recv()
            local, sends = self._copies(ins, outs, sems, False)
            for cp in sends:
                cp.wait_send()
            for cp in local:
                cp.wait()
            return
        local, copy, me, sibling, chips, blk, c = self._gather_plan(ins, outs, sems)
        for j, chip in enumerate(chips):
            for a in range(self.n):
                copy(a, 1 + j, blk(*chip, c), sibling, True).wait_recv()
                copy(a, 4 + j, blk(*chip, c), sibling, False).start()
        for a in range(self.n):
            copy(a, 0, blk(*sibling), sibling, True).wait_recv()
            for j, chip in enumerate(chips):
                copy(a, 4 + j, blk(*chip, 1 - c), sibling, False).wait_recv()
        for a in range(self.n):
            copy(a, 0, me, sibling, True).wait_send()
            for j, chip in enumerate(chips):
                copy(a, 1 + j, me, (*chip, c), True).wait_send()
                copy(a, 4 + j, blk(*chip, c), sibling, False).wait_send()
        for cp in local:
            cp.wait()


def _carried(inner, n_in, n_out, grid, ex):
    n = ex.n

    def body(*refs):
        own_in, c_in = refs[:n_in], refs[n_in:n_in + n]
        own_out, c_out = refs[n_in + n:n_in + n + n_out], refs[n_in + n + n_out:n_in + 2 * n + n_out]
        rest = refs[n_in + 2 * n + n_out:]
        own_scr, sems = rest[:len(rest) - 3], rest[len(rest) - 3:]
        pids = [pl.program_id(d) for d in range(len(grid))]
        first = functools.reduce(jnp.logical_and, [p == 0 for p in pids])
        last = functools.reduce(jnp.logical_and, [p == g - 1 for p, g in zip(pids, grid)])

        @pl.when(first)
        def _():
            ex.start(c_in, c_out, sems)

        inner(*own_in, *own_out, *own_scr)

        @pl.when(last)
        def _():
            ex.finish(c_in, c_out, sems)

    return body


def _call(inner, *, name, grid, in_specs, out_specs, out_shape, scratch_shapes, sem, args, carry=None):
    if carry is None:
        res = _pc(inner, name=name, grid=grid, in_specs=in_specs, out_specs=out_specs, out_shape=out_shape,
                  scratch_shapes=scratch_shapes, compiler_params=_params(*sem))(*args)
        return list(res), None
    ex = _Exchange(*carry)
    res = _pc(_carried(inner, len(in_specs), len(out_specs), grid, ex), name=name, grid=grid,
              in_specs=list(in_specs) + ex.specs, out_specs=list(out_specs) + ex.specs, out_shape=list(out_shape) + ex.out_shape,
              scratch_shapes=list(scratch_shapes) + ex.scratch, compiler_params=_params(*(["arbitrary"] * len(grid))))(*args, *ex.arrs)
    res = list(res)
    return res[:len(out_specs)], res[len(out_specs):]


def _mod_fwd(c_all, c_ctx, w_loc, b_loc, name):
    nr, D = c_all.shape
    nc = w_loc.shape[1]

    def body(c_ref, cc_ref, w_ref, b_ref, o_ref):
        a = jnp.concatenate([c_ref[...], jnp.broadcast_to(cc_ref[...], (8, D))], axis=0)
        s = jax.nn.silu(a).astype(bf16)
        o_ref[...] = jnp.dot(s, w_ref[...].astype(bf16), preferred_element_type=f32) + b_ref[...]

    return _pc(body, name=name, out_shape=jax.ShapeDtypeStruct((nr + 8, nc), f32), compiler_params=_params())(c_all, c_ctx, w_loc, b_loc)


def _mod_bwd(c_all, c_ctx, w_loc, dmx_loc, dmc_loc, name):
    nr, D = c_all.shape
    nc = w_loc.shape[1]

    def body(c_ref, cc_ref, w_ref, dmx_ref, dmc_ref, gw_ref, gc_ref):
        cc = cc_ref[...]
        a = jnp.concatenate([c_ref[...], jnp.broadcast_to(cc, (N_DEV, D))], axis=0)
        s = jax.nn.silu(a).astype(bf16)
        g = jnp.concatenate([dmx_ref[...], dmc_ref[...]], axis=0).astype(bf16)
        gw_ref[...] = lax.dot_general(s, g, (((0,), (0,)), ((), ())), preferred_element_type=f32)
        dmc = jnp.sum(dmc_ref[...], axis=0, keepdims=True)
        ds = lax.dot_general(jnp.broadcast_to(dmc, (8, nc)).astype(bf16), w_ref[...].astype(bf16), (((1,), (1,)), ((), ())),
                             preferred_element_type=f32)[0:1]
        sg = jax.nn.sigmoid(cc)
        gc_ref[...] = ds * (sg * (1.0 + cc * (1.0 - sg)))

    return _pc(body, name=name, out_shape=[jax.ShapeDtypeStruct((D, nc), f32), jax.ShapeDtypeStruct((1, D), f32)],
               compiler_params=_params())(c_all, c_ctx, w_loc, dmx_loc, dmc_loc)


def _adamw_math(w, g, m, v):
    m2 = ADAM_B1 * m + (1.0 - ADAM_B1) * g
    v2 = ADAM_B2 * v + (1.0 - ADAM_B2) * jnp.square(g)
    m_hat = m2 / (1.0 - ADAM_B1 ** ADAM_STEP)
    v_hat = v2 / (1.0 - ADAM_B2 ** ADAM_STEP)
    delta = -ADAM_LR * (m_hat / (jnp.sqrt(v_hat) + ADAM_EPS) + ADAM_WD * w)
    return delta, m2, v2


def _adamw(w, m, v, g, name, partials):
    r, cdim = w.shape
    tr = _pick(r, 256, 8)

    def body(w_ref, m_ref, v_ref, g_ref, og_ref, od_ref, om_ref, ov_ref):
        if partials:
            g = g_ref[0].astype(f32)
            for s in range(1, N_DEV):
                g = g + g_ref[s].astype(f32)
        else:
            g = g_ref[...]
        d, m2, v2 = _adamw_math(w_ref[...], g, m_ref[...], v_ref[...])
        og_ref[...] = g
        od_ref[...] = d
        om_ref[...] = m2
        ov_ref[...] = v2

    blk = pl.BlockSpec((tr, cdim), lambda i: (i, 0))
    g_spec = pl.BlockSpec((N_DEV, tr, cdim), lambda i: (0, i, 0)) if partials else blk
    return _pc(body, name=name, grid=(r // tr,), in_specs=[blk, blk, blk, g_spec], out_specs=[blk] * 4,
               out_shape=[jax.ShapeDtypeStruct((r, cdim), f32)] * 4, compiler_params=_params("parallel"))(w, m, v, g)


def _sum_sources(parts, name):
    def body(*refs):
        for i_ref, o_ref in zip(refs[:len(parts)], refs[len(parts):]):
            acc = i_ref[0]
            for s in range(1, i_ref.shape[0]):
                acc = acc + i_ref[s]
            o_ref[...] = acc

    return list(_pc(body, name=name, out_shape=[jax.ShapeDtypeStruct(p.shape[1:], f32) for p in parts],
                    compiler_params=_params())(*parts))


def kernel(x, c, ctx, c_ctx, w_mod, b_mod, g_ffn1, w1_gu, w1_down, g_mix, w_in, dw_weight, dw_bias, conv_ln_g, conv_ln_b, w_conv_out, w_alpha_f, b_alpha_f, w_alpha_b, b_alpha_b, gla_norm_g, w_gla_out, w_out, g_ffn2, w2_gu, w2_down, g_final, loss_target, m_c_ctx, m_w_mod, m_b_mod, m_g_ffn1, m_w1_gu, m_w1_down, m_g_mix, m_w_in, m_dw_weight, m_dw_bias, m_conv_ln_g, m_conv_ln_b, m_w_conv_out, m_w_alpha_f, m_b_alpha_f, m_w_alpha_b, m_b_alpha_b, m_gla_norm_g, m_w_gla_out, m_w_out, m_g_ffn2, m_w2_gu, m_w2_down, m_g_final, v_c_ctx, v_w_mod, v_b_mod, v_g_ffn1, v_w1_gu, v_w1_down, v_g_mix, v_w_in, v_dw_weight, v_dw_bias, v_conv_ln_g, v_conv_ln_b, v_w_conv_out, v_w_alpha_f, v_b_alpha_f, v_w_alpha_b, v_b_alpha_b, v_gla_norm_g, v_w_gla_out, v_w_out, v_g_ffn2, v_w2_gu, v_w2_down, v_g_final):
    B, L, D = x.shape
    Lc = ctx.shape[1]
    T, Tc = B * L, B * Lc
    Tall = T + Tc
    F = w1_down.shape[1] * N_DEV
    DK, DV = D // (2 * HEADS), D // HEADS
    QK = HEADS * DK
    PW = 7 * D + LR_PAD
    tm = ROW_TILE
    tpe = L // tm
    nx, nall = T // tm, Tall // tm
    me = 4 * lax.axis_index("x") + 2 * lax.axis_index("y") + lax.axis_index("c")

    rw_all = dict(tm=tm, n_tiles=nall, tpe=tpe, nx_tiles=nx, n_ex=B + 1)
    rw_x = dict(tm=tm, n_tiles=nx, tpe=tpe, nx_tiles=nx, n_ex=B)
    m2 = 2 if (L % (2 * tm) == 0 and Tc % (2 * tm) == 0) else 1
    rw_all2 = dict(tm=m2 * tm, n_tiles=nall // m2, tpe=tpe // m2, nx_tiles=nx // m2, n_ex=B + 1)
    rw_x2 = dict(tm=m2 * tm, n_tiles=nx // m2, tpe=tpe // m2, nx_tiles=nx // m2, n_ex=B)

    dww_g, waf_g, wab_g, c_g = _exchange([dw_weight[0], w_alpha_f[0], w_alpha_b[0], c], False, "gather_first")

    def cols(gat):
        return jnp.transpose(gat, (1, 0, 2)).reshape(gat.shape[1], N_DEV * gat.shape[2])

    def rows_(gat):
        return gat.reshape(N_DEV * gat.shape[1], gat.shape[2])

    dww = cols(dww_g)
    WA = jnp.zeros((LR_PAD, 2 * QK), f32).at[:LOWRANK, :QK].set(cols(waf_g)).at[LOWRANK:2 * LOWRANK, QK:].set(cols(wab_g)).astype(bf16)
    BA = jnp.concatenate([b_alpha_f, b_alpha_b], axis=1)
    c_all = c_g.reshape(N_DEV * B, D)
    c_ctx2 = c_ctx.reshape(1, D)

    ncm = w_mod.shape[2]
    b_mod_loc = lax.dynamic_slice(b_mod, (0, me * ncm), (1, ncm))
    mod_loc = _mod_fwd(c_all, c_ctx2, w_mod[0], b_mod_loc, "mod_fwd")
    (mod_g,) = _exchange([mod_loc], False, "gather_mod")
    mod_full = cols(mod_g)
    mod_tab = jnp.concatenate([lax.dynamic_slice(mod_full, (me * B, 0), (B, N_MOD * D)), mod_full[N_DEV * B:N_DEV * B + 1]], axis=0)
    mods = [mod_tab[:, i * D:(i + 1) * D].reshape(B + 1, 1, D) for i in range(N_MOD)]
    mods_x = [mm[:B] for mm in mods]

    x_all = jnp.concatenate([x.reshape(T, D), ctx.reshape(Tc, D)], axis=0)

    def f_ffn_in(tok, ex, sh):
        return [_rms_mod(tok[0], sh[0], ex[0], ex[1])], [], []

    (u1,), (w1gu_g,) = _rowwise(f_ffn_in, name="ffn1_in", tok_in=[(x_all, D, 0, False)], ex_in=[mods[0], mods[1]], sh_in=[g_ffn1],
                                tok_out=[(D, bf16)], carry=([w1_gu[0].astype(bf16)], False), **rw_all)
    W1gu = cols(w1gu_g)
    (gu1, h1), (w1d_g, win_g) = _ffn_up(u1, W1gu, "ffn1_up", carry=([w1_down[0].astype(bf16), w_in[0].astype(bf16)], False))
    W1d = rows_(w1d_g)
    lr2 = 2 * LOWRANK
    segs = [(0, 2 * D, 0), (2 * D, 2 * D + QK, 6 * D), (2 * D + QK, 3 * D, 6 * D + QK), (3 * D, 4 * D, 2 * D), (4 * D, 5 * D, 3 * D),
            (5 * D, 5 * D + lr2, 7 * D), (5 * D + lr2, 6 * D + lr2, 4 * D), (6 * D + lr2, 7 * D + lr2, 5 * D)]
    wc = w_in.shape[2]
    win_parts = []
    for lo, hi, _ in sorted(segs, key=lambda t: t[2]):
        for d in range(N_DEV):
            a0, a1 = max(lo, d * wc), min(hi, (d + 1) * wc)
            if a0 < a1:
                win_parts.append(win_g[d][:, a0 - d * wc:a1 - d * wc])
    Win = jnp.concatenate(win_parts + [jnp.zeros((D, LR_PAD - lr2), bf16)], axis=1)

    def nn(a, w):
        return jnp.dot(a.astype(bf16), w, preferred_element_type=f32)

    def nt(a, w):
        return lax.dot_general(a.astype(bf16), w, (((1,), (1,)), ((), ())), preferred_element_type=f32)

    def mix_in(xv, fv, gate, sh, sc, g):
        x1 = xv + 0.5 * gate * fv
        return x1, _rms_mod(x1, g, sh, sc)

    def f_mix_in(tok, ex, sh):
        f1v = nn(tok[1], sh[1])
        return list(mix_in(tok[0], f1v, ex[0], ex[1], ex[2], sh[0])) + [f1v], [], []

    x1, um, f1 = _rowwise(f_mix_in, name="ffn1_down_mix_in", tok_in=[(x_all, D, 0, False), (h1, F, 0, False)],
                          ex_in=[mods[2], mods[3], mods[4]], sh_in=[g_mix, W1d], tok_out=[(D, f32), (D, bf16), (D, bf16)], **rw_all2)
    p_all, (wco_g, wgo_g, wo_g, w2gu_g) = _matmul(
        um, Win, "nn", bf16, "in_proj", tm_cap=512, tn_cap=2432,
        carry=([w_conv_out[0].astype(bf16), w_gla_out[0].astype(bf16), w_out[0].astype(bf16), w2_gu[0].astype(bf16)], False))
    Wco, Wgo, Wo, W2gu = rows_(wco_g), rows_(wgo_g), rows_(wo_g), cols(w2gu_g)

    def log_decay(lr, wa, ba):
        z = _bdot(lr, wa, (1, 0)) + ba
        return _log_sigmoid(z) / TAU

    def f_decay(tok, ex, sh):
        return [log_decay(tok[0], sh[0], sh[1])], [], []

    lr_blk = (p_all, LR_PAD, 7 * D // LR_PAD, False)
    (la_all,) = _rowwise(f_decay, name="log_decay", tok_in=[lr_blk], sh_in=[WA, BA], tok_out=[(2 * QK, f32)], **rw_all)

    zeros_s = jnp.zeros((B, HEADS, DV, DK), f32)
    gla_c = dict(row0=T, nb=B, seq=Lc, D=D)
    gla_x = dict(row0=0, nb=B, seq=L, D=D)
    _, hist_cf, s_f = _gla_fwd(p_all, la_all, zeros_s, rev=False, name="gla_ctx_f", **gla_c)
    _, hist_cb, s_b = _gla_fwd(p_all, la_all, zeros_s, rev=True, name="gla_ctx_b", **gla_c)
    (o_f, hist_f, _), (w2d_g,) = _gla_fwd(p_all, la_all, s_f, rev=False, name="gla_x_f", carry=([w2_down[0].astype(bf16)], False), **gla_x)
    W2d = rows_(w2d_g)
    o_b, hist_b, _ = _gla_fwd(p_all, la_all, s_b, rev=True, name="gla_x_b", **gla_x)

    cz = _conv_fwd(p_all, dww, dw_bias, B=B, L=L, D=D, name="conv_fwd")

    def ln_silu(z, g, b):
        mu = jnp.mean(z, axis=-1, keepdims=True)
        var = jnp.mean(jnp.square(z - mu), axis=-1, keepdims=True)
        return jax.nn.silu((z - mu) * lax.rsqrt(var + EPS) * g + b)

    def f_ln(tok, ex, sh):
        zc = ln_silu(tok[0], sh[0], sh[1])
        return [zc, nn(zc, sh[2])], [], []

    zc, yc = _rowwise(f_ln, name="conv_ln_out", tok_in=[(cz, D, 0, False)], sh_in=[conv_ln_g, conv_ln_b, Wco],
                      tok_out=[(D, bf16), (D, bf16)], **rw_x)

    def gla_out(of, ob, og, gn):
        return _head_rms(of + ob, DV) * gn * jax.nn.silu(og.astype(f32))

    def f_gla_out(tok, ex, sh):
        og2 = gla_out(tok[0], tok[1], tok[2], sh[0])
        return [og2, nn(og2, sh[1])], [], []

    og_blk = (p_all, D, 3, False)
    og2, yg = _rowwise(f_gla_out, name="gla_norm_out", tok_in=[(o_f, D, 0, False), (o_b, D, 0, False), og_blk], sh_in=[gla_norm_g, Wgo],
                       tok_out=[(D, bf16), (D, bf16)], **rw_x)

    def merge(ga, gb, ycv, ygv):
        return jax.nn.sigmoid(ga.astype(f32)) * ycv.astype(f32) + jax.nn.sigmoid(gb.astype(f32)) * ygv.astype(f32)

    def f_merge(tok, ex, sh):
        mg = merge(*tok)
        return [mg, nn(mg, sh[0])], [], []

    ga_blk, gb_blk = (p_all, D, 4, False), (p_all, D, 5, False)
    mg, mix = _rowwise(f_merge, name="merge_mix_out", tok_in=[ga_blk, gb_blk, (yc, D, 0, False), (yg, D, 0, False)], sh_in=[Wo],
                       tok_out=[(D, bf16), (D, f32)], **rw_x)

    def ffn2_in(x1v, mixv, g5, sh, sc, g):
        x2 = x1v + g5 * mixv
        return x2, _rms_mod(x2, g, sh, sc)

    def f_ffn2_in(tok, ex, sh):
        return list(ffn2_in(tok[0], tok[1], ex[0], ex[1], ex[2], sh[0])), [], []

    x2, u2 = _rowwise(f_ffn2_in, name="ffn2_in", tok_in=[(x1, D, 0, False), (mix, D, 0, False)], ex_in=[mods_x[5], mods_x[6], mods_x[7]],
                      sh_in=[g_ffn2], tok_out=[(D, f32), (D, bf16)], **rw_x)
    gu2, h2 = _ffn_up(u2, W2gu, "ffn2_up")

    gf2 = g_final.reshape(1, D)

    def head_loss(x2v, f2v, g8, gf, tgt):
        x3 = x2v + 0.5 * g8 * f2v
        y = x3 * lax.rsqrt(jnp.mean(x3 * x3, axis=-1, keepdims=True) + EPS) * gf
        return 0.5 * jnp.sum(jnp.mean(jnp.square(y - tgt), axis=-1))

    def f_head(tok, ex, sh):
        loss, vjp = jax.vjp(lambda a, b_, c_, d_: head_loss(a, b_, c_, d_, tok[2]), tok[0], nn(tok[1], sh[1]), ex[0], sh[0])
        dx3, df2, dg8, dgf = vjp(jnp.ones((), f32))
        return [dx3, df2], [dg8], [dgf, jnp.broadcast_to(loss.reshape(1, 1), (1, 128))]

    dx3, df2, dg8, dgf, loss_p = _rowwise(
        f_head, name="ffn2_down_head", tok_in=[(x2, D, 0, False), (h2, F, 0, False), (loss_target.reshape(T, D), D, 0, False)],
        ex_in=[mods_x[8]], sh_in=[gf2, W2d], tok_out=[(D, f32), (D, bf16)], ex_out=[D], gl_out=[(1, D), (1, 128)], **rw_x2)

    dgu2 = _ffn_down_dx(df2, W2d, gu2, "ffn2_down_dx")
    gW2d = _matmul(h2, df2, "tn", f32, "ffn2_down_dw", tm_cap=1408)
    du2 = _matmul(dgu2, W2gu, "nt", f32, "ffn2_up_dx", halves="a")
    gW2gu = _matmul(u2, dgu2, "tn", f32, "ffn2_up_dw", halves="b")

    def f_ffn2_in_bwd(tok, ex, sh):
        _, vjp = jax.vjp(ffn2_in, tok[0], tok[1], ex[0], ex[1], ex[2], sh[0])
        dx2, dmix, dg5, dsh, dsc, dg = vjp((tok[3], tok[2]))
        return [dx2, dmix], [dg5, dsh, dsc], [dg]

    dx2, dmix, dg5, dsh6, dsc7, dg_ffn2 = _rowwise(
        f_ffn2_in_bwd, name="ffn2_in_bwd", tok_in=[(x1, D, 0, False), (mix, D, 0, False), (du2, D, 0, False), (dx3, D, 0, False)],
        ex_in=[mods_x[5], mods_x[6], mods_x[7]], sh_in=[g_ffn2], tok_out=[(D, f32), (D, bf16)], ex_out=[D, D, D], gl_out=[(1, D)], **rw_x)

    gWo = _matmul(mg, dmix, "tn", f32, "mix_out_dw")

    def f_merge_bwd(tok, ex, sh):
        _, vjp = jax.vjp(merge, *[t.astype(f32) for t in tok[:4]])
        dga, dgb, dyc, dyg = vjp(nt(tok[4], sh[0]))
        return [dga, dgb, dyc, dyg], [], []

    dga, dgb, dyc, dyg = _rowwise(f_merge_bwd, name="mix_out_merge_bwd",
                                  tok_in=[ga_blk, gb_blk, (yc, D, 0, False), (yg, D, 0, False), (dmix, D, 0, False)], sh_in=[Wo],
                                  tok_out=[(D, bf16)] * 4, **rw_x)
    gWco = _matmul(zc, dyc, "tn", f32, "conv_out_dw")
    gWgo = _matmul(og2, dyg, "tn", f32, "gla_out_dw")

    def f_ln_bwd(tok, ex, sh):
        _, vjp = jax.vjp(ln_silu, tok[0], sh[0], sh[1])
        dcz, dg, db = vjp(nt(tok[1], sh[2]))
        return [dcz], [], [dg, db, jnp.sum(dcz, axis=0, keepdims=True)]

    dcz, g_ln_g, g_ln_b, g_dwb = _rowwise(f_ln_bwd, name="conv_out_ln_bwd", tok_in=[(cz, D, 0, False), (dyc, D, 0, False)],
                                          sh_in=[conv_ln_g, conv_ln_b, Wco], tok_out=[(D, f32)], gl_out=[(1, D)] * 3, **rw_x)
    def col_shards(g):
        return jnp.transpose(g.reshape(g.shape[0], N_DEV, g.shape[1] // N_DEV), (1, 0, 2)).astype(bf16)

    def row_shards(g):
        return g.reshape(N_DEV, g.shape[0] // N_DEV, g.shape[1]).astype(bf16)

    (dca, dcb, g_dww), (r_w2d, r_w2gu, r_wo, r_wco, r_wgo) = _conv_bwd(
        p_all, dcz, dww, B=B, L=L, D=D, name="conv_bwd",
        carry=([row_shards(gW2d), col_shards(gW2gu), row_shards(gWo), row_shards(gWco), row_shards(gWgo)], True))

    def f_gla_out_bwd(tok, ex, sh):
        _, vjp = jax.vjp(gla_out, tok[0], tok[1], tok[2].astype(f32), sh[0])
        dof, _, dog, dgn = vjp(nt(tok[3], sh[1]))
        return [dof, dog], [], [dgn]

    d_o, dog, g_gn = _rowwise(f_gla_out_bwd, name="gla_out_norm_bwd",
                              tok_in=[(o_f, D, 0, False), (o_b, D, 0, False), og_blk, (dyg, D, 0, False)], sh_in=[gla_norm_g, Wgo],
                              tok_out=[(D, f32), (D, bf16)], gl_out=[(1, D)], **rw_x)

    dq_f, dk_f, dv_f, dla_f, ds_f = _gla_bwd(p_all, la_all, hist_f, d_o, zeros_s, rev=False, name="gla_x_f_bwd", **gla_x)
    dq, dk, dv, dla_b, ds_b = _gla_bwd(p_all, la_all, hist_b, d_o, zeros_s, rev=True, name="gla_x_b_bwd", add=(dq_f, dk_f, dv_f), **gla_x)
    dq_cf, dk_cf, dv_cf, dla_cf, _ = _gla_bwd(p_all, la_all, hist_cf, None, ds_f, rev=False, name="gla_ctx_f_bwd", **gla_c)
    _, dk_c, dv_c, dla_cb, _ = _gla_bwd(p_all, la_all, hist_cb, None, ds_b, rev=True, name="gla_ctx_b_bwd", add=(dq_cf, dk_cf, dv_cf), **gla_c)

    dla_all = jnp.concatenate([jnp.concatenate([dla_f, dla_b], axis=1), jnp.concatenate([dla_cf, dla_cb], axis=1)], axis=0)

    def f_decay_bwd(tok, ex, sh):
        _, vjp = jax.vjp(log_decay, tok[0].astype(f32), sh[0].astype(f32), sh[1])
        dlr, dwa, dba = vjp(tok[1])
        return [dlr], [], [dwa, dba]

    dlr, g_WA, g_BA = _rowwise(f_decay_bwd, name="log_decay_bwd", tok_in=[lr_blk, (dla_all, 2 * QK, 0, False)], sh_in=[WA, BA],
                               tok_out=[(LR_PAD, bf16)], gl_out=[(LR_PAD, 2 * QK), (1, 2 * QK)], **rw_all)

    zc_ = functools.partial(jnp.zeros, dtype=bf16)
    dp_x = jnp.concatenate([dca, dcb, dv, dog, dga, dgb, dq, dk, dlr[:T]], axis=1)
    dp_c = jnp.concatenate([zc_((Tc, 2 * D)), dv_c, zc_((Tc, 3 * D)), zc_((Tc, QK)), dk_c, dlr[T:]], axis=1)
    dp_all = jnp.concatenate([dp_x, dp_c], axis=0)
    gWin_p = _matmul(um, dp_all, "tn", f32, "in_proj_dw", tm_cap=512, tn_cap=2432)
    gwin_shards = []
    for d in range(N_DEV):
        parts = []
        for lo, hi, po in segs:
            a0, a1 = max(lo, d * wc), min(hi, (d + 1) * wc)
            if a0 < a1:
                parts.append(gWin_p[:, po + a0 - lo:po + a1 - lo])
        gwin_shards.append(jnp.concatenate(parts, axis=1))
    dum, (r_win,) = _matmul(dp_all, Win, "nt", f32, "in_proj_dx", tk_cap=2432, carry=([jnp.stack(gwin_shards).astype(bf16)], True))

    def f_mix_in_bwd(tok, ex, sh):
        _, vjp = jax.vjp(mix_in, tok[0], tok[1].astype(f32), ex[0], ex[1], ex[2], sh[0])
        dx1, df1, dgate, dsh, dsc, dg = vjp((tok[3], tok[2]))
        return [dx1, df1], [dgate, dsh, dsc], [dg]

    dx1, df1, dg2, dsh3, dsc4, dg_mix = _rowwise(
        f_mix_in_bwd, name="mix_in_bwd", tok_in=[(x_all, D, 0, False), (f1, D, 0, False), (dum, D, 0, False), (dx2, D, 0, True)],
        ex_in=[mods[2], mods[3], mods[4]], sh_in=[g_mix], tok_out=[(D, f32), (D, bf16)], ex_out=[D, D, D], gl_out=[(1, D)], **rw_all)

    dgu1 = _ffn_down_dx(df1, W1d, gu1, "ffn1_down_dx")
    gW1d = _matmul(h1, df1, "tn", f32, "ffn1_down_dw", tm_cap=1408)
    gW1gu, (r_w1d,) = _matmul(u1, dgu1, "tn", f32, "ffn1_up_dw", carry=([row_shards(gW1d)], True), halves="b")
    du1, (r_w1gu,) = _matmul(dgu1, W1gu, "nt", f32, "ffn1_up_dx", carry=([col_shards(gW1gu)], True), halves="a")

    def f_ffn_in_bwd(tok, ex, sh):
        _, vjp = jax.vjp(_rms_mod, tok[0], sh[0], ex[0], ex[1])
        dx, dg, dsh, dsc = vjp(tok[1])
        return [dx + tok[2]], [dsh, dsc], [dg]

    dx_all, dsh0, dsc1, dg_ffn1 = _rowwise(
        f_ffn_in_bwd, name="ffn1_in_bwd", tok_in=[(x_all, D, 0, False), (du1, D, 0, False), (dx1, D, 0, False)],
        ex_in=[mods[0], mods[1]], sh_in=[g_ffn1], tok_out=[(D, f32)], ex_out=[D, D], gl_out=[(1, D)], **rw_all)
    grad_x = dx_all[:T].reshape(B, L, D)

    zrow = jnp.zeros((1, 1, D), f32)
    dmod_loc = jnp.concatenate([dsh0, dsc1, dg2, dsh3, dsc4] + [jnp.concatenate([t, zrow], axis=0) for t in (dg5, dsh6, dsc7, dg8)],
                               axis=2).reshape(B + 1, N_MOD * D)
    small = [loss_p, dg_ffn1, dg_mix, g_dww[:CONV_W].reshape(1, CONV_W * D), g_dwb, g_ln_g, g_ln_b,
             g_WA[:LOWRANK, :QK].reshape(1, LOWRANK * QK), g_BA[:, :QK], g_WA[LOWRANK:2 * LOWRANK, QK:].reshape(1, LOWRANK * QK), g_BA[:, QK:],
             g_gn, dg_ffn2, dgf]
    small_w = [s.shape[1] for s in small]
    def to8(v):
        n_pad = -(-v.shape[1] // 1024) * 1024
        return jnp.pad(v, ((0, 0), (0, n_pad - v.shape[1]))).reshape(8, n_pad // 8)

    def from8(a, n):
        return a.reshape(1, a.size)[:, :n]

    dmod_g, small_g = _exchange([dmod_loc, to8(jnp.concatenate(small, axis=1))], False, "gather_small")
    dmx = dmod_g[:, :B].reshape(N_DEV * B, N_MOD * D)
    dmc = dmod_g[:, B]
    gWmod, gcc_p = _mod_bwd(c_all, c_ctx2, w_mod[0], lax.dynamic_slice(dmx, (0, me * ncm), (N_DEV * B, ncm)),
                            lax.dynamic_slice(dmc, (0, me * ncm), (N_DEV, ncm)), "mod_bwd")

    rs_out = [r_w1gu, r_w1d, r_win, r_wco, r_wgo, r_wo, r_w2gu, r_w2d]
    (gcc_g,) = _exchange([to8(gcc_p)], False, "gather_cctx")

    sums, g_cc, g_bmod = _sum_sources([small_g, gcc_g, jnp.concatenate([dmx, dmc], axis=0).reshape(N_DEV * (B + 1), 8, N_MOD * D // 8)], "sum_small")
    sums, g_cc, g_bmod = from8(sums, sum(small_w)), from8(g_cc, D), from8(g_bmod, N_MOD * D)
    offs = [0]
    for wd in small_w:
        offs.append(offs[-1] + wd)
    sm = [sums[:, offs[i]:offs[i + 1]] for i in range(len(small))]
    loss = sm[0][0, 0]
    ncd, nca = dw_weight.shape[2], w_alpha_f.shape[2]
    g_dww_loc = lax.dynamic_slice(sm[3].reshape(CONV_W, D), (0, me * ncd), (CONV_W, ncd)).reshape(1, CONV_W * ncd)
    g_waf_loc = lax.dynamic_slice(sm[7].reshape(LOWRANK, QK), (0, me * nca), (LOWRANK, nca)).reshape(1, LOWRANK * nca)
    g_wab_loc = lax.dynamic_slice(sm[9].reshape(LOWRANK, QK), (0, me * nca), (LOWRANK, nca)).reshape(1, LOWRANK * nca)

    big = {}
    for nm, wv, mv, vv, part in (("w1_gu", w1_gu, m_w1_gu, v_w1_gu, rs_out[0]), ("w1_down", w1_down, m_w1_down, v_w1_down, rs_out[1]),
                                 ("w_in", w_in, m_w_in, v_w_in, rs_out[2]), ("w_conv_out", w_conv_out, m_w_conv_out, v_w_conv_out, rs_out[3]),
                                 ("w_gla_out", w_gla_out, m_w_gla_out, v_w_gla_out, rs_out[4]), ("w_out", w_out, m_w_out, v_w_out, rs_out[5]),
                                 ("w2_gu", w2_gu, m_w2_gu, v_w2_gu, rs_out[6]), ("w2_down", w2_down, m_w2_down, v_w2_down, rs_out[7])):
        big[nm] = [t[None] for t in _adamw(wv[0], mv[0], vv[0], part, "adamw_" + nm, True)]
    big["w_mod"] = [t[None] for t in _adamw(w_mod[0], m_w_mod[0], v_w_mod[0], gWmod, "adamw_w_mod", False)]

    small_params = [("c_ctx", c_ctx, m_c_ctx, v_c_ctx, g_cc), ("b_mod", b_mod, m_b_mod, v_b_mod, g_bmod), ("g_ffn1", g_ffn1, m_g_ffn1, v_g_ffn1, sm[1]),
                    ("g_mix", g_mix, m_g_mix, v_g_mix, sm[2]), ("dw_weight", dw_weight, m_dw_weight, v_dw_weight, g_dww_loc),
                    ("dw_bias", dw_bias, m_dw_bias, v_dw_bias, sm[4]), ("conv_ln_g", conv_ln_g, m_conv_ln_g, v_conv_ln_g, sm[5]),
                    ("conv_ln_b", conv_ln_b, m_conv_ln_b, v_conv_ln_b, sm[6]), ("w_alpha_f", w_alpha_f, m_w_alpha_f, v_w_alpha_f, g_waf_loc),
                    ("b_alpha_f", b_alpha_f, m_b_alpha_f, v_b_alpha_f, sm[8]), ("w_alpha_b", w_alpha_b, m_w_alpha_b, v_w_alpha_b, g_wab_loc),
                    ("b_alpha_b", b_alpha_b, m_b_alpha_b, v_b_alpha_b, sm[10]), ("gla_norm_g", gla_norm_g, m_gla_norm_g, v_gla_norm_g, sm[11]),
                    ("g_ffn2", g_ffn2, m_g_ffn2, v_g_ffn2, sm[12]), ("g_final", g_final, m_g_final, v_g_final, sm[13])]
    flat = lambda t: t.reshape(1, t.size)
    pw, pm, pv, pg = (jnp.concatenate([flat(sp[i]) for sp in small_params], axis=1) for i in (1, 2, 3, 4))
    n_small = pw.shape[1]
    s_g, s_d, s_m, s_v = (from8(t, n_small) for t in _adamw(to8(pw), to8(pm), to8(pv), to8(pg), "adamw_small", False))
    small_out, o0 = {}, 0
    for nm, wv, _, _, _ in small_params:
        small_out[nm] = [t[:, o0:o0 + wv.size].reshape(wv.shape) for t in (s_g, s_d, s_m, s_v)]
        o0 += wv.size

    order = ["c_ctx", "w_mod", "b_mod", "g_ffn1", "w1_gu", "w1_down", "g_mix", "w_in", "dw_weight", "dw_bias", "conv_ln_g", "conv_ln_b",
             "w_conv_out", "w_alpha_f", "b_alpha_f", "w_alpha_b", "b_alpha_b", "gla_norm_g", "w_gla_out", "w_out", "g_ffn2", "w2_gu",
             "w2_down", "g_final"]
    res = {**big, **small_out}
    return (loss, grad_x, *[res[n][0] for n in order], *[res[n][1] for n in order], *[res[n][2] for n in order], *[res[n][3] for n in order])
```

```python
import functools

import jax
import jax.numpy as jnp
from jax import lax
from jax.experimental import pallas as pl
from jax.experimental.pallas import tpu as pltpu

f32, bf16 = jnp.float32, jnp.bfloat16

N_DEV = 8
HEADS = 4
LOWRANK = 16
CONV_W = 31
CONV_PAD = 16
SUBLANES = 8
CHUNK = 64
SUB = 16
GLA_ROWS = 256
GLA_SAFE_DECAY = 60.0
TAU = 16.0
EPS = 1e-6
N_MOD = 9
LR_PAD = 128
ROW_TILE = 256
V7X_VMEM_BYTES = 64 << 20
VMEM_LIMIT = (V7X_VMEM_BYTES * 3) // 4

ADAM_LR, ADAM_B1, ADAM_B2, ADAM_EPS, ADAM_WD, ADAM_STEP = 0.001, 0.9, 0.999, 1e-08, 0.01, 10

MESH = pl.DeviceIdType.MESH


def _pc(body, **kw):
    return pl.pallas_call(body, **kw)


def _params(*sem):
    return pltpu.CompilerParams(dimension_semantics=sem, vmem_limit_bytes=VMEM_LIMIT)


def _pick(n, cap, unit=128):
    best = None
    for t in range(unit, min(n, cap) + 1, unit):
        if n % t == 0:
            best = t
    return best or n


def _matmul(a, b, mode, out_dtype, name, tm_cap=1024, tn_cap=1536, tk_cap=None, carry=None, halves=None):
    tk_cap = tk_cap or (2048 if mode == "tn" else 2816)
    if halves == "a":
        (_, M, Kh), N = a.shape, b.shape[0]
        K, tk = 2 * Kh, _pick(Kh, tk_cap)
        tm, tn = _pick(M, tm_cap), _pick(N, tn_cap)
        a_spec = pl.BlockSpec((None, tm, tk), lambda i, j, k: (k // (Kh // tk), i, k % (Kh // tk)))
    elif halves == "b":
        (K, M), (_, _, Nh) = a.shape, b.shape
        N, tn = 2 * Nh, _pick(Nh, tn_cap)
        tm, tk = _pick(M, tm_cap), _pick(K, tk_cap)
    else:
        if mode == "tn":
            (K, M), N = a.shape, b.shape[1]
        elif mode == "nt":
            (M, K), N = a.shape, b.shape[0]
        else:
            (M, K), N = a.shape, b.shape[1]
        tm, tn, tk = _pick(M, tm_cap), _pick(N, tn_cap), _pick(K, tk_cap)
    nk = K // tk
    if halves != "a":
        a_spec = pl.BlockSpec((tk, tm), lambda i, j, k: (k, i)) if mode == "tn" else pl.BlockSpec((tm, tk), lambda i, j, k: (i, k))
    if halves == "b":
        b_spec = pl.BlockSpec((None, tk, tn), lambda i, j, k: (j // (Nh // tn), k, j % (Nh // tn)))
    else:
        b_spec = pl.BlockSpec((tn, tk), lambda i, j, k: (j, k)) if mode == "nt" else pl.BlockSpec((tk, tn), lambda i, j, k: (k, j))
    dims = {"nn": ((1,), (0,)), "nt": ((1,), (1,)), "tn": ((0,), (0,))}[mode]

    def body_single(a_ref, b_ref, o_ref):
        o_ref[...] = lax.dot_general(a_ref[...].astype(bf16), b_ref[...].astype(bf16), (dims, ((), ())),
                                     preferred_element_type=f32).astype(out_dtype)

    def body(a_ref, b_ref, o_ref, acc_ref):
        k = pl.program_id(2)
        part = lax.dot_general(a_ref[...].astype(bf16), b_ref[...].astype(bf16), (dims, ((), ())), preferred_element_type=f32)

        @pl.when(k == 0)
        def _():
            acc_ref[...] = part

        @pl.when(k > 0)
        def _():
            acc_ref[...] += part

        @pl.when(k == nk - 1)
        def _():
            o_ref[...] = acc_ref[...].astype(out_dtype)

    (out,), carried = _call(
        body_single if nk == 1 else body, name=name, grid=(M // tm, N // tn, nk), in_specs=[a_spec, b_spec],
        out_specs=[pl.BlockSpec((tm, tn), lambda i, j, k: (i, j))], out_shape=[jax.ShapeDtypeStruct((M, N), out_dtype)],
        scratch_shapes=[] if nk == 1 else [pltpu.VMEM((tm, tn), f32)], sem=("parallel", "parallel", "arbitrary"),
        args=(a, b), carry=carry)
    return out if carry is None else (out, carried)


def _ffn_up(u, Wgu, name, carry=None):
    M, K = u.shape
    F = Wgu.shape[1] // 2
    tm, tn = _pick(M, 512), _pick(F, 1408)
    nj = F // tn

    def body(u_ref, wa_ref, wb_ref, gu_ref, h_ref):
        uv = u_ref[...]
        a = jnp.dot(uv, wa_ref[...], preferred_element_type=f32)
        b = jnp.dot(uv, wb_ref[...], preferred_element_type=f32)
        gu_ref[0] = a.astype(bf16)
        gu_ref[1] = b.astype(bf16)
        h_ref[...] = (jax.nn.silu(a) * b).astype(bf16)

    res, carried = _call(
        body, name=name, grid=(nj, M // tm),
        in_specs=[pl.BlockSpec((tm, K), lambda j, i: (i, 0)), pl.BlockSpec((K, tn), lambda j, i: (0, j)),
                  pl.BlockSpec((K, tn), lambda j, i: (0, nj + j))],
        out_specs=[pl.BlockSpec((2, tm, tn), lambda j, i: (0, i, j)), pl.BlockSpec((tm, tn), lambda j, i: (i, j))],
        out_shape=[jax.ShapeDtypeStruct((2, M, F), bf16), jax.ShapeDtypeStruct((M, F), bf16)],
        scratch_shapes=[], sem=("parallel", "parallel"), args=(u, Wgu, Wgu), carry=carry)
    return res if carry is None else (res, carried)


def _ffn_down_dx(df, Wd, gu, name):
    M, D = df.shape
    F = Wd.shape[0]
    tm, tn = _pick(M, 512), _pick(F, 1408)

    def body(df_ref, w_ref, gu_ref, o_ref):
        dh = lax.dot_general(df_ref[...], w_ref[...], (((1,), (1,)), ((), ())), preferred_element_type=f32)
        a, b = gu_ref[0].astype(f32), gu_ref[1].astype(f32)
        sg = jax.nn.sigmoid(a)
        o_ref[0] = (dh * b * sg * (1.0 + a * (1.0 - sg))).astype(bf16)
        o_ref[1] = (dh * a * sg).astype(bf16)

    return _pc(
        body, name=name, grid=(F // tn, M // tm),
        in_specs=[pl.BlockSpec((tm, D), lambda j, i: (i, 0)), pl.BlockSpec((tn, D), lambda j, i: (j, 0)),
                  pl.BlockSpec((2, tm, tn), lambda j, i: (0, i, j))],
        out_specs=pl.BlockSpec((2, tm, tn), lambda j, i: (0, i, j)), out_shape=jax.ShapeDtypeStruct((2, M, F), bf16),
        compiler_params=_params("parallel", "parallel"))(df, Wd, gu)


def _rowwise(fn, *, name, tm, n_tiles, tpe, nx_tiles, n_ex, tok_in=(), ex_in=(), sh_in=(), tok_out=(), ex_out=(), gl_out=(), carry=None):
    def seg(i):
        return jnp.minimum(i // tpe, n_ex - 1)

    in_specs, args = [], []
    for arr, w, cb, x_only in tok_in:
        if x_only:
            in_specs.append(pl.BlockSpec((tm, w), functools.partial(lambda i, cb: (jnp.minimum(i, nx_tiles - 1), cb), cb=cb)))
        else:
            in_specs.append(pl.BlockSpec((tm, w), functools.partial(lambda i, cb: (i, cb), cb=cb)))
        args.append(arr)
    for arr in ex_in:
        in_specs.append(pl.BlockSpec((1, 1, arr.shape[-1]), lambda i: (seg(i), 0, 0)))
        args.append(arr)
    for arr in sh_in:
        in_specs.append(pl.BlockSpec(arr.shape, functools.partial(lambda i, nd: (0,) * nd, nd=arr.ndim)))
        args.append(arr)
    out_specs, out_shape = [], []
    for w, dt in tok_out:
        out_specs.append(pl.BlockSpec((tm, w), lambda i: (i, 0)))
        out_shape.append(jax.ShapeDtypeStruct((n_tiles * tm, w), dt))
    for w in ex_out:
        out_specs.append(pl.BlockSpec((1, 1, w), lambda i: (seg(i), 0, 0)))
        out_shape.append(jax.ShapeDtypeStruct((n_ex, 1, w), f32))
    for r, w in gl_out:
        out_specs.append(pl.BlockSpec((r, w), lambda i: (0, 0)))
        out_shape.append(jax.ShapeDtypeStruct((r, w), f32))
    n_tok, n_exi, n_sh = len(tok_in), len(ex_in), len(sh_in)
    n_to, n_eo = len(tok_out), len(ex_out)
    x_only_flags = [t[3] for t in tok_in]

    def body(*refs):
        i = pl.program_id(0)
        ins, outs = refs[: n_tok + n_exi + n_sh], refs[n_tok + n_exi + n_sh:]
        is_x = i < nx_tiles
        tok_vals = []
        for r, xo in zip(ins[:n_tok], x_only_flags):
            v = r[...]
            tok_vals.append(jnp.where(is_x, v, jnp.zeros_like(v)) if xo else v)
        ex_vals = [r[0] for r in ins[n_tok:n_tok + n_exi]]
        sh_vals = [r[...] for r in ins[n_tok + n_exi:]]
        t_o, e_o, g_o = fn(tok_vals, ex_vals, sh_vals)
        for r, v in zip(outs[:n_to], t_o):
            r[...] = v.astype(r.dtype)
        first = jnp.logical_and(i % tpe == 0, i <= nx_tiles)
        for r, v in zip(outs[n_to:n_to + n_eo], e_o):
            @pl.when(first)
            def _(r=r, v=v):
                r[0] = v

            @pl.when(jnp.logical_not(first))
            def _(r=r, v=v):
                r[0] += v
        for r, v in zip(outs[n_to + n_eo:], g_o):
            @pl.when(i == 0)
            def _(r=r, v=v):
                r[...] = v

            @pl.when(i > 0)
            def _(r=r, v=v):
                r[...] += v

    res, carried = _call(body, name=name, grid=(n_tiles,), in_specs=in_specs, out_specs=out_specs, out_shape=out_shape,
                         scratch_shapes=[], sem=("arbitrary",), args=args, carry=carry)
    return res if carry is None else (res, carried)


def _rms_mod(x, g, sh, sc):
    y = x * lax.rsqrt(jnp.mean(x * x, axis=-1, keepdims=True) + EPS) * g
    return y * (1.0 + sc) + sh


def _log_sigmoid(z):
    return jnp.minimum(z, 0.0) - jnp.log(1.0 + jnp.exp(-jnp.abs(z)))


def _head_rms(o, DV):
    parts = []
    for h in range(HEADS):
        oh = o[:, h * DV:(h + 1) * DV]
        parts.append(oh * lax.rsqrt(jnp.mean(oh * oh, axis=-1, keepdims=True) + EPS))
    return jnp.concatenate(parts, axis=1)


@functools.partial(jax.custom_vjp, nondiff_argnums=(2,))
def _bdot(a, b, dims):
    return lax.dot_general(a.astype(bf16), b.astype(bf16), (((dims[0],), (dims[1],)), ((), ())), preferred_element_type=f32)


def _bdot_fwd(a, b, dims):
    return _bdot(a, b, dims), (a, b)


def _bdot_bwd(dims, res, g):
    a, b = res
    ca, cb = dims
    da = _bdot(g, b, (1, 1 - cb)) if ca == 1 else _bdot(b, g, (1 - cb, 1))
    db = _bdot(a, g, (1 - ca, 0)) if cb == 0 else _bdot(g, a, (0, 1 - ca))
    return da, db


_bdot.defvjp(_bdot_fwd, _bdot_bwd)


def _split_dot(m, x, dims):
    mb, rem, acc = m.astype(bf16), x, None
    for _ in range(3):
        piece = rem.astype(bf16)
        rem = rem - piece.astype(f32)
        part = lax.dot_general(mb, piece, (((dims[0],), (dims[1],)), ((), ())), preferred_element_type=f32)
        acc = part if acc is None else acc + part
    return acc


@jax.custom_vjp
def _tri_cumsum(tri, g):
    return _split_dot(tri, g, (1, 0))


def _tri_cumsum_fwd(tri, g):
    return _tri_cumsum(tri, g), tri


def _tri_cumsum_bwd(tri, db):
    return jnp.zeros_like(tri), _split_dot(tri, db, (0, 0))


_tri_cumsum.defvjp(_tri_cumsum_fwd, _tri_cumsum_bwd)


def _gla_chunk(St, q, k, v, g, *, rev, scale, exact):
    C, DK = q.shape
    r = lax.broadcasted_iota(jnp.int32, (C, C), 0)
    c = lax.broadcasted_iota(jnp.int32, (C, C), 1)
    causal = (r <= c) if rev else (r >= c)
    b = _tri_cumsum(causal.astype(f32), g)
    qs = q * scale
    qe = qs * jnp.exp(b)
    inter = _bdot(qe, St, (1, 1))
    b_last = b[0:1] if rev else b[C - 1:C]
    kd = k * jnp.exp(b_last - b)
    St_new = St * jnp.exp(b_last) + _bdot(v, kd, (0, 0))
    if not exact:
        att = jnp.where(causal, _bdot(qe, k * jnp.exp(-b), (1, 1)), 0.0)
        return St_new, inter + _bdot(att, v, (1, 0))
    rr = lax.broadcasted_iota(jnp.int32, (SUB, SUB, DK), 0)
    cc = lax.broadcasted_iota(jnp.int32, (SUB, SUB, DK), 1)
    m3 = (rr <= cc) if rev else (rr >= cc)
    outs = []
    for i in range(C // SUB):
        lo, hi = i * SUB, (i + 1) * SUB
        bi, qi, ki, vi = b[lo:hi], qs[lo:hi], k[lo:hi], v[lo:hi]
        rel = bi[:, None, :] - bi[None, :, :]
        e = jnp.where(m3, jnp.exp(jnp.where(m3, rel, 0.0)), 0.0)
        att = jnp.sum(qi[:, None, :] * e * ki[None, :, :], axis=-1)
        acc = _bdot(att, vi, (1, 0))
        ref_row = b[hi - 1:hi] if rev else b[lo:lo + 1]
        prev = slice(hi, C) if rev else slice(0, lo)
        if (hi < C) if rev else (lo > 0):
            qn = qi * jnp.exp(bi - ref_row)
            ks = k[prev] * jnp.exp(ref_row - b[prev])
            acc = acc + _bdot(_bdot(qn, ks, (1, 1)), v[prev], (1, 0))
        outs.append(acc)
    return St_new, inter + jnp.concatenate(outs, axis=0)


def _mild_decay(la_ref):
    return jnp.min(la_ref[...]) >= -GLA_SAFE_DECAY / CHUNK


def _gla_specs(D, rev_blocks, row0, seq):
    DK, DV = D // (2 * HEADS), D // HEADS
    nblk = seq // GLA_ROWS
    rb0 = row0 // GLA_ROWS

    def blk(j):
        return (nblk - 1 - j) if rev_blocks else j

    return DK, DV, nblk, rb0, blk


def _gla_in_specs(D, rev, rows):
    QK = D // 2
    return [
        pl.BlockSpec((GLA_ROWS, QK), lambda b, j: (rows(b, j), 6 * D // QK)),
        pl.BlockSpec((GLA_ROWS, QK), lambda b, j: (rows(b, j), 6 * D // QK + 1)),
        pl.BlockSpec((GLA_ROWS, D), lambda b, j: (rows(b, j), 2)),
        pl.BlockSpec((GLA_ROWS, QK), lambda b, j: (rows(b, j), 1 if rev else 0)),
    ]


def _gla_fwd(p_all, la_all, s0, *, rev, row0, nb, seq, D, name, carry=None):
    DK, DV, nblk, rb0, blk = _gla_specs(D, rev, row0, seq)
    cpb = GLA_ROWS // CHUNK

    def rows(b, j):
        return rb0 + b * nblk + blk(j)

    in_specs = _gla_in_specs(D, rev, rows) + [pl.BlockSpec((1, HEADS, DV, DK), lambda b, j: (b, 0, 0, 0))]
    out_specs = [
        pl.BlockSpec((GLA_ROWS, D), lambda b, j: (b * nblk + blk(j), 0)),
        pl.BlockSpec((1, HEADS, cpb, DV, DK), lambda b, j: (b, 0, blk(j), 0, 0)),
        pl.BlockSpec((1, HEADS, DV, DK), lambda b, j: (b, 0, 0, 0)),
    ]
    out_shape = [
        jax.ShapeDtypeStruct((nb * seq, D), bf16),
        jax.ShapeDtypeStruct((nb, HEADS, seq // CHUNK, DV, DK), bf16),
        jax.ShapeDtypeStruct((nb, HEADS, DV, DK), f32),
    ]
    chunk = functools.partial(_gla_chunk, rev=rev, scale=DK ** -0.5)

    def body(q_ref, k_ref, v_ref, la_ref, s0_ref, o_ref, hist_ref, sfin_ref, st_ref):
        j = pl.program_id(1)

        @pl.when(j == 0)
        def _():
            st_ref[...] = s0_ref[0]

        def step(ci, exact):
            cc = (cpb - 1 - ci) if rev else ci
            sl = pl.ds(cc * CHUNK, CHUNK)
            for h in range(HEADS):
                kq, kv = pl.ds(h * DK, DK), pl.ds(h * DV, DV)
                St = st_ref[h]
                hist_ref[0, h, cc] = St.astype(bf16)
                St2, o = chunk(St, q_ref[sl, kq].astype(f32), k_ref[sl, kq].astype(f32), v_ref[sl, kv].astype(f32), la_ref[sl, kq],
                               exact=exact)
                o_ref[sl, kv] = o.astype(bf16)
                st_ref[h] = St2

        mild = _mild_decay(la_ref)
        for exact in (False, True):
            @pl.when(jnp.logical_not(mild) if exact else mild)
            def _(exact=exact):
                for ci in range(cpb):
                    step(ci, exact)

        @pl.when(j == nblk - 1)
        def _():
            sfin_ref[0] = st_ref[...]

    res, carried = _call(body, name=name, grid=(nb, nblk), in_specs=in_specs, out_specs=out_specs, out_shape=out_shape,
                         scratch_shapes=[pltpu.VMEM((HEADS, DV, DK), f32)], sem=("parallel", "arbitrary"),
                         args=(p_all, p_all, p_all, la_all, s0), carry=carry)
    return res if carry is None else (res, carried)


def _gla_bwd(p_all, la_all, hist, do, dsfin, *, rev, row0, nb, seq, D, name, add=None):
    DK, DV, nblk, rb0, blk = _gla_specs(D, not rev, row0, seq)
    cpb = GLA_ROWS // CHUNK
    QK = HEADS * DK
    has_do = do is not None

    def rows(b, j):
        return rb0 + b * nblk + blk(j)

    in_specs = _gla_in_specs(D, rev, rows) + [
        pl.BlockSpec((1, HEADS, cpb, DV, DK), lambda b, j: (b, 0, blk(j), 0, 0)),
        pl.BlockSpec((1, HEADS, DV, DK), lambda b, j: (b, 0, 0, 0)),
    ]
    args = [p_all, p_all, p_all, la_all, hist, dsfin]
    if has_do:
        in_specs.append(pl.BlockSpec((GLA_ROWS, D), lambda b, j: (b * nblk + blk(j), 0)))
        args.append(do)
    if add is not None:
        in_specs += [pl.BlockSpec((GLA_ROWS, t.shape[1]), lambda b, j: (b * nblk + blk(j), 0)) for t in add]
        args += list(add)
    gdt = f32 if add is None else bf16
    out_specs = [
        pl.BlockSpec((GLA_ROWS, QK), lambda b, j: (b * nblk + blk(j), 0)),
        pl.BlockSpec((GLA_ROWS, QK), lambda b, j: (b * nblk + blk(j), 0)),
        pl.BlockSpec((GLA_ROWS, D), lambda b, j: (b * nblk + blk(j), 0)),
        pl.BlockSpec((GLA_ROWS, QK), lambda b, j: (b * nblk + blk(j), 0)),
        pl.BlockSpec((1, HEADS, DV, DK), lambda b, j: (b, 0, 0, 0)),
    ]
    out_shape = [
        jax.ShapeDtypeStruct((nb * seq, QK), gdt), jax.ShapeDtypeStruct((nb * seq, QK), gdt),
        jax.ShapeDtypeStruct((nb * seq, D), gdt), jax.ShapeDtypeStruct((nb * seq, QK), f32),
        jax.ShapeDtypeStruct((nb, HEADS, DV, DK), f32),
    ]
    chunk = functools.partial(_gla_chunk, rev=rev, scale=DK ** -0.5)

    def body(*refs):
        refs = list(refs)
        q_ref, k_ref, v_ref, la_ref, hist_ref, dsfin_ref = refs[:6]
        do_ref = refs[6] if has_do else None
        add_refs = refs[6 + has_do:len(refs) - 6]
        dq_ref, dk_ref, dv_ref, dla_ref, ds0_ref, ds_ref = refs[len(refs) - 6:]
        j = pl.program_id(1)

        @pl.when(j == 0)
        def _():
            ds_ref[...] = dsfin_ref[0]

        def step(ci, exact):
            cc = ci if rev else (cpb - 1 - ci)
            sl = pl.ds(cc * CHUNK, CHUNK)
            for h in range(HEADS):
                kq, kv = pl.ds(h * DK, DK), pl.ds(h * DV, DV)
                prim = (hist_ref[0, h, cc].astype(f32), q_ref[sl, kq].astype(f32), k_ref[sl, kq].astype(f32), v_ref[sl, kv].astype(f32), la_ref[sl, kq])
                _, vjp = jax.vjp(functools.partial(chunk, exact=exact), *prim)
                d_o = do_ref[sl, kv].astype(f32) if has_do else jnp.zeros((CHUNK, DV), f32)
                dSt, dq, dk, dv, dg = vjp((ds_ref[h], d_o))
                if add is not None:
                    dq, dk, dv = dq + add_refs[0][sl, kq], dk + add_refs[1][sl, kq], dv + add_refs[2][sl, kv]
                dq_ref[sl, kq] = dq.astype(gdt)
                dk_ref[sl, kq] = dk.astype(gdt)
                dv_ref[sl, kv] = dv.astype(gdt)
                dla_ref[sl, kq] = dg
                ds_ref[h] = dSt

        mild = _mild_decay(la_ref)
        for exact in (False, True):
            @pl.when(jnp.logical_not(mild) if exact else mild)
            def _(exact=exact):
                for ci in range(cpb):
                    step(ci, exact)

        @pl.when(j == nblk - 1)
        def _():
            ds0_ref[0] = ds_ref[...]

    return _pc(body, name=name, grid=(nb, nblk), in_specs=in_specs, out_specs=out_specs, out_shape=out_shape,
               scratch_shapes=[pltpu.VMEM((HEADS, DV, DK), f32)], compiler_params=_params("parallel", "arbitrary"))(*args)


def _conv_fwd(p_all, dw_w, dw_b, *, B, L, D, name):
    ct = _pick(D, 256)
    nj = D // ct
    st = _pick(L, 128, 8)
    off = CONV_PAD - CONV_W // 2

    def body(a_ref, b_ref, w_ref, bias_ref, o_ref, zs_ref):
        _fill_shifted(zs_ref, L, lambda t0, n: a_ref[pl.ds(t0, n), :].astype(f32) * jax.nn.sigmoid(b_ref[pl.ds(t0, n), :].astype(f32)))
        for t0 in range(0, L, st):
            acc = jnp.zeros((st, ct), f32) + bias_ref[...]
            for k in range(CONV_W):
                acc = acc + w_ref[pl.ds(k, 1), :] * _window(zs_ref, t0 + k + off, st)
            o_ref[pl.ds(t0, st), :] = acc.astype(bf16)

    return _pc(
        body, name=name, grid=(B, nj),
        in_specs=[pl.BlockSpec((L, ct), lambda b, j: (b, j)), pl.BlockSpec((L, ct), lambda b, j: (b, nj + j)),
                  pl.BlockSpec((CONV_W, ct), lambda b, j: (0, j)), pl.BlockSpec((1, ct), lambda b, j: (0, j))],
        out_specs=pl.BlockSpec((L, ct), lambda b, j: (b, j)), out_shape=jax.ShapeDtypeStruct((B * L, D), bf16),
        scratch_shapes=[pltpu.VMEM((SUBLANES, L + 2 * CONV_PAD, ct), f32)], compiler_params=_params("parallel", "parallel"),
    )(p_all, p_all, dw_w, dw_b)


def _fill_shifted(zs_ref, L, rows):
    lp = L + 2 * CONV_PAD
    ct = zs_ref.shape[2]
    step = 256
    zs_ref[0, pl.ds(0, CONV_PAD), :] = jnp.zeros((CONV_PAD, ct), f32)
    zs_ref[0, pl.ds(CONV_PAD + L, CONV_PAD), :] = jnp.zeros((CONV_PAD, ct), f32)
    for t0 in range(0, L, step):
        n = min(step, L - t0)
        zs_ref[0, pl.ds(CONV_PAD + t0, n), :] = rows(t0, n)
    for r in range(1, SUBLANES):
        for i0 in range(0, lp - SUBLANES, step):
            n = min(step, lp - SUBLANES - i0)
            zs_ref[r, pl.ds(i0, n), :] = zs_ref[0, pl.ds(i0 + r, n), :]


def _window(zs_ref, start, n):
    r = start % SUBLANES
    return zs_ref[r, pl.ds(start - r, n), :]


def _conv_bwd(p_all, dcz, dw_w, *, B, L, D, name, carry=None):
    ct = _pick(D, 128)
    nj = D // ct
    st = _pick(L, 256, 8)
    half = CONV_W // 2

    def body(a_ref, b_ref, dcz_ref, w_ref, da_ref, db_ref, ddw_ref, zs_ref, ds_ref):
        bi = pl.program_id(1)
        _fill_shifted(zs_ref, L, lambda t0, n: a_ref[pl.ds(t0, n), :].astype(f32) * jax.nn.sigmoid(b_ref[pl.ds(t0, n), :].astype(f32)))
        _fill_shifted(ds_ref, L, lambda t0, n: dcz_ref[pl.ds(t0, n), :].astype(f32))

        @pl.when(bi == 0)
        def _():
            ddw_ref[...] = jnp.zeros_like(ddw_ref)

        for t0 in range(0, L, st):
            acc = jnp.zeros((st, ct), f32)
            for k in range(CONV_W):
                acc = acc + w_ref[pl.ds(k, 1), :] * _window(ds_ref, t0 + CONV_PAD + half - k, st)
            a_t = a_ref[pl.ds(t0, st), :].astype(f32)
            sg_t = jax.nn.sigmoid(b_ref[pl.ds(t0, st), :].astype(f32))
            da_ref[pl.ds(t0, st), :] = (acc * sg_t).astype(bf16)
            db_ref[pl.ds(t0, st), :] = (acc * a_t * sg_t * (1.0 - sg_t)).astype(bf16)

        parts = [jnp.zeros((SUBLANES, ct), f32) for _ in range(CONV_W)]
        sw = _pick(L, 64, SUBLANES)
        for t0 in range(0, L, sw):
            dout = dcz_ref[pl.ds(t0, sw), :].astype(f32)
            for k in range(CONV_W):
                prod = dout * _window(zs_ref, t0 + k + CONV_PAD - half, sw)
                for i in range(0, sw, SUBLANES):
                    parts[k] = parts[k] + prod[i:i + SUBLANES]
        for k in range(CONV_W):
            ddw_ref[pl.ds(k, 1), :] += jnp.sum(parts[k], axis=0, keepdims=True)

    res, carried = _call(
        body, name=name, grid=(nj, B),
        in_specs=[pl.BlockSpec((L, ct), lambda j, b: (b, j)), pl.BlockSpec((L, ct), lambda j, b: (b, nj + j)),
                  pl.BlockSpec((L, ct), lambda j, b: (b, j)), pl.BlockSpec((CONV_W, ct), lambda j, b: (0, j))],
        out_specs=[pl.BlockSpec((L, ct), lambda j, b: (b, j)), pl.BlockSpec((L, ct), lambda j, b: (b, j)),
                   pl.BlockSpec((2 * CONV_PAD, ct), lambda j, b: (0, j))],
        out_shape=[jax.ShapeDtypeStruct((B * L, D), bf16), jax.ShapeDtypeStruct((B * L, D), bf16),
                   jax.ShapeDtypeStruct((2 * CONV_PAD, D), f32)],
        scratch_shapes=[pltpu.VMEM((SUBLANES, L + 2 * CONV_PAD, ct), f32), pltpu.VMEM((SUBLANES, L + 2 * CONV_PAD, ct), f32)],
        sem=("parallel", "arbitrary"), args=(p_all, p_all, dcz, dw_w), carry=carry)
    return res if carry is None else (res, carried)


def _exchange(arrs, scatter, name):
    ex = _Exchange(arrs, scatter)
    n = ex.n

    def body(*refs):
        ex.start(refs[:n], refs[n:2 * n], refs[2 * n:])
        ex.finish(refs[:n], refs[n:2 * n], refs[2 * n:])

    res = _pc(body, name=name, in_specs=ex.specs, out_specs=ex.specs, out_shape=ex.out_shape, scratch_shapes=ex.scratch)(*arrs)
    return list(res)


class _Exchange:
    def __init__(self, arrs, scatter):
        self.arrs, self.scatter, self.n = list(arrs), scatter, len(arrs)
        self.out_shape = [jax.ShapeDtypeStruct(((N_DEV,) + a.shape[1:]) if scatter else ((N_DEV,) + a.shape), a.dtype) for a in arrs]
        self.specs = [pl.BlockSpec(memory_space=pl.ANY)] * self.n
        self.scratch = [pltpu.SemaphoreType.DMA((self.n, N_DEV - 1)), pltpu.SemaphoreType.DMA((self.n, N_DEV - 1)),
                        pltpu.SemaphoreType.DMA((self.n,))]

    def _copies(self, ins, outs, sems, landing):
        send_sems, recv_sems, local_sems = sems
        me = 4 * lax.axis_index("x") + 2 * lax.axis_index("y") + lax.axis_index("c")
        if landing:
            local = []
        else:
            local = [pltpu.make_async_copy(ins[a].at[me] if self.scatter else ins[a], outs[a].at[me], local_sems.at[a]) for a in range(self.n)]
        remote = []
        for k in range(1, N_DEV):
            p = (me + (N_DEV - k if landing else k)) % N_DEV
            for a in range(self.n):
                remote.append(pltpu.make_async_remote_copy(
                    src_ref=ins[a].at[p] if self.scatter else ins[a], dst_ref=outs[a].at[p if landing else me],
                    send_sem=send_sems.at[a, k - 1], recv_sem=recv_sems.at[a, k - 1],
                    device_id=(p // 4, (p // 2) % 2, p % 2), device_id_type=MESH))
        return local, remote

    def _gather_plan(self, ins, outs, sems):
        send_sems, recv_sems, local_sems = sems
        x, y, c = lax.axis_index("x"), lax.axis_index("y"), lax.axis_index("c")
        chips = [(1 - x, y), (x, 1 - y), (1 - x, 1 - y)]

        def blk(px, py, pc):
            return 4 * px + 2 * py + pc

        def copy(a, k, block, to, own):
            return pltpu.make_async_remote_copy(
                src_ref=ins[a] if own else outs[a].at[block], dst_ref=outs[a].at[block],
                send_sem=send_sems.at[a, k], recv_sem=recv_sems.at[a, k], device_id=to, device_id_type=MESH)

        me = blk(x, y, c)
        local = [pltpu.make_async_copy(ins[a], outs[a].at[me], local_sems.at[a]) for a in range(self.n)]
        return local, copy, me, (x, y, 1 - c), chips, blk, c

    def start(self, ins, outs, sems):
        if self.scatter:
            local, sends = self._copies(ins, outs, sems, False)
            for cp in local + sends:
                cp.start()
            return
        local, copy, me, sibling, chips, _, c = self._gather_plan(ins, outs, sems)
        for cp in local:
            cp.start()
        for a in range(self.n):
            copy(a, 0, me, sibling, True).start()
            for j, chip in enumerate(chips):
                copy(a, 1 + j, me, (*chip, c), True).start()

    def finish(self, ins, outs, sems):
        if self.scatter:
            for cp in self._copies(ins, outs, sems, True)[1]:
                cp.wait_recv()
            local, sends = self._copies(ins, outs, sems, False)
            for cp in sends:
                cp.wait_send()
            for cp in local:
                cp.wait()
            return
        local, copy, me, sibling, chips, blk, c = self._gather_plan(ins, outs, sems)
        for j, chip in enumerate(chips):
            for a in range(self.n):
                copy(a, 1 + j, blk(*chip, c), sibling, True).wait_recv()
                copy(a, 4 + j, blk(*chip, c), sibling, False).start()
        for a in range(self.n):
            copy(a, 0, blk(*sibling), sibling, True).wait_recv()
            for j, chip in enumerate(chips):
                copy(a, 4 + j, blk(*chip, 1 - c), sibling, False).wait_recv()
        for a in range(self.n):
            copy(a, 0, me, sibling, True).wait_send()
            for j, chip in enumerate(chips):
                copy(a, 1 + j, me, (*chip, c), True).wait_send()
                copy(a, 4 + j, blk(*chip, c), sibling, False).wait_send()
        for cp in local:
            cp.wait()


def _carried(inner, n_in, n_out, grid, ex):
    n = ex.n

    def body(*refs):
        own_in, c_in = refs[:n_in], refs[n_in:n_in + n]
        own_out, c_out = refs[n_in + n:n_in + n + n_out], refs[n_in + n + n_out:n_in + 2 * n + n_out]
        rest = refs[n_in + 2 * n + n_out:]
        own_scr, sems = rest[:len(rest) - 3], rest[len(rest) - 3:]
        pids = [pl.program_id(d) for d in range(len(grid))]
        first = functools.reduce(jnp.logical_and, [p == 0 for p in pids])
        last = functools.reduce(jnp.logical_and, [p == g - 1 for p, g in zip(pids, grid)])

        @pl.when(first)
        def _():
            ex.start(c_in, c_out, sems)

        inner(*own_in, *own_out, *own_scr)

        @pl.when(last)
        def _():
            ex.finish(c_in, c_out, sems)

    return body


def _call(inner, *, name, grid, in_specs, out_specs, out_shape, scratch_shapes, sem, args, carry=None):
    if carry is None:
        res = _pc(inner, name=name, grid=grid, in_specs=in_specs, out_specs=out_specs, out_shape=out_shape,
                  scratch_shapes=scratch_shapes, compiler_params=_params(*sem))(*args)
        return list(res), None
    ex = _Exchange(*carry)
    res = _pc(_carried(inner, len(in_specs), len(out_specs), grid, ex), name=name, grid=grid,
              in_specs=list(in_specs) + ex.specs, out_specs=list(out_specs) + ex.specs, out_shape=list(out_shape) + ex.out_shape,
              scratch_shapes=list(scratch_shapes) + ex.scratch, compiler_params=_params(*(["arbitrary"] * len(grid))))(*args, *ex.arrs)
    res = list(res)
    return res[:len(out_specs)], res[len(out_specs):]


def _mod_fwd(c_all, c_ctx, w_loc, b_loc, name):
    nr, D = c_all.shape
    nc = w_loc.shape[1]

    def body(c_ref, cc_ref, w_ref, b_ref, o_ref):
        a = jnp.concatenate([c_ref[...], jnp.broadcast_to(cc_ref[...], (8, D))], axis=0)
        s = jax.nn.silu(a).astype(bf16)
        o_ref[...] = jnp.dot(s, w_ref[...].astype(bf16), preferred_element_type=f32) + b_ref[...]

    return _pc(body, name=name, out_shape=jax.ShapeDtypeStruct((nr + 8, nc), f32), compiler_params=_params())(c_all, c_ctx, w_loc, b_loc)


def _mod_bwd(c_all, c_ctx, w_loc, dmx_loc, dmc_loc, name):
    nr, D = c_all.shape
    nc = w_loc.shape[1]

    def body(c_ref, cc_ref, w_ref, dmx_ref, dmc_ref, gw_ref, gc_ref):
        cc = cc_ref[...]
        a = jnp.concatenate([c_ref[...], jnp.broadcast_to(cc, (N_DEV, D))], axis=0)
        s = jax.nn.silu(a).astype(bf16)
        g = jnp.concatenate([dmx_ref[...], dmc_ref[...]], axis=0).astype(bf16)
        gw_ref[...] = lax.dot_general(s, g, (((0,), (0,)), ((), ())), preferred_element_type=f32)
        dmc = jnp.sum(dmc_ref[...], axis=0, keepdims=True)
        ds = lax.dot_general(jnp.broadcast_to(dmc, (8, nc)).astype(bf16), w_ref[...].astype(bf16), (((1,), (1,)), ((), ())),
                             preferred_element_type=f32)[0:1]
        sg = jax.nn.sigmoid(cc)
        gc_ref[...] = ds * (sg * (1.0 + cc * (1.0 - sg)))

    return _pc(body, name=name, out_shape=[jax.ShapeDtypeStruct((D, nc), f32), jax.ShapeDtypeStruct((1, D), f32)],
               compiler_params=_params())(c_all, c_ctx, w_loc, dmx_loc, dmc_loc)


def _adamw_math(w, g, m, v):
    m2 = ADAM_B1 * m + (1.0 - ADAM_B1) * g
    v2 = ADAM_B2 * v + (1.0 - ADAM_B2) * jnp.square(g)
    m_hat = m2 / (1.0 - ADAM_B1 ** ADAM_STEP)
    v_hat = v2 / (1.0 - ADAM_B2 ** ADAM_STEP)
    delta = -ADAM_LR * (m_hat / (jnp.sqrt(v_hat) + ADAM_EPS) + ADAM_WD * w)
    return delta, m2, v2


def _adamw(w, m, v, g, name, partials):
    r, cdim = w.shape
    tr = _pick(r, 256, 8)

    def body(w_ref, m_ref, v_ref, g_ref, og_ref, od_ref, om_ref, ov_ref):
        if partials:
            g = g_ref[0].astype(f32)
            for s in range(1, N_DEV):
                g = g + g_ref[s].astype(f32)
        else:
            g = g_ref[...]
        d, m2, v2 = _adamw_math(w_ref[...], g, m_ref[...], v_ref[...])
        og_ref[...] = g
        od_ref[...] = d
        om_ref[...] = m2
        ov_ref[...] = v2

    blk = pl.BlockSpec((tr, cdim), lambda i: (i, 0))
    g_spec = pl.BlockSpec((N_DEV, tr, cdim), lambda i: (0, i, 0)) if partials else blk
    return _pc(body, name=name, grid=(r // tr,), in_specs=[blk, blk, blk, g_spec], out_specs=[blk] * 4,
               out_shape=[jax.ShapeDtypeStruct((r, cdim), f32)] * 4, compiler_params=_params("parallel"))(w, m, v, g)


def _sum_sources(parts, name):
    def body(*refs):
        for i_ref, o_ref in zip(refs[:len(parts)], refs[len(parts):]):
            acc = i_ref[0]
            for s in range(1, i_ref.shape[0]):
                acc = acc + i_ref[s]
            o_ref[...] = acc

    return list(_pc(body, name=name, out_shape=[jax.ShapeDtypeStruct(p.shape[1:], f32) for p in parts],
                    compiler_params=_params())(*parts))


def kernel(x, c, ctx, c_ctx, w_mod, b_mod, g_ffn1, w1_gu, w1_down, g_mix, w_in, dw_weight, dw_bias, conv_ln_g, conv_ln_b, w_conv_out, w_alpha_f, b_alpha_f, w_alpha_b, b_alpha_b, gla_norm_g, w_gla_out, w_out, g_ffn2, w2_gu, w2_down, g_final, loss_target, m_c_ctx, m_w_mod, m_b_mod, m_g_ffn1, m_w1_gu, m_w1_down, m_g_mix, m_w_in, m_dw_weight, m_dw_bias, m_conv_ln_g, m_conv_ln_b, m_w_conv_out, m_w_alpha_f, m_b_alpha_f, m_w_alpha_b, m_b_alpha_b, m_gla_norm_g, m_w_gla_out, m_w_out, m_g_ffn2, m_w2_gu, m_w2_down, m_g_final, v_c_ctx, v_w_mod, v_b_mod, v_g_ffn1, v_w1_gu, v_w1_down, v_g_mix, v_w_in, v_dw_weight, v_dw_bias, v_conv_ln_g, v_conv_ln_b, v_w_conv_out, v_w_alpha_f, v_b_alpha_f, v_w_alpha_b, v_b_alpha_b, v_gla_norm_g, v_w_gla_out, v_w_out, v_g_ffn2, v_w2_gu, v_w2_down, v_g_final):
    B, L, D = x.shape
    Lc = ctx.shape[1]
    T, Tc = B * L, B * Lc
    Tall = T + Tc
    F = w1_down.shape[1] * N_DEV
    DK, DV = D // (2 * HEADS), D // HEADS
    QK = HEADS * DK
    PW = 7 * D + LR_PAD
    tm = ROW_TILE
    tpe = L // tm
    nx, nall = T // tm, Tall // tm
    me = 4 * lax.axis_index("x") + 2 * lax.axis_index("y") + lax.axis_index("c")

    rw_all = dict(tm=tm, n_tiles=nall, tpe=tpe, nx_tiles=nx, n_ex=B + 1)
    rw_x = dict(tm=tm, n_tiles=nx, tpe=tpe, nx_tiles=nx, n_ex=B)
    m2 = 2 if (L % (2 * tm) == 0 and Tc % (2 * tm) == 0) else 1
    rw_all2 = dict(tm=m2 * tm, n_tiles=nall // m2, tpe=tpe // m2, nx_tiles=nx // m2, n_ex=B + 1)
    rw_x2 = dict(tm=m2 * tm, n_tiles=nx // m2, tpe=tpe // m2, nx_tiles=nx // m2, n_ex=B)

    dww_g, waf_g, wab_g, c_g = _exchange([dw_weight[0], w_alpha_f[0], w_alpha_b[0], c], False, "gather_first")

    def cols(gat):
        return jnp.transpose(gat, (1, 0, 2)).reshape(gat.shape[1], N_DEV * gat.shape[2])

    def rows_(gat):
        return gat.reshape(N_DEV * gat.shape[1], gat.shape[2])

    dww = cols(dww_g)
    WA = jnp.zeros((LR_PAD, 2 * QK), f32).at[:LOWRANK, :QK].set(cols(waf_g)).at[LOWRANK:2 * LOWRANK, QK:].set(cols(wab_g)).astype(bf16)
    BA = jnp.concatenate([b_alpha_f, b_alpha_b], axis=1)
    c_all = c_g.reshape(N_DEV * B, D)
    c_ctx2 = c_ctx.reshape(1, D)

    ncm = w_mod.shape[2]
    b_mod_loc = lax.dynamic_slice(b_mod, (0, me * ncm), (1, ncm))
    mod_loc = _mod_fwd(c_all, c_ctx2, w_mod[0], b_mod_loc, "mod_fwd")
    (mod_g,) = _exchange([mod_loc], False, "gather_mod")
    mod_full = cols(mod_g)
    mod_tab = jnp.concatenate([lax.dynamic_slice(mod_full, (me * B, 0), (B, N_MOD * D)), mod_full[N_DEV * B:N_DEV * B + 1]], axis=0)
    mods = [mod_tab[:, i * D:(i + 1) * D].reshape(B + 1, 1, D) for i in range(N_MOD)]
    mods_x = [mm[:B] for mm in mods]

    x_all = jnp.concatenate([x.reshape(T, D), ctx.reshape(Tc, D)], axis=0)

    def f_ffn_in(tok, ex, sh):
        return [_rms_mod(tok[0], sh[0], ex[0], ex[1])], [], []

    (u1,), (w1gu_g,) = _rowwise(f_ffn_in, name="ffn1_in", tok_in=[(x_all, D, 0, False)], ex_in=[mods[0], mods[1]], sh_in=[g_ffn1],
                                tok_out=[(D, bf16)], carry=([w1_gu[0].astype(bf16)], False), **rw_all)
    W1gu = cols(w1gu_g)
    (gu1, h1), (w1d_g, win_g) = _ffn_up(u1, W1gu, "ffn1_up", carry=([w1_down[0].astype(bf16), w_in[0].astype(bf16)], False))
    W1d = rows_(w1d_g)
    lr2 = 2 * LOWRANK
    segs = [(0, 2 * D, 0), (2 * D, 2 * D + QK, 6 * D), (2 * D + QK, 3 * D, 6 * D + QK), (3 * D, 4 * D, 2 * D), (4 * D, 5 * D, 3 * D),
            (5 * D, 5 * D + lr2, 7 * D), (5 * D + lr2, 6 * D + lr2, 4 * D), (6 * D + lr2, 7 * D + lr2, 5 * D)]
    wc = w_in.shape[2]
    win_parts = []
    for lo, hi, _ in sorted(segs, key=lambda t: t[2]):
        for d in range(N_DEV):
            a0, a1 = max(lo, d * wc), min(hi, (d + 1) * wc)
            if a0 < a1:
                win_parts.append(win_g[d][:, a0 - d * wc:a1 - d * wc])
    Win = jnp.concatenate(win_parts + [jnp.zeros((D, LR_PAD - lr2), bf16)], axis=1)

    def nn(a, w):
        return jnp.dot(a.astype(bf16), w, preferred_element_type=f32)

    def nt(a, w):
        return lax.dot_general(a.astype(bf16), w, (((1,), (1,)), ((), ())), preferred_element_type=f32)

    def mix_in(xv, fv, gate, sh, sc, g):
        x1 = xv + 0.5 * gate * fv
        return x1, _rms_mod(x1, g, sh, sc)

    def f_mix_in(tok, ex, sh):
        f1v = nn(tok[1], sh[1])
        return list(mix_in(tok[0], f1v, ex[0], ex[1], ex[2], sh[0])) + [f1v], [], []

    x1, um, f1 = _rowwise(f_mix_in, name="ffn1_down_mix_in", tok_in=[(x_all, D, 0, False), (h1, F, 0, False)],
                          ex_in=[mods[2], mods[3], mods[4]], sh_in=[g_mix, W1d], tok_out=[(D, f32), (D, bf16), (D, bf16)], **rw_all2)
    p_all, (wco_g, wgo_g, wo_g, w2gu_g) = _matmul(
        um, Win, "nn", bf16, "in_proj", tm_cap=512, tn_cap=2432,
        carry=([w_conv_out[0].astype(bf16), w_gla_out[0].astype(bf16), w_out[0].astype(bf16), w2_gu[0].astype(bf16)], False))
    Wco, Wgo, Wo, W2gu = rows_(wco_g), rows_(wgo_g), rows_(wo_g), cols(w2gu_g)

    def log_decay(lr, wa, ba):
        z = _bdot(lr, wa, (1, 0)) + ba
        return _log_sigmoid(z) / TAU

    def f_decay(tok, ex, sh):
        return [log_decay(tok[0], sh[0], sh[1])], [], []

    lr_blk = (p_all, LR_PAD, 7 * D // LR_PAD, False)
    (la_all,) = _rowwise(f_decay, name="log_decay", tok_in=[lr_blk], sh_in=[WA, BA], tok_out=[(2 * QK, f32)], **rw_all)

    zeros_s = jnp.zeros((B, HEADS, DV, DK), f32)
    gla_c = dict(row0=T, nb=B, seq=Lc, D=D)
    gla_x = dict(row0=0, nb=B, seq=L, D=D)
    _, hist_cf, s_f = _gla_fwd(p_all, la_all, zeros_s, rev=False, name="gla_ctx_f", **gla_c)
    _, hist_cb, s_b = _gla_fwd(p_all, la_all, zeros_s, rev=True, name="gla_ctx_b", **gla_c)
    (o_f, hist_f, _), (w2d_g,) = _gla_fwd(p_all, la_all, s_f, rev=False, name="gla_x_f", carry=([w2_down[0].astype(bf16)], False), **gla_x)
    W2d = rows_(w2d_g)
    o_b, hist_b, _ = _gla_fwd(p_all, la_all, s_b, rev=True, name="gla_x_b", **gla_x)

    cz = _conv_fwd(p_all, dww, dw_bias, B=B, L=L, D=D, name="conv_fwd")

    def ln_silu(z, g, b):
        mu = jnp.mean(z, axis=-1, keepdims=True)
        var = jnp.mean(jnp.square(z - mu), axis=-1, keepdims=True)
        return jax.nn.silu((z - mu) * lax.rsqrt(var + EPS) * g + b)

    def f_ln(tok, ex, sh):
        zc = ln_silu(tok[0].astype(f32), sh[0], sh[1])
        return [zc, nn(zc, sh[2])], [], []

    zc, yc = _rowwise(f_ln, name="conv_ln_out", tok_in=[(cz, D, 0, False)], sh_in=[conv_ln_g, conv_ln_b, Wco],
                      tok_out=[(D, bf16), (D, bf16)], **rw_x)

    def gla_out(of, ob, og, gn):
        return _head_rms(of.astype(f32) + ob.astype(f32), DV) * gn * jax.nn.silu(og.astype(f32))

    def f_gla_out(tok, ex, sh):
        og2 = gla_out(tok[0], tok[1], tok[2], sh[0])
        return [og2, nn(og2, sh[1])], [], []

    og_blk = (p_all, D, 3, False)
    og2, yg = _rowwise(f_gla_out, name="gla_norm_out", tok_in=[(o_f, D, 0, False), (o_b, D, 0, False), og_blk], sh_in=[gla_norm_g, Wgo],
                       tok_out=[(D, bf16), (D, bf16)], **rw_x)

    def merge(ga, gb, ycv, ygv):
        return jax.nn.sigmoid(ga.astype(f32)) * ycv.astype(f32) + jax.nn.sigmoid(gb.astype(f32)) * ygv.astype(f32)

    def f_merge(tok, ex, sh):
        mg = merge(*tok)
        return [mg, nn(mg, sh[0])], [], []

    ga_blk, gb_blk = (p_all, D, 4, False), (p_all, D, 5, False)
    mg, mix = _rowwise(f_merge, name="merge_mix_out", tok_in=[ga_blk, gb_blk, (yc, D, 0, False), (yg, D, 0, False)], sh_in=[Wo],
                       tok_out=[(D, bf16), (D, f32)], **rw_x)

    def ffn2_in(x1v, mixv, g5, sh, sc, g):
        x2 = x1v + g5 * mixv
        return x2, _rms_mod(x2, g, sh, sc)

    def f_ffn2_in(tok, ex, sh):
        return list(ffn2_in(tok[0], tok[1], ex[0], ex[1], ex[2], sh[0])), [], []

    x2, u2 = _rowwise(f_ffn2_in, name="ffn2_in", tok_in=[(x1, D, 0, False), (mix, D, 0, False)], ex_in=[mods_x[5], mods_x[6], mods_x[7]],
                      sh_in=[g_ffn2], tok_out=[(D, f32), (D, bf16)], **rw_x)
    gu2, h2 = _ffn_up(u2, W2gu, "ffn2_up")

    gf2 = g_final.reshape(1, D)

    def head_loss(x2v, f2v, g8, gf, tgt):
        x3 = x2v + 0.5 * g8 * f2v
        y = x3 * lax.rsqrt(jnp.mean(x3 * x3, axis=-1, keepdims=True) + EPS) * gf
        return 0.5 * jnp.sum(jnp.mean(jnp.square(y - tgt), axis=-1))

    def f_head(tok, ex, sh):
        loss, vjp = jax.vjp(lambda a, b_, c_, d_: head_loss(a, b_, c_, d_, tok[2]), tok[0], nn(tok[1], sh[1]), ex[0], sh[0])
        dx3, df2, dg8, dgf = vjp(jnp.ones((), f32))
        return [dx3, df2], [dg8], [dgf, jnp.broadcast_to(loss.reshape(1, 1), (1, 128))]

    dx3, df2, dg8, dgf, loss_p = _rowwise(
        f_head, name="ffn2_down_head", tok_in=[(x2, D, 0, False), (h2, F, 0, False), (loss_target.reshape(T, D), D, 0, False)],
        ex_in=[mods_x[8]], sh_in=[gf2, W2d], tok_out=[(D, f32), (D, bf16)], ex_out=[D], gl_out=[(1, D), (1, 128)], **rw_x2)

    dgu2 = _ffn_down_dx(df2, W2d, gu2, "ffn2_down_dx")
    gW2d = _matmul(h2, df2, "tn", f32, "ffn2_down_dw", tm_cap=1408)
    du2 = _matmul(dgu2, W2gu, "nt", bf16, "ffn2_up_dx", halves="a")
    gW2gu = _matmul(u2, dgu2, "tn", f32, "ffn2_up_dw", halves="b")

    def f_ffn2_in_bwd(tok, ex, sh):
        _, vjp = jax.vjp(ffn2_in, tok[0], tok[1], ex[0], ex[1], ex[2], sh[0])
        dx2, dmix, dg5, dsh, dsc, dg = vjp((tok[3], tok[2].astype(f32)))
        return [dx2, dmix], [dg5, dsh, dsc], [dg]

    dx2, dmix, dg5, dsh6, dsc7, dg_ffn2 = _rowwise(
        f_ffn2_in_bwd, name="ffn2_in_bwd", tok_in=[(x1, D, 0, False), (mix, D, 0, False), (du2, D, 0, False), (dx3, D, 0, False)],
        ex_in=[mods_x[5], mods_x[6], mods_x[7]], sh_in=[g_ffn2], tok_out=[(D, f32), (D, bf16)], ex_out=[D, D, D], gl_out=[(1, D)], **rw_x)

    gWo = _matmul(mg, dmix, "tn", f32, "mix_out_dw")

    def f_merge_bwd(tok, ex, sh):
        _, vjp = jax.vjp(merge, *[t.astype(f32) for t in tok[:4]])
        dga, dgb, dyc, dyg = vjp(nt(tok[4], sh[0]))
        return [dga, dgb, dyc, dyg], [], []

    dga, dgb, dyc, dyg = _rowwise(f_merge_bwd, name="mix_out_merge_bwd",
                                  tok_in=[ga_blk, gb_blk, (yc, D, 0, False), (yg, D, 0, False), (dmix, D, 0, False)], sh_in=[Wo],
                                  tok_out=[(D, bf16)] * 4, **rw_x)
    gWco = _matmul(zc, dyc, "tn", f32, "conv_out_dw")
    gWgo = _matmul(og2, dyg, "tn", f32, "gla_out_dw")

    def f_ln_bwd(tok, ex, sh):
        _, vjp = jax.vjp(ln_silu, tok[0].astype(f32), sh[0], sh[1])
        dcz, dg, db = vjp(nt(tok[1], sh[2]))
        return [dcz], [], [dg, db, jnp.sum(dcz, axis=0, keepdims=True)]

    dcz, g_ln_g, g_ln_b, g_dwb = _rowwise(f_ln_bwd, name="conv_out_ln_bwd", tok_in=[(cz, D, 0, False), (dyc, D, 0, False)],
                                          sh_in=[conv_ln_g, conv_ln_b, Wco], tok_out=[(D, bf16)], gl_out=[(1, D)] * 3, **rw_x)
    def col_shards(g):
        return jnp.transpose(g.reshape(g.shape[0], N_DEV, g.shape[1] // N_DEV), (1, 0, 2)).astype(bf16)

    def row_shards(g):
        return g.reshape(N_DEV, g.shape[0] // N_DEV, g.shape[1]).astype(bf16)

    (dca, dcb, g_dww), (r_w2d, r_w2gu, r_wo, r_wco, r_wgo) = _conv_bwd(
        p_all, dcz, dww, B=B, L=L, D=D, name="conv_bwd",
        carry=([row_shards(gW2d), col_shards(gW2gu), row_shards(gWo), row_shards(gWco), row_shards(gWgo)], True))

    def f_gla_out_bwd(tok, ex, sh):
        _, vjp = jax.vjp(gla_out, tok[0].astype(f32), tok[1].astype(f32), tok[2].astype(f32), sh[0])
        dof, _, dog, dgn = vjp(nt(tok[3], sh[1]))
        return [dof, dog], [], [dgn]

    d_o, dog, g_gn = _rowwise(f_gla_out_bwd, name="gla_out_norm_bwd",
                              tok_in=[(o_f, D, 0, False), (o_b, D, 0, False), og_blk, (dyg, D, 0, False)], sh_in=[gla_norm_g, Wgo],
                              tok_out=[(D, bf16), (D, bf16)], gl_out=[(1, D)], **rw_x)

    dq_f, dk_f, dv_f, dla_f, ds_f = _gla_bwd(p_all, la_all, hist_f, d_o, zeros_s, rev=False, name="gla_x_f_bwd", **gla_x)
    dq, dk, dv, dla_b, ds_b = _gla_bwd(p_all, la_all, hist_b, d_o, zeros_s, rev=True, name="gla_x_b_bwd", add=(dq_f, dk_f, dv_f), **gla_x)
    dq_cf, dk_cf, dv_cf, dla_cf, _ = _gla_bwd(p_all, la_all, hist_cf, None, ds_f, rev=False, name="gla_ctx_f_bwd", **gla_c)
    _, dk_c, dv_c, dla_cb, _ = _gla_bwd(p_all, la_all, hist_cb, None, ds_b, rev=True, name="gla_ctx_b_bwd", add=(dq_cf, dk_cf, dv_cf), **gla_c)

    dla_all = jnp.concatenate([jnp.concatenate([dla_f, dla_b], axis=1), jnp.concatenate([dla_cf, dla_cb], axis=1)], axis=0)

    def f_decay_bwd(tok, ex, sh):
        _, vjp = jax.vjp(log_decay, tok[0].astype(f32), sh[0].astype(f32), sh[1])
        dlr, dwa, dba = vjp(tok[1])
        return [dlr], [], [dwa, dba]

    dlr, g_WA, g_BA = _rowwise(f_decay_bwd, name="log_decay_bwd", tok_in=[lr_blk, (dla_all, 2 * QK, 0, False)], sh_in=[WA, BA],
                               tok_out=[(LR_PAD, bf16)], gl_out=[(LR_PAD, 2 * QK), (1, 2 * QK)], **rw_all)

    zc_ = functools.partial(jnp.zeros, dtype=bf16)
    dp_x = jnp.concatenate([dca, dcb, dv, dog, dga, dgb, dq, dk, dlr[:T]], axis=1)
    dp_c = jnp.concatenate([zc_((Tc, 2 * D)), dv_c, zc_((Tc, 3 * D)), zc_((Tc, QK)), dk_c, dlr[T:]], axis=1)
    dp_all = jnp.concatenate([dp_x, dp_c], axis=0)
    gWin_p = _matmul(um, dp_all, "tn", f32, "in_proj_dw", tm_cap=512, tn_cap=2432)
    gwin_shards = []
    for d in range(N_DEV):
        parts = []
        for lo, hi, po in segs:
            a0, a1 = max(lo, d * wc), min(hi, (d + 1) * wc)
            if a0 < a1:
                parts.append(gWin_p[:, po + a0 - lo:po + a1 - lo])
        gwin_shards.append(jnp.concatenate(parts, axis=1))
    dum, (r_win,) = _matmul(dp_all, Win, "nt", bf16, "in_proj_dx", tk_cap=2432, carry=([jnp.stack(gwin_shards).astype(bf16)], True))

    def f_mix_in_bwd(tok, ex, sh):
        _, vjp = jax.vjp(mix_in, tok[0], tok[1].astype(f32), ex[0], ex[1], ex[2], sh[0])
        dx1, df1, dgate, dsh, dsc, dg = vjp((tok[3], tok[2].astype(f32)))
        return [dx1, df1], [dgate, dsh, dsc], [dg]

    dx1, df1, dg2, dsh3, dsc4, dg_mix = _rowwise(
        f_mix_in_bwd, name="mix_in_bwd", tok_in=[(x_all, D, 0, False), (f1, D, 0, False), (dum, D, 0, False), (dx2, D, 0, True)],
        ex_in=[mods[2], mods[3], mods[4]], sh_in=[g_mix], tok_out=[(D, f32), (D, bf16)], ex_out=[D, D, D], gl_out=[(1, D)], **rw_all)

    dgu1 = _ffn_down_dx(df1, W1d, gu1, "ffn1_down_dx")
    gW1d = _matmul(h1, df1, "tn", f32, "ffn1_down_dw", tm_cap=1408)
    gW1gu, (r_w1d,) = _matmul(u1, dgu1, "tn", f32, "ffn1_up_dw", carry=([row_shards(gW1d)], True), halves="b")
    du1, (r_w1gu,) = _matmul(dgu1, W1gu, "nt", bf16, "ffn1_up_dx", carry=([col_shards(gW1gu)], True), halves="a")

    def f_ffn_in_bwd(tok, ex, sh):
        _, vjp = jax.vjp(_rms_mod, tok[0], sh[0], ex[0], ex[1])
        dx, dg, dsh, dsc = vjp(tok[1].astype(f32))
        return [dx + tok[2]], [dsh, dsc], [dg]

    dx_all, dsh0, dsc1, dg_ffn1 = _rowwise(
        f_ffn_in_bwd, name="ffn1_in_bwd", tok_in=[(x_all, D, 0, False), (du1, D, 0, False), (dx1, D, 0, False)],
        ex_in=[mods[0], mods[1]], sh_in=[g_ffn1], tok_out=[(D, f32)], ex_out=[D, D], gl_out=[(1, D)], **rw_all)
    grad_x = dx_all[:T].reshape(B, L, D)

    zrow = jnp.zeros((1, 1, D), f32)
    dmod_loc = jnp.concatenate([dsh0, dsc1, dg2, dsh3, dsc4] + [jnp.concatenate([t, zrow], axis=0) for t in (dg5, dsh6, dsc7, dg8)],
                               axis=2).reshape(B + 1, N_MOD * D)
    small = [loss_p, dg_ffn1, dg_mix, g_dww[:CONV_W].reshape(1, CONV_W * D), g_dwb, g_ln_g, g_ln_b,
             g_WA[:LOWRANK, :QK].reshape(1, LOWRANK * QK), g_BA[:, :QK], g_WA[LOWRANK:2 * LOWRANK, QK:].reshape(1, LOWRANK * QK), g_BA[:, QK:],
             g_gn, dg_ffn2, dgf]
    small_w = [s.shape[1] for s in small]
    def to8(v):
        n_pad = -(-v.shape[1] // 1024) * 1024
        return jnp.pad(v, ((0, 0), (0, n_pad - v.shape[1]))).reshape(8, n_pad // 8)

    def from8(a, n):
        return a.reshape(1, a.size)[:, :n]

    dmod_g, small_g = _exchange([dmod_loc, to8(jnp.concatenate(small, axis=1))], False, "gather_small")
    dmx = dmod_g[:, :B].reshape(N_DEV * B, N_MOD * D)
    dmc = dmod_g[:, B]
    gWmod, gcc_p = _mod_bwd(c_all, c_ctx2, w_mod[0], lax.dynamic_slice(dmx, (0, me * ncm), (N_DEV * B, ncm)),
                            lax.dynamic_slice(dmc, (0, me * ncm), (N_DEV, ncm)), "mod_bwd")

    rs_out = [r_w1gu, r_w1d, r_win, r_wco, r_wgo, r_wo, r_w2gu, r_w2d]
    (gcc_g,) = _exchange([to8(gcc_p)], False, "gather_cctx")

    sums, g_cc, g_bmod = _sum_sources([small_g, gcc_g, jnp.concatenate([dmx, dmc], axis=0).reshape(N_DEV * (B + 1), 8, N_MOD * D // 8)], "sum_small")
    sums, g_cc, g_bmod = from8(sums, sum(small_w)), from8(g_cc, D), from8(g_bmod, N_MOD * D)
    offs = [0]
    for wd in small_w:
        offs.append(offs[-1] + wd)
    sm = [sums[:, offs[i]:offs[i + 1]] for i in range(len(small))]
    loss = sm[0][0, 0]
    ncd, nca = dw_weight.shape[2], w_alpha_f.shape[2]
    g_dww_loc = lax.dynamic_slice(sm[3].reshape(CONV_W, D), (0, me * ncd), (CONV_W, ncd)).reshape(1, CONV_W * ncd)
    g_waf_loc = lax.dynamic_slice(sm[7].reshape(LOWRANK, QK), (0, me * nca), (LOWRANK, nca)).reshape(1, LOWRANK * nca)
    g_wab_loc = lax.dynamic_slice(sm[9].reshape(LOWRANK, QK), (0, me * nca), (LOWRANK, nca)).reshape(1, LOWRANK * nca)

    big = {}
    for nm, wv, mv, vv, part in (("w1_gu", w1_gu, m_w1_gu, v_w1_gu, rs_out[0]), ("w1_down", w1_down, m_w1_down, v_w1_down, rs_out[1]),
                                 ("w_in", w_in, m_w_in, v_w_in, rs_out[2]), ("w_conv_out", w_conv_out, m_w_conv_out, v_w_conv_out, rs_out[3]),
                                 ("w_gla_out", w_gla_out, m_w_gla_out, v_w_gla_out, rs_out[4]), ("w_out", w_out, m_w_out, v_w_out, rs_out[5]),
                                 ("w2_gu", w2_gu, m_w2_gu, v_w2_gu, rs_out[6]), ("w2_down", w2_down, m_w2_down, v_w2_down, rs_out[7])):
        big[nm] = [t[None] for t in _adamw(wv[0], mv[0], vv[0], part, "adamw_" + nm, True)]
    big["w_mod"] = [t[None] for t in _adamw(w_mod[0], m_w_mod[0], v_w_mod[0], gWmod, "adamw_w_mod", False)]

    small_params = [("c_ctx", c_ctx, m_c_ctx, v_c_ctx, g_cc), ("b_mod", b_mod, m_b_mod, v_b_mod, g_bmod), ("g_ffn1", g_ffn1, m_g_ffn1, v_g_ffn1, sm[1]),
                    ("g_mix", g_mix, m_g_mix, v_g_mix, sm[2]), ("dw_weight", dw_weight, m_dw_weight, v_dw_weight, g_dww_loc),
                    ("dw_bias", dw_bias, m_dw_bias, v_dw_bias, sm[4]), ("conv_ln_g", conv_ln_g, m_conv_ln_g, v_conv_ln_g, sm[5]),
                    ("conv_ln_b", conv_ln_b, m_conv_ln_b, v_conv_ln_b, sm[6]), ("w_alpha_f", w_alpha_f, m_w_alpha_f, v_w_alpha_f, g_waf_loc),
                    ("b_alpha_f", b_alpha_f, m_b_alpha_f, v_b_alpha_f, sm[8]), ("w_alpha_b", w_alpha_b, m_w_alpha_b, v_w_alpha_b, g_wab_loc),
                    ("b_alpha_b", b_alpha_b, m_b_alpha_b, v_b_alpha_b, sm[10]), ("gla_norm_g", gla_norm_g, m_gla_norm_g, v_gla_norm_g, sm[11]),
                    ("g_ffn2", g_ffn2, m_g_ffn2, v_g_ffn2, sm[12]), ("g_final", g_final, m_g_final, v_g_final, sm[13])]
    flat = lambda t: t.reshape(1, t.size)
    pw, pm, pv, pg = (jnp.concatenate([flat(sp[i]) for sp in small_params], axis=1) for i in (1, 2, 3, 4))
    n_small = pw.shape[1]
    s_g, s_d, s_m, s_v = (from8(t, n_small) for t in _adamw(to8(pw), to8(pm), to8(pv), to8(pg), "adamw_small", False))
    small_out, o0 = {}, 0
    for nm, wv, _, _, _ in small_params:
        small_out[nm] = [t[:, o0:o0 + wv.size].reshape(wv.shape) for t in (s_g, s_d, s_m, s_v)]
        o0 += wv.size

    order = ["c_ctx", "w_mod", "b_mod", "g_ffn1", "w1_gu", "w1_down", "g_mix", "w_in", "dw_weight", "dw_bias", "conv_ln_g", "conv_ln_b",
             "w_conv_out", "w_alpha_f", "b_alpha_f", "w_alpha_b", "b_alpha_b", "gla_norm_g", "w_gla_out", "w_out", "g_ffn2", "w2_gu",
             "w2_down", "g_final"]
    res = {**big, **small_out}
    return (loss, grad_x, *[res[n][0] for n in order], *[res[n][1] for n in order], *[res[n][2] for n in order], *[res[n][3] for n in order])
```

```python
import functools

import jax
import jax.numpy as jnp
from jax import lax
from jax.experimental import pallas as pl
from jax.experimental.pallas import tpu as pltpu

f32, bf16 = jnp.float32, jnp.bfloat16

N_DEV = 8
HEADS = 4
LOWRANK = 16
CONV_W = 31
CONV_PAD = 16
SUBLANES = 8
CHUNK = 64
SUB = 16
GLA_ROWS = 256
GLA_SAFE_DECAY = 60.0
TAU = 16.0
EPS = 1e-6
N_MOD = 9
LR_PAD = 128
ROW_TILE = 256
V7X_VMEM_BYTES = 64 << 20
VMEM_LIMIT = (V7X_VMEM_BYTES * 3) // 4

ADAM_LR, ADAM_B1, ADAM_B2, ADAM_EPS, ADAM_WD, ADAM_STEP = 0.001, 0.9, 0.999, 1e-08, 0.01, 10

MESH = pl.DeviceIdType.MESH


def _pc(body, **kw):
    return pl.pallas_call(body, **kw)


def _params(*sem):
    return pltpu.CompilerParams(dimension_semantics=sem, vmem_limit_bytes=VMEM_LIMIT)


def _pick(n, cap, unit=128):
    best = None
    for t in range(unit, min(n, cap) + 1, unit):
        if n % t == 0:
            best = t
    return best or n


def _matmul(a, b, mode, out_dtype, name, tm_cap=1024, tn_cap=1536, tk_cap=None, carry=None, halves=None):
    tk_cap = tk_cap or (2048 if mode == "tn" else 2816)
    if halves == "a":
        (_, M, Kh), N = a.shape, b.shape[0]
        K, tk = 2 * Kh, _pick(Kh, tk_cap)
        tm, tn = _pick(M, tm_cap), _pick(N, tn_cap)
        a_spec = pl.BlockSpec((None, tm, tk), lambda i, j, k: (k // (Kh // tk), i, k % (Kh // tk)))
    elif halves == "b":
        (K, M), (_, _, Nh) = a.shape, b.shape
        N, tn = 2 * Nh, _pick(Nh, tn_cap)
        tm, tk = _pick(M, tm_cap), _pick(K, tk_cap)
    else:
        if mode == "tn":
            (K, M), N = a.shape, b.shape[1]
        elif mode == "nt":
            (M, K), N = a.shape, b.shape[0]
        else:
            (M, K), N = a.shape, b.shape[1]
        tm, tn, tk = _pick(M, tm_cap), _pick(N, tn_cap), _pick(K, tk_cap)
    nk = K // tk
    if halves != "a":
        a_spec = pl.BlockSpec((tk, tm), lambda i, j, k: (k, i)) if mode == "tn" else pl.BlockSpec((tm, tk), lambda i, j, k: (i, k))
    if halves == "b":
        b_spec = pl.BlockSpec((None, tk, tn), lambda i, j, k: (j // (Nh // tn), k, j % (Nh // tn)))
    else:
        b_spec = pl.BlockSpec((tn, tk), lambda i, j, k: (j, k)) if mode == "nt" else pl.BlockSpec((tk, tn), lambda i, j, k: (k, j))
    dims = {"nn": ((1,), (0,)), "nt": ((1,), (1,)), "tn": ((0,), (0,))}[mode]

    def body_single(a_ref, b_ref, o_ref):
        o_ref[...] = lax.dot_general(a_ref[...].astype(bf16), b_ref[...].astype(bf16), (dims, ((), ())),
                                     preferred_element_type=f32).astype(out_dtype)

    def body(a_ref, b_ref, o_ref, acc_ref):
        k = pl.program_id(2)
        part = lax.dot_general(a_ref[...].astype(bf16), b_ref[...].astype(bf16), (dims, ((), ())), preferred_element_type=f32)

        @pl.when(k == 0)
        def _():
            acc_ref[...] = part

        @pl.when(k > 0)
        def _():
            acc_ref[...] += part

        @pl.when(k == nk - 1)
        def _():
            o_ref[...] = acc_ref[...].astype(out_dtype)

    (out,), carried = _call(
        body_single if nk == 1 else body, name=name, grid=(M // tm, N // tn, nk), in_specs=[a_spec, b_spec],
        out_specs=[pl.BlockSpec((tm, tn), lambda i, j, k: (i, j))], out_shape=[jax.ShapeDtypeStruct((M, N), out_dtype)],
        scratch_shapes=[] if nk == 1 else [pltpu.VMEM((tm, tn), f32)], sem=("parallel", "parallel", "arbitrary"),
        args=(a, b), carry=carry)
    return out if carry is None else (out, carried)


def _ffn_up(u, Wgu, name, carry=None):
    M, K = u.shape
    F = Wgu.shape[1] // 2
    tm, tn = _pick(M, 512), _pick(F, 1408)
    nj = F // tn

    def body(u_ref, wa_ref, wb_ref, gu_ref, h_ref):
        uv = u_ref[...]
        a = jnp.dot(uv, wa_ref[...], preferred_element_type=f32)
        b = jnp.dot(uv, wb_ref[...], preferred_element_type=f32)
        gu_ref[0] = a.astype(bf16)
        gu_ref[1] = b.astype(bf16)
        h_ref[...] = (jax.nn.silu(a) * b).astype(bf16)

    res, carried = _call(
        body, name=name, grid=(nj, M // tm),
        in_specs=[pl.BlockSpec((tm, K), lambda j, i: (i, 0)), pl.BlockSpec((K, tn), lambda j, i: (0, j)),
                  pl.BlockSpec((K, tn), lambda j, i: (0, nj + j))],
        out_specs=[pl.BlockSpec((2, tm, tn), lambda j, i: (0, i, j)), pl.BlockSpec((tm, tn), lambda j, i: (i, j))],
        out_shape=[jax.ShapeDtypeStruct((2, M, F), bf16), jax.ShapeDtypeStruct((M, F), bf16)],
        scratch_shapes=[], sem=("parallel", "parallel"), args=(u, Wgu, Wgu), carry=carry)
    return res if carry is None else (res, carried)


def _ffn_down_dx(df, Wd, gu, name):
    M, D = df.shape
    F = Wd.shape[0]
    tm, tn = _pick(M, 512), _pick(F, 1408)

    def body(df_ref, w_ref, gu_ref, o_ref):
        dh = lax.dot_general(df_ref[...], w_ref[...], (((1,), (1,)), ((), ())), preferred_element_type=f32)
        a, b = gu_ref[0].astype(f32), gu_ref[1].astype(f32)
        sg = jax.nn.sigmoid(a)
        o_ref[0] = (dh * b * sg * (1.0 + a * (1.0 - sg))).astype(bf16)
        o_ref[1] = (dh * a * sg).astype(bf16)

    return _pc(
        body, name=name, grid=(F // tn, M // tm),
        in_specs=[pl.BlockSpec((tm, D), lambda j, i: (i, 0)), pl.BlockSpec((tn, D), lambda j, i: (j, 0)),
                  pl.BlockSpec((2, tm, tn), lambda j, i: (0, i, j))],
        out_specs=pl.BlockSpec((2, tm, tn), lambda j, i: (0, i, j)), out_shape=jax.ShapeDtypeStruct((2, M, F), bf16),
        compiler_params=_params("parallel", "parallel"))(df, Wd, gu)


def _rowwise(fn, *, name, tm, n_tiles, tpe, nx_tiles, n_ex, tok_in=(), ex_in=(), sh_in=(), tok_out=(), ex_out=(), gl_out=(), carry=None):
    def seg(i):
        return jnp.minimum(i // tpe, n_ex - 1)

    in_specs, args = [], []
    for arr, w, cb, x_only in tok_in:
        if x_only:
            in_specs.append(pl.BlockSpec((tm, w), functools.partial(lambda i, cb: (jnp.minimum(i, nx_tiles - 1), cb), cb=cb)))
        else:
            in_specs.append(pl.BlockSpec((tm, w), functools.partial(lambda i, cb: (i, cb), cb=cb)))
        args.append(arr)
    for arr in ex_in:
        in_specs.append(pl.BlockSpec((1, 1, arr.shape[-1]), lambda i: (seg(i), 0, 0)))
        args.append(arr)
    for arr in sh_in:
        in_specs.append(pl.BlockSpec(arr.shape, functools.partial(lambda i, nd: (0,) * nd, nd=arr.ndim)))
        args.append(arr)
    out_specs, out_shape = [], []
    for w, dt in tok_out:
        out_specs.append(pl.BlockSpec((tm, w), lambda i: (i, 0)))
        out_shape.append(jax.ShapeDtypeStruct((n_tiles * tm, w), dt))
    for w in ex_out:
        out_specs.append(pl.BlockSpec((1, 1, w), lambda i: (seg(i), 0, 0)))
        out_shape.append(jax.ShapeDtypeStruct((n_ex, 1, w), f32))
    for r, w in gl_out:
        out_specs.append(pl.BlockSpec((r, w), lambda i: (0, 0)))
        out_shape.append(jax.ShapeDtypeStruct((r, w), f32))
    n_tok, n_exi, n_sh = len(tok_in), len(ex_in), len(sh_in)
    n_to, n_eo = len(tok_out), len(ex_out)
    x_only_flags = [t[3] for t in tok_in]

    def body(*refs):
        i = pl.program_id(0)
        ins, outs = refs[: n_tok + n_exi + n_sh], refs[n_tok + n_exi + n_sh:]
        is_x = i < nx_tiles
        tok_vals = []
        for r, xo in zip(ins[:n_tok], x_only_flags):
            v = r[...]
            tok_vals.append(jnp.where(is_x, v, jnp.zeros_like(v)) if xo else v)
        ex_vals = [r[0] for r in ins[n_tok:n_tok + n_exi]]
        sh_vals = [r[...] for r in ins[n_tok + n_exi:]]
        t_o, e_o, g_o = fn(tok_vals, ex_vals, sh_vals)
        for r, v in zip(outs[:n_to], t_o):
            r[...] = v.astype(r.dtype)
        first = jnp.logical_and(i % tpe == 0, i <= nx_tiles)
        for r, v in zip(outs[n_to:n_to + n_eo], e_o):
            @pl.when(first)
            def _(r=r, v=v):
                r[0] = v

            @pl.when(jnp.logical_not(first))
            def _(r=r, v=v):
                r[0] += v
        for r, v in zip(outs[n_to + n_eo:], g_o):
            @pl.when(i == 0)
            def _(r=r, v=v):
                r[...] = v

            @pl.when(i > 0)
            def _(r=r, v=v):
                r[...] += v

    res, carried = _call(body, name=name, grid=(n_tiles,), in_specs=in_specs, out_specs=out_specs, out_shape=out_shape,
                         scratch_shapes=[], sem=("arbitrary",), args=args, carry=carry)
    return res if carry is None else (res, carried)


def _rms_mod(x, g, sh, sc):
    y = x * lax.rsqrt(jnp.mean(x * x, axis=-1, keepdims=True) + EPS) * g
    return y * (1.0 + sc) + sh


def _log_sigmoid(z):
    return jnp.minimum(z, 0.0) - jnp.log(1.0 + jnp.exp(-jnp.abs(z)))


def _head_rms(o, DV):
    parts = []
    for h in range(HEADS):
        oh = o[:, h * DV:(h + 1) * DV]
        parts.append(oh * lax.rsqrt(jnp.mean(oh * oh, axis=-1, keepdims=True) + EPS))
    return jnp.concatenate(parts, axis=1)


@functools.partial(jax.custom_vjp, nondiff_argnums=(2,))
def _bdot(a, b, dims):
    return lax.dot_general(a.astype(bf16), b.astype(bf16), (((dims[0],), (dims[1],)), ((), ())), preferred_element_type=f32)


def _bdot_fwd(a, b, dims):
    return _bdot(a, b, dims), (a, b)


def _bdot_bwd(dims, res, g):
    a, b = res
    ca, cb = dims
    da = _bdot(g, b, (1, 1 - cb)) if ca == 1 else _bdot(b, g, (1 - cb, 1))
    db = _bdot(a, g, (1 - ca, 0)) if cb == 0 else _bdot(g, a, (0, 1 - ca))
    return da, db


_bdot.defvjp(_bdot_fwd, _bdot_bwd)


def _split_dot(m, x, dims):
    mb, rem, acc = m.astype(bf16), x, None
    for _ in range(3):
        piece = rem.astype(bf16)
        rem = rem - piece.astype(f32)
        part = lax.dot_general(mb, piece, (((dims[0],), (dims[1],)), ((), ())), preferred_element_type=f32)
        acc = part if acc is None else acc + part
    return acc


@jax.custom_vjp
def _tri_cumsum(tri, g):
    return _split_dot(tri, g, (1, 0))


def _tri_cumsum_fwd(tri, g):
    return _tri_cumsum(tri, g), tri


def _tri_cumsum_bwd(tri, db):
    return jnp.zeros_like(tri), _split_dot(tri, db, (0, 0))


_tri_cumsum.defvjp(_tri_cumsum_fwd, _tri_cumsum_bwd)


def _gla_chunk(St, q, k, v, g, *, rev, scale, exact):
    C, DK = q.shape
    r = lax.broadcasted_iota(jnp.int32, (C, C), 0)
    c = lax.broadcasted_iota(jnp.int32, (C, C), 1)
    causal = (r <= c) if rev else (r >= c)
    b = _tri_cumsum(causal.astype(f32), g)
    qs = q * scale
    qe = qs * jnp.exp(b)
    inter = _bdot(qe, St, (1, 1))
    b_last = b[0:1] if rev else b[C - 1:C]
    kd = k * jnp.exp(b_last - b)
    St_new = St * jnp.exp(b_last) + _bdot(v, kd, (0, 0))
    if not exact:
        att = jnp.where(causal, _bdot(qe, k * jnp.exp(-b), (1, 1)), 0.0)
        return St_new, inter + _bdot(att, v, (1, 0))
    rr = lax.broadcasted_iota(jnp.int32, (SUB, SUB, DK), 0)
    cc = lax.broadcasted_iota(jnp.int32, (SUB, SUB, DK), 1)
    m3 = (rr <= cc) if rev else (rr >= cc)
    outs = []
    for i in range(C // SUB):
        lo, hi = i * SUB, (i + 1) * SUB
        bi, qi, ki, vi = b[lo:hi], qs[lo:hi], k[lo:hi], v[lo:hi]
        rel = bi[:, None, :] - bi[None, :, :]
        e = jnp.where(m3, jnp.exp(jnp.where(m3, rel, 0.0)), 0.0)
        att = jnp.sum(qi[:, None, :] * e * ki[None, :, :], axis=-1)
        acc = _bdot(att, vi, (1, 0))
        ref_row = b[hi - 1:hi] if rev else b[lo:lo + 1]
        prev = slice(hi, C) if rev else slice(0, lo)
        if (hi < C) if rev else (lo > 0):
            qn = qi * jnp.exp(bi - ref_row)
            ks = k[prev] * jnp.exp(ref_row - b[prev])
            acc = acc + _bdot(_bdot(qn, ks, (1, 1)), v[prev], (1, 0))
        outs.append(acc)
    return St_new, inter + jnp.concatenate(outs, axis=0)


def _mild_decay(la_ref):
    return jnp.min(la_ref[...]) >= -GLA_SAFE_DECAY / CHUNK


def _gla_specs(D, rev_blocks, row0, seq):
    DK, DV = D // (2 * HEADS), D // HEADS
    nblk = seq // GLA_ROWS
    rb0 = row0 // GLA_ROWS

    def blk(j):
        return (nblk - 1 - j) if rev_blocks else j

    return DK, DV, nblk, rb0, blk


def _gla_in_specs(D, rev, rows):
    QK = D // 2
    return [
        pl.BlockSpec((GLA_ROWS, QK), lambda b, j: (rows(b, j), 6 * D // QK)),
        pl.BlockSpec((GLA_ROWS, QK), lambda b, j: (rows(b, j), 6 * D // QK + 1)),
        pl.BlockSpec((GLA_ROWS, D), lambda b, j: (rows(b, j), 2)),
        pl.BlockSpec((GLA_ROWS, QK), lambda b, j: (rows(b, j), 1 if rev else 0)),
    ]


def _gla_fwd(p_all, la_all, s0, *, rev, row0, nb, seq, D, name, carry=None):
    DK, DV, nblk, rb0, blk = _gla_specs(D, rev, row0, seq)
    cpb = GLA_ROWS // CHUNK

    def rows(b, j):
        return rb0 + b * nblk + blk(j)

    in_specs = _gla_in_specs(D, rev, rows) + [pl.BlockSpec((1, HEADS, DV, DK), lambda b, j: (b, 0, 0, 0))]
    out_specs = [
        pl.BlockSpec((GLA_ROWS, D), lambda b, j: (b * nblk + blk(j), 0)),
        pl.BlockSpec((1, HEADS, cpb, DV, DK), lambda b, j: (b, 0, blk(j), 0, 0)),
        pl.BlockSpec((1, HEADS, DV, DK), lambda b, j: (b, 0, 0, 0)),
    ]
    out_shape = [
        jax.ShapeDtypeStruct((nb * seq, D), bf16),
        jax.ShapeDtypeStruct((nb, HEADS, seq // CHUNK, DV, DK), bf16),
        jax.ShapeDtypeStruct((nb, HEADS, DV, DK), f32),
    ]
    chunk = functools.partial(_gla_chunk, rev=rev, scale=DK ** -0.5)

    def body(q_ref, k_ref, v_ref, la_ref, s0_ref, o_ref, hist_ref, sfin_ref, st_ref):
        j = pl.program_id(1)

        @pl.when(j == 0)
        def _():
            st_ref[...] = s0_ref[0]

        def step(ci, exact):
            cc = (cpb - 1 - ci) if rev else ci
            sl = pl.ds(cc * CHUNK, CHUNK)
            for h in range(HEADS):
                kq, kv = pl.ds(h * DK, DK), pl.ds(h * DV, DV)
                St = st_ref[h]
                hist_ref[0, h, cc] = St.astype(bf16)
                St2, o = chunk(St, q_ref[sl, kq].astype(f32), k_ref[sl, kq].astype(f32), v_ref[sl, kv].astype(f32), la_ref[sl, kq],
                               exact=exact)
                o_ref[sl, kv] = o.astype(bf16)
                st_ref[h] = St2

        mild = _mild_decay(la_ref)
        for exact in (False, True):
            @pl.when(jnp.logical_not(mild) if exact else mild)
            def _(exact=exact):
                for ci in range(cpb):
                    step(ci, exact)

        @pl.when(j == nblk - 1)
        def _():
            sfin_ref[0] = st_ref[...]

    res, carried = _call(body, name=name, grid=(nb, nblk), in_specs=in_specs, out_specs=out_specs, out_shape=out_shape,
                         scratch_shapes=[pltpu.VMEM((HEADS, DV, DK), f32)], sem=("parallel", "arbitrary"),
                         args=(p_all, p_all, p_all, la_all, s0), carry=carry)
    return res if carry is None else (res, carried)


def _gla_bwd(p_all, la_all, hist, do, dsfin, *, rev, row0, nb, seq, D, name, add=None):
    DK, DV, nblk, rb0, blk = _gla_specs(D, not rev, row0, seq)
    cpb = GLA_ROWS // CHUNK
    QK = HEADS * DK
    has_do = do is not None

    def rows(b, j):
        return rb0 + b * nblk + blk(j)

    in_specs = _gla_in_specs(D, rev, rows) + [
        pl.BlockSpec((1, HEADS, cpb, DV, DK), lambda b, j: (b, 0, blk(j), 0, 0)),
        pl.BlockSpec((1, HEADS, DV, DK), lambda b, j: (b, 0, 0, 0)),
    ]
    args = [p_all, p_all, p_all, la_all, hist, dsfin]
    if has_do:
        in_specs.append(pl.BlockSpec((GLA_ROWS, D), lambda b, j: (b * nblk + blk(j), 0)))
        args.append(do)
    if add is not None:
        in_specs += [pl.BlockSpec((GLA_ROWS, t.shape[1]), lambda b, j: (b * nblk + blk(j), 0)) for t in add]
        args += list(add)
    gdt = f32 if add is None else bf16
    out_specs = [
        pl.BlockSpec((GLA_ROWS, QK), lambda b, j: (b * nblk + blk(j), 0)),
        pl.BlockSpec((GLA_ROWS, QK), lambda b, j: (b * nblk + blk(j), 0)),
        pl.BlockSpec((GLA_ROWS, D), lambda b, j: (b * nblk + blk(j), 0)),
        pl.BlockSpec((GLA_ROWS, QK), lambda b, j: (b * nblk + blk(j), 0)),
        pl.BlockSpec((1, HEADS, DV, DK), lambda b, j: (b, 0, 0, 0)),
    ]
    out_shape = [
        jax.ShapeDtypeStruct((nb * seq, QK), gdt), jax.ShapeDtypeStruct((nb * seq, QK), gdt),
        jax.ShapeDtypeStruct((nb * seq, D), gdt), jax.ShapeDtypeStruct((nb * seq, QK), f32),
        jax.ShapeDtypeStruct((nb, HEADS, DV, DK), f32),
    ]
    chunk = functools.partial(_gla_chunk, rev=rev, scale=DK ** -0.5)

    def body(*refs):
        refs = list(refs)
        q_ref, k_ref, v_ref, la_ref, hist_ref, dsfin_ref = refs[:6]
        do_ref = refs[6] if has_do else None
        add_refs = refs[6 + has_do:len(refs) - 6]
        dq_ref, dk_ref, dv_ref, dla_ref, ds0_ref, ds_ref = refs[len(refs) - 6:]
        j = pl.program_id(1)

        @pl.when(j == 0)
        def _():
            ds_ref[...] = dsfin_ref[0]

        def step(ci, exact):
            cc = ci if rev else (cpb - 1 - ci)
            sl = pl.ds(cc * CHUNK, CHUNK)
            for h in range(HEADS):
                kq, kv = pl.ds(h * DK, DK), pl.ds(h * DV, DV)
                prim = (hist_ref[0, h, cc].astype(f32), q_ref[sl, kq].astype(f32), k_ref[sl, kq].astype(f32), v_ref[sl, kv].astype(f32), la_ref[sl, kq])
                _, vjp = jax.vjp(functools.partial(chunk, exact=exact), *prim)
                d_o = do_ref[sl, kv].astype(f32) if has_do else jnp.zeros((CHUNK, DV), f32)
                dSt, dq, dk, dv, dg = vjp((ds_ref[h], d_o))
                if add is not None:
                    dq, dk, dv = dq + add_refs[0][sl, kq], dk + add_refs[1][sl, kq], dv + add_refs[2][sl, kv]
                dq_ref[sl, kq] = dq.astype(gdt)
                dk_ref[sl, kq] = dk.astype(gdt)
                dv_ref[sl, kv] = dv.astype(gdt)
                dla_ref[sl, kq] = dg
                ds_ref[h] = dSt

        mild = _mild_decay(la_ref)
        for exact in (False, True):
            @pl.when(jnp.logical_not(mild) if exact else mild)
            def _(exact=exact):
                for ci in range(cpb):
                    step(ci, exact)

        @pl.when(j == nblk - 1)
        def _():
            ds0_ref[0] = ds_ref[...]

    return _pc(body, name=name, grid=(nb, nblk), in_specs=in_specs, out_specs=out_specs, out_shape=out_shape,
               scratch_shapes=[pltpu.VMEM((HEADS, DV, DK), f32)], compiler_params=_params("parallel", "arbitrary"))(*args)


def _conv_fwd(p_all, dw_w, dw_b, *, B, L, D, name):
    ct = _pick(D, 256)
    nj = D // ct
    st = _pick(L, 128, 8)
    off = CONV_PAD - CONV_W // 2

    def body(a_ref, b_ref, w_ref, bias_ref, o_ref, zs_ref):
        _fill_shifted(zs_ref, L, lambda t0, n: a_ref[pl.ds(t0, n), :].astype(f32) * jax.nn.sigmoid(b_ref[pl.ds(t0, n), :].astype(f32)))
        for t0 in range(0, L, st):
            acc = jnp.zeros((st, ct), f32) + bias_ref[...]
            for k in range(CONV_W):
                acc = acc + w_ref[pl.ds(k, 1), :] * _window(zs_ref, t0 + k + off, st)
            o_ref[pl.ds(t0, st), :] = acc.astype(bf16)

    return _pc(
        body, name=name, grid=(B, nj),
        in_specs=[pl.BlockSpec((L, ct), lambda b, j: (b, j)), pl.BlockSpec((L, ct), lambda b, j: (b, nj + j)),
                  pl.BlockSpec((CONV_W, ct), lambda b, j: (0, j)), pl.BlockSpec((1, ct), lambda b, j: (0, j))],
        out_specs=pl.BlockSpec((L, ct), lambda b, j: (b, j)), out_shape=jax.ShapeDtypeStruct((B * L, D), bf16),
        scratch_shapes=[pltpu.VMEM((SUBLANES, L + 2 * CONV_PAD, ct), f32)], compiler_params=_params("parallel", "parallel"),
    )(p_all, p_all, dw_w, dw_b)


def _fill_shifted(zs_ref, L, rows):
    lp = L + 2 * CONV_PAD
    ct = zs_ref.shape[2]
    step = 256
    zs_ref[0, pl.ds(0, CONV_PAD), :] = jnp.zeros((CONV_PAD, ct), f32)
    zs_ref[0, pl.ds(CONV_PAD + L, CONV_PAD), :] = jnp.zeros((CONV_PAD, ct), f32)
    for t0 in range(0, L, step):
        n = min(step, L - t0)
        zs_ref[0, pl.ds(CONV_PAD + t0, n), :] = rows(t0, n)
    for r in range(1, SUBLANES):
        for i0 in range(0, lp - SUBLANES, step):
            n = min(step, lp - SUBLANES - i0)
            zs_ref[r, pl.ds(i0, n), :] = zs_ref[0, pl.ds(i0 + r, n), :]


def _window(zs_ref, start, n):
    r = start % SUBLANES
    return zs_ref[r, pl.ds(start - r, n), :]


def _conv_bwd(p_all, dcz, dw_w, *, B, L, D, name, carry=None):
    ct = _pick(D, 128)
    nj = D // ct
    st = _pick(L, 256, 8)
    half = CONV_W // 2

    def body(a_ref, b_ref, dcz_ref, w_ref, da_ref, db_ref, ddw_ref, zs_ref, ds_ref):
        bi = pl.program_id(1)
        _fill_shifted(zs_ref, L, lambda t0, n: a_ref[pl.ds(t0, n), :].astype(f32) * jax.nn.sigmoid(b_ref[pl.ds(t0, n), :].astype(f32)))
        _fill_shifted(ds_ref, L, lambda t0, n: dcz_ref[pl.ds(t0, n), :].astype(f32))

        @pl.when(bi == 0)
        def _():
            ddw_ref[...] = jnp.zeros_like(ddw_ref)

        for t0 in range(0, L, st):
            acc = jnp.zeros((st, ct), f32)
            for k in range(CONV_W):
                acc = acc + w_ref[pl.ds(k, 1), :] * _window(ds_ref, t0 + CONV_PAD + half - k, st)
            a_t = a_ref[pl.ds(t0, st), :].astype(f32)
            sg_t = jax.nn.sigmoid(b_ref[pl.ds(t0, st), :].astype(f32))
            da_ref[pl.ds(t0, st), :] = (acc * sg_t).astype(bf16)
            db_ref[pl.ds(t0, st), :] = (acc * a_t * sg_t * (1.0 - sg_t)).astype(bf16)

        parts = [jnp.zeros((SUBLANES, ct), f32) for _ in range(CONV_W)]
        sw = _pick(L, 64, SUBLANES)
        for t0 in range(0, L, sw):
            dout = dcz_ref[pl.ds(t0, sw), :].astype(f32)
            for k in range(CONV_W):
                prod = dout * _window(zs_ref, t0 + k + CONV_PAD - half, sw)
                for i in range(0, sw, SUBLANES):
                    parts[k] = parts[k] + prod[i:i + SUBLANES]
        for k in range(CONV_W):
            ddw_ref[pl.ds(k, 1), :] += jnp.sum(parts[k], axis=0, keepdims=True)

    res, carried = _call(
        body, name=name, grid=(nj, B),
        in_specs=[pl.BlockSpec((L, ct), lambda j, b: (b, j)), pl.BlockSpec((L, ct), lambda j, b: (b, nj + j)),
                  pl.BlockSpec((L, ct), lambda j, b: (b, j)), pl.BlockSpec((CONV_W, ct), lambda j, b: (0, j))],
        out_specs=[pl.BlockSpec((L, ct), lambda j, b: (b, j)), pl.BlockSpec((L, ct), lambda j, b: (b, j)),
                   pl.BlockSpec((2 * CONV_PAD, ct), lambda j, b: (0, j))],
        out_shape=[jax.ShapeDtypeStruct((B * L, D), bf16), jax.ShapeDtypeStruct((B * L, D), bf16),
                   jax.ShapeDtypeStruct((2 * CONV_PAD, D), f32)],
        scratch_shapes=[pltpu.VMEM((SUBLANES, L + 2 * CONV_PAD, ct), f32), pltpu.VMEM((SUBLANES, L + 2 * CONV_PAD, ct), f32)],
        sem=("parallel", "arbitrary"), args=(p_all, p_all, dcz, dw_w), carry=carry)
    return res if carry is None else (res, carried)


def _exchange(arrs, scatter, name):
    ex = _Exchange(arrs, scatter)
    n = ex.n

    def body(*refs):
        ex.start(refs[:n], refs[n:2 * n], refs[2 * n:])
        ex.finish(refs[:n], refs[n:2 * n], refs[2 * n:])

    res = _pc(body, name=name, in_specs=ex.specs, out_specs=ex.specs, out_shape=ex.out_shape, scratch_shapes=ex.scratch)(*arrs)
    return list(res)


class _Exchange:
    def __init__(self, arrs, scatter):
        self.arrs, self.scatter, self.n = list(arrs), scatter, len(arrs)
        self.out_shape = [jax.ShapeDtypeStruct(((N_DEV,) + a.shape[1:]) if scatter else ((N_DEV,) + a.shape), a.dtype) for a in arrs]
        self.specs = [pl.BlockSpec(memory_space=pl.ANY)] * self.n
        self.scratch = [pltpu.SemaphoreType.DMA((self.n, N_DEV - 1)), pltpu.SemaphoreType.DMA((self.n, N_DEV - 1)),
                        pltpu.SemaphoreType.DMA((self.n,))]

    def _copies(self, ins, outs, sems, landing):
        send_sems, recv_sems, local_sems = sems
        me = 4 * lax.axis_index("x") + 2 * lax.axis_index("y") + lax.axis_index("c")
        if landing:
            local = []
        else:
            local = [pltpu.make_async_copy(ins[a].at[me] if self.scatter else ins[a], outs[a].at[me], local_sems.at[a]) for a in range(self.n)]
        remote = []
        for k in range(1, N_DEV):
            p = (me + (N_DEV - k if landing else k)) % N_DEV
            for a in range(self.n):
                remote.append(pltpu.make_async_remote_copy(
                    src_ref=ins[a].at[p] if self.scatter else ins[a], dst_ref=outs[a].at[p if landing else me],
                    send_sem=send_sems.at[a, k - 1], recv_sem=recv_sems.at[a, k - 1],
                    device_id=(p // 4, (p // 2) % 2, p % 2), device_id_type=MESH))
        return local, remote

    def _gather_plan(self, ins, outs, sems):
        send_sems, recv_sems, local_sems = sems
        x, y, c = lax.axis_index("x"), lax.axis_index("y"), lax.axis_index("c")
        chips = [(1 - x, y), (x, 1 - y), (1 - x, 1 - y)]

        def blk(px, py, pc):
            return 4 * px + 2 * py + pc

        def copy(a, k, block, to, own):
            return pltpu.make_async_remote_copy(
                src_ref=ins[a] if own else outs[a].at[block], dst_ref=outs[a].at[block],
                send_sem=send_sems.at[a, k], recv_sem=recv_sems.at[a, k], device_id=to, device_id_type=MESH)

        me = blk(x, y, c)
        local = [pltpu.make_async_copy(ins[a], outs[a].at[me], local_sems.at[a]) for a in range(self.n)]
        return local, copy, me, (x, y, 1 - c), chips, blk, c

    def start(self, ins, outs, sems):
        if self.scatter:
            local, sends = self._copies(ins, outs, sems, False)
            for cp in local + sends:
                cp.start()
            return
        local, copy, me, sibling, chips, _, c = self._gather_plan(ins, outs, sems)
        for cp in local:
            cp.start()
        for a in range(self.n):
            copy(a, 0, me, sibling, True).start()
            for j, chip in enumerate(chips):
                copy(a, 1 + j, me, (*chip, c), True).start()

    def finish(self, ins, outs, sems):
        if self.scatter:
            for cp in self._copies(ins, outs, sems, True)[1]:
                cp.wait_recv()
            local, sends = self._copies(ins, outs, sems, False)
            for cp in sends:
                cp.wait_send()
            for cp in local:
                cp.wait()
            return
        local, copy, me, sibling, chips, blk, c = self._gather_plan(ins, outs, sems)
        for j, chip in enumerate(chips):
            for a in range(self.n):
                copy(a, 1 + j, blk(*chip, c), sibling, True).wait_recv()
                copy(a, 4 + j, blk(*chip, c), sibling, False).start()
        for a in range(self.n):
            copy(a, 0, blk(*sibling), sibling, True).wait_recv()
            for j, chip in enumerate(chips):
                copy(a, 4 + j, blk(*chip, 1 - c), sibling, False).wait_recv()
        for a in range(self.n):
            copy(a, 0, me, sibling, True).wait_send()
            for j, chip in enumerate(chips):
                copy(a, 1 + j, me, (*chip, c), True).wait_send()
                copy(a, 4 + j, blk(*chip, c), sibling, False).wait_send()
        for cp in local:
            cp.wait()


def _carried(inner, n_in, n_out, grid, ex):
    n = ex.n

    def body(*refs):
        own_in, c_in = refs[:n_in], refs[n_in:n_in + n]
        own_out, c_out = refs[n_in + n:n_in + n + n_out], refs[n_in + n + n_out:n_in + 2 * n + n_out]
        rest = refs[n_in + 2 * n + n_out:]
        own_scr, sems = rest[:len(rest) - 3], rest[len(rest) - 3:]
        pids = [pl.program_id(d) for d in range(len(grid))]
        first = functools.reduce(jnp.logical_and, [p == 0 for p in pids])
        last = functools.reduce(jnp.logical_and, [p == g - 1 for p, g in zip(pids, grid)])

        @pl.when(first)
        def _():
            ex.start(c_in, c_out, sems)

        inner(*own_in, *own_out, *own_scr)

        @pl.when(last)
        def _():
            ex.finish(c_in, c_out, sems)

    return body


def _call(inner, *, name, grid, in_specs, out_specs, out_shape, scratch_shapes, sem, args, carry=None):
    if carry is None:
        res = _pc(inner, name=name, grid=grid, in_specs=in_specs, out_specs=out_specs, out_shape=out_shape,
                  scratch_shapes=scratch_shapes, compiler_params=_params(*sem))(*args)
        return list(res), None
    ex = _Exchange(*carry)
    res = _pc(_carried(inner, len(in_specs), len(out_specs), grid, ex), name=name, grid=grid,
              in_specs=list(in_specs) + ex.specs, out_specs=list(out_specs) + ex.specs, out_shape=list(out_shape) + ex.out_shape,
              scratch_shapes=list(scratch_shapes) + ex.scratch, compiler_params=_params(*(["arbitrary"] * len(grid))))(*args, *ex.arrs)
    res = list(res)
    return res[:len(out_specs)], res[len(out_specs):]


def _mod_fwd(c_all, c_ctx, w_loc, b_loc, name):
    nr, D = c_all.shape
    nc = w_loc.shape[1]

    def body(c_ref, cc_ref, w_ref, b_ref, o_ref):
        a = jnp.concatenate([c_ref[...], jnp.broadcast_to(cc_ref[...], (8, D))], axis=0)
        s = jax.nn.silu(a).astype(bf16)
        o_ref[...] = jnp.dot(s, w_ref[...].astype(bf16), preferred_element_type=f32) + b_ref[...]

    return _pc(body, name=name, out_shape=jax.ShapeDtypeStruct((nr + 8, nc), f32), compiler_params=_params())(c_all, c_ctx, w_loc, b_loc)


def _mod_bwd(c_all, c_ctx, w_loc, dmx_loc, dmc_loc, name):
    nr, D = c_all.shape
    nc = w_loc.shape[1]

    def body(c_ref, cc_ref, w_ref, dmx_ref, dmc_ref, gw_ref, gc_ref):
        cc = cc_ref[...]
        a = jnp.concatenate([c_ref[...], jnp.broadcast_to(cc, (N_DEV, D))], axis=0)
        s = jax.nn.silu(a).astype(bf16)
        g = jnp.concatenate([dmx_ref[...], dmc_ref[...]], axis=0).astype(bf16)
        gw_ref[...] = lax.dot_general(s, g, (((0,), (0,)), ((), ())), preferred_element_type=f32)
        dmc = jnp.sum(dmc_ref[...], axis=0, keepdims=True)
        ds = lax.dot_general(jnp.broadcast_to(dmc, (8, nc)).astype(bf16), w_ref[...].astype(bf16), (((1,), (1,)), ((), ())),
                             preferred_element_type=f32)[0:1]
        sg = jax.nn.sigmoid(cc)
        gc_ref[...] = ds * (sg * (1.0 + cc * (1.0 - sg)))

    return _pc(body, name=name, out_shape=[jax.ShapeDtypeStruct((D, nc), f32), jax.ShapeDtypeStruct((1, D), f32)],
               compiler_params=_params())(c_all, c_ctx, w_loc, dmx_loc, dmc_loc)


def _adamw_math(w, g, m, v):
    m2 = ADAM_B1 * m + (1.0 - ADAM_B1) * g
    v2 = ADAM_B2 * v + (1.0 - ADAM_B2) * jnp.square(g)
    m_hat = m2 / (1.0 - ADAM_B1 ** ADAM_STEP)
    v_hat = v2 / (1.0 - ADAM_B2 ** ADAM_STEP)
    delta = -ADAM_LR * (m_hat / (jnp.sqrt(v_hat) + ADAM_EPS) + ADAM_WD * w)
    return delta, m2, v2


def _adamw_many(params, name):
    n = len(params)

    def body(*refs):
        ins, outs = refs[:4 * n], refs[4 * n:]
        for i in range(n):
            w, m, v, g = (ins[4 * i + k][...] for k in range(4))
            d, m2, v2 = _adamw_math(w, g, m, v)
            outs[3 * i][...] = d
            outs[3 * i + 1][...] = m2
            outs[3 * i + 2][...] = v2

    res = _pc(body, name=name, out_shape=[jax.ShapeDtypeStruct(p[0].shape, f32) for p in params for _ in range(3)],
              compiler_params=_params())(*[a for p in params for a in p])
    return [tuple(res[3 * i:3 * i + 3]) for i in range(n)]


def _adamw(w, m, v, g, name, partials, carry=None):
    r, cdim = w.shape
    tr = _pick(r, 256, 8)

    def body(w_ref, m_ref, v_ref, g_ref, og_ref, od_ref, om_ref, ov_ref):
        if partials:
            g = g_ref[0].astype(f32)
            for s in range(1, N_DEV):
                g = g + g_ref[s].astype(f32)
        else:
            g = g_ref[...]
        d, m2, v2 = _adamw_math(w_ref[...], g, m_ref[...], v_ref[...])
        og_ref[...] = g
        od_ref[...] = d
        om_ref[...] = m2
        ov_ref[...] = v2

    blk = pl.BlockSpec((tr, cdim), lambda i: (i, 0))
    g_spec = pl.BlockSpec((N_DEV, tr, cdim), lambda i: (0, i, 0)) if partials else blk
    res, carried = _call(body, name=name, grid=(r // tr,), in_specs=[blk, blk, blk, g_spec], out_specs=[blk] * 4,
                         out_shape=[jax.ShapeDtypeStruct((r, cdim), f32)] * 4, scratch_shapes=[], sem=("parallel",),
                         args=(w, m, v, g), carry=carry)
    return res if carry is None else (res, carried)


def _sum_sources(parts, name):
    def body(*refs):
        for i_ref, o_ref in zip(refs[:len(parts)], refs[len(parts):]):
            acc = i_ref[0]
            for s in range(1, i_ref.shape[0]):
                acc = acc + i_ref[s]
            o_ref[...] = acc

    return list(_pc(body, name=name, out_shape=[jax.ShapeDtypeStruct(p.shape[1:], f32) for p in parts],
                    compiler_params=_params())(*parts))


def kernel(x, c, ctx, c_ctx, w_mod, b_mod, g_ffn1, w1_gu, w1_down, g_mix, w_in, dw_weight, dw_bias, conv_ln_g, conv_ln_b, w_conv_out, w_alpha_f, b_alpha_f, w_alpha_b, b_alpha_b, gla_norm_g, w_gla_out, w_out, g_ffn2, w2_gu, w2_down, g_final, loss_target, m_c_ctx, m_w_mod, m_b_mod, m_g_ffn1, m_w1_gu, m_w1_down, m_g_mix, m_w_in, m_dw_weight, m_dw_bias, m_conv_ln_g, m_conv_ln_b, m_w_conv_out, m_w_alpha_f, m_b_alpha_f, m_w_alpha_b, m_b_alpha_b, m_gla_norm_g, m_w_gla_out, m_w_out, m_g_ffn2, m_w2_gu, m_w2_down, m_g_final, v_c_ctx, v_w_mod, v_b_mod, v_g_ffn1, v_w1_gu, v_w1_down, v_g_mix, v_w_in, v_dw_weight, v_dw_bias, v_conv_ln_g, v_conv_ln_b, v_w_conv_out, v_w_alpha_f, v_b_alpha_f, v_w_alpha_b, v_b_alpha_b, v_gla_norm_g, v_w_gla_out, v_w_out, v_g_ffn2, v_w2_gu, v_w2_down, v_g_final):
    B, L, D = x.shape
    Lc = ctx.shape[1]
    T, Tc = B * L, B * Lc
    Tall = T + Tc
    F = w1_down.shape[1] * N_DEV
    DK, DV = D // (2 * HEADS), D // HEADS
    QK = HEADS * DK
    PW = 7 * D + LR_PAD
    tm = ROW_TILE
    tpe = L // tm
    nx, nall = T // tm, Tall // tm
    me = 4 * lax.axis_index("x") + 2 * lax.axis_index("y") + lax.axis_index("c")

    rw_all = dict(tm=tm, n_tiles=nall, tpe=tpe, nx_tiles=nx, n_ex=B + 1)
    rw_x = dict(tm=tm, n_tiles=nx, tpe=tpe, nx_tiles=nx, n_ex=B)
    m2 = 2 if (L % (2 * tm) == 0 and Tc % (2 * tm) == 0) else 1
    rw_all2 = dict(tm=m2 * tm, n_tiles=nall // m2, tpe=tpe // m2, nx_tiles=nx // m2, n_ex=B + 1)
    rw_x2 = dict(tm=m2 * tm, n_tiles=nx // m2, tpe=tpe // m2, nx_tiles=nx // m2, n_ex=B)

    dww_g, waf_g, wab_g, c_g = _exchange([dw_weight[0], w_alpha_f[0], w_alpha_b[0], c], False, "gather_first")

    def cols(gat):
        return jnp.transpose(gat, (1, 0, 2)).reshape(gat.shape[1], N_DEV * gat.shape[2])

    def rows_(gat):
        return gat.reshape(N_DEV * gat.shape[1], gat.shape[2])

    dww = cols(dww_g)
    WA = jnp.zeros((LR_PAD, 2 * QK), f32).at[:LOWRANK, :QK].set(cols(waf_g)).at[LOWRANK:2 * LOWRANK, QK:].set(cols(wab_g)).astype(bf16)
    BA = jnp.concatenate([b_alpha_f, b_alpha_b], axis=1)
    c_all = c_g.reshape(N_DEV * B, D)
    c_ctx2 = c_ctx.reshape(1, D)

    ncm = w_mod.shape[2]
    b_mod_loc = lax.dynamic_slice(b_mod, (0, me * ncm), (1, ncm))
    mod_loc = _mod_fwd(c_all, c_ctx2, w_mod[0], b_mod_loc, "mod_fwd")
    (mod_g,) = _exchange([mod_loc], False, "gather_mod")
    mod_full = cols(mod_g)
    mod_tab = jnp.concatenate([lax.dynamic_slice(mod_full, (me * B, 0), (B, N_MOD * D)), mod_full[N_DEV * B:N_DEV * B + 1]], axis=0)
    mods = [mod_tab[:, i * D:(i + 1) * D].reshape(B + 1, 1, D) for i in range(N_MOD)]
    mods_x = [mm[:B] for mm in mods]

    x_all = jnp.concatenate([x.reshape(T, D), ctx.reshape(Tc, D)], axis=0)

    def f_ffn_in(tok, ex, sh):
        return [_rms_mod(tok[0], sh[0], ex[0], ex[1])], [], []

    (u1,), (w1gu_g,) = _rowwise(f_ffn_in, name="ffn1_in", tok_in=[(x_all, D, 0, False)], ex_in=[mods[0], mods[1]], sh_in=[g_ffn1],
                                tok_out=[(D, bf16)], carry=([w1_gu[0].astype(bf16)], False), **rw_all)
    W1gu = cols(w1gu_g)
    (gu1, h1), (w1d_g, win_g) = _ffn_up(u1, W1gu, "ffn1_up", carry=([w1_down[0].astype(bf16), w_in[0].astype(bf16)], False))
    W1d = rows_(w1d_g)
    lr2 = 2 * LOWRANK
    segs = [(0, 2 * D, 0), (2 * D, 2 * D + QK, 6 * D), (2 * D + QK, 3 * D, 6 * D + QK), (3 * D, 4 * D, 2 * D), (4 * D, 5 * D, 3 * D),
            (5 * D, 5 * D + lr2, 7 * D), (5 * D + lr2, 6 * D + lr2, 4 * D), (6 * D + lr2, 7 * D + lr2, 5 * D)]
    wc = w_in.shape[2]
    win_parts = []
    for lo, hi, _ in sorted(segs, key=lambda t: t[2]):
        for d in range(N_DEV):
            a0, a1 = max(lo, d * wc), min(hi, (d + 1) * wc)
            if a0 < a1:
                win_parts.append(win_g[d][:, a0 - d * wc:a1 - d * wc])
    Win = jnp.concatenate(win_parts + [jnp.zeros((D, LR_PAD - lr2), bf16)], axis=1)

    def nn(a, w):
        return jnp.dot(a.astype(bf16), w, preferred_element_type=f32)

    def nt(a, w):
        return lax.dot_general(a.astype(bf16), w, (((1,), (1,)), ((), ())), preferred_element_type=f32)

    def mix_in(xv, fv, gate, sh, sc, g):
        x1 = xv + 0.5 * gate * fv
        return x1, _rms_mod(x1, g, sh, sc)

    def f_mix_in(tok, ex, sh):
        f1v = nn(tok[1], sh[1])
        return list(mix_in(tok[0], f1v, ex[0], ex[1], ex[2], sh[0])) + [f1v], [], []

    x1, um, f1 = _rowwise(f_mix_in, name="ffn1_down_mix_in", tok_in=[(x_all, D, 0, False), (h1, F, 0, False)],
                          ex_in=[mods[2], mods[3], mods[4]], sh_in=[g_mix, W1d], tok_out=[(D, f32), (D, bf16), (D, bf16)], **rw_all2)
    p_all, (wco_g, wgo_g, wo_g, w2gu_g) = _matmul(
        um, Win, "nn", bf16, "in_proj", tm_cap=512, tn_cap=2432,
        carry=([w_conv_out[0].astype(bf16), w_gla_out[0].astype(bf16), w_out[0].astype(bf16), w2_gu[0].astype(bf16)], False))
    Wco, Wgo, Wo, W2gu = rows_(wco_g), rows_(wgo_g), rows_(wo_g), cols(w2gu_g)

    def log_decay(lr, wa, ba):
        z = _bdot(lr, wa, (1, 0)) + ba
        return _log_sigmoid(z) / TAU

    def f_decay(tok, ex, sh):
        return [log_decay(tok[0], sh[0], sh[1])], [], []

    lr_blk = (p_all, LR_PAD, 7 * D // LR_PAD, False)
    (la_all,) = _rowwise(f_decay, name="log_decay", tok_in=[lr_blk], sh_in=[WA, BA], tok_out=[(2 * QK, f32)], **rw_all)

    zeros_s = jnp.zeros((B, HEADS, DV, DK), f32)
    gla_c = dict(row0=T, nb=B, seq=Lc, D=D)
    gla_x = dict(row0=0, nb=B, seq=L, D=D)
    _, hist_cf, s_f = _gla_fwd(p_all, la_all, zeros_s, rev=False, name="gla_ctx_f", **gla_c)
    _, hist_cb, s_b = _gla_fwd(p_all, la_all, zeros_s, rev=True, name="gla_ctx_b", **gla_c)
    (o_f, hist_f, _), (w2d_g,) = _gla_fwd(p_all, la_all, s_f, rev=False, name="gla_x_f", carry=([w2_down[0].astype(bf16)], False), **gla_x)
    W2d = rows_(w2d_g)
    o_b, hist_b, _ = _gla_fwd(p_all, la_all, s_b, rev=True, name="gla_x_b", **gla_x)

    cz = _conv_fwd(p_all, dww, dw_bias, B=B, L=L, D=D, name="conv_fwd")

    def ln_silu(z, g, b):
        mu = jnp.mean(z, axis=-1, keepdims=True)
        var = jnp.mean(jnp.square(z - mu), axis=-1, keepdims=True)
        return jax.nn.silu((z - mu) * lax.rsqrt(var + EPS) * g + b)

    def f_ln(tok, ex, sh):
        zc = ln_silu(tok[0].astype(f32), sh[0], sh[1])
        return [zc, nn(zc, sh[2])], [], []

    zc, yc = _rowwise(f_ln, name="conv_ln_out", tok_in=[(cz, D, 0, False)], sh_in=[conv_ln_g, conv_ln_b, Wco],
                      tok_out=[(D, bf16), (D, bf16)], **rw_x)

    def gla_out(of, ob, og, gn):
        return _head_rms(of.astype(f32) + ob.astype(f32), DV) * gn * jax.nn.silu(og.astype(f32))

    def f_gla_out(tok, ex, sh):
        og2 = gla_out(tok[0], tok[1], tok[2], sh[0])
        return [og2, nn(og2, sh[1])], [], []

    og_blk = (p_all, D, 3, False)
    og2, yg = _rowwise(f_gla_out, name="gla_norm_out", tok_in=[(o_f, D, 0, False), (o_b, D, 0, False), og_blk], sh_in=[gla_norm_g, Wgo],
                       tok_out=[(D, bf16), (D, bf16)], **rw_x)

    def merge(ga, gb, ycv, ygv):
        return jax.nn.sigmoid(ga.astype(f32)) * ycv.astype(f32) + jax.nn.sigmoid(gb.astype(f32)) * ygv.astype(f32)

    def f_merge(tok, ex, sh):
        mg = merge(*tok)
        return [mg, nn(mg, sh[0])], [], []

    ga_blk, gb_blk = (p_all, D, 4, False), (p_all, D, 5, False)
    mg, mix = _rowwise(f_merge, name="merge_mix_out", tok_in=[ga_blk, gb_blk, (yc, D, 0, False), (yg, D, 0, False)], sh_in=[Wo],
                       tok_out=[(D, bf16), (D, f32)], **rw_x)

    def ffn2_in(x1v, mixv, g5, sh, sc, g):
        x2 = x1v + g5 * mixv
        return x2, _rms_mod(x2, g, sh, sc)

    def f_ffn2_in(tok, ex, sh):
        return list(ffn2_in(tok[0], tok[1], ex[0], ex[1], ex[2], sh[0])), [], []

    x2, u2 = _rowwise(f_ffn2_in, name="ffn2_in", tok_in=[(x1, D, 0, False), (mix, D, 0, False)], ex_in=[mods_x[5], mods_x[6], mods_x[7]],
                      sh_in=[g_ffn2], tok_out=[(D, f32), (D, bf16)], **rw_x)
    gu2, h2 = _ffn_up(u2, W2gu, "ffn2_up")

    gf2 = g_final.reshape(1, D)

    def head_loss(x2v, f2v, g8, gf, tgt):
        x3 = x2v + 0.5 * g8 * f2v
        y = x3 * lax.rsqrt(jnp.mean(x3 * x3, axis=-1, keepdims=True) + EPS) * gf
        return 0.5 * jnp.sum(jnp.mean(jnp.square(y - tgt), axis=-1))

    def f_head(tok, ex, sh):
        loss, vjp = jax.vjp(lambda a, b_, c_, d_: head_loss(a, b_, c_, d_, tok[2]), tok[0], nn(tok[1], sh[1]), ex[0], sh[0])
        dx3, df2, dg8, dgf = vjp(jnp.ones((), f32))
        return [dx3, df2], [dg8], [dgf, jnp.broadcast_to(loss.reshape(1, 1), (1, 128))]

    dx3, df2, dg8, dgf, loss_p = _rowwise(
        f_head, name="ffn2_down_head", tok_in=[(x2, D, 0, False), (h2, F, 0, False), (loss_target.reshape(T, D), D, 0, False)],
        ex_in=[mods_x[8]], sh_in=[gf2, W2d], tok_out=[(D, f32), (D, bf16)], ex_out=[D], gl_out=[(1, D), (1, 128)], **rw_x2)

    dgu2 = _ffn_down_dx(df2, W2d, gu2, "ffn2_down_dx")
    gW2d = _matmul(h2, df2, "tn", f32, "ffn2_down_dw", tm_cap=1408)
    du2 = _matmul(dgu2, W2gu, "nt", bf16, "ffn2_up_dx", halves="a")
    gW2gu = _matmul(u2, dgu2, "tn", f32, "ffn2_up_dw", halves="b")

    def f_ffn2_in_bwd(tok, ex, sh):
        _, vjp = jax.vjp(ffn2_in, tok[0], tok[1], ex[0], ex[1], ex[2], sh[0])
        dx2, dmix, dg5, dsh, dsc, dg = vjp((tok[3], tok[2].astype(f32)))
        return [dx2, dmix], [dg5, dsh, dsc], [dg]

    dx2, dmix, dg5, dsh6, dsc7, dg_ffn2 = _rowwise(
        f_ffn2_in_bwd, name="ffn2_in_bwd", tok_in=[(x1, D, 0, False), (mix, D, 0, False), (du2, D, 0, False), (dx3, D, 0, False)],
        ex_in=[mods_x[5], mods_x[6], mods_x[7]], sh_in=[g_ffn2], tok_out=[(D, f32), (D, bf16)], ex_out=[D, D, D], gl_out=[(1, D)], **rw_x)

    gWo = _matmul(mg, dmix, "tn", f32, "mix_out_dw")

    def f_merge_bwd(tok, ex, sh):
        _, vjp = jax.vjp(merge, *[t.astype(f32) for t in tok[:4]])
        dga, dgb, dyc, dyg = vjp(nt(tok[4], sh[0]))
        return [dga, dgb, dyc, dyg], [], []

    dga, dgb, dyc, dyg = _rowwise(f_merge_bwd, name="mix_out_merge_bwd",
                                  tok_in=[ga_blk, gb_blk, (yc, D, 0, False), (yg, D, 0, False), (dmix, D, 0, False)], sh_in=[Wo],
                                  tok_out=[(D, bf16)] * 4, **rw_x)
    gWco = _matmul(zc, dyc, "tn", f32, "conv_out_dw")
    gWgo = _matmul(og2, dyg, "tn", f32, "gla_out_dw")

    def f_ln_bwd(tok, ex, sh):
        _, vjp = jax.vjp(ln_silu, tok[0].astype(f32), sh[0], sh[1])
        dcz, dg, db = vjp(nt(tok[1], sh[2]))
        return [dcz], [], [dg, db, jnp.sum(dcz, axis=0, keepdims=True)]

    dcz, g_ln_g, g_ln_b, g_dwb = _rowwise(f_ln_bwd, name="conv_out_ln_bwd", tok_in=[(cz, D, 0, False), (dyc, D, 0, False)],
                                          sh_in=[conv_ln_g, conv_ln_b, Wco], tok_out=[(D, bf16)], gl_out=[(1, D)] * 3, **rw_x)
    def col_shards(g):
        return jnp.transpose(g.reshape(g.shape[0], N_DEV, g.shape[1] // N_DEV), (1, 0, 2)).astype(bf16)

    def row_shards(g):
        return g.reshape(N_DEV, g.shape[0] // N_DEV, g.shape[1]).astype(bf16)

    (dca, dcb, g_dww), (r_w2d, r_w2gu, r_wo, r_wco, r_wgo) = _conv_bwd(
        p_all, dcz, dww, B=B, L=L, D=D, name="conv_bwd",
        carry=([row_shards(gW2d), col_shards(gW2gu), row_shards(gWo), row_shards(gWco), row_shards(gWgo)], True))

    def f_gla_out_bwd(tok, ex, sh):
        _, vjp = jax.vjp(gla_out, tok[0].astype(f32), tok[1].astype(f32), tok[2].astype(f32), sh[0])
        dof, _, dog, dgn = vjp(nt(tok[3], sh[1]))
        return [dof, dog], [], [dgn]

    d_o, dog, g_gn = _rowwise(f_gla_out_bwd, name="gla_out_norm_bwd",
                              tok_in=[(o_f, D, 0, False), (o_b, D, 0, False), og_blk, (dyg, D, 0, False)], sh_in=[gla_norm_g, Wgo],
                              tok_out=[(D, bf16), (D, bf16)], gl_out=[(1, D)], **rw_x)

    dq_f, dk_f, dv_f, dla_f, ds_f = _gla_bwd(p_all, la_all, hist_f, d_o, zeros_s, rev=False, name="gla_x_f_bwd", **gla_x)
    dq, dk, dv, dla_b, ds_b = _gla_bwd(p_all, la_all, hist_b, d_o, zeros_s, rev=True, name="gla_x_b_bwd", add=(dq_f, dk_f, dv_f), **gla_x)
    dq_cf, dk_cf, dv_cf, dla_cf, _ = _gla_bwd(p_all, la_all, hist_cf, None, ds_f, rev=False, name="gla_ctx_f_bwd", **gla_c)
    _, dk_c, dv_c, dla_cb, _ = _gla_bwd(p_all, la_all, hist_cb, None, ds_b, rev=True, name="gla_ctx_b_bwd", add=(dq_cf, dk_cf, dv_cf), **gla_c)

    dla_all = jnp.concatenate([jnp.concatenate([dla_f, dla_b], axis=1), jnp.concatenate([dla_cf, dla_cb], axis=1)], axis=0)

    def f_decay_bwd(tok, ex, sh):
        _, vjp = jax.vjp(log_decay, tok[0].astype(f32), sh[0].astype(f32), sh[1])
        dlr, dwa, dba = vjp(tok[1])
        return [dlr], [], [dwa, dba]

    dlr, g_WA, g_BA = _rowwise(f_decay_bwd, name="log_decay_bwd", tok_in=[lr_blk, (dla_all, 2 * QK, 0, False)], sh_in=[WA, BA],
                               tok_out=[(LR_PAD, bf16)], gl_out=[(LR_PAD, 2 * QK), (1, 2 * QK)], **rw_all)

    zc_ = functools.partial(jnp.zeros, dtype=bf16)
    dp_x = jnp.concatenate([dca, dcb, dv, dog, dga, dgb, dq, dk, dlr[:T]], axis=1)
    dp_c = jnp.concatenate([zc_((Tc, 2 * D)), dv_c, zc_((Tc, 3 * D)), zc_((Tc, QK)), dk_c, dlr[T:]], axis=1)
    dp_all = jnp.concatenate([dp_x, dp_c], axis=0)
    gWin_p = _matmul(um, dp_all, "tn", f32, "in_proj_dw", tm_cap=512, tn_cap=2432)
    gwin_shards = []
    for d in range(N_DEV):
        parts = []
        for lo, hi, po in segs:
            a0, a1 = max(lo, d * wc), min(hi, (d + 1) * wc)
            if a0 < a1:
                parts.append(gWin_p[:, po + a0 - lo:po + a1 - lo])
        gwin_shards.append(jnp.concatenate(parts, axis=1))
    dum, (r_win,) = _matmul(dp_all, Win, "nt", bf16, "in_proj_dx", tk_cap=2432, carry=([jnp.stack(gwin_shards).astype(bf16)], True))

    def f_mix_in_bwd(tok, ex, sh):
        _, vjp = jax.vjp(mix_in, tok[0], tok[1].astype(f32), ex[0], ex[1], ex[2], sh[0])
        dx1, df1, dgate, dsh, dsc, dg = vjp((tok[3], tok[2].astype(f32)))
        return [dx1, df1], [dgate, dsh, dsc], [dg]

    dx1, df1, dg2, dsh3, dsc4, dg_mix = _rowwise(
        f_mix_in_bwd, name="mix_in_bwd", tok_in=[(x_all, D, 0, False), (f1, D, 0, False), (dum, D, 0, False), (dx2, D, 0, True)],
        ex_in=[mods[2], mods[3], mods[4]], sh_in=[g_mix], tok_out=[(D, f32), (D, bf16)], ex_out=[D, D, D], gl_out=[(1, D)], **rw_all)

    dgu1 = _ffn_down_dx(df1, W1d, gu1, "ffn1_down_dx")
    gW1d = _matmul(h1, df1, "tn", f32, "ffn1_down_dw", tm_cap=1408)
    gW1gu, (r_w1d,) = _matmul(u1, dgu1, "tn", f32, "ffn1_up_dw", carry=([row_shards(gW1d)], True), halves="b")
    du1, (r_w1gu,) = _matmul(dgu1, W1gu, "nt", bf16, "ffn1_up_dx", carry=([col_shards(gW1gu)], True), halves="a")

    def f_ffn_in_bwd(tok, ex, sh):
        _, vjp = jax.vjp(_rms_mod, tok[0], sh[0], ex[0], ex[1])
        dx, dg, dsh, dsc = vjp(tok[1].astype(f32))
        return [dx + tok[2]], [dsh, dsc], [dg]

    dx_all, dsh0, dsc1, dg_ffn1 = _rowwise(
        f_ffn_in_bwd, name="ffn1_in_bwd", tok_in=[(x_all, D, 0, False), (du1, D, 0, False), (dx1, D, 0, False)],
        ex_in=[mods[0], mods[1]], sh_in=[g_ffn1], tok_out=[(D, f32)], ex_out=[D, D], gl_out=[(1, D)], **rw_all)
    grad_x = dx_all[:T].reshape(B, L, D)

    zrow = jnp.zeros((1, 1, D), f32)
    dmod_loc = jnp.concatenate([dsh0, dsc1, dg2, dsh3, dsc4] + [jnp.concatenate([t, zrow], axis=0) for t in (dg5, dsh6, dsc7, dg8)],
                               axis=2).reshape(B + 1, N_MOD * D)
    rows16 = jnp.concatenate([jnp.concatenate([loss_p, jnp.zeros((1, D - loss_p.shape[1]), f32)], axis=1), dg_ffn1, dg_mix, g_dwb, g_ln_g,
                              g_ln_b, g_BA, g_gn, dg_ffn2, dgf, jnp.zeros((6, D), f32)], axis=0)

    def to8(v):
        n_pad = -(-v.shape[1] // 1024) * 1024
        return jnp.pad(v, ((0, 0), (0, n_pad - v.shape[1]))).reshape(8, n_pad // 8)

    def from8(a, n):
        return a.reshape(1, a.size)[:, :n]

    def adam_big(nm, wv, mv, vv, part, carry=None):
        out = _adamw(wv[0], mv[0], vv[0], part, "adamw_" + nm, True, carry=carry)
        res4, carried = out if carry is not None else (out, None)
        return [t[None] for t in res4], carried

    rs_out = dict(w1_gu=r_w1gu, w1_down=r_w1d, w_in=r_win, w_conv_out=r_wco, w_gla_out=r_wgo, w_out=r_wo, w2_gu=r_w2gu, w2_down=r_w2d)
    big = {}
    big["w_in"], (dmod_g, rows_g, dww_sg, wa_sg) = adam_big(
        "w_in", w_in, m_w_in, v_w_in, rs_out["w_in"], carry=([dmod_loc, rows16, g_dww, g_WA[:2 * LOWRANK]], False))
    dmx = dmod_g[:, :B].reshape(N_DEV * B, N_MOD * D)
    dmc = dmod_g[:, B]
    gWmod, gcc_p = _mod_bwd(c_all, c_ctx2, w_mod[0], lax.dynamic_slice(dmx, (0, me * ncm), (N_DEV * B, ncm)),
                            lax.dynamic_slice(dmc, (0, me * ncm), (N_DEV, ncm)), "mod_bwd")

    big["w1_gu"], (gcc_g,) = adam_big("w1_gu", w1_gu, m_w1_gu, v_w1_gu, rs_out["w1_gu"], carry=([to8(gcc_p)], False))
    for nm, wv, mv, vv in (("w1_down", w1_down, m_w1_down, v_w1_down), ("w_conv_out", w_conv_out, m_w_conv_out, v_w_conv_out),
                           ("w_gla_out", w_gla_out, m_w_gla_out, v_w_gla_out), ("w_out", w_out, m_w_out, v_w_out),
                           ("w2_gu", w2_gu, m_w2_gu, v_w2_gu), ("w2_down", w2_down, m_w2_down, v_w2_down)):
        big[nm], _ = adam_big(nm, wv, mv, vv, rs_out[nm])
    big["w_mod"] = [t[None] for t in _adamw(w_mod[0], m_w_mod[0], v_w_mod[0], gWmod, "adamw_w_mod", False)]

    rows_s, dww_s, wa_s, g_cc, g_bmod = _sum_sources(
        [rows_g, dww_sg, wa_sg, gcc_g, jnp.concatenate([dmx, dmc], axis=0).reshape(N_DEV * (B + 1), 8, N_MOD * D // 8)], "sum_small")
    g_cc, g_bmod = from8(g_cc, D), from8(g_bmod, N_MOD * D)
    loss = rows_s[0, 0]
    ncd, nca = dw_weight.shape[2], w_alpha_f.shape[2]
    g_dww_loc = lax.dynamic_slice(dww_s, (0, me * ncd), (CONV_W, ncd))
    g_waf_loc = lax.dynamic_slice(wa_s, (0, me * nca), (LOWRANK, nca))
    g_wab_loc = lax.dynamic_slice(wa_s, (LOWRANK, QK + me * nca), (LOWRANK, nca))
    sm = {k: rows_s[i:i + 1] for i, k in enumerate(["loss", "g_ffn1", "g_mix", "dw_bias", "conv_ln_g", "conv_ln_b", "b_alpha", "gla_norm_g",
                                                     "g_ffn2", "g_final"])}

    small_params = [("c_ctx", c_ctx, m_c_ctx, v_c_ctx, g_cc), ("b_mod", b_mod, m_b_mod, v_b_mod, g_bmod),
                    ("g_ffn1", g_ffn1, m_g_ffn1, v_g_ffn1, sm["g_ffn1"]), ("g_mix", g_mix, m_g_mix, v_g_mix, sm["g_mix"]),
                    ("dw_weight", dw_weight, m_dw_weight, v_dw_weight, g_dww_loc), ("dw_bias", dw_bias, m_dw_bias, v_dw_bias, sm["dw_bias"]),
                    ("conv_ln_g", conv_ln_g, m_conv_ln_g, v_conv_ln_g, sm["conv_ln_g"]),
                    ("conv_ln_b", conv_ln_b, m_conv_ln_b, v_conv_ln_b, sm["conv_ln_b"]),
                    ("w_alpha_f", w_alpha_f, m_w_alpha_f, v_w_alpha_f, g_waf_loc), ("b_alpha_f", b_alpha_f, m_b_alpha_f, v_b_alpha_f, sm["b_alpha"][:, :QK]),
                    ("w_alpha_b", w_alpha_b, m_w_alpha_b, v_w_alpha_b, g_wab_loc), ("b_alpha_b", b_alpha_b, m_b_alpha_b, v_b_alpha_b, sm["b_alpha"][:, QK:]),
                    ("gla_norm_g", gla_norm_g, m_gla_norm_g, v_gla_norm_g, sm["gla_norm_g"]),
                    ("g_ffn2", g_ffn2, m_g_ffn2, v_g_ffn2, sm["g_ffn2"]), ("g_final", g_final, m_g_final, v_g_final, sm["g_final"])]

    def two_d(t, like):
        return t.reshape(like.shape[1:]) if like.ndim == 3 else t.reshape(like.size // 128, 128)

    small_res = _adamw_many([tuple(two_d(t, wv) for t in (wv, mv, vv, gv)) for _, wv, mv, vv, gv in small_params], "adamw_small")
    small_out = {nm: [gv.reshape(wv.shape)] + [t.reshape(wv.shape) for t in r3]
                 for (nm, wv, _, _, gv), r3 in zip(small_params, small_res)}

    order = ["c_ctx", "w_mod", "b_mod", "g_ffn1", "w1_gu", "w1_down", "g_mix", "w_in", "dw_weight", "dw_bias", "conv_ln_g", "conv_ln_b",
             "w_conv_out", "w_alpha_f", "b_alpha_f", "w_alpha_b", "b_alpha_b", "gla_norm_g", "w_gla_out", "w_out", "g_ffn2", "w2_gu",
             "w2_down", "g_final"]
    res = {**big, **small_out}
    return (loss, grad_x, *[res[n][0] for n in order], *[res[n][1] for n in order], *[res[n][2] for n in order], *[res[n][3] for n in order])
```

```python
import functools

import jax
import jax.numpy as jnp
from jax import lax
from jax.experimental import pallas as pl
from jax.experimental.pallas import tpu as pltpu

f32, bf16 = jnp.float32, jnp.bfloat16

N_DEV = 8
HEADS = 4
LOWRANK = 16
CONV_W = 31
CONV_PAD = 16
SUBLANES = 8
CHUNK = 64
SUB = 16
GLA_ROWS = 256
GLA_SAFE_DECAY = 60.0
TAU = 16.0
EPS = 1e-6
N_MOD = 9
LR_PAD = 128
ROW_TILE = 512
V7X_VMEM_BYTES = 64 << 20
VMEM_LIMIT = (V7X_VMEM_BYTES * 3) // 4

ADAM_LR, ADAM_B1, ADAM_B2, ADAM_EPS, ADAM_WD, ADAM_STEP = 0.001, 0.9, 0.999, 1e-08, 0.01, 10

MESH = pl.DeviceIdType.MESH


def _pc(body, **kw):
    return pl.pallas_call(body, **kw)


def _params(*sem):
    return pltpu.CompilerParams(dimension_semantics=sem, vmem_limit_bytes=VMEM_LIMIT)


def _pick(n, cap, unit=128):
    best = None
    for t in range(unit, min(n, cap) + 1, unit):
        if n % t == 0:
            best = t
    return best or n


def _matmul(a, b, mode, out_dtype, name, tm_cap=1024, tn_cap=1536, tk_cap=None, carry=None, halves=None):
    tk_cap = tk_cap or (2048 if mode == "tn" else 2816)
    if halves == "a":
        (_, M, Kh), N = a.shape, b.shape[0]
        K, tk = 2 * Kh, _pick(Kh, tk_cap)
        tm, tn = _pick(M, tm_cap), _pick(N, tn_cap)
        a_spec = pl.BlockSpec((None, tm, tk), lambda i, j, k: (k // (Kh // tk), i, k % (Kh // tk)))
    elif halves == "b":
        (K, M), (_, _, Nh) = a.shape, b.shape
        N, tn = 2 * Nh, _pick(Nh, tn_cap)
        tm, tk = _pick(M, tm_cap), _pick(K, tk_cap)
    else:
        if mode == "tn":
            (K, M), N = a.shape, b.shape[1]
        elif mode == "nt":
            (M, K), N = a.shape, b.shape[0]
        else:
            (M, K), N = a.shape, b.shape[1]
        tm, tn, tk = _pick(M, tm_cap), _pick(N, tn_cap), _pick(K, tk_cap)
    nk = K // tk
    if halves != "a":
        a_spec = pl.BlockSpec((tk, tm), lambda i, j, k: (k, i)) if mode == "tn" else pl.BlockSpec((tm, tk), lambda i, j, k: (i, k))
    if halves == "b":
        b_spec = pl.BlockSpec((None, tk, tn), lambda i, j, k: (j // (Nh // tn), k, j % (Nh // tn)))
    else:
        b_spec = pl.BlockSpec((tn, tk), lambda i, j, k: (j, k)) if mode == "nt" else pl.BlockSpec((tk, tn), lambda i, j, k: (k, j))
    dims = {"nn": ((1,), (0,)), "nt": ((1,), (1,)), "tn": ((0,), (0,))}[mode]

    def body_single(a_ref, b_ref, o_ref):
        o_ref[...] = lax.dot_general(a_ref[...].astype(bf16), b_ref[...].astype(bf16), (dims, ((), ())),
                                     preferred_element_type=f32).astype(out_dtype)

    def body(a_ref, b_ref, o_ref, acc_ref):
        k = pl.program_id(2)
        part = lax.dot_general(a_ref[...].astype(bf16), b_ref[...].astype(bf16), (dims, ((), ())), preferred_element_type=f32)

        @pl.when(k == 0)
        def _():
            acc_ref[...] = part

        @pl.when(k > 0)
        def _():
            acc_ref[...] += part

        @pl.when(k == nk - 1)
        def _():
            o_ref[...] = acc_ref[...].astype(out_dtype)

    (out,), carried = _call(
        body_single if nk == 1 else body, name=name, grid=(M // tm, N // tn, nk), in_specs=[a_spec, b_spec],
        out_specs=[pl.BlockSpec((tm, tn), lambda i, j, k: (i, j))], out_shape=[jax.ShapeDtypeStruct((M, N), out_dtype)],
        scratch_shapes=[] if nk == 1 else [pltpu.VMEM((tm, tn), f32)], sem=("parallel", "parallel", "arbitrary"),
        args=(a, b), carry=carry)
    return out if carry is None else (out, carried)


def _ffn_up(u, Wgu, name, carry=None):
    M, K = u.shape
    F = Wgu.shape[1] // 2
    tm, tn = _pick(M, 512), _pick(F, 1408)
    nj = F // tn

    def body(u_ref, wa_ref, wb_ref, gu_ref, h_ref):
        uv = u_ref[...]
        a = jnp.dot(uv, wa_ref[...], preferred_element_type=f32)
        b = jnp.dot(uv, wb_ref[...], preferred_element_type=f32)
        gu_ref[0] = a.astype(bf16)
        gu_ref[1] = b.astype(bf16)
        h_ref[...] = (jax.nn.silu(a) * b).astype(bf16)

    res, carried = _call(
        body, name=name, grid=(nj, M // tm),
        in_specs=[pl.BlockSpec((tm, K), lambda j, i: (i, 0)), pl.BlockSpec((K, tn), lambda j, i: (0, j)),
                  pl.BlockSpec((K, tn), lambda j, i: (0, nj + j))],
        out_specs=[pl.BlockSpec((2, tm, tn), lambda j, i: (0, i, j)), pl.BlockSpec((tm, tn), lambda j, i: (i, j))],
        out_shape=[jax.ShapeDtypeStruct((2, M, F), bf16), jax.ShapeDtypeStruct((M, F), bf16)],
        scratch_shapes=[], sem=("parallel", "parallel"), args=(u, Wgu, Wgu), carry=carry)
    return res if carry is None else (res, carried)


def _ffn_down_dx(df, Wd, gu, name):
    M, D = df.shape
    F = Wd.shape[0]
    tm, tn = _pick(M, 512), _pick(F, 1408)

    def body(df_ref, w_ref, gu_ref, o_ref):
        dh = lax.dot_general(df_ref[...], w_ref[...], (((1,), (1,)), ((), ())), preferred_element_type=f32)
        a, b = gu_ref[0].astype(f32), gu_ref[1].astype(f32)
        sg = jax.nn.sigmoid(a)
        o_ref[0] = (dh * b * sg * (1.0 + a * (1.0 - sg))).astype(bf16)
        o_ref[1] = (dh * a * sg).astype(bf16)

    return _pc(
        body, name=name, grid=(F // tn, M // tm),
        in_specs=[pl.BlockSpec((tm, D), lambda j, i: (i, 0)), pl.BlockSpec((tn, D), lambda j, i: (j, 0)),
                  pl.BlockSpec((2, tm, tn), lambda j, i: (0, i, j))],
        out_specs=pl.BlockSpec((2, tm, tn), lambda j, i: (0, i, j)), out_shape=jax.ShapeDtypeStruct((2, M, F), bf16),
        compiler_params=_params("parallel", "parallel"))(df, Wd, gu)


def _rowwise(fn, *, name, tm, n_tiles, tpe, nx_tiles, n_ex, tok_in=(), ex_in=(), sh_in=(), tok_out=(), ex_out=(), gl_out=(), carry=None):
    def seg(i):
        return jnp.minimum(i // tpe, n_ex - 1)

    in_specs, args = [], []
    for arr, w, cb, x_only in tok_in:
        if x_only:
            in_specs.append(pl.BlockSpec((tm, w), functools.partial(lambda i, cb: (jnp.minimum(i, nx_tiles - 1), cb), cb=cb)))
        else:
            in_specs.append(pl.BlockSpec((tm, w), functools.partial(lambda i, cb: (i, cb), cb=cb)))
        args.append(arr)
    for arr in ex_in:
        in_specs.append(pl.BlockSpec((1, 1, arr.shape[-1]), lambda i: (seg(i), 0, 0)))
        args.append(arr)
    for arr in sh_in:
        in_specs.append(pl.BlockSpec(arr.shape, functools.partial(lambda i, nd: (0,) * nd, nd=arr.ndim)))
        args.append(arr)
    out_specs, out_shape = [], []
    for w, dt in tok_out:
        out_specs.append(pl.BlockSpec((tm, w), lambda i: (i, 0)))
        out_shape.append(jax.ShapeDtypeStruct((n_tiles * tm, w), dt))
    for w in ex_out:
        out_specs.append(pl.BlockSpec((1, 1, w), lambda i: (seg(i), 0, 0)))
        out_shape.append(jax.ShapeDtypeStruct((n_ex, 1, w), f32))
    for r, w in gl_out:
        out_specs.append(pl.BlockSpec((r, w), lambda i: (0, 0)))
        out_shape.append(jax.ShapeDtypeStruct((r, w), f32))
    n_tok, n_exi, n_sh = len(tok_in), len(ex_in), len(sh_in)
    n_to, n_eo = len(tok_out), len(ex_out)
    x_only_flags = [t[3] for t in tok_in]

    def body(*refs):
        i = pl.program_id(0)
        ins, outs = refs[: n_tok + n_exi + n_sh], refs[n_tok + n_exi + n_sh:]
        is_x = i < nx_tiles
        tok_vals = []
        for r, xo in zip(ins[:n_tok], x_only_flags):
            v = r[...]
            tok_vals.append(jnp.where(is_x, v, jnp.zeros_like(v)) if xo else v)
        ex_vals = [r[0] for r in ins[n_tok:n_tok + n_exi]]
        sh_vals = [r[...] for r in ins[n_tok + n_exi:]]
        t_o, e_o, g_o = fn(tok_vals, ex_vals, sh_vals)
        for r, v in zip(outs[:n_to], t_o):
            r[...] = v.astype(r.dtype)
        first = jnp.logical_and(i % tpe == 0, i <= nx_tiles)
        for r, v in zip(outs[n_to:n_to + n_eo], e_o):
            @pl.when(first)
            def _(r=r, v=v):
                r[0] = v

            @pl.when(jnp.logical_not(first))
            def _(r=r, v=v):
                r[0] += v
        for r, v in zip(outs[n_to + n_eo:], g_o):
            @pl.when(i == 0)
            def _(r=r, v=v):
                r[...] = v

            @pl.when(i > 0)
            def _(r=r, v=v):
                r[...] += v

    res, carried = _call(body, name=name, grid=(n_tiles,), in_specs=in_specs, out_specs=out_specs, out_shape=out_shape,
                         scratch_shapes=[], sem=("arbitrary",), args=args, carry=carry)
    return res if carry is None else (res, carried)


def _rms_mod(x, g, sh, sc):
    y = x * lax.rsqrt(jnp.mean(x * x, axis=-1, keepdims=True) + EPS) * g
    return y * (1.0 + sc) + sh


def _log_sigmoid(z):
    return jnp.minimum(z, 0.0) - jnp.log(1.0 + jnp.exp(-jnp.abs(z)))


def _head_rms(o, DV):
    parts = []
    for h in range(HEADS):
        oh = o[:, h * DV:(h + 1) * DV]
        parts.append(oh * lax.rsqrt(jnp.mean(oh * oh, axis=-1, keepdims=True) + EPS))
    return jnp.concatenate(parts, axis=1)


@functools.partial(jax.custom_vjp, nondiff_argnums=(2,))
def _bdot(a, b, dims):
    return lax.dot_general(a.astype(bf16), b.astype(bf16), (((dims[0],), (dims[1],)), ((), ())), preferred_element_type=f32)


def _bdot_fwd(a, b, dims):
    return _bdot(a, b, dims), (a, b)


def _bdot_bwd(dims, res, g):
    a, b = res
    ca, cb = dims
    da = _bdot(g, b, (1, 1 - cb)) if ca == 1 else _bdot(b, g, (1 - cb, 1))
    db = _bdot(a, g, (1 - ca, 0)) if cb == 0 else _bdot(g, a, (0, 1 - ca))
    return da, db


_bdot.defvjp(_bdot_fwd, _bdot_bwd)


def _split_dot(m, x, dims):
    mb, rem, acc = m.astype(bf16), x, None
    for _ in range(3):
        piece = rem.astype(bf16)
        rem = rem - piece.astype(f32)
        part = lax.dot_general(mb, piece, (((dims[0],), (dims[1],)), ((), ())), preferred_element_type=f32)
        acc = part if acc is None else acc + part
    return acc


@jax.custom_vjp
def _tri_cumsum(tri, g):
    return _split_dot(tri, g, (1, 0))


def _tri_cumsum_fwd(tri, g):
    return _tri_cumsum(tri, g), tri


def _tri_cumsum_bwd(tri, db):
    return jnp.zeros_like(tri), _split_dot(tri, db, (0, 0))


_tri_cumsum.defvjp(_tri_cumsum_fwd, _tri_cumsum_bwd)


def _gla_chunk(St, q, k, v, g, *, rev, scale, exact):
    C, DK = q.shape
    r = lax.broadcasted_iota(jnp.int32, (C, C), 0)
    c = lax.broadcasted_iota(jnp.int32, (C, C), 1)
    causal = (r <= c) if rev else (r >= c)
    b = _tri_cumsum(causal.astype(f32), g)
    qs = q * scale
    qe = qs * jnp.exp(b)
    inter = _bdot(qe, St, (1, 1))
    b_last = b[0:1] if rev else b[C - 1:C]
    kd = k * jnp.exp(b_last - b)
    St_new = St * jnp.exp(b_last) + _bdot(v, kd, (0, 0))
    if not exact:
        att = jnp.where(causal, _bdot(qe, k * jnp.exp(-b), (1, 1)), 0.0)
        return St_new, inter + _bdot(att, v, (1, 0))
    rr = lax.broadcasted_iota(jnp.int32, (SUB, SUB, DK), 0)
    cc = lax.broadcasted_iota(jnp.int32, (SUB, SUB, DK), 1)
    m3 = (rr <= cc) if rev else (rr >= cc)
    outs = []
    for i in range(C // SUB):
        lo, hi = i * SUB, (i + 1) * SUB
        bi, qi, ki, vi = b[lo:hi], qs[lo:hi], k[lo:hi], v[lo:hi]
        rel = bi[:, None, :] - bi[None, :, :]
        e = jnp.where(m3, jnp.exp(jnp.where(m3, rel, 0.0)), 0.0)
        att = jnp.sum(qi[:, None, :] * e * ki[None, :, :], axis=-1)
        acc = _bdot(att, vi, (1, 0))
        ref_row = b[hi - 1:hi] if rev else b[lo:lo + 1]
        prev = slice(hi, C) if rev else slice(0, lo)
        if (hi < C) if rev else (lo > 0):
            qn = qi * jnp.exp(bi - ref_row)
            ks = k[prev] * jnp.exp(ref_row - b[prev])
            acc = acc + _bdot(_bdot(qn, ks, (1, 1)), v[prev], (1, 0))
        outs.append(acc)
    return St_new, inter + jnp.concatenate(outs, axis=0)


def _mild_decay(la_ref):
    return jnp.min(la_ref[...]) >= -GLA_SAFE_DECAY / CHUNK


def _gla_specs(D, rev_blocks, row0, seq):
    DK, DV = D // (2 * HEADS), D // HEADS
    nblk = seq // GLA_ROWS
    rb0 = row0 // GLA_ROWS

    def blk(j):
        return (nblk - 1 - j) if rev_blocks else j

    return DK, DV, nblk, rb0, blk


def _gla_in_specs(D, rev, rows):
    QK = D // 2
    return [
        pl.BlockSpec((GLA_ROWS, QK), lambda b, j: (rows(b, j), 6 * D // QK)),
        pl.BlockSpec((GLA_ROWS, QK), lambda b, j: (rows(b, j), 6 * D // QK + 1)),
        pl.BlockSpec((GLA_ROWS, D), lambda b, j: (rows(b, j), 2)),
        pl.BlockSpec((GLA_ROWS, QK), lambda b, j: (rows(b, j), 1 if rev else 0)),
    ]


def _gla_fwd(p_all, la_all, s0, *, rev, row0, nb, seq, D, name, carry=None):
    DK, DV, nblk, rb0, blk = _gla_specs(D, rev, row0, seq)
    cpb = GLA_ROWS // CHUNK

    def rows(b, j):
        return rb0 + b * nblk + blk(j)

    in_specs = _gla_in_specs(D, rev, rows) + [pl.BlockSpec((1, HEADS, DV, DK), lambda b, j: (b, 0, 0, 0))]
    out_specs = [
        pl.BlockSpec((GLA_ROWS, D), lambda b, j: (b * nblk + blk(j), 0)),
        pl.BlockSpec((1, HEADS, cpb, DV, DK), lambda b, j: (b, 0, blk(j), 0, 0)),
        pl.BlockSpec((1, HEADS, DV, DK), lambda b, j: (b, 0, 0, 0)),
    ]
    out_shape = [
        jax.ShapeDtypeStruct((nb * seq, D), bf16),
        jax.ShapeDtypeStruct((nb, HEADS, seq // CHUNK, DV, DK), bf16),
        jax.ShapeDtypeStruct((nb, HEADS, DV, DK), f32),
    ]
    chunk = functools.partial(_gla_chunk, rev=rev, scale=DK ** -0.5)

    def body(q_ref, k_ref, v_ref, la_ref, s0_ref, o_ref, hist_ref, sfin_ref, st_ref):
        j = pl.program_id(1)

        @pl.when(j == 0)
        def _():
            st_ref[...] = s0_ref[0]

        def step(ci, exact):
            cc = (cpb - 1 - ci) if rev else ci
            sl = pl.ds(cc * CHUNK, CHUNK)
            for h in range(HEADS):
                kq, kv = pl.ds(h * DK, DK), pl.ds(h * DV, DV)
                St = st_ref[h]
                hist_ref[0, h, cc] = St.astype(bf16)
                St2, o = chunk(St, q_ref[sl, kq].astype(f32), k_ref[sl, kq].astype(f32), v_ref[sl, kv].astype(f32), la_ref[sl, kq],
                               exact=exact)
                o_ref[sl, kv] = o.astype(bf16)
                st_ref[h] = St2

        mild = _mild_decay(la_ref)
        for exact in (False, True):
            @pl.when(jnp.logical_not(mild) if exact else mild)
            def _(exact=exact):
                for ci in range(cpb):
                    step(ci, exact)

        @pl.when(j == nblk - 1)
        def _():
            sfin_ref[0] = st_ref[...]

    res, carried = _call(body, name=name, grid=(nb, nblk), in_specs=in_specs, out_specs=out_specs, out_shape=out_shape,
                         scratch_shapes=[pltpu.VMEM((HEADS, DV, DK), f32)], sem=("parallel", "arbitrary"),
                         args=(p_all, p_all, p_all, la_all, s0), carry=carry)
    return res if carry is None else (res, carried)


def _gla_bwd(p_all, la_all, hist, do, dsfin, *, rev, row0, nb, seq, D, name, add=None):
    DK, DV, nblk, rb0, blk = _gla_specs(D, not rev, row0, seq)
    cpb = GLA_ROWS // CHUNK
    QK = HEADS * DK
    has_do = do is not None

    def rows(b, j):
        return rb0 + b * nblk + blk(j)

    in_specs = _gla_in_specs(D, rev, rows) + [
        pl.BlockSpec((1, HEADS, cpb, DV, DK), lambda b, j: (b, 0, blk(j), 0, 0)),
        pl.BlockSpec((1, HEADS, DV, DK), lambda b, j: (b, 0, 0, 0)),
    ]
    args = [p_all, p_all, p_all, la_all, hist, dsfin]
    if has_do:
        in_specs.append(pl.BlockSpec((GLA_ROWS, D), lambda b, j: (b * nblk + blk(j), 0)))
        args.append(do)
    if add is not None:
        in_specs += [pl.BlockSpec((GLA_ROWS, t.shape[1]), lambda b, j: (b * nblk + blk(j), 0)) for t in add]
        args += list(add)
    gdt = f32 if add is None else bf16
    out_specs = [
        pl.BlockSpec((GLA_ROWS, QK), lambda b, j: (b * nblk + blk(j), 0)),
        pl.BlockSpec((GLA_ROWS, QK), lambda b, j: (b * nblk + blk(j), 0)),
        pl.BlockSpec((GLA_ROWS, D), lambda b, j: (b * nblk + blk(j), 0)),
        pl.BlockSpec((GLA_ROWS, QK), lambda b, j: (b * nblk + blk(j), 0)),
        pl.BlockSpec((1, HEADS, DV, DK), lambda b, j: (b, 0, 0, 0)),
    ]
    out_shape = [
        jax.ShapeDtypeStruct((nb * seq, QK), gdt), jax.ShapeDtypeStruct((nb * seq, QK), gdt),
        jax.ShapeDtypeStruct((nb * seq, D), gdt), jax.ShapeDtypeStruct((nb * seq, QK), f32),
        jax.ShapeDtypeStruct((nb, HEADS, DV, DK), f32),
    ]
    chunk = functools.partial(_gla_chunk, rev=rev, scale=DK ** -0.5)

    def body(*refs):
        refs = list(refs)
        q_ref, k_ref, v_ref, la_ref, hist_ref, dsfin_ref = refs[:6]
        do_ref = refs[6] if has_do else None
        add_refs = refs[6 + has_do:len(refs) - 6]
        dq_ref, dk_ref, dv_ref, dla_ref, ds0_ref, ds_ref = refs[len(refs) - 6:]
        j = pl.program_id(1)

        @pl.when(j == 0)
        def _():
            ds_ref[...] = dsfin_ref[0]

        def step(ci, exact):
            cc = ci if rev else (cpb - 1 - ci)
            sl = pl.ds(cc * CHUNK, CHUNK)
            for h in range(HEADS):
                kq, kv = pl.ds(h * DK, DK), pl.ds(h * DV, DV)
                prim = (hist_ref[0, h, cc].astype(f32), q_ref[sl, kq].astype(f32), k_ref[sl, kq].astype(f32), v_ref[sl, kv].astype(f32), la_ref[sl, kq])
                _, vjp = jax.vjp(functools.partial(chunk, exact=exact), *prim)
                d_o = do_ref[sl, kv].astype(f32) if has_do else jnp.zeros((CHUNK, DV), f32)
                dSt, dq, dk, dv, dg = vjp((ds_ref[h], d_o))
                if add is not None:
                    dq, dk, dv = dq + add_refs[0][sl, kq], dk + add_refs[1][sl, kq], dv + add_refs[2][sl, kv]
                dq_ref[sl, kq] = dq.astype(gdt)
                dk_ref[sl, kq] = dk.astype(gdt)
                dv_ref[sl, kv] = dv.astype(gdt)
                dla_ref[sl, kq] = dg
                ds_ref[h] = dSt

        mild = _mild_decay(la_ref)
        for exact in (False, True):
            @pl.when(jnp.logical_not(mild) if exact else mild)
            def _(exact=exact):
                for ci in range(cpb):
                    step(ci, exact)

        @pl.when(j == nblk - 1)
        def _():
            ds0_ref[0] = ds_ref[...]

    return _pc(body, name=name, grid=(nb, nblk), in_specs=in_specs, out_specs=out_specs, out_shape=out_shape,
               scratch_shapes=[pltpu.VMEM((HEADS, DV, DK), f32)], compiler_params=_params("parallel", "arbitrary"))(*args)


def _conv_fwd(p_all, dw_w, dw_b, *, B, L, D, name):
    ct = _pick(D, 256)
    nj = D // ct
    st = _pick(L, 128, 8)
    off = CONV_PAD - CONV_W // 2

    def body(a_ref, b_ref, w_ref, bias_ref, o_ref, zs_ref):
        _fill_shifted(zs_ref, L, lambda t0, n: a_ref[pl.ds(t0, n), :].astype(f32) * jax.nn.sigmoid(b_ref[pl.ds(t0, n), :].astype(f32)))
        for t0 in range(0, L, st):
            acc = jnp.zeros((st, ct), f32) + bias_ref[...]
            for k in range(CONV_W):
                acc = acc + w_ref[pl.ds(k, 1), :] * _window(zs_ref, t0 + k + off, st)
            o_ref[pl.ds(t0, st), :] = acc.astype(bf16)

    return _pc(
        body, name=name, grid=(B, nj),
        in_specs=[pl.BlockSpec((L, ct), lambda b, j: (b, j)), pl.BlockSpec((L, ct), lambda b, j: (b, nj + j)),
                  pl.BlockSpec((CONV_W, ct), lambda b, j: (0, j)), pl.BlockSpec((1, ct), lambda b, j: (0, j))],
        out_specs=pl.BlockSpec((L, ct), lambda b, j: (b, j)), out_shape=jax.ShapeDtypeStruct((B * L, D), bf16),
        scratch_shapes=[pltpu.VMEM((SUBLANES, L + 2 * CONV_PAD, ct), f32)], compiler_params=_params("parallel", "parallel"),
    )(p_all, p_all, dw_w, dw_b)


def _fill_shifted(zs_ref, L, rows):
    lp = L + 2 * CONV_PAD
    ct = zs_ref.shape[2]
    step = 256
    zs_ref[0, pl.ds(0, CONV_PAD), :] = jnp.zeros((CONV_PAD, ct), f32)
    zs_ref[0, pl.ds(CONV_PAD + L, CONV_PAD), :] = jnp.zeros((CONV_PAD, ct), f32)
    for t0 in range(0, L, step):
        n = min(step, L - t0)
        zs_ref[0, pl.ds(CONV_PAD + t0, n), :] = rows(t0, n)
    for r in range(1, SUBLANES):
        for i0 in range(0, lp - SUBLANES, step):
            n = min(step, lp - SUBLANES - i0)
            zs_ref[r, pl.ds(i0, n), :] = zs_ref[0, pl.ds(i0 + r, n), :]


def _window(zs_ref, start, n):
    r = start % SUBLANES
    return zs_ref[r, pl.ds(start - r, n), :]


def _conv_bwd(p_all, dcz, dw_w, *, B, L, D, name, carry=None):
    ct = _pick(D, 128)
    nj = D // ct
    st = _pick(L, 256, 8)
    half = CONV_W // 2

    def body(a_ref, b_ref, dcz_ref, w_ref, da_ref, db_ref, ddw_ref, zs_ref, ds_ref):
        bi = pl.program_id(1)
        _fill_shifted(zs_ref, L, lambda t0, n: a_ref[pl.ds(t0, n), :].astype(f32) * jax.nn.sigmoid(b_ref[pl.ds(t0, n), :].astype(f32)))
        _fill_shifted(ds_ref, L, lambda t0, n: dcz_ref[pl.ds(t0, n), :].astype(f32))

        @pl.when(bi == 0)
        def _():
            ddw_ref[...] = jnp.zeros_like(ddw_ref)

        for t0 in range(0, L, st):
            acc = jnp.zeros((st, ct), f32)
            for k in range(CONV_W):
                acc = acc + w_ref[pl.ds(k, 1), :] * _window(ds_ref, t0 + CONV_PAD + half - k, st)
            a_t = a_ref[pl.ds(t0, st), :].astype(f32)
            sg_t = jax.nn.sigmoid(b_ref[pl.ds(t0, st), :].astype(f32))
            da_ref[pl.ds(t0, st), :] = (acc * sg_t).astype(bf16)
            db_ref[pl.ds(t0, st), :] = (acc * a_t * sg_t * (1.0 - sg_t)).astype(bf16)

        parts = [jnp.zeros((SUBLANES, ct), f32) for _ in range(CONV_W)]
        sw = _pick(L, 64, SUBLANES)
        for t0 in range(0, L, sw):
            dout = dcz_ref[pl.ds(t0, sw), :].astype(f32)
            for k in range(CONV_W):
                prod = dout * _window(zs_ref, t0 + k + CONV_PAD - half, sw)
                for i in range(0, sw, SUBLANES):
                    parts[k] = parts[k] + prod[i:i + SUBLANES]
        for k in range(CONV_W):
            ddw_ref[pl.ds(k, 1), :] += jnp.sum(parts[k], axis=0, keepdims=True)

    res, carried = _call(
        body, name=name, grid=(nj, B),
        in_specs=[pl.BlockSpec((L, ct), lambda j, b: (b, j)), pl.BlockSpec((L, ct), lambda j, b: (b, nj + j)),
                  pl.BlockSpec((L, ct), lambda j, b: (b, j)), pl.BlockSpec((CONV_W, ct), lambda j, b: (0, j))],
        out_specs=[pl.BlockSpec((L, ct), lambda j, b: (b, j)), pl.BlockSpec((L, ct), lambda j, b: (b, j)),
                   pl.BlockSpec((2 * CONV_PAD, ct), lambda j, b: (0, j))],
        out_shape=[jax.ShapeDtypeStruct((B * L, D), bf16), jax.ShapeDtypeStruct((B * L, D), bf16),
                   jax.ShapeDtypeStruct((2 * CONV_PAD, D), f32)],
        scratch_shapes=[pltpu.VMEM((SUBLANES, L + 2 * CONV_PAD, ct), f32), pltpu.VMEM((SUBLANES, L + 2 * CONV_PAD, ct), f32)],
        sem=("parallel", "arbitrary"), args=(p_all, p_all, dcz, dw_w), carry=carry)
    return res if carry is None else (res, carried)


def _exchange(arrs, scatter, name):
    ex = _Exchange(arrs, scatter)
    n = ex.n

    def body(*refs):
        ex.start(refs[:n], refs[n:2 * n], refs[2 * n:])
        ex.finish(refs[:n], refs[n:2 * n], refs[2 * n:])

    res = _pc(body, name=name, in_specs=ex.specs, out_specs=ex.specs, out_shape=ex.out_shape, scratch_shapes=ex.scratch)(*arrs)
    return list(res)


class _Exchange:
    def __init__(self, arrs, scatter):
        self.arrs, self.scatter, self.n = list(arrs), scatter, len(arrs)
        self.out_shape = [jax.ShapeDtypeStruct(((N_DEV,) + a.shape[1:]) if scatter else ((N_DEV,) + a.shape), a.dtype) for a in arrs]
        self.specs = [pl.BlockSpec(memory_space=pl.ANY)] * self.n
        self.scratch = [pltpu.SemaphoreType.DMA((self.n, N_DEV - 1)), pltpu.SemaphoreType.DMA((self.n, N_DEV - 1)),
                        pltpu.SemaphoreType.DMA((self.n,))]

    def _copies(self, ins, outs, sems, landing):
        send_sems, recv_sems, local_sems = sems
        me = 4 * lax.axis_index("x") + 2 * lax.axis_index("y") + lax.axis_index("c")
        if landing:
            local = []
        else:
            local = [pltpu.make_async_copy(ins[a].at[me] if self.scatter else ins[a], outs[a].at[me], local_sems.at[a]) for a in range(self.n)]
        remote = []
        for k in range(1, N_DEV):
            p = (me + (N_DEV - k if landing else k)) % N_DEV
            for a in range(self.n):
                remote.append(pltpu.make_async_remote_copy(
                    src_ref=ins[a].at[p] if self.scatter else ins[a], dst_ref=outs[a].at[p if landing else me],
                    send_sem=send_sems.at[a, k - 1], recv_sem=recv_sems.at[a, k - 1],
                    device_id=(p // 4, (p // 2) % 2, p % 2), device_id_type=MESH))
        return local, remote

    def _gather_plan(self, ins, outs, sems):
        send_sems, recv_sems, local_sems = sems
        x, y, c = lax.axis_index("x"), lax.axis_index("y"), lax.axis_index("c")
        chips = [(1 - x, y), (x, 1 - y), (1 - x, 1 - y)]

        def blk(px, py, pc):
            return 4 * px + 2 * py + pc

        def copy(a, k, block, to, own):
            return pltpu.make_async_remote_copy(
                src_ref=ins[a] if own else outs[a].at[block], dst_ref=outs[a].at[block],
                send_sem=send_sems.at[a, k], recv_sem=recv_sems.at[a, k], device_id=to, device_id_type=MESH)

        me = blk(x, y, c)
        local = [pltpu.make_async_copy(ins[a], outs[a].at[me], local_sems.at[a]) for a in range(self.n)]
        return local, copy, me, (x, y, 1 - c), chips, blk, c

    def start(self, ins, outs, sems):
        if self.scatter:
            local, sends = self._copies(ins, outs, sems, False)
            for cp in local + sends:
                cp.start()
            return
        local, copy, me, sibling, chips, _, c = self._gather_plan(ins, outs, sems)
        for cp in local:
            cp.start()
        for a in range(self.n):
            copy(a, 0, me, sibling, True).start()
            for j, chip in enumerate(chips):
                copy(a, 1 + j, me, (*chip, c), True).start()

    def finish(self, ins, outs, sems):
        if self.scatter:
            for cp in self._copies(ins, outs, sems, True)[1]:
                cp.wait_recv()
            local, sends = self._copies(ins, outs, sems, False)
            for cp in sends:
                cp.wait_send()
            for cp in local:
                cp.wait()
            return
        local, copy, me, sibling, chips, blk, c = self._gather_plan(ins, outs, sems)
        for j, chip in enumerate(chips):
            for a in range(self.n):
                copy(a, 1 + j, blk(*chip, c), sibling, True).wait_recv()
                copy(a, 4 + j, blk(*chip, c), sibling, False).start()
        for a in range(self.n):
            copy(a, 0, blk(*sibling), sibling, True).wait_recv()
            for j, chip in enumerate(chips):
                copy(a, 4 + j, blk(*chip, 1 - c), sibling, False).wait_recv()
        for a in range(self.n):
            copy(a, 0, me, sibling, True).wait_send()
            for j, chip in enumerate(chips):
                copy(a, 1 + j, me, (*chip, c), True).wait_send()
                copy(a, 4 + j, blk(*chip, c), sibling, False).wait_send()
        for cp in local:
            cp.wait()


def _carried(inner, n_in, n_out, grid, ex):
    n = ex.n

    def body(*refs):
        own_in, c_in = refs[:n_in], refs[n_in:n_in + n]
        own_out, c_out = refs[n_in + n:n_in + n + n_out], refs[n_in + n + n_out:n_in + 2 * n + n_out]
        rest = refs[n_in + 2 * n + n_out:]
        own_scr, sems = rest[:len(rest) - 3], rest[len(rest) - 3:]
        pids = [pl.program_id(d) for d in range(len(grid))]
        first = functools.reduce(jnp.logical_and, [p == 0 for p in pids])
        last = functools.reduce(jnp.logical_and, [p == g - 1 for p, g in zip(pids, grid)])

        @pl.when(first)
        def _():
            ex.start(c_in, c_out, sems)

        inner(*own_in, *own_out, *own_scr)

        @pl.when(last)
        def _():
            ex.finish(c_in, c_out, sems)

    return body


def _call(inner, *, name, grid, in_specs, out_specs, out_shape, scratch_shapes, sem, args, carry=None):
    if carry is None:
        res = _pc(inner, name=name, grid=grid, in_specs=in_specs, out_specs=out_specs, out_shape=out_shape,
                  scratch_shapes=scratch_shapes, compiler_params=_params(*sem))(*args)
        return list(res), None
    ex = _Exchange(*carry)
    res = _pc(_carried(inner, len(in_specs), len(out_specs), grid, ex), name=name, grid=grid,
              in_specs=list(in_specs) + ex.specs, out_specs=list(out_specs) + ex.specs, out_shape=list(out_shape) + ex.out_shape,
              scratch_shapes=list(scratch_shapes) + ex.scratch, compiler_params=_params(*(["arbitrary"] * len(grid))))(*args, *ex.arrs)
    res = list(res)
    return res[:len(out_specs)], res[len(out_specs):]


def _mod_fwd(c_all, c_ctx, w_loc, b_loc, name):
    nr, D = c_all.shape
    nc = w_loc.shape[1]

    def body(c_ref, cc_ref, w_ref, b_ref, o_ref):
        a = jnp.concatenate([c_ref[...], jnp.broadcast_to(cc_ref[...], (8, D))], axis=0)
        s = jax.nn.silu(a).astype(bf16)
        o_ref[...] = jnp.dot(s, w_ref[...].astype(bf16), preferred_element_type=f32) + b_ref[...]

    return _pc(body, name=name, out_shape=jax.ShapeDtypeStruct((nr + 8, nc), f32), compiler_params=_params())(c_all, c_ctx, w_loc, b_loc)


def _mod_bwd(c_all, c_ctx, w_loc, dmx_loc, dmc_loc, name):
    nr, D = c_all.shape
    nc = w_loc.shape[1]

    def body(c_ref, cc_ref, w_ref, dmx_ref, dmc_ref, gw_ref, gc_ref):
        cc = cc_ref[...]
        a = jnp.concatenate([c_ref[...], jnp.broadcast_to(cc, (N_DEV, D))], axis=0)
        s = jax.nn.silu(a).astype(bf16)
        g = jnp.concatenate([dmx_ref[...], dmc_ref[...]], axis=0).astype(bf16)
        gw_ref[...] = lax.dot_general(s, g, (((0,), (0,)), ((), ())), preferred_element_type=f32)
        dmc = jnp.sum(dmc_ref[...], axis=0, keepdims=True)
        ds = lax.dot_general(jnp.broadcast_to(dmc, (8, nc)).astype(bf16), w_ref[...].astype(bf16), (((1,), (1,)), ((), ())),
                             preferred_element_type=f32)[0:1]
        sg = jax.nn.sigmoid(cc)
        gc_ref[...] = ds * (sg * (1.0 + cc * (1.0 - sg)))

    return _pc(body, name=name, out_shape=[jax.ShapeDtypeStruct((D, nc), f32), jax.ShapeDtypeStruct((1, D), f32)],
               compiler_params=_params())(c_all, c_ctx, w_loc, dmx_loc, dmc_loc)


def _adamw_math(w, g, m, v):
    m2 = ADAM_B1 * m + (1.0 - ADAM_B1) * g
    v2 = ADAM_B2 * v + (1.0 - ADAM_B2) * jnp.square(g)
    m_hat = m2 / (1.0 - ADAM_B1 ** ADAM_STEP)
    v_hat = v2 / (1.0 - ADAM_B2 ** ADAM_STEP)
    delta = -ADAM_LR * (m_hat / (jnp.sqrt(v_hat) + ADAM_EPS) + ADAM_WD * w)
    return delta, m2, v2


def _adamw_many(params, name):
    n = len(params)

    def body(*refs):
        ins, outs = refs[:4 * n], refs[4 * n:]
        for i in range(n):
            w, m, v, g = (ins[4 * i + k][...] for k in range(4))
            d, m2, v2 = _adamw_math(w, g, m, v)
            outs[3 * i][...] = d
            outs[3 * i + 1][...] = m2
            outs[3 * i + 2][...] = v2

    res = _pc(body, name=name, out_shape=[jax.ShapeDtypeStruct(p[0].shape, f32) for p in params for _ in range(3)],
              compiler_params=_params())(*[a for p in params for a in p])
    return [tuple(res[3 * i:3 * i + 3]) for i in range(n)]


def _adamw(w, m, v, g, name, partials, carry=None):
    r, cdim = w.shape
    tr = _pick(r, 256, 8)

    def body(w_ref, m_ref, v_ref, g_ref, og_ref, od_ref, om_ref, ov_ref):
        if partials:
            g = g_ref[0].astype(f32)
            for s in range(1, N_DEV):
                g = g + g_ref[s].astype(f32)
        else:
            g = g_ref[...]
        d, m2, v2 = _adamw_math(w_ref[...], g, m_ref[...], v_ref[...])
        og_ref[...] = g
        od_ref[...] = d
        om_ref[...] = m2
        ov_ref[...] = v2

    blk = pl.BlockSpec((tr, cdim), lambda i: (i, 0))
    g_spec = pl.BlockSpec((N_DEV, tr, cdim), lambda i: (0, i, 0)) if partials else blk
    res, carried = _call(body, name=name, grid=(r // tr,), in_specs=[blk, blk, blk, g_spec], out_specs=[blk] * 4,
                         out_shape=[jax.ShapeDtypeStruct((r, cdim), f32)] * 4, scratch_shapes=[], sem=("parallel",),
                         args=(w, m, v, g), carry=carry)
    return res if carry is None else (res, carried)


def _sum_sources(parts, name):
    def body(*refs):
        for i_ref, o_ref in zip(refs[:len(parts)], refs[len(parts):]):
            acc = i_ref[0]
            for s in range(1, i_ref.shape[0]):
                acc = acc + i_ref[s]
            o_ref[...] = acc

    return list(_pc(body, name=name, out_shape=[jax.ShapeDtypeStruct(p.shape[1:], f32) for p in parts],
                    compiler_params=_params())(*parts))


def kernel(x, c, ctx, c_ctx, w_mod, b_mod, g_ffn1, w1_gu, w1_down, g_mix, w_in, dw_weight, dw_bias, conv_ln_g, conv_ln_b, w_conv_out, w_alpha_f, b_alpha_f, w_alpha_b, b_alpha_b, gla_norm_g, w_gla_out, w_out, g_ffn2, w2_gu, w2_down, g_final, loss_target, m_c_ctx, m_w_mod, m_b_mod, m_g_ffn1, m_w1_gu, m_w1_down, m_g_mix, m_w_in, m_dw_weight, m_dw_bias, m_conv_ln_g, m_conv_ln_b, m_w_conv_out, m_w_alpha_f, m_b_alpha_f, m_w_alpha_b, m_b_alpha_b, m_gla_norm_g, m_w_gla_out, m_w_out, m_g_ffn2, m_w2_gu, m_w2_down, m_g_final, v_c_ctx, v_w_mod, v_b_mod, v_g_ffn1, v_w1_gu, v_w1_down, v_g_mix, v_w_in, v_dw_weight, v_dw_bias, v_conv_ln_g, v_conv_ln_b, v_w_conv_out, v_w_alpha_f, v_b_alpha_f, v_w_alpha_b, v_b_alpha_b, v_gla_norm_g, v_w_gla_out, v_w_out, v_g_ffn2, v_w2_gu, v_w2_down, v_g_final):
    B, L, D = x.shape
    Lc = ctx.shape[1]
    T, Tc = B * L, B * Lc
    Tall = T + Tc
    F = w1_down.shape[1] * N_DEV
    DK, DV = D // (2 * HEADS), D // HEADS
    QK = HEADS * DK
    PW = 7 * D + LR_PAD
    tm = ROW_TILE
    tpe = L // tm
    nx, nall = T // tm, Tall // tm
    me = 4 * lax.axis_index("x") + 2 * lax.axis_index("y") + lax.axis_index("c")

    rw_all = dict(tm=tm, n_tiles=nall, tpe=tpe, nx_tiles=nx, n_ex=B + 1)
    rw_x = dict(tm=tm, n_tiles=nx, tpe=tpe, nx_tiles=nx, n_ex=B)
    rw_all2, rw_x2 = rw_all, rw_x

    dww_g, waf_g, wab_g, c_g = _exchange([dw_weight[0], w_alpha_f[0], w_alpha_b[0], c], False, "gather_first")

    def cols(gat):
        return jnp.transpose(gat, (1, 0, 2)).reshape(gat.shape[1], N_DEV * gat.shape[2])

    def rows_(gat):
        return gat.reshape(N_DEV * gat.shape[1], gat.shape[2])

    dww = cols(dww_g)
    WA = jnp.zeros((LR_PAD, 2 * QK), f32).at[:LOWRANK, :QK].set(cols(waf_g)).at[LOWRANK:2 * LOWRANK, QK:].set(cols(wab_g)).astype(bf16)
    BA = jnp.concatenate([b_alpha_f, b_alpha_b], axis=1)
    c_all = c_g.reshape(N_DEV * B, D)
    c_ctx2 = c_ctx.reshape(1, D)

    ncm = w_mod.shape[2]
    b_mod_loc = lax.dynamic_slice(b_mod, (0, me * ncm), (1, ncm))
    mod_loc = _mod_fwd(c_all, c_ctx2, w_mod[0], b_mod_loc, "mod_fwd")
    (mod_g,) = _exchange([mod_loc], False, "gather_mod")
    mod_full = cols(mod_g)
    mod_tab = jnp.concatenate([lax.dynamic_slice(mod_full, (me * B, 0), (B, N_MOD * D)), mod_full[N_DEV * B:N_DEV * B + 1]], axis=0)
    mods = [mod_tab[:, i * D:(i + 1) * D].reshape(B + 1, 1, D) for i in range(N_MOD)]
    mods_x = [mm[:B] for mm in mods]

    x_all = jnp.concatenate([x.reshape(T, D), ctx.reshape(Tc, D)], axis=0)

    def f_ffn_in(tok, ex, sh):
        return [_rms_mod(tok[0], sh[0], ex[0], ex[1])], [], []

    (u1,), (w1gu_g,) = _rowwise(f_ffn_in, name="ffn1_in", tok_in=[(x_all, D, 0, False)], ex_in=[mods[0], mods[1]], sh_in=[g_ffn1],
                                tok_out=[(D, bf16)], carry=([w1_gu[0].astype(bf16)], False), **rw_all)
    W1gu = cols(w1gu_g)
    (gu1, h1), (w1d_g, win_g) = _ffn_up(u1, W1gu, "ffn1_up", carry=([w1_down[0].astype(bf16), w_in[0].astype(bf16)], False))
    W1d = rows_(w1d_g)
    lr2 = 2 * LOWRANK
    segs = [(0, 2 * D, 0), (2 * D, 2 * D + QK, 6 * D), (2 * D + QK, 3 * D, 6 * D + QK), (3 * D, 4 * D, 2 * D), (4 * D, 5 * D, 3 * D),
            (5 * D, 5 * D + lr2, 7 * D), (5 * D + lr2, 6 * D + lr2, 4 * D), (6 * D + lr2, 7 * D + lr2, 5 * D)]
    wc = w_in.shape[2]
    win_parts = []
    for lo, hi, _ in sorted(segs, key=lambda t: t[2]):
        for d in range(N_DEV):
            a0, a1 = max(lo, d * wc), min(hi, (d + 1) * wc)
            if a0 < a1:
                win_parts.append(win_g[d][:, a0 - d * wc:a1 - d * wc])
    Win = jnp.concatenate(win_parts + [jnp.zeros((D, LR_PAD - lr2), bf16)], axis=1)

    def nn(a, w):
        return jnp.dot(a.astype(bf16), w, preferred_element_type=f32)

    def nt(a, w):
        return lax.dot_general(a.astype(bf16), w, (((1,), (1,)), ((), ())), preferred_element_type=f32)

    def mix_in(xv, fv, gate, sh, sc, g):
        x1 = xv + 0.5 * gate * fv
        return x1, _rms_mod(x1, g, sh, sc)

    def f_mix_in(tok, ex, sh):
        f1v = nn(tok[1], sh[1])
        return list(mix_in(tok[0], f1v, ex[0], ex[1], ex[2], sh[0])) + [f1v], [], []

    x1, um, f1 = _rowwise(f_mix_in, name="ffn1_down_mix_in", tok_in=[(x_all, D, 0, False), (h1, F, 0, False)],
                          ex_in=[mods[2], mods[3], mods[4]], sh_in=[g_mix, W1d], tok_out=[(D, f32), (D, bf16), (D, bf16)], **rw_all2)
    p_all, (wco_g, wgo_g, wo_g, w2gu_g) = _matmul(
        um, Win, "nn", bf16, "in_proj", tm_cap=512, tn_cap=2432,
        carry=([w_conv_out[0].astype(bf16), w_gla_out[0].astype(bf16), w_out[0].astype(bf16), w2_gu[0].astype(bf16)], False))
    Wco, Wgo, Wo, W2gu = rows_(wco_g), rows_(wgo_g), rows_(wo_g), cols(w2gu_g)

    def log_decay(lr, wa, ba):
        z = _bdot(lr, wa, (1, 0)) + ba
        return _log_sigmoid(z) / TAU

    def f_decay(tok, ex, sh):
        return [log_decay(tok[0], sh[0], sh[1])], [], []

    lr_blk = (p_all, LR_PAD, 7 * D // LR_PAD, False)
    (la_all,) = _rowwise(f_decay, name="log_decay", tok_in=[lr_blk], sh_in=[WA, BA], tok_out=[(2 * QK, f32)], **rw_all)

    zeros_s = jnp.zeros((B, HEADS, DV, DK), f32)
    gla_c = dict(row0=T, nb=B, seq=Lc, D=D)
    gla_x = dict(row0=0, nb=B, seq=L, D=D)
    _, hist_cf, s_f = _gla_fwd(p_all, la_all, zeros_s, rev=False, name="gla_ctx_f", **gla_c)
    _, hist_cb, s_b = _gla_fwd(p_all, la_all, zeros_s, rev=True, name="gla_ctx_b", **gla_c)
    (o_f, hist_f, _), (w2d_g,) = _gla_fwd(p_all, la_all, s_f, rev=False, name="gla_x_f", carry=([w2_down[0].astype(bf16)], False), **gla_x)
    W2d = rows_(w2d_g)
    o_b, hist_b, _ = _gla_fwd(p_all, la_all, s_b, rev=True, name="gla_x_b", **gla_x)

    cz = _conv_fwd(p_all, dww, dw_bias, B=B, L=L, D=D, name="conv_fwd")

    def ln_silu(z, g, b):
        mu = jnp.mean(z, axis=-1, keepdims=True)
        var = jnp.mean(jnp.square(z - mu), axis=-1, keepdims=True)
        return jax.nn.silu((z - mu) * lax.rsqrt(var + EPS) * g + b)

    def f_ln(tok, ex, sh):
        zc = ln_silu(tok[0].astype(f32), sh[0], sh[1])
        return [zc, nn(zc, sh[2])], [], []

    zc, yc = _rowwise(f_ln, name="conv_ln_out", tok_in=[(cz, D, 0, False)], sh_in=[conv_ln_g, conv_ln_b, Wco],
                      tok_out=[(D, bf16), (D, bf16)], **rw_x)

    def gla_out(of, ob, og, gn):
        return _head_rms(of.astype(f32) + ob.astype(f32), DV) * gn * jax.nn.silu(og.astype(f32))

    def f_gla_out(tok, ex, sh):
        og2 = gla_out(tok[0], tok[1], tok[2], sh[0])
        return [og2, nn(og2, sh[1])], [], []

    og_blk = (p_all, D, 3, False)
    og2, yg = _rowwise(f_gla_out, name="gla_norm_out", tok_in=[(o_f, D, 0, False), (o_b, D, 0, False), og_blk], sh_in=[gla_norm_g, Wgo],
                       tok_out=[(D, bf16), (D, bf16)], **rw_x)

    def merge(ga, gb, ycv, ygv):
        return jax.nn.sigmoid(ga.astype(f32)) * ycv.astype(f32) + jax.nn.sigmoid(gb.astype(f32)) * ygv.astype(f32)

    def f_merge(tok, ex, sh):
        mg = merge(*tok)
        return [mg, nn(mg, sh[0])], [], []

    ga_blk, gb_blk = (p_all, D, 4, False), (p_all, D, 5, False)
    mg, mix = _rowwise(f_merge, name="merge_mix_out", tok_in=[ga_blk, gb_blk, (yc, D, 0, False), (yg, D, 0, False)], sh_in=[Wo],
                       tok_out=[(D, bf16), (D, f32)], **rw_x)

    def ffn2_in(x1v, mixv, g5, sh, sc, g):
        x2 = x1v + g5 * mixv
        return x2, _rms_mod(x2, g, sh, sc)

    def f_ffn2_in(tok, ex, sh):
        return list(ffn2_in(tok[0], tok[1], ex[0], ex[1], ex[2], sh[0])), [], []

    x2, u2 = _rowwise(f_ffn2_in, name="ffn2_in", tok_in=[(x1, D, 0, False), (mix, D, 0, False)], ex_in=[mods_x[5], mods_x[6], mods_x[7]],
                      sh_in=[g_ffn2], tok_out=[(D, f32), (D, bf16)], **rw_x)
    gu2, h2 = _ffn_up(u2, W2gu, "ffn2_up")

    gf2 = g_final.reshape(1, D)

    def head_loss(x2v, f2v, g8, gf, tgt):
        x3 = x2v + 0.5 * g8 * f2v
        y = x3 * lax.rsqrt(jnp.mean(x3 * x3, axis=-1, keepdims=True) + EPS) * gf
        return 0.5 * jnp.sum(jnp.mean(jnp.square(y - tgt), axis=-1))

    def f_head(tok, ex, sh):
        loss, vjp = jax.vjp(lambda a, b_, c_, d_: head_loss(a, b_, c_, d_, tok[2]), tok[0], nn(tok[1], sh[1]), ex[0], sh[0])
        dx3, df2, dg8, dgf = vjp(jnp.ones((), f32))
        return [dx3, df2], [dg8], [dgf, jnp.broadcast_to(loss.reshape(1, 1), (1, 128))]

    dx3, df2, dg8, dgf, loss_p = _rowwise(
        f_head, name="ffn2_down_head", tok_in=[(x2, D, 0, False), (h2, F, 0, False), (loss_target.reshape(T, D), D, 0, False)],
        ex_in=[mods_x[8]], sh_in=[gf2, W2d], tok_out=[(D, f32), (D, bf16)], ex_out=[D], gl_out=[(1, D), (1, 128)], **rw_x2)

    dgu2 = _ffn_down_dx(df2, W2d, gu2, "ffn2_down_dx")
    gW2d = _matmul(h2, df2, "tn", f32, "ffn2_down_dw", tm_cap=1408)
    du2 = _matmul(dgu2, W2gu, "nt", bf16, "ffn2_up_dx", halves="a")
    gW2gu = _matmul(u2, dgu2, "tn", f32, "ffn2_up_dw", halves="b")

    def f_ffn2_in_bwd(tok, ex, sh):
        _, vjp = jax.vjp(ffn2_in, tok[0], tok[1], ex[0], ex[1], ex[2], sh[0])
        dx2, dmix, dg5, dsh, dsc, dg = vjp((tok[3], tok[2].astype(f32)))
        return [dx2, dmix], [dg5, dsh, dsc], [dg]

    dx2, dmix, dg5, dsh6, dsc7, dg_ffn2 = _rowwise(
        f_ffn2_in_bwd, name="ffn2_in_bwd", tok_in=[(x1, D, 0, False), (mix, D, 0, False), (du2, D, 0, False), (dx3, D, 0, False)],
        ex_in=[mods_x[5], mods_x[6], mods_x[7]], sh_in=[g_ffn2], tok_out=[(D, f32), (D, bf16)], ex_out=[D, D, D], gl_out=[(1, D)], **rw_x)

    gWo = _matmul(mg, dmix, "tn", f32, "mix_out_dw")

    def f_merge_bwd(tok, ex, sh):
        _, vjp = jax.vjp(merge, *[t.astype(f32) for t in tok[:4]])
        dga, dgb, dyc, dyg = vjp(nt(tok[4], sh[0]))
        return [dga, dgb, dyc, dyg], [], []

    dga, dgb, dyc, dyg = _rowwise(f_merge_bwd, name="mix_out_merge_bwd",
                                  tok_in=[ga_blk, gb_blk, (yc, D, 0, False), (yg, D, 0, False), (dmix, D, 0, False)], sh_in=[Wo],
                                  tok_out=[(D, bf16)] * 4, **rw_x)
    gWco = _matmul(zc, dyc, "tn", f32, "conv_out_dw")
    gWgo = _matmul(og2, dyg, "tn", f32, "gla_out_dw")

    def f_ln_bwd(tok, ex, sh):
        _, vjp = jax.vjp(ln_silu, tok[0].astype(f32), sh[0], sh[1])
        dcz, dg, db = vjp(nt(tok[1], sh[2]))
        return [dcz], [], [dg, db, jnp.sum(dcz, axis=0, keepdims=True)]

    dcz, g_ln_g, g_ln_b, g_dwb = _rowwise(f_ln_bwd, name="conv_out_ln_bwd", tok_in=[(cz, D, 0, False), (dyc, D, 0, False)],
                                          sh_in=[conv_ln_g, conv_ln_b, Wco], tok_out=[(D, bf16)], gl_out=[(1, D)] * 3, **rw_x)
    def col_shards(g):
        return jnp.transpose(g.reshape(g.shape[0], N_DEV, g.shape[1] // N_DEV), (1, 0, 2)).astype(bf16)

    def row_shards(g):
        return g.reshape(N_DEV, g.shape[0] // N_DEV, g.shape[1]).astype(bf16)

    (dca, dcb, g_dww), (r_w2d, r_w2gu, r_wo, r_wco, r_wgo) = _conv_bwd(
        p_all, dcz, dww, B=B, L=L, D=D, name="conv_bwd",
        carry=([row_shards(gW2d), col_shards(gW2gu), row_shards(gWo), row_shards(gWco), row_shards(gWgo)], True))

    def f_gla_out_bwd(tok, ex, sh):
        _, vjp = jax.vjp(gla_out, tok[0].astype(f32), tok[1].astype(f32), tok[2].astype(f32), sh[0])
        dof, _, dog, dgn = vjp(nt(tok[3], sh[1]))
        return [dof, dog], [], [dgn]

    d_o, dog, g_gn = _rowwise(f_gla_out_bwd, name="gla_out_norm_bwd",
                              tok_in=[(o_f, D, 0, False), (o_b, D, 0, False), og_blk, (dyg, D, 0, False)], sh_in=[gla_norm_g, Wgo],
                              tok_out=[(D, bf16), (D, bf16)], gl_out=[(1, D)], **rw_x)

    dq_f, dk_f, dv_f, dla_f, ds_f = _gla_bwd(p_all, la_all, hist_f, d_o, zeros_s, rev=False, name="gla_x_f_bwd", **gla_x)
    dq, dk, dv, dla_b, ds_b = _gla_bwd(p_all, la_all, hist_b, d_o, zeros_s, rev=True, name="gla_x_b_bwd", add=(dq_f, dk_f, dv_f), **gla_x)
    dq_cf, dk_cf, dv_cf, dla_cf, _ = _gla_bwd(p_all, la_all, hist_cf, None, ds_f, rev=False, name="gla_ctx_f_bwd", **gla_c)
    _, dk_c, dv_c, dla_cb, _ = _gla_bwd(p_all, la_all, hist_cb, None, ds_b, rev=True, name="gla_ctx_b_bwd", add=(dq_cf, dk_cf, dv_cf), **gla_c)

    dla_all = jnp.concatenate([jnp.concatenate([dla_f, dla_b], axis=1), jnp.concatenate([dla_cf, dla_cb], axis=1)], axis=0)

    def f_decay_bwd(tok, ex, sh):
        _, vjp = jax.vjp(log_decay, tok[0].astype(f32), sh[0].astype(f32), sh[1])
        dlr, dwa, dba = vjp(tok[1])
        return [dlr], [], [dwa, dba]

    dlr, g_WA, g_BA = _rowwise(f_decay_bwd, name="log_decay_bwd", tok_in=[lr_blk, (dla_all, 2 * QK, 0, False)], sh_in=[WA, BA],
                               tok_out=[(LR_PAD, bf16)], gl_out=[(LR_PAD, 2 * QK), (1, 2 * QK)], **rw_all)

    zc_ = functools.partial(jnp.zeros, dtype=bf16)
    dp_x = jnp.concatenate([dca, dcb, dv, dog, dga, dgb, dq, dk, dlr[:T]], axis=1)
    dp_c = jnp.concatenate([zc_((Tc, 2 * D)), dv_c, zc_((Tc, 3 * D)), zc_((Tc, QK)), dk_c, dlr[T:]], axis=1)
    dp_all = jnp.concatenate([dp_x, dp_c], axis=0)
    gWin_p = _matmul(um, dp_all, "tn", f32, "in_proj_dw", tm_cap=512, tn_cap=2432)
    gwin_shards = []
    for d in range(N_DEV):
        parts = []
        for lo, hi, po in segs:
            a0, a1 = max(lo, d * wc), min(hi, (d + 1) * wc)
            if a0 < a1:
                parts.append(gWin_p[:, po + a0 - lo:po + a1 - lo])
        gwin_shards.append(jnp.concatenate(parts, axis=1))
    dum, (r_win,) = _matmul(dp_all, Win, "nt", bf16, "in_proj_dx", tk_cap=2432, carry=([jnp.stack(gwin_shards).astype(bf16)], True))

    def f_mix_in_bwd(tok, ex, sh):
        _, vjp = jax.vjp(mix_in, tok[0], tok[1].astype(f32), ex[0], ex[1], ex[2], sh[0])
        dx1, df1, dgate, dsh, dsc, dg = vjp((tok[3], tok[2].astype(f32)))
        return [dx1, df1], [dgate, dsh, dsc], [dg]

    dx1, df1, dg2, dsh3, dsc4, dg_mix = _rowwise(
        f_mix_in_bwd, name="mix_in_bwd", tok_in=[(x_all, D, 0, False), (f1, D, 0, False), (dum, D, 0, False), (dx2, D, 0, True)],
        ex_in=[mods[2], mods[3], mods[4]], sh_in=[g_mix], tok_out=[(D, f32), (D, bf16)], ex_out=[D, D, D], gl_out=[(1, D)], **rw_all)

    dgu1 = _ffn_down_dx(df1, W1d, gu1, "ffn1_down_dx")
    gW1d = _matmul(h1, df1, "tn", f32, "ffn1_down_dw", tm_cap=1408)
    gW1gu, (r_w1d,) = _matmul(u1, dgu1, "tn", f32, "ffn1_up_dw", carry=([row_shards(gW1d)], True), halves="b")
    du1, (r_w1gu,) = _matmul(dgu1, W1gu, "nt", bf16, "ffn1_up_dx", carry=([col_shards(gW1gu)], True), halves="a")

    def f_ffn_in_bwd(tok, ex, sh):
        _, vjp = jax.vjp(_rms_mod, tok[0], sh[0], ex[0], ex[1])
        dx, dg, dsh, dsc = vjp(tok[1].astype(f32))
        return [dx + tok[2]], [dsh, dsc], [dg]

    dx_all, dsh0, dsc1, dg_ffn1 = _rowwise(
        f_ffn_in_bwd, name="ffn1_in_bwd", tok_in=[(x_all, D, 0, False), (du1, D, 0, False), (dx1, D, 0, False)],
        ex_in=[mods[0], mods[1]], sh_in=[g_ffn1], tok_out=[(D, f32)], ex_out=[D, D], gl_out=[(1, D)], **rw_all)
    grad_x = dx_all[:T].reshape(B, L, D)

    zrow = jnp.zeros((1, 1, D), f32)
    dmod_loc = jnp.concatenate([dsh0, dsc1, dg2, dsh3, dsc4] + [jnp.concatenate([t, zrow], axis=0) for t in (dg5, dsh6, dsc7, dg8)],
                               axis=2).reshape(B + 1, N_MOD * D)
    rows16 = jnp.concatenate([jnp.concatenate([loss_p, jnp.zeros((1, D - loss_p.shape[1]), f32)], axis=1), dg_ffn1, dg_mix, g_dwb, g_ln_g,
                              g_ln_b, g_BA, g_gn, dg_ffn2, dgf, jnp.zeros((6, D), f32)], axis=0)

    def to8(v):
        n_pad = -(-v.shape[1] // 1024) * 1024
        return jnp.pad(v, ((0, 0), (0, n_pad - v.shape[1]))).reshape(8, n_pad // 8)

    def from8(a, n):
        return a.reshape(1, a.size)[:, :n]

    def adam_big(nm, wv, mv, vv, part, carry=None):
        out = _adamw(wv[0], mv[0], vv[0], part, "adamw_" + nm, True, carry=carry)
        res4, carried = out if carry is not None else (out, None)
        return [t[None] for t in res4], carried

    rs_out = dict(w1_gu=r_w1gu, w1_down=r_w1d, w_in=r_win, w_conv_out=r_wco, w_gla_out=r_wgo, w_out=r_wo, w2_gu=r_w2gu, w2_down=r_w2d)
    big = {}
    big["w_in"], (dmod_g, rows_g, dww_sg, wa_sg) = adam_big(
        "w_in", w_in, m_w_in, v_w_in, rs_out["w_in"], carry=([dmod_loc, rows16, g_dww, g_WA[:2 * LOWRANK]], False))
    dmx = dmod_g[:, :B].reshape(N_DEV * B, N_MOD * D)
    dmc = dmod_g[:, B]
    gWmod, gcc_p = _mod_bwd(c_all, c_ctx2, w_mod[0], lax.dynamic_slice(dmx, (0, me * ncm), (N_DEV * B, ncm)),
                            lax.dynamic_slice(dmc, (0, me * ncm), (N_DEV, ncm)), "mod_bwd")

    big["w1_gu"], (gcc_g,) = adam_big("w1_gu", w1_gu, m_w1_gu, v_w1_gu, rs_out["w1_gu"], carry=([to8(gcc_p)], False))
    for nm, wv, mv, vv in (("w1_down", w1_down, m_w1_down, v_w1_down), ("w_conv_out", w_conv_out, m_w_conv_out, v_w_conv_out),
                           ("w_gla_out", w_gla_out, m_w_gla_out, v_w_gla_out), ("w_out", w_out, m_w_out, v_w_out),
                           ("w2_gu", w2_gu, m_w2_gu, v_w2_gu), ("w2_down", w2_down, m_w2_down, v_w2_down)):
        big[nm], _ = adam_big(nm, wv, mv, vv, rs_out[nm])
    big["w_mod"] = [t[None] for t in _adamw(w_mod[0], m_w_mod[0], v_w_mod[0], gWmod, "adamw_w_mod", False)]

    rows_s, dww_s, wa_s, g_cc, g_bmod = _sum_sources(
        [rows_g, dww_sg, wa_sg, gcc_g, jnp.concatenate([dmx, dmc], axis=0).reshape(N_DEV * (B + 1), 8, N_MOD * D // 8)], "sum_small")
    g_cc, g_bmod = from8(g_cc, D), from8(g_bmod, N_MOD * D)
    loss = rows_s[0, 0]
    ncd, nca = dw_weight.shape[2], w_alpha_f.shape[2]
    g_dww_loc = lax.dynamic_slice(dww_s, (0, me * ncd), (CONV_W, ncd))
    g_waf_loc = lax.dynamic_slice(wa_s, (0, me * nca), (LOWRANK, nca))
    g_wab_loc = lax.dynamic_slice(wa_s, (LOWRANK, QK + me * nca), (LOWRANK, nca))
    sm = {k: rows_s[i:i + 1] for i, k in enumerate(["loss", "g_ffn1", "g_mix", "dw_bias", "conv_ln_g", "conv_ln_b", "b_alpha", "gla_norm_g",
                                                     "g_ffn2", "g_final"])}

    small_params = [("c_ctx", c_ctx, m_c_ctx, v_c_ctx, g_cc), ("b_mod", b_mod, m_b_mod, v_b_mod, g_bmod),
                    ("g_ffn1", g_ffn1, m_g_ffn1, v_g_ffn1, sm["g_ffn1"]), ("g_mix", g_mix, m_g_mix, v_g_mix, sm["g_mix"]),
                    ("dw_weight", dw_weight, m_dw_weight, v_dw_weight, g_dww_loc), ("dw_bias", dw_bias, m_dw_bias, v_dw_bias, sm["dw_bias"]),
                    ("conv_ln_g", conv_ln_g, m_conv_ln_g, v_conv_ln_g, sm["conv_ln_g"]),
                    ("conv_ln_b", conv_ln_b, m_conv_ln_b, v_conv_ln_b, sm["conv_ln_b"]),
                    ("w_alpha_f", w_alpha_f, m_w_alpha_f, v_w_alpha_f, g_waf_loc), ("b_alpha_f", b_alpha_f, m_b_alpha_f, v_b_alpha_f, sm["b_alpha"][:, :QK]),
                    ("w_alpha_b", w_alpha_b, m_w_alpha_b, v_w_alpha_b, g_wab_loc), ("b_alpha_b", b_alpha_b, m_b_alpha_b, v_b_alpha_b, sm["b_alpha"][:, QK:]),
                    ("gla_norm_g", gla_norm_g, m_gla_norm_g, v_gla_norm_g, sm["gla_norm_g"]),
                    ("g_ffn2", g_ffn2, m_g_ffn2, v_g_ffn2, sm["g_ffn2"]), ("g_final", g_final, m_g_final, v_g_final, sm["g_final"])]

    def two_d(t, like):
        return t.reshape(like.shape[1:]) if like.ndim == 3 else t.reshape(like.size // 128, 128)

    small_res = _adamw_many([tuple(two_d(t, wv) for t in (wv, mv, vv, gv)) for _, wv, mv, vv, gv in small_params], "adamw_small")
    small_out = {nm: [gv.reshape(wv.shape)] + [t.reshape(wv.shape) for t in r3]
                 for (nm, wv, _, _, gv), r3 in zip(small_params, small_res)}

    order = ["c_ctx", "w_mod", "b_mod", "g_ffn1", "w1_gu", "w1_down", "g_mix", "w_in", "dw_weight", "dw_bias", "conv_ln_g", "conv_ln_b",
             "w_conv_out", "w_alpha_f", "b_alpha_f", "w_alpha_b", "b_alpha_b", "gla_norm_g", "w_gla_out", "w_out", "g_ffn2", "w2_gu",
             "w2_down", "g_final"]
    res = {**big, **small_out}
    return (loss, grad_x, *[res[n][0] for n in order], *[res[n][1] for n in order], *[res[n][2] for n in order], *[res[n][3] for n in order])
```

```python
import functools

import jax
import jax.numpy as jnp
from jax import lax
from jax.experimental import pallas as pl
from jax.experimental.pallas import tpu as pltpu

f32, bf16 = jnp.float32, jnp.bfloat16

N_DEV = 8
HEADS = 4
LOWRANK = 16
CONV_W = 31
CONV_PAD = 16
SUBLANES = 8
CHUNK = 64
SUB = 16
GLA_ROWS = 256
GLA_SAFE_DECAY = 60.0
TAU = 16.0
EPS = 1e-6
N_MOD = 9
LR_PAD = 128
ROW_TILE = 512
V7X_VMEM_BYTES = 64 << 20
VMEM_LIMIT = (V7X_VMEM_BYTES * 3) // 4

ADAM_LR, ADAM_B1, ADAM_B2, ADAM_EPS, ADAM_WD, ADAM_STEP = 0.001, 0.9, 0.999, 1e-08, 0.01, 10

MESH = pl.DeviceIdType.MESH


def _pc(body, **kw):
    return pl.pallas_call(body, **kw)


def _params(*sem):
    return pltpu.CompilerParams(dimension_semantics=sem, vmem_limit_bytes=VMEM_LIMIT)


def _pick(n, cap, unit=128):
    best = None
    for t in range(unit, min(n, cap) + 1, unit):
        if n % t == 0:
            best = t
    return best or n


def _matmul(a, b, mode, out_dtype, name, tm_cap=1024, tn_cap=1536, tk_cap=None, carry=None, halves=None):
    tk_cap = tk_cap or (2048 if mode == "tn" else 2816)
    if halves == "a":
        (_, M, Kh), N = a.shape, b.shape[0]
        K, tk = 2 * Kh, _pick(Kh, tk_cap)
        tm, tn = _pick(M, tm_cap), _pick(N, tn_cap)
        a_spec = pl.BlockSpec((None, tm, tk), lambda i, j, k: (k // (Kh // tk), i, k % (Kh // tk)))
    elif halves == "b":
        (K, M), (_, _, Nh) = a.shape, b.shape
        N, tn = 2 * Nh, _pick(Nh, tn_cap)
        tm, tk = _pick(M, tm_cap), _pick(K, tk_cap)
    else:
        if mode == "tn":
            (K, M), N = a.shape, b.shape[1]
        elif mode == "nt":
            (M, K), N = a.shape, b.shape[0]
        else:
            (M, K), N = a.shape, b.shape[1]
        tm, tn, tk = _pick(M, tm_cap), _pick(N, tn_cap), _pick(K, tk_cap)
    nk = K // tk
    if halves != "a":
        a_spec = pl.BlockSpec((tk, tm), lambda i, j, k: (k, i)) if mode == "tn" else pl.BlockSpec((tm, tk), lambda i, j, k: (i, k))
    if halves == "b":
        b_spec = pl.BlockSpec((None, tk, tn), lambda i, j, k: (j // (Nh // tn), k, j % (Nh // tn)))
    else:
        b_spec = pl.BlockSpec((tn, tk), lambda i, j, k: (j, k)) if mode == "nt" else pl.BlockSpec((tk, tn), lambda i, j, k: (k, j))
    dims = {"nn": ((1,), (0,)), "nt": ((1,), (1,)), "tn": ((0,), (0,))}[mode]

    def body_single(a_ref, b_ref, o_ref):
        o_ref[...] = lax.dot_general(a_ref[...].astype(bf16), b_ref[...].astype(bf16), (dims, ((), ())),
                                     preferred_element_type=f32).astype(out_dtype)

    def body(a_ref, b_ref, o_ref, acc_ref):
        k = pl.program_id(2)
        part = lax.dot_general(a_ref[...].astype(bf16), b_ref[...].astype(bf16), (dims, ((), ())), preferred_element_type=f32)

        @pl.when(k == 0)
        def _():
            acc_ref[...] = part

        @pl.when(k > 0)
        def _():
            acc_ref[...] += part

        @pl.when(k == nk - 1)
        def _():
            o_ref[...] = acc_ref[...].astype(out_dtype)

    (out,), carried = _call(
        body_single if nk == 1 else body, name=name, grid=(M // tm, N // tn, nk), in_specs=[a_spec, b_spec],
        out_specs=[pl.BlockSpec((tm, tn), lambda i, j, k: (i, j))], out_shape=[jax.ShapeDtypeStruct((M, N), out_dtype)],
        scratch_shapes=[] if nk == 1 else [pltpu.VMEM((tm, tn), f32)], sem=("parallel", "parallel", "arbitrary"),
        args=(a, b), carry=carry)
    return out if carry is None else (out, carried)


def _ffn_up(u, Wgu, name, carry=None):
    M, K = u.shape
    F = Wgu.shape[1] // 2
    tm, tn = _pick(M, 512), _pick(F, 1408)
    nj = F // tn

    def body(u_ref, wa_ref, wb_ref, gu_ref, h_ref):
        uv = u_ref[...]
        a = jnp.dot(uv, wa_ref[...], preferred_element_type=f32)
        b = jnp.dot(uv, wb_ref[...], preferred_element_type=f32)
        gu_ref[0] = a.astype(bf16)
        gu_ref[1] = b.astype(bf16)
        h_ref[...] = (jax.nn.silu(a) * b).astype(bf16)

    res, carried = _call(
        body, name=name, grid=(nj, M // tm),
        in_specs=[pl.BlockSpec((tm, K), lambda j, i: (i, 0)), pl.BlockSpec((K, tn), lambda j, i: (0, j)),
                  pl.BlockSpec((K, tn), lambda j, i: (0, nj + j))],
        out_specs=[pl.BlockSpec((2, tm, tn), lambda j, i: (0, i, j)), pl.BlockSpec((tm, tn), lambda j, i: (i, j))],
        out_shape=[jax.ShapeDtypeStruct((2, M, F), bf16), jax.ShapeDtypeStruct((M, F), bf16)],
        scratch_shapes=[], sem=("parallel", "parallel"), args=(u, Wgu, Wgu), carry=carry)
    return res if carry is None else (res, carried)


def _ffn_down_dx(df, Wd, gu, name):
    M, D = df.shape
    F = Wd.shape[0]
    tm, tn = _pick(M, 512), _pick(F, 1408)

    def body(df_ref, w_ref, gu_ref, o_ref):
        dh = lax.dot_general(df_ref[...], w_ref[...], (((1,), (1,)), ((), ())), preferred_element_type=f32)
        a, b = gu_ref[0].astype(f32), gu_ref[1].astype(f32)
        sg = jax.nn.sigmoid(a)
        o_ref[0] = (dh * b * sg * (1.0 + a * (1.0 - sg))).astype(bf16)
        o_ref[1] = (dh * a * sg).astype(bf16)

    return _pc(
        body, name=name, grid=(F // tn, M // tm),
        in_specs=[pl.BlockSpec((tm, D), lambda j, i: (i, 0)), pl.BlockSpec((tn, D), lambda j, i: (j, 0)),
                  pl.BlockSpec((2, tm, tn), lambda j, i: (0, i, j))],
        out_specs=pl.BlockSpec((2, tm, tn), lambda j, i: (0, i, j)), out_shape=jax.ShapeDtypeStruct((2, M, F), bf16),
        compiler_params=_params("parallel", "parallel"))(df, Wd, gu)


def _rowwise(fn, *, name, tm, n_tiles, tpe, nx_tiles, n_ex, tok_in=(), ex_in=(), sh_in=(), tok_out=(), ex_out=(), gl_out=(), carry=None):
    def seg(i):
        return jnp.minimum(i // tpe, n_ex - 1)

    in_specs, args = [], []
    for arr, w, cb, x_only in tok_in:
        if x_only == "c":
            in_specs.append(pl.BlockSpec((tm, w), functools.partial(lambda i, cb: (jnp.maximum(i - nx_tiles, 0), cb), cb=cb)))
        elif x_only:
            in_specs.append(pl.BlockSpec((tm, w), functools.partial(lambda i, cb: (jnp.minimum(i, nx_tiles - 1), cb), cb=cb)))
        else:
            in_specs.append(pl.BlockSpec((tm, w), functools.partial(lambda i, cb: (i, cb), cb=cb)))
        args.append(arr)
    for arr in ex_in:
        in_specs.append(pl.BlockSpec((1, 1, arr.shape[-1]), lambda i: (seg(i), 0, 0)))
        args.append(arr)
    for arr in sh_in:
        in_specs.append(pl.BlockSpec(arr.shape, functools.partial(lambda i, nd: (0,) * nd, nd=arr.ndim)))
        args.append(arr)
    out_specs, out_shape = [], []
    for w, dt, *x_rows in tok_out:
        if x_rows:
            out_specs.append(pl.BlockSpec((tm, w), lambda i: (jnp.minimum(i, nx_tiles - 1), 0)))
        else:
            out_specs.append(pl.BlockSpec((tm, w), lambda i: (i, 0)))
        out_shape.append(jax.ShapeDtypeStruct(((nx_tiles if x_rows else n_tiles) * tm, w), dt))
    for w in ex_out:
        out_specs.append(pl.BlockSpec((1, 1, w), lambda i: (seg(i), 0, 0)))
        out_shape.append(jax.ShapeDtypeStruct((n_ex, 1, w), f32))
    for r, w in gl_out:
        out_specs.append(pl.BlockSpec((r, w), lambda i: (0, 0)))
        out_shape.append(jax.ShapeDtypeStruct((r, w), f32))
    n_tok, n_exi, n_sh = len(tok_in), len(ex_in), len(sh_in)
    n_to, n_eo = len(tok_out), len(ex_out)
    x_only_flags = [t[3] for t in tok_in]
    x_rows_flags = [len(t) > 2 for t in tok_out]

    def body(*refs):
        i = pl.program_id(0)
        ins, outs = refs[: n_tok + n_exi + n_sh], refs[n_tok + n_exi + n_sh:]
        is_x = i < nx_tiles
        tok_vals = []
        for r, xo in zip(ins[:n_tok], x_only_flags):
            v = r[...]
            if xo == "c":
                v = jnp.where(is_x, jnp.zeros_like(v), v)
            elif xo:
                v = jnp.where(is_x, v, jnp.zeros_like(v))
            tok_vals.append(v)
        ex_vals = [r[0] for r in ins[n_tok:n_tok + n_exi]]
        sh_vals = [r[...] for r in ins[n_tok + n_exi:]]
        t_o, e_o, g_o = fn(tok_vals, ex_vals, sh_vals)
        for r, v, xr in zip(outs[:n_to], t_o, x_rows_flags):
            if xr:
                @pl.when(is_x)
                def _(r=r, v=v):
                    r[...] = v.astype(r.dtype)
            else:
                r[...] = v.astype(r.dtype)
        first = jnp.logical_and(i % tpe == 0, i <= nx_tiles)
        for r, v in zip(outs[n_to:n_to + n_eo], e_o):
            @pl.when(first)
            def _(r=r, v=v):
                r[0] = v

            @pl.when(jnp.logical_not(first))
            def _(r=r, v=v):
                r[0] += v
        for r, v in zip(outs[n_to + n_eo:], g_o):
            @pl.when(i == 0)
            def _(r=r, v=v):
                r[...] = v

            @pl.when(i > 0)
            def _(r=r, v=v):
                r[...] += v

    res, carried = _call(body, name=name, grid=(n_tiles,), in_specs=in_specs, out_specs=out_specs, out_shape=out_shape,
                         scratch_shapes=[], sem=("arbitrary",), args=args, carry=carry)
    return res if carry is None else (res, carried)


def _rms_mod(x, g, sh, sc):
    y = x * lax.rsqrt(jnp.mean(x * x, axis=-1, keepdims=True) + EPS) * g
    return y * (1.0 + sc) + sh


def _log_sigmoid(z):
    return jnp.minimum(z, 0.0) - jnp.log(1.0 + jnp.exp(-jnp.abs(z)))


def _head_rms(o, DV):
    parts = []
    for h in range(HEADS):
        oh = o[:, h * DV:(h + 1) * DV]
        parts.append(oh * lax.rsqrt(jnp.mean(oh * oh, axis=-1, keepdims=True) + EPS))
    return jnp.concatenate(parts, axis=1)


@functools.partial(jax.custom_vjp, nondiff_argnums=(2,))
def _bdot(a, b, dims):
    return lax.dot_general(a.astype(bf16), b.astype(bf16), (((dims[0],), (dims[1],)), ((), ())), preferred_element_type=f32)


def _bdot_fwd(a, b, dims):
    return _bdot(a, b, dims), (a, b)


def _bdot_bwd(dims, res, g):
    a, b = res
    ca, cb = dims
    da = _bdot(g, b, (1, 1 - cb)) if ca == 1 else _bdot(b, g, (1 - cb, 1))
    db = _bdot(a, g, (1 - ca, 0)) if cb == 0 else _bdot(g, a, (0, 1 - ca))
    return da, db


_bdot.defvjp(_bdot_fwd, _bdot_bwd)


def _split_dot(m, x, dims):
    mb, rem, acc = m.astype(bf16), x, None
    for _ in range(3):
        piece = rem.astype(bf16)
        rem = rem - piece.astype(f32)
        part = lax.dot_general(mb, piece, (((dims[0],), (dims[1],)), ((), ())), preferred_element_type=f32)
        acc = part if acc is None else acc + part
    return acc


@jax.custom_vjp
def _tri_cumsum(tri, g):
    return _split_dot(tri, g, (1, 0))


def _tri_cumsum_fwd(tri, g):
    return _tri_cumsum(tri, g), tri


def _tri_cumsum_bwd(tri, db):
    return jnp.zeros_like(tri), _split_dot(tri, db, (0, 0))


_tri_cumsum.defvjp(_tri_cumsum_fwd, _tri_cumsum_bwd)


def _gla_chunk(St, q, k, v, g, *, rev, scale, exact):
    C, DK = q.shape
    r = lax.broadcasted_iota(jnp.int32, (C, C), 0)
    c = lax.broadcasted_iota(jnp.int32, (C, C), 1)
    causal = (r <= c) if rev else (r >= c)
    b = _tri_cumsum(causal.astype(f32), g)
    qs = q * scale
    qe = qs * jnp.exp(b)
    inter = _bdot(qe, St, (1, 1))
    b_last = b[0:1] if rev else b[C - 1:C]
    kd = k * jnp.exp(b_last - b)
    St_new = St * jnp.exp(b_last) + _bdot(v, kd, (0, 0))
    if not exact:
        att = jnp.where(causal, _bdot(qe, k * jnp.exp(-b), (1, 1)), 0.0)
        return St_new, inter + _bdot(att, v, (1, 0))
    rr = lax.broadcasted_iota(jnp.int32, (SUB, SUB, DK), 0)
    cc = lax.broadcasted_iota(jnp.int32, (SUB, SUB, DK), 1)
    m3 = (rr <= cc) if rev else (rr >= cc)
    outs = []
    for i in range(C // SUB):
        lo, hi = i * SUB, (i + 1) * SUB
        bi, qi, ki, vi = b[lo:hi], qs[lo:hi], k[lo:hi], v[lo:hi]
        rel = bi[:, None, :] - bi[None, :, :]
        e = jnp.where(m3, jnp.exp(jnp.where(m3, rel, 0.0)), 0.0)
        att = jnp.sum(qi[:, None, :] * e * ki[None, :, :], axis=-1)
        acc = _bdot(att, vi, (1, 0))
        ref_row = b[hi - 1:hi] if rev else b[lo:lo + 1]
        prev = slice(hi, C) if rev else slice(0, lo)
        if (hi < C) if rev else (lo > 0):
            qn = qi * jnp.exp(bi - ref_row)
            ks = k[prev] * jnp.exp(ref_row - b[prev])
            acc = acc + _bdot(_bdot(qn, ks, (1, 1)), v[prev], (1, 0))
        outs.append(acc)
    return St_new, inter + jnp.concatenate(outs, axis=0)


def _mild_decay(la_ref):
    return jnp.min(la_ref[...]) >= -GLA_SAFE_DECAY / CHUNK


def _gla_specs(D, rev_blocks, row0, seq):
    DK, DV = D // (2 * HEADS), D // HEADS
    nblk = seq // GLA_ROWS
    rb0 = row0 // GLA_ROWS

    def blk(j):
        return (nblk - 1 - j) if rev_blocks else j

    return DK, DV, nblk, rb0, blk


def _gla_in_specs(D, rev, rows):
    QK = D // 2
    return [
        pl.BlockSpec((GLA_ROWS, QK), lambda b, j: (rows(b, j), 6 * D // QK)),
        pl.BlockSpec((GLA_ROWS, QK), lambda b, j: (rows(b, j), 6 * D // QK + 1)),
        pl.BlockSpec((GLA_ROWS, D), lambda b, j: (rows(b, j), 2)),
        pl.BlockSpec((GLA_ROWS, QK), lambda b, j: (rows(b, j), 1 if rev else 0)),
    ]


def _gla_fwd(p_all, la_all, s0, *, rev, row0, nb, seq, D, name, carry=None):
    DK, DV, nblk, rb0, blk = _gla_specs(D, rev, row0, seq)
    cpb = GLA_ROWS // CHUNK

    def rows(b, j):
        return rb0 + b * nblk + blk(j)

    in_specs = _gla_in_specs(D, rev, rows) + [pl.BlockSpec((1, HEADS, DV, DK), lambda b, j: (b, 0, 0, 0))]
    out_specs = [
        pl.BlockSpec((GLA_ROWS, D), lambda b, j: (b * nblk + blk(j), 0)),
        pl.BlockSpec((1, HEADS, cpb, DV, DK), lambda b, j: (b, 0, blk(j), 0, 0)),
        pl.BlockSpec((1, HEADS, DV, DK), lambda b, j: (b, 0, 0, 0)),
    ]
    out_shape = [
        jax.ShapeDtypeStruct((nb * seq, D), bf16),
        jax.ShapeDtypeStruct((nb, HEADS, seq // CHUNK, DV, DK), bf16),
        jax.ShapeDtypeStruct((nb, HEADS, DV, DK), f32),
    ]
    chunk = functools.partial(_gla_chunk, rev=rev, scale=DK ** -0.5)

    def body(q_ref, k_ref, v_ref, la_ref, s0_ref, o_ref, hist_ref, sfin_ref, st_ref):
        j = pl.program_id(1)

        @pl.when(j == 0)
        def _():
            st_ref[...] = s0_ref[0]

        def step(ci, exact):
            cc = (cpb - 1 - ci) if rev else ci
            sl = pl.ds(cc * CHUNK, CHUNK)
            for h in range(HEADS):
                kq, kv = pl.ds(h * DK, DK), pl.ds(h * DV, DV)
                St = st_ref[h]
                hist_ref[0, h, cc] = St.astype(bf16)
                St2, o = chunk(St, q_ref[sl, kq].astype(f32), k_ref[sl, kq].astype(f32), v_ref[sl, kv].astype(f32), la_ref[sl, kq],
                               exact=exact)
                o_ref[sl, kv] = o.astype(bf16)
                st_ref[h] = St2

        mild = _mild_decay(la_ref)
        for exact in (False, True):
            @pl.when(jnp.logical_not(mild) if exact else mild)
            def _(exact=exact):
                for ci in range(cpb):
                    step(ci, exact)

        @pl.when(j == nblk - 1)
        def _():
            sfin_ref[0] = st_ref[...]

    res, carried = _call(body, name=name, grid=(nb, nblk), in_specs=in_specs, out_specs=out_specs, out_shape=out_shape,
                         scratch_shapes=[pltpu.VMEM((HEADS, DV, DK), f32)], sem=("parallel", "arbitrary"),
                         args=(p_all, p_all, p_all, la_all, s0), carry=carry)
    return res if carry is None else (res, carried)


def _gla_bwd(p_all, la_all, hist, do, dsfin, *, rev, row0, nb, seq, D, name, add=None):
    DK, DV, nblk, rb0, blk = _gla_specs(D, not rev, row0, seq)
    cpb = GLA_ROWS // CHUNK
    QK = HEADS * DK
    has_do = do is not None

    def rows(b, j):
        return rb0 + b * nblk + blk(j)

    in_specs = _gla_in_specs(D, rev, rows) + [
        pl.BlockSpec((1, HEADS, cpb, DV, DK), lambda b, j: (b, 0, blk(j), 0, 0)),
        pl.BlockSpec((1, HEADS, DV, DK), lambda b, j: (b, 0, 0, 0)),
    ]
    args = [p_all, p_all, p_all, la_all, hist, dsfin]
    if has_do:
        in_specs.append(pl.BlockSpec((GLA_ROWS, D), lambda b, j: (b * nblk + blk(j), 0)))
        args.append(do)
    if add is not None:
        in_specs += [pl.BlockSpec((GLA_ROWS, t.shape[1]), lambda b, j: (b * nblk + blk(j), 0)) for t in add]
        args += list(add)
    gdt = f32 if add is None else bf16
    out_specs = [
        pl.BlockSpec((GLA_ROWS, QK), lambda b, j: (b * nblk + blk(j), 0)),
        pl.BlockSpec((GLA_ROWS, QK), lambda b, j: (b * nblk + blk(j), 0)),
        pl.BlockSpec((GLA_ROWS, D), lambda b, j: (b * nblk + blk(j), 0)),
        pl.BlockSpec((GLA_ROWS, QK), lambda b, j: (b * nblk + blk(j), 0)),
        pl.BlockSpec((1, HEADS, DV, DK), lambda b, j: (b, 0, 0, 0)),
    ]
    out_shape = [
        jax.ShapeDtypeStruct((nb * seq, QK), gdt), jax.ShapeDtypeStruct((nb * seq, QK), gdt),
        jax.ShapeDtypeStruct((nb * seq, D), gdt), jax.ShapeDtypeStruct((nb * seq, QK), f32),
        jax.ShapeDtypeStruct((nb, HEADS, DV, DK), f32),
    ]
    chunk = functools.partial(_gla_chunk, rev=rev, scale=DK ** -0.5)

    def body(*refs):
        refs = list(refs)
        q_ref, k_ref, v_ref, la_ref, hist_ref, dsfin_ref = refs[:6]
        do_ref = refs[6] if has_do else None
        add_refs = refs[6 + has_do:len(refs) - 6]
        dq_ref, dk_ref, dv_ref, dla_ref, ds0_ref, ds_ref = refs[len(refs) - 6:]
        j = pl.program_id(1)

        @pl.when(j == 0)
        def _():
            ds_ref[...] = dsfin_ref[0]

        def step(ci, exact):
            cc = ci if rev else (cpb - 1 - ci)
            sl = pl.ds(cc * CHUNK, CHUNK)
            for h in range(HEADS):
                kq, kv = pl.ds(h * DK, DK), pl.ds(h * DV, DV)
                prim = (hist_ref[0, h, cc].astype(f32), q_ref[sl, kq].astype(f32), k_ref[sl, kq].astype(f32), v_ref[sl, kv].astype(f32), la_ref[sl, kq])
                _, vjp = jax.vjp(functools.partial(chunk, exact=exact), *prim)
                d_o = do_ref[sl, kv].astype(f32) if has_do else jnp.zeros((CHUNK, DV), f32)
                dSt, dq, dk, dv, dg = vjp((ds_ref[h], d_o))
                if add is not None:
                    dq, dk, dv = dq + add_refs[0][sl, kq], dk + add_refs[1][sl, kq], dv + add_refs[2][sl, kv]
                dq_ref[sl, kq] = dq.astype(gdt)
                dk_ref[sl, kq] = dk.astype(gdt)
                dv_ref[sl, kv] = dv.astype(gdt)
                dla_ref[sl, kq] = dg
                ds_ref[h] = dSt

        mild = _mild_decay(la_ref)
        for exact in (False, True):
            @pl.when(jnp.logical_not(mild) if exact else mild)
            def _(exact=exact):
                for ci in range(cpb):
                    step(ci, exact)

        @pl.when(j == nblk - 1)
        def _():
            ds0_ref[0] = ds_ref[...]

    return _pc(body, name=name, grid=(nb, nblk), in_specs=in_specs, out_specs=out_specs, out_shape=out_shape,
               scratch_shapes=[pltpu.VMEM((HEADS, DV, DK), f32)], compiler_params=_params("parallel", "arbitrary"))(*args)


def _conv_fwd(p_all, dw_w, dw_b, *, B, L, D, name):
    ct = _pick(D, 256)
    nj = D // ct
    st = _pick(L, 128, 8)
    off = CONV_PAD - CONV_W // 2

    def body(a_ref, b_ref, w_ref, bias_ref, o_ref, zs_ref):
        _fill_shifted(zs_ref, L, lambda t0, n: a_ref[pl.ds(t0, n), :].astype(f32) * jax.nn.sigmoid(b_ref[pl.ds(t0, n), :].astype(f32)))
        for t0 in range(0, L, st):
            acc = jnp.zeros((st, ct), f32) + bias_ref[...]
            for k in range(CONV_W):
                acc = acc + w_ref[pl.ds(k, 1), :] * _window(zs_ref, t0 + k + off, st)
            o_ref[pl.ds(t0, st), :] = acc.astype(bf16)

    return _pc(
        body, name=name, grid=(B, nj),
        in_specs=[pl.BlockSpec((L, ct), lambda b, j: (b, j)), pl.BlockSpec((L, ct), lambda b, j: (b, nj + j)),
                  pl.BlockSpec((CONV_W, ct), lambda b, j: (0, j)), pl.BlockSpec((1, ct), lambda b, j: (0, j))],
        out_specs=pl.BlockSpec((L, ct), lambda b, j: (b, j)), out_shape=jax.ShapeDtypeStruct((B * L, D), bf16),
        scratch_shapes=[pltpu.VMEM((SUBLANES, L + 2 * CONV_PAD, ct), f32)], compiler_params=_params("parallel", "parallel"),
    )(p_all, p_all, dw_w, dw_b)


def _fill_shifted(zs_ref, L, rows):
    lp = L + 2 * CONV_PAD
    ct = zs_ref.shape[2]
    step = 256
    zs_ref[0, pl.ds(0, CONV_PAD), :] = jnp.zeros((CONV_PAD, ct), f32)
    zs_ref[0, pl.ds(CONV_PAD + L, CONV_PAD), :] = jnp.zeros((CONV_PAD, ct), f32)
    for t0 in range(0, L, step):
        n = min(step, L - t0)
        zs_ref[0, pl.ds(CONV_PAD + t0, n), :] = rows(t0, n)
    for r in range(1, SUBLANES):
        for i0 in range(0, lp - SUBLANES, step):
            n = min(step, lp - SUBLANES - i0)
            zs_ref[r, pl.ds(i0, n), :] = zs_ref[0, pl.ds(i0 + r, n), :]


def _window(zs_ref, start, n):
    r = start % SUBLANES
    return zs_ref[r, pl.ds(start - r, n), :]


def _conv_bwd(p_all, dcz, dw_w, *, B, L, D, name, carry=None):
    ct = _pick(D, 128)
    nj = D // ct
    st = _pick(L, 256, 8)
    half = CONV_W // 2

    def body(a_ref, b_ref, dcz_ref, w_ref, da_ref, db_ref, ddw_ref, zs_ref, ds_ref):
        bi = pl.program_id(1)
        _fill_shifted(zs_ref, L, lambda t0, n: a_ref[pl.ds(t0, n), :].astype(f32) * jax.nn.sigmoid(b_ref[pl.ds(t0, n), :].astype(f32)))
        _fill_shifted(ds_ref, L, lambda t0, n: dcz_ref[pl.ds(t0, n), :].astype(f32))

        @pl.when(bi == 0)
        def _():
            ddw_ref[...] = jnp.zeros_like(ddw_ref)

        for t0 in range(0, L, st):
            acc = jnp.zeros((st, ct), f32)
            for k in range(CONV_W):
                acc = acc + w_ref[pl.ds(k, 1), :] * _window(ds_ref, t0 + CONV_PAD + half - k, st)
            a_t = a_ref[pl.ds(t0, st), :].astype(f32)
            sg_t = jax.nn.sigmoid(b_ref[pl.ds(t0, st), :].astype(f32))
            da_ref[pl.ds(t0, st), :] = (acc * sg_t).astype(bf16)
            db_ref[pl.ds(t0, st), :] = (acc * a_t * sg_t * (1.0 - sg_t)).astype(bf16)

        parts = [jnp.zeros((SUBLANES, ct), f32) for _ in range(CONV_W)]
        sw = _pick(L, 64, SUBLANES)
        for t0 in range(0, L, sw):
            dout = dcz_ref[pl.ds(t0, sw), :].astype(f32)
            for k in range(CONV_W):
                prod = dout * _window(zs_ref, t0 + k + CONV_PAD - half, sw)
                for i in range(0, sw, SUBLANES):
                    parts[k] = parts[k] + prod[i:i + SUBLANES]
        for k in range(CONV_W):
            ddw_ref[pl.ds(k, 1), :] += jnp.sum(parts[k], axis=0, keepdims=True)

    res, carried = _call(
        body, name=name, grid=(nj, B),
        in_specs=[pl.BlockSpec((L, ct), lambda j, b: (b, j)), pl.BlockSpec((L, ct), lambda j, b: (b, nj + j)),
                  pl.BlockSpec((L, ct), lambda j, b: (b, j)), pl.BlockSpec((CONV_W, ct), lambda j, b: (0, j))],
        out_specs=[pl.BlockSpec((L, ct), lambda j, b: (b, j)), pl.BlockSpec((L, ct), lambda j, b: (b, j)),
                   pl.BlockSpec((2 * CONV_PAD, ct), lambda j, b: (0, j))],
        out_shape=[jax.ShapeDtypeStruct((B * L, D), bf16), jax.ShapeDtypeStruct((B * L, D), bf16),
                   jax.ShapeDtypeStruct((2 * CONV_PAD, D), f32)],
        scratch_shapes=[pltpu.VMEM((SUBLANES, L + 2 * CONV_PAD, ct), f32), pltpu.VMEM((SUBLANES, L + 2 * CONV_PAD, ct), f32)],
        sem=("parallel", "arbitrary"), args=(p_all, p_all, dcz, dw_w), carry=carry)
    return res if carry is None else (res, carried)


def _exchange(arrs, scatter, name):
    ex = _Exchange(arrs, scatter)
    n = ex.n

    def body(*refs):
        ex.start(refs[:n], refs[n:2 * n], refs[2 * n:])
        ex.finish(refs[:n], refs[n:2 * n], refs[2 * n:])

    res = _pc(body, name=name, in_specs=ex.specs, out_specs=ex.specs, out_shape=ex.out_shape, scratch_shapes=ex.scratch)(*arrs)
    return list(res)


class _Exchange:
    def __init__(self, arrs, scatter):
        self.arrs, self.scatter, self.n = list(arrs), scatter, len(arrs)
        self.out_shape = [jax.ShapeDtypeStruct(((N_DEV,) + a.shape[1:]) if scatter else ((N_DEV,) + a.shape), a.dtype) for a in arrs]
        self.specs = [pl.BlockSpec(memory_space=pl.ANY)] * self.n
        self.scratch = [pltpu.SemaphoreType.DMA((self.n, N_DEV - 1)), pltpu.SemaphoreType.DMA((self.n, N_DEV - 1)),
                        pltpu.SemaphoreType.DMA((self.n,))]

    def _copies(self, ins, outs, sems, landing):
        send_sems, recv_sems, local_sems = sems
        me = 4 * lax.axis_index("x") + 2 * lax.axis_index("y") + lax.axis_index("c")
        if landing:
            local = []
        else:
            local = [pltpu.make_async_copy(ins[a].at[me] if self.scatter else ins[a], outs[a].at[me], local_sems.at[a]) for a in range(self.n)]
        remote = []
        for k in range(1, N_DEV):
            p = (me + (N_DEV - k if landing else k)) % N_DEV
            for a in range(self.n):
                remote.append(pltpu.make_async_remote_copy(
                    src_ref=ins[a].at[p] if self.scatter else ins[a], dst_ref=outs[a].at[p if landing else me],
                    send_sem=send_sems.at[a, k - 1], recv_sem=recv_sems.at[a, k - 1],
                    device_id=(p // 4, (p // 2) % 2, p % 2), device_id_type=MESH))
        return local, remote

    def _gather_plan(self, ins, outs, sems):
        send_sems, recv_sems, local_sems = sems
        x, y, c = lax.axis_index("x"), lax.axis_index("y"), lax.axis_index("c")
        chips = [(1 - x, y), (x, 1 - y), (1 - x, 1 - y)]

        def blk(px, py, pc):
            return 4 * px + 2 * py + pc

        def copy(a, k, block, to, own):
            return pltpu.make_async_remote_copy(
                src_ref=ins[a] if own else outs[a].at[block], dst_ref=outs[a].at[block],
                send_sem=send_sems.at[a, k], recv_sem=recv_sems.at[a, k], device_id=to, device_id_type=MESH)

        me = blk(x, y, c)
        local = [pltpu.make_async_copy(ins[a], outs[a].at[me], local_sems.at[a]) for a in range(self.n)]
        return local, copy, me, (x, y, 1 - c), chips, blk, c

    def start(self, ins, outs, sems):
        if self.scatter:
            local, sends = self._copies(ins, outs, sems, False)
            for cp in local + sends:
                cp.start()
            return
        local, copy, me, sibling, chips, _, c = self._gather_plan(ins, outs, sems)
        for cp in local:
            cp.start()
        for a in range(self.n):
            copy(a, 0, me, sibling, True).start()
            for j, chip in enumerate(chips):
                copy(a, 1 + j, me, (*chip, c), True).start()

    def finish(self, ins, outs, sems):
        if self.scatter:
            for cp in self._copies(ins, outs, sems, True)[1]:
                cp.wait_recv()
            local, sends = self._copies(ins, outs, sems, False)
            for cp in sends:
                cp.wait_send()
            for cp in local:
                cp.wait()
            return
        local, copy, me, sibling, chips, blk, c = self._gather_plan(ins, outs, sems)
        for j, chip in enumerate(chips):
            for a in range(self.n):
                copy(a, 1 + j, blk(*chip, c), sibling, True).wait_recv()
                copy(a, 4 + j, blk(*chip, c), sibling, False).start()
        for a in range(self.n):
            copy(a, 0, blk(*sibling), sibling, True).wait_recv()
            for j, chip in enumerate(chips):
                copy(a, 4 + j, blk(*chip, 1 - c), sibling, False).wait_recv()
        for a in range(self.n):
            copy(a, 0, me, sibling, True).wait_send()
            for j, chip in enumerate(chips):
                copy(a, 1 + j, me, (*chip, c), True).wait_send()
                copy(a, 4 + j, blk(*chip, c), sibling, False).wait_send()
        for cp in local:
            cp.wait()


def _carried(inner, n_in, n_out, grid, ex):
    n = ex.n

    def body(*refs):
        own_in, c_in = refs[:n_in], refs[n_in:n_in + n]
        own_out, c_out = refs[n_in + n:n_in + n + n_out], refs[n_in + n + n_out:n_in + 2 * n + n_out]
        rest = refs[n_in + 2 * n + n_out:]
        own_scr, sems = rest[:len(rest) - 3], rest[len(rest) - 3:]
        pids = [pl.program_id(d) for d in range(len(grid))]
        first = functools.reduce(jnp.logical_and, [p == 0 for p in pids])
        last = functools.reduce(jnp.logical_and, [p == g - 1 for p, g in zip(pids, grid)])

        @pl.when(first)
        def _():
            ex.start(c_in, c_out, sems)

        inner(*own_in, *own_out, *own_scr)

        @pl.when(last)
        def _():
            ex.finish(c_in, c_out, sems)

    return body


def _call(inner, *, name, grid, in_specs, out_specs, out_shape, scratch_shapes, sem, args, carry=None):
    if carry is None:
        res = _pc(inner, name=name, grid=grid, in_specs=in_specs, out_specs=out_specs, out_shape=out_shape,
                  scratch_shapes=scratch_shapes, compiler_params=_params(*sem))(*args)
        return list(res), None
    ex = _Exchange(*carry)
    res = _pc(_carried(inner, len(in_specs), len(out_specs), grid, ex), name=name, grid=grid,
              in_specs=list(in_specs) + ex.specs, out_specs=list(out_specs) + ex.specs, out_shape=list(out_shape) + ex.out_shape,
              scratch_shapes=list(scratch_shapes) + ex.scratch, compiler_params=_params(*(["arbitrary"] * len(grid))))(*args, *ex.arrs)
    res = list(res)
    return res[:len(out_specs)], res[len(out_specs):]


def _mod_fwd(c_all, c_ctx, w_loc, b_loc, name):
    nr, D = c_all.shape
    nc = w_loc.shape[1]

    def body(c_ref, cc_ref, w_ref, b_ref, o_ref):
        a = jnp.concatenate([c_ref[...], jnp.broadcast_to(cc_ref[...], (8, D))], axis=0)
        s = jax.nn.silu(a).astype(bf16)
        o_ref[...] = jnp.dot(s, w_ref[...].astype(bf16), preferred_element_type=f32) + b_ref[...]

    return _pc(body, name=name, out_shape=jax.ShapeDtypeStruct((nr + 8, nc), f32), compiler_params=_params())(c_all, c_ctx, w_loc, b_loc)


def _mod_bwd(c_all, c_ctx, w_loc, dmx_loc, dmc_loc, name):
    nr, D = c_all.shape
    nc = w_loc.shape[1]

    def body(c_ref, cc_ref, w_ref, dmx_ref, dmc_ref, gw_ref, gc_ref):
        cc = cc_ref[...]
        a = jnp.concatenate([c_ref[...], jnp.broadcast_to(cc, (N_DEV, D))], axis=0)
        s = jax.nn.silu(a).astype(bf16)
        g = jnp.concatenate([dmx_ref[...], dmc_ref[...]], axis=0).astype(bf16)
        gw_ref[...] = lax.dot_general(s, g, (((0,), (0,)), ((), ())), preferred_element_type=f32)
        dmc = jnp.sum(dmc_ref[...], axis=0, keepdims=True)
        ds = lax.dot_general(jnp.broadcast_to(dmc, (8, nc)).astype(bf16), w_ref[...].astype(bf16), (((1,), (1,)), ((), ())),
                             preferred_element_type=f32)[0:1]
        sg = jax.nn.sigmoid(cc)
        gc_ref[...] = ds * (sg * (1.0 + cc * (1.0 - sg)))

    return _pc(body, name=name, out_shape=[jax.ShapeDtypeStruct((D, nc), f32), jax.ShapeDtypeStruct((1, D), f32)],
               compiler_params=_params())(c_all, c_ctx, w_loc, dmx_loc, dmc_loc)


def _adamw_math(w, g, m, v):
    m2 = ADAM_B1 * m + (1.0 - ADAM_B1) * g
    v2 = ADAM_B2 * v + (1.0 - ADAM_B2) * jnp.square(g)
    m_hat = m2 / (1.0 - ADAM_B1 ** ADAM_STEP)
    v_hat = v2 / (1.0 - ADAM_B2 ** ADAM_STEP)
    delta = -ADAM_LR * (m_hat / (jnp.sqrt(v_hat) + ADAM_EPS) + ADAM_WD * w)
    return delta, m2, v2


def _adamw_many(params, name):
    n = len(params)

    def body(*refs):
        ins, outs = refs[:4 * n], refs[4 * n:]
        for i in range(n):
            w, m, v, g = (ins[4 * i + k][...] for k in range(4))
            d, m2, v2 = _adamw_math(w, g, m, v)
            outs[3 * i][...] = d
            outs[3 * i + 1][...] = m2
            outs[3 * i + 2][...] = v2

    res = _pc(body, name=name, out_shape=[jax.ShapeDtypeStruct(p[0].shape, f32) for p in params for _ in range(3)],
              compiler_params=_params())(*[a for p in params for a in p])
    return [tuple(res[3 * i:3 * i + 3]) for i in range(n)]


def _adamw(w, m, v, g, name, partials, carry=None):
    r, cdim = w.shape
    tr = _pick(r, 256, 8)

    def body(w_ref, m_ref, v_ref, g_ref, og_ref, od_ref, om_ref, ov_ref):
        if partials:
            g = g_ref[0].astype(f32)
            for s in range(1, N_DEV):
                g = g + g_ref[s].astype(f32)
        else:
            g = g_ref[...]
        d, m2, v2 = _adamw_math(w_ref[...], g, m_ref[...], v_ref[...])
        og_ref[...] = g
        od_ref[...] = d
        om_ref[...] = m2
        ov_ref[...] = v2

    blk = pl.BlockSpec((tr, cdim), lambda i: (i, 0))
    g_spec = pl.BlockSpec((N_DEV, tr, cdim), lambda i: (0, i, 0)) if partials else blk
    res, carried = _call(body, name=name, grid=(r // tr,), in_specs=[blk, blk, blk, g_spec], out_specs=[blk] * 4,
                         out_shape=[jax.ShapeDtypeStruct((r, cdim), f32)] * 4, scratch_shapes=[], sem=("parallel",),
                         args=(w, m, v, g), carry=carry)
    return res if carry is None else (res, carried)


def _sum_sources(parts, name):
    def body(*refs):
        for i_ref, o_ref in zip(refs[:len(parts)], refs[len(parts):]):
            acc = i_ref[0]
            for s in range(1, i_ref.shape[0]):
                acc = acc + i_ref[s]
            o_ref[...] = acc

    return list(_pc(body, name=name, out_shape=[jax.ShapeDtypeStruct(p.shape[1:], f32) for p in parts],
                    compiler_params=_params())(*parts))


def kernel(x, c, ctx, c_ctx, w_mod, b_mod, g_ffn1, w1_gu, w1_down, g_mix, w_in, dw_weight, dw_bias, conv_ln_g, conv_ln_b, w_conv_out, w_alpha_f, b_alpha_f, w_alpha_b, b_alpha_b, gla_norm_g, w_gla_out, w_out, g_ffn2, w2_gu, w2_down, g_final, loss_target, m_c_ctx, m_w_mod, m_b_mod, m_g_ffn1, m_w1_gu, m_w1_down, m_g_mix, m_w_in, m_dw_weight, m_dw_bias, m_conv_ln_g, m_conv_ln_b, m_w_conv_out, m_w_alpha_f, m_b_alpha_f, m_w_alpha_b, m_b_alpha_b, m_gla_norm_g, m_w_gla_out, m_w_out, m_g_ffn2, m_w2_gu, m_w2_down, m_g_final, v_c_ctx, v_w_mod, v_b_mod, v_g_ffn1, v_w1_gu, v_w1_down, v_g_mix, v_w_in, v_dw_weight, v_dw_bias, v_conv_ln_g, v_conv_ln_b, v_w_conv_out, v_w_alpha_f, v_b_alpha_f, v_w_alpha_b, v_b_alpha_b, v_gla_norm_g, v_w_gla_out, v_w_out, v_g_ffn2, v_w2_gu, v_w2_down, v_g_final):
    B, L, D = x.shape
    Lc = ctx.shape[1]
    T, Tc = B * L, B * Lc
    Tall = T + Tc
    F = w1_down.shape[1] * N_DEV
    DK, DV = D // (2 * HEADS), D // HEADS
    QK = HEADS * DK
    PW = 7 * D + LR_PAD
    tm = ROW_TILE
    tpe = L // tm
    nx, nall = T // tm, Tall // tm
    me = 4 * lax.axis_index("x") + 2 * lax.axis_index("y") + lax.axis_index("c")

    rw_all = dict(tm=tm, n_tiles=nall, tpe=tpe, nx_tiles=nx, n_ex=B + 1)
    rw_x = dict(tm=tm, n_tiles=nx, tpe=tpe, nx_tiles=nx, n_ex=B)
    rw_all2, rw_x2 = rw_all, rw_x

    dww_g, waf_g, wab_g, c_g = _exchange([dw_weight[0], w_alpha_f[0], w_alpha_b[0], c], False, "gather_first")

    def cols(gat):
        return jnp.transpose(gat, (1, 0, 2)).reshape(gat.shape[1], N_DEV * gat.shape[2])

    def rows_(gat):
        return gat.reshape(N_DEV * gat.shape[1], gat.shape[2])

    dww = cols(dww_g)
    WA = jnp.zeros((LR_PAD, 2 * QK), f32).at[:LOWRANK, :QK].set(cols(waf_g)).at[LOWRANK:2 * LOWRANK, QK:].set(cols(wab_g)).astype(bf16)
    BA = jnp.concatenate([b_alpha_f, b_alpha_b], axis=1)
    c_all = c_g.reshape(N_DEV * B, D)
    c_ctx2 = c_ctx.reshape(1, D)

    ncm = w_mod.shape[2]
    b_mod_loc = lax.dynamic_slice(b_mod, (0, me * ncm), (1, ncm))
    mod_loc = _mod_fwd(c_all, c_ctx2, w_mod[0], b_mod_loc, "mod_fwd")
    (mod_g,) = _exchange([mod_loc], False, "gather_mod")
    mod_full = cols(mod_g)
    mod_tab = jnp.concatenate([lax.dynamic_slice(mod_full, (me * B, 0), (B, N_MOD * D)), mod_full[N_DEV * B:N_DEV * B + 1]], axis=0)
    mods = [mod_tab[:, i * D:(i + 1) * D].reshape(B + 1, 1, D) for i in range(N_MOD)]
    mods_x = [mm[:B] for mm in mods]

    x_lat, x_ctx = (x.reshape(T, D), D, 0, True), (ctx.reshape(Tc, D), D, 0, "c")

    def f_ffn_in(tok, ex, sh):
        return [_rms_mod(tok[0] + tok[1], sh[0], ex[0], ex[1])], [], []

    (u1,), (w1gu_g,) = _rowwise(f_ffn_in, name="ffn1_in", tok_in=[x_lat, x_ctx], ex_in=[mods[0], mods[1]], sh_in=[g_ffn1],
                                tok_out=[(D, bf16)], carry=([w1_gu[0].astype(bf16)], False), **rw_all)
    W1gu = cols(w1gu_g)
    (gu1, h1), (w1d_g, win_g) = _ffn_up(u1, W1gu, "ffn1_up", carry=([w1_down[0].astype(bf16), w_in[0].astype(bf16)], False))
    W1d = rows_(w1d_g)
    lr2 = 2 * LOWRANK
    segs = [(0, 2 * D, 0), (2 * D, 2 * D + QK, 6 * D), (2 * D + QK, 3 * D, 6 * D + QK), (3 * D, 4 * D, 2 * D), (4 * D, 5 * D, 3 * D),
            (5 * D, 5 * D + lr2, 7 * D), (5 * D + lr2, 6 * D + lr2, 4 * D), (6 * D + lr2, 7 * D + lr2, 5 * D)]
    wc = w_in.shape[2]
    win_parts = []
    for lo, hi, _ in sorted(segs, key=lambda t: t[2]):
        for d in range(N_DEV):
            a0, a1 = max(lo, d * wc), min(hi, (d + 1) * wc)
            if a0 < a1:
                win_parts.append(win_g[d][:, a0 - d * wc:a1 - d * wc])
    Win = jnp.concatenate(win_parts + [jnp.zeros((D, LR_PAD - lr2), bf16)], axis=1)

    def nn(a, w):
        return jnp.dot(a.astype(bf16), w, preferred_element_type=f32)

    def nt(a, w):
        return lax.dot_general(a.astype(bf16), w, (((1,), (1,)), ((), ())), preferred_element_type=f32)

    def mix_in(xv, fv, gate, sh, sc, g):
        x1 = xv + 0.5 * gate * fv
        return x1, _rms_mod(x1, g, sh, sc)

    def f_mix_in(tok, ex, sh):
        f1v = nn(tok[2], sh[1])
        return list(mix_in(tok[0] + tok[1], f1v, ex[0], ex[1], ex[2], sh[0])) + [f1v], [], []

    x1, um, f1 = _rowwise(f_mix_in, name="ffn1_down_mix_in", tok_in=[x_lat, x_ctx, (h1, F, 0, False)],
                          ex_in=[mods[2], mods[3], mods[4]], sh_in=[g_mix, W1d], tok_out=[(D, f32), (D, bf16), (D, bf16)], **rw_all2)
    p_all, (wco_g, wgo_g, wo_g, w2gu_g) = _matmul(
        um, Win, "nn", bf16, "in_proj", tm_cap=512, tn_cap=2432,
        carry=([w_conv_out[0].astype(bf16), w_gla_out[0].astype(bf16), w_out[0].astype(bf16), w2_gu[0].astype(bf16)], False))
    Wco, Wgo, Wo, W2gu = rows_(wco_g), rows_(wgo_g), rows_(wo_g), cols(w2gu_g)

    def log_decay(lr, wa, ba):
        z = _bdot(lr, wa, (1, 0)) + ba
        return _log_sigmoid(z) / TAU

    def f_decay(tok, ex, sh):
        return [log_decay(tok[0], sh[0], sh[1])], [], []

    lr_blk = (p_all, LR_PAD, 7 * D // LR_PAD, False)
    (la_all,) = _rowwise(f_decay, name="log_decay", tok_in=[lr_blk], sh_in=[WA, BA], tok_out=[(2 * QK, f32)], **rw_all)

    zeros_s = jnp.zeros((B, HEADS, DV, DK), f32)
    gla_c = dict(row0=T, nb=B, seq=Lc, D=D)
    gla_x = dict(row0=0, nb=B, seq=L, D=D)
    _, hist_cf, s_f = _gla_fwd(p_all, la_all, zeros_s, rev=False, name="gla_ctx_f", **gla_c)
    _, hist_cb, s_b = _gla_fwd(p_all, la_all, zeros_s, rev=True, name="gla_ctx_b", **gla_c)
    (o_f, hist_f, _), (w2d_g,) = _gla_fwd(p_all, la_all, s_f, rev=False, name="gla_x_f", carry=([w2_down[0].astype(bf16)], False), **gla_x)
    W2d = rows_(w2d_g)
    o_b, hist_b, _ = _gla_fwd(p_all, la_all, s_b, rev=True, name="gla_x_b", **gla_x)

    cz = _conv_fwd(p_all, dww, dw_bias, B=B, L=L, D=D, name="conv_fwd")

    def ln_silu(z, g, b):
        mu = jnp.mean(z, axis=-1, keepdims=True)
        var = jnp.mean(jnp.square(z - mu), axis=-1, keepdims=True)
        return jax.nn.silu((z - mu) * lax.rsqrt(var + EPS) * g + b)

    def f_ln(tok, ex, sh):
        zc = ln_silu(tok[0].astype(f32), sh[0], sh[1])
        return [zc, nn(zc, sh[2])], [], []

    zc, yc = _rowwise(f_ln, name="conv_ln_out", tok_in=[(cz, D, 0, False)], sh_in=[conv_ln_g, conv_ln_b, Wco],
                      tok_out=[(D, bf16), (D, bf16)], **rw_x)

    def gla_out(of, ob, og, gn):
        return _head_rms(of.astype(f32) + ob.astype(f32), DV) * gn * jax.nn.silu(og.astype(f32))

    def f_gla_out(tok, ex, sh):
        og2 = gla_out(tok[0], tok[1], tok[2], sh[0])
        return [og2, nn(og2, sh[1])], [], []

    og_blk = (p_all, D, 3, False)
    og2, yg = _rowwise(f_gla_out, name="gla_norm_out", tok_in=[(o_f, D, 0, False), (o_b, D, 0, False), og_blk], sh_in=[gla_norm_g, Wgo],
                       tok_out=[(D, bf16), (D, bf16)], **rw_x)

    def merge(ga, gb, ycv, ygv):
        return jax.nn.sigmoid(ga.astype(f32)) * ycv.astype(f32) + jax.nn.sigmoid(gb.astype(f32)) * ygv.astype(f32)

    def f_merge(tok, ex, sh):
        mg = merge(*tok)
        return [mg, nn(mg, sh[0])], [], []

    ga_blk, gb_blk = (p_all, D, 4, False), (p_all, D, 5, False)
    mg, mix = _rowwise(f_merge, name="merge_mix_out", tok_in=[ga_blk, gb_blk, (yc, D, 0, False), (yg, D, 0, False)], sh_in=[Wo],
                       tok_out=[(D, bf16), (D, f32)], **rw_x)

    def ffn2_in(x1v, mixv, g5, sh, sc, g):
        x2 = x1v + g5 * mixv
        return x2, _rms_mod(x2, g, sh, sc)

    def f_ffn2_in(tok, ex, sh):
        return list(ffn2_in(tok[0], tok[1], ex[0], ex[1], ex[2], sh[0])), [], []

    x2, u2 = _rowwise(f_ffn2_in, name="ffn2_in", tok_in=[(x1, D, 0, False), (mix, D, 0, False)], ex_in=[mods_x[5], mods_x[6], mods_x[7]],
                      sh_in=[g_ffn2], tok_out=[(D, f32), (D, bf16)], **rw_x)
    gu2, h2 = _ffn_up(u2, W2gu, "ffn2_up")

    gf2 = g_final.reshape(1, D)

    def head_loss(x2v, f2v, g8, gf, tgt):
        x3 = x2v + 0.5 * g8 * f2v
        y = x3 * lax.rsqrt(jnp.mean(x3 * x3, axis=-1, keepdims=True) + EPS) * gf
        return 0.5 * jnp.sum(jnp.mean(jnp.square(y - tgt), axis=-1))

    def f_head(tok, ex, sh):
        loss, vjp = jax.vjp(lambda a, b_, c_, d_: head_loss(a, b_, c_, d_, tok[2]), tok[0], nn(tok[1], sh[1]), ex[0], sh[0])
        dx3, df2, dg8, dgf = vjp(jnp.ones((), f32))
        return [dx3, df2], [dg8], [dgf, jnp.broadcast_to(loss.reshape(1, 1), (1, 128))]

    dx3, df2, dg8, dgf, loss_p = _rowwise(
        f_head, name="ffn2_down_head", tok_in=[(x2, D, 0, False), (h2, F, 0, False), (loss_target.reshape(T, D), D, 0, False)],
        ex_in=[mods_x[8]], sh_in=[gf2, W2d], tok_out=[(D, f32), (D, bf16)], ex_out=[D], gl_out=[(1, D), (1, 128)], **rw_x2)

    dgu2 = _ffn_down_dx(df2, W2d, gu2, "ffn2_down_dx")
    gW2d = _matmul(h2, df2, "tn", f32, "ffn2_down_dw", tm_cap=1408)
    du2 = _matmul(dgu2, W2gu, "nt", bf16, "ffn2_up_dx", halves="a")
    gW2gu = _matmul(u2, dgu2, "tn", f32, "ffn2_up_dw", halves="b")

    def f_ffn2_in_bwd(tok, ex, sh):
        _, vjp = jax.vjp(ffn2_in, tok[0], tok[1], ex[0], ex[1], ex[2], sh[0])
        dx2, dmix, dg5, dsh, dsc, dg = vjp((tok[3], tok[2].astype(f32)))
        return [dx2, dmix], [dg5, dsh, dsc], [dg]

    dx2, dmix, dg5, dsh6, dsc7, dg_ffn2 = _rowwise(
        f_ffn2_in_bwd, name="ffn2_in_bwd", tok_in=[(x1, D, 0, False), (mix, D, 0, False), (du2, D, 0, False), (dx3, D, 0, False)],
        ex_in=[mods_x[5], mods_x[6], mods_x[7]], sh_in=[g_ffn2], tok_out=[(D, f32), (D, bf16)], ex_out=[D, D, D], gl_out=[(1, D)], **rw_x)

    gWo = _matmul(mg, dmix, "tn", f32, "mix_out_dw")

    def f_merge_bwd(tok, ex, sh):
        _, vjp = jax.vjp(merge, *[t.astype(f32) for t in tok[:4]])
        dga, dgb, dyc, dyg = vjp(nt(tok[4], sh[0]))
        return [dga, dgb, dyc, dyg], [], []

    dga, dgb, dyc, dyg = _rowwise(f_merge_bwd, name="mix_out_merge_bwd",
                                  tok_in=[ga_blk, gb_blk, (yc, D, 0, False), (yg, D, 0, False), (dmix, D, 0, False)], sh_in=[Wo],
                                  tok_out=[(D, bf16)] * 4, **rw_x)
    gWco = _matmul(zc, dyc, "tn", f32, "conv_out_dw")
    gWgo = _matmul(og2, dyg, "tn", f32, "gla_out_dw")

    def f_ln_bwd(tok, ex, sh):
        _, vjp = jax.vjp(ln_silu, tok[0].astype(f32), sh[0], sh[1])
        dcz, dg, db = vjp(nt(tok[1], sh[2]))
        return [dcz], [], [dg, db, jnp.sum(dcz, axis=0, keepdims=True)]

    dcz, g_ln_g, g_ln_b, g_dwb = _rowwise(f_ln_bwd, name="conv_out_ln_bwd", tok_in=[(cz, D, 0, False), (dyc, D, 0, False)],
                                          sh_in=[conv_ln_g, conv_ln_b, Wco], tok_out=[(D, bf16)], gl_out=[(1, D)] * 3, **rw_x)
    def col_shards(g):
        return jnp.transpose(g.reshape(g.shape[0], N_DEV, g.shape[1] // N_DEV), (1, 0, 2)).astype(bf16)

    def row_shards(g):
        return g.reshape(N_DEV, g.shape[0] // N_DEV, g.shape[1]).astype(bf16)

    (dca, dcb, g_dww), (r_w2d, r_w2gu, r_wo, r_wco, r_wgo) = _conv_bwd(
        p_all, dcz, dww, B=B, L=L, D=D, name="conv_bwd",
        carry=([row_shards(gW2d), col_shards(gW2gu), row_shards(gWo), row_shards(gWco), row_shards(gWgo)], True))

    def f_gla_out_bwd(tok, ex, sh):
        _, vjp = jax.vjp(gla_out, tok[0].astype(f32), tok[1].astype(f32), tok[2].astype(f32), sh[0])
        dof, _, dog, dgn = vjp(nt(tok[3], sh[1]))
        return [dof, dog], [], [dgn]

    d_o, dog, g_gn = _rowwise(f_gla_out_bwd, name="gla_out_norm_bwd",
                              tok_in=[(o_f, D, 0, False), (o_b, D, 0, False), og_blk, (dyg, D, 0, False)], sh_in=[gla_norm_g, Wgo],
                              tok_out=[(D, bf16), (D, bf16)], gl_out=[(1, D)], **rw_x)

    dq_f, dk_f, dv_f, dla_f, ds_f = _gla_bwd(p_all, la_all, hist_f, d_o, zeros_s, rev=False, name="gla_x_f_bwd", **gla_x)
    dq, dk, dv, dla_b, ds_b = _gla_bwd(p_all, la_all, hist_b, d_o, zeros_s, rev=True, name="gla_x_b_bwd", add=(dq_f, dk_f, dv_f), **gla_x)
    dq_cf, dk_cf, dv_cf, dla_cf, _ = _gla_bwd(p_all, la_all, hist_cf, None, ds_f, rev=False, name="gla_ctx_f_bwd", **gla_c)
    _, dk_c, dv_c, dla_cb, _ = _gla_bwd(p_all, la_all, hist_cb, None, ds_b, rev=True, name="gla_ctx_b_bwd", add=(dq_cf, dk_cf, dv_cf), **gla_c)

    dla_all = jnp.concatenate([jnp.concatenate([dla_f, dla_b], axis=1), jnp.concatenate([dla_cf, dla_cb], axis=1)], axis=0)

    def f_decay_bwd(tok, ex, sh):
        _, vjp = jax.vjp(log_decay, tok[0].astype(f32), sh[0].astype(f32), sh[1])
        dlr, dwa, dba = vjp(tok[1])
        return [dlr], [], [dwa, dba]

    dlr, g_WA, g_BA = _rowwise(f_decay_bwd, name="log_decay_bwd", tok_in=[lr_blk, (dla_all, 2 * QK, 0, False)], sh_in=[WA, BA],
                               tok_out=[(LR_PAD, bf16)], gl_out=[(LR_PAD, 2 * QK), (1, 2 * QK)], **rw_all)

    zc_ = functools.partial(jnp.zeros, dtype=bf16)
    dp_x = jnp.concatenate([dca, dcb, dv, dog, dga, dgb, dq, dk, dlr[:T]], axis=1)
    dp_c = jnp.concatenate([zc_((Tc, 2 * D)), dv_c, zc_((Tc, 3 * D)), zc_((Tc, QK)), dk_c, dlr[T:]], axis=1)
    dp_all = jnp.concatenate([dp_x, dp_c], axis=0)
    gWin_p = _matmul(um, dp_all, "tn", f32, "in_proj_dw", tm_cap=512, tn_cap=2432)
    gwin_shards = []
    for d in range(N_DEV):
        parts = []
        for lo, hi, po in segs:
            a0, a1 = max(lo, d * wc), min(hi, (d + 1) * wc)
            if a0 < a1:
                parts.append(gWin_p[:, po + a0 - lo:po + a1 - lo])
        gwin_shards.append(jnp.concatenate(parts, axis=1))
    dum, (r_win,) = _matmul(dp_all, Win, "nt", bf16, "in_proj_dx", tk_cap=2432, carry=([jnp.stack(gwin_shards).astype(bf16)], True))

    def f_mix_in_bwd(tok, ex, sh):
        _, vjp = jax.vjp(mix_in, tok[0] + tok[1], tok[2].astype(f32), ex[0], ex[1], ex[2], sh[0])
        dx1, df1, dgate, dsh, dsc, dg = vjp((tok[4], tok[3].astype(f32)))
        return [dx1, df1], [dgate, dsh, dsc], [dg]

    dx1, df1, dg2, dsh3, dsc4, dg_mix = _rowwise(
        f_mix_in_bwd, name="mix_in_bwd", tok_in=[x_lat, x_ctx, (f1, D, 0, False), (dum, D, 0, False), (dx2, D, 0, True)],
        ex_in=[mods[2], mods[3], mods[4]], sh_in=[g_mix], tok_out=[(D, f32), (D, bf16)], ex_out=[D, D, D], gl_out=[(1, D)], **rw_all)

    dgu1 = _ffn_down_dx(df1, W1d, gu1, "ffn1_down_dx")
    gW1d = _matmul(h1, df1, "tn", f32, "ffn1_down_dw", tm_cap=1408)
    gW1gu, (r_w1d,) = _matmul(u1, dgu1, "tn", f32, "ffn1_up_dw", carry=([row_shards(gW1d)], True), halves="b")
    du1, (r_w1gu,) = _matmul(dgu1, W1gu, "nt", bf16, "ffn1_up_dx", carry=([col_shards(gW1gu)], True), halves="a")

    def f_ffn_in_bwd(tok, ex, sh):
        _, vjp = jax.vjp(_rms_mod, tok[0] + tok[1], sh[0], ex[0], ex[1])
        dx, dg, dsh, dsc = vjp(tok[2].astype(f32))
        return [dx + tok[3]], [dsh, dsc], [dg]

    dx_lat, dsh0, dsc1, dg_ffn1 = _rowwise(
        f_ffn_in_bwd, name="ffn1_in_bwd", tok_in=[x_lat, x_ctx, (du1, D, 0, False), (dx1, D, 0, False)],
        ex_in=[mods[0], mods[1]], sh_in=[g_ffn1], tok_out=[(D, f32, "x")], ex_out=[D, D], gl_out=[(1, D)], **rw_all)
    grad_x = dx_lat.reshape(B, L, D)

    zrow = jnp.zeros((1, 1, D), f32)
    dmod_loc = jnp.concatenate([dsh0, dsc1, dg2, dsh3, dsc4] + [jnp.concatenate([t, zrow], axis=0) for t in (dg5, dsh6, dsc7, dg8)],
                               axis=2).reshape(B + 1, N_MOD * D)
    rows16 = jnp.concatenate([jnp.concatenate([loss_p, jnp.zeros((1, D - loss_p.shape[1]), f32)], axis=1), dg_ffn1, dg_mix, g_dwb, g_ln_g,
                              g_ln_b, g_BA, g_gn, dg_ffn2, dgf, jnp.zeros((6, D), f32)], axis=0)

    def to8(v):
        n_pad = -(-v.shape[1] // 1024) * 1024
        return jnp.pad(v, ((0, 0), (0, n_pad - v.shape[1]))).reshape(8, n_pad // 8)

    def from8(a, n):
        return a.reshape(1, a.size)[:, :n]

    def adam_big(nm, wv, mv, vv, part, carry=None):
        out = _adamw(wv[0], mv[0], vv[0], part, "adamw_" + nm, True, carry=carry)
        res4, carried = out if carry is not None else (out, None)
        return [t[None] for t in res4], carried

    rs_out = dict(w1_gu=r_w1gu, w1_down=r_w1d, w_in=r_win, w_conv_out=r_wco, w_gla_out=r_wgo, w_out=r_wo, w2_gu=r_w2gu, w2_down=r_w2d)
    big = {}
    big["w_in"], (dmod_g, rows_g, dww_sg, wa_sg) = adam_big(
        "w_in", w_in, m_w_in, v_w_in, rs_out["w_in"], carry=([dmod_loc, rows16, g_dww, g_WA[:2 * LOWRANK]], False))
    dmx = dmod_g[:, :B].reshape(N_DEV * B, N_MOD * D)
    dmc = dmod_g[:, B]
    gWmod, gcc_p = _mod_bwd(c_all, c_ctx2, w_mod[0], lax.dynamic_slice(dmx, (0, me * ncm), (N_DEV * B, ncm)),
                            lax.dynamic_slice(dmc, (0, me * ncm), (N_DEV, ncm)), "mod_bwd")

    big["w1_gu"], (gcc_g,) = adam_big("w1_gu", w1_gu, m_w1_gu, v_w1_gu, rs_out["w1_gu"], carry=([to8(gcc_p)], False))
    for nm, wv, mv, vv in (("w1_down", w1_down, m_w1_down, v_w1_down), ("w_conv_out", w_conv_out, m_w_conv_out, v_w_conv_out),
                           ("w_gla_out", w_gla_out, m_w_gla_out, v_w_gla_out), ("w_out", w_out, m_w_out, v_w_out),
                           ("w2_gu", w2_gu, m_w2_gu, v_w2_gu), ("w2_down", w2_down, m_w2_down, v_w2_down)):
        big[nm], _ = adam_big(nm, wv, mv, vv, rs_out[nm])
    big["w_mod"] = [t[None] for t in _adamw(w_mod[0], m_w_mod[0], v_w_mod[0], gWmod, "adamw_w_mod", False)]

    rows_s, dww_s, wa_s, g_cc, g_bmod = _sum_sources(
        [rows_g, dww_sg, wa_sg, gcc_g, jnp.concatenate([dmx, dmc], axis=0).reshape(N_DEV * (B + 1), 8, N_MOD * D // 8)], "sum_small")
    g_cc, g_bmod = from8(g_cc, D), from8(g_bmod, N_MOD * D)
    loss = rows_s[0, 0]
    ncd, nca = dw_weight.shape[2], w_alpha_f.shape[2]
    g_dww_loc = lax.dynamic_slice(dww_s, (0, me * ncd), (CONV_W, ncd))
    g_waf_loc = lax.dynamic_slice(wa_s, (0, me * nca), (LOWRANK, nca))
    g_wab_loc = lax.dynamic_slice(wa_s, (LOWRANK, QK + me * nca), (LOWRANK, nca))
    sm = {k: rows_s[i:i + 1] for i, k in enumerate(["loss", "g_ffn1", "g_mix", "dw_bias", "conv_ln_g", "conv_ln_b", "b_alpha", "gla_norm_g",
                                                     "g_ffn2", "g_final"])}

    small_params = [("c_ctx", c_ctx, m_c_ctx, v_c_ctx, g_cc), ("b_mod", b_mod, m_b_mod, v_b_mod, g_bmod),
                    ("g_ffn1", g_ffn1, m_g_ffn1, v_g_ffn1, sm["g_ffn1"]), ("g_mix", g_mix, m_g_mix, v_g_mix, sm["g_mix"]),
                    ("dw_weight", dw_weight, m_dw_weight, v_dw_weight, g_dww_loc), ("dw_bias", dw_bias, m_dw_bias, v_dw_bias, sm["dw_bias"]),
                    ("conv_ln_g", conv_ln_g, m_conv_ln_g, v_conv_ln_g, sm["conv_ln_g"]),
                    ("conv_ln_b", conv_ln_b, m_conv_ln_b, v_conv_ln_b, sm["conv_ln_b"]),
                    ("w_alpha_f", w_alpha_f, m_w_alpha_f, v_w_alpha_f, g_waf_loc), ("b_alpha_f", b_alpha_f, m_b_alpha_f, v_b_alpha_f, sm["b_alpha"][:, :QK]),
                    ("w_alpha_b", w_alpha_b, m_w_alpha_b, v_w_alpha_b, g_wab_loc), ("b_alpha_b", b_alpha_b, m_b_alpha_b, v_b_alpha_b, sm["b_alpha"][:, QK:]),
                    ("gla_norm_g", gla_norm_g, m_gla_norm_g, v_gla_norm_g, sm["gla_norm_g"]),
                    ("g_ffn2", g_ffn2, m_g_ffn2, v_g_ffn2, sm["g_ffn2"]), ("g_final", g_final, m_g_final, v_g_final, sm["g_final"])]

    def two_d(t, like):
        return t.reshape(like.shape[1:]) if like.ndim == 3 else t.reshape(like.size // 128, 128)

    small_res = _adamw_many([tuple(two_d(t, wv) for t in (wv, mv, vv, gv)) for _, wv, mv, vv, gv in small_params], "adamw_small")
    small_out = {nm: [gv.reshape(wv.shape)] + [t.reshape(wv.shape) for t in r3]
                 for (nm, wv, _, _, gv), r3 in zip(small_params, small_res)}

    order = ["c_ctx", "w_mod", "b_mod", "g_ffn1", "w1_gu", "w1_down", "g_mix", "w_in", "dw_weight", "dw_bias", "conv_ln_g", "conv_ln_b",
             "w_conv_out", "w_alpha_f", "b_alpha_f", "w_alpha_b", "b_alpha_b", "gla_norm_g", "w_gla_out", "w_out", "g_ffn2", "w2_gu",
             "w2_down", "g_final"]
    res = {**big, **small_out}
    return (loss, grad_x, *[res[n][0] for n in order], *[res[n][1] for n in order], *[res[n][2] for n in order], *[res[n][3] for n in order])
```

```python
import functools

import jax
import jax.numpy as jnp
from jax import lax
from jax.experimental import pallas as pl
from jax.experimental.pallas import tpu as pltpu

f32, bf16 = jnp.float32, jnp.bfloat16

N_DEV = 8
HEADS = 4
LOWRANK = 16
CONV_W = 31
CONV_PAD = 16
SUBLANES = 8
CHUNK = 64
SUB = 16
GLA_ROWS = 256
GLA_SAFE_DECAY = 60.0
TAU = 16.0
EPS = 1e-6
N_MOD = 9
LR_PAD = 128
ROW_TILE = 512
V7X_VMEM_BYTES = 64 << 20
VMEM_LIMIT = (V7X_VMEM_BYTES * 3) // 4

ADAM_LR, ADAM_B1, ADAM_B2, ADAM_EPS, ADAM_WD, ADAM_STEP = 0.001, 0.9, 0.999, 1e-08, 0.01, 10

MESH = pl.DeviceIdType.MESH


def _pc(body, **kw):
    return pl.pallas_call(body, **kw)


def _params(*sem):
    return pltpu.CompilerParams(dimension_semantics=sem, vmem_limit_bytes=VMEM_LIMIT)


def _pick(n, cap, unit=128):
    best = None
    for t in range(unit, min(n, cap) + 1, unit):
        if n % t == 0:
            best = t
    return best or n


def _matmul(a, b, mode, out_dtype, name, tm_cap=1024, tn_cap=1536, tk_cap=None, carry=None, halves=None):
    tk_cap = tk_cap or (2048 if mode == "tn" else 2816)
    if halves == "a":
        (_, M, Kh), N = a.shape, b.shape[0]
        K, tk = 2 * Kh, _pick(Kh, tk_cap)
        tm, tn = _pick(M, tm_cap), _pick(N, tn_cap)
        a_spec = pl.BlockSpec((None, tm, tk), lambda i, j, k: (k // (Kh // tk), i, k % (Kh // tk)))
    elif halves == "b":
        (K, M), (_, _, Nh) = a.shape, b.shape
        N, tn = 2 * Nh, _pick(Nh, tn_cap)
        tm, tk = _pick(M, tm_cap), _pick(K, tk_cap)
    else:
        if mode == "tn":
            (K, M), N = a.shape, b.shape[1]
        elif mode == "nt":
            (M, K), N = a.shape, b.shape[0]
        else:
            (M, K), N = a.shape, b.shape[1]
        tm, tn, tk = _pick(M, tm_cap), _pick(N, tn_cap), _pick(K, tk_cap)
    nk = K // tk
    if halves != "a":
        a_spec = pl.BlockSpec((tk, tm), lambda i, j, k: (k, i)) if mode == "tn" else pl.BlockSpec((tm, tk), lambda i, j, k: (i, k))
    if halves == "b":
        b_spec = pl.BlockSpec((None, tk, tn), lambda i, j, k: (j // (Nh // tn), k, j % (Nh // tn)))
    else:
        b_spec = pl.BlockSpec((tn, tk), lambda i, j, k: (j, k)) if mode == "nt" else pl.BlockSpec((tk, tn), lambda i, j, k: (k, j))
    dims = {"nn": ((1,), (0,)), "nt": ((1,), (1,)), "tn": ((0,), (0,))}[mode]

    def body_single(a_ref, b_ref, o_ref):
        o_ref[...] = lax.dot_general(a_ref[...].astype(bf16), b_ref[...].astype(bf16), (dims, ((), ())),
                                     preferred_element_type=f32).astype(out_dtype)

    def body(a_ref, b_ref, o_ref, acc_ref):
        k = pl.program_id(2)
        part = lax.dot_general(a_ref[...].astype(bf16), b_ref[...].astype(bf16), (dims, ((), ())), preferred_element_type=f32)

        @pl.when(k == 0)
        def _():
            acc_ref[...] = part

        @pl.when(k > 0)
        def _():
            acc_ref[...] += part

        @pl.when(k == nk - 1)
        def _():
            o_ref[...] = acc_ref[...].astype(out_dtype)

    (out,), carried = _call(
        body_single if nk == 1 else body, name=name, grid=(M // tm, N // tn, nk), in_specs=[a_spec, b_spec],
        out_specs=[pl.BlockSpec((tm, tn), lambda i, j, k: (i, j))], out_shape=[jax.ShapeDtypeStruct((M, N), out_dtype)],
        scratch_shapes=[] if nk == 1 else [pltpu.VMEM((tm, tn), f32)], sem=("parallel", "parallel", "arbitrary"),
        args=(a, b), carry=carry)
    return out if carry is None else (out, carried)


def _ffn_up(u, Wgu, name, carry=None):
    M, K = u.shape
    F = Wgu.shape[1] // 2
    tm, tn = _pick(M, 512), _pick(F, 1408)
    nj = F // tn

    def body(u_ref, wa_ref, wb_ref, gu_ref, h_ref):
        uv = u_ref[...]
        a = jnp.dot(uv, wa_ref[...], preferred_element_type=f32)
        b = jnp.dot(uv, wb_ref[...], preferred_element_type=f32)
        gu_ref[0] = a.astype(bf16)
        gu_ref[1] = b.astype(bf16)
        h_ref[...] = (jax.nn.silu(a) * b).astype(bf16)

    res, carried = _call(
        body, name=name, grid=(nj, M // tm),
        in_specs=[pl.BlockSpec((tm, K), lambda j, i: (i, 0)), pl.BlockSpec((K, tn), lambda j, i: (0, j)),
                  pl.BlockSpec((K, tn), lambda j, i: (0, nj + j))],
        out_specs=[pl.BlockSpec((2, tm, tn), lambda j, i: (0, i, j)), pl.BlockSpec((tm, tn), lambda j, i: (i, j))],
        out_shape=[jax.ShapeDtypeStruct((2, M, F), bf16), jax.ShapeDtypeStruct((M, F), bf16)],
        scratch_shapes=[], sem=("parallel", "parallel"), args=(u, Wgu, Wgu), carry=carry)
    return res if carry is None else (res, carried)


def _ffn_down_dx(df, Wd, gu, name):
    M, D = df.shape
    F = Wd.shape[0]
    tm, tn = _pick(M, 512), _pick(F, 1408)

    def body(df_ref, w_ref, gu_ref, o_ref):
        dh = lax.dot_general(df_ref[...], w_ref[...], (((1,), (1,)), ((), ())), preferred_element_type=f32)
        a, b = gu_ref[0].astype(f32), gu_ref[1].astype(f32)
        sg = jax.nn.sigmoid(a)
        o_ref[0] = (dh * b * sg * (1.0 + a * (1.0 - sg))).astype(bf16)
        o_ref[1] = (dh * a * sg).astype(bf16)

    return _pc(
        body, name=name, grid=(F // tn, M // tm),
        in_specs=[pl.BlockSpec((tm, D), lambda j, i: (i, 0)), pl.BlockSpec((tn, D), lambda j, i: (j, 0)),
                  pl.BlockSpec((2, tm, tn), lambda j, i: (0, i, j))],
        out_specs=pl.BlockSpec((2, tm, tn), lambda j, i: (0, i, j)), out_shape=jax.ShapeDtypeStruct((2, M, F), bf16),
        compiler_params=_params("parallel", "parallel"))(df, Wd, gu)


def _rowwise(fn, *, name, tm, n_tiles, tpe, nx_tiles, n_ex, tok_in=(), ex_in=(), sh_in=(), tok_out=(), ex_out=(), gl_out=(), carry=None):
    def seg(i):
        return jnp.minimum(i // tpe, n_ex - 1)

    in_specs, args = [], []
    for arr, w, cb, x_only in tok_in:
        if x_only == "c":
            in_specs.append(pl.BlockSpec((tm, w), functools.partial(lambda i, cb: (jnp.maximum(i - nx_tiles, 0), cb), cb=cb)))
        elif x_only:
            in_specs.append(pl.BlockSpec((tm, w), functools.partial(lambda i, cb: (jnp.minimum(i, nx_tiles - 1), cb), cb=cb)))
        else:
            in_specs.append(pl.BlockSpec((tm, w), functools.partial(lambda i, cb: (i, cb), cb=cb)))
        args.append(arr)
    for arr in ex_in:
        in_specs.append(pl.BlockSpec((1, 1, arr.shape[-1]), lambda i: (seg(i), 0, 0)))
        args.append(arr)
    for arr in sh_in:
        in_specs.append(pl.BlockSpec(arr.shape, functools.partial(lambda i, nd: (0,) * nd, nd=arr.ndim)))
        args.append(arr)
    out_specs, out_shape = [], []
    for w, dt, *x_rows in tok_out:
        if x_rows:
            out_specs.append(pl.BlockSpec((tm, w), lambda i: (jnp.minimum(i, nx_tiles - 1), 0)))
        else:
            out_specs.append(pl.BlockSpec((tm, w), lambda i: (i, 0)))
        out_shape.append(jax.ShapeDtypeStruct(((nx_tiles if x_rows else n_tiles) * tm, w), dt))
    for w in ex_out:
        out_specs.append(pl.BlockSpec((1, 1, w), lambda i: (seg(i), 0, 0)))
        out_shape.append(jax.ShapeDtypeStruct((n_ex, 1, w), f32))
    for r, w in gl_out:
        out_specs.append(pl.BlockSpec((r, w), lambda i: (0, 0)))
        out_shape.append(jax.ShapeDtypeStruct((r, w), f32))
    n_tok, n_exi, n_sh = len(tok_in), len(ex_in), len(sh_in)
    n_to, n_eo = len(tok_out), len(ex_out)
    x_only_flags = [t[3] for t in tok_in]
    x_rows_flags = [len(t) > 2 for t in tok_out]

    def body(*refs):
        i = pl.program_id(0)
        ins, outs = refs[: n_tok + n_exi + n_sh], refs[n_tok + n_exi + n_sh:]
        is_x = i < nx_tiles
        tok_vals = []
        for r, xo in zip(ins[:n_tok], x_only_flags):
            v = r[...]
            if xo == "c":
                v = jnp.where(is_x, jnp.zeros_like(v), v)
            elif xo:
                v = jnp.where(is_x, v, jnp.zeros_like(v))
            tok_vals.append(v)
        ex_vals = [r[0] for r in ins[n_tok:n_tok + n_exi]]
        sh_vals = [r[...] for r in ins[n_tok + n_exi:]]
        t_o, e_o, g_o = fn(tok_vals, ex_vals, sh_vals)
        for r, v, xr in zip(outs[:n_to], t_o, x_rows_flags):
            if xr:
                @pl.when(is_x)
                def _(r=r, v=v):
                    r[...] = v.astype(r.dtype)
            else:
                r[...] = v.astype(r.dtype)
        first = jnp.logical_and(i % tpe == 0, i <= nx_tiles)
        for r, v in zip(outs[n_to:n_to + n_eo], e_o):
            @pl.when(first)
            def _(r=r, v=v):
                r[0] = v

            @pl.when(jnp.logical_not(first))
            def _(r=r, v=v):
                r[0] += v
        for r, v in zip(outs[n_to + n_eo:], g_o):
            @pl.when(i == 0)
            def _(r=r, v=v):
                r[...] = v

            @pl.when(i > 0)
            def _(r=r, v=v):
                r[...] += v

    res, carried = _call(body, name=name, grid=(n_tiles,), in_specs=in_specs, out_specs=out_specs, out_shape=out_shape,
                         scratch_shapes=[], sem=("arbitrary",), args=args, carry=carry)
    return res if carry is None else (res, carried)


def _rms_mod(x, g, sh, sc):
    y = x * lax.rsqrt(jnp.mean(x * x, axis=-1, keepdims=True) + EPS) * g
    return y * (1.0 + sc) + sh


def _log_sigmoid(z):
    return jnp.minimum(z, 0.0) - jnp.log(1.0 + jnp.exp(-jnp.abs(z)))


def _head_rms(o, DV):
    parts = []
    for h in range(HEADS):
        oh = o[:, h * DV:(h + 1) * DV]
        parts.append(oh * lax.rsqrt(jnp.mean(oh * oh, axis=-1, keepdims=True) + EPS))
    return jnp.concatenate(parts, axis=1)


@functools.partial(jax.custom_vjp, nondiff_argnums=(2,))
def _bdot(a, b, dims):
    return lax.dot_general(a.astype(bf16), b.astype(bf16), (((dims[0],), (dims[1],)), ((), ())), preferred_element_type=f32)


def _bdot_fwd(a, b, dims):
    return _bdot(a, b, dims), (a, b)


def _bdot_bwd(dims, res, g):
    a, b = res
    ca, cb = dims
    da = _bdot(g, b, (1, 1 - cb)) if ca == 1 else _bdot(b, g, (1 - cb, 1))
    db = _bdot(a, g, (1 - ca, 0)) if cb == 0 else _bdot(g, a, (0, 1 - ca))
    return da, db


_bdot.defvjp(_bdot_fwd, _bdot_bwd)


def _split_dot(m, x, dims):
    mb, rem, acc = m.astype(bf16), x, None
    for _ in range(3):
        piece = rem.astype(bf16)
        rem = rem - piece.astype(f32)
        part = lax.dot_general(mb, piece, (((dims[0],), (dims[1],)), ((), ())), preferred_element_type=f32)
        acc = part if acc is None else acc + part
    return acc


@jax.custom_vjp
def _tri_cumsum(tri, g):
    return _split_dot(tri, g, (1, 0))


def _tri_cumsum_fwd(tri, g):
    return _tri_cumsum(tri, g), tri


def _tri_cumsum_bwd(tri, db):
    return jnp.zeros_like(tri), _split_dot(tri, db, (0, 0))


_tri_cumsum.defvjp(_tri_cumsum_fwd, _tri_cumsum_bwd)


def _gla_chunk(St, q, k, v, g, *, rev, scale, exact):
    C, DK = q.shape
    r = lax.broadcasted_iota(jnp.int32, (C, C), 0)
    c = lax.broadcasted_iota(jnp.int32, (C, C), 1)
    causal = (r <= c) if rev else (r >= c)
    b = _tri_cumsum(causal.astype(f32), g)
    qs = q * scale
    qe = qs * jnp.exp(b)
    inter = _bdot(qe, St, (1, 1))
    b_last = b[0:1] if rev else b[C - 1:C]
    kd = k * jnp.exp(b_last - b)
    St_new = St * jnp.exp(b_last) + _bdot(v, kd, (0, 0))
    if not exact:
        att = jnp.where(causal, _bdot(qe, k * jnp.exp(-b), (1, 1)), 0.0)
        return St_new, inter + _bdot(att, v, (1, 0))
    rr = lax.broadcasted_iota(jnp.int32, (SUB, SUB, DK), 0)
    cc = lax.broadcasted_iota(jnp.int32, (SUB, SUB, DK), 1)
    m3 = (rr <= cc) if rev else (rr >= cc)
    outs = []
    for i in range(C // SUB):
        lo, hi = i * SUB, (i + 1) * SUB
        bi, qi, ki, vi = b[lo:hi], qs[lo:hi], k[lo:hi], v[lo:hi]
        rel = bi[:, None, :] - bi[None, :, :]
        e = jnp.where(m3, jnp.exp(jnp.where(m3, rel, 0.0)), 0.0)
        att = jnp.sum(qi[:, None, :] * e * ki[None, :, :], axis=-1)
        acc = _bdot(att, vi, (1, 0))
        ref_row = b[hi - 1:hi] if rev else b[lo:lo + 1]
        prev = slice(hi, C) if rev else slice(0, lo)
        if (hi < C) if rev else (lo > 0):
            qn = qi * jnp.exp(bi - ref_row)
            ks = k[prev] * jnp.exp(ref_row - b[prev])
            acc = acc + _bdot(_bdot(qn, ks, (1, 1)), v[prev], (1, 0))
        outs.append(acc)
    return St_new, inter + jnp.concatenate(outs, axis=0)


def _mild_decay(la_ref):
    return jnp.min(la_ref[...]) >= -GLA_SAFE_DECAY / CHUNK


def _gla_specs(D, rev_blocks, row0, seq):
    DK, DV = D // (2 * HEADS), D // HEADS
    nblk = seq // GLA_ROWS
    rb0 = row0 // GLA_ROWS

    def blk(j):
        return (nblk - 1 - j) if rev_blocks else j

    return DK, DV, nblk, rb0, blk


def _gla_in_specs(D, rev, rows):
    QK = D // 2
    return [
        pl.BlockSpec((GLA_ROWS, QK), lambda b, j: (rows(b, j), 6 * D // QK)),
        pl.BlockSpec((GLA_ROWS, QK), lambda b, j: (rows(b, j), 6 * D // QK + 1)),
        pl.BlockSpec((GLA_ROWS, D), lambda b, j: (rows(b, j), 2)),
        pl.BlockSpec((GLA_ROWS, QK), lambda b, j: (rows(b, j), 1 if rev else 0)),
    ]


def _gla_fwd(p_all, la_all, s0, *, rev, row0, nb, seq, D, name, carry=None):
    DK, DV, nblk, rb0, blk = _gla_specs(D, rev, row0, seq)
    cpb = GLA_ROWS // CHUNK

    def rows(b, j):
        return rb0 + b * nblk + blk(j)

    in_specs = _gla_in_specs(D, rev, rows) + [pl.BlockSpec((1, HEADS, DV, DK), lambda b, j: (b, 0, 0, 0))]
    out_specs = [
        pl.BlockSpec((GLA_ROWS, D), lambda b, j: (b * nblk + blk(j), 0)),
        pl.BlockSpec((1, HEADS, cpb, DV, DK), lambda b, j: (b, 0, blk(j), 0, 0)),
        pl.BlockSpec((1, HEADS, DV, DK), lambda b, j: (b, 0, 0, 0)),
    ]
    out_shape = [
        jax.ShapeDtypeStruct((nb * seq, D), bf16),
        jax.ShapeDtypeStruct((nb, HEADS, seq // CHUNK, DV, DK), bf16),
        jax.ShapeDtypeStruct((nb, HEADS, DV, DK), f32),
    ]
    chunk = functools.partial(_gla_chunk, rev=rev, scale=DK ** -0.5)

    def body(q_ref, k_ref, v_ref, la_ref, s0_ref, o_ref, hist_ref, sfin_ref, st_ref):
        j = pl.program_id(1)

        @pl.when(j == 0)
        def _():
            st_ref[...] = s0_ref[0]

        def step(ci, exact):
            cc = (cpb - 1 - ci) if rev else ci
            sl = pl.ds(cc * CHUNK, CHUNK)
            for h in range(HEADS):
                kq, kv = pl.ds(h * DK, DK), pl.ds(h * DV, DV)
                St = st_ref[h]
                hist_ref[0, h, cc] = St.astype(bf16)
                St2, o = chunk(St, q_ref[sl, kq].astype(f32), k_ref[sl, kq].astype(f32), v_ref[sl, kv].astype(f32), la_ref[sl, kq],
                               exact=exact)
                o_ref[sl, kv] = o.astype(bf16)
                st_ref[h] = St2

        mild = _mild_decay(la_ref)
        for exact in (False, True):
            @pl.when(jnp.logical_not(mild) if exact else mild)
            def _(exact=exact):
                for ci in range(cpb):
                    step(ci, exact)

        @pl.when(j == nblk - 1)
        def _():
            sfin_ref[0] = st_ref[...]

    res, carried = _call(body, name=name, grid=(nb, nblk), in_specs=in_specs, out_specs=out_specs, out_shape=out_shape,
                         scratch_shapes=[pltpu.VMEM((HEADS, DV, DK), f32)], sem=("parallel", "arbitrary"),
                         args=(p_all, p_all, p_all, la_all, s0), carry=carry)
    return res if carry is None else (res, carried)


def _gla_bwd(p_all, la_all, hist, do, dsfin, *, rev, row0, nb, seq, D, name, add=None):
    DK, DV, nblk, rb0, blk = _gla_specs(D, not rev, row0, seq)
    cpb = GLA_ROWS // CHUNK
    QK = HEADS * DK
    has_do = do is not None

    def rows(b, j):
        return rb0 + b * nblk + blk(j)

    in_specs = _gla_in_specs(D, rev, rows) + [
        pl.BlockSpec((1, HEADS, cpb, DV, DK), lambda b, j: (b, 0, blk(j), 0, 0)),
        pl.BlockSpec((1, HEADS, DV, DK), lambda b, j: (b, 0, 0, 0)),
    ]
    args = [p_all, p_all, p_all, la_all, hist, dsfin]
    if has_do:
        in_specs.append(pl.BlockSpec((GLA_ROWS, D), lambda b, j: (b * nblk + blk(j), 0)))
        args.append(do)
    if add is not None:
        in_specs += [pl.BlockSpec((GLA_ROWS, t.shape[1]), lambda b, j: (b * nblk + blk(j), 0)) for t in add]
        args += list(add)
    gdt = f32 if add is None else bf16
    out_specs = [
        pl.BlockSpec((GLA_ROWS, QK), lambda b, j: (b * nblk + blk(j), 0)),
        pl.BlockSpec((GLA_ROWS, QK), lambda b, j: (b * nblk + blk(j), 0)),
        pl.BlockSpec((GLA_ROWS, D), lambda b, j: (b * nblk + blk(j), 0)),
        pl.BlockSpec((GLA_ROWS, QK), lambda b, j: (b * nblk + blk(j), 0)),
        pl.BlockSpec((1, HEADS, DV, DK), lambda b, j: (b, 0, 0, 0)),
    ]
    out_shape = [
        jax.ShapeDtypeStruct((nb * seq, QK), gdt), jax.ShapeDtypeStruct((nb * seq, QK), gdt),
        jax.ShapeDtypeStruct((nb * seq, D), gdt), jax.ShapeDtypeStruct((nb * seq, QK), f32),
        jax.ShapeDtypeStruct((nb, HEADS, DV, DK), f32),
    ]
    chunk = functools.partial(_gla_chunk, rev=rev, scale=DK ** -0.5)

    def body(*refs):
        refs = list(refs)
        q_ref, k_ref, v_ref, la_ref, hist_ref, dsfin_ref = refs[:6]
        do_ref = refs[6] if has_do else None
        add_refs = refs[6 + has_do:len(refs) - 6]
        dq_ref, dk_ref, dv_ref, dla_ref, ds0_ref, ds_ref = refs[len(refs) - 6:]
        j = pl.program_id(1)

        @pl.when(j == 0)
        def _():
            ds_ref[...] = dsfin_ref[0]

        def step(ci, exact):
            cc = ci if rev else (cpb - 1 - ci)
            sl = pl.ds(cc * CHUNK, CHUNK)
            for h in range(HEADS):
                kq, kv = pl.ds(h * DK, DK), pl.ds(h * DV, DV)
                prim = (hist_ref[0, h, cc].astype(f32), q_ref[sl, kq].astype(f32), k_ref[sl, kq].astype(f32), v_ref[sl, kv].astype(f32), la_ref[sl, kq])
                _, vjp = jax.vjp(functools.partial(chunk, exact=exact), *prim)
                d_o = do_ref[sl, kv].astype(f32) if has_do else jnp.zeros((CHUNK, DV), f32)
                dSt, dq, dk, dv, dg = vjp((ds_ref[h], d_o))
                if add is not None:
                    dq, dk, dv = dq + add_refs[0][sl, kq], dk + add_refs[1][sl, kq], dv + add_refs[2][sl, kv]
                dq_ref[sl, kq] = dq.astype(gdt)
                dk_ref[sl, kq] = dk.astype(gdt)
                dv_ref[sl, kv] = dv.astype(gdt)
                dla_ref[sl, kq] = dg
                ds_ref[h] = dSt

        mild = _mild_decay(la_ref)
        for exact in (False, True):
            @pl.when(jnp.logical_not(mild) if exact else mild)
            def _(exact=exact):
                for ci in range(cpb):
                    step(ci, exact)

        @pl.when(j == nblk - 1)
        def _():
            ds0_ref[0] = ds_ref[...]

    return _pc(body, name=name, grid=(nb, nblk), in_specs=in_specs, out_specs=out_specs, out_shape=out_shape,
               scratch_shapes=[pltpu.VMEM((HEADS, DV, DK), f32)], compiler_params=_params("parallel", "arbitrary"))(*args)


def _conv_fwd(p_all, dw_w, dw_b, *, B, L, D, name):
    ct = _pick(D, 256)
    nj = D // ct
    st = _pick(L, 128, 8)
    off = CONV_PAD - CONV_W // 2

    def body(a_ref, b_ref, w_ref, bias_ref, o_ref, zs_ref):
        _fill_shifted(zs_ref, L, lambda t0, n: a_ref[pl.ds(t0, n), :].astype(f32) * jax.nn.sigmoid(b_ref[pl.ds(t0, n), :].astype(f32)))
        for t0 in range(0, L, st):
            acc = jnp.zeros((st, ct), f32) + bias_ref[...]
            for k in range(CONV_W):
                acc = acc + w_ref[pl.ds(k, 1), :] * _window(zs_ref, t0 + k + off, st)
            o_ref[pl.ds(t0, st), :] = acc.astype(bf16)

    return _pc(
        body, name=name, grid=(B, nj),
        in_specs=[pl.BlockSpec((L, ct), lambda b, j: (b, j)), pl.BlockSpec((L, ct), lambda b, j: (b, nj + j)),
                  pl.BlockSpec((CONV_W, ct), lambda b, j: (0, j)), pl.BlockSpec((1, ct), lambda b, j: (0, j))],
        out_specs=pl.BlockSpec((L, ct), lambda b, j: (b, j)), out_shape=jax.ShapeDtypeStruct((B * L, D), bf16),
        scratch_shapes=[pltpu.VMEM((SUBLANES, L + 2 * CONV_PAD, ct), f32)], compiler_params=_params("parallel", "parallel"),
    )(p_all, p_all, dw_w, dw_b)


def _fill_shifted(zs_ref, L, rows):
    lp = L + 2 * CONV_PAD
    ct = zs_ref.shape[2]
    step = 256
    zs_ref[0, pl.ds(0, CONV_PAD), :] = jnp.zeros((CONV_PAD, ct), f32)
    zs_ref[0, pl.ds(CONV_PAD + L, CONV_PAD), :] = jnp.zeros((CONV_PAD, ct), f32)
    for t0 in range(0, L, step):
        n = min(step, L - t0)
        zs_ref[0, pl.ds(CONV_PAD + t0, n), :] = rows(t0, n)
    for r in range(1, SUBLANES):
        for i0 in range(0, lp - SUBLANES, step):
            n = min(step, lp - SUBLANES - i0)
            zs_ref[r, pl.ds(i0, n), :] = zs_ref[0, pl.ds(i0 + r, n), :]


def _window(zs_ref, start, n):
    r = start % SUBLANES
    return zs_ref[r, pl.ds(start - r, n), :]


def _conv_bwd(p_all, dcz, dw_w, *, B, L, D, name, carry=None):
    ct = _pick(D, 128)
    nj = D // ct
    st = _pick(L, 256, 8)
    half = CONV_W // 2

    def body(a_ref, b_ref, dcz_ref, w_ref, da_ref, db_ref, ddw_ref, zs_ref, ds_ref):
        bi = pl.program_id(1)
        _fill_shifted(zs_ref, L, lambda t0, n: a_ref[pl.ds(t0, n), :].astype(f32) * jax.nn.sigmoid(b_ref[pl.ds(t0, n), :].astype(f32)))
        _fill_shifted(ds_ref, L, lambda t0, n: dcz_ref[pl.ds(t0, n), :].astype(f32))

        @pl.when(bi == 0)
        def _():
            ddw_ref[...] = jnp.zeros_like(ddw_ref)

        for t0 in range(0, L, st):
            acc = jnp.zeros((st, ct), f32)
            for k in range(CONV_W):
                acc = acc + w_ref[pl.ds(k, 1), :] * _window(ds_ref, t0 + CONV_PAD + half - k, st)
            a_t = a_ref[pl.ds(t0, st), :].astype(f32)
            sg_t = jax.nn.sigmoid(b_ref[pl.ds(t0, st), :].astype(f32))
            da_ref[pl.ds(t0, st), :] = (acc * sg_t).astype(bf16)
            db_ref[pl.ds(t0, st), :] = (acc * a_t * sg_t * (1.0 - sg_t)).astype(bf16)

        parts = [jnp.zeros((SUBLANES, ct), f32) for _ in range(CONV_W)]
        sw = _pick(L, 64, SUBLANES)
        for t0 in range(0, L, sw):
            dout = dcz_ref[pl.ds(t0, sw), :].astype(f32)
            for k in range(CONV_W):
                prod = dout * _window(zs_ref, t0 + k + CONV_PAD - half, sw)
                for i in range(0, sw, SUBLANES):
                    parts[k] = parts[k] + prod[i:i + SUBLANES]
        for k in range(CONV_W):
            ddw_ref[pl.ds(k, 1), :] += jnp.sum(parts[k], axis=0, keepdims=True)

    res, carried = _call(
        body, name=name, grid=(nj, B),
        in_specs=[pl.BlockSpec((L, ct), lambda j, b: (b, j)), pl.BlockSpec((L, ct), lambda j, b: (b, nj + j)),
                  pl.BlockSpec((L, ct), lambda j, b: (b, j)), pl.BlockSpec((CONV_W, ct), lambda j, b: (0, j))],
        out_specs=[pl.BlockSpec((L, ct), lambda j, b: (b, j)), pl.BlockSpec((L, ct), lambda j, b: (b, j)),
                   pl.BlockSpec((2 * CONV_PAD, ct), lambda j, b: (0, j))],
        out_shape=[jax.ShapeDtypeStruct((B * L, D), bf16), jax.ShapeDtypeStruct((B * L, D), bf16),
                   jax.ShapeDtypeStruct((2 * CONV_PAD, D), f32)],
        scratch_shapes=[pltpu.VMEM((SUBLANES, L + 2 * CONV_PAD, ct), f32), pltpu.VMEM((SUBLANES, L + 2 * CONV_PAD, ct), f32)],
        sem=("parallel", "arbitrary"), args=(p_all, p_all, dcz, dw_w), carry=carry)
    return res if carry is None else (res, carried)


def _exchange(arrs, scatter, name):
    ex = _Exchange(arrs, scatter)
    n = ex.n

    def body(*refs):
        ex.start(refs[:n], refs[n:2 * n], refs[2 * n:])
        ex.finish(refs[:n], refs[n:2 * n], refs[2 * n:])

    res = _pc(body, name=name, in_specs=ex.specs, out_specs=ex.specs, out_shape=ex.out_shape, scratch_shapes=ex.scratch)(*arrs)
    return list(res)


class _Exchange:
    def __init__(self, arrs, scatter):
        self.arrs, self.scatter, self.n = list(arrs), scatter, len(arrs)
        self.out_shape = [jax.ShapeDtypeStruct(((N_DEV,) + a.shape[1:]) if scatter else ((N_DEV,) + a.shape), a.dtype) for a in arrs]
        self.specs = [pl.BlockSpec(memory_space=pl.ANY)] * self.n
        self.scratch = [pltpu.SemaphoreType.DMA((self.n, N_DEV - 1)), pltpu.SemaphoreType.DMA((self.n, N_DEV - 1)),
                        pltpu.SemaphoreType.DMA((self.n,))]

    def _copies(self, ins, outs, sems, landing):
        send_sems, recv_sems, local_sems = sems
        me = 4 * lax.axis_index("x") + 2 * lax.axis_index("y") + lax.axis_index("c")
        if landing:
            local = []
        else:
            local = [pltpu.make_async_copy(ins[a].at[me] if self.scatter else ins[a], outs[a].at[me], local_sems.at[a]) for a in range(self.n)]
        remote = []
        for k in range(1, N_DEV):
            p = (me + (N_DEV - k if landing else k)) % N_DEV
            for a in range(self.n):
                remote.append(pltpu.make_async_remote_copy(
                    src_ref=ins[a].at[p] if self.scatter else ins[a], dst_ref=outs[a].at[p if landing else me],
                    send_sem=send_sems.at[a, k - 1], recv_sem=recv_sems.at[a, k - 1],
                    device_id=(p // 4, (p // 2) % 2, p % 2), device_id_type=MESH))
        return local, remote

    def _gather_plan(self, ins, outs, sems):
        send_sems, recv_sems, local_sems = sems
        x, y, c = lax.axis_index("x"), lax.axis_index("y"), lax.axis_index("c")
        chips = [(1 - x, y), (x, 1 - y), (1 - x, 1 - y)]

        def blk(px, py, pc):
            return 4 * px + 2 * py + pc

        def copy(a, k, block, to, own):
            return pltpu.make_async_remote_copy(
                src_ref=ins[a] if own else outs[a].at[block], dst_ref=outs[a].at[block],
                send_sem=send_sems.at[a, k], recv_sem=recv_sems.at[a, k], device_id=to, device_id_type=MESH)

        me = blk(x, y, c)
        local = [pltpu.make_async_copy(ins[a], outs[a].at[me], local_sems.at[a]) for a in range(self.n)]
        return local, copy, me, (x, y, 1 - c), chips, blk, c

    def start(self, ins, outs, sems):
        if self.scatter:
            local, sends = self._copies(ins, outs, sems, False)
            for cp in local + sends:
                cp.start()
            return
        local, copy, me, sibling, chips, _, c = self._gather_plan(ins, outs, sems)
        for cp in local:
            cp.start()
        for a in range(self.n):
            copy(a, 0, me, sibling, True).start()
            for j, chip in enumerate(chips):
                copy(a, 1 + j, me, (*chip, c), True).start()

    def finish(self, ins, outs, sems):
        if self.scatter:
            for cp in self._copies(ins, outs, sems, True)[1]:
                cp.wait_recv()
            local, sends = self._copies(ins, outs, sems, False)
            for cp in sends:
                cp.wait_send()
            for cp in local:
                cp.wait()
            return
        local, copy, me, sibling, chips, blk, c = self._gather_plan(ins, outs, sems)
        for j, chip in enumerate(chips):
            for a in range(self.n):
                copy(a, 1 + j, blk(*chip, c), sibling, True).wait_recv()
                copy(a, 4 + j, blk(*chip, c), sibling, False).start()
        for a in range(self.n):
            copy(a, 0, blk(*sibling), sibling, True).wait_recv()
            for j, chip in enumerate(chips):
                copy(a, 4 + j, blk(*chip, 1 - c), sibling, False).wait_recv()
        for a in range(self.n):
            copy(a, 0, me, sibling, True).wait_send()
            for j, chip in enumerate(chips):
                copy(a, 1 + j, me, (*chip, c), True).wait_send()
                copy(a, 4 + j, blk(*chip, c), sibling, False).wait_send()
        for cp in local:
            cp.wait()


def _carried(inner, n_in, n_out, grid, ex):
    n = ex.n

    def body(*refs):
        own_in, c_in = refs[:n_in], refs[n_in:n_in + n]
        own_out, c_out = refs[n_in + n:n_in + n + n_out], refs[n_in + n + n_out:n_in + 2 * n + n_out]
        rest = refs[n_in + 2 * n + n_out:]
        own_scr, sems = rest[:len(rest) - 3], rest[len(rest) - 3:]
        pids = [pl.program_id(d) for d in range(len(grid))]
        first = functools.reduce(jnp.logical_and, [p == 0 for p in pids])
        last = functools.reduce(jnp.logical_and, [p == g - 1 for p, g in zip(pids, grid)])

        @pl.when(first)
        def _():
            ex.start(c_in, c_out, sems)

        inner(*own_in, *own_out, *own_scr)

        @pl.when(last)
        def _():
            ex.finish(c_in, c_out, sems)

    return body


def _call(inner, *, name, grid, in_specs, out_specs, out_shape, scratch_shapes, sem, args, carry=None):
    if carry is None:
        res = _pc(inner, name=name, grid=grid, in_specs=in_specs, out_specs=out_specs, out_shape=out_shape,
                  scratch_shapes=scratch_shapes, compiler_params=_params(*sem))(*args)
        return list(res), None
    ex = _Exchange(*carry)
    res = _pc(_carried(inner, len(in_specs), len(out_specs), grid, ex), name=name, grid=grid,
              in_specs=list(in_specs) + ex.specs, out_specs=list(out_specs) + ex.specs, out_shape=list(out_shape) + ex.out_shape,
              scratch_shapes=list(scratch_shapes) + ex.scratch, compiler_params=_params(*(["arbitrary"] * len(grid))))(*args, *ex.arrs)
    res = list(res)
    return res[:len(out_specs)], res[len(out_specs):]


def _mod_fwd(c_all, c_ctx, w_loc, b_loc, name):
    nr, D = c_all.shape
    nc = w_loc.shape[1]

    def body(c_ref, cc_ref, w_ref, b_ref, o_ref):
        a = jnp.concatenate([c_ref[...], jnp.broadcast_to(cc_ref[...], (8, D))], axis=0)
        s = jax.nn.silu(a).astype(bf16)
        o_ref[...] = jnp.dot(s, w_ref[...].astype(bf16), preferred_element_type=f32) + b_ref[...]

    return _pc(body, name=name, out_shape=jax.ShapeDtypeStruct((nr + 8, nc), f32), compiler_params=_params())(c_all, c_ctx, w_loc, b_loc)


def _mod_bwd(c_all, c_ctx, w_loc, dmx_loc, dmc_loc, name):
    nr, D = c_all.shape
    nc = w_loc.shape[1]

    def body(c_ref, cc_ref, w_ref, dmx_ref, dmc_ref, gw_ref, gc_ref):
        cc = cc_ref[...]
        a = jnp.concatenate([c_ref[...], jnp.broadcast_to(cc, (N_DEV, D))], axis=0)
        s = jax.nn.silu(a).astype(bf16)
        g = jnp.concatenate([dmx_ref[...], dmc_ref[...]], axis=0).astype(bf16)
        gw_ref[...] = lax.dot_general(s, g, (((0,), (0,)), ((), ())), preferred_element_type=f32)
        dmc = jnp.sum(dmc_ref[...], axis=0, keepdims=True)
        ds = lax.dot_general(jnp.broadcast_to(dmc, (8, nc)).astype(bf16), w_ref[...].astype(bf16), (((1,), (1,)), ((), ())),
                             preferred_element_type=f32)[0:1]
        sg = jax.nn.sigmoid(cc)
        gc_ref[...] = ds * (sg * (1.0 + cc * (1.0 - sg)))

    return _pc(body, name=name, out_shape=[jax.ShapeDtypeStruct((D, nc), f32), jax.ShapeDtypeStruct((1, D), f32)],
               compiler_params=_params())(c_all, c_ctx, w_loc, dmx_loc, dmc_loc)


def _adamw_math(w, g, m, v):
    m2 = ADAM_B1 * m + (1.0 - ADAM_B1) * g
    v2 = ADAM_B2 * v + (1.0 - ADAM_B2) * jnp.square(g)
    m_hat = m2 / (1.0 - ADAM_B1 ** ADAM_STEP)
    v_hat = v2 / (1.0 - ADAM_B2 ** ADAM_STEP)
    delta = -ADAM_LR * (m_hat / (jnp.sqrt(v_hat) + ADAM_EPS) + ADAM_WD * w)
    return delta, m2, v2


def _adamw_many(params, name):
    n = len(params)

    def body(*refs):
        ins, outs = refs[:4 * n], refs[4 * n:]
        for i in range(n):
            w, m, v, g = (ins[4 * i + k][...] for k in range(4))
            d, m2, v2 = _adamw_math(w, g, m, v)
            outs[3 * i][...] = d
            outs[3 * i + 1][...] = m2
            outs[3 * i + 2][...] = v2

    res = _pc(body, name=name, out_shape=[jax.ShapeDtypeStruct(p[0].shape, f32) for p in params for _ in range(3)],
              compiler_params=_params())(*[a for p in params for a in p])
    return [tuple(res[3 * i:3 * i + 3]) for i in range(n)]


def _adamw(w, m, v, g, name, partials, carry=None):
    r, cdim = w.shape
    tr = _pick(r, 256, 8)

    def body(w_ref, m_ref, v_ref, g_ref, og_ref, od_ref, om_ref, ov_ref):
        if partials:
            g = g_ref[0].astype(f32)
            for s in range(1, N_DEV):
                g = g + g_ref[s].astype(f32)
        else:
            g = g_ref[...]
        d, m2, v2 = _adamw_math(w_ref[...], g, m_ref[...], v_ref[...])
        og_ref[...] = g
        od_ref[...] = d
        om_ref[...] = m2
        ov_ref[...] = v2

    blk = pl.BlockSpec((tr, cdim), lambda i: (i, 0))
    g_spec = pl.BlockSpec((N_DEV, tr, cdim), lambda i: (0, i, 0)) if partials else blk
    res, carried = _call(body, name=name, grid=(r // tr,), in_specs=[blk, blk, blk, g_spec], out_specs=[blk] * 4,
                         out_shape=[jax.ShapeDtypeStruct((r, cdim), f32)] * 4, scratch_shapes=[], sem=("parallel",),
                         args=(w, m, v, g), carry=carry)
    return res if carry is None else (res, carried)


def _sum_sources(parts, name):
    def body(*refs):
        for i_ref, o_ref in zip(refs[:len(parts)], refs[len(parts):]):
            acc = i_ref[0]
            for s in range(1, i_ref.shape[0]):
                acc = acc + i_ref[s]
            o_ref[...] = acc

    return list(_pc(body, name=name, out_shape=[jax.ShapeDtypeStruct(p.shape[1:], f32) for p in parts],
                    compiler_params=_params())(*parts))


def kernel(x, c, ctx, c_ctx, w_mod, b_mod, g_ffn1, w1_gu, w1_down, g_mix, w_in, dw_weight, dw_bias, conv_ln_g, conv_ln_b, w_conv_out, w_alpha_f, b_alpha_f, w_alpha_b, b_alpha_b, gla_norm_g, w_gla_out, w_out, g_ffn2, w2_gu, w2_down, g_final, loss_target, m_c_ctx, m_w_mod, m_b_mod, m_g_ffn1, m_w1_gu, m_w1_down, m_g_mix, m_w_in, m_dw_weight, m_dw_bias, m_conv_ln_g, m_conv_ln_b, m_w_conv_out, m_w_alpha_f, m_b_alpha_f, m_w_alpha_b, m_b_alpha_b, m_gla_norm_g, m_w_gla_out, m_w_out, m_g_ffn2, m_w2_gu, m_w2_down, m_g_final, v_c_ctx, v_w_mod, v_b_mod, v_g_ffn1, v_w1_gu, v_w1_down, v_g_mix, v_w_in, v_dw_weight, v_dw_bias, v_conv_ln_g, v_conv_ln_b, v_w_conv_out, v_w_alpha_f, v_b_alpha_f, v_w_alpha_b, v_b_alpha_b, v_gla_norm_g, v_w_gla_out, v_w_out, v_g_ffn2, v_w2_gu, v_w2_down, v_g_final):
    B, L, D = x.shape
    Lc = ctx.shape[1]
    T, Tc = B * L, B * Lc
    Tall = T + Tc
    F = w1_down.shape[1] * N_DEV
    DK, DV = D // (2 * HEADS), D // HEADS
    QK = HEADS * DK
    PW = 7 * D + LR_PAD
    tm = ROW_TILE
    tpe = L // tm
    nx, nall = T // tm, Tall // tm
    me = 4 * lax.axis_index("x") + 2 * lax.axis_index("y") + lax.axis_index("c")

    rw_all = dict(tm=tm, n_tiles=nall, tpe=tpe, nx_tiles=nx, n_ex=B + 1)
    rw_x = dict(tm=tm, n_tiles=nx, tpe=tpe, nx_tiles=nx, n_ex=B)
    rw_all2, rw_x2 = rw_all, rw_x

    dww_g, waf_g, wab_g, c_g = _exchange([dw_weight[0], w_alpha_f[0], w_alpha_b[0], c], False, "gather_first")

    def cols(gat):
        return jnp.transpose(gat, (1, 0, 2)).reshape(gat.shape[1], N_DEV * gat.shape[2])

    def rows_(gat):
        return gat.reshape(N_DEV * gat.shape[1], gat.shape[2])

    dww = cols(dww_g)
    WA = jnp.zeros((LR_PAD, 2 * QK), f32).at[:LOWRANK, :QK].set(cols(waf_g)).at[LOWRANK:2 * LOWRANK, QK:].set(cols(wab_g)).astype(bf16)
    BA = jnp.concatenate([b_alpha_f, b_alpha_b], axis=1)
    c_all = c_g.reshape(N_DEV * B, D)
    c_ctx2 = c_ctx.reshape(1, D)

    ncm = w_mod.shape[2]
    b_mod_loc = lax.dynamic_slice(b_mod, (0, me * ncm), (1, ncm))
    mod_loc = _mod_fwd(c_all, c_ctx2, w_mod[0], b_mod_loc, "mod_fwd")
    (mod_g,) = _exchange([mod_loc], False, "gather_mod")
    mod_full = cols(mod_g)
    mod_tab = jnp.concatenate([lax.dynamic_slice(mod_full, (me * B, 0), (B, N_MOD * D)), mod_full[N_DEV * B:N_DEV * B + 1]], axis=0)
    mods = [mod_tab[:, i * D:(i + 1) * D].reshape(B + 1, 1, D) for i in range(N_MOD)]
    mods_x = [mm[:B] for mm in mods]

    x_lat, x_ctx = (x.reshape(T, D), D, 0, True), (ctx.reshape(Tc, D), D, 0, "c")

    def f_ffn_in(tok, ex, sh):
        return [_rms_mod(tok[0] + tok[1], sh[0], ex[0], ex[1])], [], []

    (u1,), (w1gu_g,) = _rowwise(f_ffn_in, name="ffn1_in", tok_in=[x_lat, x_ctx], ex_in=[mods[0], mods[1]], sh_in=[g_ffn1],
                                tok_out=[(D, bf16)], carry=([w1_gu[0].astype(bf16)], False), **rw_all)
    W1gu = cols(w1gu_g)
    (gu1, h1), (w1d_g, win_g) = _ffn_up(u1, W1gu, "ffn1_up", carry=([w1_down[0].astype(bf16), w_in[0].astype(bf16)], False))
    W1d = rows_(w1d_g)
    lr2 = 2 * LOWRANK
    segs = [(0, 2 * D, 0), (2 * D, 2 * D + QK, 6 * D), (2 * D + QK, 3 * D, 6 * D + QK), (3 * D, 4 * D, 2 * D), (4 * D, 5 * D, 3 * D),
            (5 * D, 5 * D + lr2, 7 * D), (5 * D + lr2, 6 * D + lr2, 4 * D), (6 * D + lr2, 7 * D + lr2, 5 * D)]
    wc = w_in.shape[2]
    win_parts = []
    for lo, hi, _ in sorted(segs, key=lambda t: t[2]):
        for d in range(N_DEV):
            a0, a1 = max(lo, d * wc), min(hi, (d + 1) * wc)
            if a0 < a1:
                win_parts.append(win_g[d][:, a0 - d * wc:a1 - d * wc])
    Win = jnp.concatenate(win_parts + [jnp.zeros((D, LR_PAD - lr2), bf16)], axis=1)

    def nn(a, w):
        return jnp.dot(a.astype(bf16), w, preferred_element_type=f32)

    def nt(a, w):
        return lax.dot_general(a.astype(bf16), w, (((1,), (1,)), ((), ())), preferred_element_type=f32)

    def mix_in(xv, fv, gate, sh, sc, g):
        x1 = xv + 0.5 * gate * fv
        return x1, _rms_mod(x1, g, sh, sc)

    def f_mix_in(tok, ex, sh):
        f1v = nn(tok[2], sh[1])
        return list(mix_in(tok[0] + tok[1], f1v, ex[0], ex[1], ex[2], sh[0])) + [f1v], [], []

    x1, um, f1 = _rowwise(f_mix_in, name="ffn1_down_mix_in", tok_in=[x_lat, x_ctx, (h1, F, 0, False)],
                          ex_in=[mods[2], mods[3], mods[4]], sh_in=[g_mix, W1d], tok_out=[(D, f32), (D, bf16), (D, bf16)], **rw_all2)
    p_all, (wco_g, wgo_g, wo_g, w2gu_g) = _matmul(
        um, Win, "nn", bf16, "in_proj", tm_cap=1024, tn_cap=2432,
        carry=([w_conv_out[0].astype(bf16), w_gla_out[0].astype(bf16), w_out[0].astype(bf16), w2_gu[0].astype(bf16)], False))
    Wco, Wgo, Wo, W2gu = rows_(wco_g), rows_(wgo_g), rows_(wo_g), cols(w2gu_g)

    def log_decay(lr, wa, ba):
        z = _bdot(lr, wa, (1, 0)) + ba
        return _log_sigmoid(z) / TAU

    def f_decay(tok, ex, sh):
        return [log_decay(tok[0], sh[0], sh[1])], [], []

    lr_blk = (p_all, LR_PAD, 7 * D // LR_PAD, False)
    (la_all,) = _rowwise(f_decay, name="log_decay", tok_in=[lr_blk], sh_in=[WA, BA], tok_out=[(2 * QK, f32)], **rw_all)

    zeros_s = jnp.zeros((B, HEADS, DV, DK), f32)
    gla_c = dict(row0=T, nb=B, seq=Lc, D=D)
    gla_x = dict(row0=0, nb=B, seq=L, D=D)
    _, hist_cf, s_f = _gla_fwd(p_all, la_all, zeros_s, rev=False, name="gla_ctx_f", **gla_c)
    _, hist_cb, s_b = _gla_fwd(p_all, la_all, zeros_s, rev=True, name="gla_ctx_b", **gla_c)
    (o_f, hist_f, _), (w2d_g,) = _gla_fwd(p_all, la_all, s_f, rev=False, name="gla_x_f", carry=([w2_down[0].astype(bf16)], False), **gla_x)
    W2d = rows_(w2d_g)
    o_b, hist_b, _ = _gla_fwd(p_all, la_all, s_b, rev=True, name="gla_x_b", **gla_x)

    cz = _conv_fwd(p_all, dww, dw_bias, B=B, L=L, D=D, name="conv_fwd")

    def ln_silu(z, g, b):
        mu = jnp.mean(z, axis=-1, keepdims=True)
        var = jnp.mean(jnp.square(z - mu), axis=-1, keepdims=True)
        return jax.nn.silu((z - mu) * lax.rsqrt(var + EPS) * g + b)

    def f_ln(tok, ex, sh):
        zc = ln_silu(tok[0].astype(f32), sh[0], sh[1])
        return [zc, nn(zc, sh[2])], [], []

    zc, yc = _rowwise(f_ln, name="conv_ln_out", tok_in=[(cz, D, 0, False)], sh_in=[conv_ln_g, conv_ln_b, Wco],
                      tok_out=[(D, bf16), (D, bf16)], **rw_x)

    def gla_out(of, ob, og, gn):
        return _head_rms(of.astype(f32) + ob.astype(f32), DV) * gn * jax.nn.silu(og.astype(f32))

    def f_gla_out(tok, ex, sh):
        og2 = gla_out(tok[0], tok[1], tok[2], sh[0])
        return [og2, nn(og2, sh[1])], [], []

    og_blk = (p_all, D, 3, False)
    og2, yg = _rowwise(f_gla_out, name="gla_norm_out", tok_in=[(o_f, D, 0, False), (o_b, D, 0, False), og_blk], sh_in=[gla_norm_g, Wgo],
                       tok_out=[(D, bf16), (D, bf16)], **rw_x)

    def merge(ga, gb, ycv, ygv):
        return jax.nn.sigmoid(ga.astype(f32)) * ycv.astype(f32) + jax.nn.sigmoid(gb.astype(f32)) * ygv.astype(f32)

    def f_merge(tok, ex, sh):
        mg = merge(*tok)
        return [mg, nn(mg, sh[0])], [], []

    ga_blk, gb_blk = (p_all, D, 4, False), (p_all, D, 5, False)
    mg, mix = _rowwise(f_merge, name="merge_mix_out", tok_in=[ga_blk, gb_blk, (yc, D, 0, False), (yg, D, 0, False)], sh_in=[Wo],
                       tok_out=[(D, bf16), (D, f32)], **rw_x)

    def ffn2_in(x1v, mixv, g5, sh, sc, g):
        x2 = x1v + g5 * mixv
        return x2, _rms_mod(x2, g, sh, sc)

    def f_ffn2_in(tok, ex, sh):
        return list(ffn2_in(tok[0], tok[1], ex[0], ex[1], ex[2], sh[0])), [], []

    x2, u2 = _rowwise(f_ffn2_in, name="ffn2_in", tok_in=[(x1, D, 0, False), (mix, D, 0, False)], ex_in=[mods_x[5], mods_x[6], mods_x[7]],
                      sh_in=[g_ffn2], tok_out=[(D, f32), (D, bf16)], **rw_x)
    gu2, h2 = _ffn_up(u2, W2gu, "ffn2_up")

    gf2 = g_final.reshape(1, D)

    def head_loss(x2v, f2v, g8, gf, tgt):
        x3 = x2v + 0.5 * g8 * f2v
        y = x3 * lax.rsqrt(jnp.mean(x3 * x3, axis=-1, keepdims=True) + EPS) * gf
        return 0.5 * jnp.sum(jnp.mean(jnp.square(y - tgt), axis=-1))

    def f_head(tok, ex, sh):
        loss, vjp = jax.vjp(lambda a, b_, c_, d_: head_loss(a, b_, c_, d_, tok[2]), tok[0], nn(tok[1], sh[1]), ex[0], sh[0])
        dx3, df2, dg8, dgf = vjp(jnp.ones((), f32))
        return [dx3, df2], [dg8], [dgf, jnp.broadcast_to(loss.reshape(1, 1), (1, 128))]

    dx3, df2, dg8, dgf, loss_p = _rowwise(
        f_head, name="ffn2_down_head", tok_in=[(x2, D, 0, False), (h2, F, 0, False), (loss_target.reshape(T, D), D, 0, False)],
        ex_in=[mods_x[8]], sh_in=[gf2, W2d], tok_out=[(D, f32), (D, bf16)], ex_out=[D], gl_out=[(1, D), (1, 128)], **rw_x2)

    dgu2 = _ffn_down_dx(df2, W2d, gu2, "ffn2_down_dx")
    gW2d = _matmul(h2, df2, "tn", f32, "ffn2_down_dw", tm_cap=1408)
    du2 = _matmul(dgu2, W2gu, "nt", bf16, "ffn2_up_dx", halves="a")
    gW2gu = _matmul(u2, dgu2, "tn", f32, "ffn2_up_dw", halves="b")

    def f_ffn2_in_bwd(tok, ex, sh):
        _, vjp = jax.vjp(ffn2_in, tok[0], tok[1], ex[0], ex[1], ex[2], sh[0])
        dx2, dmix, dg5, dsh, dsc, dg = vjp((tok[3], tok[2].astype(f32)))
        return [dx2, dmix], [dg5, dsh, dsc], [dg]

    dx2, dmix, dg5, dsh6, dsc7, dg_ffn2 = _rowwise(
        f_ffn2_in_bwd, name="ffn2_in_bwd", tok_in=[(x1, D, 0, False), (mix, D, 0, False), (du2, D, 0, False), (dx3, D, 0, False)],
        ex_in=[mods_x[5], mods_x[6], mods_x[7]], sh_in=[g_ffn2], tok_out=[(D, f32), (D, bf16)], ex_out=[D, D, D], gl_out=[(1, D)], **rw_x)

    gWo = _matmul(mg, dmix, "tn", f32, "mix_out_dw")

    def f_merge_bwd(tok, ex, sh):
        _, vjp = jax.vjp(merge, *[t.astype(f32) for t in tok[:4]])
        dga, dgb, dyc, dyg = vjp(nt(tok[4], sh[0]))
        return [dga, dgb, dyc, dyg], [], []

    dga, dgb, dyc, dyg = _rowwise(f_merge_bwd, name="mix_out_merge_bwd",
                                  tok_in=[ga_blk, gb_blk, (yc, D, 0, False), (yg, D, 0, False), (dmix, D, 0, False)], sh_in=[Wo],
                                  tok_out=[(D, bf16)] * 4, **rw_x)
    gWco = _matmul(zc, dyc, "tn", f32, "conv_out_dw")
    gWgo = _matmul(og2, dyg, "tn", f32, "gla_out_dw")

    def f_ln_bwd(tok, ex, sh):
        _, vjp = jax.vjp(ln_silu, tok[0].astype(f32), sh[0], sh[1])
        dcz, dg, db = vjp(nt(tok[1], sh[2]))
        return [dcz], [], [dg, db, jnp.sum(dcz, axis=0, keepdims=True)]

    dcz, g_ln_g, g_ln_b, g_dwb = _rowwise(f_ln_bwd, name="conv_out_ln_bwd", tok_in=[(cz, D, 0, False), (dyc, D, 0, False)],
                                          sh_in=[conv_ln_g, conv_ln_b, Wco], tok_out=[(D, bf16)], gl_out=[(1, D)] * 3, **rw_x)
    def col_shards(g):
        return jnp.transpose(g.reshape(g.shape[0], N_DEV, g.shape[1] // N_DEV), (1, 0, 2)).astype(bf16)

    def row_shards(g):
        return g.reshape(N_DEV, g.shape[0] // N_DEV, g.shape[1]).astype(bf16)

    (dca, dcb, g_dww), (r_w2d, r_w2gu, r_wo, r_wco, r_wgo) = _conv_bwd(
        p_all, dcz, dww, B=B, L=L, D=D, name="conv_bwd",
        carry=([row_shards(gW2d), col_shards(gW2gu), row_shards(gWo), row_shards(gWco), row_shards(gWgo)], True))

    def f_gla_out_bwd(tok, ex, sh):
        _, vjp = jax.vjp(gla_out, tok[0].astype(f32), tok[1].astype(f32), tok[2].astype(f32), sh[0])
        dof, _, dog, dgn = vjp(nt(tok[3], sh[1]))
        return [dof, dog], [], [dgn]

    d_o, dog, g_gn = _rowwise(f_gla_out_bwd, name="gla_out_norm_bwd",
                              tok_in=[(o_f, D, 0, False), (o_b, D, 0, False), og_blk, (dyg, D, 0, False)], sh_in=[gla_norm_g, Wgo],
                              tok_out=[(D, bf16), (D, bf16)], gl_out=[(1, D)], **rw_x)

    dq_f, dk_f, dv_f, dla_f, ds_f = _gla_bwd(p_all, la_all, hist_f, d_o, zeros_s, rev=False, name="gla_x_f_bwd", **gla_x)
    dq, dk, dv, dla_b, ds_b = _gla_bwd(p_all, la_all, hist_b, d_o, zeros_s, rev=True, name="gla_x_b_bwd", add=(dq_f, dk_f, dv_f), **gla_x)
    dq_cf, dk_cf, dv_cf, dla_cf, _ = _gla_bwd(p_all, la_all, hist_cf, None, ds_f, rev=False, name="gla_ctx_f_bwd", **gla_c)
    _, dk_c, dv_c, dla_cb, _ = _gla_bwd(p_all, la_all, hist_cb, None, ds_b, rev=True, name="gla_ctx_b_bwd", add=(dq_cf, dk_cf, dv_cf), **gla_c)

    dla_all = jnp.concatenate([jnp.concatenate([dla_f, dla_b], axis=1), jnp.concatenate([dla_cf, dla_cb], axis=1)], axis=0)

    def f_decay_bwd(tok, ex, sh):
        _, vjp = jax.vjp(log_decay, tok[0].astype(f32), sh[0].astype(f32), sh[1])
        dlr, dwa, dba = vjp(tok[1])
        return [dlr], [], [dwa, dba]

    dlr, g_WA, g_BA = _rowwise(f_decay_bwd, name="log_decay_bwd", tok_in=[lr_blk, (dla_all, 2 * QK, 0, False)], sh_in=[WA, BA],
                               tok_out=[(LR_PAD, bf16)], gl_out=[(LR_PAD, 2 * QK), (1, 2 * QK)], **rw_all)

    zc_ = functools.partial(jnp.zeros, dtype=bf16)
    dp_x = jnp.concatenate([dca, dcb, dv, dog, dga, dgb, dq, dk, dlr[:T]], axis=1)
    dp_c = jnp.concatenate([zc_((Tc, 2 * D)), dv_c, zc_((Tc, 3 * D)), zc_((Tc, QK)), dk_c, dlr[T:]], axis=1)
    dp_all = jnp.concatenate([dp_x, dp_c], axis=0)
    gWin_p = _matmul(um, dp_all, "tn", f32, "in_proj_dw", tm_cap=512, tn_cap=2432)
    gwin_shards = []
    for d in range(N_DEV):
        parts = []
        for lo, hi, po in segs:
            a0, a1 = max(lo, d * wc), min(hi, (d + 1) * wc)
            if a0 < a1:
                parts.append(gWin_p[:, po + a0 - lo:po + a1 - lo])
        gwin_shards.append(jnp.concatenate(parts, axis=1))
    dum, (r_win,) = _matmul(dp_all, Win, "nt", bf16, "in_proj_dx", tk_cap=2432, carry=([jnp.stack(gwin_shards).astype(bf16)], True))

    def f_mix_in_bwd(tok, ex, sh):
        _, vjp = jax.vjp(mix_in, tok[0] + tok[1], tok[2].astype(f32), ex[0], ex[1], ex[2], sh[0])
        dx1, df1, dgate, dsh, dsc, dg = vjp((tok[4], tok[3].astype(f32)))
        return [dx1, df1], [dgate, dsh, dsc], [dg]

    dx1, df1, dg2, dsh3, dsc4, dg_mix = _rowwise(
        f_mix_in_bwd, name="mix_in_bwd", tok_in=[x_lat, x_ctx, (f1, D, 0, False), (dum, D, 0, False), (dx2, D, 0, True)],
        ex_in=[mods[2], mods[3], mods[4]], sh_in=[g_mix], tok_out=[(D, f32), (D, bf16)], ex_out=[D, D, D], gl_out=[(1, D)], **rw_all)

    dgu1 = _ffn_down_dx(df1, W1d, gu1, "ffn1_down_dx")
    gW1d = _matmul(h1, df1, "tn", f32, "ffn1_down_dw", tm_cap=1408)
    gW1gu, (r_w1d,) = _matmul(u1, dgu1, "tn", f32, "ffn1_up_dw", carry=([row_shards(gW1d)], True), halves="b")
    du1, (r_w1gu,) = _matmul(dgu1, W1gu, "nt", bf16, "ffn1_up_dx", carry=([col_shards(gW1gu)], True), halves="a")

    def f_ffn_in_bwd(tok, ex, sh):
        _, vjp = jax.vjp(_rms_mod, tok[0] + tok[1], sh[0], ex[0], ex[1])
        dx, dg, dsh, dsc = vjp(tok[2].astype(f32))
        return [dx + tok[3]], [dsh, dsc], [dg]

    dx_lat, dsh0, dsc1, dg_ffn1 = _rowwise(
        f_ffn_in_bwd, name="ffn1_in_bwd", tok_in=[x_lat, x_ctx, (du1, D, 0, False), (dx1, D, 0, False)],
        ex_in=[mods[0], mods[1]], sh_in=[g_ffn1], tok_out=[(D, f32, "x")], ex_out=[D, D], gl_out=[(1, D)], **rw_all)
    grad_x = dx_lat.reshape(B, L, D)

    zrow = jnp.zeros((1, 1, D), f32)
    dmod_loc = jnp.concatenate([dsh0, dsc1, dg2, dsh3, dsc4] + [jnp.concatenate([t, zrow], axis=0) for t in (dg5, dsh6, dsc7, dg8)],
                               axis=2).reshape(B + 1, N_MOD * D)
    rows16 = jnp.concatenate([jnp.concatenate([loss_p, jnp.zeros((1, D - loss_p.shape[1]), f32)], axis=1), dg_ffn1, dg_mix, g_dwb, g_ln_g,
                              g_ln_b, g_BA, g_gn, dg_ffn2, dgf, jnp.zeros((6, D), f32)], axis=0)

    def to8(v):
        n_pad = -(-v.shape[1] // 1024) * 1024
        return jnp.pad(v, ((0, 0), (0, n_pad - v.shape[1]))).reshape(8, n_pad // 8)

    def from8(a, n):
        return a.reshape(1, a.size)[:, :n]

    def adam_big(nm, wv, mv, vv, part, carry=None):
        out = _adamw(wv[0], mv[0], vv[0], part, "adamw_" + nm, True, carry=carry)
        res4, carried = out if carry is not None else (out, None)
        return [t[None] for t in res4], carried

    rs_out = dict(w1_gu=r_w1gu, w1_down=r_w1d, w_in=r_win, w_conv_out=r_wco, w_gla_out=r_wgo, w_out=r_wo, w2_gu=r_w2gu, w2_down=r_w2d)
    big = {}
    big["w_in"], (dmod_g, rows_g, dww_sg, wa_sg) = adam_big(
        "w_in", w_in, m_w_in, v_w_in, rs_out["w_in"], carry=([dmod_loc, rows16, g_dww, g_WA[:2 * LOWRANK]], False))
    dmx = dmod_g[:, :B].reshape(N_DEV * B, N_MOD * D)
    dmc = dmod_g[:, B]
    gWmod, gcc_p = _mod_bwd(c_all, c_ctx2, w_mod[0], lax.dynamic_slice(dmx, (0, me * ncm), (N_DEV * B, ncm)),
                            lax.dynamic_slice(dmc, (0, me * ncm), (N_DEV, ncm)), "mod_bwd")

    big["w1_gu"], (gcc_g,) = adam_big("w1_gu", w1_gu, m_w1_gu, v_w1_gu, rs_out["w1_gu"], carry=([to8(gcc_p)], False))
    for nm, wv, mv, vv in (("w1_down", w1_down, m_w1_down, v_w1_down), ("w_conv_out", w_conv_out, m_w_conv_out, v_w_conv_out),
                           ("w_gla_out", w_gla_out, m_w_gla_out, v_w_gla_out), ("w_out", w_out, m_w_out, v_w_out),
                           ("w2_gu", w2_gu, m_w2_gu, v_w2_gu), ("w2_down", w2_down, m_w2_down, v_w2_down)):
        big[nm], _ = adam_big(nm, wv, mv, vv, rs_out[nm])
    big["w_mod"] = [t[None] for t in _adamw(w_mod[0], m_w_mod[0], v_w_mod[0], gWmod, "adamw_w_mod", False)]

    rows_s, dww_s, wa_s, g_cc, g_bmod = _sum_sources(
        [rows_g, dww_sg, wa_sg, gcc_g, jnp.concatenate([dmx, dmc], axis=0).reshape(N_DEV * (B + 1), 8, N_MOD * D // 8)], "sum_small")
    g_cc, g_bmod = from8(g_cc, D), from8(g_bmod, N_MOD * D)
    loss = rows_s[0, 0]
    ncd, nca = dw_weight.shape[2], w_alpha_f.shape[2]
    g_dww_loc = lax.dynamic_slice(dww_s, (0, me * ncd), (CONV_W, ncd))
    g_waf_loc = lax.dynamic_slice(wa_s, (0, me * nca), (LOWRANK, nca))
    g_wab_loc = lax.dynamic_slice(wa_s, (LOWRANK, QK + me * nca), (LOWRANK, nca))
    sm = {k: rows_s[i:i + 1] for i, k in enumerate(["loss", "g_ffn1", "g_mix", "dw_bias", "conv_ln_g", "conv_ln_b", "b_alpha", "gla_norm_g",
                                                     "g_ffn2", "g_final"])}

    small_params = [("c_ctx", c_ctx, m_c_ctx, v_c_ctx, g_cc), ("b_mod", b_mod, m_b_mod, v_b_mod, g_bmod),
                    ("g_ffn1", g_ffn1, m_g_ffn1, v_g_ffn1, sm["g_ffn1"]), ("g_mix", g_mix, m_g_mix, v_g_mix, sm["g_mix"]),
                    ("dw_weight", dw_weight, m_dw_weight, v_dw_weight, g_dww_loc), ("dw_bias", dw_bias, m_dw_bias, v_dw_bias, sm["dw_bias"]),
                    ("conv_ln_g", conv_ln_g, m_conv_ln_g, v_conv_ln_g, sm["conv_ln_g"]),
                    ("conv_ln_b", conv_ln_b, m_conv_ln_b, v_conv_ln_b, sm["conv_ln_b"]),
                    ("w_alpha_f", w_alpha_f, m_w_alpha_f, v_w_alpha_f, g_waf_loc), ("b_alpha_f", b_alpha_f, m_b_alpha_f, v_b_alpha_f, sm["b_alpha"][:, :QK]),
                    ("w_alpha_b", w_alpha_b, m_w_alpha_b, v_w_alpha_b, g_wab_loc), ("b_alpha_b", b_alpha_b, m_b_alpha_b, v_b_alpha_b, sm["b_alpha"][:, QK:]),
                    ("gla_norm_g", gla_norm_g, m_gla_norm_g, v_gla_norm_g, sm["gla_norm_g"]),
                    ("g_ffn2", g_ffn2, m_g_ffn2, v_g_ffn2, sm["g_ffn2"]), ("g_final", g_final, m_g_final, v_g_final, sm["g_final"])]

    def two_d(t, like):
        return t.reshape(like.shape[1:]) if like.ndim == 3 else t.reshape(like.size // 128, 128)

    small_res = _adamw_many([tuple(two_d(t, wv) for t in (wv, mv, vv, gv)) for _, wv, mv, vv, gv in small_params], "adamw_small")
    small_out = {nm: [gv.reshape(wv.shape)] + [t.reshape(wv.shape) for t in r3]
                 for (nm, wv, _, _, gv), r3 in zip(small_params, small_res)}

    order = ["c_ctx", "w_mod", "b_mod", "g_ffn1", "w1_gu", "w1_down", "g_mix", "w_in", "dw_weight", "dw_bias", "conv_ln_g", "conv_ln_b",
             "w_conv_out", "w_alpha_f", "b_alpha_f", "w_alpha_b", "b_alpha_b", "gla_norm_g", "w_gla_out", "w_out", "g_ffn2", "w2_gu",
             "w2_down", "g_final"]
    res = {**big, **small_out}
    return (loss, grad_x, *[res[n][0] for n in order], *[res[n][1] for n in order], *[res[n][2] for n in order], *[res[n][3] for n in order])
```

```python
import functools

import jax
import jax.numpy as jnp
from jax import lax
from jax.experimental import pallas as pl
from jax.experimental.pallas import tpu as pltpu

f32, bf16 = jnp.float32, jnp.bfloat16

N_DEV = 8
HEADS = 4
LOWRANK = 16
CONV_W = 31
CONV_PAD = 16
SUBLANES = 8
CHUNK = 64
SUB = 16
GLA_ROWS = 256
GLA_SAFE_DECAY = 60.0
TAU = 16.0
EPS = 1e-6
N_MOD = 9
LR_PAD = 128
ROW_TILE = 512
V7X_VMEM_BYTES = 64 << 20
VMEM_LIMIT = (V7X_VMEM_BYTES * 7) // 8

ADAM_LR, ADAM_B1, ADAM_B2, ADAM_EPS, ADAM_WD, ADAM_STEP = 0.001, 0.9, 0.999, 1e-08, 0.01, 10

MESH = pl.DeviceIdType.MESH


def _pc(body, **kw):
    return pl.pallas_call(body, **kw)


def _params(*sem):
    return pltpu.CompilerParams(dimension_semantics=sem, vmem_limit_bytes=VMEM_LIMIT)


def _pick(n, cap, unit=128):
    best = None
    for t in range(unit, min(n, cap) + 1, unit):
        if n % t == 0:
            best = t
    return best or n


def _matmul(a, b, mode, out_dtype, name, tm_cap=1024, tn_cap=1536, tk_cap=None, carry=None, halves=None):
    tk_cap = tk_cap or (2048 if mode == "tn" else 2816)
    if halves == "a":
        (_, M, Kh), N = a.shape, b.shape[0]
        K, tk = 2 * Kh, _pick(Kh, tk_cap)
        tm, tn = _pick(M, tm_cap), _pick(N, tn_cap)
        a_spec = pl.BlockSpec((None, tm, tk), lambda i, j, k: (k // (Kh // tk), i, k % (Kh // tk)))
    elif halves == "b":
        (K, M), (_, _, Nh) = a.shape, b.shape
        N, tn = 2 * Nh, _pick(Nh, tn_cap)
        tm, tk = _pick(M, tm_cap), _pick(K, tk_cap)
    else:
        if mode == "tn":
            (K, M), N = a.shape, b.shape[1]
        elif mode == "nt":
            (M, K), N = a.shape, b.shape[0]
        else:
            (M, K), N = a.shape, b.shape[1]
        tm, tn, tk = _pick(M, tm_cap), _pick(N, tn_cap), _pick(K, tk_cap)
    nk = K // tk
    if halves != "a":
        a_spec = pl.BlockSpec((tk, tm), lambda i, j, k: (k, i)) if mode == "tn" else pl.BlockSpec((tm, tk), lambda i, j, k: (i, k))
    if halves == "b":
        b_spec = pl.BlockSpec((None, tk, tn), lambda i, j, k: (j // (Nh // tn), k, j % (Nh // tn)))
    else:
        b_spec = pl.BlockSpec((tn, tk), lambda i, j, k: (j, k)) if mode == "nt" else pl.BlockSpec((tk, tn), lambda i, j, k: (k, j))
    dims = {"nn": ((1,), (0,)), "nt": ((1,), (1,)), "tn": ((0,), (0,))}[mode]

    def body_single(a_ref, b_ref, o_ref):
        o_ref[...] = lax.dot_general(a_ref[...].astype(bf16), b_ref[...].astype(bf16), (dims, ((), ())),
                                     preferred_element_type=f32).astype(out_dtype)

    def body(a_ref, b_ref, o_ref, acc_ref):
        k = pl.program_id(2)
        part = lax.dot_general(a_ref[...].astype(bf16), b_ref[...].astype(bf16), (dims, ((), ())), preferred_element_type=f32)

        @pl.when(k == 0)
        def _():
            acc_ref[...] = part

        @pl.when(k > 0)
        def _():
            acc_ref[...] += part

        @pl.when(k == nk - 1)
        def _():
            o_ref[...] = acc_ref[...].astype(out_dtype)

    (out,), carried = _call(
        body_single if nk == 1 else body, name=name, grid=(M // tm, N // tn, nk), in_specs=[a_spec, b_spec],
        out_specs=[pl.BlockSpec((tm, tn), lambda i, j, k: (i, j))], out_shape=[jax.ShapeDtypeStruct((M, N), out_dtype)],
        scratch_shapes=[] if nk == 1 else [pltpu.VMEM((tm, tn), f32)], sem=("parallel", "parallel", "arbitrary"),
        args=(a, b), carry=carry)
    return out if carry is None else (out, carried)


def _ffn_up(u, Wgu, name, carry=None):
    M, K = u.shape
    F = Wgu.shape[1] // 2
    tm, tn = _pick(M, 1024), _pick(F, 1408)
    nj = F // tn

    def body(u_ref, wa_ref, wb_ref, gu_ref, h_ref):
        uv = u_ref[...]
        a = jnp.dot(uv, wa_ref[...], preferred_element_type=f32)
        b = jnp.dot(uv, wb_ref[...], preferred_element_type=f32)
        gu_ref[0] = a.astype(bf16)
        gu_ref[1] = b.astype(bf16)
        h_ref[...] = (jax.nn.silu(a) * b).astype(bf16)

    res, carried = _call(
        body, name=name, grid=(nj, M // tm),
        in_specs=[pl.BlockSpec((tm, K), lambda j, i: (i, 0)), pl.BlockSpec((K, tn), lambda j, i: (0, j)),
                  pl.BlockSpec((K, tn), lambda j, i: (0, nj + j))],
        out_specs=[pl.BlockSpec((2, tm, tn), lambda j, i: (0, i, j)), pl.BlockSpec((tm, tn), lambda j, i: (i, j))],
        out_shape=[jax.ShapeDtypeStruct((2, M, F), bf16), jax.ShapeDtypeStruct((M, F), bf16)],
        scratch_shapes=[], sem=("parallel", "parallel"), args=(u, Wgu, Wgu), carry=carry)
    return res if carry is None else (res, carried)


def _ffn_down_dx(df, Wd, gu, name):
    M, D = df.shape
    F = Wd.shape[0]
    tm, tn = _pick(M, 512), _pick(F, 1408)

    def body(df_ref, w_ref, gu_ref, o_ref):
        dh = lax.dot_general(df_ref[...], w_ref[...], (((1,), (1,)), ((), ())), preferred_element_type=f32)
        a, b = gu_ref[0].astype(f32), gu_ref[1].astype(f32)
        sg = jax.nn.sigmoid(a)
        o_ref[0] = (dh * b * sg * (1.0 + a * (1.0 - sg))).astype(bf16)
        o_ref[1] = (dh * a * sg).astype(bf16)

    return _pc(
        body, name=name, grid=(F // tn, M // tm),
        in_specs=[pl.BlockSpec((tm, D), lambda j, i: (i, 0)), pl.BlockSpec((tn, D), lambda j, i: (j, 0)),
                  pl.BlockSpec((2, tm, tn), lambda j, i: (0, i, j))],
        out_specs=pl.BlockSpec((2, tm, tn), lambda j, i: (0, i, j)), out_shape=jax.ShapeDtypeStruct((2, M, F), bf16),
        compiler_params=_params("parallel", "parallel"))(df, Wd, gu)


def _rowwise(fn, *, name, tm, n_tiles, tpe, nx_tiles, n_ex, tok_in=(), ex_in=(), sh_in=(), tok_out=(), ex_out=(), gl_out=(), carry=None):
    def seg(i):
        return jnp.minimum(i // tpe, n_ex - 1)

    in_specs, args = [], []
    for arr, w, cb, x_only in tok_in:
        if x_only == "c":
            in_specs.append(pl.BlockSpec((tm, w), functools.partial(lambda i, cb: (jnp.maximum(i - nx_tiles, 0), cb), cb=cb)))
        elif x_only:
            in_specs.append(pl.BlockSpec((tm, w), functools.partial(lambda i, cb: (jnp.minimum(i, nx_tiles - 1), cb), cb=cb)))
        else:
            in_specs.append(pl.BlockSpec((tm, w), functools.partial(lambda i, cb: (i, cb), cb=cb)))
        args.append(arr)
    for arr in ex_in:
        in_specs.append(pl.BlockSpec((1, 1, arr.shape[-1]), lambda i: (seg(i), 0, 0)))
        args.append(arr)
    for arr in sh_in:
        in_specs.append(pl.BlockSpec(arr.shape, functools.partial(lambda i, nd: (0,) * nd, nd=arr.ndim)))
        args.append(arr)
    out_specs, out_shape = [], []
    for w, dt, *x_rows in tok_out:
        if x_rows:
            out_specs.append(pl.BlockSpec((tm, w), lambda i: (jnp.minimum(i, nx_tiles - 1), 0)))
        else:
            out_specs.append(pl.BlockSpec((tm, w), lambda i: (i, 0)))
        out_shape.append(jax.ShapeDtypeStruct(((nx_tiles if x_rows else n_tiles) * tm, w), dt))
    for w in ex_out:
        out_specs.append(pl.BlockSpec((1, 1, w), lambda i: (seg(i), 0, 0)))
        out_shape.append(jax.ShapeDtypeStruct((n_ex, 1, w), f32))
    for r, w in gl_out:
        out_specs.append(pl.BlockSpec((r, w), lambda i: (0, 0)))
        out_shape.append(jax.ShapeDtypeStruct((r, w), f32))
    n_tok, n_exi, n_sh = len(tok_in), len(ex_in), len(sh_in)
    n_to, n_eo = len(tok_out), len(ex_out)
    x_only_flags = [t[3] for t in tok_in]
    x_rows_flags = [len(t) > 2 for t in tok_out]

    def body(*refs):
        i = pl.program_id(0)
        ins, outs = refs[: n_tok + n_exi + n_sh], refs[n_tok + n_exi + n_sh:]
        is_x = i < nx_tiles
        tok_vals = []
        for r, xo in zip(ins[:n_tok], x_only_flags):
            v = r[...]
            if xo == "c":
                v = jnp.where(is_x, jnp.zeros_like(v), v)
            elif xo:
                v = jnp.where(is_x, v, jnp.zeros_like(v))
            tok_vals.append(v)
        ex_vals = [r[0] for r in ins[n_tok:n_tok + n_exi]]
        sh_vals = [r[...] for r in ins[n_tok + n_exi:]]
        t_o, e_o, g_o = fn(tok_vals, ex_vals, sh_vals)
        for r, v, xr in zip(outs[:n_to], t_o, x_rows_flags):
            if xr:
                @pl.when(is_x)
                def _(r=r, v=v):
                    r[...] = v.astype(r.dtype)
            else:
                r[...] = v.astype(r.dtype)
        first = jnp.logical_and(i % tpe == 0, i <= nx_tiles)
        for r, v in zip(outs[n_to:n_to + n_eo], e_o):
            @pl.when(first)
            def _(r=r, v=v):
                r[0] = v

            @pl.when(jnp.logical_not(first))
            def _(r=r, v=v):
                r[0] += v
        for r, v in zip(outs[n_to + n_eo:], g_o):
            @pl.when(i == 0)
            def _(r=r, v=v):
                r[...] = v

            @pl.when(i > 0)
            def _(r=r, v=v):
                r[...] += v

    res, carried = _call(body, name=name, grid=(n_tiles,), in_specs=in_specs, out_specs=out_specs, out_shape=out_shape,
                         scratch_shapes=[], sem=("arbitrary",), args=args, carry=carry)
    return res if carry is None else (res, carried)


def _rms_mod(x, g, sh, sc):
    y = x * lax.rsqrt(jnp.mean(x * x, axis=-1, keepdims=True) + EPS) * g
    return y * (1.0 + sc) + sh


def _log_sigmoid(z):
    return jnp.minimum(z, 0.0) - jnp.log(1.0 + jnp.exp(-jnp.abs(z)))


def _head_rms(o, DV):
    parts = []
    for h in range(HEADS):
        oh = o[:, h * DV:(h + 1) * DV]
        parts.append(oh * lax.rsqrt(jnp.mean(oh * oh, axis=-1, keepdims=True) + EPS))
    return jnp.concatenate(parts, axis=1)


@functools.partial(jax.custom_vjp, nondiff_argnums=(2,))
def _bdot(a, b, dims):
    return lax.dot_general(a.astype(bf16), b.astype(bf16), (((dims[0],), (dims[1],)), ((), ())), preferred_element_type=f32)


def _bdot_fwd(a, b, dims):
    return _bdot(a, b, dims), (a, b)


def _bdot_bwd(dims, res, g):
    a, b = res
    ca, cb = dims
    da = _bdot(g, b, (1, 1 - cb)) if ca == 1 else _bdot(b, g, (1 - cb, 1))
    db = _bdot(a, g, (1 - ca, 0)) if cb == 0 else _bdot(g, a, (0, 1 - ca))
    return da, db


_bdot.defvjp(_bdot_fwd, _bdot_bwd)


def _split_dot(m, x, dims):
    mb, rem, acc = m.astype(bf16), x, None
    for _ in range(3):
        piece = rem.astype(bf16)
        rem = rem - piece.astype(f32)
        part = lax.dot_general(mb, piece, (((dims[0],), (dims[1],)), ((), ())), preferred_element_type=f32)
        acc = part if acc is None else acc + part
    return acc


@jax.custom_vjp
def _tri_cumsum(tri, g):
    return _split_dot(tri, g, (1, 0))


def _tri_cumsum_fwd(tri, g):
    return _tri_cumsum(tri, g), tri


def _tri_cumsum_bwd(tri, db):
    return jnp.zeros_like(tri), _split_dot(tri, db, (0, 0))


_tri_cumsum.defvjp(_tri_cumsum_fwd, _tri_cumsum_bwd)


def _gla_chunk(St, q, k, v, g, *, rev, scale, exact):
    C, DK = q.shape
    r = lax.broadcasted_iota(jnp.int32, (C, C), 0)
    c = lax.broadcasted_iota(jnp.int32, (C, C), 1)
    causal = (r <= c) if rev else (r >= c)
    b = _tri_cumsum(causal.astype(f32), g)
    qs = q * scale
    qe = qs * jnp.exp(b)
    inter = _bdot(qe, St, (1, 1))
    b_last = b[0:1] if rev else b[C - 1:C]
    kd = k * jnp.exp(b_last - b)
    St_new = St * jnp.exp(b_last) + _bdot(v, kd, (0, 0))
    if not exact:
        att = jnp.where(causal, _bdot(qe, k * jnp.exp(-b), (1, 1)), 0.0)
        return St_new, inter + _bdot(att, v, (1, 0))
    rr = lax.broadcasted_iota(jnp.int32, (SUB, SUB, DK), 0)
    cc = lax.broadcasted_iota(jnp.int32, (SUB, SUB, DK), 1)
    m3 = (rr <= cc) if rev else (rr >= cc)
    outs = []
    for i in range(C // SUB):
        lo, hi = i * SUB, (i + 1) * SUB
        bi, qi, ki, vi = b[lo:hi], qs[lo:hi], k[lo:hi], v[lo:hi]
        rel = bi[:, None, :] - bi[None, :, :]
        e = jnp.where(m3, jnp.exp(jnp.where(m3, rel, 0.0)), 0.0)
        att = jnp.sum(qi[:, None, :] * e * ki[None, :, :], axis=-1)
        acc = _bdot(att, vi, (1, 0))
        ref_row = b[hi - 1:hi] if rev else b[lo:lo + 1]
        prev = slice(hi, C) if rev else slice(0, lo)
        if (hi < C) if rev else (lo > 0):
            qn = qi * jnp.exp(bi - ref_row)
            ks = k[prev] * jnp.exp(ref_row - b[prev])
            acc = acc + _bdot(_bdot(qn, ks, (1, 1)), v[prev], (1, 0))
        outs.append(acc)
    return St_new, inter + jnp.concatenate(outs, axis=0)


def _mild_decay(la_ref):
    return jnp.min(la_ref[...]) >= -GLA_SAFE_DECAY / CHUNK


def _gla_specs(D, rev_blocks, row0, seq):
    DK, DV = D // (2 * HEADS), D // HEADS
    nblk = seq // GLA_ROWS
    rb0 = row0 // GLA_ROWS

    def blk(j):
        return (nblk - 1 - j) if rev_blocks else j

    return DK, DV, nblk, rb0, blk


def _gla_in_specs(D, rev, rows):
    QK = D // 2
    return [
        pl.BlockSpec((GLA_ROWS, QK), lambda b, j: (rows(b, j), 6 * D // QK)),
        pl.BlockSpec((GLA_ROWS, QK), lambda b, j: (rows(b, j), 6 * D // QK + 1)),
        pl.BlockSpec((GLA_ROWS, D), lambda b, j: (rows(b, j), 2)),
        pl.BlockSpec((GLA_ROWS, QK), lambda b, j: (rows(b, j), 1 if rev else 0)),
    ]


def _gla_fwd(p_all, la_all, s0, *, rev, row0, nb, seq, D, name, carry=None):
    DK, DV, nblk, rb0, blk = _gla_specs(D, rev, row0, seq)
    cpb = GLA_ROWS // CHUNK

    def rows(b, j):
        return rb0 + b * nblk + blk(j)

    in_specs = _gla_in_specs(D, rev, rows) + [pl.BlockSpec((1, HEADS, DV, DK), lambda b, j: (b, 0, 0, 0))]
    out_specs = [
        pl.BlockSpec((GLA_ROWS, D), lambda b, j: (b * nblk + blk(j), 0)),
        pl.BlockSpec((1, HEADS, cpb, DV, DK), lambda b, j: (b, 0, blk(j), 0, 0)),
        pl.BlockSpec((1, HEADS, DV, DK), lambda b, j: (b, 0, 0, 0)),
    ]
    out_shape = [
        jax.ShapeDtypeStruct((nb * seq, D), bf16),
        jax.ShapeDtypeStruct((nb, HEADS, seq // CHUNK, DV, DK), bf16),
        jax.ShapeDtypeStruct((nb, HEADS, DV, DK), f32),
    ]
    chunk = functools.partial(_gla_chunk, rev=rev, scale=DK ** -0.5)

    def body(q_ref, k_ref, v_ref, la_ref, s0_ref, o_ref, hist_ref, sfin_ref, st_ref):
        j = pl.program_id(1)

        @pl.when(j == 0)
        def _():
            st_ref[...] = s0_ref[0]

        def step(ci, exact):
            cc = (cpb - 1 - ci) if rev else ci
            sl = pl.ds(cc * CHUNK, CHUNK)
            for h in range(HEADS):
                kq, kv = pl.ds(h * DK, DK), pl.ds(h * DV, DV)
                St = st_ref[h]
                hist_ref[0, h, cc] = St.astype(bf16)
                St2, o = chunk(St, q_ref[sl, kq].astype(f32), k_ref[sl, kq].astype(f32), v_ref[sl, kv].astype(f32), la_ref[sl, kq],
                               exact=exact)
                o_ref[sl, kv] = o.astype(bf16)
                st_ref[h] = St2

        mild = _mild_decay(la_ref)
        for exact in (False, True):
            @pl.when(jnp.logical_not(mild) if exact else mild)
            def _(exact=exact):
                for ci in range(cpb):
                    step(ci, exact)

        @pl.when(j == nblk - 1)
        def _():
            sfin_ref[0] = st_ref[...]

    res, carried = _call(body, name=name, grid=(nb, nblk), in_specs=in_specs, out_specs=out_specs, out_shape=out_shape,
                         scratch_shapes=[pltpu.VMEM((HEADS, DV, DK), f32)], sem=("parallel", "arbitrary"),
                         args=(p_all, p_all, p_all, la_all, s0), carry=carry)
    return res if carry is None else (res, carried)


def _gla_bwd(p_all, la_all, hist, do, dsfin, *, rev, row0, nb, seq, D, name, add=None):
    DK, DV, nblk, rb0, blk = _gla_specs(D, not rev, row0, seq)
    cpb = GLA_ROWS // CHUNK
    QK = HEADS * DK
    has_do = do is not None

    def rows(b, j):
        return rb0 + b * nblk + blk(j)

    in_specs = _gla_in_specs(D, rev, rows) + [
        pl.BlockSpec((1, HEADS, cpb, DV, DK), lambda b, j: (b, 0, blk(j), 0, 0)),
        pl.BlockSpec((1, HEADS, DV, DK), lambda b, j: (b, 0, 0, 0)),
    ]
    args = [p_all, p_all, p_all, la_all, hist, dsfin]
    if has_do:
        in_specs.append(pl.BlockSpec((GLA_ROWS, D), lambda b, j: (b * nblk + blk(j), 0)))
        args.append(do)
    if add is not None:
        in_specs += [pl.BlockSpec((GLA_ROWS, t.shape[1]), lambda b, j: (b * nblk + blk(j), 0)) for t in add]
        args += list(add)
    gdt = f32 if add is None else bf16
    out_specs = [
        pl.BlockSpec((GLA_ROWS, QK), lambda b, j: (b * nblk + blk(j), 0)),
        pl.BlockSpec((GLA_ROWS, QK), lambda b, j: (b * nblk + blk(j), 0)),
        pl.BlockSpec((GLA_ROWS, D), lambda b, j: (b * nblk + blk(j), 0)),
        pl.BlockSpec((GLA_ROWS, QK), lambda b, j: (b * nblk + blk(j), 0)),
        pl.BlockSpec((1, HEADS, DV, DK), lambda b, j: (b, 0, 0, 0)),
    ]
    out_shape = [
        jax.ShapeDtypeStruct((nb * seq, QK), gdt), jax.ShapeDtypeStruct((nb * seq, QK), gdt),
        jax.ShapeDtypeStruct((nb * seq, D), gdt), jax.ShapeDtypeStruct((nb * seq, QK), f32),
        jax.ShapeDtypeStruct((nb, HEADS, DV, DK), f32),
    ]
    chunk = functools.partial(_gla_chunk, rev=rev, scale=DK ** -0.5)

    def body(*refs):
        refs = list(refs)
        q_ref, k_ref, v_ref, la_ref, hist_ref, dsfin_ref = refs[:6]
        do_ref = refs[6] if has_do else None
        add_refs = refs[6 + has_do:len(refs) - 6]
        dq_ref, dk_ref, dv_ref, dla_ref, ds0_ref, ds_ref = refs[len(refs) - 6:]
        j = pl.program_id(1)

        @pl.when(j == 0)
        def _():
            ds_ref[...] = dsfin_ref[0]

        def step(ci, exact):
            cc = ci if rev else (cpb - 1 - ci)
            sl = pl.ds(cc * CHUNK, CHUNK)
            for h in range(HEADS):
                kq, kv = pl.ds(h * DK, DK), pl.ds(h * DV, DV)
                prim = (hist_ref[0, h, cc].astype(f32), q_ref[sl, kq].astype(f32), k_ref[sl, kq].astype(f32), v_ref[sl, kv].astype(f32), la_ref[sl, kq])
                _, vjp = jax.vjp(functools.partial(chunk, exact=exact), *prim)
                d_o = do_ref[sl, kv].astype(f32) if has_do else jnp.zeros((CHUNK, DV), f32)
                dSt, dq, dk, dv, dg = vjp((ds_ref[h], d_o))
                if add is not None:
                    dq, dk, dv = dq + add_refs[0][sl, kq], dk + add_refs[1][sl, kq], dv + add_refs[2][sl, kv]
                dq_ref[sl, kq] = dq.astype(gdt)
                dk_ref[sl, kq] = dk.astype(gdt)
                dv_ref[sl, kv] = dv.astype(gdt)
                dla_ref[sl, kq] = dg
                ds_ref[h] = dSt

        mild = _mild_decay(la_ref)
        for exact in (False, True):
            @pl.when(jnp.logical_not(mild) if exact else mild)
            def _(exact=exact):
                for ci in range(cpb):
                    step(ci, exact)

        @pl.when(j == nblk - 1)
        def _():
            ds0_ref[0] = ds_ref[...]

    return _pc(body, name=name, grid=(nb, nblk), in_specs=in_specs, out_specs=out_specs, out_shape=out_shape,
               scratch_shapes=[pltpu.VMEM((HEADS, DV, DK), f32)], compiler_params=_params("parallel", "arbitrary"))(*args)


def _conv_fwd(p_all, dw_w, dw_b, *, B, L, D, name):
    ct = _pick(D, 256)
    nj = D // ct
    st = _pick(L, 128, 8)
    off = CONV_PAD - CONV_W // 2

    def body(a_ref, b_ref, w_ref, bias_ref, o_ref, zs_ref):
        _fill_shifted(zs_ref, L, lambda t0, n: a_ref[pl.ds(t0, n), :].astype(f32) * jax.nn.sigmoid(b_ref[pl.ds(t0, n), :].astype(f32)))
        for t0 in range(0, L, st):
            acc = jnp.zeros((st, ct), f32) + bias_ref[...]
            for k in range(CONV_W):
                acc = acc + w_ref[pl.ds(k, 1), :] * _window(zs_ref, t0 + k + off, st)
            o_ref[pl.ds(t0, st), :] = acc.astype(bf16)

    return _pc(
        body, name=name, grid=(B, nj),
        in_specs=[pl.BlockSpec((L, ct), lambda b, j: (b, j)), pl.BlockSpec((L, ct), lambda b, j: (b, nj + j)),
                  pl.BlockSpec((CONV_W, ct), lambda b, j: (0, j)), pl.BlockSpec((1, ct), lambda b, j: (0, j))],
        out_specs=pl.BlockSpec((L, ct), lambda b, j: (b, j)), out_shape=jax.ShapeDtypeStruct((B * L, D), bf16),
        scratch_shapes=[pltpu.VMEM((SUBLANES, L + 2 * CONV_PAD, ct), f32)], compiler_params=_params("parallel", "parallel"),
    )(p_all, p_all, dw_w, dw_b)


def _fill_shifted(zs_ref, L, rows):
    lp = L + 2 * CONV_PAD
    ct = zs_ref.shape[2]
    step = 256
    zs_ref[0, pl.ds(0, CONV_PAD), :] = jnp.zeros((CONV_PAD, ct), f32)
    zs_ref[0, pl.ds(CONV_PAD + L, CONV_PAD), :] = jnp.zeros((CONV_PAD, ct), f32)
    for t0 in range(0, L, step):
        n = min(step, L - t0)
        zs_ref[0, pl.ds(CONV_PAD + t0, n), :] = rows(t0, n)
    for r in range(1, SUBLANES):
        for i0 in range(0, lp - SUBLANES, step):
            n = min(step, lp - SUBLANES - i0)
            zs_ref[r, pl.ds(i0, n), :] = zs_ref[0, pl.ds(i0 + r, n), :]


def _window(zs_ref, start, n):
    r = start % SUBLANES
    return zs_ref[r, pl.ds(start - r, n), :]


def _conv_bwd(p_all, dcz, dw_w, *, B, L, D, name, carry=None):
    ct = _pick(D, 128)
    nj = D // ct
    st = _pick(L, 256, 8)
    half = CONV_W // 2

    def body(a_ref, b_ref, dcz_ref, w_ref, da_ref, db_ref, ddw_ref, zs_ref, ds_ref):
        bi = pl.program_id(1)
        _fill_shifted(zs_ref, L, lambda t0, n: a_ref[pl.ds(t0, n), :].astype(f32) * jax.nn.sigmoid(b_ref[pl.ds(t0, n), :].astype(f32)))
        _fill_shifted(ds_ref, L, lambda t0, n: dcz_ref[pl.ds(t0, n), :].astype(f32))

        @pl.when(bi == 0)
        def _():
            ddw_ref[...] = jnp.zeros_like(ddw_ref)

        for t0 in range(0, L, st):
            acc = jnp.zeros((st, ct), f32)
            for k in range(CONV_W):
                acc = acc + w_ref[pl.ds(k, 1), :] * _window(ds_ref, t0 + CONV_PAD + half - k, st)
            a_t = a_ref[pl.ds(t0, st), :].astype(f32)
            sg_t = jax.nn.sigmoid(b_ref[pl.ds(t0, st), :].astype(f32))
            da_ref[pl.ds(t0, st), :] = (acc * sg_t).astype(bf16)
            db_ref[pl.ds(t0, st), :] = (acc * a_t * sg_t * (1.0 - sg_t)).astype(bf16)

        parts = [jnp.zeros((SUBLANES, ct), f32) for _ in range(CONV_W)]
        sw = _pick(L, 64, SUBLANES)
        for t0 in range(0, L, sw):
            dout = dcz_ref[pl.ds(t0, sw), :].astype(f32)
            for k in range(CONV_W):
                prod = dout * _window(zs_ref, t0 + k + CONV_PAD - half, sw)
                for i in range(0, sw, SUBLANES):
                    parts[k] = parts[k] + prod[i:i + SUBLANES]
        for k in range(CONV_W):
            ddw_ref[pl.ds(k, 1), :] += jnp.sum(parts[k], axis=0, keepdims=True)

    res, carried = _call(
        body, name=name, grid=(nj, B),
        in_specs=[pl.BlockSpec((L, ct), lambda j, b: (b, j)), pl.BlockSpec((L, ct), lambda j, b: (b, nj + j)),
                  pl.BlockSpec((L, ct), lambda j, b: (b, j)), pl.BlockSpec((CONV_W, ct), lambda j, b: (0, j))],
        out_specs=[pl.BlockSpec((L, ct), lambda j, b: (b, j)), pl.BlockSpec((L, ct), lambda j, b: (b, j)),
                   pl.BlockSpec((2 * CONV_PAD, ct), lambda j, b: (0, j))],
        out_shape=[jax.ShapeDtypeStruct((B * L, D), bf16), jax.ShapeDtypeStruct((B * L, D), bf16),
                   jax.ShapeDtypeStruct((2 * CONV_PAD, D), f32)],
        scratch_shapes=[pltpu.VMEM((SUBLANES, L + 2 * CONV_PAD, ct), f32), pltpu.VMEM((SUBLANES, L + 2 * CONV_PAD, ct), f32)],
        sem=("parallel", "arbitrary"), args=(p_all, p_all, dcz, dw_w), carry=carry)
    return res if carry is None else (res, carried)


def _exchange(arrs, scatter, name):
    ex = _Exchange(arrs, scatter)
    n = ex.n

    def body(*refs):
        ex.start(refs[:n], refs[n:2 * n], refs[2 * n:])
        ex.finish(refs[:n], refs[n:2 * n], refs[2 * n:])

    res = _pc(body, name=name, in_specs=ex.specs, out_specs=ex.specs, out_shape=ex.out_shape, scratch_shapes=ex.scratch)(*arrs)
    return list(res)


class _Exchange:
    def __init__(self, arrs, scatter):
        self.arrs, self.scatter, self.n = list(arrs), scatter, len(arrs)
        self.out_shape = [jax.ShapeDtypeStruct(((N_DEV,) + a.shape[1:]) if scatter else ((N_DEV,) + a.shape), a.dtype) for a in arrs]
        self.specs = [pl.BlockSpec(memory_space=pl.ANY)] * self.n
        self.scratch = [pltpu.SemaphoreType.DMA((self.n, N_DEV - 1)), pltpu.SemaphoreType.DMA((self.n, N_DEV - 1)),
                        pltpu.SemaphoreType.DMA((self.n,))]

    def _copies(self, ins, outs, sems, landing):
        send_sems, recv_sems, local_sems = sems
        me = 4 * lax.axis_index("x") + 2 * lax.axis_index("y") + lax.axis_index("c")
        if landing:
            local = []
        else:
            local = [pltpu.make_async_copy(ins[a].at[me] if self.scatter else ins[a], outs[a].at[me], local_sems.at[a]) for a in range(self.n)]
        remote = []
        for k in range(1, N_DEV):
            p = (me + (N_DEV - k if landing else k)) % N_DEV
            for a in range(self.n):
                remote.append(pltpu.make_async_remote_copy(
                    src_ref=ins[a].at[p] if self.scatter else ins[a], dst_ref=outs[a].at[p if landing else me],
                    send_sem=send_sems.at[a, k - 1], recv_sem=recv_sems.at[a, k - 1],
                    device_id=(p // 4, (p // 2) % 2, p % 2), device_id_type=MESH))
        return local, remote

    def _gather_plan(self, ins, outs, sems):
        send_sems, recv_sems, local_sems = sems
        x, y, c = lax.axis_index("x"), lax.axis_index("y"), lax.axis_index("c")
        chips = [(1 - x, y), (x, 1 - y), (1 - x, 1 - y)]

        def blk(px, py, pc):
            return 4 * px + 2 * py + pc

        def copy(a, k, block, to, own):
            return pltpu.make_async_remote_copy(
                src_ref=ins[a] if own else outs[a].at[block], dst_ref=outs[a].at[block],
                send_sem=send_sems.at[a, k], recv_sem=recv_sems.at[a, k], device_id=to, device_id_type=MESH)

        me = blk(x, y, c)
        local = [pltpu.make_async_copy(ins[a], outs[a].at[me], local_sems.at[a]) for a in range(self.n)]
        return local, copy, me, (x, y, 1 - c), chips, blk, c

    def start(self, ins, outs, sems):
        if self.scatter:
            local, sends = self._copies(ins, outs, sems, False)
            for cp in local + sends:
                cp.start()
            return
        local, copy, me, sibling, chips, _, c = self._gather_plan(ins, outs, sems)
        for cp in local:
            cp.start()
        for a in range(self.n):
            copy(a, 0, me, sibling, True).start()
            for j, chip in enumerate(chips):
                copy(a, 1 + j, me, (*chip, c), True).start()

    def finish(self, ins, outs, sems):
        if self.scatter:
            for cp in self._copies(ins, outs, sems, True)[1]:
                cp.wait_recv()
            local, sends = self._copies(ins, outs, sems, False)
            for cp in sends:
                cp.wait_send()
            for cp in local:
                cp.wait()
            return
        local, copy, me, sibling, chips, blk, c = self._gather_plan(ins, outs, sems)
        for j, chip in enumerate(chips):
            for a in range(self.n):
                copy(a, 1 + j, blk(*chip, c), sibling, True).wait_recv()
                copy(a, 4 + j, blk(*chip, c), sibling, False).start()
        for a in range(self.n):
            copy(a, 0, blk(*sibling), sibling, True).wait_recv()
            for j, chip in enumerate(chips):
                copy(a, 4 + j, blk(*chip, 1 - c), sibling, False).wait_recv()
        for a in range(self.n):
            copy(a, 0, me, sibling, True).wait_send()
            for j, chip in enumerate(chips):
                copy(a, 1 + j, me, (*chip, c), True).wait_send()
                copy(a, 4 + j, blk(*chip, c), sibling, False).wait_send()
        for cp in local:
            cp.wait()


def _carried(inner, n_in, n_out, grid, ex):
    n = ex.n

    def body(*refs):
        own_in, c_in = refs[:n_in], refs[n_in:n_in + n]
        own_out, c_out = refs[n_in + n:n_in + n + n_out], refs[n_in + n + n_out:n_in + 2 * n + n_out]
        rest = refs[n_in + 2 * n + n_out:]
        own_scr, sems = rest[:len(rest) - 3], rest[len(rest) - 3:]
        pids = [pl.program_id(d) for d in range(len(grid))]
        first = functools.reduce(jnp.logical_and, [p == 0 for p in pids])
        last = functools.reduce(jnp.logical_and, [p == g - 1 for p, g in zip(pids, grid)])

        @pl.when(first)
        def _():
            ex.start(c_in, c_out, sems)

        inner(*own_in, *own_out, *own_scr)

        @pl.when(last)
        def _():
            ex.finish(c_in, c_out, sems)

    return body


def _call(inner, *, name, grid, in_specs, out_specs, out_shape, scratch_shapes, sem, args, carry=None):
    if carry is None:
        res = _pc(inner, name=name, grid=grid, in_specs=in_specs, out_specs=out_specs, out_shape=out_shape,
                  scratch_shapes=scratch_shapes, compiler_params=_params(*sem))(*args)
        return list(res), None
    ex = _Exchange(*carry)
    res = _pc(_carried(inner, len(in_specs), len(out_specs), grid, ex), name=name, grid=grid,
              in_specs=list(in_specs) + ex.specs, out_specs=list(out_specs) + ex.specs, out_shape=list(out_shape) + ex.out_shape,
              scratch_shapes=list(scratch_shapes) + ex.scratch, compiler_params=_params(*(["arbitrary"] * len(grid))))(*args, *ex.arrs)
    res = list(res)
    return res[:len(out_specs)], res[len(out_specs):]


def _mod_fwd(c_all, c_ctx, w_loc, b_loc, name):
    nr, D = c_all.shape
    nc = w_loc.shape[1]

    def body(c_ref, cc_ref, w_ref, b_ref, o_ref):
        a = jnp.concatenate([c_ref[...], jnp.broadcast_to(cc_ref[...], (8, D))], axis=0)
        s = jax.nn.silu(a).astype(bf16)
        o_ref[...] = jnp.dot(s, w_ref[...].astype(bf16), preferred_element_type=f32) + b_ref[...]

    return _pc(body, name=name, out_shape=jax.ShapeDtypeStruct((nr + 8, nc), f32), compiler_params=_params())(c_all, c_ctx, w_loc, b_loc)


def _mod_bwd(c_all, c_ctx, w_loc, dmx_loc, dmc_loc, name):
    nr, D = c_all.shape
    nc = w_loc.shape[1]

    def body(c_ref, cc_ref, w_ref, dmx_ref, dmc_ref, gw_ref, gc_ref):
        cc = cc_ref[...]
        a = jnp.concatenate([c_ref[...], jnp.broadcast_to(cc, (N_DEV, D))], axis=0)
        s = jax.nn.silu(a).astype(bf16)
        g = jnp.concatenate([dmx_ref[...], dmc_ref[...]], axis=0).astype(bf16)
        gw_ref[...] = lax.dot_general(s, g, (((0,), (0,)), ((), ())), preferred_element_type=f32)
        dmc = jnp.sum(dmc_ref[...], axis=0, keepdims=True)
        ds = lax.dot_general(jnp.broadcast_to(dmc, (8, nc)).astype(bf16), w_ref[...].astype(bf16), (((1,), (1,)), ((), ())),
                             preferred_element_type=f32)[0:1]
        sg = jax.nn.sigmoid(cc)
        gc_ref[...] = ds * (sg * (1.0 + cc * (1.0 - sg)))

    return _pc(body, name=name, out_shape=[jax.ShapeDtypeStruct((D, nc), f32), jax.ShapeDtypeStruct((1, D), f32)],
               compiler_params=_params())(c_all, c_ctx, w_loc, dmx_loc, dmc_loc)


def _adamw_math(w, g, m, v):
    m2 = ADAM_B1 * m + (1.0 - ADAM_B1) * g
    v2 = ADAM_B2 * v + (1.0 - ADAM_B2) * jnp.square(g)
    m_hat = m2 / (1.0 - ADAM_B1 ** ADAM_STEP)
    v_hat = v2 / (1.0 - ADAM_B2 ** ADAM_STEP)
    delta = -ADAM_LR * (m_hat / (jnp.sqrt(v_hat) + ADAM_EPS) + ADAM_WD * w)
    return delta, m2, v2


def _adamw_many(params, name):
    n = len(params)

    def body(*refs):
        ins, outs = refs[:4 * n], refs[4 * n:]
        for i in range(n):
            w, m, v, g = (ins[4 * i + k][...] for k in range(4))
            d, m2, v2 = _adamw_math(w, g, m, v)
            outs[3 * i][...] = d
            outs[3 * i + 1][...] = m2
            outs[3 * i + 2][...] = v2

    res = _pc(body, name=name, out_shape=[jax.ShapeDtypeStruct(p[0].shape, f32) for p in params for _ in range(3)],
              compiler_params=_params())(*[a for p in params for a in p])
    return [tuple(res[3 * i:3 * i + 3]) for i in range(n)]


def _adamw(w, m, v, g, name, partials, carry=None):
    r, cdim = w.shape
    tr = _pick(r, 256, 8)

    def body(w_ref, m_ref, v_ref, g_ref, og_ref, od_ref, om_ref, ov_ref):
        if partials:
            g = g_ref[0].astype(f32)
            for s in range(1, N_DEV):
                g = g + g_ref[s].astype(f32)
        else:
            g = g_ref[...]
        d, m2, v2 = _adamw_math(w_ref[...], g, m_ref[...], v_ref[...])
        og_ref[...] = g
        od_ref[...] = d
        om_ref[...] = m2
        ov_ref[...] = v2

    blk = pl.BlockSpec((tr, cdim), lambda i: (i, 0))
    g_spec = pl.BlockSpec((N_DEV, tr, cdim), lambda i: (0, i, 0)) if partials else blk
    res, carried = _call(body, name=name, grid=(r // tr,), in_specs=[blk, blk, blk, g_spec], out_specs=[blk] * 4,
                         out_shape=[jax.ShapeDtypeStruct((r, cdim), f32)] * 4, scratch_shapes=[], sem=("parallel",),
                         args=(w, m, v, g), carry=carry)
    return res if carry is None else (res, carried)


def _sum_sources(parts, name):
    def body(*refs):
        for i_ref, o_ref in zip(refs[:len(parts)], refs[len(parts):]):
            acc = i_ref[0]
            for s in range(1, i_ref.shape[0]):
                acc = acc + i_ref[s]
            o_ref[...] = acc

    return list(_pc(body, name=name, out_shape=[jax.ShapeDtypeStruct(p.shape[1:], f32) for p in parts],
                    compiler_params=_params())(*parts))


def kernel(x, c, ctx, c_ctx, w_mod, b_mod, g_ffn1, w1_gu, w1_down, g_mix, w_in, dw_weight, dw_bias, conv_ln_g, conv_ln_b, w_conv_out, w_alpha_f, b_alpha_f, w_alpha_b, b_alpha_b, gla_norm_g, w_gla_out, w_out, g_ffn2, w2_gu, w2_down, g_final, loss_target, m_c_ctx, m_w_mod, m_b_mod, m_g_ffn1, m_w1_gu, m_w1_down, m_g_mix, m_w_in, m_dw_weight, m_dw_bias, m_conv_ln_g, m_conv_ln_b, m_w_conv_out, m_w_alpha_f, m_b_alpha_f, m_w_alpha_b, m_b_alpha_b, m_gla_norm_g, m_w_gla_out, m_w_out, m_g_ffn2, m_w2_gu, m_w2_down, m_g_final, v_c_ctx, v_w_mod, v_b_mod, v_g_ffn1, v_w1_gu, v_w1_down, v_g_mix, v_w_in, v_dw_weight, v_dw_bias, v_conv_ln_g, v_conv_ln_b, v_w_conv_out, v_w_alpha_f, v_b_alpha_f, v_w_alpha_b, v_b_alpha_b, v_gla_norm_g, v_w_gla_out, v_w_out, v_g_ffn2, v_w2_gu, v_w2_down, v_g_final):
    B, L, D = x.shape
    Lc = ctx.shape[1]
    T, Tc = B * L, B * Lc
    Tall = T + Tc
    F = w1_down.shape[1] * N_DEV
    DK, DV = D // (2 * HEADS), D // HEADS
    QK = HEADS * DK
    PW = 7 * D + LR_PAD
    tm = ROW_TILE
    tpe = L // tm
    nx, nall = T // tm, Tall // tm
    me = 4 * lax.axis_index("x") + 2 * lax.axis_index("y") + lax.axis_index("c")

    rw_all = dict(tm=tm, n_tiles=nall, tpe=tpe, nx_tiles=nx, n_ex=B + 1)
    rw_x = dict(tm=tm, n_tiles=nx, tpe=tpe, nx_tiles=nx, n_ex=B)
    rw_all2, rw_x2 = rw_all, rw_x

    dww_g, waf_g, wab_g, c_g = _exchange([dw_weight[0], w_alpha_f[0], w_alpha_b[0], c], False, "gather_first")

    def cols(gat):
        return jnp.transpose(gat, (1, 0, 2)).reshape(gat.shape[1], N_DEV * gat.shape[2])

    def rows_(gat):
        return gat.reshape(N_DEV * gat.shape[1], gat.shape[2])

    dww = cols(dww_g)
    WA = jnp.zeros((LR_PAD, 2 * QK), f32).at[:LOWRANK, :QK].set(cols(waf_g)).at[LOWRANK:2 * LOWRANK, QK:].set(cols(wab_g)).astype(bf16)
    BA = jnp.concatenate([b_alpha_f, b_alpha_b], axis=1)
    c_all = c_g.reshape(N_DEV * B, D)
    c_ctx2 = c_ctx.reshape(1, D)

    ncm = w_mod.shape[2]
    b_mod_loc = lax.dynamic_slice(b_mod, (0, me * ncm), (1, ncm))
    mod_loc = _mod_fwd(c_all, c_ctx2, w_mod[0], b_mod_loc, "mod_fwd")
    (mod_g,) = _exchange([mod_loc], False, "gather_mod")
    mod_full = cols(mod_g)
    mod_tab = jnp.concatenate([lax.dynamic_slice(mod_full, (me * B, 0), (B, N_MOD * D)), mod_full[N_DEV * B:N_DEV * B + 1]], axis=0)
    mods = [mod_tab[:, i * D:(i + 1) * D].reshape(B + 1, 1, D) for i in range(N_MOD)]
    mods_x = [mm[:B] for mm in mods]

    x_lat, x_ctx = (x.reshape(T, D), D, 0, True), (ctx.reshape(Tc, D), D, 0, "c")

    def f_ffn_in(tok, ex, sh):
        return [_rms_mod(tok[0] + tok[1], sh[0], ex[0], ex[1])], [], []

    (u1,), (w1gu_g,) = _rowwise(f_ffn_in, name="ffn1_in", tok_in=[x_lat, x_ctx], ex_in=[mods[0], mods[1]], sh_in=[g_ffn1],
                                tok_out=[(D, bf16)], carry=([w1_gu[0].astype(bf16)], False), **rw_all)
    W1gu = cols(w1gu_g)
    (gu1, h1), (w1d_g, win_g) = _ffn_up(u1, W1gu, "ffn1_up", carry=([w1_down[0].astype(bf16), w_in[0].astype(bf16)], False))
    W1d = rows_(w1d_g)
    lr2 = 2 * LOWRANK
    segs = [(0, 2 * D, 0), (2 * D, 2 * D + QK, 6 * D), (2 * D + QK, 3 * D, 6 * D + QK), (3 * D, 4 * D, 2 * D), (4 * D, 5 * D, 3 * D),
            (5 * D, 5 * D + lr2, 7 * D), (5 * D + lr2, 6 * D + lr2, 4 * D), (6 * D + lr2, 7 * D + lr2, 5 * D)]
    wc = w_in.shape[2]
    win_parts = []
    for lo, hi, _ in sorted(segs, key=lambda t: t[2]):
        for d in range(N_DEV):
            a0, a1 = max(lo, d * wc), min(hi, (d + 1) * wc)
            if a0 < a1:
                win_parts.append(win_g[d][:, a0 - d * wc:a1 - d * wc])
    Win = jnp.concatenate(win_parts + [jnp.zeros((D, LR_PAD - lr2), bf16)], axis=1)

    def nn(a, w):
        return jnp.dot(a.astype(bf16), w, preferred_element_type=f32)

    def nt(a, w):
        return lax.dot_general(a.astype(bf16), w, (((1,), (1,)), ((), ())), preferred_element_type=f32)

    def mix_in(xv, fv, gate, sh, sc, g):
        x1 = xv + 0.5 * gate * fv
        return x1, _rms_mod(x1, g, sh, sc)

    def f_mix_in(tok, ex, sh):
        f1v = nn(tok[2], sh[1])
        return list(mix_in(tok[0] + tok[1], f1v, ex[0], ex[1], ex[2], sh[0])) + [f1v], [], []

    x1, um, f1 = _rowwise(f_mix_in, name="ffn1_down_mix_in", tok_in=[x_lat, x_ctx, (h1, F, 0, False)],
                          ex_in=[mods[2], mods[3], mods[4]], sh_in=[g_mix, W1d], tok_out=[(D, f32), (D, bf16), (D, bf16)], **rw_all2)
    p_all, (wco_g, wgo_g, wo_g, w2gu_g) = _matmul(
        um, Win, "nn", bf16, "in_proj", tm_cap=1024, tn_cap=2432,
        carry=([w_conv_out[0].astype(bf16), w_gla_out[0].astype(bf16), w_out[0].astype(bf16), w2_gu[0].astype(bf16)], False))
    Wco, Wgo, Wo, W2gu = rows_(wco_g), rows_(wgo_g), rows_(wo_g), cols(w2gu_g)

    def log_decay(lr, wa, ba):
        z = _bdot(lr, wa, (1, 0)) + ba
        return _log_sigmoid(z) / TAU

    def f_decay(tok, ex, sh):
        return [log_decay(tok[0], sh[0], sh[1])], [], []

    lr_blk = (p_all, LR_PAD, 7 * D // LR_PAD, False)
    (la_all,) = _rowwise(f_decay, name="log_decay", tok_in=[lr_blk], sh_in=[WA, BA], tok_out=[(2 * QK, f32)], **rw_all)

    zeros_s = jnp.zeros((B, HEADS, DV, DK), f32)
    gla_c = dict(row0=T, nb=B, seq=Lc, D=D)
    gla_x = dict(row0=0, nb=B, seq=L, D=D)
    _, hist_cf, s_f = _gla_fwd(p_all, la_all, zeros_s, rev=False, name="gla_ctx_f", **gla_c)
    _, hist_cb, s_b = _gla_fwd(p_all, la_all, zeros_s, rev=True, name="gla_ctx_b", **gla_c)
    (o_f, hist_f, _), (w2d_g,) = _gla_fwd(p_all, la_all, s_f, rev=False, name="gla_x_f", carry=([w2_down[0].astype(bf16)], False), **gla_x)
    W2d = rows_(w2d_g)
    o_b, hist_b, _ = _gla_fwd(p_all, la_all, s_b, rev=True, name="gla_x_b", **gla_x)

    cz = _conv_fwd(p_all, dww, dw_bias, B=B, L=L, D=D, name="conv_fwd")

    def ln_silu(z, g, b):
        mu = jnp.mean(z, axis=-1, keepdims=True)
        var = jnp.mean(jnp.square(z - mu), axis=-1, keepdims=True)
        return jax.nn.silu((z - mu) * lax.rsqrt(var + EPS) * g + b)

    def f_ln(tok, ex, sh):
        zc = ln_silu(tok[0].astype(f32), sh[0], sh[1])
        return [zc, nn(zc, sh[2])], [], []

    zc, yc = _rowwise(f_ln, name="conv_ln_out", tok_in=[(cz, D, 0, False)], sh_in=[conv_ln_g, conv_ln_b, Wco],
                      tok_out=[(D, bf16), (D, bf16)], **rw_x)

    def gla_out(of, ob, og, gn):
        return _head_rms(of.astype(f32) + ob.astype(f32), DV) * gn * jax.nn.silu(og.astype(f32))

    def f_gla_out(tok, ex, sh):
        og2 = gla_out(tok[0], tok[1], tok[2], sh[0])
        return [og2, nn(og2, sh[1])], [], []

    og_blk = (p_all, D, 3, False)
    og2, yg = _rowwise(f_gla_out, name="gla_norm_out", tok_in=[(o_f, D, 0, False), (o_b, D, 0, False), og_blk], sh_in=[gla_norm_g, Wgo],
                       tok_out=[(D, bf16), (D, bf16)], **rw_x)

    def merge(ga, gb, ycv, ygv):
        return jax.nn.sigmoid(ga.astype(f32)) * ycv.astype(f32) + jax.nn.sigmoid(gb.astype(f32)) * ygv.astype(f32)

    def f_merge(tok, ex, sh):
        mg = merge(*tok)
        return [mg, nn(mg, sh[0])], [], []

    ga_blk, gb_blk = (p_all, D, 4, False), (p_all, D, 5, False)
    mg, mix = _rowwise(f_merge, name="merge_mix_out", tok_in=[ga_blk, gb_blk, (yc, D, 0, False), (yg, D, 0, False)], sh_in=[Wo],
                       tok_out=[(D, bf16), (D, f32)], **rw_x)

    def ffn2_in(x1v, mixv, g5, sh, sc, g):
        x2 = x1v + g5 * mixv
        return x2, _rms_mod(x2, g, sh, sc)

    def f_ffn2_in(tok, ex, sh):
        return list(ffn2_in(tok[0], tok[1], ex[0], ex[1], ex[2], sh[0])), [], []

    x2, u2 = _rowwise(f_ffn2_in, name="ffn2_in", tok_in=[(x1, D, 0, False), (mix, D, 0, False)], ex_in=[mods_x[5], mods_x[6], mods_x[7]],
                      sh_in=[g_ffn2], tok_out=[(D, f32), (D, bf16)], **rw_x)
    gu2, h2 = _ffn_up(u2, W2gu, "ffn2_up")

    gf2 = g_final.reshape(1, D)

    def head_loss(x2v, f2v, g8, gf, tgt):
        x3 = x2v + 0.5 * g8 * f2v
        y = x3 * lax.rsqrt(jnp.mean(x3 * x3, axis=-1, keepdims=True) + EPS) * gf
        return 0.5 * jnp.sum(jnp.mean(jnp.square(y - tgt), axis=-1))

    def f_head(tok, ex, sh):
        loss, vjp = jax.vjp(lambda a, b_, c_, d_: head_loss(a, b_, c_, d_, tok[2]), tok[0], nn(tok[1], sh[1]), ex[0], sh[0])
        dx3, df2, dg8, dgf = vjp(jnp.ones((), f32))
        return [dx3, df2], [dg8], [dgf, jnp.broadcast_to(loss.reshape(1, 1), (1, 128))]

    dx3, df2, dg8, dgf, loss_p = _rowwise(
        f_head, name="ffn2_down_head", tok_in=[(x2, D, 0, False), (h2, F, 0, False), (loss_target.reshape(T, D), D, 0, False)],
        ex_in=[mods_x[8]], sh_in=[gf2, W2d], tok_out=[(D, f32), (D, bf16)], ex_out=[D], gl_out=[(1, D), (1, 128)], **rw_x2)

    dgu2 = _ffn_down_dx(df2, W2d, gu2, "ffn2_down_dx")
    gW2d = _matmul(h2, df2, "tn", f32, "ffn2_down_dw", tm_cap=1408)
    du2 = _matmul(dgu2, W2gu, "nt", bf16, "ffn2_up_dx", halves="a")
    gW2gu = _matmul(u2, dgu2, "tn", f32, "ffn2_up_dw", halves="b")

    def f_ffn2_in_bwd(tok, ex, sh):
        _, vjp = jax.vjp(ffn2_in, tok[0], tok[1], ex[0], ex[1], ex[2], sh[0])
        dx2, dmix, dg5, dsh, dsc, dg = vjp((tok[3], tok[2].astype(f32)))
        return [dx2, dmix], [dg5, dsh, dsc], [dg]

    dx2, dmix, dg5, dsh6, dsc7, dg_ffn2 = _rowwise(
        f_ffn2_in_bwd, name="ffn2_in_bwd", tok_in=[(x1, D, 0, False), (mix, D, 0, False), (du2, D, 0, False), (dx3, D, 0, False)],
        ex_in=[mods_x[5], mods_x[6], mods_x[7]], sh_in=[g_ffn2], tok_out=[(D, f32), (D, bf16)], ex_out=[D, D, D], gl_out=[(1, D)], **rw_x)

    gWo = _matmul(mg, dmix, "tn", f32, "mix_out_dw")

    def f_merge_bwd(tok, ex, sh):
        _, vjp = jax.vjp(merge, *[t.astype(f32) for t in tok[:4]])
        dga, dgb, dyc, dyg = vjp(nt(tok[4], sh[0]))
        return [dga, dgb, dyc, dyg], [], []

    dga, dgb, dyc, dyg = _rowwise(f_merge_bwd, name="mix_out_merge_bwd",
                                  tok_in=[ga_blk, gb_blk, (yc, D, 0, False), (yg, D, 0, False), (dmix, D, 0, False)], sh_in=[Wo],
                                  tok_out=[(D, bf16)] * 4, **rw_x)
    gWco = _matmul(zc, dyc, "tn", f32, "conv_out_dw")
    gWgo = _matmul(og2, dyg, "tn", f32, "gla_out_dw")

    def f_ln_bwd(tok, ex, sh):
        _, vjp = jax.vjp(ln_silu, tok[0].astype(f32), sh[0], sh[1])
        dcz, dg, db = vjp(nt(tok[1], sh[2]))
        return [dcz], [], [dg, db, jnp.sum(dcz, axis=0, keepdims=True)]

    dcz, g_ln_g, g_ln_b, g_dwb = _rowwise(f_ln_bwd, name="conv_out_ln_bwd", tok_in=[(cz, D, 0, False), (dyc, D, 0, False)],
                                          sh_in=[conv_ln_g, conv_ln_b, Wco], tok_out=[(D, bf16)], gl_out=[(1, D)] * 3, **rw_x)
    def col_shards(g):
        return jnp.transpose(g.reshape(g.shape[0], N_DEV, g.shape[1] // N_DEV), (1, 0, 2)).astype(bf16)

    def row_shards(g):
        return g.reshape(N_DEV, g.shape[0] // N_DEV, g.shape[1]).astype(bf16)

    (dca, dcb, g_dww), (r_w2d, r_w2gu, r_wo, r_wco, r_wgo) = _conv_bwd(
        p_all, dcz, dww, B=B, L=L, D=D, name="conv_bwd",
        carry=([row_shards(gW2d), col_shards(gW2gu), row_shards(gWo), row_shards(gWco), row_shards(gWgo)], True))

    def f_gla_out_bwd(tok, ex, sh):
        _, vjp = jax.vjp(gla_out, tok[0].astype(f32), tok[1].astype(f32), tok[2].astype(f32), sh[0])
        dof, _, dog, dgn = vjp(nt(tok[3], sh[1]))
        return [dof, dog], [], [dgn]

    d_o, dog, g_gn = _rowwise(f_gla_out_bwd, name="gla_out_norm_bwd",
                              tok_in=[(o_f, D, 0, False), (o_b, D, 0, False), og_blk, (dyg, D, 0, False)], sh_in=[gla_norm_g, Wgo],
                              tok_out=[(D, bf16), (D, bf16)], gl_out=[(1, D)], **rw_x)

    dq_f, dk_f, dv_f, dla_f, ds_f = _gla_bwd(p_all, la_all, hist_f, d_o, zeros_s, rev=False, name="gla_x_f_bwd", **gla_x)
    dq, dk, dv, dla_b, ds_b = _gla_bwd(p_all, la_all, hist_b, d_o, zeros_s, rev=True, name="gla_x_b_bwd", add=(dq_f, dk_f, dv_f), **gla_x)
    dq_cf, dk_cf, dv_cf, dla_cf, _ = _gla_bwd(p_all, la_all, hist_cf, None, ds_f, rev=False, name="gla_ctx_f_bwd", **gla_c)
    _, dk_c, dv_c, dla_cb, _ = _gla_bwd(p_all, la_all, hist_cb, None, ds_b, rev=True, name="gla_ctx_b_bwd", add=(dq_cf, dk_cf, dv_cf), **gla_c)

    dla_all = jnp.concatenate([jnp.concatenate([dla_f, dla_b], axis=1), jnp.concatenate([dla_cf, dla_cb], axis=1)], axis=0)

    def f_decay_bwd(tok, ex, sh):
        _, vjp = jax.vjp(log_decay, tok[0].astype(f32), sh[0].astype(f32), sh[1])
        dlr, dwa, dba = vjp(tok[1])
        return [dlr], [], [dwa, dba]

    dlr, g_WA, g_BA = _rowwise(f_decay_bwd, name="log_decay_bwd", tok_in=[lr_blk, (dla_all, 2 * QK, 0, False)], sh_in=[WA, BA],
                               tok_out=[(LR_PAD, bf16)], gl_out=[(LR_PAD, 2 * QK), (1, 2 * QK)], **rw_all)

    zc_ = functools.partial(jnp.zeros, dtype=bf16)
    dp_x = jnp.concatenate([dca, dcb, dv, dog, dga, dgb, dq, dk, dlr[:T]], axis=1)
    dp_c = jnp.concatenate([zc_((Tc, 2 * D)), dv_c, zc_((Tc, 3 * D)), zc_((Tc, QK)), dk_c, dlr[T:]], axis=1)
    dp_all = jnp.concatenate([dp_x, dp_c], axis=0)
    gWin_p = _matmul(um, dp_all, "tn", f32, "in_proj_dw", tm_cap=512, tn_cap=2432)
    gwin_shards = []
    for d in range(N_DEV):
        parts = []
        for lo, hi, po in segs:
            a0, a1 = max(lo, d * wc), min(hi, (d + 1) * wc)
            if a0 < a1:
                parts.append(gWin_p[:, po + a0 - lo:po + a1 - lo])
        gwin_shards.append(jnp.concatenate(parts, axis=1))
    dum, (r_win,) = _matmul(dp_all, Win, "nt", bf16, "in_proj_dx", tk_cap=2432, carry=([jnp.stack(gwin_shards).astype(bf16)], True))

    def f_mix_in_bwd(tok, ex, sh):
        _, vjp = jax.vjp(mix_in, tok[0] + tok[1], tok[2].astype(f32), ex[0], ex[1], ex[2], sh[0])
        dx1, df1, dgate, dsh, dsc, dg = vjp((tok[4], tok[3].astype(f32)))
        return [dx1, df1], [dgate, dsh, dsc], [dg]

    dx1, df1, dg2, dsh3, dsc4, dg_mix = _rowwise(
        f_mix_in_bwd, name="mix_in_bwd", tok_in=[x_lat, x_ctx, (f1, D, 0, False), (dum, D, 0, False), (dx2, D, 0, True)],
        ex_in=[mods[2], mods[3], mods[4]], sh_in=[g_mix], tok_out=[(D, f32), (D, bf16)], ex_out=[D, D, D], gl_out=[(1, D)], **rw_all)

    dgu1 = _ffn_down_dx(df1, W1d, gu1, "ffn1_down_dx")
    gW1d = _matmul(h1, df1, "tn", f32, "ffn1_down_dw", tm_cap=1408)
    gW1gu, (r_w1d,) = _matmul(u1, dgu1, "tn", f32, "ffn1_up_dw", carry=([row_shards(gW1d)], True), halves="b")
    du1, (r_w1gu,) = _matmul(dgu1, W1gu, "nt", bf16, "ffn1_up_dx", carry=([col_shards(gW1gu)], True), halves="a")

    def f_ffn_in_bwd(tok, ex, sh):
        _, vjp = jax.vjp(_rms_mod, tok[0] + tok[1], sh[0], ex[0], ex[1])
        dx, dg, dsh, dsc = vjp(tok[2].astype(f32))
        return [dx + tok[3]], [dsh, dsc], [dg]

    dx_lat, dsh0, dsc1, dg_ffn1 = _rowwise(
        f_ffn_in_bwd, name="ffn1_in_bwd", tok_in=[x_lat, x_ctx, (du1, D, 0, False), (dx1, D, 0, False)],
        ex_in=[mods[0], mods[1]], sh_in=[g_ffn1], tok_out=[(D, f32, "x")], ex_out=[D, D], gl_out=[(1, D)], **rw_all)
    grad_x = dx_lat.reshape(B, L, D)

    zrow = jnp.zeros((1, 1, D), f32)
    dmod_loc = jnp.concatenate([dsh0, dsc1, dg2, dsh3, dsc4] + [jnp.concatenate([t, zrow], axis=0) for t in (dg5, dsh6, dsc7, dg8)],
                               axis=2).reshape(B + 1, N_MOD * D)
    rows16 = jnp.concatenate([jnp.concatenate([loss_p, jnp.zeros((1, D - loss_p.shape[1]), f32)], axis=1), dg_ffn1, dg_mix, g_dwb, g_ln_g,
                              g_ln_b, g_BA, g_gn, dg_ffn2, dgf, jnp.zeros((6, D), f32)], axis=0)

    def to8(v):
        n_pad = -(-v.shape[1] // 1024) * 1024
        return jnp.pad(v, ((0, 0), (0, n_pad - v.shape[1]))).reshape(8, n_pad // 8)

    def from8(a, n):
        return a.reshape(1, a.size)[:, :n]

    def adam_big(nm, wv, mv, vv, part, carry=None):
        out = _adamw(wv[0], mv[0], vv[0], part, "adamw_" + nm, True, carry=carry)
        res4, carried = out if carry is not None else (out, None)
        return [t[None] for t in res4], carried

    rs_out = dict(w1_gu=r_w1gu, w1_down=r_w1d, w_in=r_win, w_conv_out=r_wco, w_gla_out=r_wgo, w_out=r_wo, w2_gu=r_w2gu, w2_down=r_w2d)
    big = {}
    big["w_in"], (dmod_g, rows_g, dww_sg, wa_sg) = adam_big(
        "w_in", w_in, m_w_in, v_w_in, rs_out["w_in"], carry=([dmod_loc, rows16, g_dww, g_WA[:2 * LOWRANK]], False))
    dmx = dmod_g[:, :B].reshape(N_DEV * B, N_MOD * D)
    dmc = dmod_g[:, B]
    gWmod, gcc_p = _mod_bwd(c_all, c_ctx2, w_mod[0], lax.dynamic_slice(dmx, (0, me * ncm), (N_DEV * B, ncm)),
                            lax.dynamic_slice(dmc, (0, me * ncm), (N_DEV, ncm)), "mod_bwd")

    big["w1_gu"], (gcc_g,) = adam_big("w1_gu", w1_gu, m_w1_gu, v_w1_gu, rs_out["w1_gu"], carry=([to8(gcc_p)], False))
    for nm, wv, mv, vv in (("w1_down", w1_down, m_w1_down, v_w1_down), ("w_conv_out", w_conv_out, m_w_conv_out, v_w_conv_out),
                           ("w_gla_out", w_gla_out, m_w_gla_out, v_w_gla_out), ("w_out", w_out, m_w_out, v_w_out),
                           ("w2_gu", w2_gu, m_w2_gu, v_w2_gu), ("w2_down", w2_down, m_w2_down, v_w2_down)):
        big[nm], _ = adam_big(nm, wv, mv, vv, rs_out[nm])
    big["w_mod"] = [t[None] for t in _adamw(w_mod[0], m_w_mod[0], v_w_mod[0], gWmod, "adamw_w_mod", False)]

    rows_s, dww_s, wa_s, g_cc, g_bmod = _sum_sources(
        [rows_g, dww_sg, wa_sg, gcc_g, jnp.concatenate([dmx, dmc], axis=0).reshape(N_DEV * (B + 1), 8, N_MOD * D // 8)], "sum_small")
    g_cc, g_bmod = from8(g_cc, D), from8(g_bmod, N_MOD * D)
    loss = rows_s[0, 0]
    ncd, nca = dw_weight.shape[2], w_alpha_f.shape[2]
    g_dww_loc = lax.dynamic_slice(dww_s, (0, me * ncd), (CONV_W, ncd))
    g_waf_loc = lax.dynamic_slice(wa_s, (0, me * nca), (LOWRANK, nca))
    g_wab_loc = lax.dynamic_slice(wa_s, (LOWRANK, QK + me * nca), (LOWRANK, nca))
    sm = {k: rows_s[i:i + 1] for i, k in enumerate(["loss", "g_ffn1", "g_mix", "dw_bias", "conv_ln_g", "conv_ln_b", "b_alpha", "gla_norm_g",
                                                     "g_ffn2", "g_final"])}

    small_params = [("c_ctx", c_ctx, m_c_ctx, v_c_ctx, g_cc), ("b_mod", b_mod, m_b_mod, v_b_mod, g_bmod),
                    ("g_ffn1", g_ffn1, m_g_ffn1, v_g_ffn1, sm["g_ffn1"]), ("g_mix", g_mix, m_g_mix, v_g_mix, sm["g_mix"]),
                    ("dw_weight", dw_weight, m_dw_weight, v_dw_weight, g_dww_loc), ("dw_bias", dw_bias, m_dw_bias, v_dw_bias, sm["dw_bias"]),
                    ("conv_ln_g", conv_ln_g, m_conv_ln_g, v_conv_ln_g, sm["conv_ln_g"]),
                    ("conv_ln_b", conv_ln_b, m_conv_ln_b, v_conv_ln_b, sm["conv_ln_b"]),
                    ("w_alpha_f", w_alpha_f, m_w_alpha_f, v_w_alpha_f, g_waf_loc), ("b_alpha_f", b_alpha_f, m_b_alpha_f, v_b_alpha_f, sm["b_alpha"][:, :QK]),
                    ("w_alpha_b", w_alpha_b, m_w_alpha_b, v_w_alpha_b, g_wab_loc), ("b_alpha_b", b_alpha_b, m_b_alpha_b, v_b_alpha_b, sm["b_alpha"][:, QK:]),
                    ("gla_norm_g", gla_norm_g, m_gla_norm_g, v_gla_norm_g, sm["gla_norm_g"]),
                    ("g_ffn2", g_ffn2, m_g_ffn2, v_g_ffn2, sm["g_ffn2"]), ("g_final", g_final, m_g_final, v_g_final, sm["g_final"])]

    def two_d(t, like):
        return t.reshape(like.shape[1:]) if like.ndim == 3 else t.reshape(like.size // 128, 128)

    small_res = _adamw_many([tuple(two_d(t, wv) for t in (wv, mv, vv, gv)) for _, wv, mv, vv, gv in small_params], "adamw_small")
    small_out = {nm: [gv.reshape(wv.shape)] + [t.reshape(wv.shape) for t in r3]
                 for (nm, wv, _, _, gv), r3 in zip(small_params, small_res)}

    order = ["c_ctx", "w_mod", "b_mod", "g_ffn1", "w1_gu", "w1_down", "g_mix", "w_in", "dw_weight", "dw_bias", "conv_ln_g", "conv_ln_b",
             "w_conv_out", "w_alpha_f", "b_alpha_f", "w_alpha_b", "b_alpha_b", "gla_norm_g", "w_gla_out", "w_out", "g_ffn2", "w2_gu",
             "w2_down", "g_final"]
    res = {**big, **small_out}
    return (loss, grad_x, *[res[n][0] for n in order], *[res[n][1] for n in order], *[res[n][2] for n in order], *[res[n][3] for n in order])
```

```python
import functools

import jax
import jax.numpy as jnp
from jax import lax
from jax.experimental import pallas as pl
from jax.experimental.pallas import tpu as pltpu

f32, bf16 = jnp.float32, jnp.bfloat16

N_DEV = 8
HEADS = 4
LOWRANK = 16
CONV_W = 31
CONV_PAD = 16
SUBLANES = 8
CHUNK = 64
SUB = 16
GLA_ROWS = 256
GLA_SAFE_DECAY = 60.0
TAU = 16.0
EPS = 1e-6
N_MOD = 9
LR_PAD = 128
ROW_TILE = 512
V7X_VMEM_BYTES = 64 << 20
VMEM_LIMIT = (V7X_VMEM_BYTES * 3) // 4

ADAM_LR, ADAM_B1, ADAM_B2, ADAM_EPS, ADAM_WD, ADAM_STEP = 0.001, 0.9, 0.999, 1e-08, 0.01, 10

MESH = pl.DeviceIdType.MESH


def _pc(body, **kw):
    return pl.pallas_call(body, **kw)


def _params(*sem):
    return pltpu.CompilerParams(dimension_semantics=sem, vmem_limit_bytes=VMEM_LIMIT)


def _pick(n, cap, unit=128):
    best = None
    for t in range(unit, min(n, cap) + 1, unit):
        if n % t == 0:
            best = t
    return best or n


def _matmul(a, b, mode, out_dtype, name, tm_cap=1024, tn_cap=1536, tk_cap=None, carry=None, halves=None):
    tk_cap = tk_cap or (2048 if mode == "tn" else 2816)
    if halves == "a":
        (_, M, Kh), N = a.shape, b.shape[0]
        K, tk = 2 * Kh, _pick(Kh, tk_cap)
        tm, tn = _pick(M, tm_cap), _pick(N, tn_cap)
        a_spec = pl.BlockSpec((None, tm, tk), lambda i, j, k: (k // (Kh // tk), i, k % (Kh // tk)))
    elif halves == "b":
        (K, M), (_, _, Nh) = a.shape, b.shape
        N, tn = 2 * Nh, _pick(Nh, tn_cap)
        tm, tk = _pick(M, tm_cap), _pick(K, tk_cap)
    else:
        if mode == "tn":
            (K, M), N = a.shape, b.shape[1]
        elif mode == "nt":
            (M, K), N = a.shape, b.shape[0]
        else:
            (M, K), N = a.shape, b.shape[1]
        tm, tn, tk = _pick(M, tm_cap), _pick(N, tn_cap), _pick(K, tk_cap)
    nk = K // tk
    if halves != "a":
        a_spec = pl.BlockSpec((tk, tm), lambda i, j, k: (k, i)) if mode == "tn" else pl.BlockSpec((tm, tk), lambda i, j, k: (i, k))
    if halves == "b":
        b_spec = pl.BlockSpec((None, tk, tn), lambda i, j, k: (j // (Nh // tn), k, j % (Nh // tn)))
    else:
        b_spec = pl.BlockSpec((tn, tk), lambda i, j, k: (j, k)) if mode == "nt" else pl.BlockSpec((tk, tn), lambda i, j, k: (k, j))
    dims = {"nn": ((1,), (0,)), "nt": ((1,), (1,)), "tn": ((0,), (0,))}[mode]

    def body_single(a_ref, b_ref, o_ref):
        o_ref[...] = lax.dot_general(a_ref[...].astype(bf16), b_ref[...].astype(bf16), (dims, ((), ())),
                                     preferred_element_type=f32).astype(out_dtype)

    def body(a_ref, b_ref, o_ref, acc_ref):
        k = pl.program_id(2)
        part = lax.dot_general(a_ref[...].astype(bf16), b_ref[...].astype(bf16), (dims, ((), ())), preferred_element_type=f32)

        @pl.when(k == 0)
        def _():
            acc_ref[...] = part

        @pl.when(k > 0)
        def _():
            acc_ref[...] += part

        @pl.when(k == nk - 1)
        def _():
            o_ref[...] = acc_ref[...].astype(out_dtype)

    (out,), carried = _call(
        body_single if nk == 1 else body, name=name, grid=(M // tm, N // tn, nk), in_specs=[a_spec, b_spec],
        out_specs=[pl.BlockSpec((tm, tn), lambda i, j, k: (i, j))], out_shape=[jax.ShapeDtypeStruct((M, N), out_dtype)],
        scratch_shapes=[] if nk == 1 else [pltpu.VMEM((tm, tn), f32)], sem=("parallel", "parallel", "arbitrary"),
        args=(a, b), carry=carry)
    return out if carry is None else (out, carried)


def _ffn_up(u, Wgu, name, carry=None):
    M, K = u.shape
    F = Wgu.shape[1] // 2
    tm, tn = _pick(M, 512), _pick(F, 1408)
    nj = F // tn

    def body(u_ref, wa_ref, wb_ref, gu_ref, h_ref):
        uv = u_ref[...]
        a = jnp.dot(uv, wa_ref[...], preferred_element_type=f32)
        b = jnp.dot(uv, wb_ref[...], preferred_element_type=f32)
        gu_ref[0] = a.astype(bf16)
        gu_ref[1] = b.astype(bf16)
        h_ref[...] = (jax.nn.silu(a) * b).astype(bf16)

    res, carried = _call(
        body, name=name, grid=(nj, M // tm),
        in_specs=[pl.BlockSpec((tm, K), lambda j, i: (i, 0)), pl.BlockSpec((K, tn), lambda j, i: (0, j)),
                  pl.BlockSpec((K, tn), lambda j, i: (0, nj + j))],
        out_specs=[pl.BlockSpec((2, tm, tn), lambda j, i: (0, i, j)), pl.BlockSpec((tm, tn), lambda j, i: (i, j))],
        out_shape=[jax.ShapeDtypeStruct((2, M, F), bf16), jax.ShapeDtypeStruct((M, F), bf16)],
        scratch_shapes=[], sem=("parallel", "parallel"), args=(u, Wgu, Wgu), carry=carry)
    return res if carry is None else (res, carried)


def _ffn_down_dx(df, Wd, gu, name):
    M, D = df.shape
    F = Wd.shape[0]
    tm, tn = _pick(M, 512), _pick(F, 1408)

    def body(df_ref, w_ref, gu_ref, o_ref):
        dh = lax.dot_general(df_ref[...], w_ref[...], (((1,), (1,)), ((), ())), preferred_element_type=f32)
        a, b = gu_ref[0].astype(f32), gu_ref[1].astype(f32)
        sg = jax.nn.sigmoid(a)
        o_ref[0] = (dh * b * sg * (1.0 + a * (1.0 - sg))).astype(bf16)
        o_ref[1] = (dh * a * sg).astype(bf16)

    return _pc(
        body, name=name, grid=(F // tn, M // tm),
        in_specs=[pl.BlockSpec((tm, D), lambda j, i: (i, 0)), pl.BlockSpec((tn, D), lambda j, i: (j, 0)),
                  pl.BlockSpec((2, tm, tn), lambda j, i: (0, i, j))],
        out_specs=pl.BlockSpec((2, tm, tn), lambda j, i: (0, i, j)), out_shape=jax.ShapeDtypeStruct((2, M, F), bf16),
        compiler_params=_params("parallel", "parallel"))(df, Wd, gu)


def _rowwise(fn, *, name, tm, n_tiles, tpe, nx_tiles, n_ex, tok_in=(), ex_in=(), sh_in=(), tok_out=(), ex_out=(), gl_out=(), carry=None):
    def seg(i):
        return jnp.minimum(i // tpe, n_ex - 1)

    in_specs, args = [], []
    for arr, w, cb, x_only in tok_in:
        if x_only == "c":
            in_specs.append(pl.BlockSpec((tm, w), functools.partial(lambda i, cb: (jnp.maximum(i - nx_tiles, 0), cb), cb=cb)))
        elif x_only:
            in_specs.append(pl.BlockSpec((tm, w), functools.partial(lambda i, cb: (jnp.minimum(i, nx_tiles - 1), cb), cb=cb)))
        else:
            in_specs.append(pl.BlockSpec((tm, w), functools.partial(lambda i, cb: (i, cb), cb=cb)))
        args.append(arr)
    for arr in ex_in:
        in_specs.append(pl.BlockSpec((1, 1, arr.shape[-1]), lambda i: (seg(i), 0, 0)))
        args.append(arr)
    for arr in sh_in:
        in_specs.append(pl.BlockSpec(arr.shape, functools.partial(lambda i, nd: (0,) * nd, nd=arr.ndim)))
        args.append(arr)
    out_specs, out_shape = [], []
    for w, dt, *x_rows in tok_out:
        if x_rows:
            out_specs.append(pl.BlockSpec((tm, w), lambda i: (jnp.minimum(i, nx_tiles - 1), 0)))
        else:
            out_specs.append(pl.BlockSpec((tm, w), lambda i: (i, 0)))
        out_shape.append(jax.ShapeDtypeStruct(((nx_tiles if x_rows else n_tiles) * tm, w), dt))
    for w in ex_out:
        out_specs.append(pl.BlockSpec((1, 1, w), lambda i: (seg(i), 0, 0)))
        out_shape.append(jax.ShapeDtypeStruct((n_ex, 1, w), f32))
    for r, w in gl_out:
        out_specs.append(pl.BlockSpec((r, w), lambda i: (0, 0)))
        out_shape.append(jax.ShapeDtypeStruct((r, w), f32))
    n_tok, n_exi, n_sh = len(tok_in), len(ex_in), len(sh_in)
    n_to, n_eo = len(tok_out), len(ex_out)
    x_only_flags = [t[3] for t in tok_in]
    x_rows_flags = [len(t) > 2 for t in tok_out]

    def body(*refs):
        i = pl.program_id(0)
        ins, outs = refs[: n_tok + n_exi + n_sh], refs[n_tok + n_exi + n_sh:]
        is_x = i < nx_tiles
        tok_vals = []
        for r, xo in zip(ins[:n_tok], x_only_flags):
            v = r[...]
            if xo == "c":
                v = jnp.where(is_x, jnp.zeros_like(v), v)
            elif xo:
                v = jnp.where(is_x, v, jnp.zeros_like(v))
            tok_vals.append(v)
        ex_vals = [r[0] for r in ins[n_tok:n_tok + n_exi]]
        sh_vals = [r[...] for r in ins[n_tok + n_exi:]]
        t_o, e_o, g_o = fn(tok_vals, ex_vals, sh_vals)
        for r, v, xr in zip(outs[:n_to], t_o, x_rows_flags):
            if xr:
                @pl.when(is_x)
                def _(r=r, v=v):
                    r[...] = v.astype(r.dtype)
            else:
                r[...] = v.astype(r.dtype)
        first = jnp.logical_and(i % tpe == 0, i <= nx_tiles)
        for r, v in zip(outs[n_to:n_to + n_eo], e_o):
            @pl.when(first)
            def _(r=r, v=v):
                r[0] = v

            @pl.when(jnp.logical_not(first))
            def _(r=r, v=v):
                r[0] += v
        for r, v in zip(outs[n_to + n_eo:], g_o):
            @pl.when(i == 0)
            def _(r=r, v=v):
                r[...] = v

            @pl.when(i > 0)
            def _(r=r, v=v):
                r[...] += v

    res, carried = _call(body, name=name, grid=(n_tiles,), in_specs=in_specs, out_specs=out_specs, out_shape=out_shape,
                         scratch_shapes=[], sem=("arbitrary",), args=args, carry=carry)
    return res if carry is None else (res, carried)


def _rms_mod(x, g, sh, sc):
    y = x * lax.rsqrt(jnp.mean(x * x, axis=-1, keepdims=True) + EPS) * g
    return y * (1.0 + sc) + sh


def _log_sigmoid(z):
    return jnp.minimum(z, 0.0) - jnp.log(1.0 + jnp.exp(-jnp.abs(z)))


def _head_rms(o, DV):
    parts = []
    for h in range(HEADS):
        oh = o[:, h * DV:(h + 1) * DV]
        parts.append(oh * lax.rsqrt(jnp.mean(oh * oh, axis=-1, keepdims=True) + EPS))
    return jnp.concatenate(parts, axis=1)


@functools.partial(jax.custom_vjp, nondiff_argnums=(2,))
def _bdot(a, b, dims):
    return lax.dot_general(a.astype(bf16), b.astype(bf16), (((dims[0],), (dims[1],)), ((), ())), preferred_element_type=f32)


def _bdot_fwd(a, b, dims):
    return _bdot(a, b, dims), (a, b)


def _bdot_bwd(dims, res, g):
    a, b = res
    ca, cb = dims
    da = _bdot(g, b, (1, 1 - cb)) if ca == 1 else _bdot(b, g, (1 - cb, 1))
    db = _bdot(a, g, (1 - ca, 0)) if cb == 0 else _bdot(g, a, (0, 1 - ca))
    return da, db


_bdot.defvjp(_bdot_fwd, _bdot_bwd)


def _split_dot(m, x, dims):
    mb, rem, acc = m.astype(bf16), x, None
    for _ in range(3):
        piece = rem.astype(bf16)
        rem = rem - piece.astype(f32)
        part = lax.dot_general(mb, piece, (((dims[0],), (dims[1],)), ((), ())), preferred_element_type=f32)
        acc = part if acc is None else acc + part
    return acc


@jax.custom_vjp
def _tri_cumsum(tri, g):
    return _split_dot(tri, g, (1, 0))


def _tri_cumsum_fwd(tri, g):
    return _tri_cumsum(tri, g), tri


def _tri_cumsum_bwd(tri, db):
    return jnp.zeros_like(tri), _split_dot(tri, db, (0, 0))


_tri_cumsum.defvjp(_tri_cumsum_fwd, _tri_cumsum_bwd)


def _gla_chunk(St, q, k, v, g, *, rev, scale, exact):
    C, DK = q.shape
    r = lax.broadcasted_iota(jnp.int32, (C, C), 0)
    c = lax.broadcasted_iota(jnp.int32, (C, C), 1)
    causal = (r <= c) if rev else (r >= c)
    b = _tri_cumsum(causal.astype(f32), g)
    qs = q * scale
    qe = qs * jnp.exp(b)
    inter = _bdot(qe, St, (1, 1))
    b_last = b[0:1] if rev else b[C - 1:C]
    kd = k * jnp.exp(b_last - b)
    St_new = St * jnp.exp(b_last) + _bdot(v, kd, (0, 0))
    if not exact:
        att = jnp.where(causal, _bdot(qe, k * jnp.exp(-b), (1, 1)), 0.0)
        return St_new, inter + _bdot(att, v, (1, 0))
    rr = lax.broadcasted_iota(jnp.int32, (SUB, SUB, DK), 0)
    cc = lax.broadcasted_iota(jnp.int32, (SUB, SUB, DK), 1)
    m3 = (rr <= cc) if rev else (rr >= cc)
    outs = []
    for i in range(C // SUB):
        lo, hi = i * SUB, (i + 1) * SUB
        bi, qi, ki, vi = b[lo:hi], qs[lo:hi], k[lo:hi], v[lo:hi]
        rel = bi[:, None, :] - bi[None, :, :]
        e = jnp.where(m3, jnp.exp(jnp.where(m3, rel, 0.0)), 0.0)
        att = jnp.sum(qi[:, None, :] * e * ki[None, :, :], axis=-1)
        acc = _bdot(att, vi, (1, 0))
        ref_row = b[hi - 1:hi] if rev else b[lo:lo + 1]
        prev = slice(hi, C) if rev else slice(0, lo)
        if (hi < C) if rev else (lo > 0):
            qn = qi * jnp.exp(bi - ref_row)
            ks = k[prev] * jnp.exp(ref_row - b[prev])
            acc = acc + _bdot(_bdot(qn, ks, (1, 1)), v[prev], (1, 0))
        outs.append(acc)
    return St_new, inter + jnp.concatenate(outs, axis=0)


def _mild_decay(la_ref):
    return jnp.min(la_ref[...]) >= -GLA_SAFE_DECAY / CHUNK


def _gla_specs(D, rev_blocks, row0, seq):
    DK, DV = D // (2 * HEADS), D // HEADS
    nblk = seq // GLA_ROWS
    rb0 = row0 // GLA_ROWS

    def blk(j):
        return (nblk - 1 - j) if rev_blocks else j

    return DK, DV, nblk, rb0, blk


def _gla_in_specs(D, rev, rows):
    QK = D // 2
    return [
        pl.BlockSpec((GLA_ROWS, QK), lambda b, j: (rows(b, j), 6 * D // QK)),
        pl.BlockSpec((GLA_ROWS, QK), lambda b, j: (rows(b, j), 6 * D // QK + 1)),
        pl.BlockSpec((GLA_ROWS, D), lambda b, j: (rows(b, j), 2)),
        pl.BlockSpec((GLA_ROWS, QK), lambda b, j: (rows(b, j), 1 if rev else 0)),
    ]


def _gla_fwd(p_all, la_all, s0, *, rev, row0, nb, seq, D, name, carry=None):
    DK, DV, nblk, rb0, blk = _gla_specs(D, rev, row0, seq)
    cpb = GLA_ROWS // CHUNK

    def rows(b, j):
        return rb0 + b * nblk + blk(j)

    in_specs = _gla_in_specs(D, rev, rows) + [pl.BlockSpec((1, HEADS, DV, DK), lambda b, j: (b, 0, 0, 0))]
    out_specs = [
        pl.BlockSpec((GLA_ROWS, D), lambda b, j: (b * nblk + blk(j), 0)),
        pl.BlockSpec((1, HEADS, cpb, DV, DK), lambda b, j: (b, 0, blk(j), 0, 0)),
        pl.BlockSpec((1, HEADS, DV, DK), lambda b, j: (b, 0, 0, 0)),
    ]
    out_shape = [
        jax.ShapeDtypeStruct((nb * seq, D), bf16),
        jax.ShapeDtypeStruct((nb, HEADS, seq // CHUNK, DV, DK), bf16),
        jax.ShapeDtypeStruct((nb, HEADS, DV, DK), f32),
    ]
    chunk = functools.partial(_gla_chunk, rev=rev, scale=DK ** -0.5)

    def body(q_ref, k_ref, v_ref, la_ref, s0_ref, o_ref, hist_ref, sfin_ref, st_ref):
        j = pl.program_id(1)

        @pl.when(j == 0)
        def _():
            st_ref[...] = s0_ref[0]

        def step(ci, exact):
            cc = (cpb - 1 - ci) if rev else ci
            sl = pl.ds(cc * CHUNK, CHUNK)
            for h in range(HEADS):
                kq, kv = pl.ds(h * DK, DK), pl.ds(h * DV, DV)
                St = st_ref[h]
                hist_ref[0, h, cc] = St.astype(bf16)
                St2, o = chunk(St, q_ref[sl, kq].astype(f32), k_ref[sl, kq].astype(f32), v_ref[sl, kv].astype(f32), la_ref[sl, kq],
                               exact=exact)
                o_ref[sl, kv] = o.astype(bf16)
                st_ref[h] = St2

        mild = _mild_decay(la_ref)
        for exact in (False, True):
            @pl.when(jnp.logical_not(mild) if exact else mild)
            def _(exact=exact):
                for ci in range(cpb):
                    step(ci, exact)

        @pl.when(j == nblk - 1)
        def _():
            sfin_ref[0] = st_ref[...]

    res, carried = _call(body, name=name, grid=(nb, nblk), in_specs=in_specs, out_specs=out_specs, out_shape=out_shape,
                         scratch_shapes=[pltpu.VMEM((HEADS, DV, DK), f32)], sem=("parallel", "arbitrary"),
                         args=(p_all, p_all, p_all, la_all, s0), carry=carry)
    return res if carry is None else (res, carried)


def _gla_bwd(p_all, la_all, hist, do, dsfin, *, rev, row0, nb, seq, D, name, add=None):
    DK, DV, nblk, rb0, blk = _gla_specs(D, not rev, row0, seq)
    cpb = GLA_ROWS // CHUNK
    QK = HEADS * DK
    has_do = do is not None

    def rows(b, j):
        return rb0 + b * nblk + blk(j)

    in_specs = _gla_in_specs(D, rev, rows) + [
        pl.BlockSpec((1, HEADS, cpb, DV, DK), lambda b, j: (b, 0, blk(j), 0, 0)),
        pl.BlockSpec((1, HEADS, DV, DK), lambda b, j: (b, 0, 0, 0)),
    ]
    args = [p_all, p_all, p_all, la_all, hist, dsfin]
    if has_do:
        in_specs.append(pl.BlockSpec((GLA_ROWS, D), lambda b, j: (b * nblk + blk(j), 0)))
        args.append(do)
    if add is not None:
        in_specs += [pl.BlockSpec((GLA_ROWS, t.shape[1]), lambda b, j: (b * nblk + blk(j), 0)) for t in add]
        args += list(add)
    gdt = f32 if add is None else bf16
    out_specs = [
        pl.BlockSpec((GLA_ROWS, QK), lambda b, j: (b * nblk + blk(j), 0)),
        pl.BlockSpec((GLA_ROWS, QK), lambda b, j: (b * nblk + blk(j), 0)),
        pl.BlockSpec((GLA_ROWS, D), lambda b, j: (b * nblk + blk(j), 0)),
        pl.BlockSpec((GLA_ROWS, QK), lambda b, j: (b * nblk + blk(j), 0)),
        pl.BlockSpec((1, HEADS, DV, DK), lambda b, j: (b, 0, 0, 0)),
    ]
    out_shape = [
        jax.ShapeDtypeStruct((nb * seq, QK), gdt), jax.ShapeDtypeStruct((nb * seq, QK), gdt),
        jax.ShapeDtypeStruct((nb * seq, D), gdt), jax.ShapeDtypeStruct((nb * seq, QK), f32),
        jax.ShapeDtypeStruct((nb, HEADS, DV, DK), f32),
    ]
    chunk = functools.partial(_gla_chunk, rev=rev, scale=DK ** -0.5)

    def body(*refs):
        refs = list(refs)
        q_ref, k_ref, v_ref, la_ref, hist_ref, dsfin_ref = refs[:6]
        do_ref = refs[6] if has_do else None
        add_refs = refs[6 + has_do:len(refs) - 6]
        dq_ref, dk_ref, dv_ref, dla_ref, ds0_ref, ds_ref = refs[len(refs) - 6:]
        j = pl.program_id(1)

        @pl.when(j == 0)
        def _():
            ds_ref[...] = dsfin_ref[0]

        def step(ci, exact):
            cc = ci if rev else (cpb - 1 - ci)
            sl = pl.ds(cc * CHUNK, CHUNK)
            for h in range(HEADS):
                kq, kv = pl.ds(h * DK, DK), pl.ds(h * DV, DV)
                prim = (hist_ref[0, h, cc].astype(f32), q_ref[sl, kq].astype(f32), k_ref[sl, kq].astype(f32), v_ref[sl, kv].astype(f32), la_ref[sl, kq])
                _, vjp = jax.vjp(functools.partial(chunk, exact=exact), *prim)
                d_o = do_ref[sl, kv].astype(f32) if has_do else jnp.zeros((CHUNK, DV), f32)
                dSt, dq, dk, dv, dg = vjp((ds_ref[h], d_o))
                if add is not None:
                    dq, dk, dv = dq + add_refs[0][sl, kq], dk + add_refs[1][sl, kq], dv + add_refs[2][sl, kv]
                dq_ref[sl, kq] = dq.astype(gdt)
                dk_ref[sl, kq] = dk.astype(gdt)
                dv_ref[sl, kv] = dv.astype(gdt)
                dla_ref[sl, kq] = dg
                ds_ref[h] = dSt

        mild = _mild_decay(la_ref)
        for exact in (False, True):
            @pl.when(jnp.logical_not(mild) if exact else mild)
            def _(exact=exact):
                for ci in range(cpb):
                    step(ci, exact)

        @pl.when(j == nblk - 1)
        def _():
            ds0_ref[0] = ds_ref[...]

    return _pc(body, name=name, grid=(nb, nblk), in_specs=in_specs, out_specs=out_specs, out_shape=out_shape,
               scratch_shapes=[pltpu.VMEM((HEADS, DV, DK), f32)], compiler_params=_params("parallel", "arbitrary"))(*args)


def _conv_fwd(p_all, dw_w, dw_b, *, B, L, D, name):
    ct = _pick(D, 256)
    nj = D // ct
    st = _pick(L, 128, 8)
    off = CONV_PAD - CONV_W // 2

    def body(a_ref, b_ref, w_ref, bias_ref, o_ref, zs_ref):
        _fill_shifted(zs_ref, L, lambda t0, n: a_ref[pl.ds(t0, n), :].astype(f32) * jax.nn.sigmoid(b_ref[pl.ds(t0, n), :].astype(f32)))
        for t0 in range(0, L, st):
            acc = jnp.zeros((st, ct), f32) + bias_ref[...]
            for k in range(CONV_W):
                acc = acc + w_ref[pl.ds(k, 1), :] * _window(zs_ref, t0 + k + off, st)
            o_ref[pl.ds(t0, st), :] = acc.astype(bf16)

    return _pc(
        body, name=name, grid=(B, nj),
        in_specs=[pl.BlockSpec((L, ct), lambda b, j: (b, j)), pl.BlockSpec((L, ct), lambda b, j: (b, nj + j)),
                  pl.BlockSpec((CONV_W, ct), lambda b, j: (0, j)), pl.BlockSpec((1, ct), lambda b, j: (0, j))],
        out_specs=pl.BlockSpec((L, ct), lambda b, j: (b, j)), out_shape=jax.ShapeDtypeStruct((B * L, D), bf16),
        scratch_shapes=[pltpu.VMEM((SUBLANES, L + 2 * CONV_PAD, ct), f32)], compiler_params=_params("parallel", "parallel"),
    )(p_all, p_all, dw_w, dw_b)


def _fill_shifted(zs_ref, L, rows):
    lp = L + 2 * CONV_PAD
    ct = zs_ref.shape[2]
    step = 256
    zs_ref[0, pl.ds(0, CONV_PAD), :] = jnp.zeros((CONV_PAD, ct), f32)
    zs_ref[0, pl.ds(CONV_PAD + L, CONV_PAD), :] = jnp.zeros((CONV_PAD, ct), f32)
    for t0 in range(0, L, step):
        n = min(step, L - t0)
        zs_ref[0, pl.ds(CONV_PAD + t0, n), :] = rows(t0, n)
    for r in range(1, SUBLANES):
        for i0 in range(0, lp - SUBLANES, step):
            n = min(step, lp - SUBLANES - i0)
            zs_ref[r, pl.ds(i0, n), :] = zs_ref[0, pl.ds(i0 + r, n), :]


def _window(zs_ref, start, n):
    r = start % SUBLANES
    return zs_ref[r, pl.ds(start - r, n), :]


def _conv_bwd(p_all, dcz, dw_w, *, B, L, D, name, carry=None):
    ct = _pick(D, 128)
    nj = D // ct
    st = _pick(L, 256, 8)
    half = CONV_W // 2

    def body(a_ref, b_ref, dcz_ref, w_ref, da_ref, db_ref, ddw_ref, zs_ref, ds_ref):
        bi = pl.program_id(1)
        _fill_shifted(zs_ref, L, lambda t0, n: a_ref[pl.ds(t0, n), :].astype(f32) * jax.nn.sigmoid(b_ref[pl.ds(t0, n), :].astype(f32)))
        _fill_shifted(ds_ref, L, lambda t0, n: dcz_ref[pl.ds(t0, n), :].astype(f32))

        @pl.when(bi == 0)
        def _():
            ddw_ref[...] = jnp.zeros_like(ddw_ref)

        for t0 in range(0, L, st):
            acc = jnp.zeros((st, ct), f32)
            for k in range(CONV_W):
                acc = acc + w_ref[pl.ds(k, 1), :] * _window(ds_ref, t0 + CONV_PAD + half - k, st)
            a_t = a_ref[pl.ds(t0, st), :].astype(f32)
            sg_t = jax.nn.sigmoid(b_ref[pl.ds(t0, st), :].astype(f32))
            da_ref[pl.ds(t0, st), :] = (acc * sg_t).astype(bf16)
            db_ref[pl.ds(t0, st), :] = (acc * a_t * sg_t * (1.0 - sg_t)).astype(bf16)

        parts = [jnp.zeros((SUBLANES, ct), f32) for _ in range(CONV_W)]
        sw = _pick(L, 64, SUBLANES)
        for t0 in range(0, L, sw):
            dout = dcz_ref[pl.ds(t0, sw), :].astype(f32)
            for k in range(CONV_W):
                prod = dout * _window(zs_ref, t0 + k + CONV_PAD - half, sw)
                for i in range(0, sw, SUBLANES):
                    parts[k] = parts[k] + prod[i:i + SUBLANES]
        for k in range(CONV_W):
            ddw_ref[pl.ds(k, 1), :] += jnp.sum(parts[k], axis=0, keepdims=True)

    res, carried = _call(
        body, name=name, grid=(nj, B),
        in_specs=[pl.BlockSpec((L, ct), lambda j, b: (b, j)), pl.BlockSpec((L, ct), lambda j, b: (b, nj + j)),
                  pl.BlockSpec((L, ct), lambda j, b: (b, j)), pl.BlockSpec((CONV_W, ct), lambda j, b: (0, j))],
        out_specs=[pl.BlockSpec((L, ct), lambda j, b: (b, j)), pl.BlockSpec((L, ct), lambda j, b: (b, j)),
                   pl.BlockSpec((2 * CONV_PAD, ct), lambda j, b: (0, j))],
        out_shape=[jax.ShapeDtypeStruct((B * L, D), bf16), jax.ShapeDtypeStruct((B * L, D), bf16),
                   jax.ShapeDtypeStruct((2 * CONV_PAD, D), f32)],
        scratch_shapes=[pltpu.VMEM((SUBLANES, L + 2 * CONV_PAD, ct), f32), pltpu.VMEM((SUBLANES, L + 2 * CONV_PAD, ct), f32)],
        sem=("parallel", "arbitrary"), args=(p_all, p_all, dcz, dw_w), carry=carry)
    return res if carry is None else (res, carried)


def _exchange(arrs, scatter, name):
    ex = _Exchange(arrs, scatter)
    n = ex.n

    def body(*refs):
        ex.start(refs[:n], refs[n:2 * n], refs[2 * n:])
        ex.finish(refs[:n], refs[n:2 * n], refs[2 * n:])

    res = _pc(body, name=name, in_specs=ex.specs, out_specs=ex.specs, out_shape=ex.out_shape, scratch_shapes=ex.scratch)(*arrs)
    return list(res)


class _Exchange:
    def __init__(self, arrs, scatter):
        self.arrs, self.scatter, self.n = list(arrs), scatter, len(arrs)
        self.out_shape = [jax.ShapeDtypeStruct(((N_DEV,) + a.shape[1:]) if scatter else ((N_DEV,) + a.shape), a.dtype) for a in arrs]
        self.specs = [pl.BlockSpec(memory_space=pl.ANY)] * self.n
        self.scratch = [pltpu.SemaphoreType.DMA((self.n, N_DEV - 1)), pltpu.SemaphoreType.DMA((self.n, N_DEV - 1)),
                        pltpu.SemaphoreType.DMA((self.n,))]

    def _copies(self, ins, outs, sems, landing):
        send_sems, recv_sems, local_sems = sems
        me = 4 * lax.axis_index("x") + 2 * lax.axis_index("y") + lax.axis_index("c")
        if landing:
            local = []
        else:
            local = [pltpu.make_async_copy(ins[a].at[me] if self.scatter else ins[a], outs[a].at[me], local_sems.at[a]) for a in range(self.n)]
        remote = []
        for k in range(1, N_DEV):
            p = (me + (N_DEV - k if landing else k)) % N_DEV
            for a in range(self.n):
                remote.append(pltpu.make_async_remote_copy(
                    src_ref=ins[a].at[p] if self.scatter else ins[a], dst_ref=outs[a].at[p if landing else me],
                    send_sem=send_sems.at[a, k - 1], recv_sem=recv_sems.at[a, k - 1],
                    device_id=(p // 4, (p // 2) % 2, p % 2), device_id_type=MESH))
        return local, remote

    def _gather_plan(self, ins, outs, sems):
        send_sems, recv_sems, local_sems = sems
        x, y, c = lax.axis_index("x"), lax.axis_index("y"), lax.axis_index("c")
        chips = [(1 - x, y), (x, 1 - y), (1 - x, 1 - y)]

        def blk(px, py, pc):
            return 4 * px + 2 * py + pc

        def copy(a, k, block, to, own):
            return pltpu.make_async_remote_copy(
                src_ref=ins[a] if own else outs[a].at[block], dst_ref=outs[a].at[block],
                send_sem=send_sems.at[a, k], recv_sem=recv_sems.at[a, k], device_id=to, device_id_type=MESH)

        me = blk(x, y, c)
        local = [pltpu.make_async_copy(ins[a], outs[a].at[me], local_sems.at[a]) for a in range(self.n)]
        return local, copy, me, (x, y, 1 - c), chips, blk, c

    def start(self, ins, outs, sems):
        if self.scatter:
            local, sends = self._copies(ins, outs, sems, False)
            for cp in local + sends:
                cp.start()
            return
        local, copy, me, sibling, chips, _, c = self._gather_plan(ins, outs, sems)
        for cp in local:
            cp.start()
        for a in range(self.n):
            copy(a, 0, me, sibling, True).start()
            for j, chip in enumerate(chips):
                copy(a, 1 + j, me, (*chip, c), True).start()

    def finish(self, ins, outs, sems):
        if self.scatter:
            for cp in self._copies(ins, outs, sems, True)[1]:
                cp.wait_recv()
            local, sends = self._copies(ins, outs, sems, False)
            for cp in sends:
                cp.wait_send()
            for cp in local:
                cp.wait()
            return
        local, copy, me, sibling, chips, blk, c = self._gather_plan(ins, outs, sems)
        for j, chip in enumerate(chips):
            for a in range(self.n):
                copy(a, 1 + j, blk(*chip, c), sibling, True).wait_recv()
                copy(a, 4 + j, blk(*chip, c), sibling, False).start()
        for a in range(self.n):
            copy(a, 0, blk(*sibling), sibling, True).wait_recv()
            for j, chip in enumerate(chips):
                copy(a, 4 + j, blk(*chip, 1 - c), sibling, False).wait_recv()
        for a in range(self.n):
            copy(a, 0, me, sibling, True).wait_send()
            for j, chip in enumerate(chips):
                copy(a, 1 + j, me, (*chip, c), True).wait_send()
                copy(a, 4 + j, blk(*chip, c), sibling, False).wait_send()
        for cp in local:
            cp.wait()


def _carried(inner, n_in, n_out, grid, ex):
    n = ex.n

    def body(*refs):
        own_in, c_in = refs[:n_in], refs[n_in:n_in + n]
        own_out, c_out = refs[n_in + n:n_in + n + n_out], refs[n_in + n + n_out:n_in + 2 * n + n_out]
        rest = refs[n_in + 2 * n + n_out:]
        own_scr, sems = rest[:len(rest) - 3], rest[len(rest) - 3:]
        pids = [pl.program_id(d) for d in range(len(grid))]
        first = functools.reduce(jnp.logical_and, [p == 0 for p in pids])
        last = functools.reduce(jnp.logical_and, [p == g - 1 for p, g in zip(pids, grid)])

        @pl.when(first)
        def _():
            ex.start(c_in, c_out, sems)

        inner(*own_in, *own_out, *own_scr)

        @pl.when(last)
        def _():
            ex.finish(c_in, c_out, sems)

    return body


def _call(inner, *, name, grid, in_specs, out_specs, out_shape, scratch_shapes, sem, args, carry=None):
    if carry is None:
        res = _pc(inner, name=name, grid=grid, in_specs=in_specs, out_specs=out_specs, out_shape=out_shape,
                  scratch_shapes=scratch_shapes, compiler_params=_params(*sem))(*args)
        return list(res), None
    ex = _Exchange(*carry)
    res = _pc(_carried(inner, len(in_specs), len(out_specs), grid, ex), name=name, grid=grid,
              in_specs=list(in_specs) + ex.specs, out_specs=list(out_specs) + ex.specs, out_shape=list(out_shape) + ex.out_shape,
              scratch_shapes=list(scratch_shapes) + ex.scratch, compiler_params=_params(*(["arbitrary"] * len(grid))))(*args, *ex.arrs)
    res = list(res)
    return res[:len(out_specs)], res[len(out_specs):]


def _mod_fwd(c_all, c_ctx, w_loc, b_loc, name):
    nr, D = c_all.shape
    nc = w_loc.shape[1]

    def body(c_ref, cc_ref, w_ref, b_ref, o_ref):
        a = jnp.concatenate([c_ref[...], jnp.broadcast_to(cc_ref[...], (8, D))], axis=0)
        s = jax.nn.silu(a).astype(bf16)
        o_ref[...] = jnp.dot(s, w_ref[...].astype(bf16), preferred_element_type=f32) + b_ref[...]

    return _pc(body, name=name, out_shape=jax.ShapeDtypeStruct((nr + 8, nc), f32), compiler_params=_params())(c_all, c_ctx, w_loc, b_loc)


def _mod_bwd(c_all, c_ctx, w_loc, dmx_loc, dmc_loc, name):
    nr, D = c_all.shape
    nc = w_loc.shape[1]

    def body(c_ref, cc_ref, w_ref, dmx_ref, dmc_ref, gw_ref, gc_ref):
        cc = cc_ref[...]
        a = jnp.concatenate([c_ref[...], jnp.broadcast_to(cc, (N_DEV, D))], axis=0)
        s = jax.nn.silu(a).astype(bf16)
        g = jnp.concatenate([dmx_ref[...], dmc_ref[...]], axis=0).astype(bf16)
        gw_ref[...] = lax.dot_general(s, g, (((0,), (0,)), ((), ())), preferred_element_type=f32)
        dmc = jnp.sum(dmc_ref[...], axis=0, keepdims=True)
        ds = lax.dot_general(jnp.broadcast_to(dmc, (8, nc)).astype(bf16), w_ref[...].astype(bf16), (((1,), (1,)), ((), ())),
                             preferred_element_type=f32)[0:1]
        sg = jax.nn.sigmoid(cc)
        gc_ref[...] = ds * (sg * (1.0 + cc * (1.0 - sg)))

    return _pc(body, name=name, out_shape=[jax.ShapeDtypeStruct((D, nc), f32), jax.ShapeDtypeStruct((1, D), f32)],
               compiler_params=_params())(c_all, c_ctx, w_loc, dmx_loc, dmc_loc)


def _adamw_math(w, g, m, v):
    m2 = ADAM_B1 * m + (1.0 - ADAM_B1) * g
    v2 = ADAM_B2 * v + (1.0 - ADAM_B2) * jnp.square(g)
    m_hat = m2 / (1.0 - ADAM_B1 ** ADAM_STEP)
    v_hat = v2 / (1.0 - ADAM_B2 ** ADAM_STEP)
    delta = -ADAM_LR * (m_hat / (jnp.sqrt(v_hat) + ADAM_EPS) + ADAM_WD * w)
    return delta, m2, v2


def _adamw_many(params, name):
    n = len(params)

    def body(*refs):
        ins, outs = refs[:4 * n], refs[4 * n:]
        for i in range(n):
            w, m, v, g = (ins[4 * i + k][...] for k in range(4))
            d, m2, v2 = _adamw_math(w, g, m, v)
            outs[3 * i][...] = d
            outs[3 * i + 1][...] = m2
            outs[3 * i + 2][...] = v2

    res = _pc(body, name=name, out_shape=[jax.ShapeDtypeStruct(p[0].shape, f32) for p in params for _ in range(3)],
              compiler_params=_params())(*[a for p in params for a in p])
    return [tuple(res[3 * i:3 * i + 3]) for i in range(n)]


def _adamw(w, m, v, g, name, partials, carry=None):
    r, cdim = w.shape
    tr = _pick(r, 256, 8)

    def body(w_ref, m_ref, v_ref, g_ref, og_ref, od_ref, om_ref, ov_ref):
        if partials:
            g = g_ref[0].astype(f32)
            for s in range(1, N_DEV):
                g = g + g_ref[s].astype(f32)
        else:
            g = g_ref[...]
        d, m2, v2 = _adamw_math(w_ref[...], g, m_ref[...], v_ref[...])
        og_ref[...] = g
        od_ref[...] = d
        om_ref[...] = m2
        ov_ref[...] = v2

    blk = pl.BlockSpec((tr, cdim), lambda i: (i, 0))
    g_spec = pl.BlockSpec((N_DEV, tr, cdim), lambda i: (0, i, 0)) if partials else blk
    res, carried = _call(body, name=name, grid=(r // tr,), in_specs=[blk, blk, blk, g_spec], out_specs=[blk] * 4,
                         out_shape=[jax.ShapeDtypeStruct((r, cdim), f32)] * 4, scratch_shapes=[], sem=("parallel",),
                         args=(w, m, v, g), carry=carry)
    return res if carry is None else (res, carried)


def _sum_sources(parts, name):
    def body(*refs):
        for i_ref, o_ref in zip(refs[:len(parts)], refs[len(parts):]):
            acc = i_ref[0]
            for s in range(1, i_ref.shape[0]):
                acc = acc + i_ref[s]
            o_ref[...] = acc

    return list(_pc(body, name=name, out_shape=[jax.ShapeDtypeStruct(p.shape[1:], f32) for p in parts],
                    compiler_params=_params())(*parts))


def kernel(x, c, ctx, c_ctx, w_mod, b_mod, g_ffn1, w1_gu, w1_down, g_mix, w_in, dw_weight, dw_bias, conv_ln_g, conv_ln_b, w_conv_out, w_alpha_f, b_alpha_f, w_alpha_b, b_alpha_b, gla_norm_g, w_gla_out, w_out, g_ffn2, w2_gu, w2_down, g_final, loss_target, m_c_ctx, m_w_mod, m_b_mod, m_g_ffn1, m_w1_gu, m_w1_down, m_g_mix, m_w_in, m_dw_weight, m_dw_bias, m_conv_ln_g, m_conv_ln_b, m_w_conv_out, m_w_alpha_f, m_b_alpha_f, m_w_alpha_b, m_b_alpha_b, m_gla_norm_g, m_w_gla_out, m_w_out, m_g_ffn2, m_w2_gu, m_w2_down, m_g_final, v_c_ctx, v_w_mod, v_b_mod, v_g_ffn1, v_w1_gu, v_w1_down, v_g_mix, v_w_in, v_dw_weight, v_dw_bias, v_conv_ln_g, v_conv_ln_b, v_w_conv_out, v_w_alpha_f, v_b_alpha_f, v_w_alpha_b, v_b_alpha_b, v_gla_norm_g, v_w_gla_out, v_w_out, v_g_ffn2, v_w2_gu, v_w2_down, v_g_final):
    B, L, D = x.shape
    Lc = ctx.shape[1]
    T, Tc = B * L, B * Lc
    Tall = T + Tc
    F = w1_down.shape[1] * N_DEV
    DK, DV = D // (2 * HEADS), D // HEADS
    QK = HEADS * DK
    PW = 7 * D + LR_PAD
    tm = ROW_TILE
    tpe = L // tm
    nx, nall = T // tm, Tall // tm
    me = 4 * lax.axis_index("x") + 2 * lax.axis_index("y") + lax.axis_index("c")

    rw_all = dict(tm=tm, n_tiles=nall, tpe=tpe, nx_tiles=nx, n_ex=B + 1)
    rw_x = dict(tm=tm, n_tiles=nx, tpe=tpe, nx_tiles=nx, n_ex=B)
    rw_all2, rw_x2 = rw_all, rw_x
    mb = 2 if (L % (2 * tm) == 0 and Tc % (2 * tm) == 0) else 1
    rw_all_b = dict(tm=mb * tm, n_tiles=nall // mb, tpe=tpe // mb, nx_tiles=nx // mb, n_ex=B + 1)
    rw_x_b = dict(tm=mb * tm, n_tiles=nx // mb, tpe=tpe // mb, nx_tiles=nx // mb, n_ex=B)

    dww_g, waf_g, wab_g, c_g = _exchange([dw_weight[0], w_alpha_f[0], w_alpha_b[0], c], False, "gather_first")

    def cols(gat):
        return jnp.transpose(gat, (1, 0, 2)).reshape(gat.shape[1], N_DEV * gat.shape[2])

    def rows_(gat):
        return gat.reshape(N_DEV * gat.shape[1], gat.shape[2])

    dww = cols(dww_g)
    WA = jnp.zeros((LR_PAD, 2 * QK), f32).at[:LOWRANK, :QK].set(cols(waf_g)).at[LOWRANK:2 * LOWRANK, QK:].set(cols(wab_g)).astype(bf16)
    BA = jnp.concatenate([b_alpha_f, b_alpha_b], axis=1)
    c_all = c_g.reshape(N_DEV * B, D)
    c_ctx2 = c_ctx.reshape(1, D)

    ncm = w_mod.shape[2]
    b_mod_loc = lax.dynamic_slice(b_mod, (0, me * ncm), (1, ncm))
    mod_loc = _mod_fwd(c_all, c_ctx2, w_mod[0], b_mod_loc, "mod_fwd")
    (mod_g,) = _exchange([mod_loc], False, "gather_mod")
    mod_full = cols(mod_g)
    mod_tab = jnp.concatenate([lax.dynamic_slice(mod_full, (me * B, 0), (B, N_MOD * D)), mod_full[N_DEV * B:N_DEV * B + 1]], axis=0)
    mods = [mod_tab[:, i * D:(i + 1) * D].reshape(B + 1, 1, D) for i in range(N_MOD)]
    mods_x = [mm[:B] for mm in mods]

    x_lat, x_ctx = (x.reshape(T, D), D, 0, True), (ctx.reshape(Tc, D), D, 0, "c")

    def f_ffn_in(tok, ex, sh):
        return [_rms_mod(tok[0] + tok[1], sh[0], ex[0], ex[1])], [], []

    (u1,), (w1gu_g,) = _rowwise(f_ffn_in, name="ffn1_in", tok_in=[x_lat, x_ctx], ex_in=[mods[0], mods[1]], sh_in=[g_ffn1],
                                tok_out=[(D, bf16)], carry=([w1_gu[0].astype(bf16)], False), **rw_all_b)
    W1gu = cols(w1gu_g)
    (gu1, h1), (w1d_g, win_g) = _ffn_up(u1, W1gu, "ffn1_up", carry=([w1_down[0].astype(bf16), w_in[0].astype(bf16)], False))
    W1d = rows_(w1d_g)
    lr2 = 2 * LOWRANK
    segs = [(0, 2 * D, 0), (2 * D, 2 * D + QK, 6 * D), (2 * D + QK, 3 * D, 6 * D + QK), (3 * D, 4 * D, 2 * D), (4 * D, 5 * D, 3 * D),
            (5 * D, 5 * D + lr2, 7 * D), (5 * D + lr2, 6 * D + lr2, 4 * D), (6 * D + lr2, 7 * D + lr2, 5 * D)]
    wc = w_in.shape[2]
    win_parts = []
    for lo, hi, _ in sorted(segs, key=lambda t: t[2]):
        for d in range(N_DEV):
            a0, a1 = max(lo, d * wc), min(hi, (d + 1) * wc)
            if a0 < a1:
                win_parts.append(win_g[d][:, a0 - d * wc:a1 - d * wc])
    Win = jnp.concatenate(win_parts + [jnp.zeros((D, LR_PAD - lr2), bf16)], axis=1)

    def nn(a, w):
        return jnp.dot(a.astype(bf16), w, preferred_element_type=f32)

    def nt(a, w):
        return lax.dot_general(a.astype(bf16), w, (((1,), (1,)), ((), ())), preferred_element_type=f32)

    def mix_in(xv, fv, gate, sh, sc, g):
        x1 = xv + 0.5 * gate * fv
        return x1, _rms_mod(x1, g, sh, sc)

    def f_mix_in(tok, ex, sh):
        f1v = nn(tok[2], sh[1])
        return list(mix_in(tok[0] + tok[1], f1v, ex[0], ex[1], ex[2], sh[0])) + [f1v], [], []

    x1, um, f1 = _rowwise(f_mix_in, name="ffn1_down_mix_in", tok_in=[x_lat, x_ctx, (h1, F, 0, False)],
                          ex_in=[mods[2], mods[3], mods[4]], sh_in=[g_mix, W1d], tok_out=[(D, f32), (D, bf16), (D, bf16)], **rw_all2)
    p_all, (wco_g, wgo_g, wo_g, w2gu_g) = _matmul(
        um, Win, "nn", bf16, "in_proj", tm_cap=1024, tn_cap=2432,
        carry=([w_conv_out[0].astype(bf16), w_gla_out[0].astype(bf16), w_out[0].astype(bf16), w2_gu[0].astype(bf16)], False))
    Wco, Wgo, Wo, W2gu = rows_(wco_g), rows_(wgo_g), rows_(wo_g), cols(w2gu_g)

    def log_decay(lr, wa, ba):
        z = _bdot(lr, wa, (1, 0)) + ba
        return _log_sigmoid(z) / TAU

    def f_decay(tok, ex, sh):
        return [log_decay(tok[0], sh[0], sh[1])], [], []

    lr_blk = (p_all, LR_PAD, 7 * D // LR_PAD, False)
    (la_all,) = _rowwise(f_decay, name="log_decay", tok_in=[lr_blk], sh_in=[WA, BA], tok_out=[(2 * QK, f32)], **rw_all_b)

    zeros_s = jnp.zeros((B, HEADS, DV, DK), f32)
    gla_c = dict(row0=T, nb=B, seq=Lc, D=D)
    gla_x = dict(row0=0, nb=B, seq=L, D=D)
    _, hist_cf, s_f = _gla_fwd(p_all, la_all, zeros_s, rev=False, name="gla_ctx_f", **gla_c)
    _, hist_cb, s_b = _gla_fwd(p_all, la_all, zeros_s, rev=True, name="gla_ctx_b", **gla_c)
    (o_f, hist_f, _), (w2d_g,) = _gla_fwd(p_all, la_all, s_f, rev=False, name="gla_x_f", carry=([w2_down[0].astype(bf16)], False), **gla_x)
    W2d = rows_(w2d_g)
    o_b, hist_b, _ = _gla_fwd(p_all, la_all, s_b, rev=True, name="gla_x_b", **gla_x)

    cz = _conv_fwd(p_all, dww, dw_bias, B=B, L=L, D=D, name="conv_fwd")

    def ln_silu(z, g, b):
        mu = jnp.mean(z, axis=-1, keepdims=True)
        var = jnp.mean(jnp.square(z - mu), axis=-1, keepdims=True)
        return jax.nn.silu((z - mu) * lax.rsqrt(var + EPS) * g + b)

    def f_ln(tok, ex, sh):
        zc = ln_silu(tok[0].astype(f32), sh[0], sh[1])
        return [zc, nn(zc, sh[2])], [], []

    zc, yc = _rowwise(f_ln, name="conv_ln_out", tok_in=[(cz, D, 0, False)], sh_in=[conv_ln_g, conv_ln_b, Wco],
                      tok_out=[(D, bf16), (D, bf16)], **rw_x_b)

    def gla_out(of, ob, og, gn):
        return _head_rms(of.astype(f32) + ob.astype(f32), DV) * gn * jax.nn.silu(og.astype(f32))

    def f_gla_out(tok, ex, sh):
        og2 = gla_out(tok[0], tok[1], tok[2], sh[0])
        return [og2, nn(og2, sh[1])], [], []

    og_blk = (p_all, D, 3, False)
    og2, yg = _rowwise(f_gla_out, name="gla_norm_out", tok_in=[(o_f, D, 0, False), (o_b, D, 0, False), og_blk], sh_in=[gla_norm_g, Wgo],
                       tok_out=[(D, bf16), (D, bf16)], **rw_x_b)

    def merge(ga, gb, ycv, ygv):
        return jax.nn.sigmoid(ga.astype(f32)) * ycv.astype(f32) + jax.nn.sigmoid(gb.astype(f32)) * ygv.astype(f32)

    def f_merge(tok, ex, sh):
        mg = merge(*tok)
        return [mg, nn(mg, sh[0])], [], []

    ga_blk, gb_blk = (p_all, D, 4, False), (p_all, D, 5, False)
    mg, mix = _rowwise(f_merge, name="merge_mix_out", tok_in=[ga_blk, gb_blk, (yc, D, 0, False), (yg, D, 0, False)], sh_in=[Wo],
                       tok_out=[(D, bf16), (D, f32)], **rw_x_b)

    def ffn2_in(x1v, mixv, g5, sh, sc, g):
        x2 = x1v + g5 * mixv
        return x2, _rms_mod(x2, g, sh, sc)

    def f_ffn2_in(tok, ex, sh):
        return list(ffn2_in(tok[0], tok[1], ex[0], ex[1], ex[2], sh[0])), [], []

    x2, u2 = _rowwise(f_ffn2_in, name="ffn2_in", tok_in=[(x1, D, 0, False), (mix, D, 0, False)], ex_in=[mods_x[5], mods_x[6], mods_x[7]],
                      sh_in=[g_ffn2], tok_out=[(D, f32), (D, bf16)], **rw_x_b)
    gu2, h2 = _ffn_up(u2, W2gu, "ffn2_up")

    gf2 = g_final.reshape(1, D)

    def head_loss(x2v, f2v, g8, gf, tgt):
        x3 = x2v + 0.5 * g8 * f2v
        y = x3 * lax.rsqrt(jnp.mean(x3 * x3, axis=-1, keepdims=True) + EPS) * gf
        return 0.5 * jnp.sum(jnp.mean(jnp.square(y - tgt), axis=-1))

    def f_head(tok, ex, sh):
        loss, vjp = jax.vjp(lambda a, b_, c_, d_: head_loss(a, b_, c_, d_, tok[2]), tok[0], nn(tok[1], sh[1]), ex[0], sh[0])
        dx3, df2, dg8, dgf = vjp(jnp.ones((), f32))
        return [dx3, df2], [dg8], [dgf, jnp.broadcast_to(loss.reshape(1, 1), (1, 128))]

    dx3, df2, dg8, dgf, loss_p = _rowwise(
        f_head, name="ffn2_down_head", tok_in=[(x2, D, 0, False), (h2, F, 0, False), (loss_target.reshape(T, D), D, 0, False)],
        ex_in=[mods_x[8]], sh_in=[gf2, W2d], tok_out=[(D, f32), (D, bf16)], ex_out=[D], gl_out=[(1, D), (1, 128)], **rw_x2)

    dgu2 = _ffn_down_dx(df2, W2d, gu2, "ffn2_down_dx")
    gW2d = _matmul(h2, df2, "tn", f32, "ffn2_down_dw", tm_cap=1408)
    du2 = _matmul(dgu2, W2gu, "nt", bf16, "ffn2_up_dx", halves="a")
    gW2gu = _matmul(u2, dgu2, "tn", f32, "ffn2_up_dw", halves="b")

    def f_ffn2_in_bwd(tok, ex, sh):
        _, vjp = jax.vjp(ffn2_in, tok[0], tok[1], ex[0], ex[1], ex[2], sh[0])
        dx2, dmix, dg5, dsh, dsc, dg = vjp((tok[3], tok[2].astype(f32)))
        return [dx2, dmix], [dg5, dsh, dsc], [dg]

    dx2, dmix, dg5, dsh6, dsc7, dg_ffn2 = _rowwise(
        f_ffn2_in_bwd, name="ffn2_in_bwd", tok_in=[(x1, D, 0, False), (mix, D, 0, False), (du2, D, 0, False), (dx3, D, 0, False)],
        ex_in=[mods_x[5], mods_x[6], mods_x[7]], sh_in=[g_ffn2], tok_out=[(D, f32), (D, bf16)], ex_out=[D, D, D], gl_out=[(1, D)], **rw_x)

    gWo = _matmul(mg, dmix, "tn", f32, "mix_out_dw")

    def f_merge_bwd(tok, ex, sh):
        _, vjp = jax.vjp(merge, *[t.astype(f32) for t in tok[:4]])
        dga, dgb, dyc, dyg = vjp(nt(tok[4], sh[0]))
        return [dga, dgb, dyc, dyg], [], []

    dga, dgb, dyc, dyg = _rowwise(f_merge_bwd, name="mix_out_merge_bwd",
                                  tok_in=[ga_blk, gb_blk, (yc, D, 0, False), (yg, D, 0, False), (dmix, D, 0, False)], sh_in=[Wo],
                                  tok_out=[(D, bf16)] * 4, **rw_x)
    gWco = _matmul(zc, dyc, "tn", f32, "conv_out_dw")
    gWgo = _matmul(og2, dyg, "tn", f32, "gla_out_dw")

    def f_ln_bwd(tok, ex, sh):
        _, vjp = jax.vjp(ln_silu, tok[0].astype(f32), sh[0], sh[1])
        dcz, dg, db = vjp(nt(tok[1], sh[2]))
        return [dcz], [], [dg, db, jnp.sum(dcz, axis=0, keepdims=True)]

    dcz, g_ln_g, g_ln_b, g_dwb = _rowwise(f_ln_bwd, name="conv_out_ln_bwd", tok_in=[(cz, D, 0, False), (dyc, D, 0, False)],
                                          sh_in=[conv_ln_g, conv_ln_b, Wco], tok_out=[(D, bf16)], gl_out=[(1, D)] * 3, **rw_x)
    def col_shards(g):
        return jnp.transpose(g.reshape(g.shape[0], N_DEV, g.shape[1] // N_DEV), (1, 0, 2)).astype(bf16)

    def row_shards(g):
        return g.reshape(N_DEV, g.shape[0] // N_DEV, g.shape[1]).astype(bf16)

    (dca, dcb, g_dww), (r_w2d, r_w2gu, r_wo, r_wco, r_wgo) = _conv_bwd(
        p_all, dcz, dww, B=B, L=L, D=D, name="conv_bwd",
        carry=([row_shards(gW2d), col_shards(gW2gu), row_shards(gWo), row_shards(gWco), row_shards(gWgo)], True))

    def f_gla_out_bwd(tok, ex, sh):
        _, vjp = jax.vjp(gla_out, tok[0].astype(f32), tok[1].astype(f32), tok[2].astype(f32), sh[0])
        dof, _, dog, dgn = vjp(nt(tok[3], sh[1]))
        return [dof, dog], [], [dgn]

    d_o, dog, g_gn = _rowwise(f_gla_out_bwd, name="gla_out_norm_bwd",
                              tok_in=[(o_f, D, 0, False), (o_b, D, 0, False), og_blk, (dyg, D, 0, False)], sh_in=[gla_norm_g, Wgo],
                              tok_out=[(D, bf16), (D, bf16)], gl_out=[(1, D)], **rw_x)

    dq_f, dk_f, dv_f, dla_f, ds_f = _gla_bwd(p_all, la_all, hist_f, d_o, zeros_s, rev=False, name="gla_x_f_bwd", **gla_x)
    dq, dk, dv, dla_b, ds_b = _gla_bwd(p_all, la_all, hist_b, d_o, zeros_s, rev=True, name="gla_x_b_bwd", add=(dq_f, dk_f, dv_f), **gla_x)
    dq_cf, dk_cf, dv_cf, dla_cf, _ = _gla_bwd(p_all, la_all, hist_cf, None, ds_f, rev=False, name="gla_ctx_f_bwd", **gla_c)
    _, dk_c, dv_c, dla_cb, _ = _gla_bwd(p_all, la_all, hist_cb, None, ds_b, rev=True, name="gla_ctx_b_bwd", add=(dq_cf, dk_cf, dv_cf), **gla_c)

    dla_all = jnp.concatenate([jnp.concatenate([dla_f, dla_b], axis=1), jnp.concatenate([dla_cf, dla_cb], axis=1)], axis=0)

    def f_decay_bwd(tok, ex, sh):
        _, vjp = jax.vjp(log_decay, tok[0].astype(f32), sh[0].astype(f32), sh[1])
        dlr, dwa, dba = vjp(tok[1])
        return [dlr], [], [dwa, dba]

    dlr, g_WA, g_BA = _rowwise(f_decay_bwd, name="log_decay_bwd", tok_in=[lr_blk, (dla_all, 2 * QK, 0, False)], sh_in=[WA, BA],
                               tok_out=[(LR_PAD, bf16)], gl_out=[(LR_PAD, 2 * QK), (1, 2 * QK)], **rw_all)

    zc_ = functools.partial(jnp.zeros, dtype=bf16)
    dp_x = jnp.concatenate([dca, dcb, dv, dog, dga, dgb, dq, dk, dlr[:T]], axis=1)
    dp_c = jnp.concatenate([zc_((Tc, 2 * D)), dv_c, zc_((Tc, 3 * D)), zc_((Tc, QK)), dk_c, dlr[T:]], axis=1)
    dp_all = jnp.concatenate([dp_x, dp_c], axis=0)
    gWin_p = _matmul(um, dp_all, "tn", f32, "in_proj_dw", tm_cap=512, tn_cap=2432)
    gwin_shards = []
    for d in range(N_DEV):
        parts = []
        for lo, hi, po in segs:
            a0, a1 = max(lo, d * wc), min(hi, (d + 1) * wc)
            if a0 < a1:
                parts.append(gWin_p[:, po + a0 - lo:po + a1 - lo])
        gwin_shards.append(jnp.concatenate(parts, axis=1))
    dum, (r_win,) = _matmul(dp_all, Win, "nt", bf16, "in_proj_dx", tk_cap=2432, carry=([jnp.stack(gwin_shards).astype(bf16)], True))

    def f_mix_in_bwd(tok, ex, sh):
        _, vjp = jax.vjp(mix_in, tok[0] + tok[1], tok[2].astype(f32), ex[0], ex[1], ex[2], sh[0])
        dx1, df1, dgate, dsh, dsc, dg = vjp((tok[4], tok[3].astype(f32)))
        return [dx1, df1], [dgate, dsh, dsc], [dg]

    dx1, df1, dg2, dsh3, dsc4, dg_mix = _rowwise(
        f_mix_in_bwd, name="mix_in_bwd", tok_in=[x_lat, x_ctx, (f1, D, 0, False), (dum, D, 0, False), (dx2, D, 0, True)],
        ex_in=[mods[2], mods[3], mods[4]], sh_in=[g_mix], tok_out=[(D, f32), (D, bf16)], ex_out=[D, D, D], gl_out=[(1, D)], **rw_all)

    dgu1 = _ffn_down_dx(df1, W1d, gu1, "ffn1_down_dx")
    gW1d = _matmul(h1, df1, "tn", f32, "ffn1_down_dw", tm_cap=1408)
    gW1gu, (r_w1d,) = _matmul(u1, dgu1, "tn", f32, "ffn1_up_dw", carry=([row_shards(gW1d)], True), halves="b")
    du1, (r_w1gu,) = _matmul(dgu1, W1gu, "nt", bf16, "ffn1_up_dx", carry=([col_shards(gW1gu)], True), halves="a")

    def f_ffn_in_bwd(tok, ex, sh):
        _, vjp = jax.vjp(_rms_mod, tok[0] + tok[1], sh[0], ex[0], ex[1])
        dx, dg, dsh, dsc = vjp(tok[2].astype(f32))
        return [dx + tok[3]], [dsh, dsc], [dg]

    dx_lat, dsh0, dsc1, dg_ffn1 = _rowwise(
        f_ffn_in_bwd, name="ffn1_in_bwd", tok_in=[x_lat, x_ctx, (du1, D, 0, False), (dx1, D, 0, False)],
        ex_in=[mods[0], mods[1]], sh_in=[g_ffn1], tok_out=[(D, f32, "x")], ex_out=[D, D], gl_out=[(1, D)], **rw_all)
    grad_x = dx_lat.reshape(B, L, D)

    zrow = jnp.zeros((1, 1, D), f32)
    dmod_loc = jnp.concatenate([dsh0, dsc1, dg2, dsh3, dsc4] + [jnp.concatenate([t, zrow], axis=0) for t in (dg5, dsh6, dsc7, dg8)],
                               axis=2).reshape(B + 1, N_MOD * D)
    rows16 = jnp.concatenate([jnp.concatenate([loss_p, jnp.zeros((1, D - loss_p.shape[1]), f32)], axis=1), dg_ffn1, dg_mix, g_dwb, g_ln_g,
                              g_ln_b, g_BA, g_gn, dg_ffn2, dgf, jnp.zeros((6, D), f32)], axis=0)

    def to8(v):
        n_pad = -(-v.shape[1] // 1024) * 1024
        return jnp.pad(v, ((0, 0), (0, n_pad - v.shape[1]))).reshape(8, n_pad // 8)

    def from8(a, n):
        return a.reshape(1, a.size)[:, :n]

    def adam_big(nm, wv, mv, vv, part, carry=None):
        out = _adamw(wv[0], mv[0], vv[0], part, "adamw_" + nm, True, carry=carry)
        res4, carried = out if carry is not None else (out, None)
        return [t[None] for t in res4], carried

    rs_out = dict(w1_gu=r_w1gu, w1_down=r_w1d, w_in=r_win, w_conv_out=r_wco, w_gla_out=r_wgo, w_out=r_wo, w2_gu=r_w2gu, w2_down=r_w2d)
    big = {}
    big["w_in"], (dmod_g, rows_g, dww_sg, wa_sg) = adam_big(
        "w_in", w_in, m_w_in, v_w_in, rs_out["w_in"], carry=([dmod_loc, rows16, g_dww, g_WA[:2 * LOWRANK]], False))
    dmx = dmod_g[:, :B].reshape(N_DEV * B, N_MOD * D)
    dmc = dmod_g[:, B]
    gWmod, gcc_p = _mod_bwd(c_all, c_ctx2, w_mod[0], lax.dynamic_slice(dmx, (0, me * ncm), (N_DEV * B, ncm)),
                            lax.dynamic_slice(dmc, (0, me * ncm), (N_DEV, ncm)), "mod_bwd")

    big["w1_gu"], (gcc_g,) = adam_big("w1_gu", w1_gu, m_w1_gu, v_w1_gu, rs_out["w1_gu"], carry=([to8(gcc_p)], False))
    for nm, wv, mv, vv in (("w1_down", w1_down, m_w1_down, v_w1_down), ("w_conv_out", w_conv_out, m_w_conv_out, v_w_conv_out),
                           ("w_gla_out", w_gla_out, m_w_gla_out, v_w_gla_out), ("w_out", w_out, m_w_out, v_w_out),
                           ("w2_gu", w2_gu, m_w2_gu, v_w2_gu), ("w2_down", w2_down, m_w2_down, v_w2_down)):
        big[nm], _ = adam_big(nm, wv, mv, vv, rs_out[nm])
    big["w_mod"] = [t[None] for t in _adamw(w_mod[0], m_w_mod[0], v_w_mod[0], gWmod, "adamw_w_mod", False)]

    rows_s, dww_s, wa_s, g_cc, g_bmod = _sum_sources(
        [rows_g, dww_sg, wa_sg, gcc_g, jnp.concatenate([dmx, dmc], axis=0).reshape(N_DEV * (B + 1), 8, N_MOD * D // 8)], "sum_small")
    g_cc, g_bmod = from8(g_cc, D), from8(g_bmod, N_MOD * D)
    loss = rows_s[0, 0]
    ncd, nca = dw_weight.shape[2], w_alpha_f.shape[2]
    g_dww_loc = lax.dynamic_slice(dww_s, (0, me * ncd), (CONV_W, ncd))
    g_waf_loc = lax.dynamic_slice(wa_s, (0, me * nca), (LOWRANK, nca))
    g_wab_loc = lax.dynamic_slice(wa_s, (LOWRANK, QK + me * nca), (LOWRANK, nca))
    sm = {k: rows_s[i:i + 1] for i, k in enumerate(["loss", "g_ffn1", "g_mix", "dw_bias", "conv_ln_g", "conv_ln_b", "b_alpha", "gla_norm_g",
                                                     "g_ffn2", "g_final"])}

    small_params = [("c_ctx", c_ctx, m_c_ctx, v_c_ctx, g_cc), ("b_mod", b_mod, m_b_mod, v_b_mod, g_bmod),
                    ("g_ffn1", g_ffn1, m_g_ffn1, v_g_ffn1, sm["g_ffn1"]), ("g_mix", g_mix, m_g_mix, v_g_mix, sm["g_mix"]),
                    ("dw_weight", dw_weight, m_dw_weight, v_dw_weight, g_dww_loc), ("dw_bias", dw_bias, m_dw_bias, v_dw_bias, sm["dw_bias"]),
                    ("conv_ln_g", conv_ln_g, m_conv_ln_g, v_conv_ln_g, sm["conv_ln_g"]),
                    ("conv_ln_b", conv_ln_b, m_conv_ln_b, v_conv_ln_b, sm["conv_ln_b"]),
                    ("w_alpha_f", w_alpha_f, m_w_alpha_f, v_w_alpha_f, g_waf_loc), ("b_alpha_f", b_alpha_f, m_b_alpha_f, v_b_alpha_f, sm["b_alpha"][:, :QK]),
                    ("w_alpha_b", w_alpha_b, m_w_alpha_b, v_w_alpha_b, g_wab_loc), ("b_alpha_b", b_alpha_b, m_b_alpha_b, v_b_alpha_b, sm["b_alpha"][:, QK:]),
                    ("gla_norm_g", gla_norm_g, m_gla_norm_g, v_gla_norm_g, sm["gla_norm_g"]),
                    ("g_ffn2", g_ffn2, m_g_ffn2, v_g_ffn2, sm["g_ffn2"]), ("g_final", g_final, m_g_final, v_g_final, sm["g_final"])]

    def two_d(t, like):
        return t.reshape(like.shape[1:]) if like.ndim == 3 else t.reshape(like.size // 128, 128)

    small_res = _adamw_many([tuple(two_d(t, wv) for t in (wv, mv, vv, gv)) for _, wv, mv, vv, gv in small_params], "adamw_small")
    small_out = {nm: [gv.reshape(wv.shape)] + [t.reshape(wv.shape) for t in r3]
                 for (nm, wv, _, _, gv), r3 in zip(small_params, small_res)}

    order = ["c_ctx", "w_mod", "b_mod", "g_ffn1", "w1_gu", "w1_down", "g_mix", "w_in", "dw_weight", "dw_bias", "conv_ln_g", "conv_ln_b",
             "w_conv_out", "w_alpha_f", "b_alpha_f", "w_alpha_b", "b_alpha_b", "gla_norm_g", "w_gla_out", "w_out", "g_ffn2", "w2_gu",
             "w2_down", "g_final"]
    res = {**big, **small_out}
    return (loss, grad_x, *[res[n][0] for n in order], *[res[n][1] for n in order], *[res[n][2] for n in order], *[res[n][3] for n in order])
```

```python
import functools

import jax
import jax.numpy as jnp
from jax import lax
from jax.experimental import pallas as pl
from jax.experimental.pallas import tpu as pltpu

f32, bf16 = jnp.float32, jnp.bfloat16

N_DEV = 8
HEADS = 4
LOWRANK = 16
CONV_W = 31
CONV_PAD = 16
SUBLANES = 8
CHUNK = 64
SUB = 16
GLA_ROWS = 256
GLA_SAFE_DECAY = 60.0
TAU = 16.0
EPS = 1e-6
N_MOD = 9
LR_PAD = 128
ROW_TILE = 512
V7X_VMEM_BYTES = 64 << 20
VMEM_LIMIT = (V7X_VMEM_BYTES * 3) // 4

ADAM_LR, ADAM_B1, ADAM_B2, ADAM_EPS, ADAM_WD, ADAM_STEP = 0.001, 0.9, 0.999, 1e-08, 0.01, 10

MESH = pl.DeviceIdType.MESH


def _pc(body, **kw):
    return pl.pallas_call(body, **kw)


def _params(*sem):
    return pltpu.CompilerParams(dimension_semantics=sem, vmem_limit_bytes=VMEM_LIMIT)


def _pick(n, cap, unit=128):
    best = None
    for t in range(unit, min(n, cap) + 1, unit):
        if n % t == 0:
            best = t
    return best or n


def _matmul(a, b, mode, out_dtype, name, tm_cap=1024, tn_cap=1536, tk_cap=None, carry=None, halves=None):
    tk_cap = tk_cap or (2048 if mode == "tn" else 2816)
    if halves == "a":
        (_, M, Kh), N = a.shape, b.shape[0]
        K, tk = 2 * Kh, _pick(Kh, tk_cap)
        tm, tn = _pick(M, tm_cap), _pick(N, tn_cap)
        a_spec = pl.BlockSpec((None, tm, tk), lambda i, j, k: (k // (Kh // tk), i, k % (Kh // tk)))
    elif halves == "b":
        (K, M), (_, _, Nh) = a.shape, b.shape
        N, tn = 2 * Nh, _pick(Nh, tn_cap)
        tm, tk = _pick(M, tm_cap), _pick(K, tk_cap)
    else:
        if mode == "tn":
            (K, M), N = a.shape, b.shape[1]
        elif mode == "nt":
            (M, K), N = a.shape, b.shape[0]
        else:
            (M, K), N = a.shape, b.shape[1]
        tm, tn, tk = _pick(M, tm_cap), _pick(N, tn_cap), _pick(K, tk_cap)
    nk = K // tk
    if halves != "a":
        a_spec = pl.BlockSpec((tk, tm), lambda i, j, k: (k, i)) if mode == "tn" else pl.BlockSpec((tm, tk), lambda i, j, k: (i, k))
    if halves == "b":
        b_spec = pl.BlockSpec((None, tk, tn), lambda i, j, k: (j // (Nh // tn), k, j % (Nh // tn)))
    else:
        b_spec = pl.BlockSpec((tn, tk), lambda i, j, k: (j, k)) if mode == "nt" else pl.BlockSpec((tk, tn), lambda i, j, k: (k, j))
    dims = {"nn": ((1,), (0,)), "nt": ((1,), (1,)), "tn": ((0,), (0,))}[mode]

    def body_single(a_ref, b_ref, o_ref):
        o_ref[...] = lax.dot_general(a_ref[...].astype(bf16), b_ref[...].astype(bf16), (dims, ((), ())),
                                     preferred_element_type=f32).astype(out_dtype)

    def body(a_ref, b_ref, o_ref, acc_ref):
        k = pl.program_id(2)
        part = lax.dot_general(a_ref[...].astype(bf16), b_ref[...].astype(bf16), (dims, ((), ())), preferred_element_type=f32)

        @pl.when(k == 0)
        def _():
            acc_ref[...] = part

        @pl.when(k > 0)
        def _():
            acc_ref[...] += part

        @pl.when(k == nk - 1)
        def _():
            o_ref[...] = acc_ref[...].astype(out_dtype)

    (out,), carried = _call(
        body_single if nk == 1 else body, name=name, grid=(M // tm, N // tn, nk), in_specs=[a_spec, b_spec],
        out_specs=[pl.BlockSpec((tm, tn), lambda i, j, k: (i, j))], out_shape=[jax.ShapeDtypeStruct((M, N), out_dtype)],
        scratch_shapes=[] if nk == 1 else [pltpu.VMEM((tm, tn), f32)], sem=("parallel", "parallel", "arbitrary"),
        args=(a, b), carry=carry)
    return out if carry is None else (out, carried)


def _ffn_up(u, Wgu, name, carry=None):
    M, K = u.shape
    F = Wgu.shape[1] // 2
    tm, tn = _pick(M, 512), _pick(F, 1408)
    nj = F // tn

    def body(u_ref, wa_ref, wb_ref, gu_ref, h_ref):
        uv = u_ref[...]
        a = jnp.dot(uv, wa_ref[...], preferred_element_type=f32)
        b = jnp.dot(uv, wb_ref[...], preferred_element_type=f32)
        gu_ref[0] = a.astype(bf16)
        gu_ref[1] = b.astype(bf16)
        h_ref[...] = (jax.nn.silu(a) * b).astype(bf16)

    res, carried = _call(
        body, name=name, grid=(nj, M // tm),
        in_specs=[pl.BlockSpec((tm, K), lambda j, i: (i, 0)), pl.BlockSpec((K, tn), lambda j, i: (0, j)),
                  pl.BlockSpec((K, tn), lambda j, i: (0, nj + j))],
        out_specs=[pl.BlockSpec((2, tm, tn), lambda j, i: (0, i, j)), pl.BlockSpec((tm, tn), lambda j, i: (i, j))],
        out_shape=[jax.ShapeDtypeStruct((2, M, F), bf16), jax.ShapeDtypeStruct((M, F), bf16)],
        scratch_shapes=[], sem=("parallel", "parallel"), args=(u, Wgu, Wgu), carry=carry)
    return res if carry is None else (res, carried)


def _ffn_down_dx(df, Wd, gu, name):
    M, D = df.shape
    F = Wd.shape[0]
    tm, tn = _pick(M, 512), _pick(F, 1408)

    def body(df_ref, w_ref, gu_ref, o_ref):
        dh = lax.dot_general(df_ref[...], w_ref[...], (((1,), (1,)), ((), ())), preferred_element_type=f32)
        a, b = gu_ref[0].astype(f32), gu_ref[1].astype(f32)
        sg = jax.nn.sigmoid(a)
        o_ref[0] = (dh * b * sg * (1.0 + a * (1.0 - sg))).astype(bf16)
        o_ref[1] = (dh * a * sg).astype(bf16)

    return _pc(
        body, name=name, grid=(F // tn, M // tm),
        in_specs=[pl.BlockSpec((tm, D), lambda j, i: (i, 0)), pl.BlockSpec((tn, D), lambda j, i: (j, 0)),
                  pl.BlockSpec((2, tm, tn), lambda j, i: (0, i, j))],
        out_specs=pl.BlockSpec((2, tm, tn), lambda j, i: (0, i, j)), out_shape=jax.ShapeDtypeStruct((2, M, F), bf16),
        compiler_params=_params("parallel", "parallel"))(df, Wd, gu)


def _rowwise(fn, *, name, tm, n_tiles, tpe, nx_tiles, n_ex, tok_in=(), ex_in=(), sh_in=(), tok_out=(), ex_out=(), gl_out=(), carry=None):
    def seg(i):
        return jnp.minimum(i // tpe, n_ex - 1)

    in_specs, args = [], []
    for arr, w, cb, x_only in tok_in:
        if x_only == "c":
            in_specs.append(pl.BlockSpec((tm, w), functools.partial(lambda i, cb: (jnp.maximum(i - nx_tiles, 0), cb), cb=cb)))
        elif x_only:
            in_specs.append(pl.BlockSpec((tm, w), functools.partial(lambda i, cb: (jnp.minimum(i, nx_tiles - 1), cb), cb=cb)))
        else:
            in_specs.append(pl.BlockSpec((tm, w), functools.partial(lambda i, cb: (i, cb), cb=cb)))
        args.append(arr)
    for arr in ex_in:
        in_specs.append(pl.BlockSpec((1, 1, arr.shape[-1]), lambda i: (seg(i), 0, 0)))
        args.append(arr)
    for arr in sh_in:
        in_specs.append(pl.BlockSpec(arr.shape, functools.partial(lambda i, nd: (0,) * nd, nd=arr.ndim)))
        args.append(arr)
    out_specs, out_shape = [], []
    for w, dt, *x_rows in tok_out:
        if x_rows:
            out_specs.append(pl.BlockSpec((tm, w), lambda i: (jnp.minimum(i, nx_tiles - 1), 0)))
        else:
            out_specs.append(pl.BlockSpec((tm, w), lambda i: (i, 0)))
        out_shape.append(jax.ShapeDtypeStruct(((nx_tiles if x_rows else n_tiles) * tm, w), dt))
    for w in ex_out:
        out_specs.append(pl.BlockSpec((1, 1, w), lambda i: (seg(i), 0, 0)))
        out_shape.append(jax.ShapeDtypeStruct((n_ex, 1, w), f32))
    for r, w in gl_out:
        out_specs.append(pl.BlockSpec((r, w), lambda i: (0, 0)))
        out_shape.append(jax.ShapeDtypeStruct((r, w), f32))
    n_tok, n_exi, n_sh = len(tok_in), len(ex_in), len(sh_in)
    n_to, n_eo = len(tok_out), len(ex_out)
    x_only_flags = [t[3] for t in tok_in]
    x_rows_flags = [len(t) > 2 for t in tok_out]

    def body(*refs):
        i = pl.program_id(0)
        ins, outs = refs[: n_tok + n_exi + n_sh], refs[n_tok + n_exi + n_sh:]
        is_x = i < nx_tiles
        tok_vals = []
        for r, xo in zip(ins[:n_tok], x_only_flags):
            v = r[...]
            if xo == "c":
                v = jnp.where(is_x, jnp.zeros_like(v), v)
            elif xo:
                v = jnp.where(is_x, v, jnp.zeros_like(v))
            tok_vals.append(v)
        ex_vals = [r[0] for r in ins[n_tok:n_tok + n_exi]]
        sh_vals = [r[...] for r in ins[n_tok + n_exi:]]
        t_o, e_o, g_o = fn(tok_vals, ex_vals, sh_vals)
        for r, v, xr in zip(outs[:n_to], t_o, x_rows_flags):
            if xr:
                @pl.when(is_x)
                def _(r=r, v=v):
                    r[...] = v.astype(r.dtype)
            else:
                r[...] = v.astype(r.dtype)
        first = jnp.logical_and(i % tpe == 0, i <= nx_tiles)
        for r, v in zip(outs[n_to:n_to + n_eo], e_o):
            @pl.when(first)
            def _(r=r, v=v):
                r[0] = v

            @pl.when(jnp.logical_not(first))
            def _(r=r, v=v):
                r[0] += v
        for r, v in zip(outs[n_to + n_eo:], g_o):
            @pl.when(i == 0)
            def _(r=r, v=v):
                r[...] = v

            @pl.when(i > 0)
            def _(r=r, v=v):
                r[...] += v

    res, carried = _call(body, name=name, grid=(n_tiles,), in_specs=in_specs, out_specs=out_specs, out_shape=out_shape,
                         scratch_shapes=[], sem=("arbitrary",), args=args, carry=carry)
    return res if carry is None else (res, carried)


def _rms_mod(x, g, sh, sc):
    y = x * lax.rsqrt(jnp.mean(x * x, axis=-1, keepdims=True) + EPS) * g
    return y * (1.0 + sc) + sh


def _log_sigmoid(z):
    return jnp.minimum(z, 0.0) - jnp.log(1.0 + jnp.exp(-jnp.abs(z)))


def _head_rms(o, DV):
    parts = []
    for h in range(HEADS):
        oh = o[:, h * DV:(h + 1) * DV]
        parts.append(oh * lax.rsqrt(jnp.mean(oh * oh, axis=-1, keepdims=True) + EPS))
    return jnp.concatenate(parts, axis=1)


@functools.partial(jax.custom_vjp, nondiff_argnums=(2,))
def _bdot(a, b, dims):
    return lax.dot_general(a.astype(bf16), b.astype(bf16), (((dims[0],), (dims[1],)), ((), ())), preferred_element_type=f32)


def _bdot_fwd(a, b, dims):
    return _bdot(a, b, dims), (a, b)


def _bdot_bwd(dims, res, g):
    a, b = res
    ca, cb = dims
    da = _bdot(g, b, (1, 1 - cb)) if ca == 1 else _bdot(b, g, (1 - cb, 1))
    db = _bdot(a, g, (1 - ca, 0)) if cb == 0 else _bdot(g, a, (0, 1 - ca))
    return da, db


_bdot.defvjp(_bdot_fwd, _bdot_bwd)


def _split_dot(m, x, dims):
    mb, rem, acc = m.astype(bf16), x, None
    for _ in range(3):
        piece = rem.astype(bf16)
        rem = rem - piece.astype(f32)
        part = lax.dot_general(mb, piece, (((dims[0],), (dims[1],)), ((), ())), preferred_element_type=f32)
        acc = part if acc is None else acc + part
    return acc


@jax.custom_vjp
def _tri_cumsum(tri, g):
    return _split_dot(tri, g, (1, 0))


def _tri_cumsum_fwd(tri, g):
    return _tri_cumsum(tri, g), tri


def _tri_cumsum_bwd(tri, db):
    return jnp.zeros_like(tri), _split_dot(tri, db, (0, 0))


_tri_cumsum.defvjp(_tri_cumsum_fwd, _tri_cumsum_bwd)


def _gla_chunk(St, q, k, v, g, *, rev, scale, exact):
    C, DK = q.shape
    r = lax.broadcasted_iota(jnp.int32, (C, C), 0)
    c = lax.broadcasted_iota(jnp.int32, (C, C), 1)
    causal = (r <= c) if rev else (r >= c)
    b = _tri_cumsum(causal.astype(f32), g)
    qs = q * scale
    qe = qs * jnp.exp(b)
    inter = _bdot(qe, St, (1, 1))
    b_last = b[0:1] if rev else b[C - 1:C]
    kd = k * jnp.exp(b_last - b)
    St_new = St * jnp.exp(b_last) + _bdot(v, kd, (0, 0))
    if not exact:
        att = jnp.where(causal, _bdot(qe, k * jnp.exp(-b), (1, 1)), 0.0)
        return St_new, inter + _bdot(att, v, (1, 0))
    rr = lax.broadcasted_iota(jnp.int32, (SUB, SUB, DK), 0)
    cc = lax.broadcasted_iota(jnp.int32, (SUB, SUB, DK), 1)
    m3 = (rr <= cc) if rev else (rr >= cc)
    outs = []
    for i in range(C // SUB):
        lo, hi = i * SUB, (i + 1) * SUB
        bi, qi, ki, vi = b[lo:hi], qs[lo:hi], k[lo:hi], v[lo:hi]
        rel = bi[:, None, :] - bi[None, :, :]
        e = jnp.where(m3, jnp.exp(jnp.where(m3, rel, 0.0)), 0.0)
        att = jnp.sum(qi[:, None, :] * e * ki[None, :, :], axis=-1)
        acc = _bdot(att, vi, (1, 0))
        ref_row = b[hi - 1:hi] if rev else b[lo:lo + 1]
        prev = slice(hi, C) if rev else slice(0, lo)
        if (hi < C) if rev else (lo > 0):
            qn = qi * jnp.exp(bi - ref_row)
            ks = k[prev] * jnp.exp(ref_row - b[prev])
            acc = acc + _bdot(_bdot(qn, ks, (1, 1)), v[prev], (1, 0))
        outs.append(acc)
    return St_new, inter + jnp.concatenate(outs, axis=0)


def _mild_decay(la_ref):
    return jnp.min(la_ref[...]) >= -GLA_SAFE_DECAY / CHUNK


def _gla_specs(D, rev_blocks, row0, seq):
    DK, DV = D // (2 * HEADS), D // HEADS
    nblk = seq // GLA_ROWS
    rb0 = row0 // GLA_ROWS

    def blk(j):
        return (nblk - 1 - j) if rev_blocks else j

    return DK, DV, nblk, rb0, blk


def _gla_in_specs(D, rev, rows):
    QK = D // 2
    return [
        pl.BlockSpec((GLA_ROWS, QK), lambda b, j: (rows(b, j), 6 * D // QK)),
        pl.BlockSpec((GLA_ROWS, QK), lambda b, j: (rows(b, j), 6 * D // QK + 1)),
        pl.BlockSpec((GLA_ROWS, D), lambda b, j: (rows(b, j), 2)),
        pl.BlockSpec((GLA_ROWS, QK), lambda b, j: (rows(b, j), 1 if rev else 0)),
    ]


def _gla_fwd(p_all, la_all, s0, *, rev, row0, nb, seq, D, name, carry=None):
    DK, DV, nblk, rb0, blk = _gla_specs(D, rev, row0, seq)
    cpb = GLA_ROWS // CHUNK

    def rows(b, j):
        return rb0 + b * nblk + blk(j)

    in_specs = _gla_in_specs(D, rev, rows) + [pl.BlockSpec((1, HEADS, DV, DK), lambda b, j: (b, 0, 0, 0))]
    out_specs = [
        pl.BlockSpec((GLA_ROWS, D), lambda b, j: (b * nblk + blk(j), 0)),
        pl.BlockSpec((1, HEADS, cpb, DV, DK), lambda b, j: (b, 0, blk(j), 0, 0)),
        pl.BlockSpec((1, HEADS, DV, DK), lambda b, j: (b, 0, 0, 0)),
    ]
    out_shape = [
        jax.ShapeDtypeStruct((nb * seq, D), bf16),
        jax.ShapeDtypeStruct((nb, HEADS, seq // CHUNK, DV, DK), bf16),
        jax.ShapeDtypeStruct((nb, HEADS, DV, DK), f32),
    ]
    chunk = functools.partial(_gla_chunk, rev=rev, scale=DK ** -0.5)

    def body(q_ref, k_ref, v_ref, la_ref, s0_ref, o_ref, hist_ref, sfin_ref, st_ref):
        j = pl.program_id(1)

        @pl.when(j == 0)
        def _():
            st_ref[...] = s0_ref[0]

        def step(ci, exact):
            cc = (cpb - 1 - ci) if rev else ci
            sl = pl.ds(cc * CHUNK, CHUNK)
            for h in range(HEADS):
                kq, kv = pl.ds(h * DK, DK), pl.ds(h * DV, DV)
                St = st_ref[h]
                hist_ref[0, h, cc] = St.astype(bf16)
                St2, o = chunk(St, q_ref[sl, kq].astype(f32), k_ref[sl, kq].astype(f32), v_ref[sl, kv].astype(f32), la_ref[sl, kq],
                               exact=exact)
                o_ref[sl, kv] = o.astype(bf16)
                st_ref[h] = St2

        mild = _mild_decay(la_ref)
        for exact in (False, True):
            @pl.when(jnp.logical_not(mild) if exact else mild)
            def _(exact=exact):
                for ci in range(cpb):
                    step(ci, exact)

        @pl.when(j == nblk - 1)
        def _():
            sfin_ref[0] = st_ref[...]

    res, carried = _call(body, name=name, grid=(nb, nblk), in_specs=in_specs, out_specs=out_specs, out_shape=out_shape,
                         scratch_shapes=[pltpu.VMEM((HEADS, DV, DK), f32)], sem=("parallel", "arbitrary"),
                         args=(p_all, p_all, p_all, la_all, s0), carry=carry)
    return res if carry is None else (res, carried)


def _gla_bwd(p_all, la_all, hist, do, dsfin, *, rev, row0, nb, seq, D, name, add=None):
    DK, DV, nblk, rb0, blk = _gla_specs(D, not rev, row0, seq)
    cpb = GLA_ROWS // CHUNK
    QK = HEADS * DK
    has_do = do is not None

    def rows(b, j):
        return rb0 + b * nblk + blk(j)

    in_specs = _gla_in_specs(D, rev, rows) + [
        pl.BlockSpec((1, HEADS, cpb, DV, DK), lambda b, j: (b, 0, blk(j), 0, 0)),
        pl.BlockSpec((1, HEADS, DV, DK), lambda b, j: (b, 0, 0, 0)),
    ]
    args = [p_all, p_all, p_all, la_all, hist, dsfin]
    if has_do:
        in_specs.append(pl.BlockSpec((GLA_ROWS, D), lambda b, j: (b * nblk + blk(j), 0)))
        args.append(do)
    if add is not None:
        in_specs += [pl.BlockSpec((GLA_ROWS, t.shape[1]), lambda b, j: (b * nblk + blk(j), 0)) for t in add]
        args += list(add)
    gdt = f32 if add is None else bf16
    out_specs = [
        pl.BlockSpec((GLA_ROWS, QK), lambda b, j: (b * nblk + blk(j), 0)),
        pl.BlockSpec((GLA_ROWS, QK), lambda b, j: (b * nblk + blk(j), 0)),
        pl.BlockSpec((GLA_ROWS, D), lambda b, j: (b * nblk + blk(j), 0)),
        pl.BlockSpec((GLA_ROWS, QK), lambda b, j: (b * nblk + blk(j), 0)),
        pl.BlockSpec((1, HEADS, DV, DK), lambda b, j: (b, 0, 0, 0)),
    ]
    out_shape = [
        jax.ShapeDtypeStruct((nb * seq, QK), gdt), jax.ShapeDtypeStruct((nb * seq, QK), gdt),
        jax.ShapeDtypeStruct((nb * seq, D), gdt), jax.ShapeDtypeStruct((nb * seq, QK), f32),
        jax.ShapeDtypeStruct((nb, HEADS, DV, DK), f32),
    ]
    chunk = functools.partial(_gla_chunk, rev=rev, scale=DK ** -0.5)

    def body(*refs):
        refs = list(refs)
        q_ref, k_ref, v_ref, la_ref, hist_ref, dsfin_ref = refs[:6]
        do_ref = refs[6] if has_do else None
        add_refs = refs[6 + has_do:len(refs) - 6]
        dq_ref, dk_ref, dv_ref, dla_ref, ds0_ref, ds_ref = refs[len(refs) - 6:]
        j = pl.program_id(1)

        @pl.when(j == 0)
        def _():
            ds_ref[...] = dsfin_ref[0]

        def step(ci, exact):
            cc = ci if rev else (cpb - 1 - ci)
            sl = pl.ds(cc * CHUNK, CHUNK)
            for h in range(HEADS):
                kq, kv = pl.ds(h * DK, DK), pl.ds(h * DV, DV)
                prim = (hist_ref[0, h, cc].astype(f32), q_ref[sl, kq].astype(f32), k_ref[sl, kq].astype(f32), v_ref[sl, kv].astype(f32), la_ref[sl, kq])
                _, vjp = jax.vjp(functools.partial(chunk, exact=exact), *prim)
                d_o = do_ref[sl, kv].astype(f32) if has_do else jnp.zeros((CHUNK, DV), f32)
                dSt, dq, dk, dv, dg = vjp((ds_ref[h], d_o))
                if add is not None:
                    dq, dk, dv = dq + add_refs[0][sl, kq], dk + add_refs[1][sl, kq], dv + add_refs[2][sl, kv]
                dq_ref[sl, kq] = dq.astype(gdt)
                dk_ref[sl, kq] = dk.astype(gdt)
                dv_ref[sl, kv] = dv.astype(gdt)
                dla_ref[sl, kq] = dg
                ds_ref[h] = dSt

        mild = _mild_decay(la_ref)
        for exact in (False, True):
            @pl.when(jnp.logical_not(mild) if exact else mild)
            def _(exact=exact):
                for ci in range(cpb):
                    step(ci, exact)

        @pl.when(j == nblk - 1)
        def _():
            ds0_ref[0] = ds_ref[...]

    return _pc(body, name=name, grid=(nb, nblk), in_specs=in_specs, out_specs=out_specs, out_shape=out_shape,
               scratch_shapes=[pltpu.VMEM((HEADS, DV, DK), f32)], compiler_params=_params("parallel", "arbitrary"))(*args)


def _conv_fwd(p_all, dw_w, dw_b, *, B, L, D, name):
    ct = _pick(D, 256)
    nj = D // ct
    st = _pick(L, 128, 8)
    off = CONV_PAD - CONV_W // 2

    def body(a_ref, b_ref, w_ref, bias_ref, o_ref, zs_ref):
        _fill_shifted(zs_ref, L, lambda t0, n: a_ref[pl.ds(t0, n), :].astype(f32) * jax.nn.sigmoid(b_ref[pl.ds(t0, n), :].astype(f32)))
        for t0 in range(0, L, st):
            acc = jnp.zeros((st, ct), f32) + bias_ref[...]
            for k in range(CONV_W):
                acc = acc + w_ref[pl.ds(k, 1), :] * _window(zs_ref, t0 + k + off, st)
            o_ref[pl.ds(t0, st), :] = acc.astype(bf16)

    return _pc(
        body, name=name, grid=(B, nj),
        in_specs=[pl.BlockSpec((L, ct), lambda b, j: (b, j)), pl.BlockSpec((L, ct), lambda b, j: (b, nj + j)),
                  pl.BlockSpec((CONV_W, ct), lambda b, j: (0, j)), pl.BlockSpec((1, ct), lambda b, j: (0, j))],
        out_specs=pl.BlockSpec((L, ct), lambda b, j: (b, j)), out_shape=jax.ShapeDtypeStruct((B * L, D), bf16),
        scratch_shapes=[pltpu.VMEM((SUBLANES, L + 2 * CONV_PAD, ct), f32)], compiler_params=_params("parallel", "parallel"),
    )(p_all, p_all, dw_w, dw_b)


def _fill_shifted(zs_ref, L, rows):
    lp = L + 2 * CONV_PAD
    ct = zs_ref.shape[2]
    step = 256
    zs_ref[0, pl.ds(0, CONV_PAD), :] = jnp.zeros((CONV_PAD, ct), f32)
    zs_ref[0, pl.ds(CONV_PAD + L, CONV_PAD), :] = jnp.zeros((CONV_PAD, ct), f32)
    for t0 in range(0, L, step):
        n = min(step, L - t0)
        zs_ref[0, pl.ds(CONV_PAD + t0, n), :] = rows(t0, n)
    for r in range(1, SUBLANES):
        for i0 in range(0, lp - SUBLANES, step):
            n = min(step, lp - SUBLANES - i0)
            zs_ref[r, pl.ds(i0, n), :] = zs_ref[0, pl.ds(i0 + r, n), :]


def _window(zs_ref, start, n):
    r = start % SUBLANES
    return zs_ref[r, pl.ds(start - r, n), :]


def _conv_bwd(p_all, dcz, dw_w, *, B, L, D, name, carry=None):
    ct = _pick(D, 128)
    nj = D // ct
    st = _pick(L, 256, 8)
    half = CONV_W // 2

    def body(a_ref, b_ref, dcz_ref, w_ref, da_ref, db_ref, ddw_ref, zs_ref, ds_ref):
        bi = pl.program_id(1)
        _fill_shifted(zs_ref, L, lambda t0, n: a_ref[pl.ds(t0, n), :].astype(f32) * jax.nn.sigmoid(b_ref[pl.ds(t0, n), :].astype(f32)))
        _fill_shifted(ds_ref, L, lambda t0, n: dcz_ref[pl.ds(t0, n), :].astype(f32))

        @pl.when(bi == 0)
        def _():
            ddw_ref[...] = jnp.zeros_like(ddw_ref)

        for t0 in range(0, L, st):
            acc = jnp.zeros((st, ct), f32)
            for k in range(CONV_W):
                acc = acc + w_ref[pl.ds(k, 1), :] * _window(ds_ref, t0 + CONV_PAD + half - k, st)
            a_t = a_ref[pl.ds(t0, st), :].astype(f32)
            sg_t = jax.nn.sigmoid(b_ref[pl.ds(t0, st), :].astype(f32))
            da_ref[pl.ds(t0, st), :] = (acc * sg_t).astype(bf16)
            db_ref[pl.ds(t0, st), :] = (acc * a_t * sg_t * (1.0 - sg_t)).astype(bf16)

        parts = [jnp.zeros((SUBLANES, ct), f32) for _ in range(CONV_W)]
        sw = _pick(L, 64, SUBLANES)
        for t0 in range(0, L, sw):
            dout = dcz_ref[pl.ds(t0, sw), :].astype(f32)
            for k in range(CONV_W):
                prod = dout * _window(zs_ref, t0 + k + CONV_PAD - half, sw)
                for i in range(0, sw, SUBLANES):
                    parts[k] = parts[k] + prod[i:i + SUBLANES]
        for k in range(CONV_W):
            ddw_ref[pl.ds(k, 1), :] += jnp.sum(parts[k], axis=0, keepdims=True)

    res, carried = _call(
        body, name=name, grid=(nj, B),
        in_specs=[pl.BlockSpec((L, ct), lambda j, b: (b, j)), pl.BlockSpec((L, ct), lambda j, b: (b, nj + j)),
                  pl.BlockSpec((L, ct), lambda j, b: (b, j)), pl.BlockSpec((CONV_W, ct), lambda j, b: (0, j))],
        out_specs=[pl.BlockSpec((L, ct), lambda j, b: (b, j)), pl.BlockSpec((L, ct), lambda j, b: (b, j)),
                   pl.BlockSpec((2 * CONV_PAD, ct), lambda j, b: (0, j))],
        out_shape=[jax.ShapeDtypeStruct((B * L, D), bf16), jax.ShapeDtypeStruct((B * L, D), bf16),
                   jax.ShapeDtypeStruct((2 * CONV_PAD, D), f32)],
        scratch_shapes=[pltpu.VMEM((SUBLANES, L + 2 * CONV_PAD, ct), f32), pltpu.VMEM((SUBLANES, L + 2 * CONV_PAD, ct), f32)],
        sem=("parallel", "arbitrary"), args=(p_all, p_all, dcz, dw_w), carry=carry)
    return res if carry is None else (res, carried)


def _exchange(arrs, scatter, name):
    ex = _Exchange(arrs, scatter)
    n = ex.n

    def body(*refs):
        ex.start(refs[:n], refs[n:2 * n], refs[2 * n:])
        ex.finish(refs[:n], refs[n:2 * n], refs[2 * n:])

    res = _pc(body, name=name, in_specs=ex.specs, out_specs=ex.specs, out_shape=ex.out_shape, scratch_shapes=ex.scratch)(*arrs)
    return list(res)


class _Exchange:
    def __init__(self, arrs, scatter):
        self.arrs, self.scatter, self.n = list(arrs), scatter, len(arrs)
        self.out_shape = [jax.ShapeDtypeStruct(((N_DEV,) + a.shape[1:]) if scatter else ((N_DEV,) + a.shape), a.dtype) for a in arrs]
        self.specs = [pl.BlockSpec(memory_space=pl.ANY)] * self.n
        self.scratch = [pltpu.SemaphoreType.DMA((self.n, N_DEV - 1)), pltpu.SemaphoreType.DMA((self.n, N_DEV - 1)),
                        pltpu.SemaphoreType.DMA((self.n,))]

    def _copies(self, ins, outs, sems, landing):
        send_sems, recv_sems, local_sems = sems
        me = 4 * lax.axis_index("x") + 2 * lax.axis_index("y") + lax.axis_index("c")
        if landing:
            local = []
        else:
            local = [pltpu.make_async_copy(ins[a].at[me] if self.scatter else ins[a], outs[a].at[me], local_sems.at[a]) for a in range(self.n)]
        remote = []
        for k in range(1, N_DEV):
            p = (me + (N_DEV - k if landing else k)) % N_DEV
            for a in range(self.n):
                remote.append(pltpu.make_async_remote_copy(
                    src_ref=ins[a].at[p] if self.scatter else ins[a], dst_ref=outs[a].at[p if landing else me],
                    send_sem=send_sems.at[a, k - 1], recv_sem=recv_sems.at[a, k - 1],
                    device_id=(p // 4, (p // 2) % 2, p % 2), device_id_type=MESH))
        return local, remote

    def _gather_plan(self, ins, outs, sems):
        send_sems, recv_sems, local_sems = sems
        x, y, c = lax.axis_index("x"), lax.axis_index("y"), lax.axis_index("c")
        chips = [(1 - x, y), (x, 1 - y), (1 - x, 1 - y)]

        def blk(px, py, pc):
            return 4 * px + 2 * py + pc

        def copy(a, k, block, to, own):
            return pltpu.make_async_remote_copy(
                src_ref=ins[a] if own else outs[a].at[block], dst_ref=outs[a].at[block],
                send_sem=send_sems.at[a, k], recv_sem=recv_sems.at[a, k], device_id=to, device_id_type=MESH)

        me = blk(x, y, c)
        local = [pltpu.make_async_copy(ins[a], outs[a].at[me], local_sems.at[a]) for a in range(self.n)]
        return local, copy, me, (x, y, 1 - c), chips, blk, c

    def start(self, ins, outs, sems):
        if self.scatter:
            local, sends = self._copies(ins, outs, sems, False)
            for cp in local + sends:
                cp.start()
            return
        local, copy, me, sibling, chips, _, c = self._gather_plan(ins, outs, sems)
        for cp in local:
            cp.start()
        for a in range(self.n):
            copy(a, 0, me, sibling, True).start()
            for j, chip in enumerate(chips):
                copy(a, 1 + j, me, (*chip, c), True).start()

    def finish(self, ins, outs, sems):
        if self.scatter:
            for cp in self._copies(ins, outs, sems, True)[1]:
                cp.wait_recv()
            local, sends = self._copies(ins, outs, sems, False)
            for cp in sends:
                cp.wait_send()
            for cp in local:
                cp.wait()
            return
        local, copy, me, sibling, chips, blk, c = self._gather_plan(ins, outs, sems)
        for j, chip in enumerate(chips):
            for a in range(self.n):
                copy(a, 1 + j, blk(*chip, c), sibling, True).wait_recv()
                copy(a, 4 + j, blk(*chip, c), sibling, False).start()
        for a in range(self.n):
            copy(a, 0, blk(*sibling), sibling, True).wait_recv()
            for j, chip in enumerate(chips):
                copy(a, 4 + j, blk(*chip, 1 - c), sibling, False).wait_recv()
        for a in range(self.n):
            copy(a, 0, me, sibling, True).wait_send()
            for j, chip in enumerate(chips):
                copy(a, 1 + j, me, (*chip, c), True).wait_send()
                copy(a, 4 + j, blk(*chip, c), sibling, False).wait_send()
        for cp in local:
            cp.wait()


def _carried(inner, n_in, n_out, grid, ex):
    n = ex.n

    def body(*refs):
        own_in, c_in = refs[:n_in], refs[n_in:n_in + n]
        own_out, c_out = refs[n_in + n:n_in + n + n_out], refs[n_in + n + n_out:n_in + 2 * n + n_out]
        rest = refs[n_in + 2 * n + n_out:]
        own_scr, sems = rest[:len(rest) - 3], rest[len(rest) - 3:]
        pids = [pl.program_id(d) for d in range(len(grid))]
        first = functools.reduce(jnp.logical_and, [p == 0 for p in pids])
        last = functools.reduce(jnp.logical_and, [p == g - 1 for p, g in zip(pids, grid)])

        @pl.when(first)
        def _():
            ex.start(c_in, c_out, sems)

        inner(*own_in, *own_out, *own_scr)

        @pl.when(last)
        def _():
            ex.finish(c_in, c_out, sems)

    return body


def _call(inner, *, name, grid, in_specs, out_specs, out_shape, scratch_shapes, sem, args, carry=None):
    if carry is None:
        res = _pc(inner, name=name, grid=grid, in_specs=in_specs, out_specs=out_specs, out_shape=out_shape,
                  scratch_shapes=scratch_shapes, compiler_params=_params(*sem))(*args)
        return list(res), None
    ex = _Exchange(*carry)
    res = _pc(_carried(inner, len(in_specs), len(out_specs), grid, ex), name=name, grid=grid,
              in_specs=list(in_specs) + ex.specs, out_specs=list(out_specs) + ex.specs, out_shape=list(out_shape) + ex.out_shape,
              scratch_shapes=list(scratch_shapes) + ex.scratch, compiler_params=_params(*(["arbitrary"] * len(grid))))(*args, *ex.arrs)
    res = list(res)
    return res[:len(out_specs)], res[len(out_specs):]


def _mod_fwd(c_all, c_ctx, w_loc, b_loc, name):
    nr, D = c_all.shape
    nc = w_loc.shape[1]

    def body(c_ref, cc_ref, w_ref, b_ref, o_ref):
        a = jnp.concatenate([c_ref[...], jnp.broadcast_to(cc_ref[...], (8, D))], axis=0)
        s = jax.nn.silu(a).astype(bf16)
        o_ref[...] = jnp.dot(s, w_ref[...].astype(bf16), preferred_element_type=f32) + b_ref[...]

    return _pc(body, name=name, out_shape=jax.ShapeDtypeStruct((nr + 8, nc), f32), compiler_params=_params())(c_all, c_ctx, w_loc, b_loc)


def _mod_bwd(c_all, c_ctx, w_loc, dmx_loc, dmc_loc, name):
    nr, D = c_all.shape
    nc = w_loc.shape[1]

    def body(c_ref, cc_ref, w_ref, dmx_ref, dmc_ref, gw_ref, gc_ref):
        cc = cc_ref[...]
        a = jnp.concatenate([c_ref[...], jnp.broadcast_to(cc, (N_DEV, D))], axis=0)
        s = jax.nn.silu(a).astype(bf16)
        g = jnp.concatenate([dmx_ref[...], dmc_ref[...]], axis=0).astype(bf16)
        gw_ref[...] = lax.dot_general(s, g, (((0,), (0,)), ((), ())), preferred_element_type=f32)
        dmc = jnp.sum(dmc_ref[...], axis=0, keepdims=True)
        ds = lax.dot_general(jnp.broadcast_to(dmc, (8, nc)).astype(bf16), w_ref[...].astype(bf16), (((1,), (1,)), ((), ())),
                             preferred_element_type=f32)[0:1]
        sg = jax.nn.sigmoid(cc)
        gc_ref[...] = ds * (sg * (1.0 + cc * (1.0 - sg)))

    return _pc(body, name=name, out_shape=[jax.ShapeDtypeStruct((D, nc), f32), jax.ShapeDtypeStruct((1, D), f32)],
               compiler_params=_params())(c_all, c_ctx, w_loc, dmx_loc, dmc_loc)


def _adamw_math(w, g, m, v):
    m2 = ADAM_B1 * m + (1.0 - ADAM_B1) * g
    v2 = ADAM_B2 * v + (1.0 - ADAM_B2) * jnp.square(g)
    m_hat = m2 / (1.0 - ADAM_B1 ** ADAM_STEP)
    v_hat = v2 / (1.0 - ADAM_B2 ** ADAM_STEP)
    delta = -ADAM_LR * (m_hat / (jnp.sqrt(v_hat) + ADAM_EPS) + ADAM_WD * w)
    return delta, m2, v2


def _adamw_many(params, name):
    n = len(params)

    def body(*refs):
        ins, outs = refs[:4 * n], refs[4 * n:]
        for i in range(n):
            w, m, v, g = (ins[4 * i + k][...] for k in range(4))
            d, m2, v2 = _adamw_math(w, g, m, v)
            outs[3 * i][...] = d
            outs[3 * i + 1][...] = m2
            outs[3 * i + 2][...] = v2

    res = _pc(body, name=name, out_shape=[jax.ShapeDtypeStruct(p[0].shape, f32) for p in params for _ in range(3)],
              compiler_params=_params())(*[a for p in params for a in p])
    return [tuple(res[3 * i:3 * i + 3]) for i in range(n)]


def _adamw(w, m, v, g, name, partials, carry=None):
    r, cdim = w.shape
    tr = _pick(r, 256, 8)

    def body(w_ref, m_ref, v_ref, g_ref, og_ref, od_ref, om_ref, ov_ref):
        if partials:
            g = g_ref[0].astype(f32)
            for s in range(1, N_DEV):
                g = g + g_ref[s].astype(f32)
        else:
            g = g_ref[...]
        d, m2, v2 = _adamw_math(w_ref[...], g, m_ref[...], v_ref[...])
        og_ref[...] = g
        od_ref[...] = d
        om_ref[...] = m2
        ov_ref[...] = v2

    blk = pl.BlockSpec((tr, cdim), lambda i: (i, 0))
    g_spec = pl.BlockSpec((N_DEV, tr, cdim), lambda i: (0, i, 0)) if partials else blk
    res, carried = _call(body, name=name, grid=(r // tr,), in_specs=[blk, blk, blk, g_spec], out_specs=[blk] * 4,
                         out_shape=[jax.ShapeDtypeStruct((r, cdim), f32)] * 4, scratch_shapes=[], sem=("parallel",),
                         args=(w, m, v, g), carry=carry)
    return res if carry is None else (res, carried)


def _sum_sources(parts, name):
    def body(*refs):
        for i_ref, o_ref in zip(refs[:len(parts)], refs[len(parts):]):
            acc = i_ref[0]
            for s in range(1, i_ref.shape[0]):
                acc = acc + i_ref[s]
            o_ref[...] = acc

    return list(_pc(body, name=name, out_shape=[jax.ShapeDtypeStruct(p.shape[1:], f32) for p in parts],
                    compiler_params=_params())(*parts))


def kernel(x, c, ctx, c_ctx, w_mod, b_mod, g_ffn1, w1_gu, w1_down, g_mix, w_in, dw_weight, dw_bias, conv_ln_g, conv_ln_b, w_conv_out, w_alpha_f, b_alpha_f, w_alpha_b, b_alpha_b, gla_norm_g, w_gla_out, w_out, g_ffn2, w2_gu, w2_down, g_final, loss_target, m_c_ctx, m_w_mod, m_b_mod, m_g_ffn1, m_w1_gu, m_w1_down, m_g_mix, m_w_in, m_dw_weight, m_dw_bias, m_conv_ln_g, m_conv_ln_b, m_w_conv_out, m_w_alpha_f, m_b_alpha_f, m_w_alpha_b, m_b_alpha_b, m_gla_norm_g, m_w_gla_out, m_w_out, m_g_ffn2, m_w2_gu, m_w2_down, m_g_final, v_c_ctx, v_w_mod, v_b_mod, v_g_ffn1, v_w1_gu, v_w1_down, v_g_mix, v_w_in, v_dw_weight, v_dw_bias, v_conv_ln_g, v_conv_ln_b, v_w_conv_out, v_w_alpha_f, v_b_alpha_f, v_w_alpha_b, v_b_alpha_b, v_gla_norm_g, v_w_gla_out, v_w_out, v_g_ffn2, v_w2_gu, v_w2_down, v_g_final):
    B, L, D = x.shape
    Lc = ctx.shape[1]
    T, Tc = B * L, B * Lc
    Tall = T + Tc
    F = w1_down.shape[1] * N_DEV
    DK, DV = D // (2 * HEADS), D // HEADS
    QK = HEADS * DK
    PW = 7 * D + LR_PAD
    tm = ROW_TILE
    tpe = L // tm
    nx, nall = T // tm, Tall // tm
    me = 4 * lax.axis_index("x") + 2 * lax.axis_index("y") + lax.axis_index("c")

    rw_all = dict(tm=tm, n_tiles=nall, tpe=tpe, nx_tiles=nx, n_ex=B + 1)
    rw_x = dict(tm=tm, n_tiles=nx, tpe=tpe, nx_tiles=nx, n_ex=B)
    rw_all2, rw_x2 = rw_all, rw_x
    mb = 2 if (L % (2 * tm) == 0 and Tc % (2 * tm) == 0) else 1
    rw_all_b = dict(tm=mb * tm, n_tiles=nall // mb, tpe=tpe // mb, nx_tiles=nx // mb, n_ex=B + 1)
    rw_x_b = dict(tm=mb * tm, n_tiles=nx // mb, tpe=tpe // mb, nx_tiles=nx // mb, n_ex=B)

    dww_g, waf_g, wab_g, c_g = _exchange([dw_weight[0], w_alpha_f[0], w_alpha_b[0], c], False, "gather_first")

    def cols(gat):
        return jnp.transpose(gat, (1, 0, 2)).reshape(gat.shape[1], N_DEV * gat.shape[2])

    def rows_(gat):
        return gat.reshape(N_DEV * gat.shape[1], gat.shape[2])

    dww = cols(dww_g)
    WA = jnp.zeros((LR_PAD, 2 * QK), f32).at[:LOWRANK, :QK].set(cols(waf_g)).at[LOWRANK:2 * LOWRANK, QK:].set(cols(wab_g)).astype(bf16)
    BA = jnp.concatenate([b_alpha_f, b_alpha_b], axis=1)
    c_all = c_g.reshape(N_DEV * B, D)
    c_ctx2 = c_ctx.reshape(1, D)

    ncm = w_mod.shape[2]
    b_mod_loc = lax.dynamic_slice(b_mod, (0, me * ncm), (1, ncm))
    mod_loc = _mod_fwd(c_all, c_ctx2, w_mod[0], b_mod_loc, "mod_fwd")
    (mod_g,) = _exchange([mod_loc], False, "gather_mod")
    mod_full = cols(mod_g)
    mod_tab = jnp.concatenate([lax.dynamic_slice(mod_full, (me * B, 0), (B, N_MOD * D)), mod_full[N_DEV * B:N_DEV * B + 1]], axis=0)
    mods = [mod_tab[:, i * D:(i + 1) * D].reshape(B + 1, 1, D) for i in range(N_MOD)]
    mods_x = [mm[:B] for mm in mods]

    x_lat, x_ctx = (x.reshape(T, D), D, 0, True), (ctx.reshape(Tc, D), D, 0, "c")

    def f_ffn_in(tok, ex, sh):
        return [_rms_mod(tok[0] + tok[1], sh[0], ex[0], ex[1])], [], []

    (u1,), (w1gu_g,) = _rowwise(f_ffn_in, name="ffn1_in", tok_in=[x_lat, x_ctx], ex_in=[mods[0], mods[1]], sh_in=[g_ffn1],
                                tok_out=[(D, bf16)], carry=([w1_gu[0].astype(bf16)], False), **rw_all_b)
    W1gu = cols(w1gu_g)
    (gu1, h1), (w1d_g, win_g) = _ffn_up(u1, W1gu, "ffn1_up", carry=([w1_down[0].astype(bf16), w_in[0].astype(bf16)], False))
    W1d = rows_(w1d_g)
    lr2 = 2 * LOWRANK
    segs = [(0, 2 * D, 0), (2 * D, 2 * D + QK, 6 * D), (2 * D + QK, 3 * D, 6 * D + QK), (3 * D, 4 * D, 2 * D), (4 * D, 5 * D, 3 * D),
            (5 * D, 5 * D + lr2, 7 * D), (5 * D + lr2, 6 * D + lr2, 4 * D), (6 * D + lr2, 7 * D + lr2, 5 * D)]
    wc = w_in.shape[2]
    win_parts = []
    for lo, hi, _ in sorted(segs, key=lambda t: t[2]):
        for d in range(N_DEV):
            a0, a1 = max(lo, d * wc), min(hi, (d + 1) * wc)
            if a0 < a1:
                win_parts.append(win_g[d][:, a0 - d * wc:a1 - d * wc])
    Win = jnp.concatenate(win_parts + [jnp.zeros((D, LR_PAD - lr2), bf16)], axis=1)

    def nn(a, w):
        return jnp.dot(a.astype(bf16), w, preferred_element_type=f32)

    def nt(a, w):
        return lax.dot_general(a.astype(bf16), w, (((1,), (1,)), ((), ())), preferred_element_type=f32)

    def mix_in(xv, fv, gate, sh, sc, g):
        x1 = xv + 0.5 * gate * fv
        return x1, _rms_mod(x1, g, sh, sc)

    def f_mix_in(tok, ex, sh):
        f1v = nn(tok[2], sh[1])
        return list(mix_in(tok[0] + tok[1], f1v, ex[0], ex[1], ex[2], sh[0])) + [f1v], [], []

    x1, um, f1 = _rowwise(f_mix_in, name="ffn1_down_mix_in", tok_in=[x_lat, x_ctx, (h1, F, 0, False)],
                          ex_in=[mods[2], mods[3], mods[4]], sh_in=[g_mix, W1d], tok_out=[(D, f32), (D, bf16), (D, bf16)], **rw_all2)
    p_all, (wco_g, wgo_g, wo_g, w2gu_g) = _matmul(
        um, Win, "nn", bf16, "in_proj", tm_cap=1024, tn_cap=2432,
        carry=([w_conv_out[0].astype(bf16), w_gla_out[0].astype(bf16), w_out[0].astype(bf16), w2_gu[0].astype(bf16)], False))
    Wco, Wgo, Wo, W2gu = rows_(wco_g), rows_(wgo_g), rows_(wo_g), cols(w2gu_g)

    def log_decay(lr, wa, ba):
        z = _bdot(lr, wa, (1, 0)) + ba
        return _log_sigmoid(z) / TAU

    def f_decay(tok, ex, sh):
        return [log_decay(tok[0], sh[0], sh[1])], [], []

    lr_blk = (p_all, LR_PAD, 7 * D // LR_PAD, False)
    (la_all,) = _rowwise(f_decay, name="log_decay", tok_in=[lr_blk], sh_in=[WA, BA], tok_out=[(2 * QK, f32)], **rw_all_b)

    zeros_s = jnp.zeros((B, HEADS, DV, DK), f32)
    gla_c = dict(row0=T, nb=B, seq=Lc, D=D)
    gla_x = dict(row0=0, nb=B, seq=L, D=D)
    _, hist_cf, s_f = _gla_fwd(p_all, la_all, zeros_s, rev=False, name="gla_ctx_f", **gla_c)
    _, hist_cb, s_b = _gla_fwd(p_all, la_all, zeros_s, rev=True, name="gla_ctx_b", **gla_c)
    (o_f, hist_f, _), (w2d_g,) = _gla_fwd(p_all, la_all, s_f, rev=False, name="gla_x_f", carry=([w2_down[0].astype(bf16)], False), **gla_x)
    W2d = rows_(w2d_g)
    o_b, hist_b, _ = _gla_fwd(p_all, la_all, s_b, rev=True, name="gla_x_b", **gla_x)

    cz = _conv_fwd(p_all, dww, dw_bias, B=B, L=L, D=D, name="conv_fwd")

    def ln_silu(z, g, b):
        mu = jnp.mean(z, axis=-1, keepdims=True)
        var = jnp.mean(jnp.square(z - mu), axis=-1, keepdims=True)
        return jax.nn.silu((z - mu) * lax.rsqrt(var + EPS) * g + b)

    def f_ln(tok, ex, sh):
        zc = ln_silu(tok[0].astype(f32), sh[0], sh[1])
        return [zc, nn(zc, sh[2])], [], []

    zc, yc = _rowwise(f_ln, name="conv_ln_out", tok_in=[(cz, D, 0, False)], sh_in=[conv_ln_g, conv_ln_b, Wco],
                      tok_out=[(D, bf16), (D, bf16)], **rw_x_b)

    def gla_out(of, ob, og, gn):
        return _head_rms(of.astype(f32) + ob.astype(f32), DV) * gn * jax.nn.silu(og.astype(f32))

    def f_gla_out(tok, ex, sh):
        og2 = gla_out(tok[0], tok[1], tok[2], sh[0])
        return [og2, nn(og2, sh[1])], [], []

    og_blk = (p_all, D, 3, False)
    og2, yg = _rowwise(f_gla_out, name="gla_norm_out", tok_in=[(o_f, D, 0, False), (o_b, D, 0, False), og_blk], sh_in=[gla_norm_g, Wgo],
                       tok_out=[(D, bf16), (D, bf16)], **rw_x_b)

    def merge(ga, gb, ycv, ygv):
        return jax.nn.sigmoid(ga.astype(f32)) * ycv.astype(f32) + jax.nn.sigmoid(gb.astype(f32)) * ygv.astype(f32)

    def f_merge(tok, ex, sh):
        mg = merge(*tok)
        return [mg, nn(mg, sh[0])], [], []

    ga_blk, gb_blk = (p_all, D, 4, False), (p_all, D, 5, False)
    mg, mix = _rowwise(f_merge, name="merge_mix_out", tok_in=[ga_blk, gb_blk, (yc, D, 0, False), (yg, D, 0, False)], sh_in=[Wo],
                       tok_out=[(D, bf16), (D, f32)], **rw_x_b)

    def ffn2_in(x1v, mixv, g5, sh, sc, g):
        x2 = x1v + g5 * mixv
        return x2, _rms_mod(x2, g, sh, sc)

    def f_ffn2_in(tok, ex, sh):
        return list(ffn2_in(tok[0], tok[1], ex[0], ex[1], ex[2], sh[0])), [], []

    x2, u2 = _rowwise(f_ffn2_in, name="ffn2_in", tok_in=[(x1, D, 0, False), (mix, D, 0, False)], ex_in=[mods_x[5], mods_x[6], mods_x[7]],
                      sh_in=[g_ffn2], tok_out=[(D, f32), (D, bf16)], **rw_x_b)
    gu2, h2 = _ffn_up(u2, W2gu, "ffn2_up")

    gf2 = g_final.reshape(1, D)

    def head_loss(x2v, f2v, g8, gf, tgt):
        x3 = x2v + 0.5 * g8 * f2v
        y = x3 * lax.rsqrt(jnp.mean(x3 * x3, axis=-1, keepdims=True) + EPS) * gf
        return 0.5 * jnp.sum(jnp.mean(jnp.square(y - tgt), axis=-1))

    def f_head(tok, ex, sh):
        loss, vjp = jax.vjp(lambda a, b_, c_, d_: head_loss(a, b_, c_, d_, tok[2]), tok[0], nn(tok[1], sh[1]), ex[0], sh[0])
        dx3, df2, dg8, dgf = vjp(jnp.ones((), f32))
        return [dx3, df2], [dg8], [dgf, jnp.broadcast_to(loss.reshape(1, 1), (1, 128))]

    dx3, df2, dg8, dgf, loss_p = _rowwise(
        f_head, name="ffn2_down_head", tok_in=[(x2, D, 0, False), (h2, F, 0, False), (loss_target.reshape(T, D), D, 0, False)],
        ex_in=[mods_x[8]], sh_in=[gf2, W2d], tok_out=[(D, f32), (D, bf16)], ex_out=[D], gl_out=[(1, D), (1, 128)], **rw_x2)

    dgu2 = _ffn_down_dx(df2, W2d, gu2, "ffn2_down_dx")
    gW2d = _matmul(h2, df2, "tn", f32, "ffn2_down_dw", tm_cap=1408)
    du2 = _matmul(dgu2, W2gu, "nt", bf16, "ffn2_up_dx", halves="a")
    gW2gu = _matmul(u2, dgu2, "tn", f32, "ffn2_up_dw", halves="b")

    def f_ffn2_in_bwd(tok, ex, sh):
        _, vjp = jax.vjp(ffn2_in, tok[0], tok[1], ex[0], ex[1], ex[2], sh[0])
        dx2, dmix, dg5, dsh, dsc, dg = vjp((tok[3], tok[2].astype(f32)))
        return [dx2, dmix], [dg5, dsh, dsc], [dg]

    dx2, dmix, dg5, dsh6, dsc7, dg_ffn2 = _rowwise(
        f_ffn2_in_bwd, name="ffn2_in_bwd", tok_in=[(x1, D, 0, False), (mix, D, 0, False), (du2, D, 0, False), (dx3, D, 0, False)],
        ex_in=[mods_x[5], mods_x[6], mods_x[7]], sh_in=[g_ffn2], tok_out=[(D, f32), (D, bf16)], ex_out=[D, D, D], gl_out=[(1, D)], **rw_x)

    gWo = _matmul(mg, dmix, "tn", f32, "mix_out_dw")

    def f_merge_bwd(tok, ex, sh):
        _, vjp = jax.vjp(merge, *[t.astype(f32) for t in tok[:4]])
        dga, dgb, dyc, dyg = vjp(nt(tok[4], sh[0]))
        return [dga, dgb, dyc, dyg], [], []

    dga, dgb, dyc, dyg = _rowwise(f_merge_bwd, name="mix_out_merge_bwd",
                                  tok_in=[ga_blk, gb_blk, (yc, D, 0, False), (yg, D, 0, False), (dmix, D, 0, False)], sh_in=[Wo],
                                  tok_out=[(D, bf16)] * 4, **rw_x)
    gWco = _matmul(zc, dyc, "tn", f32, "conv_out_dw")
    gWgo = _matmul(og2, dyg, "tn", f32, "gla_out_dw")

    def f_ln_bwd(tok, ex, sh):
        _, vjp = jax.vjp(ln_silu, tok[0].astype(f32), sh[0], sh[1])
        dcz, dg, db = vjp(nt(tok[1], sh[2]))
        return [dcz], [], [dg, db, jnp.sum(dcz, axis=0, keepdims=True)]

    dcz, g_ln_g, g_ln_b, g_dwb = _rowwise(f_ln_bwd, name="conv_out_ln_bwd", tok_in=[(cz, D, 0, False), (dyc, D, 0, False)],
                                          sh_in=[conv_ln_g, conv_ln_b, Wco], tok_out=[(D, bf16)], gl_out=[(1, D)] * 3, **rw_x_b)
    def col_shards(g):
        return jnp.transpose(g.reshape(g.shape[0], N_DEV, g.shape[1] // N_DEV), (1, 0, 2)).astype(bf16)

    def row_shards(g):
        return g.reshape(N_DEV, g.shape[0] // N_DEV, g.shape[1]).astype(bf16)

    (dca, dcb, g_dww), (r_w2d, r_w2gu, r_wo, r_wco, r_wgo) = _conv_bwd(
        p_all, dcz, dww, B=B, L=L, D=D, name="conv_bwd",
        carry=([row_shards(gW2d), col_shards(gW2gu), row_shards(gWo), row_shards(gWco), row_shards(gWgo)], True))

    def f_gla_out_bwd(tok, ex, sh):
        _, vjp = jax.vjp(gla_out, tok[0].astype(f32), tok[1].astype(f32), tok[2].astype(f32), sh[0])
        dof, _, dog, dgn = vjp(nt(tok[3], sh[1]))
        return [dof, dog], [], [dgn]

    d_o, dog, g_gn = _rowwise(f_gla_out_bwd, name="gla_out_norm_bwd",
                              tok_in=[(o_f, D, 0, False), (o_b, D, 0, False), og_blk, (dyg, D, 0, False)], sh_in=[gla_norm_g, Wgo],
                              tok_out=[(D, bf16), (D, bf16)], gl_out=[(1, D)], **rw_x)

    dq_f, dk_f, dv_f, dla_f, ds_f = _gla_bwd(p_all, la_all, hist_f, d_o, zeros_s, rev=False, name="gla_x_f_bwd", **gla_x)
    dq, dk, dv, dla_b, ds_b = _gla_bwd(p_all, la_all, hist_b, d_o, zeros_s, rev=True, name="gla_x_b_bwd", add=(dq_f, dk_f, dv_f), **gla_x)
    dq_cf, dk_cf, dv_cf, dla_cf, _ = _gla_bwd(p_all, la_all, hist_cf, None, ds_f, rev=False, name="gla_ctx_f_bwd", **gla_c)
    _, dk_c, dv_c, dla_cb, _ = _gla_bwd(p_all, la_all, hist_cb, None, ds_b, rev=True, name="gla_ctx_b_bwd", add=(dq_cf, dk_cf, dv_cf), **gla_c)

    dla_all = jnp.concatenate([jnp.concatenate([dla_f, dla_b], axis=1), jnp.concatenate([dla_cf, dla_cb], axis=1)], axis=0)

    def f_decay_bwd(tok, ex, sh):
        _, vjp = jax.vjp(log_decay, tok[0].astype(f32), sh[0].astype(f32), sh[1])
        dlr, dwa, dba = vjp(tok[1])
        return [dlr], [], [dwa, dba]

    dlr, g_WA, g_BA = _rowwise(f_decay_bwd, name="log_decay_bwd", tok_in=[lr_blk, (dla_all, 2 * QK, 0, False)], sh_in=[WA, BA],
                               tok_out=[(LR_PAD, bf16)], gl_out=[(LR_PAD, 2 * QK), (1, 2 * QK)], **rw_all_b)

    zc_ = functools.partial(jnp.zeros, dtype=bf16)
    dp_x = jnp.concatenate([dca, dcb, dv, dog, dga, dgb, dq, dk, dlr[:T]], axis=1)
    dp_c = jnp.concatenate([zc_((Tc, 2 * D)), dv_c, zc_((Tc, 3 * D)), zc_((Tc, QK)), dk_c, dlr[T:]], axis=1)
    dp_all = jnp.concatenate([dp_x, dp_c], axis=0)
    gWin_p = _matmul(um, dp_all, "tn", f32, "in_proj_dw", tm_cap=512, tn_cap=2432)
    gwin_shards = []
    for d in range(N_DEV):
        parts = []
        for lo, hi, po in segs:
            a0, a1 = max(lo, d * wc), min(hi, (d + 1) * wc)
            if a0 < a1:
                parts.append(gWin_p[:, po + a0 - lo:po + a1 - lo])
        gwin_shards.append(jnp.concatenate(parts, axis=1))
    dum, (r_win,) = _matmul(dp_all, Win, "nt", bf16, "in_proj_dx", tk_cap=2432, carry=([jnp.stack(gwin_shards).astype(bf16)], True))

    def f_mix_in_bwd(tok, ex, sh):
        _, vjp = jax.vjp(mix_in, tok[0] + tok[1], tok[2].astype(f32), ex[0], ex[1], ex[2], sh[0])
        dx1, df1, dgate, dsh, dsc, dg = vjp((tok[4], tok[3].astype(f32)))
        return [dx1, df1], [dgate, dsh, dsc], [dg]

    dx1, df1, dg2, dsh3, dsc4, dg_mix = _rowwise(
        f_mix_in_bwd, name="mix_in_bwd", tok_in=[x_lat, x_ctx, (f1, D, 0, False), (dum, D, 0, False), (dx2, D, 0, True)],
        ex_in=[mods[2], mods[3], mods[4]], sh_in=[g_mix], tok_out=[(D, f32), (D, bf16)], ex_out=[D, D, D], gl_out=[(1, D)], **rw_all)

    dgu1 = _ffn_down_dx(df1, W1d, gu1, "ffn1_down_dx")
    gW1d = _matmul(h1, df1, "tn", f32, "ffn1_down_dw", tm_cap=1408)
    gW1gu, (r_w1d,) = _matmul(u1, dgu1, "tn", f32, "ffn1_up_dw", carry=([row_shards(gW1d)], True), halves="b")
    du1, (r_w1gu,) = _matmul(dgu1, W1gu, "nt", bf16, "ffn1_up_dx", carry=([col_shards(gW1gu)], True), halves="a")

    def f_ffn_in_bwd(tok, ex, sh):
        _, vjp = jax.vjp(_rms_mod, tok[0] + tok[1], sh[0], ex[0], ex[1])
        dx, dg, dsh, dsc = vjp(tok[2].astype(f32))
        return [dx + tok[3]], [dsh, dsc], [dg]

    dx_lat, dsh0, dsc1, dg_ffn1 = _rowwise(
        f_ffn_in_bwd, name="ffn1_in_bwd", tok_in=[x_lat, x_ctx, (du1, D, 0, False), (dx1, D, 0, False)],
        ex_in=[mods[0], mods[1]], sh_in=[g_ffn1], tok_out=[(D, f32, "x")], ex_out=[D, D], gl_out=[(1, D)], **rw_all)
    grad_x = dx_lat.reshape(B, L, D)

    zrow = jnp.zeros((1, 1, D), f32)
    dmod_loc = jnp.concatenate([dsh0, dsc1, dg2, dsh3, dsc4] + [jnp.concatenate([t, zrow], axis=0) for t in (dg5, dsh6, dsc7, dg8)],
                               axis=2).reshape(B + 1, N_MOD * D)
    rows16 = jnp.concatenate([jnp.concatenate([loss_p, jnp.zeros((1, D - loss_p.shape[1]), f32)], axis=1), dg_ffn1, dg_mix, g_dwb, g_ln_g,
                              g_ln_b, g_BA, g_gn, dg_ffn2, dgf, jnp.zeros((6, D), f32)], axis=0)

    def to8(v):
        n_pad = -(-v.shape[1] // 1024) * 1024
        return jnp.pad(v, ((0, 0), (0, n_pad - v.shape[1]))).reshape(8, n_pad // 8)

    def from8(a, n):
        return a.reshape(1, a.size)[:, :n]

    def adam_big(nm, wv, mv, vv, part, carry=None):
        out = _adamw(wv[0], mv[0], vv[0], part, "adamw_" + nm, True, carry=carry)
        res4, carried = out if carry is not None else (out, None)
        return [t[None] for t in res4], carried

    rs_out = dict(w1_gu=r_w1gu, w1_down=r_w1d, w_in=r_win, w_conv_out=r_wco, w_gla_out=r_wgo, w_out=r_wo, w2_gu=r_w2gu, w2_down=r_w2d)
    big = {}
    big["w_in"], (dmod_g, rows_g, dww_sg, wa_sg) = adam_big(
        "w_in", w_in, m_w_in, v_w_in, rs_out["w_in"], carry=([dmod_loc, rows16, g_dww, g_WA[:2 * LOWRANK]], False))
    dmx = dmod_g[:, :B].reshape(N_DEV * B, N_MOD * D)
    dmc = dmod_g[:, B]
    gWmod, gcc_p = _mod_bwd(c_all, c_ctx2, w_mod[0], lax.dynamic_slice(dmx, (0, me * ncm), (N_DEV * B, ncm)),
                            lax.dynamic_slice(dmc, (0, me * ncm), (N_DEV, ncm)), "mod_bwd")

    big["w1_gu"], (gcc_g,) = adam_big("w1_gu", w1_gu, m_w1_gu, v_w1_gu, rs_out["w1_gu"], carry=([to8(gcc_p)], False))
    for nm, wv, mv, vv in (("w1_down", w1_down, m_w1_down, v_w1_down), ("w_conv_out", w_conv_out, m_w_conv_out, v_w_conv_out),
                           ("w_gla_out", w_gla_out, m_w_gla_out, v_w_gla_out), ("w_out", w_out, m_w_out, v_w_out),
                           ("w2_gu", w2_gu, m_w2_gu, v_w2_gu), ("w2_down", w2_down, m_w2_down, v_w2_down)):
        big[nm], _ = adam_big(nm, wv, mv, vv, rs_out[nm])
    big["w_mod"] = [t[None] for t in _adamw(w_mod[0], m_w_mod[0], v_w_mod[0], gWmod, "adamw_w_mod", False)]

    rows_s, dww_s, wa_s, g_cc, g_bmod = _sum_sources(
        [rows_g, dww_sg, wa_sg, gcc_g, jnp.concatenate([dmx, dmc], axis=0).reshape(N_DEV * (B + 1), 8, N_MOD * D // 8)], "sum_small")
    g_cc, g_bmod = from8(g_cc, D), from8(g_bmod, N_MOD * D)
    loss = rows_s[0, 0]
    ncd, nca = dw_weight.shape[2], w_alpha_f.shape[2]
    g_dww_loc = lax.dynamic_slice(dww_s, (0, me * ncd), (CONV_W, ncd))
    g_waf_loc = lax.dynamic_slice(wa_s, (0, me * nca), (LOWRANK, nca))
    g_wab_loc = lax.dynamic_slice(wa_s, (LOWRANK, QK + me * nca), (LOWRANK, nca))
    sm = {k: rows_s[i:i + 1] for i, k in enumerate(["loss", "g_ffn1", "g_mix", "dw_bias", "conv_ln_g", "conv_ln_b", "b_alpha", "gla_norm_g",
                                                     "g_ffn2", "g_final"])}

    small_params = [("c_ctx", c_ctx, m_c_ctx, v_c_ctx, g_cc), ("b_mod", b_mod, m_b_mod, v_b_mod, g_bmod),
                    ("g_ffn1", g_ffn1, m_g_ffn1, v_g_ffn1, sm["g_ffn1"]), ("g_mix", g_mix, m_g_mix, v_g_mix, sm["g_mix"]),
                    ("dw_weight", dw_weight, m_dw_weight, v_dw_weight, g_dww_loc), ("dw_bias", dw_bias, m_dw_bias, v_dw_bias, sm["dw_bias"]),
                    ("conv_ln_g", conv_ln_g, m_conv_ln_g, v_conv_ln_g, sm["conv_ln_g"]),
                    ("conv_ln_b", conv_ln_b, m_conv_ln_b, v_conv_ln_b, sm["conv_ln_b"]),
                    ("w_alpha_f", w_alpha_f, m_w_alpha_f, v_w_alpha_f, g_waf_loc), ("b_alpha_f", b_alpha_f, m_b_alpha_f, v_b_alpha_f, sm["b_alpha"][:, :QK]),
                    ("w_alpha_b", w_alpha_b, m_w_alpha_b, v_w_alpha_b, g_wab_loc), ("b_alpha_b", b_alpha_b, m_b_alpha_b, v_b_alpha_b, sm["b_alpha"][:, QK:]),
                    ("gla_norm_g", gla_norm_g, m_gla_norm_g, v_gla_norm_g, sm["gla_norm_g"]),
                    ("g_ffn2", g_ffn2, m_g_ffn2, v_g_ffn2, sm["g_ffn2"]), ("g_final", g_final, m_g_final, v_g_final, sm["g_final"])]

    def two_d(t, like):
        return t.reshape(like.shape[1:]) if like.ndim == 3 else t.reshape(like.size // 128, 128)

    small_res = _adamw_many([tuple(two_d(t, wv) for t in (wv, mv, vv, gv)) for _, wv, mv, vv, gv in small_params], "adamw_small")
    small_out = {nm: [gv.reshape(wv.shape)] + [t.reshape(wv.shape) for t in r3]
                 for (nm, wv, _, _, gv), r3 in zip(small_params, small_res)}

    order = ["c_ctx", "w_mod", "b_mod", "g_ffn1", "w1_gu", "w1_down", "g_mix", "w_in", "dw_weight", "dw_bias", "conv_ln_g", "conv_ln_b",
             "w_conv_out", "w_alpha_f", "b_alpha_f", "w_alpha_b", "b_alpha_b", "gla_norm_g", "w_gla_out", "w_out", "g_ffn2", "w2_gu",
             "w2_down", "g_final"]
    res = {**big, **small_out}
    return (loss, grad_x, *[res[n][0] for n in order], *[res[n][1] for n in order], *[res[n][2] for n in order], *[res[n][3] for n in order])
```

```python
import functools

import jax
import jax.numpy as jnp
from jax import lax
from jax.experimental import pallas as pl
from jax.experimental.pallas import tpu as pltpu

f32, bf16 = jnp.float32, jnp.bfloat16

N_DEV = 8
HEADS = 4
LOWRANK = 16
CONV_W = 31
CONV_PAD = 16
SUBLANES = 8
CHUNK = 64
SUB = 16
GLA_ROWS = 256
GLA_SAFE_DECAY = 60.0
TAU = 16.0
EPS = 1e-6
N_MOD = 9
LR_PAD = 128
ROW_TILE = 512
V7X_VMEM_BYTES = 64 << 20
VMEM_LIMIT = (V7X_VMEM_BYTES * 3) // 4

ADAM_LR, ADAM_B1, ADAM_B2, ADAM_EPS, ADAM_WD, ADAM_STEP = 0.001, 0.9, 0.999, 1e-08, 0.01, 10

MESH = pl.DeviceIdType.MESH


def _pc(body, **kw):
    return pl.pallas_call(body, **kw)


def _params(*sem):
    return pltpu.CompilerParams(dimension_semantics=sem, vmem_limit_bytes=VMEM_LIMIT)


def _pick(n, cap, unit=128):
    best = None
    for t in range(unit, min(n, cap) + 1, unit):
        if n % t == 0:
            best = t
    return best or n


def _matmul(a, b, mode, out_dtype, name, tm_cap=1024, tn_cap=1536, tk_cap=None, carry=None, halves=None):
    tk_cap = tk_cap or (2048 if mode == "tn" else 2816)
    if halves == "a":
        (_, M, Kh), N = a.shape, b.shape[0]
        K, tk = 2 * Kh, _pick(Kh, tk_cap)
        tm, tn = _pick(M, tm_cap), _pick(N, tn_cap)
        a_spec = pl.BlockSpec((None, tm, tk), lambda i, j, k: (k // (Kh // tk), i, k % (Kh // tk)))
    elif halves == "b":
        (K, M), (_, _, Nh) = a.shape, b.shape
        N, tn = 2 * Nh, _pick(Nh, tn_cap)
        tm, tk = _pick(M, tm_cap), _pick(K, tk_cap)
    else:
        if mode == "tn":
            (K, M), N = a.shape, b.shape[1]
        elif mode == "nt":
            (M, K), N = a.shape, b.shape[0]
        else:
            (M, K), N = a.shape, b.shape[1]
        tm, tn, tk = _pick(M, tm_cap), _pick(N, tn_cap), _pick(K, tk_cap)
    nk = K // tk
    if halves != "a":
        a_spec = pl.BlockSpec((tk, tm), lambda i, j, k: (k, i)) if mode == "tn" else pl.BlockSpec((tm, tk), lambda i, j, k: (i, k))
    if halves == "b":
        b_spec = pl.BlockSpec((None, tk, tn), lambda i, j, k: (j // (Nh // tn), k, j % (Nh // tn)))
    else:
        b_spec = pl.BlockSpec((tn, tk), lambda i, j, k: (j, k)) if mode == "nt" else pl.BlockSpec((tk, tn), lambda i, j, k: (k, j))
    dims = {"nn": ((1,), (0,)), "nt": ((1,), (1,)), "tn": ((0,), (0,))}[mode]

    def body_single(a_ref, b_ref, o_ref):
        o_ref[...] = lax.dot_general(a_ref[...].astype(bf16), b_ref[...].astype(bf16), (dims, ((), ())),
                                     preferred_element_type=f32).astype(out_dtype)

    def body(a_ref, b_ref, o_ref, acc_ref):
        k = pl.program_id(2)
        part = lax.dot_general(a_ref[...].astype(bf16), b_ref[...].astype(bf16), (dims, ((), ())), preferred_element_type=f32)

        @pl.when(k == 0)
        def _():
            acc_ref[...] = part

        @pl.when(k > 0)
        def _():
            acc_ref[...] += part

        @pl.when(k == nk - 1)
        def _():
            o_ref[...] = acc_ref[...].astype(out_dtype)

    (out,), carried = _call(
        body_single if nk == 1 else body, name=name, grid=(M // tm, N // tn, nk), in_specs=[a_spec, b_spec],
        out_specs=[pl.BlockSpec((tm, tn), lambda i, j, k: (i, j))], out_shape=[jax.ShapeDtypeStruct((M, N), out_dtype)],
        scratch_shapes=[] if nk == 1 else [pltpu.VMEM((tm, tn), f32)], sem=("parallel", "parallel", "arbitrary"),
        args=(a, b), carry=carry)
    return out if carry is None else (out, carried)


def _ffn_up(u, Wgu, name, carry=None):
    M, K = u.shape
    F = Wgu.shape[1] // 2
    tm, tn = _pick(M, 512), _pick(F, 1408)
    nj = F // tn

    def body(u_ref, wa_ref, wb_ref, gu_ref, h_ref):
        uv = u_ref[...]
        a = jnp.dot(uv, wa_ref[...], preferred_element_type=f32)
        b = jnp.dot(uv, wb_ref[...], preferred_element_type=f32)
        gu_ref[0] = a.astype(bf16)
        gu_ref[1] = b.astype(bf16)
        h_ref[...] = (jax.nn.silu(a) * b).astype(bf16)

    res, carried = _call(
        body, name=name, grid=(nj, M // tm),
        in_specs=[pl.BlockSpec((tm, K), lambda j, i: (i, 0)), pl.BlockSpec((K, tn), lambda j, i: (0, j)),
                  pl.BlockSpec((K, tn), lambda j, i: (0, nj + j))],
        out_specs=[pl.BlockSpec((2, tm, tn), lambda j, i: (0, i, j)), pl.BlockSpec((tm, tn), lambda j, i: (i, j))],
        out_shape=[jax.ShapeDtypeStruct((2, M, F), bf16), jax.ShapeDtypeStruct((M, F), bf16)],
        scratch_shapes=[], sem=("parallel", "parallel"), args=(u, Wgu, Wgu), carry=carry)
    return res if carry is None else (res, carried)


def _ffn_down_dx(df, Wd, gu, name):
    M, D = df.shape
    F = Wd.shape[0]
    tm, tn = _pick(M, 512), _pick(F, 1408)

    def body(df_ref, w_ref, gu_ref, o_ref):
        dh = lax.dot_general(df_ref[...], w_ref[...], (((1,), (1,)), ((), ())), preferred_element_type=f32)
        a, b = gu_ref[0].astype(f32), gu_ref[1].astype(f32)
        sg = jax.nn.sigmoid(a)
        o_ref[0] = (dh * b * sg * (1.0 + a * (1.0 - sg))).astype(bf16)
        o_ref[1] = (dh * a * sg).astype(bf16)

    return _pc(
        body, name=name, grid=(F // tn, M // tm),
        in_specs=[pl.BlockSpec((tm, D), lambda j, i: (i, 0)), pl.BlockSpec((tn, D), lambda j, i: (j, 0)),
                  pl.BlockSpec((2, tm, tn), lambda j, i: (0, i, j))],
        out_specs=pl.BlockSpec((2, tm, tn), lambda j, i: (0, i, j)), out_shape=jax.ShapeDtypeStruct((2, M, F), bf16),
        compiler_params=_params("parallel", "parallel"))(df, Wd, gu)


def _rowwise(fn, *, name, tm, n_tiles, tpe, nx_tiles, n_ex, tok_in=(), ex_in=(), sh_in=(), tok_out=(), ex_out=(), gl_out=(), carry=None):
    def seg(i):
        return jnp.minimum(i // tpe, n_ex - 1)

    in_specs, args = [], []
    for arr, w, cb, x_only in tok_in:
        if x_only == "c":
            in_specs.append(pl.BlockSpec((tm, w), functools.partial(lambda i, cb: (jnp.maximum(i - nx_tiles, 0), cb), cb=cb)))
        elif x_only:
            in_specs.append(pl.BlockSpec((tm, w), functools.partial(lambda i, cb: (jnp.minimum(i, nx_tiles - 1), cb), cb=cb)))
        else:
            in_specs.append(pl.BlockSpec((tm, w), functools.partial(lambda i, cb: (i, cb), cb=cb)))
        args.append(arr)
    for arr in ex_in:
        in_specs.append(pl.BlockSpec((1, 1, arr.shape[-1]), lambda i: (seg(i), 0, 0)))
        args.append(arr)
    for arr in sh_in:
        in_specs.append(pl.BlockSpec(arr.shape, functools.partial(lambda i, nd: (0,) * nd, nd=arr.ndim)))
        args.append(arr)
    out_specs, out_shape = [], []
    for w, dt, *x_rows in tok_out:
        if x_rows:
            out_specs.append(pl.BlockSpec((tm, w), lambda i: (jnp.minimum(i, nx_tiles - 1), 0)))
        else:
            out_specs.append(pl.BlockSpec((tm, w), lambda i: (i, 0)))
        out_shape.append(jax.ShapeDtypeStruct(((nx_tiles if x_rows else n_tiles) * tm, w), dt))
    for w in ex_out:
        out_specs.append(pl.BlockSpec((1, 1, w), lambda i: (seg(i), 0, 0)))
        out_shape.append(jax.ShapeDtypeStruct((n_ex, 1, w), f32))
    for r, w in gl_out:
        out_specs.append(pl.BlockSpec((r, w), lambda i: (0, 0)))
        out_shape.append(jax.ShapeDtypeStruct((r, w), f32))
    n_tok, n_exi, n_sh = len(tok_in), len(ex_in), len(sh_in)
    n_to, n_eo = len(tok_out), len(ex_out)
    x_only_flags = [t[3] for t in tok_in]
    x_rows_flags = [len(t) > 2 for t in tok_out]

    def body(*refs):
        i = pl.program_id(0)
        ins, outs = refs[: n_tok + n_exi + n_sh], refs[n_tok + n_exi + n_sh:]
        is_x = i < nx_tiles
        tok_vals = []
        for r, xo in zip(ins[:n_tok], x_only_flags):
            v = r[...]
            if xo == "c":
                v = jnp.where(is_x, jnp.zeros_like(v), v)
            elif xo:
                v = jnp.where(is_x, v, jnp.zeros_like(v))
            tok_vals.append(v)
        ex_vals = [r[0] for r in ins[n_tok:n_tok + n_exi]]
        sh_vals = [r[...] for r in ins[n_tok + n_exi:]]
        t_o, e_o, g_o = fn(tok_vals, ex_vals, sh_vals)
        for r, v, xr in zip(outs[:n_to], t_o, x_rows_flags):
            if xr:
                @pl.when(is_x)
                def _(r=r, v=v):
                    r[...] = v.astype(r.dtype)
            else:
                r[...] = v.astype(r.dtype)
        first = jnp.logical_and(i % tpe == 0, i <= nx_tiles)
        for r, v in zip(outs[n_to:n_to + n_eo], e_o):
            @pl.when(first)
            def _(r=r, v=v):
                r[0] = v

            @pl.when(jnp.logical_not(first))
            def _(r=r, v=v):
                r[0] += v
        for r, v in zip(outs[n_to + n_eo:], g_o):
            @pl.when(i == 0)
            def _(r=r, v=v):
                r[...] = v

            @pl.when(i > 0)
            def _(r=r, v=v):
                r[...] += v

    res, carried = _call(body, name=name, grid=(n_tiles,), in_specs=in_specs, out_specs=out_specs, out_shape=out_shape,
                         scratch_shapes=[], sem=("arbitrary",), args=args, carry=carry)
    return res if carry is None else (res, carried)


def _rms_mod(x, g, sh, sc):
    y = x * lax.rsqrt(jnp.mean(x * x, axis=-1, keepdims=True) + EPS) * g
    return y * (1.0 + sc) + sh


def _log_sigmoid(z):
    return jnp.minimum(z, 0.0) - jnp.log(1.0 + jnp.exp(-jnp.abs(z)))


def _head_rms(o, DV):
    parts = []
    for h in range(HEADS):
        oh = o[:, h * DV:(h + 1) * DV]
        parts.append(oh * lax.rsqrt(jnp.mean(oh * oh, axis=-1, keepdims=True) + EPS))
    return jnp.concatenate(parts, axis=1)


@functools.partial(jax.custom_vjp, nondiff_argnums=(2,))
def _bdot(a, b, dims):
    return lax.dot_general(a.astype(bf16), b.astype(bf16), (((dims[0],), (dims[1],)), ((), ())), preferred_element_type=f32)


def _bdot_fwd(a, b, dims):
    return _bdot(a, b, dims), (a, b)


def _bdot_bwd(dims, res, g):
    a, b = res
    ca, cb = dims
    da = _bdot(g, b, (1, 1 - cb)) if ca == 1 else _bdot(b, g, (1 - cb, 1))
    db = _bdot(a, g, (1 - ca, 0)) if cb == 0 else _bdot(g, a, (0, 1 - ca))
    return da, db


_bdot.defvjp(_bdot_fwd, _bdot_bwd)


def _split_dot(m, x, dims):
    mb, rem, acc = m.astype(bf16), x, None
    for _ in range(3):
        piece = rem.astype(bf16)
        rem = rem - piece.astype(f32)
        part = lax.dot_general(mb, piece, (((dims[0],), (dims[1],)), ((), ())), preferred_element_type=f32)
        acc = part if acc is None else acc + part
    return acc


@jax.custom_vjp
def _tri_cumsum(tri, g):
    return _split_dot(tri, g, (1, 0))


def _tri_cumsum_fwd(tri, g):
    return _tri_cumsum(tri, g), tri


def _tri_cumsum_bwd(tri, db):
    return jnp.zeros_like(tri), _split_dot(tri, db, (0, 0))


_tri_cumsum.defvjp(_tri_cumsum_fwd, _tri_cumsum_bwd)


def _gla_chunk(St, q, k, v, g, *, rev, scale, exact):
    C, DK = q.shape
    r = lax.broadcasted_iota(jnp.int32, (C, C), 0)
    c = lax.broadcasted_iota(jnp.int32, (C, C), 1)
    causal = (r <= c) if rev else (r >= c)
    b = _tri_cumsum(causal.astype(f32), g)
    qs = q * scale
    qe = qs * jnp.exp(b)
    inter = _bdot(qe, St, (1, 1))
    b_last = b[0:1] if rev else b[C - 1:C]
    kd = k * jnp.exp(b_last - b)
    St_new = St * jnp.exp(b_last) + _bdot(v, kd, (0, 0))
    if not exact:
        att = jnp.where(causal, _bdot(qe, k * jnp.exp(-b), (1, 1)), 0.0)
        return St_new, inter + _bdot(att, v, (1, 0))
    rr = lax.broadcasted_iota(jnp.int32, (SUB, SUB, DK), 0)
    cc = lax.broadcasted_iota(jnp.int32, (SUB, SUB, DK), 1)
    m3 = (rr <= cc) if rev else (rr >= cc)
    outs = []
    for i in range(C // SUB):
        lo, hi = i * SUB, (i + 1) * SUB
        bi, qi, ki, vi = b[lo:hi], qs[lo:hi], k[lo:hi], v[lo:hi]
        rel = bi[:, None, :] - bi[None, :, :]
        e = jnp.where(m3, jnp.exp(jnp.where(m3, rel, 0.0)), 0.0)
        att = jnp.sum(qi[:, None, :] * e * ki[None, :, :], axis=-1)
        acc = _bdot(att, vi, (1, 0))
        ref_row = b[hi - 1:hi] if rev else b[lo:lo + 1]
        prev = slice(hi, C) if rev else slice(0, lo)
        if (hi < C) if rev else (lo > 0):
            qn = qi * jnp.exp(bi - ref_row)
            ks = k[prev] * jnp.exp(ref_row - b[prev])
            acc = acc + _bdot(_bdot(qn, ks, (1, 1)), v[prev], (1, 0))
        outs.append(acc)
    return St_new, inter + jnp.concatenate(outs, axis=0)


def _mild_decay(la_ref):
    return jnp.min(la_ref[...]) >= -GLA_SAFE_DECAY / CHUNK


def _gla_specs(D, rev_blocks, row0, seq):
    DK, DV = D // (2 * HEADS), D // HEADS
    nblk = seq // GLA_ROWS
    rb0 = row0 // GLA_ROWS

    def blk(j):
        return (nblk - 1 - j) if rev_blocks else j

    return DK, DV, nblk, rb0, blk


def _gla_in_specs(D, rev, rows):
    QK = D // 2
    return [
        pl.BlockSpec((GLA_ROWS, QK), lambda b, j: (rows(b, j), 6 * D // QK)),
        pl.BlockSpec((GLA_ROWS, QK), lambda b, j: (rows(b, j), 6 * D // QK + 1)),
        pl.BlockSpec((GLA_ROWS, D), lambda b, j: (rows(b, j), 2)),
        pl.BlockSpec((GLA_ROWS, QK), lambda b, j: (rows(b, j), 1 if rev else 0)),
    ]


def _gla_fwd(p_all, la_all, s0, *, rev, row0, nb, seq, D, name, carry=None):
    DK, DV, nblk, rb0, blk = _gla_specs(D, rev, row0, seq)
    cpb = GLA_ROWS // CHUNK

    def rows(b, j):
        return rb0 + b * nblk + blk(j)

    in_specs = _gla_in_specs(D, rev, rows) + [pl.BlockSpec((1, HEADS, DV, DK), lambda b, j: (b, 0, 0, 0))]
    out_specs = [
        pl.BlockSpec((GLA_ROWS, D), lambda b, j: (b * nblk + blk(j), 0)),
        pl.BlockSpec((1, HEADS, cpb, DV, DK), lambda b, j: (b, 0, blk(j), 0, 0)),
        pl.BlockSpec((1, HEADS, DV, DK), lambda b, j: (b, 0, 0, 0)),
    ]
    out_shape = [
        jax.ShapeDtypeStruct((nb * seq, D), bf16),
        jax.ShapeDtypeStruct((nb, HEADS, seq // CHUNK, DV, DK), bf16),
        jax.ShapeDtypeStruct((nb, HEADS, DV, DK), f32),
    ]
    chunk = functools.partial(_gla_chunk, rev=rev, scale=DK ** -0.5)

    def body(q_ref, k_ref, v_ref, la_ref, s0_ref, o_ref, hist_ref, sfin_ref, st_ref):
        j = pl.program_id(1)

        @pl.when(j == 0)
        def _():
            st_ref[...] = s0_ref[0]

        def step(ci, exact):
            cc = (cpb - 1 - ci) if rev else ci
            sl = pl.ds(cc * CHUNK, CHUNK)
            for h in range(HEADS):
                kq, kv = pl.ds(h * DK, DK), pl.ds(h * DV, DV)
                St = st_ref[h]
                hist_ref[0, h, cc] = St.astype(bf16)
                St2, o = chunk(St, q_ref[sl, kq].astype(f32), k_ref[sl, kq].astype(f32), v_ref[sl, kv].astype(f32), la_ref[sl, kq],
                               exact=exact)
                o_ref[sl, kv] = o.astype(bf16)
                st_ref[h] = St2

        mild = _mild_decay(la_ref)
        for exact in (False, True):
            @pl.when(jnp.logical_not(mild) if exact else mild)
            def _(exact=exact):
                for ci in range(cpb):
                    step(ci, exact)

        @pl.when(j == nblk - 1)
        def _():
            sfin_ref[0] = st_ref[...]

    res, carried = _call(body, name=name, grid=(nb, nblk), in_specs=in_specs, out_specs=out_specs, out_shape=out_shape,
                         scratch_shapes=[pltpu.VMEM((HEADS, DV, DK), f32)], sem=("parallel", "arbitrary"),
                         args=(p_all, p_all, p_all, la_all, s0), carry=carry)
    return res if carry is None else (res, carried)


def _gla_bwd(p_all, la_all, hist, do, dsfin, *, rev, row0, nb, seq, D, name, add=None):
    DK, DV, nblk, rb0, blk = _gla_specs(D, not rev, row0, seq)
    cpb = GLA_ROWS // CHUNK
    QK = HEADS * DK
    has_do = do is not None

    def rows(b, j):
        return rb0 + b * nblk + blk(j)

    in_specs = _gla_in_specs(D, rev, rows) + [
        pl.BlockSpec((1, HEADS, cpb, DV, DK), lambda b, j: (b, 0, blk(j), 0, 0)),
        pl.BlockSpec((1, HEADS, DV, DK), lambda b, j: (b, 0, 0, 0)),
    ]
    args = [p_all, p_all, p_all, la_all, hist, dsfin]
    if has_do:
        in_specs.append(pl.BlockSpec((GLA_ROWS, D), lambda b, j: (b * nblk + blk(j), 0)))
        args.append(do)
    if add is not None:
        in_specs += [pl.BlockSpec((GLA_ROWS, t.shape[1]), lambda b, j: (b * nblk + blk(j), 0)) for t in add]
        args += list(add)
    gdt = f32 if add is None else bf16
    out_specs = [
        pl.BlockSpec((GLA_ROWS, QK), lambda b, j: (b * nblk + blk(j), 0)),
        pl.BlockSpec((GLA_ROWS, QK), lambda b, j: (b * nblk + blk(j), 0)),
        pl.BlockSpec((GLA_ROWS, D), lambda b, j: (b * nblk + blk(j), 0)),
        pl.BlockSpec((GLA_ROWS, QK), lambda b, j: (b * nblk + blk(j), 0)),
        pl.BlockSpec((1, HEADS, DV, DK), lambda b, j: (b, 0, 0, 0)),
    ]
    out_shape = [
        jax.ShapeDtypeStruct((nb * seq, QK), gdt), jax.ShapeDtypeStruct((nb * seq, QK), gdt),
        jax.ShapeDtypeStruct((nb * seq, D), gdt), jax.ShapeDtypeStruct((nb * seq, QK), f32),
        jax.ShapeDtypeStruct((nb, HEADS, DV, DK), f32),
    ]
    chunk = functools.partial(_gla_chunk, rev=rev, scale=DK ** -0.5)

    def body(*refs):
        refs = list(refs)
        q_ref, k_ref, v_ref, la_ref, hist_ref, dsfin_ref = refs[:6]
        do_ref = refs[6] if has_do else None
        add_refs = refs[6 + has_do:len(refs) - 6]
        dq_ref, dk_ref, dv_ref, dla_ref, ds0_ref, ds_ref = refs[len(refs) - 6:]
        j = pl.program_id(1)

        @pl.when(j == 0)
        def _():
            ds_ref[...] = dsfin_ref[0]

        def step(ci, exact):
            cc = ci if rev else (cpb - 1 - ci)
            sl = pl.ds(cc * CHUNK, CHUNK)
            for h in range(HEADS):
                kq, kv = pl.ds(h * DK, DK), pl.ds(h * DV, DV)
                prim = (hist_ref[0, h, cc].astype(f32), q_ref[sl, kq].astype(f32), k_ref[sl, kq].astype(f32), v_ref[sl, kv].astype(f32), la_ref[sl, kq])
                _, vjp = jax.vjp(functools.partial(chunk, exact=exact), *prim)
                d_o = do_ref[sl, kv].astype(f32) if has_do else jnp.zeros((CHUNK, DV), f32)
                dSt, dq, dk, dv, dg = vjp((ds_ref[h], d_o))
                if add is not None:
                    dq, dk, dv = dq + add_refs[0][sl, kq], dk + add_refs[1][sl, kq], dv + add_refs[2][sl, kv]
                dq_ref[sl, kq] = dq.astype(gdt)
                dk_ref[sl, kq] = dk.astype(gdt)
                dv_ref[sl, kv] = dv.astype(gdt)
                dla_ref[sl, kq] = dg
                ds_ref[h] = dSt

        mild = _mild_decay(la_ref)
        for exact in (False, True):
            @pl.when(jnp.logical_not(mild) if exact else mild)
            def _(exact=exact):
                for ci in range(cpb):
                    step(ci, exact)

        @pl.when(j == nblk - 1)
        def _():
            ds0_ref[0] = ds_ref[...]

    return _pc(body, name=name, grid=(nb, nblk), in_specs=in_specs, out_specs=out_specs, out_shape=out_shape,
               scratch_shapes=[pltpu.VMEM((HEADS, DV, DK), f32)], compiler_params=_params("parallel", "arbitrary"))(*args)


def _conv_fwd(p_all, dw_w, dw_b, *, B, L, D, name):
    ct = _pick(D, 256)
    nj = D // ct
    st = _pick(L, 128, 8)
    off = CONV_PAD - CONV_W // 2

    def body(a_ref, b_ref, w_ref, bias_ref, o_ref, zs_ref):
        _fill_shifted(zs_ref, L, lambda t0, n: a_ref[pl.ds(t0, n), :].astype(f32) * jax.nn.sigmoid(b_ref[pl.ds(t0, n), :].astype(f32)))
        for t0 in range(0, L, st):
            acc = jnp.zeros((st, ct), f32) + bias_ref[...]
            for k in range(CONV_W):
                acc = acc + w_ref[pl.ds(k, 1), :] * _window(zs_ref, t0 + k + off, st)
            o_ref[pl.ds(t0, st), :] = acc.astype(bf16)

    return _pc(
        body, name=name, grid=(B, nj),
        in_specs=[pl.BlockSpec((L, ct), lambda b, j: (b, j)), pl.BlockSpec((L, ct), lambda b, j: (b, nj + j)),
                  pl.BlockSpec((CONV_W, ct), lambda b, j: (0, j)), pl.BlockSpec((1, ct), lambda b, j: (0, j))],
        out_specs=pl.BlockSpec((L, ct), lambda b, j: (b, j)), out_shape=jax.ShapeDtypeStruct((B * L, D), bf16),
        scratch_shapes=[pltpu.VMEM((SUBLANES, L + 2 * CONV_PAD, ct), f32)], compiler_params=_params("parallel", "parallel"),
    )(p_all, p_all, dw_w, dw_b)


def _fill_shifted(zs_ref, L, rows):
    lp = L + 2 * CONV_PAD
    ct = zs_ref.shape[2]
    step = 256
    zs_ref[0, pl.ds(0, CONV_PAD), :] = jnp.zeros((CONV_PAD, ct), f32)
    zs_ref[0, pl.ds(CONV_PAD + L, CONV_PAD), :] = jnp.zeros((CONV_PAD, ct), f32)
    for t0 in range(0, L, step):
        n = min(step, L - t0)
        zs_ref[0, pl.ds(CONV_PAD + t0, n), :] = rows(t0, n)
    for r in range(1, SUBLANES):
        for i0 in range(0, lp - SUBLANES, step):
            n = min(step, lp - SUBLANES - i0)
            zs_ref[r, pl.ds(i0, n), :] = zs_ref[0, pl.ds(i0 + r, n), :]


def _window(zs_ref, start, n):
    r = start % SUBLANES
    return zs_ref[r, pl.ds(start - r, n), :]


def _conv_bwd(p_all, dcz, dw_w, *, B, L, D, name, carry=None):
    ct = _pick(D, 128)
    nj = D // ct
    st = _pick(L, 256, 8)
    half = CONV_W // 2

    def body(a_ref, b_ref, dcz_ref, w_ref, da_ref, db_ref, ddw_ref, zs_ref, ds_ref):
        bi = pl.program_id(1)
        _fill_shifted(zs_ref, L, lambda t0, n: a_ref[pl.ds(t0, n), :].astype(f32) * jax.nn.sigmoid(b_ref[pl.ds(t0, n), :].astype(f32)))
        _fill_shifted(ds_ref, L, lambda t0, n: dcz_ref[pl.ds(t0, n), :].astype(f32))

        @pl.when(bi == 0)
        def _():
            ddw_ref[...] = jnp.zeros_like(ddw_ref)

        for t0 in range(0, L, st):
            acc = jnp.zeros((st, ct), f32)
            for k in range(CONV_W):
                acc = acc + w_ref[pl.ds(k, 1), :] * _window(ds_ref, t0 + CONV_PAD + half - k, st)
            a_t = a_ref[pl.ds(t0, st), :].astype(f32)
            sg_t = jax.nn.sigmoid(b_ref[pl.ds(t0, st), :].astype(f32))
            da_ref[pl.ds(t0, st), :] = (acc * sg_t).astype(bf16)
            db_ref[pl.ds(t0, st), :] = (acc * a_t * sg_t * (1.0 - sg_t)).astype(bf16)

        parts = [jnp.zeros((SUBLANES, ct), f32) for _ in range(CONV_W)]
        sw = _pick(L, 64, SUBLANES)
        for t0 in range(0, L, sw):
            dout = dcz_ref[pl.ds(t0, sw), :].astype(f32)
            for k in range(CONV_W):
                prod = dout * _window(zs_ref, t0 + k + CONV_PAD - half, sw)
                for i in range(0, sw, SUBLANES):
                    parts[k] = parts[k] + prod[i:i + SUBLANES]
        for k in range(CONV_W):
            ddw_ref[pl.ds(k, 1), :] += jnp.sum(parts[k], axis=0, keepdims=True)

    res, carried = _call(
        body, name=name, grid=(nj, B),
        in_specs=[pl.BlockSpec((L, ct), lambda j, b: (b, j)), pl.BlockSpec((L, ct), lambda j, b: (b, nj + j)),
                  pl.BlockSpec((L, ct), lambda j, b: (b, j)), pl.BlockSpec((CONV_W, ct), lambda j, b: (0, j))],
        out_specs=[pl.BlockSpec((L, ct), lambda j, b: (b, j)), pl.BlockSpec((L, ct), lambda j, b: (b, j)),
                   pl.BlockSpec((2 * CONV_PAD, ct), lambda j, b: (0, j))],
        out_shape=[jax.ShapeDtypeStruct((B * L, D), bf16), jax.ShapeDtypeStruct((B * L, D), bf16),
                   jax.ShapeDtypeStruct((2 * CONV_PAD, D), f32)],
        scratch_shapes=[pltpu.VMEM((SUBLANES, L + 2 * CONV_PAD, ct), f32), pltpu.VMEM((SUBLANES, L + 2 * CONV_PAD, ct), f32)],
        sem=("parallel", "arbitrary"), args=(p_all, p_all, dcz, dw_w), carry=carry)
    return res if carry is None else (res, carried)


def _exchange(arrs, scatter, name):
    ex = _Exchange(arrs, scatter)
    n = ex.n

    def body(*refs):
        ex.start(refs[:n], refs[n:2 * n], refs[2 * n:])
        ex.finish(refs[:n], refs[n:2 * n], refs[2 * n:])

    res = _pc(body, name=name, in_specs=ex.specs, out_specs=ex.specs, out_shape=ex.out_shape, scratch_shapes=ex.scratch)(*arrs)
    return list(res)


class _Exchange:
    def __init__(self, arrs, scatter):
        self.arrs, self.scatter, self.n = list(arrs), scatter, len(arrs)
        self.out_shape = [jax.ShapeDtypeStruct(((N_DEV,) + a.shape[1:]) if scatter else ((N_DEV,) + a.shape), a.dtype) for a in arrs]
        self.specs = [pl.BlockSpec(memory_space=pl.ANY)] * self.n
        self.scratch = [pltpu.SemaphoreType.DMA((self.n, N_DEV - 1)), pltpu.SemaphoreType.DMA((self.n, N_DEV - 1)),
                        pltpu.SemaphoreType.DMA((self.n,))]

    def _copies(self, ins, outs, sems, landing):
        send_sems, recv_sems, local_sems = sems
        me = 4 * lax.axis_index("x") + 2 * lax.axis_index("y") + lax.axis_index("c")
        if landing:
            local = []
        else:
            local = [pltpu.make_async_copy(ins[a].at[me] if self.scatter else ins[a], outs[a].at[me], local_sems.at[a]) for a in range(self.n)]
        remote = []
        for k in range(1, N_DEV):
            p = (me + (N_DEV - k if landing else k)) % N_DEV
            for a in range(self.n):
                remote.append(pltpu.make_async_remote_copy(
                    src_ref=ins[a].at[p] if self.scatter else ins[a], dst_ref=outs[a].at[p if landing else me],
                    send_sem=send_sems.at[a, k - 1], recv_sem=recv_sems.at[a, k - 1],
                    device_id=(p // 4, (p // 2) % 2, p % 2), device_id_type=MESH))
        return local, remote

    def _gather_plan(self, ins, outs, sems):
        send_sems, recv_sems, local_sems = sems
        x, y, c = lax.axis_index("x"), lax.axis_index("y"), lax.axis_index("c")
        chips = [(1 - x, y), (x, 1 - y), (1 - x, 1 - y)]

        def blk(px, py, pc):
            return 4 * px + 2 * py + pc

        def copy(a, k, block, to, own):
            return pltpu.make_async_remote_copy(
                src_ref=ins[a] if own else outs[a].at[block], dst_ref=outs[a].at[block],
                send_sem=send_sems.at[a, k], recv_sem=recv_sems.at[a, k], device_id=to, device_id_type=MESH)

        me = blk(x, y, c)
        local = [pltpu.make_async_copy(ins[a], outs[a].at[me], local_sems.at[a]) for a in range(self.n)]
        return local, copy, me, (x, y, 1 - c), chips, blk, c

    def start(self, ins, outs, sems):
        if self.scatter:
            local, sends = self._copies(ins, outs, sems, False)
            for cp in local + sends:
                cp.start()
            return
        local, copy, me, sibling, chips, _, c = self._gather_plan(ins, outs, sems)
        for cp in local:
            cp.start()
        for a in range(self.n):
            copy(a, 0, me, sibling, True).start()
            for j, chip in enumerate(chips):
                copy(a, 1 + j, me, (*chip, c), True).start()

    def finish(self, ins, outs, sems):
        if self.scatter:
            for cp in self._copies(ins, outs, sems, True)[1]:
                cp.wait_recv()
            local, sends = self._copies(ins, outs, sems, False)
            for cp in sends:
                cp.wait_send()
            for cp in local:
                cp.wait()
            return
        local, copy, me, sibling, chips, blk, c = self._gather_plan(ins, outs, sems)
        for j, chip in enumerate(chips):
            for a in range(self.n):
                copy(a, 1 + j, blk(*chip, c), sibling, True).wait_recv()
                copy(a, 4 + j, blk(*chip, c), sibling, False).start()
        for a in range(self.n):
            copy(a, 0, blk(*sibling), sibling, True).wait_recv()
            for j, chip in enumerate(chips):
                copy(a, 4 + j, blk(*chip, 1 - c), sibling, False).wait_recv()
        for a in range(self.n):
            copy(a, 0, me, sibling, True).wait_send()
            for j, chip in enumerate(chips):
                copy(a, 1 + j, me, (*chip, c), True).wait_send()
                copy(a, 4 + j, blk(*chip, c), sibling, False).wait_send()
        for cp in local:
            cp.wait()


def _carried(inner, n_in, n_out, grid, ex):
    n = ex.n

    def body(*refs):
        own_in, c_in = refs[:n_in], refs[n_in:n_in + n]
        own_out, c_out = refs[n_in + n:n_in + n + n_out], refs[n_in + n + n_out:n_in + 2 * n + n_out]
        rest = refs[n_in + 2 * n + n_out:]
        own_scr, sems = rest[:len(rest) - 3], rest[len(rest) - 3:]
        pids = [pl.program_id(d) for d in range(len(grid))]
        first = functools.reduce(jnp.logical_and, [p == 0 for p in pids])
        last = functools.reduce(jnp.logical_and, [p == g - 1 for p, g in zip(pids, grid)])

        @pl.when(first)
        def _():
            ex.start(c_in, c_out, sems)

        inner(*own_in, *own_out, *own_scr)

        @pl.when(last)
        def _():
            ex.finish(c_in, c_out, sems)

    return body


def _call(inner, *, name, grid, in_specs, out_specs, out_shape, scratch_shapes, sem, args, carry=None):
    if carry is None:
        res = _pc(inner, name=name, grid=grid, in_specs=in_specs, out_specs=out_specs, out_shape=out_shape,
                  scratch_shapes=scratch_shapes, compiler_params=_params(*sem))(*args)
        return list(res), None
    ex = _Exchange(*carry)
    res = _pc(_carried(inner, len(in_specs), len(out_specs), grid, ex), name=name, grid=grid,
              in_specs=list(in_specs) + ex.specs, out_specs=list(out_specs) + ex.specs, out_shape=list(out_shape) + ex.out_shape,
              scratch_shapes=list(scratch_shapes) + ex.scratch, compiler_params=_params(*(["arbitrary"] * len(grid))))(*args, *ex.arrs)
    res = list(res)
    return res[:len(out_specs)], res[len(out_specs):]


def _mod_fwd(c_all, c_ctx, w_loc, b_loc, name):
    nr, D = c_all.shape
    nc = w_loc.shape[1]

    def body(c_ref, cc_ref, w_ref, b_ref, o_ref):
        a = jnp.concatenate([c_ref[...], jnp.broadcast_to(cc_ref[...], (8, D))], axis=0)
        s = jax.nn.silu(a).astype(bf16)
        o_ref[...] = jnp.dot(s, w_ref[...].astype(bf16), preferred_element_type=f32) + b_ref[...]

    return _pc(body, name=name, out_shape=jax.ShapeDtypeStruct((nr + 8, nc), f32), compiler_params=_params())(c_all, c_ctx, w_loc, b_loc)


def _mod_bwd(c_all, c_ctx, w_loc, dmx_loc, dmc_loc, name):
    nr, D = c_all.shape
    nc = w_loc.shape[1]

    def body(c_ref, cc_ref, w_ref, dmx_ref, dmc_ref, gw_ref, gc_ref):
        cc = cc_ref[...]
        a = jnp.concatenate([c_ref[...], jnp.broadcast_to(cc, (N_DEV, D))], axis=0)
        s = jax.nn.silu(a).astype(bf16)
        g = jnp.concatenate([dmx_ref[...], dmc_ref[...]], axis=0).astype(bf16)
        gw_ref[...] = lax.dot_general(s, g, (((0,), (0,)), ((), ())), preferred_element_type=f32)
        dmc = jnp.sum(dmc_ref[...], axis=0, keepdims=True)
        ds = lax.dot_general(jnp.broadcast_to(dmc, (8, nc)).astype(bf16), w_ref[...].astype(bf16), (((1,), (1,)), ((), ())),
                             preferred_element_type=f32)[0:1]
        sg = jax.nn.sigmoid(cc)
        gc_ref[...] = ds * (sg * (1.0 + cc * (1.0 - sg)))

    return _pc(body, name=name, out_shape=[jax.ShapeDtypeStruct((D, nc), f32), jax.ShapeDtypeStruct((1, D), f32)],
               compiler_params=_params())(c_all, c_ctx, w_loc, dmx_loc, dmc_loc)


def _adamw_math(w, g, m, v):
    m2 = ADAM_B1 * m + (1.0 - ADAM_B1) * g
    v2 = ADAM_B2 * v + (1.0 - ADAM_B2) * jnp.square(g)
    m_hat = m2 / (1.0 - ADAM_B1 ** ADAM_STEP)
    v_hat = v2 / (1.0 - ADAM_B2 ** ADAM_STEP)
    delta = -ADAM_LR * (m_hat / (jnp.sqrt(v_hat) + ADAM_EPS) + ADAM_WD * w)
    return delta, m2, v2


def _adamw_many(params, name):
    n = len(params)

    def body(*refs):
        ins, outs = refs[:4 * n], refs[4 * n:]
        for i in range(n):
            w, m, v, g = (ins[4 * i + k][...] for k in range(4))
            d, m2, v2 = _adamw_math(w, g, m, v)
            outs[3 * i][...] = d
            outs[3 * i + 1][...] = m2
            outs[3 * i + 2][...] = v2

    res = _pc(body, name=name, out_shape=[jax.ShapeDtypeStruct(p[0].shape, f32) for p in params for _ in range(3)],
              compiler_params=_params())(*[a for p in params for a in p])
    return [tuple(res[3 * i:3 * i + 3]) for i in range(n)]


def _adamw(w, m, v, g, name, partials, carry=None):
    r, cdim = w.shape
    tr = _pick(r, 256, 8)

    def body(w_ref, m_ref, v_ref, g_ref, og_ref, od_ref, om_ref, ov_ref):
        if partials:
            g = g_ref[0].astype(f32)
            for s in range(1, N_DEV):
                g = g + g_ref[s].astype(f32)
        else:
            g = g_ref[...]
        d, m2, v2 = _adamw_math(w_ref[...], g, m_ref[...], v_ref[...])
        og_ref[...] = g
        od_ref[...] = d
        om_ref[...] = m2
        ov_ref[...] = v2

    blk = pl.BlockSpec((tr, cdim), lambda i: (i, 0))
    g_spec = pl.BlockSpec((N_DEV, tr, cdim), lambda i: (0, i, 0)) if partials else blk
    res, carried = _call(body, name=name, grid=(r // tr,), in_specs=[blk, blk, blk, g_spec], out_specs=[blk] * 4,
                         out_shape=[jax.ShapeDtypeStruct((r, cdim), f32)] * 4, scratch_shapes=[], sem=("parallel",),
                         args=(w, m, v, g), carry=carry)
    return res if carry is None else (res, carried)


def _sum_sources(parts, name):
    def body(*refs):
        for i_ref, o_ref in zip(refs[:len(parts)], refs[len(parts):]):
            acc = i_ref[0]
            for s in range(1, i_ref.shape[0]):
                acc = acc + i_ref[s]
            o_ref[...] = acc

    return list(_pc(body, name=name, out_shape=[jax.ShapeDtypeStruct(p.shape[1:], f32) for p in parts],
                    compiler_params=_params())(*parts))


def kernel(x, c, ctx, c_ctx, w_mod, b_mod, g_ffn1, w1_gu, w1_down, g_mix, w_in, dw_weight, dw_bias, conv_ln_g, conv_ln_b, w_conv_out, w_alpha_f, b_alpha_f, w_alpha_b, b_alpha_b, gla_norm_g, w_gla_out, w_out, g_ffn2, w2_gu, w2_down, g_final, loss_target, m_c_ctx, m_w_mod, m_b_mod, m_g_ffn1, m_w1_gu, m_w1_down, m_g_mix, m_w_in, m_dw_weight, m_dw_bias, m_conv_ln_g, m_conv_ln_b, m_w_conv_out, m_w_alpha_f, m_b_alpha_f, m_w_alpha_b, m_b_alpha_b, m_gla_norm_g, m_w_gla_out, m_w_out, m_g_ffn2, m_w2_gu, m_w2_down, m_g_final, v_c_ctx, v_w_mod, v_b_mod, v_g_ffn1, v_w1_gu, v_w1_down, v_g_mix, v_w_in, v_dw_weight, v_dw_bias, v_conv_ln_g, v_conv_ln_b, v_w_conv_out, v_w_alpha_f, v_b_alpha_f, v_w_alpha_b, v_b_alpha_b, v_gla_norm_g, v_w_gla_out, v_w_out, v_g_ffn2, v_w2_gu, v_w2_down, v_g_final):
    B, L, D = x.shape
    Lc = ctx.shape[1]
    T, Tc = B * L, B * Lc
    Tall = T + Tc
    F = w1_down.shape[1] * N_DEV
    DK, DV = D // (2 * HEADS), D // HEADS
    QK = HEADS * DK
    PW = 7 * D + LR_PAD
    tm = ROW_TILE
    tpe = L // tm
    nx, nall = T // tm, Tall // tm
    me = 4 * lax.axis_index("x") + 2 * lax.axis_index("y") + lax.axis_index("c")

    rw_all = dict(tm=tm, n_tiles=nall, tpe=tpe, nx_tiles=nx, n_ex=B + 1)
    rw_x = dict(tm=tm, n_tiles=nx, tpe=tpe, nx_tiles=nx, n_ex=B)
    rw_all2, rw_x2 = rw_all, rw_x
    mb = 2 if (L % (2 * tm) == 0 and Tc % (2 * tm) == 0) else 1
    rw_all_b = dict(tm=mb * tm, n_tiles=nall // mb, tpe=tpe // mb, nx_tiles=nx // mb, n_ex=B + 1)
    rw_x_b = dict(tm=mb * tm, n_tiles=nx // mb, tpe=tpe // mb, nx_tiles=nx // mb, n_ex=B)

    dww_g, waf_g, wab_g, c_g = _exchange([dw_weight[0], w_alpha_f[0], w_alpha_b[0], c], False, "gather_first")

    def cols(gat):
        return jnp.transpose(gat, (1, 0, 2)).reshape(gat.shape[1], N_DEV * gat.shape[2])

    def rows_(gat):
        return gat.reshape(N_DEV * gat.shape[1], gat.shape[2])

    dww = cols(dww_g)
    WA = jnp.zeros((LR_PAD, 2 * QK), f32).at[:LOWRANK, :QK].set(cols(waf_g)).at[LOWRANK:2 * LOWRANK, QK:].set(cols(wab_g)).astype(bf16)
    BA = jnp.concatenate([b_alpha_f, b_alpha_b], axis=1)
    c_all = c_g.reshape(N_DEV * B, D)
    c_ctx2 = c_ctx.reshape(1, D)

    ncm = w_mod.shape[2]
    b_mod_loc = lax.dynamic_slice(b_mod, (0, me * ncm), (1, ncm))
    mod_loc = _mod_fwd(c_all, c_ctx2, w_mod[0], b_mod_loc, "mod_fwd")
    (mod_g,) = _exchange([mod_loc], False, "gather_mod")
    mod_full = cols(mod_g)
    mod_tab = jnp.concatenate([lax.dynamic_slice(mod_full, (me * B, 0), (B, N_MOD * D)), mod_full[N_DEV * B:N_DEV * B + 1]], axis=0)
    mods = [mod_tab[:, i * D:(i + 1) * D].reshape(B + 1, 1, D) for i in range(N_MOD)]
    mods_x = [mm[:B] for mm in mods]

    x_lat, x_ctx = (x.reshape(T, D), D, 0, True), (ctx.reshape(Tc, D), D, 0, "c")

    def f_ffn_in(tok, ex, sh):
        return [_rms_mod(tok[0] + tok[1], sh[0], ex[0], ex[1])], [], []

    (u1,), (w1gu_g,) = _rowwise(f_ffn_in, name="ffn1_in", tok_in=[x_lat, x_ctx], ex_in=[mods[0], mods[1]], sh_in=[g_ffn1],
                                tok_out=[(D, bf16)], carry=([w1_gu[0].astype(bf16)], False), **rw_all_b)
    W1gu = cols(w1gu_g)
    (gu1, h1), (w1d_g, win_g) = _ffn_up(u1, W1gu, "ffn1_up", carry=([w1_down[0].astype(bf16), w_in[0].astype(bf16)], False))
    W1d = rows_(w1d_g)
    lr2 = 2 * LOWRANK
    segs = [(0, 2 * D, 0), (2 * D, 2 * D + QK, 6 * D), (2 * D + QK, 3 * D, 6 * D + QK), (3 * D, 4 * D, 2 * D), (4 * D, 5 * D, 3 * D),
            (5 * D, 5 * D + lr2, 7 * D), (5 * D + lr2, 6 * D + lr2, 4 * D), (6 * D + lr2, 7 * D + lr2, 5 * D)]
    wc = w_in.shape[2]
    win_parts = []
    for lo, hi, _ in sorted(segs, key=lambda t: t[2]):
        for d in range(N_DEV):
            a0, a1 = max(lo, d * wc), min(hi, (d + 1) * wc)
            if a0 < a1:
                win_parts.append(win_g[d][:, a0 - d * wc:a1 - d * wc])
    Win = jnp.concatenate(win_parts + [jnp.zeros((D, LR_PAD - lr2), bf16)], axis=1)

    def nn(a, w):
        return jnp.dot(a.astype(bf16), w, preferred_element_type=f32)

    def nt(a, w):
        return lax.dot_general(a.astype(bf16), w, (((1,), (1,)), ((), ())), preferred_element_type=f32)

    def mix_in(xv, fv, gate, sh, sc, g):
        x1 = xv + 0.5 * gate * fv
        return x1, _rms_mod(x1, g, sh, sc)

    def f_mix_in(tok, ex, sh):
        f1v = nn(tok[2], sh[1])
        return list(mix_in(tok[0] + tok[1], f1v, ex[0], ex[1], ex[2], sh[0])) + [f1v], [], []

    x1, um, f1 = _rowwise(f_mix_in, name="ffn1_down_mix_in", tok_in=[x_lat, x_ctx, (h1, F, 0, False)],
                          ex_in=[mods[2], mods[3], mods[4]], sh_in=[g_mix, W1d], tok_out=[(D, f32), (D, bf16), (D, bf16)], **rw_all2)
    p_all, (wco_g, wgo_g, wo_g, w2gu_g) = _matmul(
        um, Win, "nn", bf16, "in_proj", tm_cap=1024, tn_cap=2432,
        carry=([w_conv_out[0].astype(bf16), w_gla_out[0].astype(bf16), w_out[0].astype(bf16), w2_gu[0].astype(bf16)], False))
    Wco, Wgo, Wo, W2gu = rows_(wco_g), rows_(wgo_g), rows_(wo_g), cols(w2gu_g)

    def log_decay(lr, wa, ba):
        z = _bdot(lr, wa, (1, 0)) + ba
        return _log_sigmoid(z) / TAU

    def f_decay(tok, ex, sh):
        return [log_decay(tok[0], sh[0], sh[1])], [], []

    lr_blk = (p_all, LR_PAD, 7 * D // LR_PAD, False)
    (la_all,) = _rowwise(f_decay, name="log_decay", tok_in=[lr_blk], sh_in=[WA, BA], tok_out=[(2 * QK, f32)], **rw_all_b)

    zeros_s = jnp.zeros((B, HEADS, DV, DK), f32)
    gla_c = dict(row0=T, nb=B, seq=Lc, D=D)
    gla_x = dict(row0=0, nb=B, seq=L, D=D)
    _, hist_cf, s_f = _gla_fwd(p_all, la_all, zeros_s, rev=False, name="gla_ctx_f", **gla_c)
    _, hist_cb, s_b = _gla_fwd(p_all, la_all, zeros_s, rev=True, name="gla_ctx_b", **gla_c)
    (o_f, hist_f, _), (w2d_g,) = _gla_fwd(p_all, la_all, s_f, rev=False, name="gla_x_f", carry=([w2_down[0].astype(bf16)], False), **gla_x)
    W2d = rows_(w2d_g)
    o_b, hist_b, _ = _gla_fwd(p_all, la_all, s_b, rev=True, name="gla_x_b", **gla_x)

    cz = _conv_fwd(p_all, dww, dw_bias, B=B, L=L, D=D, name="conv_fwd")

    def ln_silu(z, g, b):
        mu = jnp.mean(z, axis=-1, keepdims=True)
        var = jnp.mean(jnp.square(z - mu), axis=-1, keepdims=True)
        return jax.nn.silu((z - mu) * lax.rsqrt(var + EPS) * g + b)

    def f_ln(tok, ex, sh):
        zc = ln_silu(tok[0].astype(f32), sh[0], sh[1])
        return [zc, nn(zc, sh[2])], [], []

    zc, yc = _rowwise(f_ln, name="conv_ln_out", tok_in=[(cz, D, 0, False)], sh_in=[conv_ln_g, conv_ln_b, Wco],
                      tok_out=[(D, bf16), (D, bf16)], **rw_x_b)

    def gla_out(of, ob, og, gn):
        return _head_rms(of.astype(f32) + ob.astype(f32), DV) * gn * jax.nn.silu(og.astype(f32))

    def f_gla_out(tok, ex, sh):
        og2 = gla_out(tok[0], tok[1], tok[2], sh[0])
        return [og2, nn(og2, sh[1])], [], []

    og_blk = (p_all, D, 3, False)
    og2, yg = _rowwise(f_gla_out, name="gla_norm_out", tok_in=[(o_f, D, 0, False), (o_b, D, 0, False), og_blk], sh_in=[gla_norm_g, Wgo],
                       tok_out=[(D, bf16), (D, bf16)], **rw_x_b)

    def merge(ga, gb, ycv, ygv):
        return jax.nn.sigmoid(ga.astype(f32)) * ycv.astype(f32) + jax.nn.sigmoid(gb.astype(f32)) * ygv.astype(f32)

    def f_merge(tok, ex, sh):
        mg = merge(*tok)
        return [mg, nn(mg, sh[0])], [], []

    ga_blk, gb_blk = (p_all, D, 4, False), (p_all, D, 5, False)
    mg, mix = _rowwise(f_merge, name="merge_mix_out", tok_in=[ga_blk, gb_blk, (yc, D, 0, False), (yg, D, 0, False)], sh_in=[Wo],
                       tok_out=[(D, bf16), (D, f32)], **rw_x_b)

    def ffn2_in(x1v, mixv, g5, sh, sc, g):
        x2 = x1v + g5 * mixv
        return x2, _rms_mod(x2, g, sh, sc)

    def f_ffn2_in(tok, ex, sh):
        return list(ffn2_in(tok[0], tok[1], ex[0], ex[1], ex[2], sh[0])), [], []

    x2, u2 = _rowwise(f_ffn2_in, name="ffn2_in", tok_in=[(x1, D, 0, False), (mix, D, 0, False)], ex_in=[mods_x[5], mods_x[6], mods_x[7]],
                      sh_in=[g_ffn2], tok_out=[(D, f32), (D, bf16)], **rw_x_b)
    gu2, h2 = _ffn_up(u2, W2gu, "ffn2_up")

    gf2 = g_final.reshape(1, D)

    def head_loss(x2v, f2v, g8, gf, tgt):
        x3 = x2v + 0.5 * g8 * f2v
        y = x3 * lax.rsqrt(jnp.mean(x3 * x3, axis=-1, keepdims=True) + EPS) * gf
        return 0.5 * jnp.sum(jnp.mean(jnp.square(y - tgt), axis=-1))

    def f_head(tok, ex, sh):
        loss, vjp = jax.vjp(lambda a, b_, c_, d_: head_loss(a, b_, c_, d_, tok[2]), tok[0], nn(tok[1], sh[1]), ex[0], sh[0])
        dx3, df2, dg8, dgf = vjp(jnp.ones((), f32))
        return [dx3, df2], [dg8], [dgf, jnp.broadcast_to(loss.reshape(1, 1), (1, 128))]

    dx3, df2, dg8, dgf, loss_p = _rowwise(
        f_head, name="ffn2_down_head", tok_in=[(x2, D, 0, False), (h2, F, 0, False), (loss_target.reshape(T, D), D, 0, False)],
        ex_in=[mods_x[8]], sh_in=[gf2, W2d], tok_out=[(D, f32), (D, bf16)], ex_out=[D], gl_out=[(1, D), (1, 128)], **rw_x2)

    dgu2 = _ffn_down_dx(df2, W2d, gu2, "ffn2_down_dx")
    gW2d = _matmul(h2, df2, "tn", f32, "ffn2_down_dw", tm_cap=1408)
    du2 = _matmul(dgu2, W2gu, "nt", bf16, "ffn2_up_dx", halves="a")
    gW2gu = _matmul(u2, dgu2, "tn", f32, "ffn2_up_dw", halves="b")

    def f_ffn2_in_bwd(tok, ex, sh):
        _, vjp = jax.vjp(ffn2_in, tok[0], tok[1], ex[0], ex[1], ex[2], sh[0])
        dx2, dmix, dg5, dsh, dsc, dg = vjp((tok[3], tok[2].astype(f32)))
        return [dx2, dmix], [dg5, dsh, dsc], [dg]

    dx2, dmix, dg5, dsh6, dsc7, dg_ffn2 = _rowwise(
        f_ffn2_in_bwd, name="ffn2_in_bwd", tok_in=[(x1, D, 0, False), (mix, D, 0, False), (du2, D, 0, False), (dx3, D, 0, False)],
        ex_in=[mods_x[5], mods_x[6], mods_x[7]], sh_in=[g_ffn2], tok_out=[(D, f32), (D, bf16)], ex_out=[D, D, D], gl_out=[(1, D)], **rw_x)

    gWo = _matmul(mg, dmix, "tn", f32, "mix_out_dw")

    def f_merge_bwd(tok, ex, sh):
        _, vjp = jax.vjp(merge, *[t.astype(f32) for t in tok[:4]])
        dga, dgb, dyc, dyg = vjp(nt(tok[4], sh[0]))
        return [dga, dgb, dyc, dyg], [], []

    dga, dgb, dyc, dyg = _rowwise(f_merge_bwd, name="mix_out_merge_bwd",
                                  tok_in=[ga_blk, gb_blk, (yc, D, 0, False), (yg, D, 0, False), (dmix, D, 0, False)], sh_in=[Wo],
                                  tok_out=[(D, bf16)] * 4, **rw_x_b)
    gWco = _matmul(zc, dyc, "tn", f32, "conv_out_dw")
    gWgo = _matmul(og2, dyg, "tn", f32, "gla_out_dw")

    def f_ln_bwd(tok, ex, sh):
        _, vjp = jax.vjp(ln_silu, tok[0].astype(f32), sh[0], sh[1])
        dcz, dg, db = vjp(nt(tok[1], sh[2]))
        return [dcz], [], [dg, db, jnp.sum(dcz, axis=0, keepdims=True)]

    dcz, g_ln_g, g_ln_b, g_dwb = _rowwise(f_ln_bwd, name="conv_out_ln_bwd", tok_in=[(cz, D, 0, False), (dyc, D, 0, False)],
                                          sh_in=[conv_ln_g, conv_ln_b, Wco], tok_out=[(D, bf16)], gl_out=[(1, D)] * 3, **rw_x_b)
    def col_shards(g):
        return jnp.transpose(g.reshape(g.shape[0], N_DEV, g.shape[1] // N_DEV), (1, 0, 2)).astype(bf16)

    def row_shards(g):
        return g.reshape(N_DEV, g.shape[0] // N_DEV, g.shape[1]).astype(bf16)

    (dca, dcb, g_dww), (r_w2d, r_w2gu, r_wo, r_wco, r_wgo) = _conv_bwd(
        p_all, dcz, dww, B=B, L=L, D=D, name="conv_bwd",
        carry=([row_shards(gW2d), col_shards(gW2gu), row_shards(gWo), row_shards(gWco), row_shards(gWgo)], True))

    def f_gla_out_bwd(tok, ex, sh):
        _, vjp = jax.vjp(gla_out, tok[0].astype(f32), tok[1].astype(f32), tok[2].astype(f32), sh[0])
        dof, _, dog, dgn = vjp(nt(tok[3], sh[1]))
        return [dof, dog], [], [dgn]

    d_o, dog, g_gn = _rowwise(f_gla_out_bwd, name="gla_out_norm_bwd",
                              tok_in=[(o_f, D, 0, False), (o_b, D, 0, False), og_blk, (dyg, D, 0, False)], sh_in=[gla_norm_g, Wgo],
                              tok_out=[(D, bf16), (D, bf16)], gl_out=[(1, D)], **rw_x_b)

    dq_f, dk_f, dv_f, dla_f, ds_f = _gla_bwd(p_all, la_all, hist_f, d_o, zeros_s, rev=False, name="gla_x_f_bwd", **gla_x)
    dq, dk, dv, dla_b, ds_b = _gla_bwd(p_all, la_all, hist_b, d_o, zeros_s, rev=True, name="gla_x_b_bwd", add=(dq_f, dk_f, dv_f), **gla_x)
    dq_cf, dk_cf, dv_cf, dla_cf, _ = _gla_bwd(p_all, la_all, hist_cf, None, ds_f, rev=False, name="gla_ctx_f_bwd", **gla_c)
    _, dk_c, dv_c, dla_cb, _ = _gla_bwd(p_all, la_all, hist_cb, None, ds_b, rev=True, name="gla_ctx_b_bwd", add=(dq_cf, dk_cf, dv_cf), **gla_c)

    dla_all = jnp.concatenate([jnp.concatenate([dla_f, dla_b], axis=1), jnp.concatenate([dla_cf, dla_cb], axis=1)], axis=0)

    def f_decay_bwd(tok, ex, sh):
        _, vjp = jax.vjp(log_decay, tok[0].astype(f32), sh[0].astype(f32), sh[1])
        dlr, dwa, dba = vjp(tok[1])
        return [dlr], [], [dwa, dba]

    dlr, g_WA, g_BA = _rowwise(f_decay_bwd, name="log_decay_bwd", tok_in=[lr_blk, (dla_all, 2 * QK, 0, False)], sh_in=[WA, BA],
                               tok_out=[(LR_PAD, bf16)], gl_out=[(LR_PAD, 2 * QK), (1, 2 * QK)], **rw_all_b)

    zc_ = functools.partial(jnp.zeros, dtype=bf16)
    dp_x = jnp.concatenate([dca, dcb, dv, dog, dga, dgb, dq, dk, dlr[:T]], axis=1)
    dp_c = jnp.concatenate([zc_((Tc, 2 * D)), dv_c, zc_((Tc, 3 * D)), zc_((Tc, QK)), dk_c, dlr[T:]], axis=1)
    dp_all = jnp.concatenate([dp_x, dp_c], axis=0)
    gWin_p = _matmul(um, dp_all, "tn", f32, "in_proj_dw", tm_cap=512, tn_cap=2432)
    gwin_shards = []
    for d in range(N_DEV):
        parts = []
        for lo, hi, po in segs:
            a0, a1 = max(lo, d * wc), min(hi, (d + 1) * wc)
            if a0 < a1:
                parts.append(gWin_p[:, po + a0 - lo:po + a1 - lo])
        gwin_shards.append(jnp.concatenate(parts, axis=1))
    dum, (r_win,) = _matmul(dp_all, Win, "nt", bf16, "in_proj_dx", tk_cap=2432, carry=([jnp.stack(gwin_shards).astype(bf16)], True))

    def f_mix_in_bwd(tok, ex, sh):
        _, vjp = jax.vjp(mix_in, tok[0] + tok[1], tok[2].astype(f32), ex[0], ex[1], ex[2], sh[0])
        dx1, df1, dgate, dsh, dsc, dg = vjp((tok[4], tok[3].astype(f32)))
        return [dx1, df1], [dgate, dsh, dsc], [dg]

    dx1, df1, dg2, dsh3, dsc4, dg_mix = _rowwise(
        f_mix_in_bwd, name="mix_in_bwd", tok_in=[x_lat, x_ctx, (f1, D, 0, False), (dum, D, 0, False), (dx2, D, 0, True)],
        ex_in=[mods[2], mods[3], mods[4]], sh_in=[g_mix], tok_out=[(D, f32), (D, bf16)], ex_out=[D, D, D], gl_out=[(1, D)], **rw_all)

    dgu1 = _ffn_down_dx(df1, W1d, gu1, "ffn1_down_dx")
    gW1d = _matmul(h1, df1, "tn", f32, "ffn1_down_dw", tm_cap=1408)
    gW1gu, (r_w1d,) = _matmul(u1, dgu1, "tn", f32, "ffn1_up_dw", carry=([row_shards(gW1d)], True), halves="b")
    du1, (r_w1gu,) = _matmul(dgu1, W1gu, "nt", bf16, "ffn1_up_dx", carry=([col_shards(gW1gu)], True), halves="a")

    def f_ffn_in_bwd(tok, ex, sh):
        _, vjp = jax.vjp(_rms_mod, tok[0] + tok[1], sh[0], ex[0], ex[1])
        dx, dg, dsh, dsc = vjp(tok[2].astype(f32))
        return [dx + tok[3]], [dsh, dsc], [dg]

    dx_lat, dsh0, dsc1, dg_ffn1 = _rowwise(
        f_ffn_in_bwd, name="ffn1_in_bwd", tok_in=[x_lat, x_ctx, (du1, D, 0, False), (dx1, D, 0, False)],
        ex_in=[mods[0], mods[1]], sh_in=[g_ffn1], tok_out=[(D, f32, "x")], ex_out=[D, D], gl_out=[(1, D)], **rw_all)
    grad_x = dx_lat.reshape(B, L, D)

    zrow = jnp.zeros((1, 1, D), f32)
    dmod_loc = jnp.concatenate([dsh0, dsc1, dg2, dsh3, dsc4] + [jnp.concatenate([t, zrow], axis=0) for t in (dg5, dsh6, dsc7, dg8)],
                               axis=2).reshape(B + 1, N_MOD * D)
    rows16 = jnp.concatenate([jnp.concatenate([loss_p, jnp.zeros((1, D - loss_p.shape[1]), f32)], axis=1), dg_ffn1, dg_mix, g_dwb, g_ln_g,
                              g_ln_b, g_BA, g_gn, dg_ffn2, dgf, jnp.zeros((6, D), f32)], axis=0)

    def to8(v):
        n_pad = -(-v.shape[1] // 1024) * 1024
        return jnp.pad(v, ((0, 0), (0, n_pad - v.shape[1]))).reshape(8, n_pad // 8)

    def from8(a, n):
        return a.reshape(1, a.size)[:, :n]

    def adam_big(nm, wv, mv, vv, part, carry=None):
        out = _adamw(wv[0], mv[0], vv[0], part, "adamw_" + nm, True, carry=carry)
        res4, carried = out if carry is not None else (out, None)
        return [t[None] for t in res4], carried

    rs_out = dict(w1_gu=r_w1gu, w1_down=r_w1d, w_in=r_win, w_conv_out=r_wco, w_gla_out=r_wgo, w_out=r_wo, w2_gu=r_w2gu, w2_down=r_w2d)
    big = {}
    big["w_in"], (dmod_g, rows_g, dww_sg, wa_sg) = adam_big(
        "w_in", w_in, m_w_in, v_w_in, rs_out["w_in"], carry=([dmod_loc, rows16, g_dww, g_WA[:2 * LOWRANK]], False))
    dmx = dmod_g[:, :B].reshape(N_DEV * B, N_MOD * D)
    dmc = dmod_g[:, B]
    gWmod, gcc_p = _mod_bwd(c_all, c_ctx2, w_mod[0], lax.dynamic_slice(dmx, (0, me * ncm), (N_DEV * B, ncm)),
                            lax.dynamic_slice(dmc, (0, me * ncm), (N_DEV, ncm)), "mod_bwd")

    big["w1_gu"], (gcc_g,) = adam_big("w1_gu", w1_gu, m_w1_gu, v_w1_gu, rs_out["w1_gu"], carry=([to8(gcc_p)], False))
    for nm, wv, mv, vv in (("w1_down", w1_down, m_w1_down, v_w1_down), ("w_conv_out", w_conv_out, m_w_conv_out, v_w_conv_out),
                           ("w_gla_out", w_gla_out, m_w_gla_out, v_w_gla_out), ("w_out", w_out, m_w_out, v_w_out),
                           ("w2_gu", w2_gu, m_w2_gu, v_w2_gu), ("w2_down", w2_down, m_w2_down, v_w2_down)):
        big[nm], _ = adam_big(nm, wv, mv, vv, rs_out[nm])
    big["w_mod"] = [t[None] for t in _adamw(w_mod[0], m_w_mod[0], v_w_mod[0], gWmod, "adamw_w_mod", False)]

    rows_s, dww_s, wa_s, g_cc, g_bmod = _sum_sources(
        [rows_g, dww_sg, wa_sg, gcc_g, jnp.concatenate([dmx, dmc], axis=0).reshape(N_DEV * (B + 1), 8, N_MOD * D // 8)], "sum_small")
    g_cc, g_bmod = from8(g_cc, D), from8(g_bmod, N_MOD * D)
    loss = rows_s[0, 0]
    ncd, nca = dw_weight.shape[2], w_alpha_f.shape[2]
    g_dww_loc = lax.dynamic_slice(dww_s, (0, me * ncd), (CONV_W, ncd))
    g_waf_loc = lax.dynamic_slice(wa_s, (0, me * nca), (LOWRANK, nca))
    g_wab_loc = lax.dynamic_slice(wa_s, (LOWRANK, QK + me * nca), (LOWRANK, nca))
    sm = {k: rows_s[i:i + 1] for i, k in enumerate(["loss", "g_ffn1", "g_mix", "dw_bias", "conv_ln_g", "conv_ln_b", "b_alpha", "gla_norm_g",
                                                     "g_ffn2", "g_final"])}

    small_params = [("c_ctx", c_ctx, m_c_ctx, v_c_ctx, g_cc), ("b_mod", b_mod, m_b_mod, v_b_mod, g_bmod),
                    ("g_ffn1", g_ffn1, m_g_ffn1, v_g_ffn1, sm["g_ffn1"]), ("g_mix", g_mix, m_g_mix, v_g_mix, sm["g_mix"]),
                    ("dw_weight", dw_weight, m_dw_weight, v_dw_weight, g_dww_loc), ("dw_bias", dw_bias, m_dw_bias, v_dw_bias, sm["dw_bias"]),
                    ("conv_ln_g", conv_ln_g, m_conv_ln_g, v_conv_ln_g, sm["conv_ln_g"]),
                    ("conv_ln_b", conv_ln_b, m_conv_ln_b, v_conv_ln_b, sm["conv_ln_b"]),
                    ("w_alpha_f", w_alpha_f, m_w_alpha_f, v_w_alpha_f, g_waf_loc), ("b_alpha_f", b_alpha_f, m_b_alpha_f, v_b_alpha_f, sm["b_alpha"][:, :QK]),
                    ("w_alpha_b", w_alpha_b, m_w_alpha_b, v_w_alpha_b, g_wab_loc), ("b_alpha_b", b_alpha_b, m_b_alpha_b, v_b_alpha_b, sm["b_alpha"][:, QK:]),
                    ("gla_norm_g", gla_norm_g, m_gla_norm_g, v_gla_norm_g, sm["gla_norm_g"]),
                    ("g_ffn2", g_ffn2, m_g_ffn2, v_g_ffn2, sm["g_ffn2"]), ("g_final", g_final, m_g_final, v_g_final, sm["g_final"])]

    def two_d(t, like):
        return t.reshape(like.shape[1:]) if like.ndim == 3 else t.reshape(like.size // 128, 128)

    small_res = _adamw_many([tuple(two_d(t, wv) for t in (wv, mv, vv, gv)) for _, wv, mv, vv, gv in small_params], "adamw_small")
    small_out = {nm: [gv.reshape(wv.shape)] + [t.reshape(wv.shape) for t in r3]
                 for (nm, wv, _, _, gv), r3 in zip(small_params, small_res)}

    order = ["c_ctx", "w_mod", "b_mod", "g_ffn1", "w1_gu", "w1_down", "g_mix", "w_in", "dw_weight", "dw_bias", "conv_ln_g", "conv_ln_b",
             "w_conv_out", "w_alpha_f", "b_alpha_f", "w_alpha_b", "b_alpha_b", "gla_norm_g", "w_gla_out", "w_out", "g_ffn2", "w2_gu",
             "w2_down", "g_final"]
    res = {**big, **small_out}
    return (loss, grad_x, *[res[n][0] for n in order], *[res[n][1] for n in order], *[res[n][2] for n in order], *[res[n][3] for n in order])
```
